```python
import jax, jax.numpy as jnp
from jax import lax
import numpy as np

D_MODEL = 1024
BATCH = 8
SEQ = 4096
DEPTH = 4

D_MIX = D_MODEL
CONV_CH = D_MIX // 2
CONV_K = 31
SB_HEADS = 8
SB_HEAD_DIM = (D_MIX - CONV_CH) // SB_HEADS
SB_WIDTH = SB_HEADS * SB_HEAD_DIM
BLOCK_Q = 128
MEM_LEN = 256
MEM_HEADS = 4
MEM_HEAD_DIM = D_MODEL // MEM_HEADS
D_FF = ((8 * D_MODEL // 3 + 127) // 128) * 128
FFN_K = 3
IN_WIDTH = 2 * CONV_CH + 3 * SB_WIDTH
DEEPNORM_ALPHA = (2.0 * DEPTH) ** 0.25
DEEPNORM_BETA = (8.0 * DEPTH) ** -0.25
LN_EPS = 1e-5

kernel_name = "hybrid_conformer_stickbreak_deepnorm_trunk"


def layer_norm(x, g, b):
    xf = x.astype(jnp.float32)
    mu = jnp.mean(xf, axis=-1, keepdims=True)
    var = jnp.mean(jnp.square(xf - mu), axis=-1, keepdims=True)
    y = (xf - mu) * lax.rsqrt(var + LN_EPS)
    return (y * g.astype(jnp.float32) + b.astype(jnp.float32)).astype(x.dtype)


def causal_dwconv(x, w, b):
    k = w.shape[0]
    y = lax.conv_general_dilated(
        x, w[:, None, :].astype(x.dtype), window_strides=(1,), padding=[(k - 1, 0)],
        dimension_numbers=('NWC', 'WIO', 'NWC'), feature_group_count=x.shape[-1])
    return y + b


def stick_breaking_attention(q, k, v):
    bsz, seq, nh, dh = q.shape
    nb = seq // BLOCK_Q
    scale = dh ** -0.5
    qf = q.astype(jnp.float32).transpose(0, 2, 1, 3)
    kf = k.astype(jnp.float32).transpose(0, 2, 1, 3)
    vf = v.astype(jnp.float32).transpose(0, 2, 1, 3)
    q_blocks = qf.reshape(bsz, nh, nb, BLOCK_Q, dh).transpose(2, 0, 1, 3, 4)
    key_pos = jnp.arange(seq)

    def one_block(args):
        q_blk, blk = args
        q_pos = blk * BLOCK_Q + jnp.arange(BLOCK_Q)
        visible = key_pos[None, :] < q_pos[:, None]
        z = jnp.einsum('bhqd,bhkd->bhqk', q_blk, kf) * scale
        log_keep = jnp.where(visible, jax.nn.log_sigmoid(-z), 0.0)
        later = lax.cumsum(log_keep, axis=3, reverse=True) - log_keep
        a = jnp.where(visible, jnp.exp(jax.nn.log_sigmoid(z) + later), 0.0)
        return jnp.einsum('bhqk,bhkd->bhqd', a, vf)

    out = lax.map(one_block, (q_blocks, jnp.arange(nb)))
    out = out.transpose(1, 0, 3, 2, 4).reshape(bsz, seq, nh * dh)
    return out.astype(q.dtype)


def hybrid_mixer(h, w_in, conv_w, conv_b, conv_ln_g, conv_ln_b, w_out):
    bsz, seq, _ = h.shape
    proj = h @ w_in
    glu_a, glu_g, q, k, v = jnp.split(
        proj, [CONV_CH, 2 * CONV_CH, 2 * CONV_CH + SB_WIDTH, 2 * CONV_CH + 2 * SB_WIDTH], axis=-1)
    u = glu_a * jax.nn.sigmoid(glu_g)
    u = causal_dwconv(u, conv_w, conv_b)
    u = jax.nn.silu(layer_norm(u, conv_ln_g, conv_ln_b))
    heads = lambda t: t.reshape(bsz, seq, SB_HEADS, SB_HEAD_DIM)
    a = stick_breaking_attention(heads(q), heads(k), heads(v))
    return jnp.concatenate([u, a], axis=-1) @ w_out


def memory_cross_attention(h, mem, wq, wk, wv, wo):
    bsz, seq, _ = h.shape
    q = (h @ wq).reshape(bsz, seq, MEM_HEADS, MEM_HEAD_DIM).astype(jnp.float32)
    k = (mem @ wk).reshape(bsz, -1, MEM_HEADS, MEM_HEAD_DIM).astype(jnp.float32)
    v = (mem @ wv).reshape(bsz, -1, MEM_HEADS, MEM_HEAD_DIM).astype(jnp.float32)
    s = jnp.einsum('bqhd,bmhd->bhqm', q, k) * (MEM_HEAD_DIM ** -0.5)
    p = jax.nn.softmax(s, axis=-1)
    o = jnp.einsum('bhqm,bmhd->bqhd', p, v).reshape(bsz, seq, D_MODEL).astype(h.dtype)
    return o @ wo


def conv_gated_ffn(h, w_up, conv_w, conv_b, w_down):
    up = causal_dwconv(h @ w_up, conv_w, conv_b)
    val, gate = jnp.split(up, 2, axis=-1)
    return (jax.nn.silu(gate) * val) @ w_down


def _fwd_setup_inputs(seed: int = 0) -> dict:
    key = jax.random.key(seed)
    ks = jax.random.split(key, 32)
    nrm = lambda k, shape, s: jax.random.normal(k, shape, jnp.float32) * s
    gain = lambda k, n: 1.0 + 0.05 * jax.random.normal(k, (DEPTH, n), jnp.float32)
    bias = lambda k, n: 0.01 * jax.random.normal(k, (DEPTH, n), jnp.float32)
    col = jnp.arange(IN_WIDTH)
    v_scale = jnp.where(col >= 2 * CONV_CH + 2 * SB_WIDTH, DEEPNORM_BETA, 1.0).astype(jnp.float32)
    w_in = nrm(ks[2], (DEPTH, D_MODEL, IN_WIDTH), D_MODEL ** -0.5) * v_scale
    return {
        "x": nrm(ks[0], (BATCH, SEQ, D_MODEL), 1.0),
        "mem": nrm(ks[1], (BATCH, MEM_LEN, D_MODEL), 1.0),
        "w_in": w_in,
        "conv_w": nrm(ks[3], (DEPTH, CONV_K, CONV_CH), CONV_K ** -0.5),
        "conv_b": bias(ks[4], CONV_CH),
        "conv_ln_g": gain(ks[5], CONV_CH),
        "conv_ln_b": bias(ks[6], CONV_CH),
        "w_out": nrm(ks[7], (DEPTH, D_MIX, D_MODEL), D_MIX ** -0.5 * DEEPNORM_BETA),
        "ln1_g": gain(ks[8], D_MODEL),
        "ln1_b": bias(ks[9], D_MODEL),
        "mem_wq": nrm(ks[10], (DEPTH, D_MODEL, D_MODEL), D_MODEL ** -0.5),
        "mem_wk": nrm(ks[11], (DEPTH, D_MODEL, D_MODEL), D_MODEL ** -0.5),
        "mem_wv": nrm(ks[12], (DEPTH, D_MODEL, D_MODEL), D_MODEL ** -0.5 * DEEPNORM_BETA),
        "mem_wo": nrm(ks[13], (DEPTH, D_MODEL, D_MODEL), D_MODEL ** -0.5 * DEEPNORM_BETA),
        "ln2_g": gain(ks[14], D_MODEL),
        "ln2_b": bias(ks[15], D_MODEL),
        "ffn_up": nrm(ks[16], (DEPTH, D_MODEL, 2 * D_FF), D_MODEL ** -0.5),
        "ffn_conv_w": nrm(ks[17], (DEPTH, FFN_K, 2 * D_FF), FFN_K ** -0.5),
        "ffn_conv_b": bias(ks[18], 2 * D_FF),
        "ffn_down": nrm(ks[19], (DEPTH, D_FF, D_MODEL), D_FF ** -0.5 * DEEPNORM_BETA),
        "ln3_g": gain(ks[20], D_MODEL),
        "ln3_b": bias(ks[21], D_MODEL),
    }


def _fwd_reference(x, mem, w_in, conv_w, conv_b, conv_ln_g, conv_ln_b, w_out, ln1_g, ln1_b,
              mem_wq, mem_wk, mem_wv, mem_wo, ln2_g, ln2_b,
              ffn_up, ffn_conv_w, ffn_conv_b, ffn_down, ln3_g, ln3_b):
    for l in range(DEPTH):
        mix = hybrid_mixer(x, w_in[l], conv_w[l], conv_b[l], conv_ln_g[l], conv_ln_b[l], w_out[l])
        x = layer_norm(DEEPNORM_ALPHA * x + mix, ln1_g[l], ln1_b[l])
        cross = memory_cross_attention(x, mem, mem_wq[l], mem_wk[l], mem_wv[l], mem_wo[l])
        x = layer_norm(DEEPNORM_ALPHA * x + cross, ln2_g[l], ln2_b[l])
        ffn = conv_gated_ffn(x, ffn_up[l], ffn_conv_w[l], ffn_conv_b[l], ffn_down[l])
        x = layer_norm(DEEPNORM_ALPHA * x + ffn, ln3_g[l], ln3_b[l])
    return x


import jax as _jax
import jax.numpy as _jnp

TWIN_FORMAT = 'train_step'
FWD_PARAMS = ['x', 'mem', 'w_in', 'conv_w', 'conv_b', 'conv_ln_g', 'conv_ln_b', 'w_out', 'ln1_g', 'ln1_b', 'mem_wq', 'mem_wk', 'mem_wv', 'mem_wo', 'ln2_g', 'ln2_b', 'ffn_up', 'ffn_conv_w', 'ffn_conv_b', 'ffn_down', 'ln3_g', 'ln3_b']
TWIN_WEIGHTS = ['w_in', 'conv_w', 'conv_b', 'conv_ln_g', 'conv_ln_b', 'w_out', 'ln1_g', 'ln1_b', 'mem_wq', 'mem_wk', 'mem_wv', 'mem_wo', 'ln2_g', 'ln2_b', 'ffn_up', 'ffn_conv_w', 'ffn_conv_b', 'ffn_down', 'ln3_g', 'ln3_b']
TWIN_DIFF_INPUT = 'x'
TWIN_INPUTS = ['x', 'mem', 'w_in', 'conv_w', 'conv_b', 'conv_ln_g', 'conv_ln_b', 'w_out', 'ln1_g', 'ln1_b', 'mem_wq', 'mem_wk', 'mem_wv', 'mem_wo', 'ln2_g', 'ln2_b', 'ffn_up', 'ffn_conv_w', 'ffn_conv_b', 'ffn_down', 'ln3_g', 'ln3_b', 'loss_target', 'm_w_in', 'm_conv_w', 'm_conv_b', 'm_conv_ln_g', 'm_conv_ln_b', 'm_w_out', 'm_ln1_g', 'm_ln1_b', 'm_mem_wq', 'm_mem_wk', 'm_mem_wv', 'm_mem_wo', 'm_ln2_g', 'm_ln2_b', 'm_ffn_up', 'm_ffn_conv_w', 'm_ffn_conv_b', 'm_ffn_down', 'm_ln3_g', 'm_ln3_b', 'v_w_in', 'v_conv_w', 'v_conv_b', 'v_conv_ln_g', 'v_conv_ln_b', 'v_w_out', 'v_ln1_g', 'v_ln1_b', 'v_mem_wq', 'v_mem_wk', 'v_mem_wv', 'v_mem_wo', 'v_ln2_g', 'v_ln2_b', 'v_ffn_up', 'v_ffn_conv_w', 'v_ffn_conv_b', 'v_ffn_down', 'v_ln3_g', 'v_ln3_b']
TWIN_OUTPUTS = ['loss', 'grad_x', 'grad_w_in', 'grad_conv_w', 'grad_conv_b', 'grad_conv_ln_g', 'grad_conv_ln_b', 'grad_w_out', 'grad_ln1_g', 'grad_ln1_b', 'grad_mem_wq', 'grad_mem_wk', 'grad_mem_wv', 'grad_mem_wo', 'grad_ln2_g', 'grad_ln2_b', 'grad_ffn_up', 'grad_ffn_conv_w', 'grad_ffn_conv_b', 'grad_ffn_down', 'grad_ln3_g', 'grad_ln3_b', 'delta_w_in', 'delta_conv_w', 'delta_conv_b', 'delta_conv_ln_g', 'delta_conv_ln_b', 'delta_w_out', 'delta_ln1_g', 'delta_ln1_b', 'delta_mem_wq', 'delta_mem_wk', 'delta_mem_wv', 'delta_mem_wo', 'delta_ln2_g', 'delta_ln2_b', 'delta_ffn_up', 'delta_ffn_conv_w', 'delta_ffn_conv_b', 'delta_ffn_down', 'delta_ln3_g', 'delta_ln3_b', 'new_m_w_in', 'new_m_conv_w', 'new_m_conv_b', 'new_m_conv_ln_g', 'new_m_conv_ln_b', 'new_m_w_out', 'new_m_ln1_g', 'new_m_ln1_b', 'new_m_mem_wq', 'new_m_mem_wk', 'new_m_mem_wv', 'new_m_mem_wo', 'new_m_ln2_g', 'new_m_ln2_b', 'new_m_ffn_up', 'new_m_ffn_conv_w', 'new_m_ffn_conv_b', 'new_m_ffn_down', 'new_m_ln3_g', 'new_m_ln3_b', 'new_v_w_in', 'new_v_conv_w', 'new_v_conv_b', 'new_v_conv_ln_g', 'new_v_conv_ln_b', 'new_v_w_out', 'new_v_ln1_g', 'new_v_ln1_b', 'new_v_mem_wq', 'new_v_mem_wk', 'new_v_mem_wv', 'new_v_mem_wo', 'new_v_ln2_g', 'new_v_ln2_b', 'new_v_ffn_up', 'new_v_ffn_conv_w', 'new_v_ffn_conv_b', 'new_v_ffn_down', 'new_v_ln3_g', 'new_v_ln3_b']
TWIN_LEAF_KINDS = {'loss': 'loss', 'grad_x': 'grad_x', 'grad_w_in': 'grad_w', 'grad_conv_w': 'grad_w', 'grad_conv_b': 'grad_w', 'grad_conv_ln_g': 'grad_w', 'grad_conv_ln_b': 'grad_w', 'grad_w_out': 'grad_w', 'grad_ln1_g': 'grad_w', 'grad_ln1_b': 'grad_w', 'grad_mem_wq': 'grad_w', 'grad_mem_wk': 'grad_w', 'grad_mem_wv': 'grad_w', 'grad_mem_wo': 'grad_w', 'grad_ln2_g': 'grad_w', 'grad_ln2_b': 'grad_w', 'grad_ffn_up': 'grad_w', 'grad_ffn_conv_w': 'grad_w', 'grad_ffn_conv_b': 'grad_w', 'grad_ffn_down': 'grad_w', 'grad_ln3_g': 'grad_w', 'grad_ln3_b': 'grad_w', 'delta_w_in': 'delta_w', 'delta_conv_w': 'delta_w', 'delta_conv_b': 'delta_w', 'delta_conv_ln_g': 'delta_w', 'delta_conv_ln_b': 'delta_w', 'delta_w_out': 'delta_w', 'delta_ln1_g': 'delta_w', 'delta_ln1_b': 'delta_w', 'delta_mem_wq': 'delta_w', 'delta_mem_wk': 'delta_w', 'delta_mem_wv': 'delta_w', 'delta_mem_wo': 'delta_w', 'delta_ln2_g': 'delta_w', 'delta_ln2_b': 'delta_w', 'delta_ffn_up': 'delta_w', 'delta_ffn_conv_w': 'delta_w', 'delta_ffn_conv_b': 'delta_w', 'delta_ffn_down': 'delta_w', 'delta_ln3_g': 'delta_w', 'delta_ln3_b': 'delta_w', 'new_m_w_in': 'new_m', 'new_m_conv_w': 'new_m', 'new_m_conv_b': 'new_m', 'new_m_conv_ln_g': 'new_m', 'new_m_conv_ln_b': 'new_m', 'new_m_w_out': 'new_m', 'new_m_ln1_g': 'new_m', 'new_m_ln1_b': 'new_m', 'new_m_mem_wq': 'new_m', 'new_m_mem_wk': 'new_m', 'new_m_mem_wv': 'new_m', 'new_m_mem_wo': 'new_m', 'new_m_ln2_g': 'new_m', 'new_m_ln2_b': 'new_m', 'new_m_ffn_up': 'new_m', 'new_m_ffn_conv_w': 'new_m', 'new_m_ffn_conv_b': 'new_m', 'new_m_ffn_down': 'new_m', 'new_m_ln3_g': 'new_m', 'new_m_ln3_b': 'new_m', 'new_v_w_in': 'new_v', 'new_v_conv_w': 'new_v', 'new_v_conv_b': 'new_v', 'new_v_conv_ln_g': 'new_v', 'new_v_conv_ln_b': 'new_v', 'new_v_w_out': 'new_v', 'new_v_ln1_g': 'new_v', 'new_v_ln1_b': 'new_v', 'new_v_mem_wq': 'new_v', 'new_v_mem_wk': 'new_v', 'new_v_mem_wv': 'new_v', 'new_v_mem_wo': 'new_v', 'new_v_ln2_g': 'new_v', 'new_v_ln2_b': 'new_v', 'new_v_ffn_up': 'new_v', 'new_v_ffn_conv_w': 'new_v', 'new_v_ffn_conv_b': 'new_v', 'new_v_ffn_down': 'new_v', 'new_v_ln3_g': 'new_v', 'new_v_ln3_b': 'new_v'}


def _forward(args):
    return _fwd_reference(*[args[k] for k in FWD_PARAMS])


def _output_shape():
    out = _jax.eval_shape(lambda: _forward(_fwd_setup_inputs(0)))
    return out.shape, out.dtype

N_MICROBATCH = 1
ADAM_LR = 0.001
ADAM_B1 = 0.9
ADAM_B2 = 0.999
ADAM_EPS = 1e-08
ADAM_WD = 0.01
ADAM_STEP = 10
PER_EXAMPLE_BATCH_AXIS = {'x': 0, 'mem': 0, 'loss_target': 0}
SHARED_INPUTS = []
_WEIGHT_DTYPES = {'w_in': _jnp.float32, 'conv_w': _jnp.float32, 'conv_b': _jnp.float32, 'conv_ln_g': _jnp.float32, 'conv_ln_b': _jnp.float32, 'w_out': _jnp.float32, 'ln1_g': _jnp.float32, 'ln1_b': _jnp.float32, 'mem_wq': _jnp.float32, 'mem_wk': _jnp.float32, 'mem_wv': _jnp.float32, 'mem_wo': _jnp.float32, 'ln2_g': _jnp.float32, 'ln2_b': _jnp.float32, 'ffn_up': _jnp.float32, 'ffn_conv_w': _jnp.float32, 'ffn_conv_b': _jnp.float32, 'ffn_down': _jnp.float32, 'ln3_g': _jnp.float32, 'ln3_b': _jnp.float32}
MOMENT_SCALE = {'w_in': 1.944145e-02, 'conv_w': 2.867126e-02, 'conv_b': 6.510270e-02, 'conv_ln_g': 3.860102e-02, 'conv_ln_b': 4.077651e-02, 'w_out': 5.394840e-02, 'ln1_g': 3.019493e+00, 'ln1_b': 2.904680e-01, 'mem_wq': 2.038972e-03, 'mem_wk': 2.050136e-03, 'mem_wv': 5.172252e-03, 'mem_wo': 5.107257e-03, 'ln2_g': 3.021711e+00, 'ln2_b': 2.918995e-01, 'ffn_up': 1.678884e-02, 'ffn_conv_w': 1.695149e-02, 'ffn_conv_b': 1.695185e-02, 'ffn_down': 6.538639e-02, 'ln3_g': 1.691123e+01, 'ln3_b': 1.082814e+00}


def _to_microbatches(a, axis):
    t = _jnp.moveaxis(a, axis, 0)
    t = t.reshape((N_MICROBATCH, t.shape[0] // N_MICROBATCH) + t.shape[1:])
    return _jnp.moveaxis(t, 1, axis + 1)


def setup_inputs(seed: int = 0) -> dict:
    inp = _fwd_setup_inputs(seed)
    key = _jax.random.fold_in(_jax.random.key(seed), 7919)
    shape, _ = _output_shape()
    out = dict(inp)
    out["loss_target"] = _jax.random.normal(_jax.random.fold_in(key, 0), shape, _jnp.float32)
    for i, name in enumerate(TWIN_WEIGHTS):
        w = inp[name].astype(_jnp.float32)
        if MOMENT_SCALE is None:
            s = _jnp.sqrt(_jnp.mean(_jnp.square(w)) + 1e-30)
        else:
            s = MOMENT_SCALE[name]
        km, kv = _jax.random.split(_jax.random.fold_in(key, i + 1))
        out[name] = w
        out["m_" + name] = s * _jax.random.normal(km, w.shape, _jnp.float32)
        out["v_" + name] = (s * s) * _jax.random.uniform(kv, w.shape, _jnp.float32, 0.5, 1.5)
    if N_MICROBATCH > 1:
        for name, axis in PER_EXAMPLE_BATCH_AXIS.items():
            out[name] = _to_microbatches(out[name], axis)
    return {'x': out['x'], 'mem': out['mem'], 'w_in': out['w_in'], 'conv_w': out['conv_w'], 'conv_b': out['conv_b'], 'conv_ln_g': out['conv_ln_g'], 'conv_ln_b': out['conv_ln_b'], 'w_out': out['w_out'], 'ln1_g': out['ln1_g'], 'ln1_b': out['ln1_b'], 'mem_wq': out['mem_wq'], 'mem_wk': out['mem_wk'], 'mem_wv': out['mem_wv'], 'mem_wo': out['mem_wo'], 'ln2_g': out['ln2_g'], 'ln2_b': out['ln2_b'], 'ffn_up': out['ffn_up'], 'ffn_conv_w': out['ffn_conv_w'], 'ffn_conv_b': out['ffn_conv_b'], 'ffn_down': out['ffn_down'], 'ln3_g': out['ln3_g'], 'ln3_b': out['ln3_b'], 'loss_target': out['loss_target'], 'm_w_in': out['m_w_in'], 'm_conv_w': out['m_conv_w'], 'm_conv_b': out['m_conv_b'], 'm_conv_ln_g': out['m_conv_ln_g'], 'm_conv_ln_b': out['m_conv_ln_b'], 'm_w_out': out['m_w_out'], 'm_ln1_g': out['m_ln1_g'], 'm_ln1_b': out['m_ln1_b'], 'm_mem_wq': out['m_mem_wq'], 'm_mem_wk': out['m_mem_wk'], 'm_mem_wv': out['m_mem_wv'], 'm_mem_wo': out['m_mem_wo'], 'm_ln2_g': out['m_ln2_g'], 'm_ln2_b': out['m_ln2_b'], 'm_ffn_up': out['m_ffn_up'], 'm_ffn_conv_w': out['m_ffn_conv_w'], 'm_ffn_conv_b': out['m_ffn_conv_b'], 'm_ffn_down': out['m_ffn_down'], 'm_ln3_g': out['m_ln3_g'], 'm_ln3_b': out['m_ln3_b'], 'v_w_in': out['v_w_in'], 'v_conv_w': out['v_conv_w'], 'v_conv_b': out['v_conv_b'], 'v_conv_ln_g': out['v_conv_ln_g'], 'v_conv_ln_b': out['v_conv_ln_b'], 'v_w_out': out['v_w_out'], 'v_ln1_g': out['v_ln1_g'], 'v_ln1_b': out['v_ln1_b'], 'v_mem_wq': out['v_mem_wq'], 'v_mem_wk': out['v_mem_wk'], 'v_mem_wv': out['v_mem_wv'], 'v_mem_wo': out['v_mem_wo'], 'v_ln2_g': out['v_ln2_g'], 'v_ln2_b': out['v_ln2_b'], 'v_ffn_up': out['v_ffn_up'], 'v_ffn_conv_w': out['v_ffn_conv_w'], 'v_ffn_conv_b': out['v_ffn_conv_b'], 'v_ffn_down': out['v_ffn_down'], 'v_ln3_g': out['v_ln3_g'], 'v_ln3_b': out['v_ln3_b']}


def _loss(weights, diff, rest, loss_target):
    with _jax.named_scope("forward"):
        args = {**rest, TWIN_DIFF_INPUT: diff, **{k: w.astype(_WEIGHT_DTYPES[k]) for k, w in weights.items()}}
        y = _forward(args)
    with _jax.named_scope("loss_head"):
        err = _jnp.square(y.astype(_jnp.float32) - loss_target)
        return 0.5 * _jnp.sum(_jnp.mean(err, axis=-1)) if err.ndim else 0.5 * err


def _adamw(w, g, m, v):
    m = ADAM_B1 * m + (1.0 - ADAM_B1) * g
    v = ADAM_B2 * v + (1.0 - ADAM_B2) * _jnp.square(g)
    m_hat = m / (1.0 - ADAM_B1 ** ADAM_STEP)
    v_hat = v / (1.0 - ADAM_B2 ** ADAM_STEP)
    delta = -ADAM_LR * (m_hat / (_jnp.sqrt(v_hat) + ADAM_EPS) + ADAM_WD * w)
    return delta, m, v


def reference(x, mem, w_in, conv_w, conv_b, conv_ln_g, conv_ln_b, w_out, ln1_g, ln1_b, mem_wq, mem_wk, mem_wv, mem_wo, ln2_g, ln2_b, ffn_up, ffn_conv_w, ffn_conv_b, ffn_down, ln3_g, ln3_b, loss_target, m_w_in, m_conv_w, m_conv_b, m_conv_ln_g, m_conv_ln_b, m_w_out, m_ln1_g, m_ln1_b, m_mem_wq, m_mem_wk, m_mem_wv, m_mem_wo, m_ln2_g, m_ln2_b, m_ffn_up, m_ffn_conv_w, m_ffn_conv_b, m_ffn_down, m_ln3_g, m_ln3_b, v_w_in, v_conv_w, v_conv_b, v_conv_ln_g, v_conv_ln_b, v_w_out, v_ln1_g, v_ln1_b, v_mem_wq, v_mem_wk, v_mem_wv, v_mem_wo, v_ln2_g, v_ln2_b, v_ffn_up, v_ffn_conv_w, v_ffn_conv_b, v_ffn_down, v_ln3_g, v_ln3_b):
    given = dict(x=x, mem=mem, w_in=w_in, conv_w=conv_w, conv_b=conv_b, conv_ln_g=conv_ln_g, conv_ln_b=conv_ln_b, w_out=w_out, ln1_g=ln1_g, ln1_b=ln1_b, mem_wq=mem_wq, mem_wk=mem_wk, mem_wv=mem_wv, mem_wo=mem_wo, ln2_g=ln2_g, ln2_b=ln2_b, ffn_up=ffn_up, ffn_conv_w=ffn_conv_w, ffn_conv_b=ffn_conv_b, ffn_down=ffn_down, ln3_g=ln3_g, ln3_b=ln3_b, loss_target=loss_target, m_w_in=m_w_in, m_conv_w=m_conv_w, m_conv_b=m_conv_b, m_conv_ln_g=m_conv_ln_g, m_conv_ln_b=m_conv_ln_b, m_w_out=m_w_out, m_ln1_g=m_ln1_g, m_ln1_b=m_ln1_b, m_mem_wq=m_mem_wq, m_mem_wk=m_mem_wk, m_mem_wv=m_mem_wv, m_mem_wo=m_mem_wo, m_ln2_g=m_ln2_g, m_ln2_b=m_ln2_b, m_ffn_up=m_ffn_up, m_ffn_conv_w=m_ffn_conv_w, m_ffn_conv_b=m_ffn_conv_b, m_ffn_down=m_ffn_down, m_ln3_g=m_ln3_g, m_ln3_b=m_ln3_b, v_w_in=v_w_in, v_conv_w=v_conv_w, v_conv_b=v_conv_b, v_conv_ln_g=v_conv_ln_g, v_conv_ln_b=v_conv_ln_b, v_w_out=v_w_out, v_ln1_g=v_ln1_g, v_ln1_b=v_ln1_b, v_mem_wq=v_mem_wq, v_mem_wk=v_mem_wk, v_mem_wv=v_mem_wv, v_mem_wo=v_mem_wo, v_ln2_g=v_ln2_g, v_ln2_b=v_ln2_b, v_ffn_up=v_ffn_up, v_ffn_conv_w=v_ffn_conv_w, v_ffn_conv_b=v_ffn_conv_b, v_ffn_down=v_ffn_down, v_ln3_g=v_ln3_g, v_ln3_b=v_ln3_b)
    weights = {n: given[n] for n in TWIN_WEIGHTS}
    shared = {n: given[n] for n in SHARED_INPUTS}
    per_example = {n: given[n] for n in ['x', 'mem']}
    grad_fn = _jax.value_and_grad(_loss, argnums=(0, 1))

    def one_microbatch(ex, loss_target):
        ex = dict(ex)
        diff = ex.pop(TWIN_DIFF_INPUT)
        return grad_fn(weights, diff, {**shared, **ex}, loss_target)

    if N_MICROBATCH == 1:
        loss, (grad_w, grad_x) = one_microbatch(per_example, given["loss_target"])
    else:
        def body(carry, xs):
            loss_sum, grad_sum = carry
            l_k, (gw_k, gx_k) = one_microbatch(xs[0], xs[1])
            with _jax.named_scope("update"):
                return (loss_sum + l_k, _jax.tree.map(_jnp.add, grad_sum, gw_k)), gx_k

        init = (_jnp.zeros((), _jnp.float32), _jax.tree.map(_jnp.zeros_like, weights))
        (loss, grad_w), grad_x = _jax.lax.scan(body, init, (per_example, given["loss_target"]))
    with _jax.named_scope("update"):
        delta_w, new_m, new_v = {}, {}, {}
        for n in TWIN_WEIGHTS:
            delta_w[n], new_m[n], new_v[n] = _adamw(weights[n], grad_w[n], given["m_" + n], given["v_" + n])
    return (loss, grad_x, *[grad_w[n] for n in TWIN_WEIGHTS], *[delta_w[n] for n in TWIN_WEIGHTS],
            *[new_m[n] for n in TWIN_WEIGHTS], *[new_v[n] for n in TWIN_WEIGHTS])
```

```python
import functools

import jax
import jax.numpy as jnp
from jax import lax
from jax.experimental import pallas as pl
from jax.experimental.pallas import tpu as pltpu

F32 = jnp.float32
BF16 = jnp.bfloat16
MESH = pl.DeviceIdType.MESH

LN_EPS = 1e-5
SB_HEADS = 8
MEM_HEADS = 4
ADAM_LR, ADAM_B1, ADAM_B2, ADAM_EPS, ADAM_WD, ADAM_STEP = 0.001, 0.9, 0.999, 1e-08, 0.01, 10

LANES = 128
V7X_VMEM_BYTES = 64 << 20
VMEM_CAP = V7X_VMEM_BYTES - (6 << 20)
N_CHIPS = 4
N_DEV = 8

BIG = ('w_in', 'w_out', 'mem_wq', 'mem_wk', 'mem_wv', 'mem_wo', 'ffn_up', 'ffn_down')
COL_SHARDED = ('w_in', 'ffn_up')
SMALL = ('conv_w', 'conv_b', 'conv_ln_g', 'conv_ln_b', 'ln1_g', 'ln1_b', 'ln2_g', 'ln2_b',
         'ffn_conv_w', 'ffn_conv_b', 'ln3_g', 'ln3_b')
SMALL_SHARDED = ('conv_w', 'ffn_conv_w')
WEIGHTS = ('w_in', 'conv_w', 'conv_b', 'conv_ln_g', 'conv_ln_b', 'w_out', 'ln1_g', 'ln1_b',
           'mem_wq', 'mem_wk', 'mem_wv', 'mem_wo', 'ln2_g', 'ln2_b', 'ffn_up', 'ffn_conv_w',
           'ffn_conv_b', 'ffn_down', 'ln3_g', 'ln3_b')


def _params(block_bytes, semantics=None, **kw):
    limit = int(min(max(2 * block_bytes + (8 << 20), 32 << 20), VMEM_CAP))
    return pltpu.CompilerParams(dimension_semantics=semantics, vmem_limit_bytes=limit, **kw)


def _nbytes(shape, dtype):
    n = 1
    for s in shape:
        n *= s
    return n * jnp.dtype(dtype).itemsize


def _dot(a, b):
    return jnp.dot(a, b, preferred_element_type=F32)


def _dot_nt(a, b):
    return lax.dot_general(a, b, (((1,), (1,)), ((), ())), preferred_element_type=F32)


def _dot_tn(a, b):
    return lax.dot_general(a, b, (((0,), (0,)), ((), ())), preferred_element_type=F32)


def _sigmoid(x):
    return 1.0 / (1.0 + jnp.exp(-x))


def mm_nn(a, b, out_dtype, *, tm, tn, name):
    M, K = a.shape
    sharded = b.ndim == 3
    if sharded:
        nsh, _, ns = b.shape
        N, per = nsh * ns, ns // tn
        b_spec = pl.BlockSpec((None, K, tn), lambda i, j: (j // per, 0, j % per))
    else:
        N = b.shape[1]
        b_spec = pl.BlockSpec((K, tn), lambda i, j: (0, j))

    def body(a_ref, b_ref, o_ref):
        o_ref[...] = _dot(a_ref[...].astype(BF16), b_ref[...]).astype(o_ref.dtype)

    blk = _nbytes((tm, K), a.dtype) + _nbytes((K, tn), BF16) + _nbytes((tm, tn), out_dtype)
    return pl.pallas_call(
        body, name=name, out_shape=jax.ShapeDtypeStruct((M, N), out_dtype), grid=(M // tm, N // tn),
        in_specs=[pl.BlockSpec((tm, K), lambda i, j: (i, 0)), b_spec],
        out_specs=pl.BlockSpec((tm, tn), lambda i, j: (i, j)),
        compiler_params=_params(blk, ("parallel", "parallel")))(a, b)


def mm_ln(a, b, x, gamma, beta, alpha, *, tm, name):
    M, K = a.shape
    D = b.shape[1]

    def body(a_ref, b_ref, x_ref, g_ref, be_ref, y_ref, yb_ref, zh_ref, rs_ref):
        z = alpha * x_ref[...] + _dot(a_ref[...], b_ref[...])
        mu = jnp.mean(z, axis=-1, keepdims=True)
        zc = z - mu
        rstd = lax.rsqrt(jnp.mean(zc * zc, axis=-1, keepdims=True) + LN_EPS)
        zh = zc * rstd
        y = zh * g_ref[...] + be_ref[...]
        y_ref[...] = y
        yb_ref[...] = y.astype(BF16)
        zh_ref[...] = zh
        rs_ref[...] = rstd

    row = lambda i: (i, 0)
    fix = lambda i: (0, 0)
    blk = _nbytes((tm, K), BF16) + _nbytes((K, D), BF16) + 4 * _nbytes((tm, D), F32)
    return pl.pallas_call(
        body, name=name, grid=(M // tm,),
        out_shape=(jax.ShapeDtypeStruct((M, D), F32), jax.ShapeDtypeStruct((M, D), BF16),
                   jax.ShapeDtypeStruct((M, D), F32), jax.ShapeDtypeStruct((M, 1), F32)),
        in_specs=[pl.BlockSpec((tm, K), row), pl.BlockSpec((K, D), fix), pl.BlockSpec((tm, D), row),
                  pl.BlockSpec((1, D), fix), pl.BlockSpec((1, D), fix)],
        out_specs=(pl.BlockSpec((tm, D), row), pl.BlockSpec((tm, D), row), pl.BlockSpec((tm, D), row),
                   pl.BlockSpec((tm, 1), row)),
        compiler_params=_params(blk, ("parallel",)))(a, b, x, gamma, beta)


def ln_bwd(dy, zh, rstd, gamma, *, tm, name):
    M, D = dy.shape

    def body(dy_ref, zh_ref, rs_ref, g_ref, dz_ref, dzb_ref, dg_ref, db_ref):
        @pl.when(pl.program_id(0) == 0)
        def _():
            dg_ref[...] = jnp.zeros_like(dg_ref)
            db_ref[...] = jnp.zeros_like(db_ref)

        dyv, zhv = dy_ref[...], zh_ref[...]
        dg_ref[...] += jnp.sum(dyv * zhv, axis=0, keepdims=True)
        db_ref[...] += jnp.sum(dyv, axis=0, keepdims=True)
        dzh = dyv * g_ref[...]
        m1 = jnp.mean(dzh, axis=-1, keepdims=True)
        m2 = jnp.mean(dzh * zhv, axis=-1, keepdims=True)
        dz = rs_ref[...] * (dzh - m1 - zhv * m2)
        dz_ref[...] = dz
        dzb_ref[...] = dz.astype(BF16)

    row = lambda i: (i, 0)
    fix = lambda i: (0, 0)
    return pl.pallas_call(
        body, name=name, grid=(M // tm,),
        out_shape=(jax.ShapeDtypeStruct((M, D), F32), jax.ShapeDtypeStruct((M, D), BF16),
                   jax.ShapeDtypeStruct((1, D), F32), jax.ShapeDtypeStruct((1, D), F32)),
        in_specs=[pl.BlockSpec((tm, D), row), pl.BlockSpec((tm, D), row), pl.BlockSpec((tm, 1), row),
                  pl.BlockSpec((1, D), fix)],
        out_specs=(pl.BlockSpec((tm, D), row), pl.BlockSpec((tm, D), row), pl.BlockSpec((1, D), fix),
                   pl.BlockSpec((1, D), fix)),
        compiler_params=_params(4 * _nbytes((tm, D), F32), ("arbitrary",)))(dy, zh, rstd, gamma)


def mm_nt(a_list, b, out_dtype, *, tm, tk, name, res=None, alpha=None):
    M = a_list[0].shape[0]
    widths = [a.shape[1] for a in a_list]
    sharded = b.ndim == 3
    if sharded:
        nsh, K, ns = b.shape
        b_spec = pl.BlockSpec((nsh, tk, ns), lambda i, j: (0, j, 0))
        for w in widths:
            assert w % ns == 0
    else:
        K, N = b.shape
        ns = None
        b_spec = pl.BlockSpec((tk, N), lambda i, j: (j, 0))
    n_a = len(a_list)

    def body(*refs):
        a_refs, b_ref = refs[:n_a], refs[n_a]
        o_ref = refs[-1]
        acc = None
        off = 0
        for a_ref, w in zip(a_refs, widths):
            if sharded:
                for p in range(w // ns):
                    t = _dot_nt(a_ref[:, p * ns:(p + 1) * ns].astype(BF16), b_ref[off // ns + p])
                    acc = t if acc is None else acc + t
            else:
                t = _dot_nt(a_ref[...].astype(BF16), b_ref[:, off:off + w])
                acc = t if acc is None else acc + t
            off += w
        if res is not None:
            acc = acc + alpha * refs[n_a + 1][...]
        o_ref[...] = acc.astype(o_ref.dtype)

    in_specs = [pl.BlockSpec((tm, w), lambda i, j: (i, 0)) for w in widths] + [b_spec]
    args = list(a_list) + [b]
    if res is not None:
        in_specs.append(pl.BlockSpec((tm, tk), lambda i, j: (i, j)))
        args.append(res)
    blk = (sum(_nbytes((tm, w), a.dtype) for a, w in zip(a_list, widths)) + _nbytes((tk, sum(widths)), BF16)
           + 2 * _nbytes((tm, tk), F32))
    return pl.pallas_call(
        body, name=name, out_shape=jax.ShapeDtypeStruct((M, K), out_dtype), grid=(M // tm, K // tk),
        in_specs=in_specs, out_specs=pl.BlockSpec((tm, tk), lambda i, j: (i, j)),
        compiler_params=_params(blk, ("parallel", "parallel")))(*args)


def mm_tn(a, b_list, *, tk, tn, name, shard_width=None, tmc=None):
    M, K = a.shape
    tmc = M if tmc is None else tmc
    nm = M // tmc
    widths = [b.shape[1] for b in b_list]
    N = sum(widths)
    starts, s = [], 0
    for w in widths:
        assert w % tn == 0
        starts.append(s)
        s += w // tn
    n_b = len(b_list)

    def body(*refs):
        a_ref, b_refs, o_ref, acc = refs[0], refs[1:1 + n_b], refs[-2], refs[-1]
        j, m = pl.program_id(1), pl.program_id(2)
        for b_ref, st, w in zip(b_refs, starts, widths):
            @pl.when((j >= st) & (j < st + w // tn))
            def _(b_ref=b_ref):
                t = _dot_tn(a_ref[...].astype(BF16), b_ref[...].astype(BF16))
                if nm == 1:
                    o_ref[...] = t.astype(o_ref.dtype)
                else:
                    @pl.when(m == 0)
                    def _():
                        acc[...] = t

                    @pl.when(m > 0)
                    def _():
                        acc[...] += t

                    @pl.when(m == nm - 1)
                    def _():
                        o_ref[...] = acc[...].astype(o_ref.dtype)

    def b_map(st, w):
        nb = w // tn
        return lambda i, j, m: (jnp.where((j >= st) & (j < st + nb), m, 0), jnp.clip(j - st, 0, nb - 1))

    in_specs = [pl.BlockSpec((tmc, tk), lambda i, j, m: (m, i))]
    in_specs += [pl.BlockSpec((tmc, tn), b_map(st, w)) for st, w in zip(starts, widths)]
    if shard_width is None:
        out_shape = jax.ShapeDtypeStruct((K, N), BF16)
        out_spec = pl.BlockSpec((tk, tn), lambda i, j, m: (i, j))
    else:
        per = shard_width // tn
        out_shape = jax.ShapeDtypeStruct((N // shard_width, K, shard_width), BF16)
        out_spec = pl.BlockSpec((None, tk, tn), lambda i, j, m: (j // per, i, j % per))
    acc_shape = (tk, tn) if nm > 1 else (8, LANES)
    blk = (_nbytes((tmc, tk), a.dtype) + n_b * _nbytes((tmc, tn), b_list[0].dtype) + 2 * _nbytes((tk, tn), F32))
    return pl.pallas_call(
        body, name=name, out_shape=out_shape, grid=(K // tk, N // tn, nm), in_specs=in_specs, out_specs=out_spec,
        scratch_shapes=[pltpu.VMEM(acc_shape, F32)],
        compiler_params=_params(blk, ("parallel", "arbitrary", "arbitrary")))(a, *b_list)


CONV_PAD = 32
CONV_CHUNK = 512


def conv_fwd(proj, conv_w, conv_b, *, name):
    S = proj.shape[0]
    KW, C = conv_w.shape
    nct = C // LANES
    rc = min(CONV_CHUNK, S)

    def body(a_ref, g_ref, w_ref, b_ref, o_ref, pad):
        pad[0:CONV_PAD, :] = jnp.zeros((CONV_PAD, LANES), F32)
        pad[CONV_PAD:, :] = a_ref[...] * _sigmoid(g_ref[...])
        w = w_ref[...]
        bias = b_ref[...]

        def chunk(i, _):
            base = pl.multiple_of(i * rc, rc)
            win = pad[pl.ds(base, rc + CONV_PAD), :]
            acc = jnp.zeros((rc, LANES), F32) + bias
            for k in range(KW):
                off = CONV_PAD - (KW - 1) + k
                acc = acc + w[k:k + 1, :] * win[off:off + rc, :]
            o_ref[pl.ds(base, rc), :] = acc
            return 0

        lax.fori_loop(0, S // rc, chunk, 0)

    return pl.pallas_call(
        body, name=name, out_shape=jax.ShapeDtypeStruct((S, C), F32), grid=(nct,),
        in_specs=[pl.BlockSpec((S, LANES), lambda c: (0, c)), pl.BlockSpec((S, LANES), lambda c: (0, c + nct)),
                  pl.BlockSpec((KW, LANES), lambda c: (0, c)), pl.BlockSpec((1, LANES), lambda c: (0, c))],
        out_specs=pl.BlockSpec((S, LANES), lambda c: (0, c)),
        scratch_shapes=[pltpu.VMEM((S + CONV_PAD, LANES), F32)],
        compiler_params=_params(4 * _nbytes((S, LANES), F32), ("parallel",)))(proj, proj, conv_w, conv_b)


def conv_bwd(du1, proj, conv_w, *, name):
    S = proj.shape[0]
    KW, C = conv_w.shape
    nct = C // LANES
    rc = min(CONV_CHUNK, S)

    def body(d_ref, a_ref, g_ref, w_ref, da_ref, dg_ref, dw_ref, db_ref, pad_u, pad_d, du0, dw_acc):
        dw_acc[...] = jnp.zeros_like(dw_acc)
        pad_u[0:CONV_PAD, :] = jnp.zeros((CONV_PAD, LANES), F32)
        pad_u[CONV_PAD:, :] = a_ref[...] * _sigmoid(g_ref[...])
        pad_d[0:S, :] = d_ref[...]
        pad_d[S:, :] = jnp.zeros((CONV_PAD, LANES), F32)
        w = w_ref[...]
        db_ref[...] = jnp.sum(d_ref[...], axis=0, keepdims=True)

        def chunk(i, _):
            base = pl.multiple_of(i * rc, rc)
            d = pad_d[pl.ds(base, rc), :]
            win_u = pad_u[pl.ds(base, rc + CONV_PAD), :]
            win_d = pad_d[pl.ds(base, rc + CONV_PAD), :]
            acc = jnp.zeros((rc, LANES), F32)
            for k in range(KW):
                off = CONV_PAD - (KW - 1) + k
                dw_acc[k:k + 1, :] += jnp.sum(d * win_u[off:off + rc, :], axis=0, keepdims=True)
                acc = acc + w[k:k + 1, :] * win_d[KW - 1 - k:KW - 1 - k + rc, :]
            du0[pl.ds(base, rc), :] = acc
            return 0

        lax.fori_loop(0, S // rc, chunk, 0)
        dw_ref[...] = dw_acc[0:KW, :]
        a, sg = a_ref[...], _sigmoid(g_ref[...])
        d0 = du0[...]
        da_ref[...] = (d0 * sg).astype(BF16)
        dg_ref[...] = (d0 * a * sg * (1.0 - sg)).astype(BF16)

    col = lambda c: (0, c)
    return pl.pallas_call(
        body, name=name, grid=(nct,),
        out_shape=(jax.ShapeDtypeStruct((S, C), BF16), jax.ShapeDtypeStruct((S, C), BF16),
                   jax.ShapeDtypeStruct((KW, C), F32), jax.ShapeDtypeStruct((1, C), F32)),
        in_specs=[pl.BlockSpec((S, LANES), col), pl.BlockSpec((S, LANES), col),
                  pl.BlockSpec((S, LANES), lambda c: (0, c + nct)), pl.BlockSpec((KW, LANES), col)],
        out_specs=(pl.BlockSpec((S, LANES), col), pl.BlockSpec((S, LANES), col), pl.BlockSpec((KW, LANES), col),
                   pl.BlockSpec((1, LANES), col)),
        scratch_shapes=[pltpu.VMEM((S + CONV_PAD, LANES), F32), pltpu.VMEM((S + CONV_PAD, LANES), F32),
                        pltpu.VMEM((S, LANES), F32), pltpu.VMEM((CONV_PAD, LANES), F32)],
        compiler_params=_params(8 * _nbytes((S, LANES), F32), ("parallel",)))(du1, proj, proj, conv_w)


def ln_silu(u1, o_sb, gamma, beta, *, tm, name):
    S, C = u1.shape

    def body(u_ref, o_ref, g_ref, b_ref, out_ref):
        z = u_ref[...]
        mu = jnp.mean(z, axis=-1, keepdims=True)
        zc = z - mu
        y = zc * lax.rsqrt(jnp.mean(zc * zc, axis=-1, keepdims=True) + LN_EPS) * g_ref[...] + b_ref[...]
        out_ref[:, 0:C] = (y * _sigmoid(y)).astype(BF16)
        out_ref[:, C:] = o_ref[...].astype(BF16)

    row = lambda i: (i, 0)
    fix = lambda i: (0, 0)
    return pl.pallas_call(
        body, name=name, out_shape=jax.ShapeDtypeStruct((S, 2 * C), BF16), grid=(S // tm,),
        in_specs=[pl.BlockSpec((tm, C), row), pl.BlockSpec((tm, C), row), pl.BlockSpec((1, C), fix),
                  pl.BlockSpec((1, C), fix)],
        out_specs=pl.BlockSpec((tm, 2 * C), row),
        compiler_params=_params(4 * _nbytes((tm, C), F32), ("parallel",)))(u1, o_sb, gamma, beta)


def ln_silu_bwd(dua, u1, gamma, beta, *, tm, name):
    S, C = u1.shape

    def body(d_ref, u_ref, g_ref, b_ref, du1_ref, dg_ref, db_ref):
        @pl.when(pl.program_id(0) == 0)
        def _():
            dg_ref[...] = jnp.zeros_like(dg_ref)
            db_ref[...] = jnp.zeros_like(db_ref)

        z = u_ref[...]
        mu = jnp.mean(z, axis=-1, keepdims=True)
        zc = z - mu
        rstd = lax.rsqrt(jnp.mean(zc * zc, axis=-1, keepdims=True) + LN_EPS)
        zh = zc * rstd
        y = zh * g_ref[...] + b_ref[...]
        sg = _sigmoid(y)
        dy = d_ref[...] * (sg * (1.0 + y * (1.0 - sg)))
        dg_ref[...] += jnp.sum(dy * zh, axis=0, keepdims=True)
        db_ref[...] += jnp.sum(dy, axis=0, keepdims=True)
        dzh = dy * g_ref[...]
        m1 = jnp.mean(dzh, axis=-1, keepdims=True)
        m2 = jnp.mean(dzh * zh, axis=-1, keepdims=True)
        du1_ref[...] = rstd * (dzh - m1 - zh * m2)

    row = lambda i: (i, 0)
    fix = lambda i: (0, 0)
    return pl.pallas_call(
        body, name=name, grid=(S // tm,),
        out_shape=(jax.ShapeDtypeStruct((S, C), F32), jax.ShapeDtypeStruct((1, C), F32),
                   jax.ShapeDtypeStruct((1, C), F32)),
        in_specs=[pl.BlockSpec((tm, C), row), pl.BlockSpec((tm, C), row), pl.BlockSpec((1, C), fix),
                  pl.BlockSpec((1, C), fix)],
        out_specs=(pl.BlockSpec((tm, C), row), pl.BlockSpec((1, C), fix), pl.BlockSpec((1, C), fix)),
        compiler_params=_params(4 * _nbytes((tm, C), F32), ("arbitrary",)))(dua, u1, gamma, beta)


SB_BLOCK = 256


def _split_dot(x, tri):
    hi = x.astype(BF16)
    lo = (x - hi.astype(F32)).astype(BF16)
    return _dot(hi, tri) + _dot(lo, tri)


def _neg_softplus(z):
    return -(jnp.maximum(z, 0.0) + jnp.log(1.0 + jnp.exp(-jnp.abs(z))))


def sb_fwd(proj, *, q_col, name):
    S = proj.shape[0]
    dh = LANES // 2
    W = SB_HEADS * dh
    npair = W // LANES
    T = min(SB_BLOCK, S)
    nblk = S // T
    scale = dh ** -0.5
    qb0 = q_col // LANES

    def body(q_ref, k_ref, v_ref, o_ref, l_ref, qs, ks, vs):
        r_i = lax.broadcasted_iota(jnp.int32, (T, T), 0)
        c_i = lax.broadcasted_iota(jnp.int32, (T, T), 1)
        tri = (r_i >= c_i).astype(BF16)
        vis = c_i < r_i

        for hh in range(2):
            sl = slice(hh * dh, (hh + 1) * dh)
            qs[...] = (q_ref[:, sl] * scale).astype(BF16)
            ks[...] = k_ref[:, sl].astype(BF16)
            vs[...] = v_ref[:, sl].astype(BF16)

            def step(qb, j0, diag, acc, R):
                kb = ks[pl.ds(j0, T), :]
                vb = vs[pl.ds(j0, T), :]
                z = _dot_nt(qb, kb)
                lk = _neg_softplus(z)
                if diag:
                    lk = jnp.where(vis, lk, 0.0)
                C = _split_dot(lk, tri)
                A = jnp.exp(z + C + R)
                if diag:
                    A = jnp.where(vis, A, 0.0)
                return acc + _dot(A.astype(BF16), vb), R + C[:, 0:1]

            def qblock(i, _):
                r0 = pl.multiple_of(i * T, T)
                qb = qs[pl.ds(r0, T), :]
                acc, R = step(qb, r0, True, jnp.zeros((T, dh), F32), jnp.zeros((T, 1), F32))

                def inner(jj, carry):
                    j0 = pl.multiple_of((i - 1 - jj) * T, T)
                    return step(qb, j0, False, *carry)

                acc, R = lax.fori_loop(0, i, inner, (acc, R))
                o_ref[pl.ds(r0, T), sl] = acc
                l_ref[pl.ds(r0, T), sl] = jnp.broadcast_to(R, (T, dh))
                return 0

            lax.fori_loop(0, nblk, qblock, 0)

    blk = lambda off: pl.BlockSpec((S, LANES), lambda h: (0, qb0 + off * npair + h))
    out = pl.BlockSpec((S, LANES), lambda h: (0, h))
    return pl.pallas_call(
        body, name=name, grid=(npair,),
        out_shape=(jax.ShapeDtypeStruct((S, W), F32), jax.ShapeDtypeStruct((S, W), F32)),
        in_specs=[blk(0), blk(1), blk(2)], out_specs=(out, out),
        scratch_shapes=[pltpu.VMEM((S, dh), BF16)] * 3,
        compiler_params=_params(6 * _nbytes((S, LANES), F32), ("parallel",)))(proj, proj, proj)


def sb_bwd(proj, ltot, dua, *, q_col, do_col, name):
    S = proj.shape[0]
    dh = LANES // 2
    W = SB_HEADS * dh
    npair = W // LANES
    T = min(SB_BLOCK, S)
    nblk = S // T
    scale = dh ** -0.5
    qb0 = q_col // LANES
    db0 = do_col // LANES

    def body(q_ref, k_ref, v_ref, l_ref, do_ref, dq_ref, dk_ref, dv_ref, qs, ks, vs, dos, dks, dvs):
        r_i = lax.broadcasted_iota(jnp.int32, (T, T), 0)
        c_i = lax.broadcasted_iota(jnp.int32, (T, T), 1)
        tri_rev = (r_i >= c_i).astype(BF16)
        tri_fwd = (r_i <= c_i).astype(BF16)
        vis = c_i < r_i

        for hh in range(2):
            sl = slice(hh * dh, (hh + 1) * dh)
            qs[...] = (q_ref[:, sl] * scale).astype(BF16)
            ks[...] = k_ref[:, sl].astype(BF16)
            vs[...] = v_ref[:, sl].astype(BF16)
            dos[...] = do_ref[:, sl].astype(BF16)
            dks[...] = jnp.zeros_like(dks)
            dvs[...] = jnp.zeros_like(dvs)

            def step(qb, dob, Lt, j0, diag, dq, P, Gp):
                kb = ks[pl.ds(j0, T), :]
                vb = vs[pl.ds(j0, T), :]
                z = _dot_nt(qb, kb)
                lk = _neg_softplus(z)
                beta = jnp.exp(z + lk)
                if diag:
                    lk = jnp.where(vis, lk, 0.0)
                C = _split_dot(lk, tri_rev)
                rowsum = C[:, 0:1]
                A = jnp.exp(z + C + (Lt - P - rowsum))
                if diag:
                    A = jnp.where(vis, A, 0.0)
                g = A * _dot_nt(dob, vb)
                Gin = _split_dot(g, tri_fwd)
                dz = g - beta * (Gp + Gin)
                if diag:
                    dz = jnp.where(vis, dz, 0.0)
                dzb = dz.astype(BF16)
                dks[pl.ds(j0, T), :] += _dot_tn(dzb, qb)
                dvs[pl.ds(j0, T), :] += _dot_tn(A.astype(BF16), dob)
                return dq + _dot(dzb, kb), P + rowsum, Gp + Gin[:, T - 1:T]

            def qblock(i, _):
                r0 = pl.multiple_of(i * T, T)
                qb = qs[pl.ds(r0, T), :]
                dob = dos[pl.ds(r0, T), :]
                Lt = l_ref[pl.ds(r0, T), hh * dh:hh * dh + 1]

                def inner(j, carry):
                    return step(qb, dob, Lt, pl.multiple_of(j * T, T), False, *carry)

                zero = jnp.zeros((T, 1), F32)
                carry = lax.fori_loop(0, i, inner, (jnp.zeros((T, dh), F32), zero, zero))
                dq, _, _ = step(qb, dob, Lt, r0, True, *carry)
                dq_ref[pl.ds(r0, T), sl] = (dq * scale).astype(BF16)
                return 0

            lax.fori_loop(0, nblk, qblock, 0)
            dk_ref[:, sl] = dks[...].astype(BF16)
            dv_ref[:, sl] = dvs[...].astype(BF16)

    blk = lambda off: pl.BlockSpec((S, LANES), lambda h: (0, qb0 + off * npair + h))
    out = pl.BlockSpec((S, LANES), lambda h: (0, h))
    o_shape = jax.ShapeDtypeStruct((S, W), BF16)
    return pl.pallas_call(
        body, name=name, grid=(npair,), out_shape=(o_shape, o_shape, o_shape),
        in_specs=[blk(0), blk(1), blk(2), out, pl.BlockSpec((S, LANES), lambda h: (0, db0 + h))],
        out_specs=(out, out, out),
        scratch_shapes=[pltpu.VMEM((S, dh), BF16)] * 4 + [pltpu.VMEM((S, dh), F32)] * 2,
        compiler_params=_params(8 * _nbytes((S, LANES), F32), ("parallel",)))(proj, proj, proj, ltot, dua)


def xattn_fwd(q, k, v, *, tm, name):
    S, D = q.shape
    Mlen = k.shape[0]
    hd = D // MEM_HEADS
    scale = hd ** -0.5

    def body(q_ref, k_ref, v_ref, o_ref):
        for h in range(MEM_HEADS):
            sl = slice(h * hd, (h + 1) * hd)
            s = _dot_nt(q_ref[:, sl], k_ref[:, sl]) * scale
            e = jnp.exp(s - jnp.max(s, axis=-1, keepdims=True))
            p = e / jnp.sum(e, axis=-1, keepdims=True)
            o_ref[:, sl] = _dot(p.astype(BF16), v_ref[:, sl]).astype(BF16)

    row = lambda i: (i, 0)
    fix = lambda i: (0, 0)
    return pl.pallas_call(
        body, name=name, out_shape=jax.ShapeDtypeStruct((S, D), BF16), grid=(S // tm,),
        in_specs=[pl.BlockSpec((tm, D), row), pl.BlockSpec((Mlen, D), fix), pl.BlockSpec((Mlen, D), fix)],
        out_specs=pl.BlockSpec((tm, D), row),
        compiler_params=_params(4 * _nbytes((tm, D), F32), ("parallel",)))(q, k, v)


def xattn_bwd(q, do, k, v, *, tm, name):
    S, D = q.shape
    Mlen = k.shape[0]
    hd = D // MEM_HEADS
    scale = hd ** -0.5

    def body(q_ref, do_ref, k_ref, v_ref, dq_ref, dk_ref, dv_ref):
        @pl.when(pl.program_id(0) == 0)
        def _():
            dk_ref[...] = jnp.zeros_like(dk_ref)
            dv_ref[...] = jnp.zeros_like(dv_ref)

        for h in range(MEM_HEADS):
            sl = slice(h * hd, (h + 1) * hd)
            qh, doh, kh, vh = q_ref[:, sl], do_ref[:, sl], k_ref[:, sl], v_ref[:, sl]
            s = _dot_nt(qh, kh) * scale
            e = jnp.exp(s - jnp.max(s, axis=-1, keepdims=True))
            p = e / jnp.sum(e, axis=-1, keepdims=True)
            dp = _dot_nt(doh, vh)
            ds = (p * (dp - jnp.sum(p * dp, axis=-1, keepdims=True)) * scale).astype(BF16)
            dq_ref[:, sl] = _dot(ds, kh).astype(BF16)
            dk_ref[:, sl] += _dot_tn(ds, qh)
            dv_ref[:, sl] += _dot_tn(p.astype(BF16), doh)

    row = lambda i: (i, 0)
    fix = lambda i: (0, 0)
    return pl.pallas_call(
        body, name=name, grid=(S // tm,),
        out_shape=(jax.ShapeDtypeStruct((S, D), BF16), jax.ShapeDtypeStruct((Mlen, D), F32),
                   jax.ShapeDtypeStruct((Mlen, D), F32)),
        in_specs=[pl.BlockSpec((tm, D), row), pl.BlockSpec((tm, D), row), pl.BlockSpec((Mlen, D), fix),
                  pl.BlockSpec((Mlen, D), fix)],
        out_specs=(pl.BlockSpec((tm, D), row), pl.BlockSpec((Mlen, D), fix), pl.BlockSpec((Mlen, D), fix)),
        compiler_params=_params(6 * _nbytes((tm, D), F32), ("arbitrary",)))(q, do, k, v)


FFN_HALO = 8


def _conv3(ext, w, lo):
    tm = ext.shape[0] - FFN_HALO
    return (w[0:1, :] * ext[lo:lo + tm, :] + w[1:2, :] * ext[lo + 1:lo + 1 + tm, :]
            + w[2:3, :] * ext[lo + 2:lo + 2 + tm, :])


def ffn_up_fwd(xb, w_up, conv_w, conv_b, *, tm, tn, name):
    S, D = xb.shape
    nsh, _, ns = w_up.shape
    F = nsh * ns // 2
    per = ns // tn
    ncol = F // tn
    KW = conv_w.shape[0]
    assert KW == 3

    def body(x_ref, wv_ref, wg_ref, cwv_ref, cwg_ref, cbv_ref, cbg_ref, uv_ref, ug_ref, h_ref, carry):
        @pl.when(pl.program_id(1) == 0)
        def _():
            carry[...] = jnp.zeros_like(carry)

        x = x_ref[...]
        uv = _dot(x, wv_ref[...])
        ug = _dot(x, wg_ref[...])
        uv_ref[...] = uv.astype(BF16)
        ug_ref[...] = ug.astype(BF16)
        lo = FFN_HALO - (KW - 1)
        cv = _conv3(jnp.concatenate([carry[0], uv], axis=0), cwv_ref[...], lo) + cbv_ref[...]
        cg = _conv3(jnp.concatenate([carry[1], ug], axis=0), cwg_ref[...], lo) + cbg_ref[...]
        carry[0] = uv[tm - FFN_HALO:, :]
        carry[1] = ug[tm - FFN_HALO:, :]
        h_ref[...] = (cg * _sigmoid(cg) * cv).astype(BF16)

    wspec = lambda half: pl.BlockSpec((None, D, tn), lambda j, i: (half * (nsh // 2) + j // per, 0, j % per))
    cspec = lambda rows, half: pl.BlockSpec((rows, tn), lambda j, i: (0, half * ncol + j))
    out = pl.BlockSpec((tm, tn), lambda j, i: (i, j))
    o_shape = jax.ShapeDtypeStruct((S, F), BF16)
    blk = _nbytes((tm, D), BF16) + 2 * _nbytes((D, tn), BF16) + 8 * _nbytes((tm, tn), F32)
    return pl.pallas_call(
        body, name=name, grid=(ncol, S // tm), out_shape=(o_shape, o_shape, o_shape),
        in_specs=[pl.BlockSpec((tm, D), lambda j, i: (i, 0)), wspec(0), wspec(1), cspec(KW, 0), cspec(KW, 1),
                  cspec(1, 0), cspec(1, 1)],
        out_specs=(out, out, out),
        scratch_shapes=[pltpu.VMEM((2, FFN_HALO, tn), F32)],
        compiler_params=_params(blk, ("parallel", "arbitrary")))(xb, w_up, w_up, conv_w, conv_w, conv_b, conv_b)


def ffn_mid_bwd(dzb, w_down, up_v, up_g, conv_w, conv_b, *, tm, tn, name):
    S, D = dzb.shape
    F = up_v.shape[1]
    ncol = F // tn
    nrow = S // tm
    KW = conv_w.shape[0]
    assert KW == 3
    hb = tm // FFN_HALO

    def body(dz_ref, wd_ref, uv_ref, ug_ref, hv_ref, hg_ref, cwv_ref, cwg_ref, cbv_ref, cbg_ref,
             dv_ref, dg_ref, dwv_ref, dwg_ref, dbv_ref, dbg_ref, carry):
        i = pl.program_id(1)

        @pl.when(i == 0)
        def _():
            carry[...] = jnp.zeros_like(carry)
            for r in (dwv_ref, dwg_ref, dbv_ref, dbg_ref):
                r[...] = jnp.zeros_like(r)

        first = i == nrow - 1
        halo_v = jnp.where(first, 0.0, hv_ref[...].astype(F32))
        halo_g = jnp.where(first, 0.0, hg_ref[...].astype(F32))
        ext_v = jnp.concatenate([halo_v, uv_ref[...].astype(F32)], axis=0)
        ext_g = jnp.concatenate([halo_g, ug_ref[...].astype(F32)], axis=0)
        cwv, cwg = cwv_ref[...], cwg_ref[...]
        lo = FFN_HALO - (KW - 1)
        cv = _conv3(ext_v, cwv, lo) + cbv_ref[...]
        cg = _conv3(ext_g, cwg, lo) + cbg_ref[...]
        dh = _dot_nt(dz_ref[...], wd_ref[...])
        sg = _sigmoid(cg)
        dcv = dh * (cg * sg)
        dcg = dh * cv * (sg * (1.0 + cg * (1.0 - sg)))

        def back(dc, ext, cw, slot, du_ref, dw_ref, db_ref):
            ext2 = jnp.concatenate([dc, carry[slot]], axis=0)
            du = cw[2:3, :] * ext2[0:tm, :] + cw[1:2, :] * ext2[1:tm + 1, :] + cw[0:1, :] * ext2[2:tm + 2, :]
            du_ref[...] = du.astype(BF16)
            carry[slot] = dc[0:FFN_HALO, :]
            for k in range(KW):
                dw_ref[k:k + 1, :] += jnp.sum(dc * ext[lo + k:lo + k + tm, :], axis=0, keepdims=True)
            db_ref[...] += jnp.sum(dc, axis=0, keepdims=True)

        back(dcv, ext_v, cwv, 0, dv_ref, dwv_ref, dbv_ref)
        back(dcg, ext_g, cwg, 1, dg_ref, dwg_ref, dbg_ref)

    rev = lambda i: nrow - 1 - i
    tile = pl.BlockSpec((tm, tn), lambda j, i: (rev(i), j))
    halo = pl.BlockSpec((FFN_HALO, tn), lambda j, i: (jnp.maximum(rev(i) * hb - 1, 0), j))
    cspec = lambda rows, half: pl.BlockSpec((rows, tn), lambda j, i: (0, half * ncol + j))
    acc = lambda rows: pl.BlockSpec((rows, tn), lambda j, i: (0, j))
    big = jax.ShapeDtypeStruct((S, F), BF16)
    blk = _nbytes((tm, D), BF16) + _nbytes((tn, D), BF16) + 10 * _nbytes((tm, tn), F32)
    return pl.pallas_call(
        body, name=name, grid=(ncol, nrow),
        out_shape=(big, big, jax.ShapeDtypeStruct((KW, F), F32), jax.ShapeDtypeStruct((KW, F), F32),
                   jax.ShapeDtypeStruct((1, F), F32), jax.ShapeDtypeStruct((1, F), F32)),
        in_specs=[pl.BlockSpec((tm, D), lambda j, i: (rev(i), 0)), pl.BlockSpec((tn, D), lambda j, i: (j, 0)),
                  tile, tile, halo, halo, cspec(KW, 0), cspec(KW, 1), cspec(1, 0), cspec(1, 1)],
        out_specs=(tile, tile, acc(KW), acc(KW), acc(1), acc(1)),
        scratch_shapes=[pltpu.VMEM((2, FFN_HALO, tn), F32)],
        compiler_params=_params(blk, ("parallel", "arbitrary")))(
            dzb, w_down, up_v, up_g, up_v, up_g, conv_w, conv_w, conv_b, conv_b)


def loss_head(y, target, *, tm, name):
    S, D = y.shape

    def body(y_ref, t_ref, dy_ref, l_ref):
        @pl.when(pl.program_id(0) == 0)
        def _():
            l_ref[...] = jnp.zeros_like(l_ref)

        e = y_ref[...] - t_ref[...]
        dy_ref[...] = e * (1.0 / D)
        l_ref[...] += 0.5 * jnp.sum(jnp.mean(e * e, axis=-1, keepdims=True), axis=0, keepdims=True)

    row = lambda i: (i, 0)
    return pl.pallas_call(
        body, name=name, grid=(S // tm,),
        out_shape=(jax.ShapeDtypeStruct((S, D), F32), jax.ShapeDtypeStruct((1, 1), F32)),
        in_specs=[pl.BlockSpec((tm, D), row), pl.BlockSpec((tm, D), row)],
        out_specs=(pl.BlockSpec((tm, D), row), pl.BlockSpec((1, 1), lambda i: (0, 0))),
        compiler_params=_params(3 * _nbytes((tm, D), F32), ("arbitrary",)))(y, target)


def adamw(w, g, m, v, *, tr, name):
    R, C = w.shape
    c1 = 1.0 - ADAM_B1 ** ADAM_STEP
    c2 = 1.0 - ADAM_B2 ** ADAM_STEP

    def body(w_ref, g_ref, m_ref, v_ref, go_ref, d_ref, mo_ref, vo_ref):
        gv = g_ref[...]
        mn = ADAM_B1 * m_ref[...] + (1.0 - ADAM_B1) * gv
        vn = ADAM_B2 * v_ref[...] + (1.0 - ADAM_B2) * (gv * gv)
        go_ref[...] = gv
        mo_ref[...] = mn
        vo_ref[...] = vn
        d_ref[...] = -ADAM_LR * ((mn / c1) / (jnp.sqrt(vn / c2) + ADAM_EPS) + ADAM_WD * w_ref[...])

    spec = pl.BlockSpec((tr, C), lambda i: (i, 0))
    shape = jax.ShapeDtypeStruct((R, C), F32)
    return pl.pallas_call(
        body, name=name, grid=(R // tr,), out_shape=(shape,) * 4, in_specs=[spec] * 4, out_specs=(spec,) * 4,
        compiler_params=_params(8 * _nbytes((tr, C), F32), ("parallel",)))(w, g, m, v)


def add_pair(a, b, *, name):
    n, R, C = a.shape

    def body(a_ref, b_ref, o_ref):
        o_ref[...] = (a_ref[...].astype(F32) + b_ref[...].astype(F32)).astype(BF16)

    spec = pl.BlockSpec((None, R, C), lambda i: (i, 0, 0))
    return pl.pallas_call(
        body, name=name, grid=(n,), out_shape=jax.ShapeDtypeStruct(a.shape, BF16), in_specs=[spec, spec],
        out_specs=spec, compiler_params=_params(4 * _nbytes((R, C), F32), ("parallel",)))(a, b)


def sum_chips(b, *, name):
    n, R, C = b.shape
    tr = R // 2 if (R // 2) % 16 == 0 else R

    def body(b_ref, o_ref):
        acc = b_ref[0].astype(F32)
        for p in range(1, n):
            acc = acc + b_ref[p].astype(F32)
        o_ref[...] = acc

    return pl.pallas_call(
        body, name=name, grid=(R // tr,), out_shape=jax.ShapeDtypeStruct((R, C), F32),
        in_specs=[pl.BlockSpec((n, tr, C), lambda i: (0, i, 0))], out_specs=pl.BlockSpec((tr, C), lambda i: (i, 0)),
        compiler_params=_params(8 * _nbytes((tr, C), F32), ("parallel",)))(b)


_HBM = pl.BlockSpec(memory_space=pltpu.HBM)


def _place():
    x, y, c = lax.axis_index("x"), lax.axis_index("y"), lax.axis_index("c")
    chips = [(1 - x, y), (x, 1 - y), (1 - x, 1 - y)]
    return x, y, c, chips


def allgather_layer(shards, *, name):
    n = len(shards)

    def body(*refs):
        ins, outs = refs[:n], refs[n:2 * n]
        send_ici, recv_ici, send_d2d, recv_d2d, local = refs[2 * n:]
        x, y, c, chips = _place()
        me = 2 * x + y
        locals_ = [pltpu.make_async_copy(ins[w], outs[w].at[me], local.at[w]) for w in range(n)]
        for cp in locals_:
            cp.start()

        def ici(w, j):
            px, py = chips[j]
            return pltpu.make_async_remote_copy(
                src_ref=ins[w].at[c], dst_ref=outs[w].at[me, c], send_sem=send_ici.at[w, j],
                recv_sem=recv_ici.at[w, j], device_id=(px, py, c), device_id_type=MESH)

        def landed(w, j, half):
            px, py = chips[j]
            return outs[w].at[2 * px + py, half]

        def d2d(w, j, half):
            return pltpu.make_async_remote_copy(
                src_ref=landed(w, j, half), dst_ref=landed(w, j, half), send_sem=send_d2d.at[w, j],
                recv_sem=recv_d2d.at[w, j], device_id=(x, y, 1 - c), device_id_type=MESH)

        def ici_arrival(w, j):
            return pltpu.make_async_remote_copy(
                src_ref=landed(w, j, c), dst_ref=landed(w, j, c), send_sem=send_ici.at[w, j],
                recv_sem=recv_ici.at[w, j], device_id=(x, y, c), device_id_type=MESH)

        for w in range(n):
            for j in range(3):
                ici(w, j).start()
        for w in range(n):
            for j in range(3):
                ici_arrival(w, j).wait_recv()
                d2d(w, j, c).start()
        for w in range(n):
            for j in range(3):
                d2d(w, j, 1 - c).wait_recv()
        for w in range(n):
            for j in range(3):
                ici(w, j).wait_send()
                d2d(w, j, c).wait_send()
        for cp in locals_:
            cp.wait()

    out_shape = tuple(jax.ShapeDtypeStruct((N_CHIPS,) + s.shape, s.dtype) for s in shards)
    return pl.pallas_call(
        body, name=name, out_shape=out_shape, in_specs=[_HBM] * n, out_specs=(_HBM,) * n,
        scratch_shapes=[pltpu.SemaphoreType.DMA((n, 3))] * 4 + [pltpu.SemaphoreType.DMA((n,))],
    )(*shards)


def allgather_small(shards, *, name):
    n = len(shards)

    def body(*refs):
        ins, outs = refs[:n], refs[n:2 * n]
        send, recv, local = refs[2 * n:]
        x, y, c, chips = _place()
        me = 2 * x + y
        locals_ = [pltpu.make_async_copy(ins[w], outs[w].at[me], local.at[w]) for w in range(n)]
        for cp in locals_:
            cp.start()

        def copy(w, j):
            px, py = chips[j]
            return pltpu.make_async_remote_copy(
                src_ref=ins[w], dst_ref=outs[w].at[me], send_sem=send.at[w, j], recv_sem=recv.at[w, j],
                device_id=(px, py, c), device_id_type=MESH)

        def arrival(w, j):
            px, py = chips[j]
            blk = outs[w].at[2 * px + py]
            return pltpu.make_async_remote_copy(
                src_ref=blk, dst_ref=blk, send_sem=send.at[w, j], recv_sem=recv.at[w, j],
                device_id=(x, y, c), device_id_type=MESH)

        for w in range(n):
            for j in range(3):
                copy(w, j).start()
        for w in range(n):
            for j in range(3):
                arrival(w, j).wait_recv()
        for w in range(n):
            for j in range(3):
                copy(w, j).wait_send()
        for cp in locals_:
            cp.wait()

    out_shape = tuple(jax.ShapeDtypeStruct((N_CHIPS,) + s.shape, s.dtype) for s in shards)
    return pl.pallas_call(
        body, name=name, out_shape=out_shape, in_specs=[_HBM] * n, out_specs=(_HBM,) * n,
        scratch_shapes=[pltpu.SemaphoreType.DMA((n, 3))] * 2 + [pltpu.SemaphoreType.DMA((n,))],
    )(*shards)


def rs_sibling_swap(grads, *, name):
    n = len(grads)

    def body(*refs):
        ins, owns, gots = refs[:n], refs[n:2 * n], refs[2 * n:3 * n]
        send, recv, local = refs[3 * n:]
        x, y, c, _ = _place()
        locals_, swaps = [], []
        for w in range(n):
            for p in range(N_CHIPS):
                locals_.append(pltpu.make_async_copy(ins[w].at[p, c], owns[w].at[p], local.at[w, p]))
                swaps.append(pltpu.make_async_remote_copy(
                    src_ref=ins[w].at[p, 1 - c], dst_ref=gots[w].at[p], send_sem=send.at[w, p],
                    recv_sem=recv.at[w, p], device_id=(x, y, 1 - c), device_id_type=MESH))
        for cp in locals_ + swaps:
            cp.start()
        for cp in swaps:
            cp.wait_recv()
        for cp in swaps:
            cp.wait_send()
        for cp in locals_:
            cp.wait()

    half = tuple(jax.ShapeDtypeStruct((N_CHIPS,) + g.shape[2:], g.dtype) for g in grads)
    outs = pl.pallas_call(
        body, name=name, out_shape=half + half, in_specs=[_HBM] * n, out_specs=(_HBM,) * (2 * n),
        scratch_shapes=[pltpu.SemaphoreType.DMA((n, N_CHIPS))] * 3,
    )(*grads)
    return outs[:n], outs[n:]


def rs_chip_scatter(parts, *, name):
    n = len(parts)

    def body(*refs):
        ins, outs = refs[:n], refs[n:2 * n]
        send, recv, local = refs[2 * n:]
        x, y, c, chips = _place()
        me = 2 * x + y
        locals_ = [pltpu.make_async_copy(ins[w].at[me], outs[w].at[me], local.at[w]) for w in range(n)]
        for cp in locals_:
            cp.start()

        def copy(w, j):
            px, py = chips[j]
            return pltpu.make_async_remote_copy(
                src_ref=ins[w].at[2 * px + py], dst_ref=outs[w].at[me], send_sem=send.at[w, j],
                recv_sem=recv.at[w, j], device_id=(px, py, c), device_id_type=MESH)

        def arrival(w, j):
            px, py = chips[j]
            blk = outs[w].at[2 * px + py]
            return pltpu.make_async_remote_copy(
                src_ref=blk, dst_ref=blk, send_sem=send.at[w, j], recv_sem=recv.at[w, j],
                device_id=(x, y, c), device_id_type=MESH)

        for w in range(n):
            for j in range(3):
                copy(w, j).start()
        for w in range(n):
            for j in range(3):
                arrival(w, j).wait_recv()
        for w in range(n):
            for j in range(3):
                copy(w, j).wait_send()
        for cp in locals_:
            cp.wait()

    out_shape = tuple(jax.ShapeDtypeStruct(p.shape, p.dtype) for p in parts)
    return pl.pallas_call(
        body, name=name, out_shape=out_shape, in_specs=[_HBM] * n, out_specs=(_HBM,) * n,
        scratch_shapes=[pltpu.SemaphoreType.DMA((n, 3))] * 2 + [pltpu.SemaphoreType.DMA((n,))],
    )(*parts)


def rs_sibling_share(halves, n_layers, *, name):
    n = len(halves) // n_layers

    def body(*refs):
        ins, outs = refs[:n_layers * n], refs[n_layers * n:n_layers * n + n]
        send, recv, local = refs[n_layers * n + n:]
        x, y, c, _ = _place()
        locals_, shares, arrivals = [], [], []
        for l in range(n_layers):
            for w in range(n):
                src = ins[l * n + w]
                locals_.append(pltpu.make_async_copy(src, outs[w].at[l, c], local.at[l, w]))
                shares.append(pltpu.make_async_remote_copy(
                    src_ref=src, dst_ref=outs[w].at[l, c], send_sem=send.at[l, w], recv_sem=recv.at[l, w],
                    device_id=(x, y, 1 - c), device_id_type=MESH))
                other = outs[w].at[l, 1 - c]
                arrivals.append(pltpu.make_async_remote_copy(
                    src_ref=other, dst_ref=other, send_sem=send.at[l, w], recv_sem=recv.at[l, w],
                    device_id=(x, y, c), device_id_type=MESH))
        for cp in locals_ + shares:
            cp.start()
        for cp in arrivals:
            cp.wait_recv()
        for cp in shares:
            cp.wait_send()
        for cp in locals_:
            cp.wait()

    out_shape = tuple(jax.ShapeDtypeStruct((n_layers, 2) + halves[w].shape, F32) for w in range(n))
    return pl.pallas_call(
        body, name=name, out_shape=out_shape, in_specs=[_HBM] * (n_layers * n), out_specs=(_HBM,) * n,
        scratch_shapes=[pltpu.SemaphoreType.DMA((n_layers, n))] * 3,
    )(*halves)


def allreduce_small(v, *, name):
    R, C = v.shape

    def body(v_ref, o_ref, land, send, recv):
        x, y, c, _ = _place()
        me = 4 * x + 2 * y + c
        land[me] = v_ref[...]

        def flip(k):
            return (1 - x) if k & 4 else x, (1 - y) if k & 2 else y, (1 - c) if k & 1 else c

        copies = []
        for k in range(1, N_DEV):
            px, py, pc = flip(k)
            copies.append(pltpu.make_async_remote_copy(
                src_ref=v_ref, dst_ref=land.at[me], send_sem=send.at[k - 1], recv_sem=recv.at[k - 1],
                device_id=(px, py, pc), device_id_type=MESH))
        for cp in copies:
            cp.start()
        for k in range(1, N_DEV):
            px, py, pc = flip(k)
            blk = land.at[4 * px + 2 * py + pc]
            pltpu.make_async_remote_copy(
                src_ref=blk, dst_ref=blk, send_sem=send.at[k - 1], recv_sem=recv.at[k - 1],
                device_id=(x, y, c), device_id_type=MESH).wait_recv()
        for cp in copies:
            cp.wait_send()
        acc = land[0]
        for d in range(1, N_DEV):
            acc = acc + land[d]
        o_ref[...] = acc

    vm = pl.BlockSpec(memory_space=pltpu.VMEM)
    return pl.pallas_call(
        body, name=name, out_shape=jax.ShapeDtypeStruct((R, C), F32), in_specs=[vm], out_specs=vm,
        scratch_shapes=[pltpu.VMEM((N_DEV, R, C), F32), pltpu.SemaphoreType.DMA((N_DEV - 1,)),
                        pltpu.SemaphoreType.DMA((N_DEV - 1,))],
        compiler_params=pltpu.CompilerParams(vmem_limit_bytes=int(min(12 * R * C * 4 + (8 << 20), VMEM_CAP))),
    )(v)


def _pack(arrays):
    flat = jnp.concatenate([a.reshape(-1) for a in arrays])
    return flat.reshape(-1, LANES)


def _unpack(packed, shapes):
    flat = packed.reshape(-1)
    out, off = [], 0
    for s in shapes:
        n = 1
        for d in s:
            n *= d
        out.append(flat[off:off + n].reshape(s))
        off += n
    return out


def _row_tile(rows, cap=512):
    t = 1 << (cap.bit_length() - 1)
    while rows % t:
        t //= 2
    return t


def _adamw_tile(rows, cols):
    return _row_tile(rows, max(8, (1 << 20) // (4 * cols)))


def kernel(x, mem, w_in, conv_w, conv_b, conv_ln_g, conv_ln_b, w_out, ln1_g, ln1_b, mem_wq, mem_wk, mem_wv, mem_wo, ln2_g, ln2_b, ffn_up, ffn_conv_w, ffn_conv_b, ffn_down, ln3_g, ln3_b, loss_target, m_w_in, m_conv_w, m_conv_b, m_conv_ln_g, m_conv_ln_b, m_w_out, m_ln1_g, m_ln1_b, m_mem_wq, m_mem_wk, m_mem_wv, m_mem_wo, m_ln2_g, m_ln2_b, m_ffn_up, m_ffn_conv_w, m_ffn_conv_b, m_ffn_down, m_ln3_g, m_ln3_b, v_w_in, v_conv_w, v_conv_b, v_conv_ln_g, v_conv_ln_b, v_w_out, v_ln1_g, v_ln1_b, v_mem_wq, v_mem_wk, v_mem_wv, v_mem_wo, v_ln2_g, v_ln2_b, v_ffn_up, v_ffn_conv_w, v_ffn_conv_b, v_ffn_down, v_ln3_g, v_ln3_b):
    W = dict(w_in=w_in, conv_w=conv_w, conv_b=conv_b, conv_ln_g=conv_ln_g, conv_ln_b=conv_ln_b, w_out=w_out,
             ln1_g=ln1_g, ln1_b=ln1_b, mem_wq=mem_wq, mem_wk=mem_wk, mem_wv=mem_wv, mem_wo=mem_wo, ln2_g=ln2_g,
             ln2_b=ln2_b, ffn_up=ffn_up, ffn_conv_w=ffn_conv_w, ffn_conv_b=ffn_conv_b, ffn_down=ffn_down,
             ln3_g=ln3_g, ln3_b=ln3_b)
    M1 = dict(w_in=m_w_in, conv_w=m_conv_w, conv_b=m_conv_b, conv_ln_g=m_conv_ln_g, conv_ln_b=m_conv_ln_b,
              w_out=m_w_out, ln1_g=m_ln1_g, ln1_b=m_ln1_b, mem_wq=m_mem_wq, mem_wk=m_mem_wk, mem_wv=m_mem_wv,
              mem_wo=m_mem_wo, ln2_g=m_ln2_g, ln2_b=m_ln2_b, ffn_up=m_ffn_up, ffn_conv_w=m_ffn_conv_w,
              ffn_conv_b=m_ffn_conv_b, ffn_down=m_ffn_down, ln3_g=m_ln3_g, ln3_b=m_ln3_b)
    V2 = dict(w_in=v_w_in, conv_w=v_conv_w, conv_b=v_conv_b, conv_ln_g=v_conv_ln_g, conv_ln_b=v_conv_ln_b,
              w_out=v_w_out, ln1_g=v_ln1_g, ln1_b=v_ln1_b, mem_wq=v_mem_wq, mem_wk=v_mem_wk, mem_wv=v_mem_wv,
              mem_wo=v_mem_wo, ln2_g=v_ln2_g, ln2_b=v_ln2_b, ffn_up=v_ffn_up, ffn_conv_w=v_ffn_conv_w,
              ffn_conv_b=v_ffn_conv_b, ffn_down=v_ffn_down, ln3_g=v_ln3_g, ln3_b=v_ln3_b)

    L = w_in.shape[0]
    S, D = x.shape[1], x.shape[2]
    C = conv_b.shape[1]
    alpha = (2.0 * L) ** 0.25
    chip = 2 * lax.axis_index("x") + lax.axis_index("y")
    xs, mems, tgt = x[0], mem[0], loss_target[0]
    mem_bf = mems.astype(BF16)
    tm = _row_tile(S)
    tm_ffn = _row_tile(S, 256)

    full = []
    for l in range(L):
        shards = []
        for n in BIG:
            wl = W[n][l].astype(BF16)
            shards.append(wl.reshape(2, wl.shape[0] // 2, wl.shape[1]))
        got = allgather_layer(shards, name="allgather_layer")
        layer = {}
        for n, g in zip(BIG, got):
            rows, cols = W[n].shape[1], W[n].shape[2]
            layer[n] = g.reshape(N_CHIPS, rows, cols) if n in COL_SHARDED else g.reshape(N_CHIPS * rows, cols)
        full.append(layer)
    cw_all, fcw_all = allgather_small([conv_w, ffn_conv_w], name="allgather_small")
    cw_full = jnp.transpose(cw_all, (1, 2, 0, 3)).reshape(L, conv_w.shape[1], -1)
    fcw_full = jnp.transpose(fcw_all, (1, 2, 0, 3)).reshape(L, ffn_conv_w.shape[1], -1)

    saved = []
    h, hb = xs, xs.astype(BF16)
    for l in range(L):
        fw = full[l]
        s = dict(x=h, xb=hb)
        s['proj'] = mm_nn(hb, fw['w_in'], F32, tm=min(1024, S), tn=fw['w_in'].shape[2], name="proj")
        s['u1'] = conv_fwd(s['proj'], cw_full[l], conv_b[l][None], name="conv_fwd")
        s['o_sb'], s['ltot'] = sb_fwd(s['proj'], q_col=2 * C, name="sb_fwd")
        s['ua'] = ln_silu(s['u1'], s['o_sb'], conv_ln_g[l][None], conv_ln_b[l][None], tm=tm, name="ln_silu")
        s['x1'], s['x1b'], s['zh1'], s['rs1'] = mm_ln(
            s['ua'], fw['w_out'], h, ln1_g[l][None], ln1_b[l][None], alpha, tm=tm, name="out_proj_ln")
        s['q2'] = mm_nn(s['x1b'], fw['mem_wq'], BF16, tm=min(1024, S), tn=512, name="mem_q")
        s['k2'] = mm_nn(mem_bf, fw['mem_wk'], BF16, tm=mem_bf.shape[0], tn=512, name="mem_kv")
        s['v2'] = mm_nn(mem_bf, fw['mem_wv'], BF16, tm=mem_bf.shape[0], tn=512, name="mem_kv")
        s['o2'] = xattn_fwd(s['q2'], s['k2'], s['v2'], tm=tm, name="xattn_fwd")
        s['x2'], s['x2b'], s['zh2'], s['rs2'] = mm_ln(
            s['o2'], fw['mem_wo'], s['x1'], ln2_g[l][None], ln2_b[l][None], alpha, tm=tm, name="mem_o_ln")
        s['upv'], s['upg'], s['hmid'] = ffn_up_fwd(
            s['x2b'], fw['ffn_up'], fcw_full[l], ffn_conv_b[l][None], tm=tm_ffn, tn=fw['ffn_up'].shape[2],
            name="ffn_up_fwd")
        h, hb, s['zh3'], s['rs3'] = mm_ln(
            s['hmid'], fw['ffn_down'], s['x2'], ln3_g[l][None], ln3_b[l][None], alpha, tm=tm, name="ffn_down_ln")
        saved.append(s)

    dx, loss_part = loss_head(h, tgt, tm=tm, name="loss_head")
    loss = lax.psum(loss_part[0, 0], ("x", "y", "c"))

    big_grads = [None] * L
    small_grads = [None] * L
    for l in reversed(range(L)):
        fw, s = full[l], saved[l]
        g = {}
        dz3, dz3b, g['ln3_g'], g['ln3_b'] = ln_bwd(dx, s['zh3'], s['rs3'], ln3_g[l][None], tm=tm, name="ln_bwd")
        ftn = fw['ffn_up'].shape[2]
        dupv, dupg, dfw_v, dfw_g, dfb_v, dfb_g = ffn_mid_bwd(
            dz3b, fw['ffn_down'], s['upv'], s['upg'], fcw_full[l], ffn_conv_b[l][None], tm=tm_ffn, tn=ftn,
            name="ffn_mid_bwd")
        g['ffn_conv_w'] = jnp.concatenate([dfw_v, dfw_g], axis=1)
        g['ffn_conv_b'] = jnp.concatenate([dfb_v, dfb_g], axis=1)[0]
        g['ffn_down'] = mm_tn(s['hmid'], [dz3b], tk=ftn, tn=512, tmc=min(1024, S), name="grad_ffn_down")
        dx2 = mm_nt([dupv, dupg], fw['ffn_up'], F32, tm=tm, tk=512, res=dz3, alpha=alpha, name="ffn_up_bwd")
        g['ffn_up'] = mm_tn(s['x2b'], [dupv, dupg], tk=512, tn=ftn, shard_width=ftn, tmc=min(1024, S),
                            name="grad_ffn_up")

        dz2, dz2b, g['ln2_g'], g['ln2_b'] = ln_bwd(dx2, s['zh2'], s['rs2'], ln2_g[l][None], tm=tm, name="ln_bwd")
        do2 = mm_nt([dz2b], fw['mem_wo'], BF16, tm=tm, tk=512, name="mem_o_bwd")
        g['mem_wo'] = mm_tn(s['o2'], [dz2b], tk=512, tn=512, name="grad_sq")
        dq2, dk2, dv2 = xattn_bwd(s['q2'], do2, s['k2'], s['v2'], tm=tm, name="xattn_bwd")
        dx1 = mm_nt([dq2], fw['mem_wq'], F32, tm=tm, tk=512, res=dz2, alpha=alpha, name="mem_q_bwd")
        g['mem_wq'] = mm_tn(s['x1b'], [dq2], tk=512, tn=512, name="grad_sq")
        g['mem_wk'] = mm_tn(mem_bf, [dk2], tk=512, tn=512, name="grad_mem_kv")
        g['mem_wv'] = mm_tn(mem_bf, [dv2], tk=512, tn=512, name="grad_mem_kv")

        dz1, dz1b, g['ln1_g'], g['ln1_b'] = ln_bwd(dx1, s['zh1'], s['rs1'], ln1_g[l][None], tm=tm, name="ln_bwd")
        dua = mm_nt([dz1b], fw['w_out'], F32, tm=tm, tk=512, name="out_proj_bwd")
        g['w_out'] = mm_tn(s['ua'], [dz1b], tk=512, tn=512, name="grad_sq")
        dq, dk, dv = sb_bwd(s['proj'], s['ltot'], dua, q_col=2 * C, do_col=C, name="sb_bwd")
        du1, g['conv_ln_g'], g['conv_ln_b'] = ln_silu_bwd(
            dua, s['u1'], conv_ln_g[l][None], conv_ln_b[l][None], tm=tm, name="ln_silu_bwd")
        da, dg, g['conv_w'], dcb = conv_bwd(du1, s['proj'], cw_full[l], name="conv_bwd")
        g['conv_b'] = dcb
        dproj = jnp.concatenate([da, dg, dq, dk, dv], axis=1)
        ns_in = fw['w_in'].shape[2]
        dx = mm_nt([dproj], fw['w_in'], F32, tm=tm, tk=512, res=dz1, alpha=alpha, name="proj_bwd")
        g['w_in'] = mm_tn(s['xb'], [dproj], tk=512, tn=ns_in, shard_width=ns_in, name="grad_w_in")

        parts = []
        for n in BIG:
            rows, cols = W[n].shape[1], W[n].shape[2]
            parts.append(g[n].reshape(N_CHIPS, 2, rows // 2, cols))
        own, got = rs_sibling_swap(parts, name="rs_sibling_swap")
        pre = [add_pair(a, b, name="rs_add_pair") for a, b in zip(own, got)]
        scattered = rs_chip_scatter(pre, name="rs_chip_scatter")
        big_grads[l] = [sum_chips(b, name="rs_sum_chips") for b in scattered]
        small_grads[l] = {n: g[n].reshape(W[n].shape[1:-1] + (-1,)) for n in SMALL}

    grad_x = dx[None]

    halves = [hlf for l in range(L) for hlf in big_grads[l]]
    shared = rs_sibling_share(halves, L, name="rs_sibling_share")
    G = {}
    for n, sh in zip(BIG, shared):
        G[n] = sh.reshape(W[n].shape)

    small_full_shapes = []
    small_stack = []
    for n in SMALL:
        st = jnp.stack([small_grads[l][n] for l in range(L)])
        small_stack.append(st)
        small_full_shapes.append(st.shape)
    reduced = _unpack(allreduce_small(_pack(small_stack), name="allreduce_small"), small_full_shapes)
    for n, r in zip(SMALL, reduced):
        if n in SMALL_SHARDED:
            width = W[n].shape[-1]
            r = lax.dynamic_slice_in_dim(r, chip * width, width, axis=2)
        G[n] = r

    out_g, out_d, out_m, out_v = {}, {}, {}, {}
    for n in BIG:
        shp = W[n].shape
        flat = lambda a: a.reshape(shp[0] * shp[1], shp[2])
        res = adamw(flat(W[n]), flat(G[n]), flat(M1[n]), flat(V2[n]), tr=_adamw_tile(shp[0] * shp[1], shp[2]), name="adamw")
        out_g[n], out_d[n], out_m[n], out_v[n] = [r.reshape(shp) for r in res]
    small_shapes = [W[n].shape for n in SMALL]
    packed = [_pack([d[n] for n in SMALL]) for d in (W, G, M1, V2)]
    res = adamw(*packed, tr=packed[0].shape[0], name="adamw_small")
    for d, r in zip((out_g, out_d, out_m, out_v), res):
        for n, a in zip(SMALL, _unpack(r, small_shapes)):
            d[n] = a

    return (loss, grad_x, *[out_g[n] for n in WEIGHTS], *[out_d[n] for n in WEIGHTS],
            *[out_m[n] for n in WEIGHTS], *[out_v[n] for n in WEIGHTS])
```

```python
import functools

import jax
import jax.numpy as jnp
from jax import lax
from jax.experimental import pallas as pl
from jax.experimental.pallas import tpu as pltpu

F32 = jnp.float32
BF16 = jnp.bfloat16
MESH = pl.DeviceIdType.MESH

LN_EPS = 1e-5
SB_HEADS = 8
MEM_HEADS = 4
ADAM_LR, ADAM_B1, ADAM_B2, ADAM_EPS, ADAM_WD, ADAM_STEP = 0.001, 0.9, 0.999, 1e-08, 0.01, 10

LANES = 128
V7X_VMEM_BYTES = 64 << 20
VMEM_CAP = V7X_VMEM_BYTES - (6 << 20)
N_CHIPS = 4
N_DEV = 8

BIG = ('w_in', 'w_out', 'mem_wq', 'mem_wk', 'mem_wv', 'mem_wo', 'ffn_up', 'ffn_down')
COL_SHARDED = ('w_in', 'ffn_up')
SMALL = ('conv_w', 'conv_b', 'conv_ln_g', 'conv_ln_b', 'ln1_g', 'ln1_b', 'ln2_g', 'ln2_b',
         'ffn_conv_w', 'ffn_conv_b', 'ln3_g', 'ln3_b')
SMALL_SHARDED = ('conv_w', 'ffn_conv_w')
WEIGHTS = ('w_in', 'conv_w', 'conv_b', 'conv_ln_g', 'conv_ln_b', 'w_out', 'ln1_g', 'ln1_b',
           'mem_wq', 'mem_wk', 'mem_wv', 'mem_wo', 'ln2_g', 'ln2_b', 'ffn_up', 'ffn_conv_w',
           'ffn_conv_b', 'ffn_down', 'ln3_g', 'ln3_b')


def _params(block_bytes, semantics=None, **kw):
    limit = int(min(max(2 * block_bytes + (8 << 20), 32 << 20), VMEM_CAP))
    return pltpu.CompilerParams(dimension_semantics=semantics, vmem_limit_bytes=limit, **kw)


def _nbytes(shape, dtype):
    n = 1
    for s in shape:
        n *= s
    return n * jnp.dtype(dtype).itemsize


def _dot(a, b):
    return jnp.dot(a, b, preferred_element_type=F32)


def _dot_nt(a, b):
    return lax.dot_general(a, b, (((1,), (1,)), ((), ())), preferred_element_type=F32)


def _dot_tn(a, b):
    return lax.dot_general(a, b, (((0,), (0,)), ((), ())), preferred_element_type=F32)


def _sigmoid(x):
    return 1.0 / (1.0 + jnp.exp(-x))


def mm_nn(a, b, out_dtype, *, tm, tn, name):
    M, K = a.shape
    sharded = b.ndim == 3
    if sharded:
        nsh, _, ns = b.shape
        N, per = nsh * ns, ns // tn
        b_spec = pl.BlockSpec((None, K, tn), lambda i, j: (j // per, 0, j % per))
    else:
        N = b.shape[1]
        b_spec = pl.BlockSpec((K, tn), lambda i, j: (0, j))

    def body(a_ref, b_ref, o_ref):
        o_ref[...] = _dot(a_ref[...].astype(BF16), b_ref[...]).astype(o_ref.dtype)

    blk = _nbytes((tm, K), a.dtype) + _nbytes((K, tn), BF16) + _nbytes((tm, tn), out_dtype)
    return pl.pallas_call(
        body, name=name, out_shape=jax.ShapeDtypeStruct((M, N), out_dtype), grid=(M // tm, N // tn),
        in_specs=[pl.BlockSpec((tm, K), lambda i, j: (i, 0)), b_spec],
        out_specs=pl.BlockSpec((tm, tn), lambda i, j: (i, j)),
        compiler_params=_params(blk, ("parallel", "parallel")))(a, b)


def mm_ln(a, b, x, gamma, beta, alpha, *, tm, name):
    M, K = a.shape
    D = b.shape[1]

    def body(a_ref, b_ref, x_ref, g_ref, be_ref, y_ref, yb_ref, zh_ref, rs_ref):
        z = alpha * x_ref[...] + _dot(a_ref[...], b_ref[...])
        mu = jnp.mean(z, axis=-1, keepdims=True)
        zc = z - mu
        rstd = lax.rsqrt(jnp.mean(zc * zc, axis=-1, keepdims=True) + LN_EPS)
        zh = zc * rstd
        y = zh * g_ref[...] + be_ref[...]
        y_ref[...] = y
        yb_ref[...] = y.astype(BF16)
        zh_ref[...] = zh
        rs_ref[...] = rstd

    row = lambda i: (i, 0)
    fix = lambda i: (0, 0)
    blk = _nbytes((tm, K), BF16) + _nbytes((K, D), BF16) + 4 * _nbytes((tm, D), F32)
    return pl.pallas_call(
        body, name=name, grid=(M // tm,),
        out_shape=(jax.ShapeDtypeStruct((M, D), F32), jax.ShapeDtypeStruct((M, D), BF16),
                   jax.ShapeDtypeStruct((M, D), F32), jax.ShapeDtypeStruct((M, 1), F32)),
        in_specs=[pl.BlockSpec((tm, K), row), pl.BlockSpec((K, D), fix), pl.BlockSpec((tm, D), row),
                  pl.BlockSpec((1, D), fix), pl.BlockSpec((1, D), fix)],
        out_specs=(pl.BlockSpec((tm, D), row), pl.BlockSpec((tm, D), row), pl.BlockSpec((tm, D), row),
                   pl.BlockSpec((tm, 1), row)),
        compiler_params=_params(blk, ("parallel",)))(a, b, x, gamma, beta)


def ln_bwd(dy, zh, rstd, gamma, *, tm, name):
    M, D = dy.shape

    def body(dy_ref, zh_ref, rs_ref, g_ref, dz_ref, dzb_ref, dg_ref, db_ref):
        @pl.when(pl.program_id(0) == 0)
        def _():
            dg_ref[...] = jnp.zeros_like(dg_ref)
            db_ref[...] = jnp.zeros_like(db_ref)

        dyv, zhv = dy_ref[...], zh_ref[...]
        dg_ref[...] += jnp.sum(dyv * zhv, axis=0, keepdims=True)
        db_ref[...] += jnp.sum(dyv, axis=0, keepdims=True)
        dzh = dyv * g_ref[...]
        m1 = jnp.mean(dzh, axis=-1, keepdims=True)
        m2 = jnp.mean(dzh * zhv, axis=-1, keepdims=True)
        dz = rs_ref[...] * (dzh - m1 - zhv * m2)
        dz_ref[...] = dz
        dzb_ref[...] = dz.astype(BF16)

    row = lambda i: (i, 0)
    fix = lambda i: (0, 0)
    return pl.pallas_call(
        body, name=name, grid=(M // tm,),
        out_shape=(jax.ShapeDtypeStruct((M, D), F32), jax.ShapeDtypeStruct((M, D), BF16),
                   jax.ShapeDtypeStruct((1, D), F32), jax.ShapeDtypeStruct((1, D), F32)),
        in_specs=[pl.BlockSpec((tm, D), row), pl.BlockSpec((tm, D), row), pl.BlockSpec((tm, 1), row),
                  pl.BlockSpec((1, D), fix)],
        out_specs=(pl.BlockSpec((tm, D), row), pl.BlockSpec((tm, D), row), pl.BlockSpec((1, D), fix),
                   pl.BlockSpec((1, D), fix)),
        compiler_params=_params(4 * _nbytes((tm, D), F32), ("arbitrary",)))(dy, zh, rstd, gamma)


def mm_nt(a_list, b, out_dtype, *, tm, tk, name, res=None, alpha=None):
    M = a_list[0].shape[0]
    widths = [a.shape[1] for a in a_list]
    sharded = b.ndim == 3
    if sharded:
        nsh, K, ns = b.shape
        b_spec = pl.BlockSpec((nsh, tk, ns), lambda i, j: (0, j, 0))
        for w in widths:
            assert w % ns == 0
    else:
        K, N = b.shape
        ns = None
        b_spec = pl.BlockSpec((tk, N), lambda i, j: (j, 0))
    n_a = len(a_list)

    def body(*refs):
        a_refs, b_ref = refs[:n_a], refs[n_a]
        o_ref = refs[-1]
        acc = None
        off = 0
        for a_ref, w in zip(a_refs, widths):
            if sharded:
                for p in range(w // ns):
                    t = _dot_nt(a_ref[:, p * ns:(p + 1) * ns].astype(BF16), b_ref[off // ns + p])
                    acc = t if acc is None else acc + t
            else:
                t = _dot_nt(a_ref[...].astype(BF16), b_ref[:, off:off + w])
                acc = t if acc is None else acc + t
            off += w
        if res is not None:
            acc = acc + alpha * refs[n_a + 1][...]
        o_ref[...] = acc.astype(o_ref.dtype)

    in_specs = [pl.BlockSpec((tm, w), lambda i, j: (i, 0)) for w in widths] + [b_spec]
    args = list(a_list) + [b]
    if res is not None:
        in_specs.append(pl.BlockSpec((tm, tk), lambda i, j: (i, j)))
        args.append(res)
    blk = (sum(_nbytes((tm, w), a.dtype) for a, w in zip(a_list, widths)) + _nbytes((tk, sum(widths)), BF16)
           + 2 * _nbytes((tm, tk), F32))
    return pl.pallas_call(
        body, name=name, out_shape=jax.ShapeDtypeStruct((M, K), out_dtype), grid=(M // tm, K // tk),
        in_specs=in_specs, out_specs=pl.BlockSpec((tm, tk), lambda i, j: (i, j)),
        compiler_params=_params(blk, ("parallel", "parallel")))(*args)


def mm_tn(a, b_list, *, tk, tn, name, shard_width=None, tmc=None):
    M, K = a.shape
    tmc = M if tmc is None else tmc
    nm = M // tmc
    widths = [b.shape[1] for b in b_list]
    N = sum(widths)
    starts, s = [], 0
    for w in widths:
        assert w % tn == 0
        starts.append(s)
        s += w // tn
    n_b = len(b_list)

    def body(*refs):
        a_ref, b_refs, o_ref, acc = refs[0], refs[1:1 + n_b], refs[-2], refs[-1]
        j, m = pl.program_id(1), pl.program_id(2)
        for b_ref, st, w in zip(b_refs, starts, widths):
            @pl.when((j >= st) & (j < st + w // tn))
            def _(b_ref=b_ref):
                t = _dot_tn(a_ref[...].astype(BF16), b_ref[...].astype(BF16))
                if nm == 1:
                    o_ref[...] = t.astype(o_ref.dtype)
                else:
                    @pl.when(m == 0)
                    def _():
                        acc[...] = t

                    @pl.when(m > 0)
                    def _():
                        acc[...] += t

                    @pl.when(m == nm - 1)
                    def _():
                        o_ref[...] = acc[...].astype(o_ref.dtype)

    def b_map(st, w):
        nb = w // tn
        return lambda i, j, m: (jnp.where((j >= st) & (j < st + nb), m, 0), jnp.clip(j - st, 0, nb - 1))

    in_specs = [pl.BlockSpec((tmc, tk), lambda i, j, m: (m, i))]
    in_specs += [pl.BlockSpec((tmc, tn), b_map(st, w)) for st, w in zip(starts, widths)]
    if shard_width is None:
        out_shape = jax.ShapeDtypeStruct((K, N), BF16)
        out_spec = pl.BlockSpec((tk, tn), lambda i, j, m: (i, j))
    else:
        per = shard_width // tn
        out_shape = jax.ShapeDtypeStruct((N // shard_width, K, shard_width), BF16)
        out_spec = pl.BlockSpec((None, tk, tn), lambda i, j, m: (j // per, i, j % per))
    acc_shape = (tk, tn) if nm > 1 else (8, LANES)
    blk = (_nbytes((tmc, tk), a.dtype) + n_b * _nbytes((tmc, tn), b_list[0].dtype) + 2 * _nbytes((tk, tn), F32))
    return pl.pallas_call(
        body, name=name, out_shape=out_shape, grid=(K // tk, N // tn, nm), in_specs=in_specs, out_specs=out_spec,
        scratch_shapes=[pltpu.VMEM(acc_shape, F32)],
        compiler_params=_params(blk, ("parallel", "arbitrary", "arbitrary")))(a, *b_list)


CONV_PAD = 32
CONV_CHUNK = 512


def conv_fwd(proj, conv_w, conv_b, *, name):
    S = proj.shape[0]
    KW, C = conv_w.shape
    nct = C // LANES
    rc = min(CONV_CHUNK, S)

    def body(a_ref, g_ref, w_ref, b_ref, o_ref, pad):
        pad[0:CONV_PAD, :] = jnp.zeros((CONV_PAD, LANES), F32)
        pad[CONV_PAD:, :] = a_ref[...] * _sigmoid(g_ref[...])
        w = w_ref[...]
        bias = b_ref[...]

        def chunk(i, _):
            base = pl.multiple_of(i * rc, rc)
            win = pad[pl.ds(base, rc + CONV_PAD), :]
            acc = jnp.zeros((rc, LANES), F32) + bias
            for k in range(KW):
                off = CONV_PAD - (KW - 1) + k
                acc = acc + w[k:k + 1, :] * win[off:off + rc, :]
            o_ref[pl.ds(base, rc), :] = acc
            return 0

        lax.fori_loop(0, S // rc, chunk, 0)

    return pl.pallas_call(
        body, name=name, out_shape=jax.ShapeDtypeStruct((S, C), F32), grid=(nct,),
        in_specs=[pl.BlockSpec((S, LANES), lambda c: (0, c)), pl.BlockSpec((S, LANES), lambda c: (0, c + nct)),
                  pl.BlockSpec((KW, LANES), lambda c: (0, c)), pl.BlockSpec((1, LANES), lambda c: (0, c))],
        out_specs=pl.BlockSpec((S, LANES), lambda c: (0, c)),
        scratch_shapes=[pltpu.VMEM((S + CONV_PAD, LANES), F32)],
        compiler_params=_params(4 * _nbytes((S, LANES), F32), ("parallel",)))(proj, proj, conv_w, conv_b)


def conv_bwd(du1, proj, conv_w, *, name):
    S = proj.shape[0]
    KW, C = conv_w.shape
    nct = C // LANES
    rc = min(CONV_CHUNK, S)

    def body(d_ref, a_ref, g_ref, w_ref, da_ref, dg_ref, dw_ref, db_ref, pad_u, pad_d, du0, dw_acc):
        dw_acc[...] = jnp.zeros_like(dw_acc)
        pad_u[0:CONV_PAD, :] = jnp.zeros((CONV_PAD, LANES), F32)
        pad_u[CONV_PAD:, :] = a_ref[...] * _sigmoid(g_ref[...])
        pad_d[0:S, :] = d_ref[...]
        pad_d[S:, :] = jnp.zeros((CONV_PAD, LANES), F32)
        w = w_ref[...]
        db_ref[...] = jnp.sum(d_ref[...], axis=0, keepdims=True)

        def chunk(i, _):
            base = pl.multiple_of(i * rc, rc)
            d = pad_d[pl.ds(base, rc), :]
            win_u = pad_u[pl.ds(base, rc + CONV_PAD), :]
            win_d = pad_d[pl.ds(base, rc + CONV_PAD), :]
            acc = jnp.zeros((rc, LANES), F32)
            for k in range(KW):
                off = CONV_PAD - (KW - 1) + k
                dw_acc[k:k + 1, :] += jnp.sum(d * win_u[off:off + rc, :], axis=0, keepdims=True)
                acc = acc + w[k:k + 1, :] * win_d[KW - 1 - k:KW - 1 - k + rc, :]
            du0[pl.ds(base, rc), :] = acc
            return 0

        lax.fori_loop(0, S // rc, chunk, 0)
        dw_ref[...] = dw_acc[0:KW, :]
        a, sg = a_ref[...], _sigmoid(g_ref[...])
        d0 = du0[...]
        da_ref[...] = (d0 * sg).astype(BF16)
        dg_ref[...] = (d0 * a * sg * (1.0 - sg)).astype(BF16)

    col = lambda c: (0, c)
    return pl.pallas_call(
        body, name=name, grid=(nct,),
        out_shape=(jax.ShapeDtypeStruct((S, C), BF16), jax.ShapeDtypeStruct((S, C), BF16),
                   jax.ShapeDtypeStruct((KW, C), F32), jax.ShapeDtypeStruct((1, C), F32)),
        in_specs=[pl.BlockSpec((S, LANES), col), pl.BlockSpec((S, LANES), col),
                  pl.BlockSpec((S, LANES), lambda c: (0, c + nct)), pl.BlockSpec((KW, LANES), col)],
        out_specs=(pl.BlockSpec((S, LANES), col), pl.BlockSpec((S, LANES), col), pl.BlockSpec((KW, LANES), col),
                   pl.BlockSpec((1, LANES), col)),
        scratch_shapes=[pltpu.VMEM((S + CONV_PAD, LANES), F32), pltpu.VMEM((S + CONV_PAD, LANES), F32),
                        pltpu.VMEM((S, LANES), F32), pltpu.VMEM((CONV_PAD, LANES), F32)],
        compiler_params=_params(8 * _nbytes((S, LANES), F32), ("parallel",)))(du1, proj, proj, conv_w)


def ln_silu(u1, o_sb, gamma, beta, *, tm, name):
    S, C = u1.shape

    def body(u_ref, o_ref, g_ref, b_ref, out_ref):
        z = u_ref[...]
        mu = jnp.mean(z, axis=-1, keepdims=True)
        zc = z - mu
        y = zc * lax.rsqrt(jnp.mean(zc * zc, axis=-1, keepdims=True) + LN_EPS) * g_ref[...] + b_ref[...]
        out_ref[:, 0:C] = (y * _sigmoid(y)).astype(BF16)
        out_ref[:, C:] = o_ref[...].astype(BF16)

    row = lambda i: (i, 0)
    fix = lambda i: (0, 0)
    return pl.pallas_call(
        body, name=name, out_shape=jax.ShapeDtypeStruct((S, 2 * C), BF16), grid=(S // tm,),
        in_specs=[pl.BlockSpec((tm, C), row), pl.BlockSpec((tm, C), row), pl.BlockSpec((1, C), fix),
                  pl.BlockSpec((1, C), fix)],
        out_specs=pl.BlockSpec((tm, 2 * C), row),
        compiler_params=_params(4 * _nbytes((tm, C), F32), ("parallel",)))(u1, o_sb, gamma, beta)


def ln_silu_bwd(dua, u1, gamma, beta, *, tm, name):
    S, C = u1.shape

    def body(d_ref, u_ref, g_ref, b_ref, du1_ref, dg_ref, db_ref):
        @pl.when(pl.program_id(0) == 0)
        def _():
            dg_ref[...] = jnp.zeros_like(dg_ref)
            db_ref[...] = jnp.zeros_like(db_ref)

        z = u_ref[...]
        mu = jnp.mean(z, axis=-1, keepdims=True)
        zc = z - mu
        rstd = lax.rsqrt(jnp.mean(zc * zc, axis=-1, keepdims=True) + LN_EPS)
        zh = zc * rstd
        y = zh * g_ref[...] + b_ref[...]
        sg = _sigmoid(y)
        dy = d_ref[...] * (sg * (1.0 + y * (1.0 - sg)))
        dg_ref[...] += jnp.sum(dy * zh, axis=0, keepdims=True)
        db_ref[...] += jnp.sum(dy, axis=0, keepdims=True)
        dzh = dy * g_ref[...]
        m1 = jnp.mean(dzh, axis=-1, keepdims=True)
        m2 = jnp.mean(dzh * zh, axis=-1, keepdims=True)
        du1_ref[...] = rstd * (dzh - m1 - zh * m2)

    row = lambda i: (i, 0)
    fix = lambda i: (0, 0)
    return pl.pallas_call(
        body, name=name, grid=(S // tm,),
        out_shape=(jax.ShapeDtypeStruct((S, C), F32), jax.ShapeDtypeStruct((1, C), F32),
                   jax.ShapeDtypeStruct((1, C), F32)),
        in_specs=[pl.BlockSpec((tm, C), row), pl.BlockSpec((tm, C), row), pl.BlockSpec((1, C), fix),
                  pl.BlockSpec((1, C), fix)],
        out_specs=(pl.BlockSpec((tm, C), row), pl.BlockSpec((1, C), fix), pl.BlockSpec((1, C), fix)),
        compiler_params=_params(4 * _nbytes((tm, C), F32), ("arbitrary",)))(dua, u1, gamma, beta)


SB_BLOCK = 256
SB_STOP = -105.0


def _split_dot(x, tri):
    hi = x.astype(BF16)
    lo = (x - hi.astype(F32)).astype(BF16)
    return _dot(hi, tri) + _dot(lo, tri)


def _neg_softplus(z):
    return -(jnp.maximum(z, 0.0) + jnp.log(1.0 + jnp.exp(-jnp.abs(z))))


def sb_fwd(proj, *, q_col, name):
    S = proj.shape[0]
    dh = LANES // 2
    W = SB_HEADS * dh
    npair = W // LANES
    T = min(SB_BLOCK, S)
    nblk = S // T
    scale = dh ** -0.5
    qb0 = q_col // LANES

    def body(q_ref, k_ref, v_ref, o_ref, l_ref, qs, ks, vs):
        r_i = lax.broadcasted_iota(jnp.int32, (T, T), 0)
        c_i = lax.broadcasted_iota(jnp.int32, (T, T), 1)
        tri = (r_i >= c_i).astype(BF16)
        vis = c_i < r_i
        lane = lax.broadcasted_iota(jnp.int32, (T, dh), 1)

        for hh in range(2):
            sl = slice(hh * dh, (hh + 1) * dh)
            qs[hh] = (q_ref[:, sl] * scale).astype(BF16)
            ks[hh] = k_ref[:, sl].astype(BF16)
            vs[hh] = v_ref[:, sl].astype(BF16)

        def step(hh, qb, j0, diag, acc, R):
            kb = ks[hh, pl.ds(j0, T), :]
            vb = vs[hh, pl.ds(j0, T), :]
            z = _dot_nt(qb, kb)
            lk = _neg_softplus(z)
            if diag:
                lk = jnp.where(vis, lk, 0.0)
            C = _split_dot(lk, tri)
            A = jnp.exp(z + C + R)
            if diag:
                A = jnp.where(vis, A, 0.0)
            return acc + _dot(A.astype(BF16), vb), R + C[:, 0:1]

        def qblock(i, _):
            r0 = pl.multiple_of(i * T, T)
            qb = [qs[hh, pl.ds(r0, T), :] for hh in range(2)]
            state = ()
            for hh in range(2):
                state += step(hh, qb[hh], r0, True, jnp.zeros((T, dh), F32), jnp.zeros((T, 1), F32))

            def more(c):
                return (c[0] >= 0) & (jnp.max(jnp.maximum(c[2], c[4])) >= SB_STOP)

            def walk(c):
                j0 = pl.multiple_of(c[0] * T, T)
                return (c[0] - 1,) + step(0, qb[0], j0, False, c[1], c[2]) + step(1, qb[1], j0, False, c[3], c[4])

            c = lax.while_loop(more, walk, (i - 1,) + state)
            walked = (i - c[0]).astype(F32)
            for hh in range(2):
                sl = slice(hh * dh, (hh + 1) * dh)
                o_ref[pl.ds(r0, T), sl] = c[1 + 2 * hh]
                l_ref[pl.ds(r0, T), sl] = jnp.where(lane == 1, walked, c[2 + 2 * hh])
            return 0

        lax.fori_loop(0, nblk, qblock, 0)

    blk = lambda off: pl.BlockSpec((S, LANES), lambda h: (0, qb0 + off * npair + h))
    out = pl.BlockSpec((S, LANES), lambda h: (0, h))
    return pl.pallas_call(
        body, name=name, grid=(npair,),
        out_shape=(jax.ShapeDtypeStruct((S, W), F32), jax.ShapeDtypeStruct((S, W), F32)),
        in_specs=[blk(0), blk(1), blk(2)], out_specs=(out, out),
        scratch_shapes=[pltpu.VMEM((2, S, dh), BF16)] * 3,
        compiler_params=_params(6 * _nbytes((S, LANES), F32), ("parallel",)))(proj, proj, proj)


def sb_bwd(proj, ltot, dua, *, q_col, do_col, name):
    S = proj.shape[0]
    dh = LANES // 2
    W = SB_HEADS * dh
    npair = W // LANES
    T = min(SB_BLOCK, S)
    nblk = S // T
    scale = dh ** -0.5
    qb0 = q_col // LANES
    db0 = do_col // LANES

    def body(q_ref, k_ref, v_ref, l_ref, do_ref, dq_ref, dk_ref, dv_ref, qs, ks, vs, dos, dks, dvs):
        r_i = lax.broadcasted_iota(jnp.int32, (T, T), 0)
        c_i = lax.broadcasted_iota(jnp.int32, (T, T), 1)
        tri_rev = (r_i >= c_i).astype(BF16)
        tri_fwd = (r_i <= c_i).astype(BF16)
        vis = c_i < r_i

        for hh in range(2):
            sl = slice(hh * dh, (hh + 1) * dh)
            qs[hh] = (q_ref[:, sl] * scale).astype(BF16)
            ks[hh] = k_ref[:, sl].astype(BF16)
            vs[hh] = v_ref[:, sl].astype(BF16)
            dos[hh] = do_ref[:, sl].astype(BF16)
        dks[...] = jnp.zeros_like(dks)
        dvs[...] = jnp.zeros_like(dvs)

        def step(hh, qb, dob, Lt, j0, diag, dq, P, Gp):
            kb = ks[hh, pl.ds(j0, T), :]
            vb = vs[hh, pl.ds(j0, T), :]
            z = _dot_nt(qb, kb)
            lk = _neg_softplus(z)
            beta = jnp.exp(z + lk)
            if diag:
                lk = jnp.where(vis, lk, 0.0)
            C = _split_dot(lk, tri_rev)
            rowsum = C[:, 0:1]
            A = jnp.exp(z + C + (Lt - P - rowsum))
            if diag:
                A = jnp.where(vis, A, 0.0)
            g = A * _dot_nt(dob, vb)
            Gin = _split_dot(g, tri_fwd)
            dz = g - beta * (Gp + Gin)
            if diag:
                dz = jnp.where(vis, dz, 0.0)
            dzb = dz.astype(BF16)
            dks[hh, pl.ds(j0, T), :] += _dot_tn(dzb, qb)
            dvs[hh, pl.ds(j0, T), :] += _dot_tn(A.astype(BF16), dob)
            return dq + _dot(dzb, kb), P + rowsum, Gp + Gin[:, T - 1:T]

        def qblock(i, _):
            r0 = pl.multiple_of(i * T, T)
            qb = [qs[hh, pl.ds(r0, T), :] for hh in range(2)]
            dob = [dos[hh, pl.ds(r0, T), :] for hh in range(2)]
            Lt = [l_ref[pl.ds(r0, T), hh * dh:hh * dh + 1] for hh in range(2)]
            walked = jnp.clip(jnp.max(l_ref[pl.ds(r0, 8), 1:2]).astype(jnp.int32), 1, i + 1)

            def inner(j, c):
                j0 = pl.multiple_of(j * T, T)
                return (step(0, qb[0], dob[0], Lt[0], j0, False, *c[0:3])
                        + step(1, qb[1], dob[1], Lt[1], j0, False, *c[3:6]))

            zero = jnp.zeros((T, 1), F32)
            init = (jnp.zeros((T, dh), F32), zero, zero)
            c = lax.fori_loop(i + 1 - walked, i, inner, init + init)
            for hh in range(2):
                dq, _, _ = step(hh, qb[hh], dob[hh], Lt[hh], r0, True, *c[3 * hh:3 * hh + 3])
                dq_ref[pl.ds(r0, T), hh * dh:(hh + 1) * dh] = (dq * scale).astype(BF16)
            return 0

        lax.fori_loop(0, nblk, qblock, 0)
        for hh in range(2):
            sl = slice(hh * dh, (hh + 1) * dh)
            dk_ref[:, sl] = dks[hh].astype(BF16)
            dv_ref[:, sl] = dvs[hh].astype(BF16)

    blk = lambda off: pl.BlockSpec((S, LANES), lambda h: (0, qb0 + off * npair + h))
    out = pl.BlockSpec((S, LANES), lambda h: (0, h))
    o_shape = jax.ShapeDtypeStruct((S, W), BF16)
    return pl.pallas_call(
        body, name=name, grid=(npair,), out_shape=(o_shape, o_shape, o_shape),
        in_specs=[blk(0), blk(1), blk(2), out, pl.BlockSpec((S, LANES), lambda h: (0, db0 + h))],
        out_specs=(out, out, out),
        scratch_shapes=[pltpu.VMEM((2, S, dh), BF16)] * 4 + [pltpu.VMEM((2, S, dh), F32)] * 2,
        compiler_params=_params(12 * _nbytes((S, LANES), F32), ("parallel",)))(proj, proj, proj, ltot, dua)


def xattn_fwd(q, k, v, *, tm, name):
    S, D = q.shape
    Mlen = k.shape[0]
    hd = D // MEM_HEADS
    scale = hd ** -0.5

    def body(q_ref, k_ref, v_ref, o_ref):
        for h in range(MEM_HEADS):
            sl = slice(h * hd, (h + 1) * hd)
            s = _dot_nt(q_ref[:, sl], k_ref[:, sl]) * scale
            e = jnp.exp(s - jnp.max(s, axis=-1, keepdims=True))
            p = e / jnp.sum(e, axis=-1, keepdims=True)
            o_ref[:, sl] = _dot(p.astype(BF16), v_ref[:, sl]).astype(BF16)

    row = lambda i: (i, 0)
    fix = lambda i: (0, 0)
    return pl.pallas_call(
        body, name=name, out_shape=jax.ShapeDtypeStruct((S, D), BF16), grid=(S // tm,),
        in_specs=[pl.BlockSpec((tm, D), row), pl.BlockSpec((Mlen, D), fix), pl.BlockSpec((Mlen, D), fix)],
        out_specs=pl.BlockSpec((tm, D), row),
        compiler_params=_params(4 * _nbytes((tm, D), F32), ("parallel",)))(q, k, v)


def xattn_bwd(q, do, k, v, *, tm, name):
    S, D = q.shape
    Mlen = k.shape[0]
    hd = D // MEM_HEADS
    scale = hd ** -0.5

    def body(q_ref, do_ref, k_ref, v_ref, dq_ref, dk_ref, dv_ref):
        @pl.when(pl.program_id(0) == 0)
        def _():
            dk_ref[...] = jnp.zeros_like(dk_ref)
            dv_ref[...] = jnp.zeros_like(dv_ref)

        for h in range(MEM_HEADS):
            sl = slice(h * hd, (h + 1) * hd)
            qh, doh, kh, vh = q_ref[:, sl], do_ref[:, sl], k_ref[:, sl], v_ref[:, sl]
            s = _dot_nt(qh, kh) * scale
            e = jnp.exp(s - jnp.max(s, axis=-1, keepdims=True))
            p = e / jnp.sum(e, axis=-1, keepdims=True)
            dp = _dot_nt(doh, vh)
            ds = (p * (dp - jnp.sum(p * dp, axis=-1, keepdims=True)) * scale).astype(BF16)
            dq_ref[:, sl] = _dot(ds, kh).astype(BF16)
            dk_ref[:, sl] += _dot_tn(ds, qh)
            dv_ref[:, sl] += _dot_tn(p.astype(BF16), doh)

    row = lambda i: (i, 0)
    fix = lambda i: (0, 0)
    return pl.pallas_call(
        body, name=name, grid=(S // tm,),
        out_shape=(jax.ShapeDtypeStruct((S, D), BF16), jax.ShapeDtypeStruct((Mlen, D), F32),
                   jax.ShapeDtypeStruct((Mlen, D), F32)),
        in_specs=[pl.BlockSpec((tm, D), row), pl.BlockSpec((tm, D), row), pl.BlockSpec((Mlen, D), fix),
                  pl.BlockSpec((Mlen, D), fix)],
        out_specs=(pl.BlockSpec((tm, D), row), pl.BlockSpec((Mlen, D), fix), pl.BlockSpec((Mlen, D), fix)),
        compiler_params=_params(6 * _nbytes((tm, D), F32), ("arbitrary",)))(q, do, k, v)


FFN_HALO = 8


def _conv3(ext, w, lo):
    tm = ext.shape[0] - FFN_HALO
    return (w[0:1, :] * ext[lo:lo + tm, :] + w[1:2, :] * ext[lo + 1:lo + 1 + tm, :]
            + w[2:3, :] * ext[lo + 2:lo + 2 + tm, :])


def ffn_up_fwd(xb, w_up, conv_w, conv_b, *, tm, tn, name):
    S, D = xb.shape
    nsh, _, ns = w_up.shape
    F = nsh * ns // 2
    per = ns // tn
    ncol = F // tn
    KW = conv_w.shape[0]
    assert KW == 3

    def body(x_ref, wv_ref, wg_ref, cwv_ref, cwg_ref, cbv_ref, cbg_ref, uv_ref, ug_ref, h_ref, carry):
        @pl.when(pl.program_id(1) == 0)
        def _():
            carry[...] = jnp.zeros_like(carry)

        x = x_ref[...]
        uv = _dot(x, wv_ref[...])
        ug = _dot(x, wg_ref[...])
        uv_ref[...] = uv.astype(BF16)
        ug_ref[...] = ug.astype(BF16)
        lo = FFN_HALO - (KW - 1)
        cv = _conv3(jnp.concatenate([carry[0], uv], axis=0), cwv_ref[...], lo) + cbv_ref[...]
        cg = _conv3(jnp.concatenate([carry[1], ug], axis=0), cwg_ref[...], lo) + cbg_ref[...]
        carry[0] = uv[tm - FFN_HALO:, :]
        carry[1] = ug[tm - FFN_HALO:, :]
        h_ref[...] = (cg * _sigmoid(cg) * cv).astype(BF16)

    wspec = lambda half: pl.BlockSpec((None, D, tn), lambda j, i: (half * (nsh // 2) + j // per, 0, j % per))
    cspec = lambda rows, half: pl.BlockSpec((rows, tn), lambda j, i: (0, half * ncol + j))
    out = pl.BlockSpec((tm, tn), lambda j, i: (i, j))
    o_shape = jax.ShapeDtypeStruct((S, F), BF16)
    blk = _nbytes((tm, D), BF16) + 2 * _nbytes((D, tn), BF16) + 8 * _nbytes((tm, tn), F32)
    return pl.pallas_call(
        body, name=name, grid=(ncol, S // tm), out_shape=(o_shape, o_shape, o_shape),
        in_specs=[pl.BlockSpec((tm, D), lambda j, i: (i, 0)), wspec(0), wspec(1), cspec(KW, 0), cspec(KW, 1),
                  cspec(1, 0), cspec(1, 1)],
        out_specs=(out, out, out),
        scratch_shapes=[pltpu.VMEM((2, FFN_HALO, tn), F32)],
        compiler_params=_params(blk, ("parallel", "arbitrary")))(xb, w_up, w_up, conv_w, conv_w, conv_b, conv_b)


def ffn_mid_bwd(dzb, w_down, up_v, up_g, conv_w, conv_b, *, tm, tn, name):
    S, D = dzb.shape
    F = up_v.shape[1]
    ncol = F // tn
    nrow = S // tm
    KW = conv_w.shape[0]
    assert KW == 3
    hb = tm // FFN_HALO

    def body(dz_ref, wd_ref, uv_ref, ug_ref, hv_ref, hg_ref, cwv_ref, cwg_ref, cbv_ref, cbg_ref,
             dv_ref, dg_ref, dwv_ref, dwg_ref, dbv_ref, dbg_ref, carry):
        i = pl.program_id(1)

        @pl.when(i == 0)
        def _():
            carry[...] = jnp.zeros_like(carry)
            for r in (dwv_ref, dwg_ref, dbv_ref, dbg_ref):
                r[...] = jnp.zeros_like(r)

        first = i == nrow - 1
        halo_v = jnp.where(first, 0.0, hv_ref[...].astype(F32))
        halo_g = jnp.where(first, 0.0, hg_ref[...].astype(F32))
        ext_v = jnp.concatenate([halo_v, uv_ref[...].astype(F32)], axis=0)
        ext_g = jnp.concatenate([halo_g, ug_ref[...].astype(F32)], axis=0)
        cwv, cwg = cwv_ref[...], cwg_ref[...]
        lo = FFN_HALO - (KW - 1)
        cv = _conv3(ext_v, cwv, lo) + cbv_ref[...]
        cg = _conv3(ext_g, cwg, lo) + cbg_ref[...]
        dh = _dot_nt(dz_ref[...], wd_ref[...])
        sg = _sigmoid(cg)
        dcv = dh * (cg * sg)
        dcg = dh * cv * (sg * (1.0 + cg * (1.0 - sg)))

        def back(dc, ext, cw, slot, du_ref, dw_ref, db_ref):
            ext2 = jnp.concatenate([dc, carry[slot]], axis=0)
            du = cw[2:3, :] * ext2[0:tm, :] + cw[1:2, :] * ext2[1:tm + 1, :] + cw[0:1, :] * ext2[2:tm + 2, :]
            du_ref[...] = du.astype(BF16)
            carry[slot] = dc[0:FFN_HALO, :]
            for k in range(KW):
                dw_ref[k:k + 1, :] += jnp.sum(dc * ext[lo + k:lo + k + tm, :], axis=0, keepdims=True)
            db_ref[...] += jnp.sum(dc, axis=0, keepdims=True)

        back(dcv, ext_v, cwv, 0, dv_ref, dwv_ref, dbv_ref)
        back(dcg, ext_g, cwg, 1, dg_ref, dwg_ref, dbg_ref)

    rev = lambda i: nrow - 1 - i
    tile = pl.BlockSpec((tm, tn), lambda j, i: (rev(i), j))
    halo = pl.BlockSpec((FFN_HALO, tn), lambda j, i: (jnp.maximum(rev(i) * hb - 1, 0), j))
    cspec = lambda rows, half: pl.BlockSpec((rows, tn), lambda j, i: (0, half * ncol + j))
    acc = lambda rows: pl.BlockSpec((rows, tn), lambda j, i: (0, j))
    big = jax.ShapeDtypeStruct((S, F), BF16)
    blk = _nbytes((tm, D), BF16) + _nbytes((tn, D), BF16) + 10 * _nbytes((tm, tn), F32)
    return pl.pallas_call(
        body, name=name, grid=(ncol, nrow),
        out_shape=(big, big, jax.ShapeDtypeStruct((KW, F), F32), jax.ShapeDtypeStruct((KW, F), F32),
                   jax.ShapeDtypeStruct((1, F), F32), jax.ShapeDtypeStruct((1, F), F32)),
        in_specs=[pl.BlockSpec((tm, D), lambda j, i: (rev(i), 0)), pl.BlockSpec((tn, D), lambda j, i: (j, 0)),
                  tile, tile, halo, halo, cspec(KW, 0), cspec(KW, 1), cspec(1, 0), cspec(1, 1)],
        out_specs=(tile, tile, acc(KW), acc(KW), acc(1), acc(1)),
        scratch_shapes=[pltpu.VMEM((2, FFN_HALO, tn), F32)],
        compiler_params=_params(blk, ("parallel", "arbitrary")))(
            dzb, w_down, up_v, up_g, up_v, up_g, conv_w, conv_w, conv_b, conv_b)


def loss_head(y, target, *, tm, name):
    S, D = y.shape

    def body(y_ref, t_ref, dy_ref, l_ref):
        @pl.when(pl.program_id(0) == 0)
        def _():
            l_ref[...] = jnp.zeros_like(l_ref)

        e = y_ref[...] - t_ref[...]
        dy_ref[...] = e * (1.0 / D)
        l_ref[...] += 0.5 * jnp.sum(jnp.mean(e * e, axis=-1, keepdims=True), axis=0, keepdims=True)

    row = lambda i: (i, 0)
    return pl.pallas_call(
        body, name=name, grid=(S // tm,),
        out_shape=(jax.ShapeDtypeStruct((S, D), F32), jax.ShapeDtypeStruct((1, 1), F32)),
        in_specs=[pl.BlockSpec((tm, D), row), pl.BlockSpec((tm, D), row)],
        out_specs=(pl.BlockSpec((tm, D), row), pl.BlockSpec((1, 1), lambda i: (0, 0))),
        compiler_params=_params(3 * _nbytes((tm, D), F32), ("arbitrary",)))(y, target)


def adamw(w, g, m, v, *, tr, name):
    R, C = w.shape
    c1 = 1.0 - ADAM_B1 ** ADAM_STEP
    c2 = 1.0 - ADAM_B2 ** ADAM_STEP

    def body(w_ref, g_ref, m_ref, v_ref, go_ref, d_ref, mo_ref, vo_ref):
        gv = g_ref[...]
        mn = ADAM_B1 * m_ref[...] + (1.0 - ADAM_B1) * gv
        vn = ADAM_B2 * v_ref[...] + (1.0 - ADAM_B2) * (gv * gv)
        go_ref[...] = gv
        mo_ref[...] = mn
        vo_ref[...] = vn
        d_ref[...] = -ADAM_LR * ((mn / c1) / (jnp.sqrt(vn / c2) + ADAM_EPS) + ADAM_WD * w_ref[...])

    spec = pl.BlockSpec((tr, C), lambda i: (i, 0))
    shape = jax.ShapeDtypeStruct((R, C), F32)
    return pl.pallas_call(
        body, name=name, grid=(R // tr,), out_shape=(shape,) * 4, in_specs=[spec] * 4, out_specs=(spec,) * 4,
        compiler_params=_params(8 * _nbytes((tr, C), F32), ("parallel",)))(w, g, m, v)


def add_pair(g, got, core, *, name):
    n, _, R, C = g.shape

    def body(c_ref, a_ref, b_ref, o_ref):
        o_ref[...] = (a_ref[...].astype(F32) + b_ref[...].astype(F32)).astype(BF16)

    spec = pl.BlockSpec((None, R, C), lambda i, c: (i, 0, 0))
    grid_spec = pltpu.PrefetchScalarGridSpec(
        num_scalar_prefetch=1, grid=(n,),
        in_specs=[pl.BlockSpec((None, None, R, C), lambda i, c: (i, c[0], 0, 0)), spec], out_specs=spec)
    return pl.pallas_call(
        body, name=name, grid_spec=grid_spec, out_shape=jax.ShapeDtypeStruct(got.shape, BF16),
        compiler_params=_params(4 * _nbytes((R, C), F32), ("parallel",)))(core, g, got)


def sum_chips_into(b, dest, layer, core, *, name):
    n, R, C = b.shape
    tr = R // 2 if (R // 2) % 16 == 0 else R

    def body(c_ref, b_ref, d_ref, o_ref):
        acc = b_ref[0].astype(F32)
        for p in range(1, n):
            acc = acc + b_ref[p].astype(F32)
        o_ref[...] = acc

    grid_spec = pltpu.PrefetchScalarGridSpec(
        num_scalar_prefetch=1, grid=(R // tr,),
        in_specs=[pl.BlockSpec((n, tr, C), lambda i, c: (0, i, 0)), pl.BlockSpec(memory_space=pl.ANY)],
        out_specs=pl.BlockSpec((None, None, tr, C), lambda i, c: (layer, c[0], i, 0)))
    return pl.pallas_call(
        body, name=name, grid_spec=grid_spec, out_shape=jax.ShapeDtypeStruct(dest.shape, F32),
        input_output_aliases={2: 0},
        compiler_params=_params(8 * _nbytes((tr, C), F32), ("parallel",)))(core, b, dest)


_HBM = pl.BlockSpec(memory_space=pltpu.HBM)


def _place():
    x, y, c = lax.axis_index("x"), lax.axis_index("y"), lax.axis_index("c")
    chips = [(1 - x, y), (x, 1 - y), (1 - x, 1 - y)]
    return x, y, c, chips


def allgather_layer(shards, *, name):
    n = len(shards)

    def body(*refs):
        ins, outs = refs[:n], refs[n:2 * n]
        send_ici, recv_ici, send_d2d, recv_d2d, local = refs[2 * n:]
        x, y, c, chips = _place()
        me = 2 * x + y
        locals_ = [pltpu.make_async_copy(ins[w], outs[w].at[me], local.at[w]) for w in range(n)]
        for cp in locals_:
            cp.start()

        def ici(w, j):
            px, py = chips[j]
            return pltpu.make_async_remote_copy(
                src_ref=ins[w].at[c], dst_ref=outs[w].at[me, c], send_sem=send_ici.at[w, j],
                recv_sem=recv_ici.at[w, j], device_id=(px, py, c), device_id_type=MESH)

        def landed(w, j, half):
            px, py = chips[j]
            return outs[w].at[2 * px + py, half]

        def d2d(w, j, half):
            return pltpu.make_async_remote_copy(
                src_ref=landed(w, j, half), dst_ref=landed(w, j, half), send_sem=send_d2d.at[w, j],
                recv_sem=recv_d2d.at[w, j], device_id=(x, y, 1 - c), device_id_type=MESH)

        def ici_arrival(w, j):
            return pltpu.make_async_remote_copy(
                src_ref=landed(w, j, c), dst_ref=landed(w, j, c), send_sem=send_ici.at[w, j],
                recv_sem=recv_ici.at[w, j], device_id=(x, y, c), device_id_type=MESH)

        for w in range(n):
            for j in range(3):
                ici(w, j).start()
        for w in range(n):
            for j in range(3):
                ici_arrival(w, j).wait_recv()
                d2d(w, j, c).start()
        for w in range(n):
            for j in range(3):
                d2d(w, j, 1 - c).wait_recv()
        for w in range(n):
            for j in range(3):
                ici(w, j).wait_send()
                d2d(w, j, c).wait_send()
        for cp in locals_:
            cp.wait()

    out_shape = tuple(jax.ShapeDtypeStruct((N_CHIPS,) + s.shape, s.dtype) for s in shards)
    return pl.pallas_call(
        body, name=name, out_shape=out_shape, in_specs=[_HBM] * n, out_specs=(_HBM,) * n,
        scratch_shapes=[pltpu.SemaphoreType.DMA((n, 3))] * 4 + [pltpu.SemaphoreType.DMA((n,))],
    )(*shards)


def allgather_small(shards, *, name):
    n = len(shards)

    def body(*refs):
        ins, outs = refs[:n], refs[n:2 * n]
        send, recv, local = refs[2 * n:]
        x, y, c, chips = _place()
        me = 2 * x + y
        locals_ = [pltpu.make_async_copy(ins[w], outs[w].at[me], local.at[w]) for w in range(n)]
        for cp in locals_:
            cp.start()

        def copy(w, j):
            px, py = chips[j]
            return pltpu.make_async_remote_copy(
                src_ref=ins[w], dst_ref=outs[w].at[me], send_sem=send.at[w, j], recv_sem=recv.at[w, j],
                device_id=(px, py, c), device_id_type=MESH)

        def arrival(w, j):
            px, py = chips[j]
            blk = outs[w].at[2 * px + py]
            return pltpu.make_async_remote_copy(
                src_ref=blk, dst_ref=blk, send_sem=send.at[w, j], recv_sem=recv.at[w, j],
                device_id=(x, y, c), device_id_type=MESH)

        for w in range(n):
            for j in range(3):
                copy(w, j).start()
        for w in range(n):
            for j in range(3):
                arrival(w, j).wait_recv()
        for w in range(n):
            for j in range(3):
                copy(w, j).wait_send()
        for cp in locals_:
            cp.wait()

    out_shape = tuple(jax.ShapeDtypeStruct((N_CHIPS,) + s.shape, s.dtype) for s in shards)
    return pl.pallas_call(
        body, name=name, out_shape=out_shape, in_specs=[_HBM] * n, out_specs=(_HBM,) * n,
        scratch_shapes=[pltpu.SemaphoreType.DMA((n, 3))] * 2 + [pltpu.SemaphoreType.DMA((n,))],
    )(*shards)


def rs_sibling_swap(grads, *, name):
    n = len(grads)

    def body(*refs):
        ins, gots = refs[:n], refs[n:2 * n]
        send, recv = refs[2 * n:]
        x, y, c, _ = _place()
        swaps = [pltpu.make_async_remote_copy(
            src_ref=ins[w].at[:, 1 - c], dst_ref=gots[w], send_sem=send.at[w], recv_sem=recv.at[w],
            device_id=(x, y, 1 - c), device_id_type=MESH) for w in range(n)]
        for cp in swaps:
            cp.start()
        for cp in swaps:
            cp.wait_recv()
        for cp in swaps:
            cp.wait_send()

    half = tuple(jax.ShapeDtypeStruct((N_CHIPS,) + g.shape[2:], g.dtype) for g in grads)
    return pl.pallas_call(
        body, name=name, out_shape=half, in_specs=[_HBM] * n, out_specs=(_HBM,) * n,
        scratch_shapes=[pltpu.SemaphoreType.DMA((n,))] * 2,
    )(*grads)


def rs_chip_scatter(parts, *, name):
    n = len(parts)

    def body(*refs):
        ins, outs = refs[:n], refs[n:2 * n]
        send, recv, local = refs[2 * n:]
        x, y, c, chips = _place()
        me = 2 * x + y
        locals_ = [pltpu.make_async_copy(ins[w].at[me], outs[w].at[me], local.at[w]) for w in range(n)]
        for cp in locals_:
            cp.start()

        def copy(w, j):
            px, py = chips[j]
            return pltpu.make_async_remote_copy(
                src_ref=ins[w].at[2 * px + py], dst_ref=outs[w].at[me], send_sem=send.at[w, j],
                recv_sem=recv.at[w, j], device_id=(px, py, c), device_id_type=MESH)

        def arrival(w, j):
            px, py = chips[j]
            blk = outs[w].at[2 * px + py]
            return pltpu.make_async_remote_copy(
                src_ref=blk, dst_ref=blk, send_sem=send.at[w, j], recv_sem=recv.at[w, j],
                device_id=(x, y, c), device_id_type=MESH)

        for w in range(n):
            for j in range(3):
                copy(w, j).start()
        for w in range(n):
            for j in range(3):
                arrival(w, j).wait_recv()
        for w in range(n):
            for j in range(3):
                copy(w, j).wait_send()
        for cp in locals_:
            cp.wait()

    out_shape = tuple(jax.ShapeDtypeStruct(p.shape, p.dtype) for p in parts)
    return pl.pallas_call(
        body, name=name, out_shape=out_shape, in_specs=[_HBM] * n, out_specs=(_HBM,) * n,
        scratch_shapes=[pltpu.SemaphoreType.DMA((n, 3))] * 2 + [pltpu.SemaphoreType.DMA((n,))],
    )(*parts)


def rs_sibling_share(stacked, *, name):
    n = len(stacked)

    def body(*refs):
        bufs = refs[n:2 * n]
        send, recv = refs[2 * n:]
        x, y, c, _ = _place()
        shares, arrivals = [], []
        for w in range(n):
            mine, other = bufs[w].at[:, c], bufs[w].at[:, 1 - c]
            shares.append(pltpu.make_async_remote_copy(
                src_ref=mine, dst_ref=mine, send_sem=send.at[w], recv_sem=recv.at[w],
                device_id=(x, y, 1 - c), device_id_type=MESH))
            arrivals.append(pltpu.make_async_remote_copy(
                src_ref=other, dst_ref=other, send_sem=send.at[w], recv_sem=recv.at[w],
                device_id=(x, y, c), device_id_type=MESH))
        for cp in shares:
            cp.start()
        for cp in arrivals:
            cp.wait_recv()
        for cp in shares:
            cp.wait_send()

    out_shape = tuple(jax.ShapeDtypeStruct(s.shape, F32) for s in stacked)
    return pl.pallas_call(
        body, name=name, out_shape=out_shape, in_specs=[_HBM] * n, out_specs=(_HBM,) * n,
        input_output_aliases={w: w for w in range(n)},
        scratch_shapes=[pltpu.SemaphoreType.DMA((n,))] * 2,
    )(*stacked)


def allreduce_small(v, *, name):
    R, C = v.shape

    def body(v_ref, o_ref, land, send, recv):
        x, y, c, _ = _place()
        me = 4 * x + 2 * y + c
        land[me] = v_ref[...]

        def flip(k):
            return (1 - x) if k & 4 else x, (1 - y) if k & 2 else y, (1 - c) if k & 1 else c

        copies = []
        for k in range(1, N_DEV):
            px, py, pc = flip(k)
            copies.append(pltpu.make_async_remote_copy(
                src_ref=v_ref, dst_ref=land.at[me], send_sem=send.at[k - 1], recv_sem=recv.at[k - 1],
                device_id=(px, py, pc), device_id_type=MESH))
        for cp in copies:
            cp.start()
        for k in range(1, N_DEV):
            px, py, pc = flip(k)
            blk = land.at[4 * px + 2 * py + pc]
            pltpu.make_async_remote_copy(
                src_ref=blk, dst_ref=blk, send_sem=send.at[k - 1], recv_sem=recv.at[k - 1],
                device_id=(x, y, c), device_id_type=MESH).wait_recv()
        for cp in copies:
            cp.wait_send()
        acc = land[0]
        for d in range(1, N_DEV):
            acc = acc + land[d]
        o_ref[...] = acc

    vm = pl.BlockSpec(memory_space=pltpu.VMEM)
    return pl.pallas_call(
        body, name=name, out_shape=jax.ShapeDtypeStruct((R, C), F32), in_specs=[vm], out_specs=vm,
        scratch_shapes=[pltpu.VMEM((N_DEV, R, C), F32), pltpu.SemaphoreType.DMA((N_DEV - 1,)),
                        pltpu.SemaphoreType.DMA((N_DEV - 1,))],
        compiler_params=pltpu.CompilerParams(vmem_limit_bytes=int(min(12 * R * C * 4 + (8 << 20), VMEM_CAP))),
    )(v)


def _pack(arrays):
    flat = jnp.concatenate([a.reshape(-1) for a in arrays])
    return flat.reshape(-1, LANES)


def _unpack(packed, shapes):
    flat = packed.reshape(-1)
    out, off = [], 0
    for s in shapes:
        n = 1
        for d in s:
            n *= d
        out.append(flat[off:off + n].reshape(s))
        off += n
    return out


def _row_tile(rows, cap=512):
    t = 1 << (cap.bit_length() - 1)
    while rows % t:
        t //= 2
    return t


def _adamw_tile(rows, cols):
    return _row_tile(rows, max(8, (1 << 20) // (4 * cols)))


def kernel(x, mem, w_in, conv_w, conv_b, conv_ln_g, conv_ln_b, w_out, ln1_g, ln1_b, mem_wq, mem_wk, mem_wv, mem_wo, ln2_g, ln2_b, ffn_up, ffn_conv_w, ffn_conv_b, ffn_down, ln3_g, ln3_b, loss_target, m_w_in, m_conv_w, m_conv_b, m_conv_ln_g, m_conv_ln_b, m_w_out, m_ln1_g, m_ln1_b, m_mem_wq, m_mem_wk, m_mem_wv, m_mem_wo, m_ln2_g, m_ln2_b, m_ffn_up, m_ffn_conv_w, m_ffn_conv_b, m_ffn_down, m_ln3_g, m_ln3_b, v_w_in, v_conv_w, v_conv_b, v_conv_ln_g, v_conv_ln_b, v_w_out, v_ln1_g, v_ln1_b, v_mem_wq, v_mem_wk, v_mem_wv, v_mem_wo, v_ln2_g, v_ln2_b, v_ffn_up, v_ffn_conv_w, v_ffn_conv_b, v_ffn_down, v_ln3_g, v_ln3_b):
    W = dict(w_in=w_in, conv_w=conv_w, conv_b=conv_b, conv_ln_g=conv_ln_g, conv_ln_b=conv_ln_b, w_out=w_out,
             ln1_g=ln1_g, ln1_b=ln1_b, mem_wq=mem_wq, mem_wk=mem_wk, mem_wv=mem_wv, mem_wo=mem_wo, ln2_g=ln2_g,
             ln2_b=ln2_b, ffn_up=ffn_up, ffn_conv_w=ffn_conv_w, ffn_conv_b=ffn_conv_b, ffn_down=ffn_down,
             ln3_g=ln3_g, ln3_b=ln3_b)
    M1 = dict(w_in=m_w_in, conv_w=m_conv_w, conv_b=m_conv_b, conv_ln_g=m_conv_ln_g, conv_ln_b=m_conv_ln_b,
              w_out=m_w_out, ln1_g=m_ln1_g, ln1_b=m_ln1_b, mem_wq=m_mem_wq, mem_wk=m_mem_wk, mem_wv=m_mem_wv,
              mem_wo=m_mem_wo, ln2_g=m_ln2_g, ln2_b=m_ln2_b, ffn_up=m_ffn_up, ffn_conv_w=m_ffn_conv_w,
              ffn_conv_b=m_ffn_conv_b, ffn_down=m_ffn_down, ln3_g=m_ln3_g, ln3_b=m_ln3_b)
    V2 = dict(w_in=v_w_in, conv_w=v_conv_w, conv_b=v_conv_b, conv_ln_g=v_conv_ln_g, conv_ln_b=v_conv_ln_b,
              w_out=v_w_out, ln1_g=v_ln1_g, ln1_b=v_ln1_b, mem_wq=v_mem_wq, mem_wk=v_mem_wk, mem_wv=v_mem_wv,
              mem_wo=v_mem_wo, ln2_g=v_ln2_g, ln2_b=v_ln2_b, ffn_up=v_ffn_up, ffn_conv_w=v_ffn_conv_w,
              ffn_conv_b=v_ffn_conv_b, ffn_down=v_ffn_down, ln3_g=v_ln3_g, ln3_b=v_ln3_b)

    L = w_in.shape[0]
    S, D = x.shape[1], x.shape[2]
    C = conv_b.shape[1]
    alpha = (2.0 * L) ** 0.25
    chip = 2 * lax.axis_index("x") + lax.axis_index("y")
    xs, mems, tgt = x[0], mem[0], loss_target[0]
    mem_bf = mems.astype(BF16)
    tm = _row_tile(S)
    tm_ffn = _row_tile(S, 256)

    full = []
    for l in range(L):
        shards = []
        for n in BIG:
            wl = W[n][l].astype(BF16)
            shards.append(wl.reshape(2, wl.shape[0] // 2, wl.shape[1]))
        got = allgather_layer(shards, name="allgather_layer")
        layer = {}
        for n, g in zip(BIG, got):
            rows, cols = W[n].shape[1], W[n].shape[2]
            layer[n] = g.reshape(N_CHIPS, rows, cols) if n in COL_SHARDED else g.reshape(N_CHIPS * rows, cols)
        full.append(layer)
    cw_all, fcw_all = allgather_small([conv_w, ffn_conv_w], name="allgather_small")
    cw_full = jnp.transpose(cw_all, (1, 2, 0, 3)).reshape(L, conv_w.shape[1], -1)
    fcw_full = jnp.transpose(fcw_all, (1, 2, 0, 3)).reshape(L, ffn_conv_w.shape[1], -1)

    saved = []
    h, hb = xs, xs.astype(BF16)
    for l in range(L):
        fw = full[l]
        s = dict(x=h, xb=hb)
        s['proj'] = mm_nn(hb, fw['w_in'], F32, tm=min(1024, S), tn=fw['w_in'].shape[2], name="proj")
        s['u1'] = conv_fwd(s['proj'], cw_full[l], conv_b[l][None], name="conv_fwd")
        s['o_sb'], s['ltot'] = sb_fwd(s['proj'], q_col=2 * C, name="sb_fwd")
        s['ua'] = ln_silu(s['u1'], s['o_sb'], conv_ln_g[l][None], conv_ln_b[l][None], tm=tm, name="ln_silu")
        s['x1'], s['x1b'], s['zh1'], s['rs1'] = mm_ln(
            s['ua'], fw['w_out'], h, ln1_g[l][None], ln1_b[l][None], alpha, tm=tm, name="out_proj_ln")
        s['q2'] = mm_nn(s['x1b'], fw['mem_wq'], BF16, tm=min(1024, S), tn=512, name="mem_q")
        s['k2'] = mm_nn(mem_bf, fw['mem_wk'], BF16, tm=mem_bf.shape[0], tn=512, name="mem_kv")
        s['v2'] = mm_nn(mem_bf, fw['mem_wv'], BF16, tm=mem_bf.shape[0], tn=512, name="mem_kv")
        s['o2'] = xattn_fwd(s['q2'], s['k2'], s['v2'], tm=tm, name="xattn_fwd")
        s['x2'], s['x2b'], s['zh2'], s['rs2'] = mm_ln(
            s['o2'], fw['mem_wo'], s['x1'], ln2_g[l][None], ln2_b[l][None], alpha, tm=tm, name="mem_o_ln")
        s['upv'], s['upg'], s['hmid'] = ffn_up_fwd(
            s['x2b'], fw['ffn_up'], fcw_full[l], ffn_conv_b[l][None], tm=tm_ffn, tn=fw['ffn_up'].shape[2],
            name="ffn_up_fwd")
        h, hb, s['zh3'], s['rs3'] = mm_ln(
            s['hmid'], fw['ffn_down'], s['x2'], ln3_g[l][None], ln3_b[l][None], alpha, tm=tm, name="ffn_down_ln")
        saved.append(s)

    dx, loss_part = loss_head(h, tgt, tm=tm, name="loss_head")
    loss = lax.psum(loss_part[0, 0], ("x", "y", "c"))

    core = lax.axis_index("c").astype(jnp.int32).reshape(1)
    reduced_big = [lax.empty((L, 2, W[n].shape[1] // 2, W[n].shape[2]), F32) for n in BIG]
    small_grads = [None] * L
    for l in reversed(range(L)):
        fw, s = full[l], saved[l]
        g = {}
        dz3, dz3b, g['ln3_g'], g['ln3_b'] = ln_bwd(dx, s['zh3'], s['rs3'], ln3_g[l][None], tm=tm, name="ln_bwd")
        ftn = fw['ffn_up'].shape[2]
        dupv, dupg, dfw_v, dfw_g, dfb_v, dfb_g = ffn_mid_bwd(
            dz3b, fw['ffn_down'], s['upv'], s['upg'], fcw_full[l], ffn_conv_b[l][None], tm=tm_ffn, tn=ftn,
            name="ffn_mid_bwd")
        g['ffn_conv_w'] = jnp.concatenate([dfw_v, dfw_g], axis=1)
        g['ffn_conv_b'] = jnp.concatenate([dfb_v, dfb_g], axis=1)[0]
        g['ffn_down'] = mm_tn(s['hmid'], [dz3b], tk=ftn, tn=512, tmc=min(1024, S), name="grad_ffn_down")
        dx2 = mm_nt([dupv, dupg], fw['ffn_up'], F32, tm=tm, tk=512, res=dz3, alpha=alpha, name="ffn_up_bwd")
        g['ffn_up'] = mm_tn(s['x2b'], [dupv, dupg], tk=512, tn=ftn, shard_width=ftn, tmc=min(1024, S),
                            name="grad_ffn_up")

        dz2, dz2b, g['ln2_g'], g['ln2_b'] = ln_bwd(dx2, s['zh2'], s['rs2'], ln2_g[l][None], tm=tm, name="ln_bwd")
        do2 = mm_nt([dz2b], fw['mem_wo'], BF16, tm=tm, tk=512, name="mem_o_bwd")
        g['mem_wo'] = mm_tn(s['o2'], [dz2b], tk=512, tn=512, name="grad_sq")
        dq2, dk2, dv2 = xattn_bwd(s['q2'], do2, s['k2'], s['v2'], tm=tm, name="xattn_bwd")
        dx1 = mm_nt([dq2], fw['mem_wq'], F32, tm=tm, tk=512, res=dz2, alpha=alpha, name="mem_q_bwd")
        g['mem_wq'] = mm_tn(s['x1b'], [dq2], tk=512, tn=512, name="grad_sq")
        g['mem_wk'] = mm_tn(mem_bf, [dk2], tk=512, tn=512, name="grad_mem_kv")
        g['mem_wv'] = mm_tn(mem_bf, [dv2], tk=512, tn=512, name="grad_mem_kv")

        dz1, dz1b, g['ln1_g'], g['ln1_b'] = ln_bwd(dx1, s['zh1'], s['rs1'], ln1_g[l][None], tm=tm, name="ln_bwd")
        dua = mm_nt([dz1b], fw['w_out'], F32, tm=tm, tk=512, name="out_proj_bwd")
        g['w_out'] = mm_tn(s['ua'], [dz1b], tk=512, tn=512, name="grad_sq")
        dq, dk, dv = sb_bwd(s['proj'], s['ltot'], dua, q_col=2 * C, do_col=C, name="sb_bwd")
        du1, g['conv_ln_g'], g['conv_ln_b'] = ln_silu_bwd(
            dua, s['u1'], conv_ln_g[l][None], conv_ln_b[l][None], tm=tm, name="ln_silu_bwd")
        da, dg, g['conv_w'], dcb = conv_bwd(du1, s['proj'], cw_full[l], name="conv_bwd")
        g['conv_b'] = dcb
        dproj = jnp.concatenate([da, dg, dq, dk, dv], axis=1)
        ns_in = fw['w_in'].shape[2]
        dx = mm_nt([dproj], fw['w_in'], F32, tm=tm, tk=512, res=dz1, alpha=alpha, name="proj_bwd")
        g['w_in'] = mm_tn(s['xb'], [dproj], tk=512, tn=ns_in, shard_width=ns_in, name="grad_w_in")

        parts = []
        for n in BIG:
            rows, cols = W[n].shape[1], W[n].shape[2]
            parts.append(g[n].reshape(N_CHIPS, 2, rows // 2, cols))
        got = rs_sibling_swap(parts, name="rs_sibling_swap")
        pre = [add_pair(a, b, core, name="rs_add_pair") for a, b in zip(parts, got)]
        scattered = rs_chip_scatter(pre, name="rs_chip_scatter")
        reduced_big = [sum_chips_into(b, dest, l, core, name="rs_sum_chips")
                       for b, dest in zip(scattered, reduced_big)]
        small_grads[l] = {n: g[n].reshape(W[n].shape[1:-1] + (-1,)) for n in SMALL}

    grad_x = dx[None]

    shared = rs_sibling_share(reduced_big, name="rs_sibling_share")
    G = {}
    for n, sh in zip(BIG, shared):
        G[n] = sh.reshape(W[n].shape)

    small_full_shapes = []
    small_stack = []
    for n in SMALL:
        st = jnp.stack([small_grads[l][n] for l in range(L)])
        small_stack.append(st)
        small_full_shapes.append(st.shape)
    reduced = _unpack(allreduce_small(_pack(small_stack), name="allreduce_small"), small_full_shapes)
    for n, r in zip(SMALL, reduced):
        if n in SMALL_SHARDED:
            width = W[n].shape[-1]
            r = lax.dynamic_slice_in_dim(r, chip * width, width, axis=2)
        G[n] = r

    out_g, out_d, out_m, out_v = {}, {}, {}, {}
    for n in BIG:
        shp = W[n].shape
        flat = lambda a: a.reshape(shp[0] * shp[1], shp[2])
        res = adamw(flat(W[n]), flat(G[n]), flat(M1[n]), flat(V2[n]), tr=_adamw_tile(shp[0] * shp[1], shp[2]), name="adamw")
        out_g[n], out_d[n], out_m[n], out_v[n] = [r.reshape(shp) for r in res]
    small_shapes = [W[n].shape for n in SMALL]
    packed = [_pack([d[n] for n in SMALL]) for d in (W, G, M1, V2)]
    res = adamw(*packed, tr=packed[0].shape[0], name="adamw_small")
    for d, r in zip((out_g, out_d, out_m, out_v), res):
        for n, a in zip(SMALL, _unpack(r, small_shapes)):
            d[n] = a

    return (loss, grad_x, *[out_g[n] for n in WEIGHTS], *[out_d[n] for n in WEIGHTS],
            *[out_m[n] for n in WEIGHTS], *[out_v[n] for n in WEIGHTS])
```

```python
import functools

import jax
import jax.numpy as jnp
from jax import lax
from jax.experimental import pallas as pl
from jax.experimental.pallas import tpu as pltpu

F32 = jnp.float32
BF16 = jnp.bfloat16
MESH = pl.DeviceIdType.MESH

LN_EPS = 1e-5
SB_HEADS = 8
MEM_HEADS = 4
ADAM_LR, ADAM_B1, ADAM_B2, ADAM_EPS, ADAM_WD, ADAM_STEP = 0.001, 0.9, 0.999, 1e-08, 0.01, 10

LANES = 128
V7X_VMEM_BYTES = 64 << 20
VMEM_CAP = V7X_VMEM_BYTES - (6 << 20)
N_CHIPS = 4
N_DEV = 8

BIG = ('w_in', 'w_out', 'mem_wq', 'mem_wk', 'mem_wv', 'mem_wo', 'ffn_up', 'ffn_down')
RIDE_A = ('ffn_up', 'ffn_down', 'w_in')
RIDE_B = ('w_out', 'mem_wq', 'mem_wk', 'mem_wv', 'mem_wo')
COL_SHARDED = ('w_in', 'ffn_up')
SMALL = ('conv_w', 'conv_b', 'conv_ln_g', 'conv_ln_b', 'ln1_g', 'ln1_b', 'ln2_g', 'ln2_b',
         'ffn_conv_w', 'ffn_conv_b', 'ln3_g', 'ln3_b')
SMALL_SHARDED = ('conv_w', 'ffn_conv_w')
WEIGHTS = ('w_in', 'conv_w', 'conv_b', 'conv_ln_g', 'conv_ln_b', 'w_out', 'ln1_g', 'ln1_b',
           'mem_wq', 'mem_wk', 'mem_wv', 'mem_wo', 'ln2_g', 'ln2_b', 'ffn_up', 'ffn_conv_w',
           'ffn_conv_b', 'ffn_down', 'ln3_g', 'ln3_b')


def _params(block_bytes, semantics=None, **kw):
    limit = int(min(max(2 * block_bytes + (8 << 20), 32 << 20), VMEM_CAP))
    return pltpu.CompilerParams(dimension_semantics=semantics, vmem_limit_bytes=limit, **kw)


def _pallas(body, **kw):
    call = pl.pallas_call(body, **kw)

    def run(*args):
        return call(*[pltpu.with_memory_space_constraint(a, pltpu.HBM)
                      if jnp.issubdtype(a.dtype, jnp.floating) else a for a in args])

    return run


def _nbytes(shape, dtype):
    n = 1
    for s in shape:
        n *= s
    return n * jnp.dtype(dtype).itemsize


def _dot(a, b):
    return jnp.dot(a, b, preferred_element_type=F32)


def _dot_nt(a, b):
    return lax.dot_general(a, b, (((1,), (1,)), ((), ())), preferred_element_type=F32)


def _dot_tn(a, b):
    return lax.dot_general(a, b, (((0,), (0,)), ((), ())), preferred_element_type=F32)


def _sigmoid(x):
    return 1.0 / (1.0 + jnp.exp(-x))


def mm_nn(a, b, out_dtype, *, tm, tn, name):
    M, K = a.shape
    sharded = b.ndim == 3
    if sharded:
        nsh, _, ns = b.shape
        N, per = nsh * ns, ns // tn
        b_spec = pl.BlockSpec((None, K, tn), lambda i, j: (j // per, 0, j % per))
    else:
        N = b.shape[1]
        b_spec = pl.BlockSpec((K, tn), lambda i, j: (0, j))

    def body(a_ref, b_ref, o_ref):
        o_ref[...] = _dot(a_ref[...].astype(BF16), b_ref[...]).astype(o_ref.dtype)

    blk = _nbytes((tm, K), a.dtype) + _nbytes((K, tn), BF16) + _nbytes((tm, tn), out_dtype)
    return _pallas(
        body, name=name, out_shape=pltpu.HBM((M, N), out_dtype), grid=(M // tm, N // tn),
        in_specs=[pl.BlockSpec((tm, K), lambda i, j: (i, 0)), b_spec],
        out_specs=pl.BlockSpec((tm, tn), lambda i, j: (i, j)),
        compiler_params=_params(blk, ("parallel", "parallel")))(a, b)


def mm_ln(a, b, x, gamma, beta, alpha, *, tm, name):
    M, K = a.shape
    D = b.shape[1]

    def body(a_ref, b_ref, x_ref, g_ref, be_ref, y_ref, yb_ref, zh_ref, rs_ref):
        z = alpha * x_ref[...] + _dot(a_ref[...], b_ref[...])
        mu = jnp.mean(z, axis=-1, keepdims=True)
        zc = z - mu
        rstd = lax.rsqrt(jnp.mean(zc * zc, axis=-1, keepdims=True) + LN_EPS)
        zh = zc * rstd
        y = zh * g_ref[...] + be_ref[...]
        y_ref[...] = y
        yb_ref[...] = y.astype(BF16)
        zh_ref[...] = zh
        rs_ref[...] = rstd

    row = lambda i: (i, 0)
    fix = lambda i: (0, 0)
    blk = _nbytes((tm, K), BF16) + _nbytes((K, D), BF16) + 4 * _nbytes((tm, D), F32)
    return _pallas(
        body, name=name, grid=(M // tm,),
        out_shape=(pltpu.HBM((M, D), F32), pltpu.HBM((M, D), BF16),
                   pltpu.HBM((M, D), F32), pltpu.HBM((M, 1), F32)),
        in_specs=[pl.BlockSpec((tm, K), row), pl.BlockSpec((K, D), fix), pl.BlockSpec((tm, D), row),
                  pl.BlockSpec((1, D), fix), pl.BlockSpec((1, D), fix)],
        out_specs=(pl.BlockSpec((tm, D), row), pl.BlockSpec((tm, D), row), pl.BlockSpec((tm, D), row),
                   pl.BlockSpec((tm, 1), row)),
        compiler_params=_params(blk, ("parallel",)))(a, b, x, gamma, beta)


def ln_bwd(dy, zh, rstd, gamma, *, tm, name):
    M, D = dy.shape

    def body(dy_ref, zh_ref, rs_ref, g_ref, dz_ref, dzb_ref, dg_ref, db_ref):
        @pl.when(pl.program_id(0) == 0)
        def _():
            dg_ref[...] = jnp.zeros_like(dg_ref)
            db_ref[...] = jnp.zeros_like(db_ref)

        dyv, zhv = dy_ref[...], zh_ref[...]
        dg_ref[...] += jnp.sum(dyv * zhv, axis=0, keepdims=True)
        db_ref[...] += jnp.sum(dyv, axis=0, keepdims=True)
        dzh = dyv * g_ref[...]
        m1 = jnp.mean(dzh, axis=-1, keepdims=True)
        m2 = jnp.mean(dzh * zhv, axis=-1, keepdims=True)
        dz = rs_ref[...] * (dzh - m1 - zhv * m2)
        dz_ref[...] = dz
        dzb_ref[...] = dz.astype(BF16)

    row = lambda i: (i, 0)
    fix = lambda i: (0, 0)
    return _pallas(
        body, name=name, grid=(M // tm,),
        out_shape=(pltpu.HBM((M, D), F32), pltpu.HBM((M, D), BF16),
                   pltpu.HBM((1, D), F32), pltpu.HBM((1, D), F32)),
        in_specs=[pl.BlockSpec((tm, D), row), pl.BlockSpec((tm, D), row), pl.BlockSpec((tm, 1), row),
                  pl.BlockSpec((1, D), fix)],
        out_specs=(pl.BlockSpec((tm, D), row), pl.BlockSpec((tm, D), row), pl.BlockSpec((1, D), fix),
                   pl.BlockSpec((1, D), fix)),
        compiler_params=_params(4 * _nbytes((tm, D), F32), ("arbitrary",)))(dy, zh, rstd, gamma)


def mm_nt(a_list, b, out_dtype, *, tm, tk, name, res=None, alpha=None):
    M = a_list[0].shape[0]
    widths = [a.shape[1] for a in a_list]
    sharded = b.ndim == 3
    if sharded:
        nsh, K, ns = b.shape
        b_spec = pl.BlockSpec((nsh, tk, ns), lambda i, j: (0, j, 0))
        for w in widths:
            assert w % ns == 0
    else:
        K, N = b.shape
        ns = None
        b_spec = pl.BlockSpec((tk, N), lambda i, j: (j, 0))
    n_a = len(a_list)

    def body(*refs):
        a_refs, b_ref = refs[:n_a], refs[n_a]
        o_ref = refs[-1]
        acc = None
        off = 0
        for a_ref, w in zip(a_refs, widths):
            if sharded:
                for p in range(w // ns):
                    t = _dot_nt(a_ref[:, p * ns:(p + 1) * ns].astype(BF16), b_ref[off // ns + p])
                    acc = t if acc is None else acc + t
            else:
                t = _dot_nt(a_ref[...].astype(BF16), b_ref[:, off:off + w])
                acc = t if acc is None else acc + t
            off += w
        if res is not None:
            acc = acc + alpha * refs[n_a + 1][...]
        o_ref[...] = acc.astype(o_ref.dtype)

    in_specs = [pl.BlockSpec((tm, w), lambda i, j: (i, 0)) for w in widths] + [b_spec]
    args = list(a_list) + [b]
    if res is not None:
        in_specs.append(pl.BlockSpec((tm, tk), lambda i, j: (i, j)))
        args.append(res)
    blk = (sum(_nbytes((tm, w), a.dtype) for a, w in zip(a_list, widths)) + _nbytes((tk, sum(widths)), BF16)
           + 2 * _nbytes((tm, tk), F32))
    return _pallas(
        body, name=name, out_shape=pltpu.HBM((M, K), out_dtype), grid=(M // tm, K // tk),
        in_specs=in_specs, out_specs=pl.BlockSpec((tm, tk), lambda i, j: (i, j)),
        compiler_params=_params(blk, ("parallel", "parallel")))(*args)


def mm_tn(a, b_list, *, tk, tn, name, shard_width=None, tmc=None):
    M, K = a.shape
    tmc = M if tmc is None else tmc
    nm = M // tmc
    widths = [b.shape[1] for b in b_list]
    N = sum(widths)
    starts, s = [], 0
    for w in widths:
        assert w % tn == 0
        starts.append(s)
        s += w // tn
    n_b = len(b_list)

    def body(*refs):
        a_ref, b_refs, o_ref, acc = refs[0], refs[1:1 + n_b], refs[-2], refs[-1]
        j, m = pl.program_id(1), pl.program_id(2)
        for b_ref, st, w in zip(b_refs, starts, widths):
            @pl.when((j >= st) & (j < st + w // tn))
            def _(b_ref=b_ref):
                t = _dot_tn(a_ref[...].astype(BF16), b_ref[...].astype(BF16))
                if nm == 1:
                    o_ref[...] = t.astype(o_ref.dtype)
                else:
                    @pl.when(m == 0)
                    def _():
                        acc[...] = t

                    @pl.when(m > 0)
                    def _():
                        acc[...] += t

                    @pl.when(m == nm - 1)
                    def _():
                        o_ref[...] = acc[...].astype(o_ref.dtype)

    def b_map(st, w):
        nb = w // tn
        return lambda i, j, m: (jnp.where((j >= st) & (j < st + nb), m, 0), jnp.clip(j - st, 0, nb - 1))

    in_specs = [pl.BlockSpec((tmc, tk), lambda i, j, m: (m, i))]
    in_specs += [pl.BlockSpec((tmc, tn), b_map(st, w)) for st, w in zip(starts, widths)]
    if shard_width is None:
        out_shape = pltpu.HBM((K, N), BF16)
        out_spec = pl.BlockSpec((tk, tn), lambda i, j, m: (i, j))
    else:
        per = shard_width // tn
        out_shape = pltpu.HBM((N // shard_width, K, shard_width), BF16)
        out_spec = pl.BlockSpec((None, tk, tn), lambda i, j, m: (j // per, i, j % per))
    acc_shape = (tk, tn) if nm > 1 else (8, LANES)
    blk = (_nbytes((tmc, tk), a.dtype) + n_b * _nbytes((tmc, tn), b_list[0].dtype) + 2 * _nbytes((tk, tn), F32))
    return _pallas(
        body, name=name, out_shape=out_shape, grid=(K // tk, N // tn, nm), in_specs=in_specs, out_specs=out_spec,
        scratch_shapes=[pltpu.VMEM(acc_shape, F32)],
        compiler_params=_params(blk, ("parallel", "arbitrary", "arbitrary")))(a, *b_list)


CONV_PAD = 32
CONV_CHUNK = 512


def conv_fwd(proj, conv_w, conv_b, *, name):
    S = proj.shape[0]
    KW, C = conv_w.shape
    nct = C // LANES
    rc = min(CONV_CHUNK, S)

    def body(a_ref, g_ref, w_ref, b_ref, o_ref, pad):
        pad[0:CONV_PAD, :] = jnp.zeros((CONV_PAD, LANES), F32)
        pad[CONV_PAD:, :] = a_ref[...] * _sigmoid(g_ref[...])
        w = w_ref[...]
        bias = b_ref[...]

        def chunk(i, _):
            base = pl.multiple_of(i * rc, rc)
            win = pad[pl.ds(base, rc + CONV_PAD), :]
            acc = jnp.zeros((rc, LANES), F32) + bias
            for k in range(KW):
                off = CONV_PAD - (KW - 1) + k
                acc = acc + w[k:k + 1, :] * win[off:off + rc, :]
            o_ref[pl.ds(base, rc), :] = acc
            return 0

        lax.fori_loop(0, S // rc, chunk, 0)

    return _pallas(
        body, name=name, out_shape=pltpu.HBM((S, C), F32), grid=(nct,),
        in_specs=[pl.BlockSpec((S, LANES), lambda c: (0, c)), pl.BlockSpec((S, LANES), lambda c: (0, c + nct)),
                  pl.BlockSpec((KW, LANES), lambda c: (0, c)), pl.BlockSpec((1, LANES), lambda c: (0, c))],
        out_specs=pl.BlockSpec((S, LANES), lambda c: (0, c)),
        scratch_shapes=[pltpu.VMEM((S + CONV_PAD, LANES), F32)],
        compiler_params=_params(4 * _nbytes((S, LANES), F32), ("parallel",)))(proj, proj, conv_w, conv_b)


def conv_bwd(du1, proj, conv_w, *, name):
    S = proj.shape[0]
    KW, C = conv_w.shape
    nct = C // LANES
    rc = min(CONV_CHUNK, S)

    def body(d_ref, a_ref, g_ref, w_ref, da_ref, dg_ref, dw_ref, db_ref, pad_u, pad_d, du0, dw_acc):
        dw_acc[...] = jnp.zeros_like(dw_acc)
        pad_u[0:CONV_PAD, :] = jnp.zeros((CONV_PAD, LANES), F32)
        pad_u[CONV_PAD:, :] = a_ref[...] * _sigmoid(g_ref[...])
        pad_d[0:S, :] = d_ref[...]
        pad_d[S:, :] = jnp.zeros((CONV_PAD, LANES), F32)
        w = w_ref[...]
        db_ref[...] = jnp.sum(d_ref[...], axis=0, keepdims=True)

        def chunk(i, _):
            base = pl.multiple_of(i * rc, rc)
            d = pad_d[pl.ds(base, rc), :]
            win_u = pad_u[pl.ds(base, rc + CONV_PAD), :]
            win_d = pad_d[pl.ds(base, rc + CONV_PAD), :]
            acc = jnp.zeros((rc, LANES), F32)
            for k in range(KW):
                off = CONV_PAD - (KW - 1) + k
                dw_acc[k:k + 1, :] += jnp.sum(d * win_u[off:off + rc, :], axis=0, keepdims=True)
                acc = acc + w[k:k + 1, :] * win_d[KW - 1 - k:KW - 1 - k + rc, :]
            du0[pl.ds(base, rc), :] = acc
            return 0

        lax.fori_loop(0, S // rc, chunk, 0)
        dw_ref[...] = dw_acc[0:KW, :]
        a, sg = a_ref[...], _sigmoid(g_ref[...])
        d0 = du0[...]
        da_ref[...] = (d0 * sg).astype(BF16)
        dg_ref[...] = (d0 * a * sg * (1.0 - sg)).astype(BF16)

    col = lambda c: (0, c)
    return _pallas(
        body, name=name, grid=(nct,),
        out_shape=(pltpu.HBM((S, C), BF16), pltpu.HBM((S, C), BF16),
                   pltpu.HBM((KW, C), F32), pltpu.HBM((1, C), F32)),
        in_specs=[pl.BlockSpec((S, LANES), col), pl.BlockSpec((S, LANES), col),
                  pl.BlockSpec((S, LANES), lambda c: (0, c + nct)), pl.BlockSpec((KW, LANES), col)],
        out_specs=(pl.BlockSpec((S, LANES), col), pl.BlockSpec((S, LANES), col), pl.BlockSpec((KW, LANES), col),
                   pl.BlockSpec((1, LANES), col)),
        scratch_shapes=[pltpu.VMEM((S + CONV_PAD, LANES), F32), pltpu.VMEM((S + CONV_PAD, LANES), F32),
                        pltpu.VMEM((S, LANES), F32), pltpu.VMEM((CONV_PAD, LANES), F32)],
        compiler_params=_params(8 * _nbytes((S, LANES), F32), ("parallel",)))(du1, proj, proj, conv_w)


def ln_silu(u1, o_sb, gamma, beta, *, tm, name):
    S, C = u1.shape

    def body(u_ref, o_ref, g_ref, b_ref, out_ref):
        z = u_ref[...]
        mu = jnp.mean(z, axis=-1, keepdims=True)
        zc = z - mu
        y = zc * lax.rsqrt(jnp.mean(zc * zc, axis=-1, keepdims=True) + LN_EPS) * g_ref[...] + b_ref[...]
        out_ref[:, 0:C] = (y * _sigmoid(y)).astype(BF16)
        out_ref[:, C:] = o_ref[...].astype(BF16)

    row = lambda i: (i, 0)
    fix = lambda i: (0, 0)
    return _pallas(
        body, name=name, out_shape=pltpu.HBM((S, 2 * C), BF16), grid=(S // tm,),
        in_specs=[pl.BlockSpec((tm, C), row), pl.BlockSpec((tm, C), row), pl.BlockSpec((1, C), fix),
                  pl.BlockSpec((1, C), fix)],
        out_specs=pl.BlockSpec((tm, 2 * C), row),
        compiler_params=_params(4 * _nbytes((tm, C), F32), ("parallel",)))(u1, o_sb, gamma, beta)


def ln_silu_bwd(dua, u1, gamma, beta, *, tm, name):
    S, C = u1.shape

    def body(d_ref, u_ref, g_ref, b_ref, du1_ref, dg_ref, db_ref):
        @pl.when(pl.program_id(0) == 0)
        def _():
            dg_ref[...] = jnp.zeros_like(dg_ref)
            db_ref[...] = jnp.zeros_like(db_ref)

        z = u_ref[...]
        mu = jnp.mean(z, axis=-1, keepdims=True)
        zc = z - mu
        rstd = lax.rsqrt(jnp.mean(zc * zc, axis=-1, keepdims=True) + LN_EPS)
        zh = zc * rstd
        y = zh * g_ref[...] + b_ref[...]
        sg = _sigmoid(y)
        dy = d_ref[...] * (sg * (1.0 + y * (1.0 - sg)))
        dg_ref[...] += jnp.sum(dy * zh, axis=0, keepdims=True)
        db_ref[...] += jnp.sum(dy, axis=0, keepdims=True)
        dzh = dy * g_ref[...]
        m1 = jnp.mean(dzh, axis=-1, keepdims=True)
        m2 = jnp.mean(dzh * zh, axis=-1, keepdims=True)
        du1_ref[...] = rstd * (dzh - m1 - zh * m2)

    row = lambda i: (i, 0)
    fix = lambda i: (0, 0)
    return _pallas(
        body, name=name, grid=(S // tm,),
        out_shape=(pltpu.HBM((S, C), F32), pltpu.HBM((1, C), F32),
                   pltpu.HBM((1, C), F32)),
        in_specs=[pl.BlockSpec((tm, C), row), pl.BlockSpec((tm, C), row), pl.BlockSpec((1, C), fix),
                  pl.BlockSpec((1, C), fix)],
        out_specs=(pl.BlockSpec((tm, C), row), pl.BlockSpec((1, C), fix), pl.BlockSpec((1, C), fix)),
        compiler_params=_params(4 * _nbytes((tm, C), F32), ("arbitrary",)))(dua, u1, gamma, beta)


SB_BLOCK = 256
SB_STOP = -105.0


def _split_dot(x, tri):
    hi = x.astype(BF16)
    lo = (x - hi.astype(F32)).astype(BF16)
    return _dot(hi, tri) + _dot(lo, tri)


def _neg_softplus(z):
    return -(jnp.maximum(z, 0.0) + jnp.log(1.0 + jnp.exp(-jnp.abs(z))))


def sb_fwd(proj, *, q_col, name, rider=None):
    S = proj.shape[0]
    dh = LANES // 2
    W = SB_HEADS * dh
    npair = W // LANES
    T = min(SB_BLOCK, S)
    nblk = S // T
    scale = dh ** -0.5
    qb0 = q_col // LANES

    def body(q_ref, k_ref, v_ref, o_ref, l_ref, qs, ks, vs):
        r_i = lax.broadcasted_iota(jnp.int32, (T, T), 0)
        c_i = lax.broadcasted_iota(jnp.int32, (T, T), 1)
        tri = (r_i >= c_i).astype(BF16)
        vis = c_i < r_i
        lane = lax.broadcasted_iota(jnp.int32, (T, dh), 1)

        for hh in range(2):
            sl = slice(hh * dh, (hh + 1) * dh)
            qs[hh] = (q_ref[:, sl] * scale).astype(BF16)
            ks[hh] = k_ref[:, sl].astype(BF16)
            vs[hh] = v_ref[:, sl].astype(BF16)

        def step(qb, j0, diag, st):
            two = range(2)
            kb = [ks[hh, pl.ds(j0, T), :] for hh in two]
            vb = [vs[hh, pl.ds(j0, T), :] for hh in two]
            z = [_dot_nt(qb[hh], kb[hh]) for hh in two]
            lk = [_neg_softplus(z[hh]) for hh in two]
            if diag:
                lk = [jnp.where(vis, lk[hh], 0.0) for hh in two]
            C = [_split_dot(lk[hh], tri) for hh in two]
            A = [jnp.exp(z[hh] + C[hh] + st[2 * hh + 1]) for hh in two]
            if diag:
                A = [jnp.where(vis, A[hh], 0.0) for hh in two]
            acc = [st[2 * hh] + _dot(A[hh].astype(BF16), vb[hh]) for hh in two]
            return (acc[0], st[1] + C[0][:, 0:1], acc[1], st[3] + C[1][:, 0:1])

        def qblock(i, _):
            r0 = pl.multiple_of(i * T, T)
            qb = [qs[hh, pl.ds(r0, T), :] for hh in range(2)]
            zero = (jnp.zeros((T, dh), F32), jnp.zeros((T, 1), F32))
            state = step(qb, r0, True, zero + zero)

            def more(c):
                return (c[0] >= 0) & (jnp.max(jnp.maximum(c[2], c[4])) >= SB_STOP)

            def walk(c):
                return (c[0] - 1,) + step(qb, pl.multiple_of(c[0] * T, T), False, c[1:])

            c = lax.while_loop(more, walk, (i - 1,) + state)
            walked = (i - c[0]).astype(F32)
            for hh in range(2):
                sl = slice(hh * dh, (hh + 1) * dh)
                o_ref[pl.ds(r0, T), sl] = c[1 + 2 * hh]
                l_ref[pl.ds(r0, T), sl] = jnp.where(lane == 1, walked, c[2 + 2 * hh])
            return 0

        lax.fori_loop(0, nblk, qblock, 0)

    blk = lambda off: pl.BlockSpec((S, LANES), lambda h: (0, qb0 + off * npair + h))
    out = pl.BlockSpec((S, LANES), lambda h: (0, h))
    in_specs, out_specs, out_shape, scratch = _carry_specs(
        rider, [blk(0), blk(1), blk(2)], (out, out), (pltpu.HBM((S, W), F32), pltpu.HBM((S, W), F32)),
        [pltpu.VMEM((2, S, dh), BF16)] * 3)
    first = lambda: pl.program_id(0) == 0
    last = lambda: pl.program_id(0) == npair - 1
    res = _pallas(
        _carry(rider, body, 3, 2, first, last), name=name, grid=(npair,), out_shape=out_shape,
        in_specs=in_specs, out_specs=out_specs, scratch_shapes=scratch,
        compiler_params=_params(6 * _nbytes((S, LANES), F32), ("arbitrary",)))(
            proj, proj, proj, *(rider.arrays if rider else ()))
    return res[0], res[1], list(res[2:])


def sb_bwd(proj, ltot, dua, *, q_col, do_col, name, rider=None):
    S = proj.shape[0]
    dh = LANES // 2
    W = SB_HEADS * dh
    npair = W // LANES
    T = min(SB_BLOCK, S)
    nblk = S // T
    scale = dh ** -0.5
    qb0 = q_col // LANES
    db0 = do_col // LANES

    def body(q_ref, k_ref, v_ref, l_ref, do_ref, dq_ref, dk_ref, dv_ref, qs, ks, vs, dos, dks, dvs):
        r_i = lax.broadcasted_iota(jnp.int32, (T, T), 0)
        c_i = lax.broadcasted_iota(jnp.int32, (T, T), 1)
        tri_rev = (r_i >= c_i).astype(BF16)
        tri_fwd = (r_i <= c_i).astype(BF16)
        vis = c_i < r_i

        for hh in range(2):
            sl = slice(hh * dh, (hh + 1) * dh)
            qs[hh] = (q_ref[:, sl] * scale).astype(BF16)
            ks[hh] = k_ref[:, sl].astype(BF16)
            vs[hh] = v_ref[:, sl].astype(BF16)
            dos[hh] = do_ref[:, sl].astype(BF16)
        dks[...] = jnp.zeros_like(dks)
        dvs[...] = jnp.zeros_like(dvs)

        def step(qb, dob, Lt, j0, diag, st):
            two = range(2)
            kb = [ks[hh, pl.ds(j0, T), :] for hh in two]
            vb = [vs[hh, pl.ds(j0, T), :] for hh in two]
            z = [_dot_nt(qb[hh], kb[hh]) for hh in two]
            dA = [_dot_nt(dob[hh], vb[hh]) for hh in two]
            lk = [_neg_softplus(z[hh]) for hh in two]
            beta = [jnp.exp(z[hh] + lk[hh]) for hh in two]
            if diag:
                lk = [jnp.where(vis, lk[hh], 0.0) for hh in two]
            C = [_split_dot(lk[hh], tri_rev) for hh in two]
            rowsum = [C[hh][:, 0:1] for hh in two]
            A = [jnp.exp(z[hh] + C[hh] + (Lt[hh] - st[3 * hh + 1] - rowsum[hh])) for hh in two]
            if diag:
                A = [jnp.where(vis, A[hh], 0.0) for hh in two]
            g = [A[hh] * dA[hh] for hh in two]
            Gin = [_split_dot(g[hh], tri_fwd) for hh in two]
            dz = [g[hh] - beta[hh] * (st[3 * hh + 2] + Gin[hh]) for hh in two]
            if diag:
                dz = [jnp.where(vis, dz[hh], 0.0) for hh in two]
            dzb = [dz[hh].astype(BF16) for hh in two]
            out = ()
            for hh in two:
                dvs[hh, pl.ds(j0, T), :] += _dot_tn(A[hh].astype(BF16), dob[hh])
                dks[hh, pl.ds(j0, T), :] += _dot_tn(dzb[hh], qb[hh])
                out += (st[3 * hh] + _dot(dzb[hh], kb[hh]), st[3 * hh + 1] + rowsum[hh],
                        st[3 * hh + 2] + Gin[hh][:, T - 1:T])
            return out

        def qblock(i, _):
            r0 = pl.multiple_of(i * T, T)
            qb = [qs[hh, pl.ds(r0, T), :] for hh in range(2)]
            dob = [dos[hh, pl.ds(r0, T), :] for hh in range(2)]
            Lt = [l_ref[pl.ds(r0, T), hh * dh:hh * dh + 1] for hh in range(2)]
            walked = jnp.clip(jnp.max(l_ref[pl.ds(r0, 8), 1:2]).astype(jnp.int32), 1, i + 1)

            def inner(j, c):
                return step(qb, dob, Lt, pl.multiple_of(j * T, T), False, c)

            zero = jnp.zeros((T, 1), F32)
            init = (jnp.zeros((T, dh), F32), zero, zero)
            c = lax.fori_loop(i + 1 - walked, i, inner, init + init)
            c = step(qb, dob, Lt, r0, True, c)
            for hh in range(2):
                dq_ref[pl.ds(r0, T), hh * dh:(hh + 1) * dh] = (c[3 * hh] * scale).astype(BF16)
            return 0

        lax.fori_loop(0, nblk, qblock, 0)
        for hh in range(2):
            sl = slice(hh * dh, (hh + 1) * dh)
            dk_ref[:, sl] = dks[hh].astype(BF16)
            dv_ref[:, sl] = dvs[hh].astype(BF16)

    blk = lambda off: pl.BlockSpec((S, LANES), lambda h: (0, qb0 + off * npair + h))
    out = pl.BlockSpec((S, LANES), lambda h: (0, h))
    o_shape = pltpu.HBM((S, W), BF16)
    in_specs, out_specs, out_shape, scratch = _carry_specs(
        rider, [blk(0), blk(1), blk(2), out, pl.BlockSpec((S, LANES), lambda h: (0, db0 + h))], (out, out, out),
        (o_shape, o_shape, o_shape), [pltpu.VMEM((2, S, dh), BF16)] * 4 + [pltpu.VMEM((2, S, dh), F32)] * 2)
    first = lambda: pl.program_id(0) == 0
    last = lambda: pl.program_id(0) == npair - 1
    res = _pallas(
        _carry(rider, body, 5, 3, first, last), name=name, grid=(npair,), out_shape=out_shape,
        in_specs=in_specs, out_specs=out_specs, scratch_shapes=scratch,
        compiler_params=_params(12 * _nbytes((S, LANES), F32), ("arbitrary",)))(
            proj, proj, proj, ltot, dua, *(rider.arrays if rider else ()))
    return res[0], res[1], res[2], list(res[3:])


def xattn_fwd(q, k, v, *, tm, name):
    S, D = q.shape
    Mlen = k.shape[0]
    hd = D // MEM_HEADS
    scale = hd ** -0.5

    def body(q_ref, k_ref, v_ref, o_ref):
        for h in range(MEM_HEADS):
            sl = slice(h * hd, (h + 1) * hd)
            s = _dot_nt(q_ref[:, sl], k_ref[:, sl]) * scale
            e = jnp.exp(s - jnp.max(s, axis=-1, keepdims=True))
            p = e / jnp.sum(e, axis=-1, keepdims=True)
            o_ref[:, sl] = _dot(p.astype(BF16), v_ref[:, sl]).astype(BF16)

    row = lambda i: (i, 0)
    fix = lambda i: (0, 0)
    return _pallas(
        body, name=name, out_shape=pltpu.HBM((S, D), BF16), grid=(S // tm,),
        in_specs=[pl.BlockSpec((tm, D), row), pl.BlockSpec((Mlen, D), fix), pl.BlockSpec((Mlen, D), fix)],
        out_specs=pl.BlockSpec((tm, D), row),
        compiler_params=_params(4 * _nbytes((tm, D), F32), ("parallel",)))(q, k, v)


def xattn_bwd(q, do, k, v, *, tm, name):
    S, D = q.shape
    Mlen = k.shape[0]
    hd = D // MEM_HEADS
    scale = hd ** -0.5

    def body(q_ref, do_ref, k_ref, v_ref, dq_ref, dk_ref, dv_ref):
        @pl.when(pl.program_id(0) == 0)
        def _():
            dk_ref[...] = jnp.zeros_like(dk_ref)
            dv_ref[...] = jnp.zeros_like(dv_ref)

        for h in range(MEM_HEADS):
            sl = slice(h * hd, (h + 1) * hd)
            qh, doh, kh, vh = q_ref[:, sl], do_ref[:, sl], k_ref[:, sl], v_ref[:, sl]
            s = _dot_nt(qh, kh) * scale
            e = jnp.exp(s - jnp.max(s, axis=-1, keepdims=True))
            p = e / jnp.sum(e, axis=-1, keepdims=True)
            dp = _dot_nt(doh, vh)
            ds = (p * (dp - jnp.sum(p * dp, axis=-1, keepdims=True)) * scale).astype(BF16)
            dq_ref[:, sl] = _dot(ds, kh).astype(BF16)
            dk_ref[:, sl] += _dot_tn(ds, qh)
            dv_ref[:, sl] += _dot_tn(p.astype(BF16), doh)

    row = lambda i: (i, 0)
    fix = lambda i: (0, 0)
    return _pallas(
        body, name=name, grid=(S // tm,),
        out_shape=(pltpu.HBM((S, D), BF16), pltpu.HBM((Mlen, D), F32),
                   pltpu.HBM((Mlen, D), F32)),
        in_specs=[pl.BlockSpec((tm, D), row), pl.BlockSpec((tm, D), row), pl.BlockSpec((Mlen, D), fix),
                  pl.BlockSpec((Mlen, D), fix)],
        out_specs=(pl.BlockSpec((tm, D), row), pl.BlockSpec((Mlen, D), fix), pl.BlockSpec((Mlen, D), fix)),
        compiler_params=_params(6 * _nbytes((tm, D), F32), ("arbitrary",)))(q, do, k, v)


FFN_HALO = 8


def _conv3(ext, w, lo):
    tm = ext.shape[0] - FFN_HALO
    return (w[0:1, :] * ext[lo:lo + tm, :] + w[1:2, :] * ext[lo + 1:lo + 1 + tm, :]
            + w[2:3, :] * ext[lo + 2:lo + 2 + tm, :])


def ffn_up_fwd(xb, w_up, conv_w, conv_b, *, tm, tn, name, rider=None):
    S, D = xb.shape
    nsh, _, ns = w_up.shape
    F = nsh * ns // 2
    per = ns // tn
    ncol = F // tn
    KW = conv_w.shape[0]
    assert KW == 3

    def body(x_ref, wv_ref, wg_ref, cwv_ref, cwg_ref, cbv_ref, cbg_ref, uv_ref, ug_ref, h_ref, carry):
        @pl.when(pl.program_id(1) == 0)
        def _():
            carry[...] = jnp.zeros_like(carry)

        x = x_ref[...]
        uv = _dot(x, wv_ref[...])
        ug = _dot(x, wg_ref[...])
        uv_ref[...] = uv.astype(BF16)
        ug_ref[...] = ug.astype(BF16)
        lo = FFN_HALO - (KW - 1)
        cv = _conv3(jnp.concatenate([carry[0], uv], axis=0), cwv_ref[...], lo) + cbv_ref[...]
        cg = _conv3(jnp.concatenate([carry[1], ug], axis=0), cwg_ref[...], lo) + cbg_ref[...]
        carry[0] = uv[tm - FFN_HALO:, :]
        carry[1] = ug[tm - FFN_HALO:, :]
        h_ref[...] = (cg * _sigmoid(cg) * cv).astype(BF16)

    wspec = lambda half: pl.BlockSpec((None, D, tn), lambda j, i: (half * (nsh // 2) + j // per, 0, j % per))
    cspec = lambda rows, half: pl.BlockSpec((rows, tn), lambda j, i: (0, half * ncol + j))
    out = pl.BlockSpec((tm, tn), lambda j, i: (i, j))
    o_shape = pltpu.HBM((S, F), BF16)
    blk = _nbytes((tm, D), BF16) + 2 * _nbytes((D, tn), BF16) + 8 * _nbytes((tm, tn), F32)
    nrow = S // tm
    in_specs, out_specs, out_shape, scratch = _carry_specs(
        rider, [pl.BlockSpec((tm, D), lambda j, i: (i, 0)), wspec(0), wspec(1), cspec(KW, 0), cspec(KW, 1),
                cspec(1, 0), cspec(1, 1)], (out, out, out), (o_shape, o_shape, o_shape),
        [pltpu.VMEM((2, FFN_HALO, tn), F32)])
    first = lambda: (pl.program_id(0) == 0) & (pl.program_id(1) == 0)
    last = lambda: (pl.program_id(0) == ncol - 1) & (pl.program_id(1) == nrow - 1)
    res = _pallas(
        _carry(rider, body, 7, 3, first, last), name=name, grid=(ncol, nrow), out_shape=out_shape,
        in_specs=in_specs, out_specs=out_specs, scratch_shapes=scratch,
        compiler_params=_params(blk, ("arbitrary", "arbitrary")))(
            xb, w_up, w_up, conv_w, conv_w, conv_b, conv_b, *(rider.arrays if rider else ()))
    return res[0], res[1], res[2], list(res[3:])


def ffn_mid_bwd(dzb, w_down, up_v, up_g, conv_w, conv_b, *, tm, tn, name, rider=None):
    S, D = dzb.shape
    F = up_v.shape[1]
    ncol = F // tn
    nrow = S // tm
    KW = conv_w.shape[0]
    assert KW == 3
    hb = tm // FFN_HALO

    def body(dz_ref, wd_ref, uv_ref, ug_ref, hv_ref, hg_ref, cwv_ref, cwg_ref, cbv_ref, cbg_ref,
             dv_ref, dg_ref, dwv_ref, dwg_ref, dbv_ref, dbg_ref, carry):
        i = pl.program_id(1)

        @pl.when(i == 0)
        def _():
            carry[...] = jnp.zeros_like(carry)
            for r in (dwv_ref, dwg_ref, dbv_ref, dbg_ref):
                r[...] = jnp.zeros_like(r)

        first = i == nrow - 1
        halo_v = jnp.where(first, 0.0, hv_ref[...].astype(F32))
        halo_g = jnp.where(first, 0.0, hg_ref[...].astype(F32))
        ext_v = jnp.concatenate([halo_v, uv_ref[...].astype(F32)], axis=0)
        ext_g = jnp.concatenate([halo_g, ug_ref[...].astype(F32)], axis=0)
        cwv, cwg = cwv_ref[...], cwg_ref[...]
        lo = FFN_HALO - (KW - 1)
        cv = _conv3(ext_v, cwv, lo) + cbv_ref[...]
        cg = _conv3(ext_g, cwg, lo) + cbg_ref[...]
        dh = _dot_nt(dz_ref[...], wd_ref[...])
        sg = _sigmoid(cg)
        dcv = dh * (cg * sg)
        dcg = dh * cv * (sg * (1.0 + cg * (1.0 - sg)))

        def back(dc, ext, cw, slot, du_ref, dw_ref, db_ref):
            ext2 = jnp.concatenate([dc, carry[slot]], axis=0)
            du = cw[2:3, :] * ext2[0:tm, :] + cw[1:2, :] * ext2[1:tm + 1, :] + cw[0:1, :] * ext2[2:tm + 2, :]
            du_ref[...] = du.astype(BF16)
            carry[slot] = dc[0:FFN_HALO, :]
            for k in range(KW):
                dw_ref[k:k + 1, :] += jnp.sum(dc * ext[lo + k:lo + k + tm, :], axis=0, keepdims=True)
            db_ref[...] += jnp.sum(dc, axis=0, keepdims=True)

        back(dcv, ext_v, cwv, 0, dv_ref, dwv_ref, dbv_ref)
        back(dcg, ext_g, cwg, 1, dg_ref, dwg_ref, dbg_ref)

    rev = lambda i: nrow - 1 - i
    tile = pl.BlockSpec((tm, tn), lambda j, i: (rev(i), j))
    halo = pl.BlockSpec((FFN_HALO, tn), lambda j, i: (jnp.maximum(rev(i) * hb - 1, 0), j))
    cspec = lambda rows, half: pl.BlockSpec((rows, tn), lambda j, i: (0, half * ncol + j))
    acc = lambda rows: pl.BlockSpec((rows, tn), lambda j, i: (0, j))
    big = pltpu.HBM((S, F), BF16)
    blk = _nbytes((tm, D), BF16) + _nbytes((tn, D), BF16) + 10 * _nbytes((tm, tn), F32)
    in_specs, out_specs, out_shape, scratch = _carry_specs(
        rider, [pl.BlockSpec((tm, D), lambda j, i: (rev(i), 0)), pl.BlockSpec((tn, D), lambda j, i: (j, 0)),
                tile, tile, halo, halo, cspec(KW, 0), cspec(KW, 1), cspec(1, 0), cspec(1, 1)],
        (tile, tile, acc(KW), acc(KW), acc(1), acc(1)),
        (big, big, pltpu.HBM((KW, F), F32), pltpu.HBM((KW, F), F32), pltpu.HBM((1, F), F32),
         pltpu.HBM((1, F), F32)), [pltpu.VMEM((2, FFN_HALO, tn), F32)])
    first = lambda: (pl.program_id(0) == 0) & (pl.program_id(1) == 0)
    last = lambda: (pl.program_id(0) == ncol - 1) & (pl.program_id(1) == nrow - 1)
    res = _pallas(
        _carry(rider, body, 10, 6, first, last), name=name, grid=(ncol, nrow), out_shape=out_shape,
        in_specs=in_specs, out_specs=out_specs, scratch_shapes=scratch,
        compiler_params=_params(blk, ("arbitrary", "arbitrary")))(
            dzb, w_down, up_v, up_g, up_v, up_g, conv_w, conv_w, conv_b, conv_b,
            *(rider.arrays if rider else ()))
    return res[:6], list(res[6:])


def loss_head(y, target, *, tm, name):
    S, D = y.shape

    def body(y_ref, t_ref, dy_ref, l_ref):
        @pl.when(pl.program_id(0) == 0)
        def _():
            l_ref[...] = jnp.zeros_like(l_ref)

        e = y_ref[...] - t_ref[...]
        dy_ref[...] = e * (1.0 / D)
        l_ref[...] += 0.5 * jnp.sum(jnp.mean(e * e, axis=-1, keepdims=True), axis=0, keepdims=True)

    row = lambda i: (i, 0)
    return _pallas(
        body, name=name, grid=(S // tm,),
        out_shape=(pltpu.HBM((S, D), F32), pltpu.HBM((1, 1), F32)),
        in_specs=[pl.BlockSpec((tm, D), row), pl.BlockSpec((tm, D), row)],
        out_specs=(pl.BlockSpec((tm, D), row), pl.BlockSpec((1, 1), lambda i: (0, 0))),
        compiler_params=_params(3 * _nbytes((tm, D), F32), ("arbitrary",)))(y, target)


def adamw(w, g, m, v, *, tr, name):
    R, C = w.shape
    c1 = 1.0 - ADAM_B1 ** ADAM_STEP
    c2 = 1.0 - ADAM_B2 ** ADAM_STEP

    def body(w_ref, g_ref, m_ref, v_ref, go_ref, d_ref, mo_ref, vo_ref):
        gv = g_ref[...]
        mn = ADAM_B1 * m_ref[...] + (1.0 - ADAM_B1) * gv
        vn = ADAM_B2 * v_ref[...] + (1.0 - ADAM_B2) * (gv * gv)
        go_ref[...] = gv
        mo_ref[...] = mn
        vo_ref[...] = vn
        d_ref[...] = -ADAM_LR * ((mn / c1) / (jnp.sqrt(vn / c2) + ADAM_EPS) + ADAM_WD * w_ref[...])

    spec = pl.BlockSpec((tr, C), lambda i: (i, 0))
    shape = pltpu.HBM((R, C), F32)
    return _pallas(
        body, name=name, grid=(R // tr,), out_shape=(shape,) * 4, in_specs=[spec] * 4, out_specs=(spec,) * 4,
        compiler_params=_params(8 * _nbytes((tr, C), F32), ("parallel",)))(w, g, m, v)


def add_pair(g, got, core, *, name):
    n, _, R, C = g.shape

    def body(c_ref, a_ref, b_ref, o_ref):
        o_ref[...] = (a_ref[...].astype(F32) + b_ref[...].astype(F32)).astype(BF16)

    spec = pl.BlockSpec((None, R, C), lambda i, c: (i, 0, 0))
    grid_spec = pltpu.PrefetchScalarGridSpec(
        num_scalar_prefetch=1, grid=(n,),
        in_specs=[pl.BlockSpec((None, None, R, C), lambda i, c: (i, c[0], 0, 0)), spec], out_specs=spec)
    return _pallas(
        body, name=name, grid_spec=grid_spec, out_shape=pltpu.HBM(got.shape, BF16),
        compiler_params=_params(4 * _nbytes((R, C), F32), ("parallel",)))(core, g, got)


def sum_chips_into(b, dest, layer, core, *, name):
    n, R, C = b.shape
    tr = R // 2 if (R // 2) % 16 == 0 else R

    def body(c_ref, b_ref, d_ref, o_ref):
        acc = b_ref[0].astype(F32)
        for p in range(1, n):
            acc = acc + b_ref[p].astype(F32)
        o_ref[...] = acc

    grid_spec = pltpu.PrefetchScalarGridSpec(
        num_scalar_prefetch=1, grid=(R // tr,),
        in_specs=[pl.BlockSpec((n, tr, C), lambda i, c: (0, i, 0)), pl.BlockSpec(memory_space=pl.ANY)],
        out_specs=pl.BlockSpec((None, None, tr, C), lambda i, c: (layer, c[0], i, 0)))
    return _pallas(
        body, name=name, grid_spec=grid_spec, out_shape=pltpu.HBM(dest.shape, F32),
        input_output_aliases={2: 0},
        compiler_params=_params(8 * _nbytes((tr, C), F32), ("parallel",)))(core, b, dest)


_HBM = pl.BlockSpec(memory_space=pltpu.HBM)


def _place():
    x, y, c = lax.axis_index("x"), lax.axis_index("y"), lax.axis_index("c")
    chips = [(1 - x, y), (x, 1 - y), (1 - x, 1 - y)]
    return x, y, c, chips


class GatherRider:
    def __init__(self, shards):
        self.arrays = list(shards)
        self.n = n = len(shards)
        self.out_shape = tuple(pltpu.HBM((N_CHIPS,) + s.shape, s.dtype) for s in shards)
        self.scratch = [pltpu.SemaphoreType.DMA((n, 3))] * 4 + [pltpu.SemaphoreType.DMA((n,))]

    def _copies(self, ins, outs, sems):
        send_ici, recv_ici, send_d2d, recv_d2d, local = sems
        x, y, c, chips = _place()
        me = 2 * x + y

        def own(w):
            return pltpu.make_async_copy(ins[w], outs[w].at[me], local.at[w])

        def ici(w, j):
            px, py = chips[j]
            return pltpu.make_async_remote_copy(
                src_ref=ins[w].at[c], dst_ref=outs[w].at[me, c], send_sem=send_ici.at[w, j],
                recv_sem=recv_ici.at[w, j], device_id=(px, py, c), device_id_type=MESH)

        def landed(w, j, half):
            px, py = chips[j]
            return outs[w].at[2 * px + py, half]

        def d2d(w, j, half):
            return pltpu.make_async_remote_copy(
                src_ref=landed(w, j, half), dst_ref=landed(w, j, half), send_sem=send_d2d.at[w, j],
                recv_sem=recv_d2d.at[w, j], device_id=(x, y, 1 - c), device_id_type=MESH)

        def ici_arrival(w, j):
            return pltpu.make_async_remote_copy(
                src_ref=landed(w, j, c), dst_ref=landed(w, j, c), send_sem=send_ici.at[w, j],
                recv_sem=recv_ici.at[w, j], device_id=(x, y, c), device_id_type=MESH)

        return c, own, ici, d2d, ici_arrival

    def start(self, ins, outs, sems):
        c, own, ici, d2d, ici_arrival = self._copies(ins, outs, sems)
        for w in range(self.n):
            own(w).start()
            for j in range(3):
                ici(w, j).start()

    def finish(self, ins, outs, sems):
        c, own, ici, d2d, ici_arrival = self._copies(ins, outs, sems)
        for w in range(self.n):
            for j in range(3):
                ici_arrival(w, j).wait_recv()
                d2d(w, j, c).start()
        for w in range(self.n):
            for j in range(3):
                d2d(w, j, 1 - c).wait_recv()
        for w in range(self.n):
            for j in range(3):
                ici(w, j).wait_send()
                d2d(w, j, c).wait_send()
            own(w).wait()


class ScatterRider:
    def __init__(self, parts):
        self.arrays = list(parts)
        self.n = n = len(parts)
        self.out_shape = tuple(pltpu.HBM(p.shape, p.dtype) for p in parts)
        self.scratch = [pltpu.SemaphoreType.DMA((n, 3))] * 2 + [pltpu.SemaphoreType.DMA((n,))]

    def _copies(self, ins, outs, sems):
        send, recv, local = sems
        x, y, c, chips = _place()
        me = 2 * x + y

        def own(w):
            return pltpu.make_async_copy(ins[w].at[me], outs[w].at[me], local.at[w])

        def copy(w, j):
            px, py = chips[j]
            return pltpu.make_async_remote_copy(
                src_ref=ins[w].at[2 * px + py], dst_ref=outs[w].at[me], send_sem=send.at[w, j],
                recv_sem=recv.at[w, j], device_id=(px, py, c), device_id_type=MESH)

        def arrival(w, j):
            px, py = chips[j]
            blk = outs[w].at[2 * px + py]
            return pltpu.make_async_remote_copy(
                src_ref=blk, dst_ref=blk, send_sem=send.at[w, j], recv_sem=recv.at[w, j],
                device_id=(x, y, c), device_id_type=MESH)

        return own, copy, arrival

    def start(self, ins, outs, sems):
        own, copy, arrival = self._copies(ins, outs, sems)
        for w in range(self.n):
            own(w).start()
            for j in range(3):
                copy(w, j).start()

    def finish(self, ins, outs, sems):
        own, copy, arrival = self._copies(ins, outs, sems)
        for w in range(self.n):
            for j in range(3):
                arrival(w, j).wait_recv()
        for w in range(self.n):
            for j in range(3):
                copy(w, j).wait_send()
            own(w).wait()


def _carry(rider, body, n_in, n_out, first, last):
    if rider is None:
        return body
    k, m = rider.n, len(rider.scratch)

    def carried(*refs):
        ins, r_in = refs[:n_in], refs[n_in:n_in + k]
        outs, r_out = refs[n_in + k:n_in + k + n_out], refs[n_in + k + n_out:n_in + 2 * k + n_out]
        rest = refs[n_in + 2 * k + n_out:]
        scratch, sems = rest[:len(rest) - m], rest[len(rest) - m:]

        @pl.when(first())
        def _():
            rider.start(r_in, r_out, sems)

        body(*ins, *outs, *scratch)

        @pl.when(last())
        def _():
            rider.finish(r_in, r_out, sems)

    return carried


def _carry_specs(rider, in_specs, out_specs, out_shape, scratch):
    if rider is None:
        return list(in_specs), tuple(out_specs), tuple(out_shape), list(scratch)
    k = rider.n
    return (list(in_specs) + [_HBM] * k, tuple(out_specs) + (_HBM,) * k, tuple(out_shape) + rider.out_shape,
            list(scratch) + list(rider.scratch))


def run_rider(rider, *, name):
    k = rider.n

    def body(*refs):
        rider.start(refs[:k], refs[k:2 * k], refs[2 * k:])
        rider.finish(refs[:k], refs[k:2 * k], refs[2 * k:])

    return _pallas(body, name=name, out_shape=rider.out_shape, in_specs=[_HBM] * k, out_specs=(_HBM,) * k,
                   scratch_shapes=rider.scratch)(*rider.arrays)


def allgather_small(shards, *, name):
    n = len(shards)

    def body(*refs):
        ins, outs = refs[:n], refs[n:2 * n]
        send, recv, local = refs[2 * n:]
        x, y, c, chips = _place()
        me = 2 * x + y
        locals_ = [pltpu.make_async_copy(ins[w], outs[w].at[me], local.at[w]) for w in range(n)]
        for cp in locals_:
            cp.start()

        def copy(w, j):
            px, py = chips[j]
            return pltpu.make_async_remote_copy(
                src_ref=ins[w], dst_ref=outs[w].at[me], send_sem=send.at[w, j], recv_sem=recv.at[w, j],
                device_id=(px, py, c), device_id_type=MESH)

        def arrival(w, j):
            px, py = chips[j]
            blk = outs[w].at[2 * px + py]
            return pltpu.make_async_remote_copy(
                src_ref=blk, dst_ref=blk, send_sem=send.at[w, j], recv_sem=recv.at[w, j],
                device_id=(x, y, c), device_id_type=MESH)

        for w in range(n):
            for j in range(3):
                copy(w, j).start()
        for w in range(n):
            for j in range(3):
                arrival(w, j).wait_recv()
        for w in range(n):
            for j in range(3):
                copy(w, j).wait_send()
        for cp in locals_:
            cp.wait()

    out_shape = tuple(pltpu.HBM((N_CHIPS,) + s.shape, s.dtype) for s in shards)
    return _pallas(
        body, name=name, out_shape=out_shape, in_specs=[_HBM] * n, out_specs=(_HBM,) * n,
        scratch_shapes=[pltpu.SemaphoreType.DMA((n, 3))] * 2 + [pltpu.SemaphoreType.DMA((n,))],
    )(*shards)


def rs_sibling_swap(grads, *, name):
    n = len(grads)

    def body(*refs):
        ins, gots = refs[:n], refs[n:2 * n]
        send, recv = refs[2 * n:]
        x, y, c, _ = _place()
        swaps = [pltpu.make_async_remote_copy(
            src_ref=ins[w].at[:, 1 - c], dst_ref=gots[w], send_sem=send.at[w], recv_sem=recv.at[w],
            device_id=(x, y, 1 - c), device_id_type=MESH) for w in range(n)]
        for cp in swaps:
            cp.start()
        for cp in swaps:
            cp.wait_recv()
        for cp in swaps:
            cp.wait_send()

    half = tuple(pltpu.HBM((N_CHIPS,) + g.shape[2:], g.dtype) for g in grads)
    return _pallas(
        body, name=name, out_shape=half, in_specs=[_HBM] * n, out_specs=(_HBM,) * n,
        scratch_shapes=[pltpu.SemaphoreType.DMA((n,))] * 2,
    )(*grads)


def rs_sibling_share(stacked, *, name):
    n = len(stacked)

    def body(*refs):
        bufs = refs[n:2 * n]
        send, recv = refs[2 * n:]
        x, y, c, _ = _place()
        shares, arrivals = [], []
        for w in range(n):
            mine, other = bufs[w].at[:, c], bufs[w].at[:, 1 - c]
            shares.append(pltpu.make_async_remote_copy(
                src_ref=mine, dst_ref=mine, send_sem=send.at[w], recv_sem=recv.at[w],
                device_id=(x, y, 1 - c), device_id_type=MESH))
            arrivals.append(pltpu.make_async_remote_copy(
                src_ref=other, dst_ref=other, send_sem=send.at[w], recv_sem=recv.at[w],
                device_id=(x, y, c), device_id_type=MESH))
        for cp in shares:
            cp.start()
        for cp in arrivals:
            cp.wait_recv()
        for cp in shares:
            cp.wait_send()

    out_shape = tuple(pltpu.HBM(s.shape, F32) for s in stacked)
    return _pallas(
        body, name=name, out_shape=out_shape, in_specs=[_HBM] * n, out_specs=(_HBM,) * n,
        input_output_aliases={w: w for w in range(n)},
        scratch_shapes=[pltpu.SemaphoreType.DMA((n,))] * 2,
    )(*stacked)


def allreduce_small(v, *, name):
    R, C = v.shape

    def body(v_ref, o_ref, land, send, recv):
        x, y, c, _ = _place()
        me = 4 * x + 2 * y + c
        land[me] = v_ref[...]

        def flip(k):
            return (1 - x) if k & 4 else x, (1 - y) if k & 2 else y, (1 - c) if k & 1 else c

        copies = []
        for k in range(1, N_DEV):
            px, py, pc = flip(k)
            copies.append(pltpu.make_async_remote_copy(
                src_ref=v_ref, dst_ref=land.at[me], send_sem=send.at[k - 1], recv_sem=recv.at[k - 1],
                device_id=(px, py, pc), device_id_type=MESH))
        for cp in copies:
            cp.start()
        for k in range(1, N_DEV):
            px, py, pc = flip(k)
            blk = land.at[4 * px + 2 * py + pc]
            pltpu.make_async_remote_copy(
                src_ref=blk, dst_ref=blk, send_sem=send.at[k - 1], recv_sem=recv.at[k - 1],
                device_id=(x, y, c), device_id_type=MESH).wait_recv()
        for cp in copies:
            cp.wait_send()
        acc = land[0]
        for d in range(1, N_DEV):
            acc = acc + land[d]
        o_ref[...] = acc

    vm = pl.BlockSpec(memory_space=pltpu.VMEM)
    return pl.pallas_call(
        body, name=name, out_shape=jax.ShapeDtypeStruct((R, C), F32), in_specs=[vm], out_specs=vm,
        scratch_shapes=[pltpu.VMEM((N_DEV, R, C), F32), pltpu.SemaphoreType.DMA((N_DEV - 1,)),
                        pltpu.SemaphoreType.DMA((N_DEV - 1,))],
        compiler_params=pltpu.CompilerParams(vmem_limit_bytes=int(min(12 * R * C * 4 + (8 << 20), VMEM_CAP))),
    )(v)


def _pack(arrays):
    flat = jnp.concatenate([a.reshape(-1) for a in arrays])
    return flat.reshape(-1, LANES)


def _unpack(packed, shapes):
    flat = packed.reshape(-1)
    out, off = [], 0
    for s in shapes:
        n = 1
        for d in s:
            n *= d
        out.append(flat[off:off + n].reshape(s))
        off += n
    return out


def _row_tile(rows, cap=512):
    t = 1 << (cap.bit_length() - 1)
    while rows % t:
        t //= 2
    return t


def _adamw_tile(rows, cols):
    return _row_tile(rows, max(8, (1 << 20) // (4 * cols)))


def kernel(x, mem, w_in, conv_w, conv_b, conv_ln_g, conv_ln_b, w_out, ln1_g, ln1_b, mem_wq, mem_wk, mem_wv, mem_wo, ln2_g, ln2_b, ffn_up, ffn_conv_w, ffn_conv_b, ffn_down, ln3_g, ln3_b, loss_target, m_w_in, m_conv_w, m_conv_b, m_conv_ln_g, m_conv_ln_b, m_w_out, m_ln1_g, m_ln1_b, m_mem_wq, m_mem_wk, m_mem_wv, m_mem_wo, m_ln2_g, m_ln2_b, m_ffn_up, m_ffn_conv_w, m_ffn_conv_b, m_ffn_down, m_ln3_g, m_ln3_b, v_w_in, v_conv_w, v_conv_b, v_conv_ln_g, v_conv_ln_b, v_w_out, v_ln1_g, v_ln1_b, v_mem_wq, v_mem_wk, v_mem_wv, v_mem_wo, v_ln2_g, v_ln2_b, v_ffn_up, v_ffn_conv_w, v_ffn_conv_b, v_ffn_down, v_ln3_g, v_ln3_b):
    W = dict(w_in=w_in, conv_w=conv_w, conv_b=conv_b, conv_ln_g=conv_ln_g, conv_ln_b=conv_ln_b, w_out=w_out,
             ln1_g=ln1_g, ln1_b=ln1_b, mem_wq=mem_wq, mem_wk=mem_wk, mem_wv=mem_wv, mem_wo=mem_wo, ln2_g=ln2_g,
             ln2_b=ln2_b, ffn_up=ffn_up, ffn_conv_w=ffn_conv_w, ffn_conv_b=ffn_conv_b, ffn_down=ffn_down,
             ln3_g=ln3_g, ln3_b=ln3_b)
    M1 = dict(w_in=m_w_in, conv_w=m_conv_w, conv_b=m_conv_b, conv_ln_g=m_conv_ln_g, conv_ln_b=m_conv_ln_b,
              w_out=m_w_out, ln1_g=m_ln1_g, ln1_b=m_ln1_b, mem_wq=m_mem_wq, mem_wk=m_mem_wk, mem_wv=m_mem_wv,
              mem_wo=m_mem_wo, ln2_g=m_ln2_g, ln2_b=m_ln2_b, ffn_up=m_ffn_up, ffn_conv_w=m_ffn_conv_w,
              ffn_conv_b=m_ffn_conv_b, ffn_down=m_ffn_down, ln3_g=m_ln3_g, ln3_b=m_ln3_b)
    V2 = dict(w_in=v_w_in, conv_w=v_conv_w, conv_b=v_conv_b, conv_ln_g=v_conv_ln_g, conv_ln_b=v_conv_ln_b,
              w_out=v_w_out, ln1_g=v_ln1_g, ln1_b=v_ln1_b, mem_wq=v_mem_wq, mem_wk=v_mem_wk, mem_wv=v_mem_wv,
              mem_wo=v_mem_wo, ln2_g=v_ln2_g, ln2_b=v_ln2_b, ffn_up=v_ffn_up, ffn_conv_w=v_ffn_conv_w,
              ffn_conv_b=v_ffn_conv_b, ffn_down=v_ffn_down, ln3_g=v_ln3_g, ln3_b=v_ln3_b)

    L = w_in.shape[0]
    S, D = x.shape[1], x.shape[2]
    C = conv_b.shape[1]
    alpha = (2.0 * L) ** 0.25
    chip = 2 * lax.axis_index("x") + lax.axis_index("y")
    xs, mems, tgt = x[0], mem[0], loss_target[0]
    mem_bf = mems.astype(BF16)
    tm = _row_tile(S)
    tm_ffn = _row_tile(S, 256)

    def shards_of(l, names):
        out = []
        for n in names:
            wl = W[n][l].astype(BF16)
            out.append(wl.reshape(2, wl.shape[0] // 2, wl.shape[1]))
        return out

    def gathered(names, got):
        layer = {}
        for n, g in zip(names, got):
            rows, cols = W[n].shape[1], W[n].shape[2]
            layer[n] = g.reshape(N_CHIPS, rows, cols) if n in COL_SHARDED else g.reshape(N_CHIPS * rows, cols)
        return layer

    full = [gathered(BIG, run_rider(GatherRider(shards_of(0, BIG)), name="allgather_layer"))]
    cw_all, fcw_all = allgather_small([conv_w, ffn_conv_w], name="allgather_small")
    cw_full = jnp.transpose(cw_all, (1, 2, 0, 3)).reshape(L, conv_w.shape[1], -1)
    fcw_full = jnp.transpose(fcw_all, (1, 2, 0, 3)).reshape(L, ffn_conv_w.shape[1], -1)

    saved = []
    h, hb = xs, xs.astype(BF16)
    for l in range(L):
        fw = full[l]
        s = dict(x=h, xb=hb)
        s['proj'] = mm_nn(hb, fw['w_in'], F32, tm=min(1024, S), tn=fw['w_in'].shape[2], name="proj")
        s['u1'] = conv_fwd(s['proj'], cw_full[l], conv_b[l][None], name="conv_fwd")
        more = l + 1 < L
        s['o_sb'], s['ltot'], got_a = sb_fwd(
            s['proj'], q_col=2 * C, name="sb_fwd",
            rider=GatherRider(shards_of(l + 1, RIDE_A)) if more else None)
        s['ua'] = ln_silu(s['u1'], s['o_sb'], conv_ln_g[l][None], conv_ln_b[l][None], tm=tm, name="ln_silu")
        s['x1'], s['x1b'], s['zh1'], s['rs1'] = mm_ln(
            s['ua'], fw['w_out'], h, ln1_g[l][None], ln1_b[l][None], alpha, tm=tm, name="out_proj_ln")
        s['q2'] = mm_nn(s['x1b'], fw['mem_wq'], BF16, tm=min(1024, S), tn=512, name="mem_q")
        s['k2'] = mm_nn(mem_bf, fw['mem_wk'], BF16, tm=mem_bf.shape[0], tn=512, name="mem_kv")
        s['v2'] = mm_nn(mem_bf, fw['mem_wv'], BF16, tm=mem_bf.shape[0], tn=512, name="mem_kv")
        s['o2'] = xattn_fwd(s['q2'], s['k2'], s['v2'], tm=tm, name="xattn_fwd")
        s['x2'], s['x2b'], s['zh2'], s['rs2'] = mm_ln(
            s['o2'], fw['mem_wo'], s['x1'], ln2_g[l][None], ln2_b[l][None], alpha, tm=tm, name="mem_o_ln")
        s['upv'], s['upg'], s['hmid'], got_b = ffn_up_fwd(
            s['x2b'], fw['ffn_up'], fcw_full[l], ffn_conv_b[l][None], tm=tm_ffn, tn=fw['ffn_up'].shape[2],
            name="ffn_up_fwd", rider=GatherRider(shards_of(l + 1, RIDE_B)) if more else None)
        if more:
            full.append({**gathered(RIDE_A, got_a), **gathered(RIDE_B, got_b)})
        h, hb, s['zh3'], s['rs3'] = mm_ln(
            s['hmid'], fw['ffn_down'], s['x2'], ln3_g[l][None], ln3_b[l][None], alpha, tm=tm, name="ffn_down_ln")
        saved.append(s)

    dx, loss_part = loss_head(h, tgt, tm=tm, name="loss_head")
    loss = lax.psum(loss_part[0, 0], ("x", "y", "c"))

    core = lax.axis_index("c").astype(jnp.int32).reshape(1)
    reduced_big = {n: lax.empty((L, 2, W[n].shape[1] // 2, W[n].shape[2]), F32) for n in BIG}
    small_grads = [None] * L
    pending = None
    for l in reversed(range(L)):
        fw, s = full[l], saved[l]
        g = {}
        dz3, dz3b, g['ln3_g'], g['ln3_b'] = ln_bwd(dx, s['zh3'], s['rs3'], ln3_g[l][None], tm=tm, name="ln_bwd")
        ftn = fw['ffn_up'].shape[2]
        (dupv, dupg, dfw_v, dfw_g, dfb_v, dfb_g), sc_a = ffn_mid_bwd(
            dz3b, fw['ffn_down'], s['upv'], s['upg'], fcw_full[l], ffn_conv_b[l][None], tm=tm_ffn, tn=ftn,
            name="ffn_mid_bwd", rider=ScatterRider([pending[n] for n in RIDE_A]) if pending else None)
        g['ffn_conv_w'] = jnp.concatenate([dfw_v, dfw_g], axis=1)
        g['ffn_conv_b'] = jnp.concatenate([dfb_v, dfb_g], axis=1)[0]
        g['ffn_down'] = mm_tn(s['hmid'], [dz3b], tk=ftn, tn=512, tmc=min(1024, S), name="grad_ffn_down")
        dx2 = mm_nt([dupv, dupg], fw['ffn_up'], F32, tm=tm, tk=512, res=dz3, alpha=alpha, name="ffn_up_bwd")
        g['ffn_up'] = mm_tn(s['x2b'], [dupv, dupg], tk=512, tn=ftn, shard_width=ftn, tmc=min(1024, S),
                            name="grad_ffn_up")

        dz2, dz2b, g['ln2_g'], g['ln2_b'] = ln_bwd(dx2, s['zh2'], s['rs2'], ln2_g[l][None], tm=tm, name="ln_bwd")
        do2 = mm_nt([dz2b], fw['mem_wo'], BF16, tm=tm, tk=512, name="mem_o_bwd")
        g['mem_wo'] = mm_tn(s['o2'], [dz2b], tk=512, tn=512, name="grad_sq")
        dq2, dk2, dv2 = xattn_bwd(s['q2'], do2, s['k2'], s['v2'], tm=tm, name="xattn_bwd")
        dx1 = mm_nt([dq2], fw['mem_wq'], F32, tm=tm, tk=512, res=dz2, alpha=alpha, name="mem_q_bwd")
        g['mem_wq'] = mm_tn(s['x1b'], [dq2], tk=512, tn=512, name="grad_sq")
        g['mem_wk'] = mm_tn(mem_bf, [dk2], tk=512, tn=512, name="grad_mem_kv")
        g['mem_wv'] = mm_tn(mem_bf, [dv2], tk=512, tn=512, name="grad_mem_kv")

        dz1, dz1b, g['ln1_g'], g['ln1_b'] = ln_bwd(dx1, s['zh1'], s['rs1'], ln1_g[l][None], tm=tm, name="ln_bwd")
        dua = mm_nt([dz1b], fw['w_out'], F32, tm=tm, tk=512, name="out_proj_bwd")
        g['w_out'] = mm_tn(s['ua'], [dz1b], tk=512, tn=512, name="grad_sq")
        dq, dk, dv, sc_b = sb_bwd(
            s['proj'], s['ltot'], dua, q_col=2 * C, do_col=C, name="sb_bwd",
            rider=ScatterRider([pending[n] for n in RIDE_B]) if pending else None)
        if pending:
            for n, b in zip(RIDE_A + RIDE_B, sc_a + sc_b):
                reduced_big[n] = sum_chips_into(b, reduced_big[n], l + 1, core, name="rs_sum_chips")
        du1, g['conv_ln_g'], g['conv_ln_b'] = ln_silu_bwd(
            dua, s['u1'], conv_ln_g[l][None], conv_ln_b[l][None], tm=tm, name="ln_silu_bwd")
        da, dg, g['conv_w'], dcb = conv_bwd(du1, s['proj'], cw_full[l], name="conv_bwd")
        g['conv_b'] = dcb
        dproj = jnp.concatenate([da, dg, dq, dk, dv], axis=1)
        ns_in = fw['w_in'].shape[2]
        dx = mm_nt([dproj], fw['w_in'], F32, tm=tm, tk=512, res=dz1, alpha=alpha, name="proj_bwd")
        g['w_in'] = mm_tn(s['xb'], [dproj], tk=512, tn=ns_in, shard_width=ns_in, name="grad_w_in")

        parts = []
        for n in BIG:
            rows, cols = W[n].shape[1], W[n].shape[2]
            parts.append(g[n].reshape(N_CHIPS, 2, rows // 2, cols))
        got = rs_sibling_swap(parts, name="rs_sibling_swap")
        pending = {n: add_pair(a, b, core, name="rs_add_pair") for n, a, b in zip(BIG, parts, got)}
        small_grads[l] = {n: g[n].reshape(W[n].shape[1:-1] + (-1,)) for n in SMALL}

    grad_x = dx[None]

    scattered = run_rider(ScatterRider([pending[n] for n in BIG]), name="rs_chip_scatter")
    for n, b in zip(BIG, scattered):
        reduced_big[n] = sum_chips_into(b, reduced_big[n], 0, core, name="rs_sum_chips")
    shared = rs_sibling_share([reduced_big[n] for n in BIG], name="rs_sibling_share")
    G = {}
    for n, sh in zip(BIG, shared):
        G[n] = sh.reshape(W[n].shape)

    small_full_shapes = []
    small_stack = []
    for n in SMALL:
        st = jnp.stack([small_grads[l][n] for l in range(L)])
        small_stack.append(st)
        small_full_shapes.append(st.shape)
    reduced = _unpack(allreduce_small(_pack(small_stack), name="allreduce_small"), small_full_shapes)
    for n, r in zip(SMALL, reduced):
        if n in SMALL_SHARDED:
            width = W[n].shape[-1]
            r = lax.dynamic_slice_in_dim(r, chip * width, width, axis=2)
        G[n] = r

    out_g, out_d, out_m, out_v = {}, {}, {}, {}
    for n in BIG:
        shp = W[n].shape
        flat = lambda a: a.reshape(shp[0] * shp[1], shp[2])
        res = adamw(flat(W[n]), flat(G[n]), flat(M1[n]), flat(V2[n]), tr=_adamw_tile(shp[0] * shp[1], shp[2]), name="adamw")
        out_g[n], out_d[n], out_m[n], out_v[n] = [r.reshape(shp) for r in res]
    small_shapes = [W[n].shape for n in SMALL]
    packed = [_pack([d[n] for n in SMALL]) for d in (W, G, M1, V2)]
    res = adamw(*packed, tr=packed[0].shape[0], name="adamw_small")
    for d, r in zip((out_g, out_d, out_m, out_v), res):
        for n, a in zip(SMALL, _unpack(r, small_shapes)):
            d[n] = a

    return (loss, grad_x, *[out_g[n] for n in WEIGHTS], *[out_d[n] for n in WEIGHTS],
            *[out_m[n] for n in WEIGHTS], *[out_v[n] for n in WEIGHTS])
```

```python
import functools

import jax
import jax.numpy as jnp
from jax import lax
from jax.experimental import pallas as pl
from jax.experimental.pallas import tpu as pltpu

F32 = jnp.float32
BF16 = jnp.bfloat16
MESH = pl.DeviceIdType.MESH

LN_EPS = 1e-5
SB_HEADS = 8
MEM_HEADS = 4
ADAM_LR, ADAM_B1, ADAM_B2, ADAM_EPS, ADAM_WD, ADAM_STEP = 0.001, 0.9, 0.999, 1e-08, 0.01, 10

LANES = 128
V7X_VMEM_BYTES = 64 << 20
VMEM_CAP = V7X_VMEM_BYTES - (6 << 20)
N_CHIPS = 4
N_DEV = 8

BIG = ('w_in', 'w_out', 'mem_wq', 'mem_wk', 'mem_wv', 'mem_wo', 'ffn_up', 'ffn_down')
RIDE_IN = ('w_in',)
RIDE_ATT = ('w_out', 'mem_wq', 'mem_wk', 'mem_wv', 'mem_wo')
RIDE_FFN = ('ffn_up', 'ffn_down')
RIDE_MIX = ('w_in', 'w_out')
RIDE_REST = ('mem_wq', 'mem_wk', 'mem_wv', 'mem_wo', 'ffn_up', 'ffn_down')
COL_SHARDED = ('w_in', 'ffn_up')
SMALL = ('conv_w', 'conv_b', 'conv_ln_g', 'conv_ln_b', 'ln1_g', 'ln1_b', 'ln2_g', 'ln2_b',
         'ffn_conv_w', 'ffn_conv_b', 'ln3_g', 'ln3_b')
SMALL_SHARDED = ('conv_w', 'ffn_conv_w')
WEIGHTS = ('w_in', 'conv_w', 'conv_b', 'conv_ln_g', 'conv_ln_b', 'w_out', 'ln1_g', 'ln1_b',
           'mem_wq', 'mem_wk', 'mem_wv', 'mem_wo', 'ln2_g', 'ln2_b', 'ffn_up', 'ffn_conv_w',
           'ffn_conv_b', 'ffn_down', 'ln3_g', 'ln3_b')


def _params(block_bytes, semantics=None, **kw):
    limit = int(min(max(2 * block_bytes + (8 << 20), 32 << 20), VMEM_CAP))
    return pltpu.CompilerParams(dimension_semantics=semantics, vmem_limit_bytes=limit, **kw)


def _pallas(body, **kw):
    call = pl.pallas_call(body, **kw)

    def run(*args):
        return call(*[pltpu.with_memory_space_constraint(a, pltpu.HBM)
                      if jnp.issubdtype(a.dtype, jnp.floating) else a for a in args])

    return run


def _nbytes(shape, dtype):
    n = 1
    for s in shape:
        n *= s
    return n * jnp.dtype(dtype).itemsize


def _dot(a, b):
    return jnp.dot(a, b, preferred_element_type=F32)


def _dot_nt(a, b):
    return lax.dot_general(a, b, (((1,), (1,)), ((), ())), preferred_element_type=F32)


def _dot_tn(a, b):
    return lax.dot_general(a, b, (((0,), (0,)), ((), ())), preferred_element_type=F32)


def _sigmoid(x):
    return 1.0 / (1.0 + jnp.exp(-x))


def mm_nn(a, b, out_dtype, *, tm, tn, name):
    M, K = a.shape
    sharded = b.ndim == 3
    if sharded:
        nsh, _, ns = b.shape
        N, per = nsh * ns, ns // tn
        b_spec = pl.BlockSpec((None, K, tn), lambda i, j: (j // per, 0, j % per))
    else:
        N = b.shape[1]
        b_spec = pl.BlockSpec((K, tn), lambda i, j: (0, j))

    def body(a_ref, b_ref, o_ref):
        o_ref[...] = _dot(a_ref[...].astype(BF16), b_ref[...]).astype(o_ref.dtype)

    blk = _nbytes((tm, K), a.dtype) + _nbytes((K, tn), BF16) + _nbytes((tm, tn), out_dtype)
    return _pallas(
        body, name=name, out_shape=pltpu.HBM((M, N), out_dtype), grid=(M // tm, N // tn),
        in_specs=[pl.BlockSpec((tm, K), lambda i, j: (i, 0)), b_spec],
        out_specs=pl.BlockSpec((tm, tn), lambda i, j: (i, j)),
        compiler_params=_params(blk, ("parallel", "parallel")))(a, b)


def mm_ln(a, b, x, gamma, beta, alpha, *, tm, name):
    M, K = a.shape
    D = b.shape[1]

    def body(a_ref, b_ref, x_ref, g_ref, be_ref, y_ref, yb_ref, zh_ref, rs_ref):
        z = alpha * x_ref[...] + _dot(a_ref[...], b_ref[...])
        mu = jnp.mean(z, axis=-1, keepdims=True)
        zc = z - mu
        rstd = lax.rsqrt(jnp.mean(zc * zc, axis=-1, keepdims=True) + LN_EPS)
        zh = zc * rstd
        y = zh * g_ref[...] + be_ref[...]
        y_ref[...] = y
        yb_ref[...] = y.astype(BF16)
        zh_ref[...] = zh
        rs_ref[...] = rstd

    row = lambda i: (i, 0)
    fix = lambda i: (0, 0)
    blk = _nbytes((tm, K), BF16) + _nbytes((K, D), BF16) + 4 * _nbytes((tm, D), F32)
    return _pallas(
        body, name=name, grid=(M // tm,),
        out_shape=(pltpu.HBM((M, D), F32), pltpu.HBM((M, D), BF16),
                   pltpu.HBM((M, D), F32), pltpu.HBM((M, 1), F32)),
        in_specs=[pl.BlockSpec((tm, K), row), pl.BlockSpec((K, D), fix), pl.BlockSpec((tm, D), row),
                  pl.BlockSpec((1, D), fix), pl.BlockSpec((1, D), fix)],
        out_specs=(pl.BlockSpec((tm, D), row), pl.BlockSpec((tm, D), row), pl.BlockSpec((tm, D), row),
                   pl.BlockSpec((tm, 1), row)),
        compiler_params=_params(blk, ("parallel",)))(a, b, x, gamma, beta)


def ln_bwd(dy, zh, rstd, gamma, *, tm, name):
    M, D = dy.shape

    def body(dy_ref, zh_ref, rs_ref, g_ref, dz_ref, dzb_ref, dg_ref, db_ref):
        @pl.when(pl.program_id(0) == 0)
        def _():
            dg_ref[...] = jnp.zeros_like(dg_ref)
            db_ref[...] = jnp.zeros_like(db_ref)

        dyv, zhv = dy_ref[...], zh_ref[...]
        dg_ref[...] += jnp.sum(dyv * zhv, axis=0, keepdims=True)
        db_ref[...] += jnp.sum(dyv, axis=0, keepdims=True)
        dzh = dyv * g_ref[...]
        m1 = jnp.mean(dzh, axis=-1, keepdims=True)
        m2 = jnp.mean(dzh * zhv, axis=-1, keepdims=True)
        dz = rs_ref[...] * (dzh - m1 - zhv * m2)
        dz_ref[...] = dz
        dzb_ref[...] = dz.astype(BF16)

    row = lambda i: (i, 0)
    fix = lambda i: (0, 0)
    return _pallas(
        body, name=name, grid=(M // tm,),
        out_shape=(pltpu.HBM((M, D), F32), pltpu.HBM((M, D), BF16),
                   pltpu.HBM((1, D), F32), pltpu.HBM((1, D), F32)),
        in_specs=[pl.BlockSpec((tm, D), row), pl.BlockSpec((tm, D), row), pl.BlockSpec((tm, 1), row),
                  pl.BlockSpec((1, D), fix)],
        out_specs=(pl.BlockSpec((tm, D), row), pl.BlockSpec((tm, D), row), pl.BlockSpec((1, D), fix),
                   pl.BlockSpec((1, D), fix)),
        compiler_params=_params(4 * _nbytes((tm, D), F32), ("arbitrary",)))(dy, zh, rstd, gamma)


def mm_nt(a_list, b, out_dtype, *, tm, tk, name, res=None, alpha=None):
    M = a_list[0].shape[0]
    widths = [a.shape[1] for a in a_list]
    sharded = b.ndim == 3
    if sharded:
        nsh, K, ns = b.shape
        b_spec = pl.BlockSpec((nsh, tk, ns), lambda i, j: (0, j, 0))
        for w in widths:
            assert w % ns == 0
    else:
        K, N = b.shape
        ns = None
        b_spec = pl.BlockSpec((tk, N), lambda i, j: (j, 0))
    n_a = len(a_list)

    def body(*refs):
        a_refs, b_ref = refs[:n_a], refs[n_a]
        o_ref = refs[-1]
        acc = None
        off = 0
        for a_ref, w in zip(a_refs, widths):
            if sharded:
                for p in range(w // ns):
                    t = _dot_nt(a_ref[:, p * ns:(p + 1) * ns].astype(BF16), b_ref[off // ns + p])
                    acc = t if acc is None else acc + t
            else:
                t = _dot_nt(a_ref[...].astype(BF16), b_ref[:, off:off + w])
                acc = t if acc is None else acc + t
            off += w
        if res is not None:
            acc = acc + alpha * refs[n_a + 1][...]
        o_ref[...] = acc.astype(o_ref.dtype)

    in_specs = [pl.BlockSpec((tm, w), lambda i, j: (i, 0)) for w in widths] + [b_spec]
    args = list(a_list) + [b]
    if res is not None:
        in_specs.append(pl.BlockSpec((tm, tk), lambda i, j: (i, j)))
        args.append(res)
    blk = (sum(_nbytes((tm, w), a.dtype) for a, w in zip(a_list, widths)) + _nbytes((tk, sum(widths)), BF16)
           + 2 * _nbytes((tm, tk), F32))
    return _pallas(
        body, name=name, out_shape=pltpu.HBM((M, K), out_dtype), grid=(M // tm, K // tk),
        in_specs=in_specs, out_specs=pl.BlockSpec((tm, tk), lambda i, j: (i, j)),
        compiler_params=_params(blk, ("parallel", "parallel")))(*args)


def mm_tn(a, b_list, *, tk, tn, name, shard_width=None, tmc=None):
    M, K = a.shape
    tmc = M if tmc is None else tmc
    nm = M // tmc
    widths = [b.shape[1] for b in b_list]
    N = sum(widths)
    starts, s = [], 0
    for w in widths:
        assert w % tn == 0
        starts.append(s)
        s += w // tn
    n_b = len(b_list)

    def body(*refs):
        a_ref, b_refs, o_ref, acc = refs[0], refs[1:1 + n_b], refs[-2], refs[-1]
        j, m = pl.program_id(1), pl.program_id(2)
        for b_ref, st, w in zip(b_refs, starts, widths):
            @pl.when((j >= st) & (j < st + w // tn))
            def _(b_ref=b_ref):
                t = _dot_tn(a_ref[...].astype(BF16), b_ref[...].astype(BF16))
                if nm == 1:
                    o_ref[...] = t.astype(o_ref.dtype)
                else:
                    @pl.when(m == 0)
                    def _():
                        acc[...] = t

                    @pl.when(m > 0)
                    def _():
                        acc[...] += t

                    @pl.when(m == nm - 1)
                    def _():
                        o_ref[...] = acc[...].astype(o_ref.dtype)

    def b_map(st, w):
        nb = w // tn
        return lambda i, j, m: (jnp.where((j >= st) & (j < st + nb), m, 0), jnp.clip(j - st, 0, nb - 1))

    in_specs = [pl.BlockSpec((tmc, tk), lambda i, j, m: (m, i))]
    in_specs += [pl.BlockSpec((tmc, tn), b_map(st, w)) for st, w in zip(starts, widths)]
    if shard_width is None:
        out_shape = pltpu.HBM((K, N), BF16)
        out_spec = pl.BlockSpec((tk, tn), lambda i, j, m: (i, j))
    else:
        per = shard_width // tn
        out_shape = pltpu.HBM((N // shard_width, K, shard_width), BF16)
        out_spec = pl.BlockSpec((None, tk, tn), lambda i, j, m: (j // per, i, j % per))
    acc_shape = (tk, tn) if nm > 1 else (8, LANES)
    blk = (_nbytes((tmc, tk), a.dtype) + n_b * _nbytes((tmc, tn), b_list[0].dtype) + 2 * _nbytes((tk, tn), F32))
    return _pallas(
        body, name=name, out_shape=out_shape, grid=(K // tk, N // tn, nm), in_specs=in_specs, out_specs=out_spec,
        scratch_shapes=[pltpu.VMEM(acc_shape, F32)],
        compiler_params=_params(blk, ("parallel", "arbitrary", "arbitrary")))(a, *b_list)


CONV_PAD = 32
CONV_CHUNK = 512


def _rows(win, off, n, shifts):
    b, a = off % 8, off // 8
    if b not in shifts:
        shifts[b] = win if b == 0 else win[b:b + n + CONV_PAD - 8, :]
    return shifts[b][8 * a:8 * a + n, :]


def conv_fwd(proj, conv_w, conv_b, *, name, rider=None):
    S = proj.shape[0]
    KW, C = conv_w.shape
    nct = C // LANES
    rc = min(CONV_CHUNK, S)

    def body(a_ref, g_ref, w_ref, b_ref, o_ref, pad):
        pad[0:CONV_PAD, :] = jnp.zeros((CONV_PAD, LANES), F32)
        pad[CONV_PAD:, :] = a_ref[...] * _sigmoid(g_ref[...])
        w = w_ref[...]
        bias = b_ref[...]

        def chunk(i, _):
            base = pl.multiple_of(i * rc, rc)
            win = pad[pl.ds(base, rc + CONV_PAD), :]
            acc = jnp.zeros((rc, LANES), F32) + bias
            shifts = {}
            for k in range(KW):
                acc = acc + w[k:k + 1, :] * _rows(win, CONV_PAD - (KW - 1) + k, rc, shifts)
            o_ref[pl.ds(base, rc), :] = acc
            return 0

        lax.fori_loop(0, S // rc, chunk, 0)

    in_specs, out_specs, out_shape, scratch = _carry_specs(
        rider, [pl.BlockSpec((S, LANES), lambda c: (0, c)), pl.BlockSpec((S, LANES), lambda c: (0, c + nct)),
                pl.BlockSpec((KW, LANES), lambda c: (0, c)), pl.BlockSpec((1, LANES), lambda c: (0, c))],
        (pl.BlockSpec((S, LANES), lambda c: (0, c)),), (pltpu.HBM((S, C), F32),),
        [pltpu.VMEM((S + CONV_PAD, LANES), F32)])
    first = lambda: pl.program_id(0) == 0
    last = lambda: pl.program_id(0) == nct - 1
    res = _pallas(
        _carry(rider, body, 4, 1, first, last), name=name, grid=(nct,), out_shape=out_shape,
        in_specs=in_specs, out_specs=out_specs, scratch_shapes=scratch,
        compiler_params=_params(4 * _nbytes((S, LANES), F32), ("arbitrary",)))(
            proj, proj, conv_w, conv_b, *(rider.arrays if rider else ()))
    return res[0], list(res[1:])


def conv_bwd(du1, proj, conv_w, *, name):
    S = proj.shape[0]
    KW, C = conv_w.shape
    nct = C // LANES
    rc = min(CONV_CHUNK, S)

    def body(d_ref, a_ref, g_ref, w_ref, da_ref, dg_ref, dw_ref, db_ref, pad_u, pad_d, du0, dw_acc):
        dw_acc[...] = jnp.zeros_like(dw_acc)
        pad_u[0:CONV_PAD, :] = jnp.zeros((CONV_PAD, LANES), F32)
        pad_u[CONV_PAD:, :] = a_ref[...] * _sigmoid(g_ref[...])
        pad_d[0:S, :] = d_ref[...]
        pad_d[S:, :] = jnp.zeros((CONV_PAD, LANES), F32)
        w = w_ref[...]
        db_ref[...] = jnp.sum(d_ref[...], axis=0, keepdims=True)

        def chunk(i, _):
            base = pl.multiple_of(i * rc, rc)
            d = pad_d[pl.ds(base, rc), :]
            win_u = pad_u[pl.ds(base, rc + CONV_PAD), :]
            win_d = pad_d[pl.ds(base, rc + CONV_PAD), :]
            acc = jnp.zeros((rc, LANES), F32)
            shifts_u, shifts_d = {}, {}
            for k in range(KW):
                u_k = _rows(win_u, CONV_PAD - (KW - 1) + k, rc, shifts_u)
                dw_acc[k:k + 1, :] += jnp.sum(d * u_k, axis=0, keepdims=True)
                acc = acc + w[k:k + 1, :] * _rows(win_d, KW - 1 - k, rc, shifts_d)
            du0[pl.ds(base, rc), :] = acc
            return 0

        lax.fori_loop(0, S // rc, chunk, 0)
        dw_ref[...] = dw_acc[0:KW, :]
        a, sg = a_ref[...], _sigmoid(g_ref[...])
        d0 = du0[...]
        da_ref[...] = (d0 * sg).astype(BF16)
        dg_ref[...] = (d0 * a * sg * (1.0 - sg)).astype(BF16)

    col = lambda c: (0, c)
    return _pallas(
        body, name=name, grid=(nct,),
        out_shape=(pltpu.HBM((S, C), BF16), pltpu.HBM((S, C), BF16),
                   pltpu.HBM((KW, C), F32), pltpu.HBM((1, C), F32)),
        in_specs=[pl.BlockSpec((S, LANES), col), pl.BlockSpec((S, LANES), col),
                  pl.BlockSpec((S, LANES), lambda c: (0, c + nct)), pl.BlockSpec((KW, LANES), col)],
        out_specs=(pl.BlockSpec((S, LANES), col), pl.BlockSpec((S, LANES), col), pl.BlockSpec((KW, LANES), col),
                   pl.BlockSpec((1, LANES), col)),
        scratch_shapes=[pltpu.VMEM((S + CONV_PAD, LANES), F32), pltpu.VMEM((S + CONV_PAD, LANES), F32),
                        pltpu.VMEM((S, LANES), F32), pltpu.VMEM((CONV_PAD, LANES), F32)],
        compiler_params=_params(8 * _nbytes((S, LANES), F32), ("parallel",)))(du1, proj, proj, conv_w)


def ln_silu(u1, o_sb, gamma, beta, *, tm, name):
    S, C = u1.shape

    def body(u_ref, o_ref, g_ref, b_ref, out_ref):
        z = u_ref[...]
        mu = jnp.mean(z, axis=-1, keepdims=True)
        zc = z - mu
        y = zc * lax.rsqrt(jnp.mean(zc * zc, axis=-1, keepdims=True) + LN_EPS) * g_ref[...] + b_ref[...]
        out_ref[:, 0:C] = (y * _sigmoid(y)).astype(BF16)
        out_ref[:, C:] = o_ref[...].astype(BF16)

    row = lambda i: (i, 0)
    fix = lambda i: (0, 0)
    return _pallas(
        body, name=name, out_shape=pltpu.HBM((S, 2 * C), BF16), grid=(S // tm,),
        in_specs=[pl.BlockSpec((tm, C), row), pl.BlockSpec((tm, C), row), pl.BlockSpec((1, C), fix),
                  pl.BlockSpec((1, C), fix)],
        out_specs=pl.BlockSpec((tm, 2 * C), row),
        compiler_params=_params(4 * _nbytes((tm, C), F32), ("parallel",)))(u1, o_sb, gamma, beta)


def ln_silu_bwd(dua, u1, gamma, beta, *, tm, name):
    S, C = u1.shape

    def body(d_ref, u_ref, g_ref, b_ref, du1_ref, dg_ref, db_ref):
        @pl.when(pl.program_id(0) == 0)
        def _():
            dg_ref[...] = jnp.zeros_like(dg_ref)
            db_ref[...] = jnp.zeros_like(db_ref)

        z = u_ref[...]
        mu = jnp.mean(z, axis=-1, keepdims=True)
        zc = z - mu
        rstd = lax.rsqrt(jnp.mean(zc * zc, axis=-1, keepdims=True) + LN_EPS)
        zh = zc * rstd
        y = zh * g_ref[...] + b_ref[...]
        sg = _sigmoid(y)
        dy = d_ref[...] * (sg * (1.0 + y * (1.0 - sg)))
        dg_ref[...] += jnp.sum(dy * zh, axis=0, keepdims=True)
        db_ref[...] += jnp.sum(dy, axis=0, keepdims=True)
        dzh = dy * g_ref[...]
        m1 = jnp.mean(dzh, axis=-1, keepdims=True)
        m2 = jnp.mean(dzh * zh, axis=-1, keepdims=True)
        du1_ref[...] = rstd * (dzh - m1 - zh * m2)

    row = lambda i: (i, 0)
    fix = lambda i: (0, 0)
    return _pallas(
        body, name=name, grid=(S // tm,),
        out_shape=(pltpu.HBM((S, C), F32), pltpu.HBM((1, C), F32),
                   pltpu.HBM((1, C), F32)),
        in_specs=[pl.BlockSpec((tm, C), row), pl.BlockSpec((tm, C), row), pl.BlockSpec((1, C), fix),
                  pl.BlockSpec((1, C), fix)],
        out_specs=(pl.BlockSpec((tm, C), row), pl.BlockSpec((1, C), fix), pl.BlockSpec((1, C), fix)),
        compiler_params=_params(4 * _nbytes((tm, C), F32), ("arbitrary",)))(dua, u1, gamma, beta)


SB_BLOCK = 256
SB_STOP = -105.0


def _split_dot(x, tri):
    hi = x.astype(BF16)
    lo = (x - hi.astype(F32)).astype(BF16)
    return _dot(hi, tri) + _dot(lo, tri)


def _neg_softplus(z):
    return -(jnp.maximum(z, 0.0) + jnp.log(1.0 + jnp.exp(-jnp.abs(z))))


def sb_fwd(proj, *, q_col, name, rider=None):
    S = proj.shape[0]
    dh = LANES // 2
    W = SB_HEADS * dh
    npair = W // LANES
    T = min(SB_BLOCK, S)
    nblk = S // T
    scale = dh ** -0.5
    qb0 = q_col // LANES

    def body(q_ref, k_ref, v_ref, o_ref, l_ref, qs, ks, vs):
        r_i = lax.broadcasted_iota(jnp.int32, (T, T), 0)
        c_i = lax.broadcasted_iota(jnp.int32, (T, T), 1)
        tri = (r_i >= c_i).astype(BF16)
        vis = c_i < r_i
        lane = lax.broadcasted_iota(jnp.int32, (T, dh), 1)

        for hh in range(2):
            sl = slice(hh * dh, (hh + 1) * dh)
            qs[hh] = (q_ref[:, sl] * scale).astype(BF16)
            ks[hh] = k_ref[:, sl].astype(BF16)
            vs[hh] = v_ref[:, sl].astype(BF16)

        def step(qb, j0, diag, st):
            two = range(2)
            kb = [ks[hh, pl.ds(j0, T), :] for hh in two]
            vb = [vs[hh, pl.ds(j0, T), :] for hh in two]
            z = [_dot_nt(qb[hh], kb[hh]) for hh in two]
            lk = [_neg_softplus(z[hh]) for hh in two]
            if diag:
                lk = [jnp.where(vis, lk[hh], 0.0) for hh in two]
            C = [_split_dot(lk[hh], tri) for hh in two]
            A = [jnp.exp(z[hh] + C[hh] + st[2 * hh + 1]) for hh in two]
            if diag:
                A = [jnp.where(vis, A[hh], 0.0) for hh in two]
            acc = [st[2 * hh] + _dot(A[hh].astype(BF16), vb[hh]) for hh in two]
            return (acc[0], st[1] + C[0][:, 0:1], acc[1], st[3] + C[1][:, 0:1])

        def qblock(i, _):
            r0 = pl.multiple_of(i * T, T)
            qb = [qs[hh, pl.ds(r0, T), :] for hh in range(2)]
            zero = (jnp.zeros((T, dh), F32), jnp.zeros((T, 1), F32))
            state = step(qb, r0, True, zero + zero)

            def more(c):
                return (c[0] >= 0) & (jnp.max(jnp.maximum(c[2], c[4])) >= SB_STOP)

            def walk(c):
                return (c[0] - 1,) + step(qb, pl.multiple_of(c[0] * T, T), False, c[1:])

            c = lax.while_loop(more, walk, (i - 1,) + state)
            walked = (i - c[0]).astype(F32)
            for hh in range(2):
                sl = slice(hh * dh, (hh + 1) * dh)
                o_ref[pl.ds(r0, T), sl] = c[1 + 2 * hh]
                l_ref[pl.ds(r0, T), sl] = jnp.where(lane == 1, walked, c[2 + 2 * hh])
            return 0

        lax.fori_loop(0, nblk, qblock, 0)

    blk = lambda off: pl.BlockSpec((S, LANES), lambda h: (0, qb0 + off * npair + h))
    out = pl.BlockSpec((S, LANES), lambda h: (0, h))
    in_specs, out_specs, out_shape, scratch = _carry_specs(
        rider, [blk(0), blk(1), blk(2)], (out, out), (pltpu.HBM((S, W), F32), pltpu.HBM((S, W), F32)),
        [pltpu.VMEM((2, S, dh), BF16)] * 3)
    first = lambda: pl.program_id(0) == 0
    last = lambda: pl.program_id(0) == npair - 1
    res = _pallas(
        _carry(rider, body, 3, 2, first, last), name=name, grid=(npair,), out_shape=out_shape,
        in_specs=in_specs, out_specs=out_specs, scratch_shapes=scratch,
        compiler_params=_params(6 * _nbytes((S, LANES), F32), ("arbitrary",)))(
            proj, proj, proj, *(rider.arrays if rider else ()))
    return res[0], res[1], list(res[2:])


def sb_bwd(proj, ltot, dua, *, q_col, do_col, name, rider=None):
    S = proj.shape[0]
    dh = LANES // 2
    W = SB_HEADS * dh
    npair = W // LANES
    T = min(SB_BLOCK, S)
    nblk = S // T
    scale = dh ** -0.5
    qb0 = q_col // LANES
    db0 = do_col // LANES

    def body(q_ref, k_ref, v_ref, l_ref, do_ref, dq_ref, dk_ref, dv_ref, qs, ks, vs, dos, dks, dvs):
        r_i = lax.broadcasted_iota(jnp.int32, (T, T), 0)
        c_i = lax.broadcasted_iota(jnp.int32, (T, T), 1)
        tri_rev = (r_i >= c_i).astype(BF16)
        tri_fwd = (r_i <= c_i).astype(BF16)
        vis = c_i < r_i

        for hh in range(2):
            sl = slice(hh * dh, (hh + 1) * dh)
            qs[hh] = (q_ref[:, sl] * scale).astype(BF16)
            ks[hh] = k_ref[:, sl].astype(BF16)
            vs[hh] = v_ref[:, sl].astype(BF16)
            dos[hh] = do_ref[:, sl].astype(BF16)
        dks[...] = jnp.zeros_like(dks)
        dvs[...] = jnp.zeros_like(dvs)

        def step(qb, dob, Lt, j0, diag, st):
            two = range(2)
            kb = [ks[hh, pl.ds(j0, T), :] for hh in two]
            vb = [vs[hh, pl.ds(j0, T), :] for hh in two]
            z = [_dot_nt(qb[hh], kb[hh]) for hh in two]
            dA = [_dot_nt(dob[hh], vb[hh]) for hh in two]
            lk = [_neg_softplus(z[hh]) for hh in two]
            beta = [jnp.exp(z[hh] + lk[hh]) for hh in two]
            if diag:
                lk = [jnp.where(vis, lk[hh], 0.0) for hh in two]
            C = [_split_dot(lk[hh], tri_rev) for hh in two]
            rowsum = [C[hh][:, 0:1] for hh in two]
            A = [jnp.exp(z[hh] + C[hh] + (Lt[hh] - st[3 * hh + 1] - rowsum[hh])) for hh in two]
            if diag:
                A = [jnp.where(vis, A[hh], 0.0) for hh in two]
            g = [A[hh] * dA[hh] for hh in two]
            Gin = [_split_dot(g[hh], tri_fwd) for hh in two]
            dz = [g[hh] - beta[hh] * (st[3 * hh + 2] + Gin[hh]) for hh in two]
            if diag:
                dz = [jnp.where(vis, dz[hh], 0.0) for hh in two]
            dzb = [dz[hh].astype(BF16) for hh in two]
            out = ()
            for hh in two:
                dvs[hh, pl.ds(j0, T), :] += _dot_tn(A[hh].astype(BF16), dob[hh])
                dks[hh, pl.ds(j0, T), :] += _dot_tn(dzb[hh], qb[hh])
                out += (st[3 * hh] + _dot(dzb[hh], kb[hh]), st[3 * hh + 1] + rowsum[hh],
                        st[3 * hh + 2] + Gin[hh][:, T - 1:T])
            return out

        def qblock(i, _):
            r0 = pl.multiple_of(i * T, T)
            qb = [qs[hh, pl.ds(r0, T), :] for hh in range(2)]
            dob = [dos[hh, pl.ds(r0, T), :] for hh in range(2)]
            Lt = [l_ref[pl.ds(r0, T), hh * dh:hh * dh + 1] for hh in range(2)]
            walked = jnp.clip(jnp.max(l_ref[pl.ds(r0, 8), 1:2]).astype(jnp.int32), 1, i + 1)

            def inner(j, c):
                return step(qb, dob, Lt, pl.multiple_of(j * T, T), False, c)

            zero = jnp.zeros((T, 1), F32)
            init = (jnp.zeros((T, dh), F32), zero, zero)
            c = lax.fori_loop(i + 1 - walked, i, inner, init + init)
            c = step(qb, dob, Lt, r0, True, c)
            for hh in range(2):
                dq_ref[pl.ds(r0, T), hh * dh:(hh + 1) * dh] = (c[3 * hh] * scale).astype(BF16)
            return 0

        lax.fori_loop(0, nblk, qblock, 0)
        for hh in range(2):
            sl = slice(hh * dh, (hh + 1) * dh)
            dk_ref[:, sl] = dks[hh].astype(BF16)
            dv_ref[:, sl] = dvs[hh].astype(BF16)

    blk = lambda off: pl.BlockSpec((S, LANES), lambda h: (0, qb0 + off * npair + h))
    out = pl.BlockSpec((S, LANES), lambda h: (0, h))
    o_shape = pltpu.HBM((S, W), BF16)
    in_specs, out_specs, out_shape, scratch = _carry_specs(
        rider, [blk(0), blk(1), blk(2), out, pl.BlockSpec((S, LANES), lambda h: (0, db0 + h))], (out, out, out),
        (o_shape, o_shape, o_shape), [pltpu.VMEM((2, S, dh), BF16)] * 4 + [pltpu.VMEM((2, S, dh), F32)] * 2)
    first = lambda: pl.program_id(0) == 0
    last = lambda: pl.program_id(0) == npair - 1
    res = _pallas(
        _carry(rider, body, 5, 3, first, last), name=name, grid=(npair,), out_shape=out_shape,
        in_specs=in_specs, out_specs=out_specs, scratch_shapes=scratch,
        compiler_params=_params(12 * _nbytes((S, LANES), F32), ("arbitrary",)))(
            proj, proj, proj, ltot, dua, *(rider.arrays if rider else ()))
    return res[0], res[1], res[2], list(res[3:])


def xattn_fwd(q, k, v, *, tm, name):
    S, D = q.shape
    Mlen = k.shape[0]
    hd = D // MEM_HEADS
    scale = hd ** -0.5

    def body(q_ref, k_ref, v_ref, o_ref):
        for h in range(MEM_HEADS):
            sl = slice(h * hd, (h + 1) * hd)
            s = _dot_nt(q_ref[:, sl], k_ref[:, sl]) * scale
            e = jnp.exp(s - jnp.max(s, axis=-1, keepdims=True))
            p = e / jnp.sum(e, axis=-1, keepdims=True)
            o_ref[:, sl] = _dot(p.astype(BF16), v_ref[:, sl]).astype(BF16)

    row = lambda i: (i, 0)
    fix = lambda i: (0, 0)
    return _pallas(
        body, name=name, out_shape=pltpu.HBM((S, D), BF16), grid=(S // tm,),
        in_specs=[pl.BlockSpec((tm, D), row), pl.BlockSpec((Mlen, D), fix), pl.BlockSpec((Mlen, D), fix)],
        out_specs=pl.BlockSpec((tm, D), row),
        compiler_params=_params(4 * _nbytes((tm, D), F32), ("parallel",)))(q, k, v)


def xattn_bwd(q, do, k, v, *, tm, name):
    S, D = q.shape
    Mlen = k.shape[0]
    hd = D // MEM_HEADS
    scale = hd ** -0.5

    def body(q_ref, do_ref, k_ref, v_ref, dq_ref, dk_ref, dv_ref):
        @pl.when(pl.program_id(0) == 0)
        def _():
            dk_ref[...] = jnp.zeros_like(dk_ref)
            dv_ref[...] = jnp.zeros_like(dv_ref)

        for h in range(MEM_HEADS):
            sl = slice(h * hd, (h + 1) * hd)
            qh, doh, kh, vh = q_ref[:, sl], do_ref[:, sl], k_ref[:, sl], v_ref[:, sl]
            s = _dot_nt(qh, kh) * scale
            e = jnp.exp(s - jnp.max(s, axis=-1, keepdims=True))
            p = e / jnp.sum(e, axis=-1, keepdims=True)
            dp = _dot_nt(doh, vh)
            ds = (p * (dp - jnp.sum(p * dp, axis=-1, keepdims=True)) * scale).astype(BF16)
            dq_ref[:, sl] = _dot(ds, kh).astype(BF16)
            dk_ref[:, sl] += _dot_tn(ds, qh)
            dv_ref[:, sl] += _dot_tn(p.astype(BF16), doh)

    row = lambda i: (i, 0)
    fix = lambda i: (0, 0)
    return _pallas(
        body, name=name, grid=(S // tm,),
        out_shape=(pltpu.HBM((S, D), BF16), pltpu.HBM((Mlen, D), F32),
                   pltpu.HBM((Mlen, D), F32)),
        in_specs=[pl.BlockSpec((tm, D), row), pl.BlockSpec((tm, D), row), pl.BlockSpec((Mlen, D), fix),
                  pl.BlockSpec((Mlen, D), fix)],
        out_specs=(pl.BlockSpec((tm, D), row), pl.BlockSpec((Mlen, D), fix), pl.BlockSpec((Mlen, D), fix)),
        compiler_params=_params(6 * _nbytes((tm, D), F32), ("arbitrary",)))(q, do, k, v)


FFN_HALO = 8


def _conv3(ext, w, lo):
    tm = ext.shape[0] - FFN_HALO
    return (w[0:1, :] * ext[lo:lo + tm, :] + w[1:2, :] * ext[lo + 1:lo + 1 + tm, :]
            + w[2:3, :] * ext[lo + 2:lo + 2 + tm, :])


def ffn_up_fwd(xb, w_up, conv_w, conv_b, *, tm, tn, name, rider=None):
    S, D = xb.shape
    nsh, _, ns = w_up.shape
    F = nsh * ns // 2
    per = ns // tn
    ncol = F // tn
    KW = conv_w.shape[0]
    assert KW == 3

    def body(x_ref, wv_ref, wg_ref, cwv_ref, cwg_ref, cbv_ref, cbg_ref, uv_ref, ug_ref, h_ref, carry):
        @pl.when(pl.program_id(1) == 0)
        def _():
            carry[...] = jnp.zeros_like(carry)

        x = x_ref[...]
        uv = _dot(x, wv_ref[...])
        ug = _dot(x, wg_ref[...])
        uv_ref[...] = uv.astype(BF16)
        ug_ref[...] = ug.astype(BF16)
        lo = FFN_HALO - (KW - 1)
        cv = _conv3(jnp.concatenate([carry[0], uv], axis=0), cwv_ref[...], lo) + cbv_ref[...]
        cg = _conv3(jnp.concatenate([carry[1], ug], axis=0), cwg_ref[...], lo) + cbg_ref[...]
        carry[0] = uv[tm - FFN_HALO:, :]
        carry[1] = ug[tm - FFN_HALO:, :]
        h_ref[...] = (cg * _sigmoid(cg) * cv).astype(BF16)

    wspec = lambda half: pl.BlockSpec((None, D, tn), lambda j, i: (half * (nsh // 2) + j // per, 0, j % per))
    cspec = lambda rows, half: pl.BlockSpec((rows, tn), lambda j, i: (0, half * ncol + j))
    out = pl.BlockSpec((tm, tn), lambda j, i: (i, j))
    o_shape = pltpu.HBM((S, F), BF16)
    blk = _nbytes((tm, D), BF16) + 2 * _nbytes((D, tn), BF16) + 8 * _nbytes((tm, tn), F32)
    nrow = S // tm
    in_specs, out_specs, out_shape, scratch = _carry_specs(
        rider, [pl.BlockSpec((tm, D), lambda j, i: (i, 0)), wspec(0), wspec(1), cspec(KW, 0), cspec(KW, 1),
                cspec(1, 0), cspec(1, 1)], (out, out, out), (o_shape, o_shape, o_shape),
        [pltpu.VMEM((2, FFN_HALO, tn), F32)])
    first = lambda: (pl.program_id(0) == 0) & (pl.program_id(1) == 0)
    last = lambda: (pl.program_id(0) == ncol - 1) & (pl.program_id(1) == nrow - 1)
    res = _pallas(
        _carry(rider, body, 7, 3, first, last), name=name, grid=(ncol, nrow), out_shape=out_shape,
        in_specs=in_specs, out_specs=out_specs, scratch_shapes=scratch,
        compiler_params=_params(blk, ("arbitrary", "arbitrary")))(
            xb, w_up, w_up, conv_w, conv_w, conv_b, conv_b, *(rider.arrays if rider else ()))
    return res[0], res[1], res[2], list(res[3:])


def ffn_mid_bwd(dzb, w_down, up_v, up_g, conv_w, conv_b, *, tm, tn, name, rider=None):
    S, D = dzb.shape
    F = up_v.shape[1]
    ncol = F // tn
    nrow = S // tm
    KW = conv_w.shape[0]
    assert KW == 3
    hb = tm // FFN_HALO

    def body(dz_ref, wd_ref, uv_ref, ug_ref, hv_ref, hg_ref, cwv_ref, cwg_ref, cbv_ref, cbg_ref,
             dv_ref, dg_ref, dwv_ref, dwg_ref, dbv_ref, dbg_ref, carry):
        i = pl.program_id(1)

        @pl.when(i == 0)
        def _():
            carry[...] = jnp.zeros_like(carry)
            for r in (dwv_ref, dwg_ref, dbv_ref, dbg_ref):
                r[...] = jnp.zeros_like(r)

        first = i == nrow - 1
        halo_v = jnp.where(first, 0.0, hv_ref[...].astype(F32))
        halo_g = jnp.where(first, 0.0, hg_ref[...].astype(F32))
        ext_v = jnp.concatenate([halo_v, uv_ref[...].astype(F32)], axis=0)
        ext_g = jnp.concatenate([halo_g, ug_ref[...].astype(F32)], axis=0)
        cwv, cwg = cwv_ref[...], cwg_ref[...]
        lo = FFN_HALO - (KW - 1)
        taps_v = [ext_v[lo + k:lo + k + tm, :] for k in range(KW)]
        taps_g = [ext_g[lo + k:lo + k + tm, :] for k in range(KW)]
        cv = cwv[0:1, :] * taps_v[0] + cwv[1:2, :] * taps_v[1] + cwv[2:3, :] * taps_v[2] + cbv_ref[...]
        cg = cwg[0:1, :] * taps_g[0] + cwg[1:2, :] * taps_g[1] + cwg[2:3, :] * taps_g[2] + cbg_ref[...]
        dh = _dot_nt(dz_ref[...], wd_ref[...])
        sg = _sigmoid(cg)
        dcv = dh * (cg * sg)
        dcg = dh * cv * (sg * (1.0 + cg * (1.0 - sg)))

        def back(dc, taps, cw, slot, du_ref, dw_ref, db_ref):
            ext2 = jnp.concatenate([dc, carry[slot]], axis=0)
            du = cw[2:3, :] * dc + cw[1:2, :] * ext2[1:tm + 1, :] + cw[0:1, :] * ext2[2:tm + 2, :]
            du_ref[...] = du.astype(BF16)
            carry[slot] = dc[0:FFN_HALO, :]
            for k in range(KW):
                dw_ref[k:k + 1, :] += jnp.sum(dc * taps[k], axis=0, keepdims=True)
            db_ref[...] += jnp.sum(dc, axis=0, keepdims=True)

        back(dcv, taps_v, cwv, 0, dv_ref, dwv_ref, dbv_ref)
        back(dcg, taps_g, cwg, 1, dg_ref, dwg_ref, dbg_ref)

    rev = lambda i: nrow - 1 - i
    tile = pl.BlockSpec((tm, tn), lambda j, i: (rev(i), j))
    halo = pl.BlockSpec((FFN_HALO, tn), lambda j, i: (jnp.maximum(rev(i) * hb - 1, 0), j))
    cspec = lambda rows, half: pl.BlockSpec((rows, tn), lambda j, i: (0, half * ncol + j))
    acc = lambda rows: pl.BlockSpec((rows, tn), lambda j, i: (0, j))
    big = pltpu.HBM((S, F), BF16)
    blk = _nbytes((tm, D), BF16) + _nbytes((tn, D), BF16) + 10 * _nbytes((tm, tn), F32)
    in_specs, out_specs, out_shape, scratch = _carry_specs(
        rider, [pl.BlockSpec((tm, D), lambda j, i: (rev(i), 0)), pl.BlockSpec((tn, D), lambda j, i: (j, 0)),
                tile, tile, halo, halo, cspec(KW, 0), cspec(KW, 1), cspec(1, 0), cspec(1, 1)],
        (tile, tile, acc(KW), acc(KW), acc(1), acc(1)),
        (big, big, pltpu.HBM((KW, F), F32), pltpu.HBM((KW, F), F32), pltpu.HBM((1, F), F32),
         pltpu.HBM((1, F), F32)), [pltpu.VMEM((2, FFN_HALO, tn), F32)])
    first = lambda: (pl.program_id(0) == 0) & (pl.program_id(1) == 0)
    last = lambda: (pl.program_id(0) == ncol - 1) & (pl.program_id(1) == nrow - 1)
    res = _pallas(
        _carry(rider, body, 10, 6, first, last), name=name, grid=(ncol, nrow), out_shape=out_shape,
        in_specs=in_specs, out_specs=out_specs, scratch_shapes=scratch,
        compiler_params=_params(blk, ("arbitrary", "arbitrary")))(
            dzb, w_down, up_v, up_g, up_v, up_g, conv_w, conv_w, conv_b, conv_b,
            *(rider.arrays if rider else ()))
    return res[:6], list(res[6:])


def loss_head(y, target, *, tm, name):
    S, D = y.shape

    def body(y_ref, t_ref, dy_ref, l_ref):
        @pl.when(pl.program_id(0) == 0)
        def _():
            l_ref[...] = jnp.zeros_like(l_ref)

        e = y_ref[...] - t_ref[...]
        dy_ref[...] = e * (1.0 / D)
        l_ref[...] += 0.5 * jnp.sum(jnp.mean(e * e, axis=-1, keepdims=True), axis=0, keepdims=True)

    row = lambda i: (i, 0)
    return _pallas(
        body, name=name, grid=(S // tm,),
        out_shape=(pltpu.HBM((S, D), F32), pltpu.HBM((1, 1), F32)),
        in_specs=[pl.BlockSpec((tm, D), row), pl.BlockSpec((tm, D), row)],
        out_specs=(pl.BlockSpec((tm, D), row), pl.BlockSpec((1, 1), lambda i: (0, 0))),
        compiler_params=_params(3 * _nbytes((tm, D), F32), ("arbitrary",)))(y, target)


def adamw(w, g, m, v, *, tr, name):
    R, C = w.shape
    c1 = 1.0 - ADAM_B1 ** ADAM_STEP
    c2 = 1.0 - ADAM_B2 ** ADAM_STEP

    def body(w_ref, g_ref, m_ref, v_ref, go_ref, d_ref, mo_ref, vo_ref):
        gv = g_ref[...]
        mn = ADAM_B1 * m_ref[...] + (1.0 - ADAM_B1) * gv
        vn = ADAM_B2 * v_ref[...] + (1.0 - ADAM_B2) * (gv * gv)
        go_ref[...] = gv
        mo_ref[...] = mn
        vo_ref[...] = vn
        d_ref[...] = -ADAM_LR * ((mn / c1) / (jnp.sqrt(vn / c2) + ADAM_EPS) + ADAM_WD * w_ref[...])

    spec = pl.BlockSpec((tr, C), lambda i: (i, 0))
    shape = pltpu.HBM((R, C), F32)
    return _pallas(
        body, name=name, grid=(R // tr,), out_shape=(shape,) * 4, in_specs=[spec] * 4, out_specs=(spec,) * 4,
        compiler_params=_params(8 * _nbytes((tr, C), F32), ("parallel",)))(w, g, m, v)


def add_pairs(gs, gots, core, *, name):
    k = len(gs)

    def body(c_ref, *refs):
        for a_ref, b_ref, o_ref in zip(refs[:k], refs[k:2 * k], refs[2 * k:]):
            o_ref[...] = (a_ref[...].astype(F32) + b_ref[...].astype(F32)).astype(BF16)

    own = [pl.BlockSpec((None, None) + g.shape[2:], lambda i, c: (i, c[0], 0, 0)) for g in gs]
    half = [pl.BlockSpec((None,) + g.shape[1:], lambda i, c: (i, 0, 0)) for g in gots]
    grid_spec = pltpu.PrefetchScalarGridSpec(
        num_scalar_prefetch=1, grid=(N_CHIPS,), in_specs=own + half, out_specs=tuple(half))
    blk = 3 * sum(_nbytes(g.shape[1:], BF16) for g in gots)
    return _pallas(
        body, name=name, grid_spec=grid_spec, out_shape=tuple(pltpu.HBM(g.shape, BF16) for g in gots),
        compiler_params=_params(blk, ("parallel",)))(core, *gs, *gots)


def sum_chips_into(bs, dests, layer, core, *, name):
    k = len(bs)
    steps = 2

    def body(c_ref, *refs):
        for b_ref, o_ref in zip(refs[:k], refs[2 * k:]):
            acc = b_ref[0].astype(F32)
            for p in range(1, N_CHIPS):
                acc = acc + b_ref[p].astype(F32)
            o_ref[...] = acc

    ins = [pl.BlockSpec((N_CHIPS, b.shape[1] // steps, b.shape[2]), lambda i, c: (0, i, 0)) for b in bs]
    outs = tuple(pl.BlockSpec((None, None, b.shape[1] // steps, b.shape[2]), lambda i, c: (layer, c[0], i, 0))
                 for b in bs)
    grid_spec = pltpu.PrefetchScalarGridSpec(
        num_scalar_prefetch=1, grid=(steps,), in_specs=ins + [pl.BlockSpec(memory_space=pl.ANY)] * k,
        out_specs=outs)
    blk = sum(_nbytes(b.shape, BF16) + _nbytes(b.shape[1:], F32) for b in bs) // steps
    return _pallas(
        body, name=name, grid_spec=grid_spec, out_shape=tuple(pltpu.HBM(d.shape, F32) for d in dests),
        input_output_aliases={1 + k + w: w for w in range(k)},
        compiler_params=_params(blk, ("parallel",)))(core, *bs, *dests)


_HBM = pl.BlockSpec(memory_space=pltpu.HBM)


def _place():
    x, y, c = lax.axis_index("x"), lax.axis_index("y"), lax.axis_index("c")
    chips = [(1 - x, y), (x, 1 - y), (1 - x, 1 - y)]
    return x, y, c, chips


class GatherRider:
    def __init__(self, shards):
        self.arrays = list(shards)
        self.n = n = len(shards)
        self.out_shape = tuple(pltpu.HBM((N_CHIPS,) + s.shape, s.dtype) for s in shards)
        self.scratch = [pltpu.SemaphoreType.DMA((n, 3))] * 4 + [pltpu.SemaphoreType.DMA((n,))]

    def _copies(self, ins, outs, sems):
        send_ici, recv_ici, send_d2d, recv_d2d, local = sems
        x, y, c, chips = _place()
        me = 2 * x + y

        def own(w):
            return pltpu.make_async_copy(ins[w], outs[w].at[me], local.at[w])

        def ici(w, j):
            px, py = chips[j]
            return pltpu.make_async_remote_copy(
                src_ref=ins[w].at[c], dst_ref=outs[w].at[me, c], send_sem=send_ici.at[w, j],
                recv_sem=recv_ici.at[w, j], device_id=(px, py, c), device_id_type=MESH)

        def landed(w, j, half):
            px, py = chips[j]
            return outs[w].at[2 * px + py, half]

        def d2d(w, j, half):
            return pltpu.make_async_remote_copy(
                src_ref=landed(w, j, half), dst_ref=landed(w, j, half), send_sem=send_d2d.at[w, j],
                recv_sem=recv_d2d.at[w, j], device_id=(x, y, 1 - c), device_id_type=MESH)

        def ici_arrival(w, j):
            return pltpu.make_async_remote_copy(
                src_ref=landed(w, j, c), dst_ref=landed(w, j, c), send_sem=send_ici.at[w, j],
                recv_sem=recv_ici.at[w, j], device_id=(x, y, c), device_id_type=MESH)

        return c, own, ici, d2d, ici_arrival

    def start(self, ins, outs, sems):
        c, own, ici, d2d, ici_arrival = self._copies(ins, outs, sems)
        for w in range(self.n):
            own(w).start()
            for j in range(3):
                ici(w, j).start()

    def finish(self, ins, outs, sems):
        c, own, ici, d2d, ici_arrival = self._copies(ins, outs, sems)
        for w in range(self.n):
            for j in range(3):
                ici_arrival(w, j).wait_recv()
                d2d(w, j, c).start()
        for w in range(self.n):
            for j in range(3):
                d2d(w, j, 1 - c).wait_recv()
        for w in range(self.n):
            for j in range(3):
                ici(w, j).wait_send()
                d2d(w, j, c).wait_send()
            own(w).wait()


class ScatterRider:
    def __init__(self, parts):
        self.arrays = list(parts)
        self.n = n = len(parts)
        self.out_shape = tuple(pltpu.HBM(p.shape, p.dtype) for p in parts)
        self.scratch = [pltpu.SemaphoreType.DMA((n, 3))] * 2 + [pltpu.SemaphoreType.DMA((n,))]

    def _copies(self, ins, outs, sems):
        send, recv, local = sems
        x, y, c, chips = _place()
        me = 2 * x + y

        def own(w):
            return pltpu.make_async_copy(ins[w].at[me], outs[w].at[me], local.at[w])

        def copy(w, j):
            px, py = chips[j]
            return pltpu.make_async_remote_copy(
                src_ref=ins[w].at[2 * px + py], dst_ref=outs[w].at[me], send_sem=send.at[w, j],
                recv_sem=recv.at[w, j], device_id=(px, py, c), device_id_type=MESH)

        def arrival(w, j):
            px, py = chips[j]
            blk = outs[w].at[2 * px + py]
            return pltpu.make_async_remote_copy(
                src_ref=blk, dst_ref=blk, send_sem=send.at[w, j], recv_sem=recv.at[w, j],
                device_id=(x, y, c), device_id_type=MESH)

        return own, copy, arrival

    def start(self, ins, outs, sems):
        own, copy, arrival = self._copies(ins, outs, sems)
        for w in range(self.n):
            own(w).start()
            for j in range(3):
                copy(w, j).start()

    def finish(self, ins, outs, sems):
        own, copy, arrival = self._copies(ins, outs, sems)
        for w in range(self.n):
            for j in range(3):
                arrival(w, j).wait_recv()
        for w in range(self.n):
            for j in range(3):
                copy(w, j).wait_send()
            own(w).wait()


def _carry(rider, body, n_in, n_out, first, last):
    if rider is None:
        return body
    k, m = rider.n, len(rider.scratch)

    def carried(*refs):
        ins, r_in = refs[:n_in], refs[n_in:n_in + k]
        outs, r_out = refs[n_in + k:n_in + k + n_out], refs[n_in + k + n_out:n_in + 2 * k + n_out]
        rest = refs[n_in + 2 * k + n_out:]
        scratch, sems = rest[:len(rest) - m], rest[len(rest) - m:]

        @pl.when(first())
        def _():
            rider.start(r_in, r_out, sems)

        body(*ins, *outs, *scratch)

        @pl.when(last())
        def _():
            rider.finish(r_in, r_out, sems)

    return carried


def _carry_specs(rider, in_specs, out_specs, out_shape, scratch):
    if rider is None:
        return list(in_specs), tuple(out_specs), tuple(out_shape), list(scratch)
    k = rider.n
    return (list(in_specs) + [_HBM] * k, tuple(out_specs) + (_HBM,) * k, tuple(out_shape) + rider.out_shape,
            list(scratch) + list(rider.scratch))


def run_rider(rider, *, name):
    k = rider.n

    def body(*refs):
        rider.start(refs[:k], refs[k:2 * k], refs[2 * k:])
        rider.finish(refs[:k], refs[k:2 * k], refs[2 * k:])

    return _pallas(body, name=name, out_shape=rider.out_shape, in_specs=[_HBM] * k, out_specs=(_HBM,) * k,
                   scratch_shapes=rider.scratch)(*rider.arrays)


def allgather_small(shards, *, name):
    n = len(shards)

    def body(*refs):
        ins, outs = refs[:n], refs[n:2 * n]
        send, recv, local = refs[2 * n:]
        x, y, c, chips = _place()
        me = 2 * x + y
        locals_ = [pltpu.make_async_copy(ins[w], outs[w].at[me], local.at[w]) for w in range(n)]
        for cp in locals_:
            cp.start()

        def copy(w, j):
            px, py = chips[j]
            return pltpu.make_async_remote_copy(
                src_ref=ins[w], dst_ref=outs[w].at[me], send_sem=send.at[w, j], recv_sem=recv.at[w, j],
                device_id=(px, py, c), device_id_type=MESH)

        def arrival(w, j):
            px, py = chips[j]
            blk = outs[w].at[2 * px + py]
            return pltpu.make_async_remote_copy(
                src_ref=blk, dst_ref=blk, send_sem=send.at[w, j], recv_sem=recv.at[w, j],
                device_id=(x, y, c), device_id_type=MESH)

        for w in range(n):
            for j in range(3):
                copy(w, j).start()
        for w in range(n):
            for j in range(3):
                arrival(w, j).wait_recv()
        for w in range(n):
            for j in range(3):
                copy(w, j).wait_send()
        for cp in locals_:
            cp.wait()

    out_shape = tuple(pltpu.HBM((N_CHIPS,) + s.shape, s.dtype) for s in shards)
    return _pallas(
        body, name=name, out_shape=out_shape, in_specs=[_HBM] * n, out_specs=(_HBM,) * n,
        scratch_shapes=[pltpu.SemaphoreType.DMA((n, 3))] * 2 + [pltpu.SemaphoreType.DMA((n,))],
    )(*shards)


def rs_sibling_swap(grads, *, name):
    n = len(grads)

    def body(*refs):
        ins, gots = refs[:n], refs[n:2 * n]
        send, recv = refs[2 * n:]
        x, y, c, _ = _place()
        swaps = [pltpu.make_async_remote_copy(
            src_ref=ins[w].at[:, 1 - c], dst_ref=gots[w], send_sem=send.at[w], recv_sem=recv.at[w],
            device_id=(x, y, 1 - c), device_id_type=MESH) for w in range(n)]
        for cp in swaps:
            cp.start()
        for cp in swaps:
            cp.wait_recv()
        for cp in swaps:
            cp.wait_send()

    half = tuple(pltpu.HBM((N_CHIPS,) + g.shape[2:], g.dtype) for g in grads)
    return _pallas(
        body, name=name, out_shape=half, in_specs=[_HBM] * n, out_specs=(_HBM,) * n,
        scratch_shapes=[pltpu.SemaphoreType.DMA((n,))] * 2,
    )(*grads)


def rs_sibling_share(stacked, *, name):
    n = len(stacked)

    def body(*refs):
        bufs = refs[n:2 * n]
        send, recv = refs[2 * n:]
        x, y, c, _ = _place()
        shares, arrivals = [], []
        for w in range(n):
            mine, other = bufs[w].at[:, c], bufs[w].at[:, 1 - c]
            shares.append(pltpu.make_async_remote_copy(
                src_ref=mine, dst_ref=mine, send_sem=send.at[w], recv_sem=recv.at[w],
                device_id=(x, y, 1 - c), device_id_type=MESH))
            arrivals.append(pltpu.make_async_remote_copy(
                src_ref=other, dst_ref=other, send_sem=send.at[w], recv_sem=recv.at[w],
                device_id=(x, y, c), device_id_type=MESH))
        for cp in shares:
            cp.start()
        for cp in arrivals:
            cp.wait_recv()
        for cp in shares:
            cp.wait_send()

    out_shape = tuple(pltpu.HBM(s.shape, F32) for s in stacked)
    return _pallas(
        body, name=name, out_shape=out_shape, in_specs=[_HBM] * n, out_specs=(_HBM,) * n,
        input_output_aliases={w: w for w in range(n)},
        scratch_shapes=[pltpu.SemaphoreType.DMA((n,))] * 2,
    )(*stacked)


def allreduce_small(v, *, name):
    R, C = v.shape

    def body(v_ref, o_ref, land, send, recv):
        x, y, c, _ = _place()
        me = 4 * x + 2 * y + c
        land[me] = v_ref[...]

        def flip(k):
            return (1 - x) if k & 4 else x, (1 - y) if k & 2 else y, (1 - c) if k & 1 else c

        copies = []
        for k in range(1, N_DEV):
            px, py, pc = flip(k)
            copies.append(pltpu.make_async_remote_copy(
                src_ref=v_ref, dst_ref=land.at[me], send_sem=send.at[k - 1], recv_sem=recv.at[k - 1],
                device_id=(px, py, pc), device_id_type=MESH))
        for cp in copies:
            cp.start()
        for k in range(1, N_DEV):
            px, py, pc = flip(k)
            blk = land.at[4 * px + 2 * py + pc]
            pltpu.make_async_remote_copy(
                src_ref=blk, dst_ref=blk, send_sem=send.at[k - 1], recv_sem=recv.at[k - 1],
                device_id=(x, y, c), device_id_type=MESH).wait_recv()
        for cp in copies:
            cp.wait_send()
        acc = land[0]
        for d in range(1, N_DEV):
            acc = acc + land[d]
        o_ref[...] = acc

    vm = pl.BlockSpec(memory_space=pltpu.VMEM)
    return pl.pallas_call(
        body, name=name, out_shape=jax.ShapeDtypeStruct((R, C), F32), in_specs=[vm], out_specs=vm,
        scratch_shapes=[pltpu.VMEM((N_DEV, R, C), F32), pltpu.SemaphoreType.DMA((N_DEV - 1,)),
                        pltpu.SemaphoreType.DMA((N_DEV - 1,))],
        compiler_params=pltpu.CompilerParams(vmem_limit_bytes=int(min(12 * R * C * 4 + (8 << 20), VMEM_CAP))),
    )(v)


def _pack(arrays):
    flat = jnp.concatenate([a.reshape(-1) for a in arrays])
    return flat.reshape(-1, LANES)


def _unpack(packed, shapes):
    flat = packed.reshape(-1)
    out, off = [], 0
    for s in shapes:
        n = 1
        for d in s:
            n *= d
        out.append(flat[off:off + n].reshape(s))
        off += n
    return out


def _row_tile(rows, cap=512):
    t = 1 << (cap.bit_length() - 1)
    while rows % t:
        t //= 2
    return t


def _adamw_tile(rows, cols):
    return _row_tile(rows, max(8, (1 << 20) // (4 * cols)))


def kernel(x, mem, w_in, conv_w, conv_b, conv_ln_g, conv_ln_b, w_out, ln1_g, ln1_b, mem_wq, mem_wk, mem_wv, mem_wo, ln2_g, ln2_b, ffn_up, ffn_conv_w, ffn_conv_b, ffn_down, ln3_g, ln3_b, loss_target, m_w_in, m_conv_w, m_conv_b, m_conv_ln_g, m_conv_ln_b, m_w_out, m_ln1_g, m_ln1_b, m_mem_wq, m_mem_wk, m_mem_wv, m_mem_wo, m_ln2_g, m_ln2_b, m_ffn_up, m_ffn_conv_w, m_ffn_conv_b, m_ffn_down, m_ln3_g, m_ln3_b, v_w_in, v_conv_w, v_conv_b, v_conv_ln_g, v_conv_ln_b, v_w_out, v_ln1_g, v_ln1_b, v_mem_wq, v_mem_wk, v_mem_wv, v_mem_wo, v_ln2_g, v_ln2_b, v_ffn_up, v_ffn_conv_w, v_ffn_conv_b, v_ffn_down, v_ln3_g, v_ln3_b):
    W = dict(w_in=w_in, conv_w=conv_w, conv_b=conv_b, conv_ln_g=conv_ln_g, conv_ln_b=conv_ln_b, w_out=w_out,
             ln1_g=ln1_g, ln1_b=ln1_b, mem_wq=mem_wq, mem_wk=mem_wk, mem_wv=mem_wv, mem_wo=mem_wo, ln2_g=ln2_g,
             ln2_b=ln2_b, ffn_up=ffn_up, ffn_conv_w=ffn_conv_w, ffn_conv_b=ffn_conv_b, ffn_down=ffn_down,
             ln3_g=ln3_g, ln3_b=ln3_b)
    M1 = dict(w_in=m_w_in, conv_w=m_conv_w, conv_b=m_conv_b, conv_ln_g=m_conv_ln_g, conv_ln_b=m_conv_ln_b,
              w_out=m_w_out, ln1_g=m_ln1_g, ln1_b=m_ln1_b, mem_wq=m_mem_wq, mem_wk=m_mem_wk, mem_wv=m_mem_wv,
              mem_wo=m_mem_wo, ln2_g=m_ln2_g, ln2_b=m_ln2_b, ffn_up=m_ffn_up, ffn_conv_w=m_ffn_conv_w,
              ffn_conv_b=m_ffn_conv_b, ffn_down=m_ffn_down, ln3_g=m_ln3_g, ln3_b=m_ln3_b)
    V2 = dict(w_in=v_w_in, conv_w=v_conv_w, conv_b=v_conv_b, conv_ln_g=v_conv_ln_g, conv_ln_b=v_conv_ln_b,
              w_out=v_w_out, ln1_g=v_ln1_g, ln1_b=v_ln1_b, mem_wq=v_mem_wq, mem_wk=v_mem_wk, mem_wv=v_mem_wv,
              mem_wo=v_mem_wo, ln2_g=v_ln2_g, ln2_b=v_ln2_b, ffn_up=v_ffn_up, ffn_conv_w=v_ffn_conv_w,
              ffn_conv_b=v_ffn_conv_b, ffn_down=v_ffn_down, ln3_g=v_ln3_g, ln3_b=v_ln3_b)

    L = w_in.shape[0]
    S, D = x.shape[1], x.shape[2]
    C = conv_b.shape[1]
    alpha = (2.0 * L) ** 0.25
    chip = 2 * lax.axis_index("x") + lax.axis_index("y")
    xs, mems, tgt = x[0], mem[0], loss_target[0]
    mem_bf = mems.astype(BF16)
    tm = _row_tile(S)
    tm_ffn = _row_tile(S, 256)

    def shards_of(l, names):
        out = []
        for n in names:
            wl = W[n][l].astype(BF16)
            out.append(wl.reshape(2, wl.shape[0] // 2, wl.shape[1]))
        return out

    def gathered(names, got):
        layer = {}
        for n, g in zip(names, got):
            rows, cols = W[n].shape[1], W[n].shape[2]
            layer[n] = g.reshape(N_CHIPS, rows, cols) if n in COL_SHARDED else g.reshape(N_CHIPS * rows, cols)
        return layer

    full = [dict() for _ in range(L)]
    full[0].update(gathered(RIDE_IN, run_rider(GatherRider(shards_of(0, RIDE_IN)), name="allgather_w_in")))
    cw_all, fcw_all = allgather_small([conv_w, ffn_conv_w], name="allgather_small")
    cw_full = jnp.transpose(cw_all, (1, 2, 0, 3)).reshape(L, conv_w.shape[1], -1)
    fcw_full = jnp.transpose(fcw_all, (1, 2, 0, 3)).reshape(L, ffn_conv_w.shape[1], -1)

    saved = []
    h, hb = xs, xs.astype(BF16)
    for l in range(L):
        fw = full[l]
        s = dict(x=h, xb=hb)
        s['proj'] = mm_nn(hb, fw['w_in'], F32, tm=min(1024, S), tn=fw['w_in'].shape[2], name="proj")
        s['u1'], got = conv_fwd(s['proj'], cw_full[l], conv_b[l][None], name="conv_fwd",
                                rider=GatherRider(shards_of(0, RIDE_ATT)) if l == 0 else None)
        if l == 0:
            fw.update(gathered(RIDE_ATT, got))
        more = l + 1 < L
        s['o_sb'], s['ltot'], got = sb_fwd(
            s['proj'], q_col=2 * C, name="sb_fwd",
            rider=GatherRider(shards_of(l, RIDE_FFN) + (shards_of(l + 1, RIDE_IN) if more else [])))
        fw.update(gathered(RIDE_FFN, got[:len(RIDE_FFN)]))
        if more:
            full[l + 1].update(gathered(RIDE_IN, got[len(RIDE_FFN):]))
        s['ua'] = ln_silu(s['u1'], s['o_sb'], conv_ln_g[l][None], conv_ln_b[l][None], tm=tm, name="ln_silu")
        s['x1'], s['x1b'], s['zh1'], s['rs1'] = mm_ln(
            s['ua'], fw['w_out'], h, ln1_g[l][None], ln1_b[l][None], alpha, tm=tm, name="out_proj_ln")
        s['q2'] = mm_nn(s['x1b'], fw['mem_wq'], BF16, tm=min(1024, S), tn=512, name="mem_q")
        s['k2'] = mm_nn(mem_bf, fw['mem_wk'], BF16, tm=mem_bf.shape[0], tn=512, name="mem_kv")
        s['v2'] = mm_nn(mem_bf, fw['mem_wv'], BF16, tm=mem_bf.shape[0], tn=512, name="mem_kv")
        s['o2'] = xattn_fwd(s['q2'], s['k2'], s['v2'], tm=tm, name="xattn_fwd")
        s['x2'], s['x2b'], s['zh2'], s['rs2'] = mm_ln(
            s['o2'], fw['mem_wo'], s['x1'], ln2_g[l][None], ln2_b[l][None], alpha, tm=tm, name="mem_o_ln")
        s['upv'], s['upg'], s['hmid'], got = ffn_up_fwd(
            s['x2b'], fw['ffn_up'], fcw_full[l], ffn_conv_b[l][None], tm=tm_ffn, tn=fw['ffn_up'].shape[2],
            name="ffn_up_fwd", rider=GatherRider(shards_of(l + 1, RIDE_ATT)) if more else None)
        if more:
            full[l + 1].update(gathered(RIDE_ATT, got))
        h, hb, s['zh3'], s['rs3'] = mm_ln(
            s['hmid'], fw['ffn_down'], s['x2'], ln3_g[l][None], ln3_b[l][None], alpha, tm=tm, name="ffn_down_ln")
        saved.append(s)

    dx, loss_part = loss_head(h, tgt, tm=tm, name="loss_head")
    loss = lax.psum(loss_part[0, 0], ("x", "y", "c"))

    core = lax.axis_index("c").astype(jnp.int32).reshape(1)
    reduced_big = {n: lax.empty((L, 2, W[n].shape[1] // 2, W[n].shape[2]), F32) for n in BIG}
    small_grads = [None] * L

    def pre_add(g, names):
        parts = []
        for n in names:
            rows, cols = W[n].shape[1], W[n].shape[2]
            parts.append(g[n].reshape(N_CHIPS, 2, rows // 2, cols))
        got = rs_sibling_swap(parts, name="rs_sibling_swap")
        return list(add_pairs(parts, got, core, name="rs_add_pairs"))

    def reduce_into(names, scattered, layer):
        reduced_big.update(zip(names, sum_chips_into(
            list(scattered), [reduced_big[n] for n in names], layer, core, name="rs_sum_chips")))

    pending = None
    for l in reversed(range(L)):
        fw, s = full[l], saved[l]
        g = {}
        dz3, dz3b, g['ln3_g'], g['ln3_b'] = ln_bwd(dx, s['zh3'], s['rs3'], ln3_g[l][None], tm=tm, name="ln_bwd")
        ftn = fw['ffn_up'].shape[2]
        (dupv, dupg, dfw_v, dfw_g, dfb_v, dfb_g), sc = ffn_mid_bwd(
            dz3b, fw['ffn_down'], s['upv'], s['upg'], fcw_full[l], ffn_conv_b[l][None], tm=tm_ffn, tn=ftn,
            name="ffn_mid_bwd", rider=ScatterRider(pending) if pending else None)
        if pending:
            reduce_into(RIDE_MIX, sc, l + 1)
        g['ffn_conv_w'] = jnp.concatenate([dfw_v, dfw_g], axis=1)
        g['ffn_conv_b'] = jnp.concatenate([dfb_v, dfb_g], axis=1)[0]
        g['ffn_down'] = mm_tn(s['hmid'], [dz3b], tk=ftn, tn=512, tmc=min(1024, S), name="grad_ffn_down")
        dx2 = mm_nt([dupv, dupg], fw['ffn_up'], F32, tm=tm, tk=512, res=dz3, alpha=alpha, name="ffn_up_bwd")
        g['ffn_up'] = mm_tn(s['x2b'], [dupv, dupg], tk=512, tn=ftn, shard_width=ftn, tmc=min(1024, S),
                            name="grad_ffn_up")

        dz2, dz2b, g['ln2_g'], g['ln2_b'] = ln_bwd(dx2, s['zh2'], s['rs2'], ln2_g[l][None], tm=tm, name="ln_bwd")
        do2 = mm_nt([dz2b], fw['mem_wo'], BF16, tm=tm, tk=512, name="mem_o_bwd")
        g['mem_wo'] = mm_tn(s['o2'], [dz2b], tk=512, tn=512, name="grad_sq")
        dq2, dk2, dv2 = xattn_bwd(s['q2'], do2, s['k2'], s['v2'], tm=tm, name="xattn_bwd")
        dx1 = mm_nt([dq2], fw['mem_wq'], F32, tm=tm, tk=512, res=dz2, alpha=alpha, name="mem_q_bwd")
        g['mem_wq'] = mm_tn(s['x1b'], [dq2], tk=512, tn=512, name="grad_sq")
        g['mem_wk'] = mm_tn(mem_bf, [dk2], tk=512, tn=512, name="grad_mem_kv")
        g['mem_wv'] = mm_tn(mem_bf, [dv2], tk=512, tn=512, name="grad_mem_kv")

        dz1, dz1b, g['ln1_g'], g['ln1_b'] = ln_bwd(dx1, s['zh1'], s['rs1'], ln1_g[l][None], tm=tm, name="ln_bwd")
        dua = mm_nt([dz1b], fw['w_out'], F32, tm=tm, tk=512, name="out_proj_bwd")
        g['w_out'] = mm_tn(s['ua'], [dz1b], tk=512, tn=512, name="grad_sq")
        dq, dk, dv, sc = sb_bwd(
            s['proj'], s['ltot'], dua, q_col=2 * C, do_col=C, name="sb_bwd",
            rider=ScatterRider(pre_add(g, RIDE_REST)))
        reduce_into(RIDE_REST, sc, l)
        du1, g['conv_ln_g'], g['conv_ln_b'] = ln_silu_bwd(
            dua, s['u1'], conv_ln_g[l][None], conv_ln_b[l][None], tm=tm, name="ln_silu_bwd")
        da, dg, g['conv_w'], dcb = conv_bwd(du1, s['proj'], cw_full[l], name="conv_bwd")
        g['conv_b'] = dcb
        dproj = jnp.concatenate([da, dg, dq, dk, dv], axis=1)
        ns_in = fw['w_in'].shape[2]
        dx = mm_nt([dproj], fw['w_in'], F32, tm=tm, tk=512, res=dz1, alpha=alpha, name="proj_bwd")
        g['w_in'] = mm_tn(s['xb'], [dproj], tk=512, tn=ns_in, shard_width=ns_in, name="grad_w_in")

        pending = pre_add(g, RIDE_MIX)
        small_grads[l] = {n: g[n].reshape(W[n].shape[1:-1] + (-1,)) for n in SMALL}

    grad_x = dx[None]

    reduce_into(RIDE_MIX, run_rider(ScatterRider(pending), name="rs_chip_scatter"), 0)
    shared = rs_sibling_share([reduced_big[n] for n in BIG], name="rs_sibling_share")
    G = {}
    for n, sh in zip(BIG, shared):
        G[n] = sh.reshape(W[n].shape)

    small_full_shapes = []
    small_stack = []
    for n in SMALL:
        st = jnp.stack([small_grads[l][n] for l in range(L)])
        small_stack.append(st)
        small_full_shapes.append(st.shape)
    reduced = _unpack(allreduce_small(_pack(small_stack), name="allreduce_small"), small_full_shapes)
    for n, r in zip(SMALL, reduced):
        if n in SMALL_SHARDED:
            width = W[n].shape[-1]
            r = lax.dynamic_slice_in_dim(r, chip * width, width, axis=2)
        G[n] = r

    out_g, out_d, out_m, out_v = {}, {}, {}, {}
    for n in BIG:
        shp = W[n].shape
        flat = lambda a: a.reshape(shp[0] * shp[1], shp[2])
        res = adamw(flat(W[n]), flat(G[n]), flat(M1[n]), flat(V2[n]), tr=_adamw_tile(shp[0] * shp[1], shp[2]), name="adamw")
        out_g[n], out_d[n], out_m[n], out_v[n] = [r.reshape(shp) for r in res]
    small_shapes = [W[n].shape for n in SMALL]
    packed = [_pack([d[n] for n in SMALL]) for d in (W, G, M1, V2)]
    res = adamw(*packed, tr=packed[0].shape[0], name="adamw_small")
    for d, r in zip((out_g, out_d, out_m, out_v), res):
        for n, a in zip(SMALL, _unpack(r, small_shapes)):
            d[n] = a

    return (loss, grad_x, *[out_g[n] for n in WEIGHTS], *[out_d[n] for n in WEIGHTS],
            *[out_m[n] for n in WEIGHTS], *[out_v[n] for n in WEIGHTS])
```

```python
import functools

import jax
import jax.numpy as jnp
from jax import lax
from jax.experimental import pallas as pl
from jax.experimental.pallas import tpu as pltpu

F32 = jnp.float32
BF16 = jnp.bfloat16
MESH = pl.DeviceIdType.MESH

LN_EPS = 1e-5
SB_HEADS = 8
MEM_HEADS = 4
ADAM_LR, ADAM_B1, ADAM_B2, ADAM_EPS, ADAM_WD, ADAM_STEP = 0.001, 0.9, 0.999, 1e-08, 0.01, 10

LANES = 128
V7X_VMEM_BYTES = 64 << 20
VMEM_CAP = V7X_VMEM_BYTES - (6 << 20)
N_CHIPS = 4
N_DEV = 8

BIG = ('w_in', 'w_out', 'mem_wq', 'mem_wk', 'mem_wv', 'mem_wo', 'ffn_up', 'ffn_down')
RIDE_IN = ('w_in',)
RIDE_ATT = ('w_out', 'mem_wq', 'mem_wk', 'mem_wv', 'mem_wo')
RIDE_FFN = ('ffn_up', 'ffn_down')
RIDE_MIX = ('w_in', 'w_out')
RIDE_REST = ('mem_wq', 'mem_wk', 'mem_wv', 'mem_wo', 'ffn_up', 'ffn_down')
COL_SHARDED = ('w_in', 'ffn_up')
SMALL = ('conv_w', 'conv_b', 'conv_ln_g', 'conv_ln_b', 'ln1_g', 'ln1_b', 'ln2_g', 'ln2_b',
         'ffn_conv_w', 'ffn_conv_b', 'ln3_g', 'ln3_b')
SMALL_SHARDED = ('conv_w', 'ffn_conv_w')
WEIGHTS = ('w_in', 'conv_w', 'conv_b', 'conv_ln_g', 'conv_ln_b', 'w_out', 'ln1_g', 'ln1_b',
           'mem_wq', 'mem_wk', 'mem_wv', 'mem_wo', 'ln2_g', 'ln2_b', 'ffn_up', 'ffn_conv_w',
           'ffn_conv_b', 'ffn_down', 'ln3_g', 'ln3_b')


def _params(block_bytes, semantics=None, **kw):
    limit = int(min(max(2 * block_bytes + (8 << 20), 32 << 20), VMEM_CAP))
    return pltpu.CompilerParams(dimension_semantics=semantics, vmem_limit_bytes=limit, **kw)


def _pallas(body, **kw):
    call = pl.pallas_call(body, **kw)

    def run(*args):
        return call(*[pltpu.with_memory_space_constraint(a, pltpu.HBM)
                      if jnp.issubdtype(a.dtype, jnp.floating) else a for a in args])

    return run


def _nbytes(shape, dtype):
    n = 1
    for s in shape:
        n *= s
    return n * jnp.dtype(dtype).itemsize


def _dot(a, b):
    return jnp.dot(a, b, preferred_element_type=F32)


def _dot_nt(a, b):
    return lax.dot_general(a, b, (((1,), (1,)), ((), ())), preferred_element_type=F32)


def _dot_tn(a, b):
    return lax.dot_general(a, b, (((0,), (0,)), ((), ())), preferred_element_type=F32)


def _sigmoid(x):
    return 1.0 / (1.0 + jnp.exp(-x))


def mm_nn(a, b, out_dtype, *, tm, tn, name):
    M, K = a.shape
    sharded = b.ndim == 3
    if sharded:
        nsh, _, ns = b.shape
        N, per = nsh * ns, ns // tn
        b_spec = pl.BlockSpec((None, K, tn), lambda i, j: (j // per, 0, j % per))
    else:
        N = b.shape[1]
        b_spec = pl.BlockSpec((K, tn), lambda i, j: (0, j))

    def body(a_ref, b_ref, o_ref):
        o_ref[...] = _dot(a_ref[...].astype(BF16), b_ref[...]).astype(o_ref.dtype)

    blk = _nbytes((tm, K), a.dtype) + _nbytes((K, tn), BF16) + _nbytes((tm, tn), out_dtype)
    return _pallas(
        body, name=name, out_shape=pltpu.HBM((M, N), out_dtype), grid=(M // tm, N // tn),
        in_specs=[pl.BlockSpec((tm, K), lambda i, j: (i, 0)), b_spec],
        out_specs=pl.BlockSpec((tm, tn), lambda i, j: (i, j)),
        compiler_params=_params(blk, ("parallel", "parallel")))(a, b)


def mm_ln(a, b, x, gamma, beta, alpha, *, tm, name):
    M, K = a.shape
    D = b.shape[1]

    def body(a_ref, b_ref, x_ref, g_ref, be_ref, y_ref, yb_ref, zh_ref, rs_ref):
        z = alpha * x_ref[...] + _dot(a_ref[...], b_ref[...])
        mu = jnp.mean(z, axis=-1, keepdims=True)
        zc = z - mu
        rstd = lax.rsqrt(jnp.mean(zc * zc, axis=-1, keepdims=True) + LN_EPS)
        zh = zc * rstd
        y = zh * g_ref[...] + be_ref[...]
        y_ref[...] = y
        yb_ref[...] = y.astype(BF16)
        zh_ref[...] = zh
        rs_ref[...] = rstd

    row = lambda i: (i, 0)
    fix = lambda i: (0, 0)
    blk = _nbytes((tm, K), BF16) + _nbytes((K, D), BF16) + 4 * _nbytes((tm, D), F32)
    return _pallas(
        body, name=name, grid=(M // tm,),
        out_shape=(pltpu.HBM((M, D), F32), pltpu.HBM((M, D), BF16),
                   pltpu.HBM((M, D), F32), pltpu.HBM((M, 1), F32)),
        in_specs=[pl.BlockSpec((tm, K), row), pl.BlockSpec((K, D), fix), pl.BlockSpec((tm, D), row),
                  pl.BlockSpec((1, D), fix), pl.BlockSpec((1, D), fix)],
        out_specs=(pl.BlockSpec((tm, D), row), pl.BlockSpec((tm, D), row), pl.BlockSpec((tm, D), row),
                   pl.BlockSpec((tm, 1), row)),
        compiler_params=_params(blk, ("parallel",)))(a, b, x, gamma, beta)


def ln_bwd(dy, zh, rstd, gamma, *, tm, name):
    M, D = dy.shape

    def body(dy_ref, zh_ref, rs_ref, g_ref, dz_ref, dzb_ref, dg_ref, db_ref):
        @pl.when(pl.program_id(0) == 0)
        def _():
            dg_ref[...] = jnp.zeros_like(dg_ref)
            db_ref[...] = jnp.zeros_like(db_ref)

        dyv, zhv = dy_ref[...], zh_ref[...]
        dg_ref[...] += jnp.sum(dyv * zhv, axis=0, keepdims=True)
        db_ref[...] += jnp.sum(dyv, axis=0, keepdims=True)
        dzh = dyv * g_ref[...]
        m1 = jnp.mean(dzh, axis=-1, keepdims=True)
        m2 = jnp.mean(dzh * zhv, axis=-1, keepdims=True)
        dz = rs_ref[...] * (dzh - m1 - zhv * m2)
        dz_ref[...] = dz
        dzb_ref[...] = dz.astype(BF16)

    row = lambda i: (i, 0)
    fix = lambda i: (0, 0)
    return _pallas(
        body, name=name, grid=(M // tm,),
        out_shape=(pltpu.HBM((M, D), F32), pltpu.HBM((M, D), BF16),
                   pltpu.HBM((1, D), F32), pltpu.HBM((1, D), F32)),
        in_specs=[pl.BlockSpec((tm, D), row), pl.BlockSpec((tm, D), row), pl.BlockSpec((tm, 1), row),
                  pl.BlockSpec((1, D), fix)],
        out_specs=(pl.BlockSpec((tm, D), row), pl.BlockSpec((tm, D), row), pl.BlockSpec((1, D), fix),
                   pl.BlockSpec((1, D), fix)),
        compiler_params=_params(4 * _nbytes((tm, D), F32), ("arbitrary",)))(dy, zh, rstd, gamma)


def mm_nt(a_list, b, out_dtype, *, tm, tk, name, res=None, alpha=None):
    M = a_list[0].shape[0]
    widths = [a.shape[1] for a in a_list]
    sharded = b.ndim == 3
    if sharded:
        nsh, K, ns = b.shape
        b_spec = pl.BlockSpec((nsh, tk, ns), lambda i, j: (0, j, 0))
        for w in widths:
            assert w % ns == 0
    else:
        K, N = b.shape
        ns = None
        b_spec = pl.BlockSpec((tk, N), lambda i, j: (j, 0))
    n_a = len(a_list)

    def body(*refs):
        a_refs, b_ref = refs[:n_a], refs[n_a]
        o_ref = refs[-1]
        acc = None
        off = 0
        for a_ref, w in zip(a_refs, widths):
            if sharded:
                for p in range(w // ns):
                    t = _dot_nt(a_ref[:, p * ns:(p + 1) * ns].astype(BF16), b_ref[off // ns + p])
                    acc = t if acc is None else acc + t
            else:
                t = _dot_nt(a_ref[...].astype(BF16), b_ref[:, off:off + w])
                acc = t if acc is None else acc + t
            off += w
        if res is not None:
            acc = acc + alpha * refs[n_a + 1][...]
        o_ref[...] = acc.astype(o_ref.dtype)

    in_specs = [pl.BlockSpec((tm, w), lambda i, j: (i, 0)) for w in widths] + [b_spec]
    args = list(a_list) + [b]
    if res is not None:
        in_specs.append(pl.BlockSpec((tm, tk), lambda i, j: (i, j)))
        args.append(res)
    blk = (sum(_nbytes((tm, w), a.dtype) for a, w in zip(a_list, widths)) + _nbytes((tk, sum(widths)), BF16)
           + 2 * _nbytes((tm, tk), F32))
    return _pallas(
        body, name=name, out_shape=pltpu.HBM((M, K), out_dtype), grid=(M // tm, K // tk),
        in_specs=in_specs, out_specs=pl.BlockSpec((tm, tk), lambda i, j: (i, j)),
        compiler_params=_params(blk, ("parallel", "parallel")))(*args)


def mm_nt_ln_bwd(a_list, b, res, alpha, zh, rstd, gamma, *, tm, name):
    M, D = res.shape
    widths = [a.shape[1] for a in a_list]
    sharded = b.ndim == 3
    if sharded:
        nsh, _, ns = b.shape
        b_spec = pl.BlockSpec((nsh, D, ns), lambda i: (0, 0, 0))
    else:
        ns = None
        b_spec = pl.BlockSpec((D, b.shape[1]), lambda i: (0, 0))
    n_a = len(a_list)

    def body(*refs):
        a_refs, b_ref = refs[:n_a], refs[n_a]
        res_ref, zh_ref, rs_ref, g_ref = refs[n_a + 1:n_a + 5]
        dz_ref, dzb_ref, dg_ref, db_ref = refs[n_a + 5:]

        @pl.when(pl.program_id(0) == 0)
        def _():
            dg_ref[...] = jnp.zeros_like(dg_ref)
            db_ref[...] = jnp.zeros_like(db_ref)

        dy = alpha * res_ref[...]
        off = 0
        for a_ref, w in zip(a_refs, widths):
            if sharded:
                for p in range(w // ns):
                    dy = dy + _dot_nt(a_ref[:, p * ns:(p + 1) * ns], b_ref[off // ns + p])
            else:
                dy = dy + _dot_nt(a_ref[...], b_ref[:, off:off + w])
            off += w
        zhv = zh_ref[...]
        dg_ref[...] += jnp.sum(dy * zhv, axis=0, keepdims=True)
        db_ref[...] += jnp.sum(dy, axis=0, keepdims=True)
        dzh = dy * g_ref[...]
        m1 = jnp.mean(dzh, axis=-1, keepdims=True)
        m2 = jnp.mean(dzh * zhv, axis=-1, keepdims=True)
        dz = rs_ref[...] * (dzh - m1 - zhv * m2)
        dz_ref[...] = dz
        dzb_ref[...] = dz.astype(BF16)

    row = lambda i: (i, 0)
    fix = lambda i: (0, 0)
    in_specs = [pl.BlockSpec((tm, w), row) for w in widths] + [
        b_spec, pl.BlockSpec((tm, D), row), pl.BlockSpec((tm, D), row), pl.BlockSpec((tm, 1), row),
        pl.BlockSpec((1, D), fix)]
    blk = (sum(_nbytes((tm, w), BF16) for w in widths) + _nbytes((D, sum(widths)), BF16)
           + 5 * _nbytes((tm, D), F32))
    return _pallas(
        body, name=name, grid=(M // tm,),
        out_shape=(pltpu.HBM((M, D), F32), pltpu.HBM((M, D), BF16), pltpu.HBM((1, D), F32),
                   pltpu.HBM((1, D), F32)),
        in_specs=in_specs,
        out_specs=(pl.BlockSpec((tm, D), row), pl.BlockSpec((tm, D), row), pl.BlockSpec((1, D), fix),
                   pl.BlockSpec((1, D), fix)),
        compiler_params=_params(blk, ("arbitrary",)))(*a_list, b, res, zh, rstd, gamma)


def mm_tn(a, b_list, *, tk, tn, name, shard_width=None, tmc=None):
    M, K = a.shape
    tmc = M if tmc is None else tmc
    nm = M // tmc
    widths = [b.shape[1] for b in b_list]
    N = sum(widths)
    starts, s = [], 0
    for w in widths:
        assert w % tn == 0
        starts.append(s)
        s += w // tn
    n_b = len(b_list)

    def body(*refs):
        a_ref, b_refs, o_ref, acc = refs[0], refs[1:1 + n_b], refs[-2], refs[-1]
        j, m = pl.program_id(1), pl.program_id(2)
        for b_ref, st, w in zip(b_refs, starts, widths):
            @pl.when((j >= st) & (j < st + w // tn))
            def _(b_ref=b_ref):
                t = _dot_tn(a_ref[...].astype(BF16), b_ref[...].astype(BF16))
                if nm == 1:
                    o_ref[...] = t.astype(o_ref.dtype)
                else:
                    @pl.when(m == 0)
                    def _():
                        acc[...] = t

                    @pl.when(m > 0)
                    def _():
                        acc[...] += t

                    @pl.when(m == nm - 1)
                    def _():
                        o_ref[...] = acc[...].astype(o_ref.dtype)

    def b_map(st, w):
        nb = w // tn
        return lambda i, j, m: (jnp.where((j >= st) & (j < st + nb), m, 0), jnp.clip(j - st, 0, nb - 1))

    in_specs = [pl.BlockSpec((tmc, tk), lambda i, j, m: (m, i))]
    in_specs += [pl.BlockSpec((tmc, tn), b_map(st, w)) for st, w in zip(starts, widths)]
    if shard_width is None:
        out_shape = pltpu.HBM((K, N), BF16)
        out_spec = pl.BlockSpec((tk, tn), lambda i, j, m: (i, j))
    else:
        per = shard_width // tn
        out_shape = pltpu.HBM((N // shard_width, K, shard_width), BF16)
        out_spec = pl.BlockSpec((None, tk, tn), lambda i, j, m: (j // per, i, j % per))
    acc_shape = (tk, tn) if nm > 1 else (8, LANES)
    blk = (_nbytes((tmc, tk), a.dtype) + n_b * _nbytes((tmc, tn), b_list[0].dtype) + 2 * _nbytes((tk, tn), F32))
    return _pallas(
        body, name=name, out_shape=out_shape, grid=(K // tk, N // tn, nm), in_specs=in_specs, out_specs=out_spec,
        scratch_shapes=[pltpu.VMEM(acc_shape, F32)],
        compiler_params=_params(blk, ("parallel", "arbitrary", "arbitrary")))(a, *b_list)


CONV_PAD = 32
CONV_CHUNK = 128


def _rows(win, off, n, shifts):
    b, a = off % 8, off // 8
    if b not in shifts:
        shifts[b] = win if b == 0 else win[b:b + n + CONV_PAD - 8, :]
    return shifts[b][8 * a:8 * a + n, :]


def _by_residue(n_taps, offset):
    return sorted(range(n_taps), key=lambda k: (offset(k) % 8, k))


def conv_fwd(proj, conv_w, conv_b, *, name, rider=None):
    S = proj.shape[0]
    KW, C = conv_w.shape
    nct = C // LANES
    rc = min(CONV_CHUNK, S)

    def body(a_ref, g_ref, w_ref, b_ref, o_ref, pad):
        pad[0:CONV_PAD, :] = jnp.zeros((CONV_PAD, LANES), F32)
        pad[CONV_PAD:, :] = a_ref[...] * _sigmoid(g_ref[...])
        w = w_ref[...]
        bias = b_ref[...]

        def chunk(i, _):
            base = pl.multiple_of(i * rc, rc)
            win = pad[pl.ds(base, rc + CONV_PAD), :]
            acc = jnp.zeros((rc, LANES), F32) + bias
            shifts = {}
            for k in _by_residue(KW, lambda k: CONV_PAD - (KW - 1) + k):
                acc = acc + w[k:k + 1, :] * _rows(win, CONV_PAD - (KW - 1) + k, rc, shifts)
            o_ref[pl.ds(base, rc), :] = acc
            return 0

        lax.fori_loop(0, S // rc, chunk, 0)

    in_specs, out_specs, out_shape, scratch = _carry_specs(
        rider, [pl.BlockSpec((S, LANES), lambda c: (0, c)), pl.BlockSpec((S, LANES), lambda c: (0, c + nct)),
                pl.BlockSpec((KW, LANES), lambda c: (0, c)), pl.BlockSpec((1, LANES), lambda c: (0, c))],
        (pl.BlockSpec((S, LANES), lambda c: (0, c)),), (pltpu.HBM((S, C), F32),),
        [pltpu.VMEM((S + CONV_PAD, LANES), F32)])
    first = lambda: pl.program_id(0) == 0
    last = lambda: pl.program_id(0) == nct - 1
    res = _pallas(
        _carry(rider, body, 4, 1, first, last), name=name, grid=(nct,), out_shape=out_shape,
        in_specs=in_specs, out_specs=out_specs, scratch_shapes=scratch,
        compiler_params=_params(4 * _nbytes((S, LANES), F32), ("arbitrary",)))(
            proj, proj, conv_w, conv_b, *(rider.arrays if rider else ()))
    return res[0], list(res[1:])


def conv_bwd(du1, proj, conv_w, *, name):
    S = proj.shape[0]
    KW, C = conv_w.shape
    nct = C // LANES
    rc = min(CONV_CHUNK, S)

    def body(d_ref, a_ref, g_ref, w_ref, da_ref, dg_ref, dw_ref, db_ref, pad_u, pad_d, du0, dw_acc):
        dw_acc[...] = jnp.zeros_like(dw_acc)
        pad_u[0:CONV_PAD, :] = jnp.zeros((CONV_PAD, LANES), F32)
        pad_u[CONV_PAD:, :] = a_ref[...] * _sigmoid(g_ref[...])
        pad_d[0:S, :] = d_ref[...]
        pad_d[S:, :] = jnp.zeros((CONV_PAD, LANES), F32)
        w = w_ref[...]
        db_ref[...] = jnp.sum(d_ref[...], axis=0, keepdims=True)

        def chunk(i, _):
            base = pl.multiple_of(i * rc, rc)
            d = pad_d[pl.ds(base, rc), :]
            win_u = pad_u[pl.ds(base, rc + CONV_PAD), :]
            win_d = pad_d[pl.ds(base, rc + CONV_PAD), :]
            shifts = {}
            for k in _by_residue(KW, lambda k: CONV_PAD - (KW - 1) + k):
                u_k = _rows(win_u, CONV_PAD - (KW - 1) + k, rc, shifts)
                dw_acc[k:k + 1, :] += jnp.sum(d * u_k, axis=0, keepdims=True)
            acc = jnp.zeros((rc, LANES), F32)
            shifts = {}
            for k in _by_residue(KW, lambda k: KW - 1 - k):
                acc = acc + w[k:k + 1, :] * _rows(win_d, KW - 1 - k, rc, shifts)
            du0[pl.ds(base, rc), :] = acc
            return 0

        lax.fori_loop(0, S // rc, chunk, 0)
        dw_ref[...] = dw_acc[0:KW, :]
        a, sg = a_ref[...], _sigmoid(g_ref[...])
        d0 = du0[...]
        da_ref[...] = (d0 * sg).astype(BF16)
        dg_ref[...] = (d0 * a * sg * (1.0 - sg)).astype(BF16)

    col = lambda c: (0, c)
    return _pallas(
        body, name=name, grid=(nct,),
        out_shape=(pltpu.HBM((S, C), BF16), pltpu.HBM((S, C), BF16),
                   pltpu.HBM((KW, C), F32), pltpu.HBM((1, C), F32)),
        in_specs=[pl.BlockSpec((S, LANES), col), pl.BlockSpec((S, LANES), col),
                  pl.BlockSpec((S, LANES), lambda c: (0, c + nct)), pl.BlockSpec((KW, LANES), col)],
        out_specs=(pl.BlockSpec((S, LANES), col), pl.BlockSpec((S, LANES), col), pl.BlockSpec((KW, LANES), col),
                   pl.BlockSpec((1, LANES), col)),
        scratch_shapes=[pltpu.VMEM((S + CONV_PAD, LANES), F32), pltpu.VMEM((S + CONV_PAD, LANES), F32),
                        pltpu.VMEM((S, LANES), F32), pltpu.VMEM((CONV_PAD, LANES), F32)],
        compiler_params=_params(8 * _nbytes((S, LANES), F32), ("parallel",)))(du1, proj, proj, conv_w)


def ln_silu(u1, o_sb, gamma, beta, *, tm, name):
    S, C = u1.shape

    def body(u_ref, o_ref, g_ref, b_ref, out_ref):
        z = u_ref[...]
        mu = jnp.mean(z, axis=-1, keepdims=True)
        zc = z - mu
        y = zc * lax.rsqrt(jnp.mean(zc * zc, axis=-1, keepdims=True) + LN_EPS) * g_ref[...] + b_ref[...]
        out_ref[:, 0:C] = (y * _sigmoid(y)).astype(BF16)
        out_ref[:, C:] = o_ref[...].astype(BF16)

    row = lambda i: (i, 0)
    fix = lambda i: (0, 0)
    return _pallas(
        body, name=name, out_shape=pltpu.HBM((S, 2 * C), BF16), grid=(S // tm,),
        in_specs=[pl.BlockSpec((tm, C), row), pl.BlockSpec((tm, C), row), pl.BlockSpec((1, C), fix),
                  pl.BlockSpec((1, C), fix)],
        out_specs=pl.BlockSpec((tm, 2 * C), row),
        compiler_params=_params(4 * _nbytes((tm, C), F32), ("parallel",)))(u1, o_sb, gamma, beta)


def ln_silu_bwd(dua, u1, gamma, beta, *, tm, name):
    S, C = u1.shape

    def body(d_ref, u_ref, g_ref, b_ref, du1_ref, dg_ref, db_ref):
        @pl.when(pl.program_id(0) == 0)
        def _():
            dg_ref[...] = jnp.zeros_like(dg_ref)
            db_ref[...] = jnp.zeros_like(db_ref)

        z = u_ref[...]
        mu = jnp.mean(z, axis=-1, keepdims=True)
        zc = z - mu
        rstd = lax.rsqrt(jnp.mean(zc * zc, axis=-1, keepdims=True) + LN_EPS)
        zh = zc * rstd
        y = zh * g_ref[...] + b_ref[...]
        sg = _sigmoid(y)
        dy = d_ref[...] * (sg * (1.0 + y * (1.0 - sg)))
        dg_ref[...] += jnp.sum(dy * zh, axis=0, keepdims=True)
        db_ref[...] += jnp.sum(dy, axis=0, keepdims=True)
        dzh = dy * g_ref[...]
        m1 = jnp.mean(dzh, axis=-1, keepdims=True)
        m2 = jnp.mean(dzh * zh, axis=-1, keepdims=True)
        du1_ref[...] = rstd * (dzh - m1 - zh * m2)

    row = lambda i: (i, 0)
    fix = lambda i: (0, 0)
    return _pallas(
        body, name=name, grid=(S // tm,),
        out_shape=(pltpu.HBM((S, C), F32), pltpu.HBM((1, C), F32),
                   pltpu.HBM((1, C), F32)),
        in_specs=[pl.BlockSpec((tm, C), row), pl.BlockSpec((tm, C), row), pl.BlockSpec((1, C), fix),
                  pl.BlockSpec((1, C), fix)],
        out_specs=(pl.BlockSpec((tm, C), row), pl.BlockSpec((1, C), fix), pl.BlockSpec((1, C), fix)),
        compiler_params=_params(4 * _nbytes((tm, C), F32), ("arbitrary",)))(dua, u1, gamma, beta)


SB_BLOCK = 256
SB_STOP = -105.0


def _split_dot(x, tri):
    hi = x.astype(BF16)
    lo = (x - hi.astype(F32)).astype(BF16)
    return _dot(hi, tri) + _dot(lo, tri)


def _neg_softplus(z):
    return -(jnp.maximum(z, 0.0) + jnp.log(1.0 + jnp.exp(-jnp.abs(z))))


def sb_fwd(proj, *, q_col, name, rider=None):
    S = proj.shape[0]
    dh = LANES // 2
    W = SB_HEADS * dh
    npair = W // LANES
    T = min(SB_BLOCK, S)
    nblk = S // T
    scale = dh ** -0.5
    qb0 = q_col // LANES

    def body(q_ref, k_ref, v_ref, o_ref, l_ref, qs, ks, vs):
        r_i = lax.broadcasted_iota(jnp.int32, (T, T), 0)
        c_i = lax.broadcasted_iota(jnp.int32, (T, T), 1)
        tri = (r_i >= c_i).astype(BF16)
        vis = c_i < r_i
        lane = lax.broadcasted_iota(jnp.int32, (T, dh), 1)

        for hh in range(2):
            sl = slice(hh * dh, (hh + 1) * dh)
            qs[hh] = (q_ref[:, sl] * scale).astype(BF16)
            ks[hh] = k_ref[:, sl].astype(BF16)
            vs[hh] = v_ref[:, sl].astype(BF16)

        def step(qb, j0, diag, st):
            two = range(2)
            kb = [ks[hh, pl.ds(j0, T), :] for hh in two]
            vb = [vs[hh, pl.ds(j0, T), :] for hh in two]
            z = [_dot_nt(qb[hh], kb[hh]) for hh in two]
            lk = [_neg_softplus(z[hh]) for hh in two]
            if diag:
                lk = [jnp.where(vis, lk[hh], 0.0) for hh in two]
            C = [_split_dot(lk[hh], tri) for hh in two]
            A = [jnp.exp(z[hh] + C[hh] + st[2 * hh + 1]) for hh in two]
            if diag:
                A = [jnp.where(vis, A[hh], 0.0) for hh in two]
            acc = [st[2 * hh] + _dot(A[hh].astype(BF16), vb[hh]) for hh in two]
            return (acc[0], st[1] + C[0][:, 0:1], acc[1], st[3] + C[1][:, 0:1])

        def qblock(i, _):
            r0 = pl.multiple_of(i * T, T)
            qb = [qs[hh, pl.ds(r0, T), :] for hh in range(2)]
            zero = (jnp.zeros((T, dh), F32), jnp.zeros((T, 1), F32))
            state = step(qb, r0, True, zero + zero)

            def more(c):
                return (c[0] >= 0) & (jnp.max(jnp.maximum(c[2], c[4])) >= SB_STOP)

            def walk(c):
                return (c[0] - 1,) + step(qb, pl.multiple_of(c[0] * T, T), False, c[1:])

            c = lax.while_loop(more, walk, (i - 1,) + state)
            walked = (i - c[0]).astype(F32)
            for hh in range(2):
                sl = slice(hh * dh, (hh + 1) * dh)
                o_ref[pl.ds(r0, T), sl] = c[1 + 2 * hh]
                l_ref[pl.ds(r0, T), sl] = jnp.where(lane == 1, walked, c[2 + 2 * hh])
            return 0

        lax.fori_loop(0, nblk, qblock, 0)

    blk = lambda off: pl.BlockSpec((S, LANES), lambda h: (0, qb0 + off * npair + h))
    out = pl.BlockSpec((S, LANES), lambda h: (0, h))
    in_specs, out_specs, out_shape, scratch = _carry_specs(
        rider, [blk(0), blk(1), blk(2)], (out, out), (pltpu.HBM((S, W), F32), pltpu.HBM((S, W), F32)),
        [pltpu.VMEM((2, S, dh), BF16)] * 3)
    first = lambda: pl.program_id(0) == 0
    last = lambda: pl.program_id(0) == npair - 1
    res = _pallas(
        _carry(rider, body, 3, 2, first, last), name=name, grid=(npair,), out_shape=out_shape,
        in_specs=in_specs, out_specs=out_specs, scratch_shapes=scratch,
        compiler_params=_params(6 * _nbytes((S, LANES), F32), ("arbitrary",)))(
            proj, proj, proj, *(rider.arrays if rider else ()))
    return res[0], res[1], list(res[2:])


def sb_bwd(proj, ltot, dua, *, q_col, do_col, name, rider=None):
    S = proj.shape[0]
    dh = LANES // 2
    W = SB_HEADS * dh
    npair = W // LANES
    T = min(SB_BLOCK, S)
    nblk = S // T
    scale = dh ** -0.5
    qb0 = q_col // LANES
    db0 = do_col // LANES

    def body(q_ref, k_ref, v_ref, l_ref, do_ref, dq_ref, dk_ref, dv_ref, qs, ks, vs, dos, dks, dvs):
        r_i = lax.broadcasted_iota(jnp.int32, (T, T), 0)
        c_i = lax.broadcasted_iota(jnp.int32, (T, T), 1)
        tri_rev = (r_i >= c_i).astype(BF16)
        tri_fwd = (r_i <= c_i).astype(BF16)
        vis = c_i < r_i

        for hh in range(2):
            sl = slice(hh * dh, (hh + 1) * dh)
            qs[hh] = (q_ref[:, sl] * scale).astype(BF16)
            ks[hh] = k_ref[:, sl].astype(BF16)
            vs[hh] = v_ref[:, sl].astype(BF16)
            dos[hh] = do_ref[:, sl].astype(BF16)
        dks[...] = jnp.zeros_like(dks)
        dvs[...] = jnp.zeros_like(dvs)

        def step(qb, dob, Lt, j0, diag, st):
            two = range(2)
            kb = [ks[hh, pl.ds(j0, T), :] for hh in two]
            vb = [vs[hh, pl.ds(j0, T), :] for hh in two]
            z = [_dot_nt(qb[hh], kb[hh]) for hh in two]
            dA = [_dot_nt(dob[hh], vb[hh]) for hh in two]
            lk = [_neg_softplus(z[hh]) for hh in two]
            beta = [jnp.exp(z[hh] + lk[hh]) for hh in two]
            if diag:
                lk = [jnp.where(vis, lk[hh], 0.0) for hh in two]
            C = [_split_dot(lk[hh], tri_rev) for hh in two]
            rowsum = [C[hh][:, 0:1] for hh in two]
            A = [jnp.exp(z[hh] + C[hh] + (Lt[hh] - st[3 * hh + 1] - rowsum[hh])) for hh in two]
            if diag:
                A = [jnp.where(vis, A[hh], 0.0) for hh in two]
            g = [A[hh] * dA[hh] for hh in two]
            Gin = [_split_dot(g[hh], tri_fwd) for hh in two]
            dz = [g[hh] - beta[hh] * (st[3 * hh + 2] + Gin[hh]) for hh in two]
            if diag:
                dz = [jnp.where(vis, dz[hh], 0.0) for hh in two]
            dzb = [dz[hh].astype(BF16) for hh in two]
            out = ()
            for hh in two:
                dvs[hh, pl.ds(j0, T), :] += _dot_tn(A[hh].astype(BF16), dob[hh])
                dks[hh, pl.ds(j0, T), :] += _dot_tn(dzb[hh], qb[hh])
                out += (st[3 * hh] + _dot(dzb[hh], kb[hh]), st[3 * hh + 1] + rowsum[hh],
                        st[3 * hh + 2] + Gin[hh][:, T - 1:T])
            return out

        def qblock(i, _):
            r0 = pl.multiple_of(i * T, T)
            qb = [qs[hh, pl.ds(r0, T), :] for hh in range(2)]
            dob = [dos[hh, pl.ds(r0, T), :] for hh in range(2)]
            Lt = [l_ref[pl.ds(r0, T), hh * dh:hh * dh + 1] for hh in range(2)]
            walked = jnp.clip(jnp.max(l_ref[pl.ds(r0, 8), 1:2]).astype(jnp.int32), 1, i + 1)

            def inner(j, c):
                return step(qb, dob, Lt, pl.multiple_of(j * T, T), False, c)

            zero = jnp.zeros((T, 1), F32)
            init = (jnp.zeros((T, dh), F32), zero, zero)
            c = lax.fori_loop(i + 1 - walked, i, inner, init + init)
            c = step(qb, dob, Lt, r0, True, c)
            for hh in range(2):
                dq_ref[pl.ds(r0, T), hh * dh:(hh + 1) * dh] = (c[3 * hh] * scale).astype(BF16)
            return 0

        lax.fori_loop(0, nblk, qblock, 0)
        for hh in range(2):
            sl = slice(hh * dh, (hh + 1) * dh)
            dk_ref[:, sl] = dks[hh].astype(BF16)
            dv_ref[:, sl] = dvs[hh].astype(BF16)

    blk = lambda off: pl.BlockSpec((S, LANES), lambda h: (0, qb0 + off * npair + h))
    out = pl.BlockSpec((S, LANES), lambda h: (0, h))
    o_shape = pltpu.HBM((S, W), BF16)
    in_specs, out_specs, out_shape, scratch = _carry_specs(
        rider, [blk(0), blk(1), blk(2), out, pl.BlockSpec((S, LANES), lambda h: (0, db0 + h))], (out, out, out),
        (o_shape, o_shape, o_shape), [pltpu.VMEM((2, S, dh), BF16)] * 4 + [pltpu.VMEM((2, S, dh), F32)] * 2)
    first = lambda: pl.program_id(0) == 0
    last = lambda: pl.program_id(0) == npair - 1
    res = _pallas(
        _carry(rider, body, 5, 3, first, last), name=name, grid=(npair,), out_shape=out_shape,
        in_specs=in_specs, out_specs=out_specs, scratch_shapes=scratch,
        compiler_params=_params(12 * _nbytes((S, LANES), F32), ("arbitrary",)))(
            proj, proj, proj, ltot, dua, *(rider.arrays if rider else ()))
    return res[0], res[1], res[2], list(res[3:])


def xattn_fwd(q, k, v, *, tm, name):
    S, D = q.shape
    Mlen = k.shape[0]
    hd = D // MEM_HEADS
    scale = hd ** -0.5

    def body(q_ref, k_ref, v_ref, o_ref):
        for h in range(MEM_HEADS):
            sl = slice(h * hd, (h + 1) * hd)
            s = _dot_nt(q_ref[:, sl], k_ref[:, sl]) * scale
            e = jnp.exp(s - jnp.max(s, axis=-1, keepdims=True))
            p = e / jnp.sum(e, axis=-1, keepdims=True)
            o_ref[:, sl] = _dot(p.astype(BF16), v_ref[:, sl]).astype(BF16)

    row = lambda i: (i, 0)
    fix = lambda i: (0, 0)
    return _pallas(
        body, name=name, out_shape=pltpu.HBM((S, D), BF16), grid=(S // tm,),
        in_specs=[pl.BlockSpec((tm, D), row), pl.BlockSpec((Mlen, D), fix), pl.BlockSpec((Mlen, D), fix)],
        out_specs=pl.BlockSpec((tm, D), row),
        compiler_params=_params(4 * _nbytes((tm, D), F32), ("parallel",)))(q, k, v)


def xattn_bwd(q, do, k, v, *, tm, name):
    S, D = q.shape
    Mlen = k.shape[0]
    hd = D // MEM_HEADS
    scale = hd ** -0.5

    def body(q_ref, do_ref, k_ref, v_ref, dq_ref, dk_ref, dv_ref):
        @pl.when(pl.program_id(0) == 0)
        def _():
            dk_ref[...] = jnp.zeros_like(dk_ref)
            dv_ref[...] = jnp.zeros_like(dv_ref)

        for h in range(MEM_HEADS):
            sl = slice(h * hd, (h + 1) * hd)
            qh, doh, kh, vh = q_ref[:, sl], do_ref[:, sl], k_ref[:, sl], v_ref[:, sl]
            s = _dot_nt(qh, kh) * scale
            e = jnp.exp(s - jnp.max(s, axis=-1, keepdims=True))
            p = e / jnp.sum(e, axis=-1, keepdims=True)
            dp = _dot_nt(doh, vh)
            ds = (p * (dp - jnp.sum(p * dp, axis=-1, keepdims=True)) * scale).astype(BF16)
            dq_ref[:, sl] = _dot(ds, kh).astype(BF16)
            dk_ref[:, sl] += _dot_tn(ds, qh)
            dv_ref[:, sl] += _dot_tn(p.astype(BF16), doh)

    row = lambda i: (i, 0)
    fix = lambda i: (0, 0)
    return _pallas(
        body, name=name, grid=(S // tm,),
        out_shape=(pltpu.HBM((S, D), BF16), pltpu.HBM((Mlen, D), F32),
                   pltpu.HBM((Mlen, D), F32)),
        in_specs=[pl.BlockSpec((tm, D), row), pl.BlockSpec((tm, D), row), pl.BlockSpec((Mlen, D), fix),
                  pl.BlockSpec((Mlen, D), fix)],
        out_specs=(pl.BlockSpec((tm, D), row), pl.BlockSpec((Mlen, D), fix), pl.BlockSpec((Mlen, D), fix)),
        compiler_params=_params(6 * _nbytes((tm, D), F32), ("arbitrary",)))(q, do, k, v)


FFN_HALO = 8


def _conv3(ext, w, lo):
    tm = ext.shape[0] - FFN_HALO
    return (w[0:1, :] * ext[lo:lo + tm, :] + w[1:2, :] * ext[lo + 1:lo + 1 + tm, :]
            + w[2:3, :] * ext[lo + 2:lo + 2 + tm, :])


def ffn_up_fwd(xb, w_up, conv_w, conv_b, *, tm, tn, name, rider=None):
    S, D = xb.shape
    nsh, _, ns = w_up.shape
    F = nsh * ns // 2
    per = ns // tn
    ncol = F // tn
    KW = conv_w.shape[0]
    assert KW == 3

    def body(x_ref, wv_ref, wg_ref, cwv_ref, cwg_ref, cbv_ref, cbg_ref, uv_ref, ug_ref, h_ref, carry):
        @pl.when(pl.program_id(1) == 0)
        def _():
            carry[...] = jnp.zeros_like(carry)

        x = x_ref[...]
        uv = _dot(x, wv_ref[...])
        ug = _dot(x, wg_ref[...])
        uv_ref[...] = uv.astype(BF16)
        ug_ref[...] = ug.astype(BF16)
        lo = FFN_HALO - (KW - 1)
        cv = _conv3(jnp.concatenate([carry[0], uv], axis=0), cwv_ref[...], lo) + cbv_ref[...]
        cg = _conv3(jnp.concatenate([carry[1], ug], axis=0), cwg_ref[...], lo) + cbg_ref[...]
        carry[0] = uv[tm - FFN_HALO:, :]
        carry[1] = ug[tm - FFN_HALO:, :]
        h_ref[...] = (cg * _sigmoid(cg) * cv).astype(BF16)

    wspec = lambda half: pl.BlockSpec((None, D, tn), lambda j, i: (half * (nsh // 2) + j // per, 0, j % per))
    cspec = lambda rows, half: pl.BlockSpec((rows, tn), lambda j, i: (0, half * ncol + j))
    out = pl.BlockSpec((tm, tn), lambda j, i: (i, j))
    o_shape = pltpu.HBM((S, F), BF16)
    blk = _nbytes((tm, D), BF16) + 2 * _nbytes((D, tn), BF16) + 8 * _nbytes((tm, tn), F32)
    nrow = S // tm
    in_specs, out_specs, out_shape, scratch = _carry_specs(
        rider, [pl.BlockSpec((tm, D), lambda j, i: (i, 0)), wspec(0), wspec(1), cspec(KW, 0), cspec(KW, 1),
                cspec(1, 0), cspec(1, 1)], (out, out, out), (o_shape, o_shape, o_shape),
        [pltpu.VMEM((2, FFN_HALO, tn), F32)])
    first = lambda: (pl.program_id(0) == 0) & (pl.program_id(1) == 0)
    last = lambda: (pl.program_id(0) == ncol - 1) & (pl.program_id(1) == nrow - 1)
    res = _pallas(
        _carry(rider, body, 7, 3, first, last), name=name, grid=(ncol, nrow), out_shape=out_shape,
        in_specs=in_specs, out_specs=out_specs, scratch_shapes=scratch,
        compiler_params=_params(blk, ("arbitrary", "arbitrary")))(
            xb, w_up, w_up, conv_w, conv_w, conv_b, conv_b, *(rider.arrays if rider else ()))
    return res[0], res[1], res[2], list(res[3:])


def ffn_mid_bwd(dzb, w_down, up_v, up_g, conv_w, conv_b, *, tm, tn, name, rider=None):
    S, D = dzb.shape
    F = up_v.shape[1]
    ncol = F // tn
    nrow = S // tm
    KW = conv_w.shape[0]
    assert KW == 3
    hb = tm // FFN_HALO

    def body(dz_ref, wd_ref, uv_ref, ug_ref, hv_ref, hg_ref, cwv_ref, cwg_ref, cbv_ref, cbg_ref,
             dv_ref, dg_ref, dwv_ref, dwg_ref, dbv_ref, dbg_ref, carry):
        i = pl.program_id(1)

        @pl.when(i == 0)
        def _():
            carry[...] = jnp.zeros_like(carry)
            for r in (dwv_ref, dwg_ref, dbv_ref, dbg_ref):
                r[...] = jnp.zeros_like(r)

        first = i == nrow - 1
        halo_v = jnp.where(first, 0.0, hv_ref[...].astype(F32))
        halo_g = jnp.where(first, 0.0, hg_ref[...].astype(F32))
        ext_v = jnp.concatenate([halo_v, uv_ref[...].astype(F32)], axis=0)
        ext_g = jnp.concatenate([halo_g, ug_ref[...].astype(F32)], axis=0)
        cwv, cwg = cwv_ref[...], cwg_ref[...]
        lo = FFN_HALO - (KW - 1)
        taps_v = [ext_v[lo + k:lo + k + tm, :] for k in range(KW)]
        taps_g = [ext_g[lo + k:lo + k + tm, :] for k in range(KW)]
        cv = cwv[0:1, :] * taps_v[0] + cwv[1:2, :] * taps_v[1] + cwv[2:3, :] * taps_v[2] + cbv_ref[...]
        cg = cwg[0:1, :] * taps_g[0] + cwg[1:2, :] * taps_g[1] + cwg[2:3, :] * taps_g[2] + cbg_ref[...]
        dh = _dot_nt(dz_ref[...], wd_ref[...])
        sg = _sigmoid(cg)
        dcv = dh * (cg * sg)
        dcg = dh * cv * (sg * (1.0 + cg * (1.0 - sg)))

        def back(dc, taps, cw, slot, du_ref, dw_ref, db_ref):
            ext2 = jnp.concatenate([dc, carry[slot]], axis=0)
            du = cw[2:3, :] * dc + cw[1:2, :] * ext2[1:tm + 1, :] + cw[0:1, :] * ext2[2:tm + 2, :]
            du_ref[...] = du.astype(BF16)
            carry[slot] = dc[0:FFN_HALO, :]
            for k in range(KW):
                dw_ref[k:k + 1, :] += jnp.sum(dc * taps[k], axis=0, keepdims=True)
            db_ref[...] += jnp.sum(dc, axis=0, keepdims=True)

        back(dcv, taps_v, cwv, 0, dv_ref, dwv_ref, dbv_ref)
        back(dcg, taps_g, cwg, 1, dg_ref, dwg_ref, dbg_ref)

    rev = lambda i: nrow - 1 - i
    tile = pl.BlockSpec((tm, tn), lambda j, i: (rev(i), j))
    halo = pl.BlockSpec((FFN_HALO, tn), lambda j, i: (jnp.maximum(rev(i) * hb - 1, 0), j))
    cspec = lambda rows, half: pl.BlockSpec((rows, tn), lambda j, i: (0, half * ncol + j))
    acc = lambda rows: pl.BlockSpec((rows, tn), lambda j, i: (0, j))
    big = pltpu.HBM((S, F), BF16)
    blk = _nbytes((tm, D), BF16) + _nbytes((tn, D), BF16) + 10 * _nbytes((tm, tn), F32)
    in_specs, out_specs, out_shape, scratch = _carry_specs(
        rider, [pl.BlockSpec((tm, D), lambda j, i: (rev(i), 0)), pl.BlockSpec((tn, D), lambda j, i: (j, 0)),
                tile, tile, halo, halo, cspec(KW, 0), cspec(KW, 1), cspec(1, 0), cspec(1, 1)],
        (tile, tile, acc(KW), acc(KW), acc(1), acc(1)),
        (big, big, pltpu.HBM((KW, F), F32), pltpu.HBM((KW, F), F32), pltpu.HBM((1, F), F32),
         pltpu.HBM((1, F), F32)), [pltpu.VMEM((2, FFN_HALO, tn), F32)])
    first = lambda: (pl.program_id(0) == 0) & (pl.program_id(1) == 0)
    last = lambda: (pl.program_id(0) == ncol - 1) & (pl.program_id(1) == nrow - 1)
    res = _pallas(
        _carry(rider, body, 10, 6, first, last), name=name, grid=(ncol, nrow), out_shape=out_shape,
        in_specs=in_specs, out_specs=out_specs, scratch_shapes=scratch,
        compiler_params=_params(blk, ("arbitrary", "arbitrary")))(
            dzb, w_down, up_v, up_g, up_v, up_g, conv_w, conv_w, conv_b, conv_b,
            *(rider.arrays if rider else ()))
    return res[:6], list(res[6:])


def loss_head(y, target, *, tm, name):
    S, D = y.shape

    def body(y_ref, t_ref, dy_ref, l_ref):
        @pl.when(pl.program_id(0) == 0)
        def _():
            l_ref[...] = jnp.zeros_like(l_ref)

        e = y_ref[...] - t_ref[...]
        dy_ref[...] = e * (1.0 / D)
        l_ref[...] += 0.5 * jnp.sum(jnp.mean(e * e, axis=-1, keepdims=True), axis=0, keepdims=True)

    row = lambda i: (i, 0)
    return _pallas(
        body, name=name, grid=(S // tm,),
        out_shape=(pltpu.HBM((S, D), F32), pltpu.HBM((1, 1), F32)),
        in_specs=[pl.BlockSpec((tm, D), row), pl.BlockSpec((tm, D), row)],
        out_specs=(pl.BlockSpec((tm, D), row), pl.BlockSpec((1, 1), lambda i: (0, 0))),
        compiler_params=_params(3 * _nbytes((tm, D), F32), ("arbitrary",)))(y, target)


def adamw(w, g, m, v, *, tr, name):
    R, C = w.shape
    c1 = 1.0 - ADAM_B1 ** ADAM_STEP
    c2 = 1.0 - ADAM_B2 ** ADAM_STEP

    def body(w_ref, g_ref, m_ref, v_ref, go_ref, d_ref, mo_ref, vo_ref):
        gv = g_ref[...]
        mn = ADAM_B1 * m_ref[...] + (1.0 - ADAM_B1) * gv
        vn = ADAM_B2 * v_ref[...] + (1.0 - ADAM_B2) * (gv * gv)
        go_ref[...] = gv
        mo_ref[...] = mn
        vo_ref[...] = vn
        d_ref[...] = -ADAM_LR * ((mn / c1) / (jnp.sqrt(vn / c2) + ADAM_EPS) + ADAM_WD * w_ref[...])

    spec = pl.BlockSpec((tr, C), lambda i: (i, 0))
    shape = pltpu.HBM((R, C), F32)
    return _pallas(
        body, name=name, grid=(R // tr,), out_shape=(shape,) * 4, in_specs=[spec] * 4, out_specs=(spec,) * 4,
        compiler_params=_params(8 * _nbytes((tr, C), F32), ("parallel",)))(w, g, m, v)


def add_pairs(gs, gots, core, *, name):
    k = len(gs)

    def body(c_ref, *refs):
        for a_ref, b_ref, o_ref in zip(refs[:k], refs[k:2 * k], refs[2 * k:]):
            o_ref[...] = (a_ref[...].astype(F32) + b_ref[...].astype(F32)).astype(BF16)

    own = [pl.BlockSpec((None, None) + g.shape[2:], lambda i, c: (i, c[0], 0, 0)) for g in gs]
    half = [pl.BlockSpec((None,) + g.shape[1:], lambda i, c: (i, 0, 0)) for g in gots]
    grid_spec = pltpu.PrefetchScalarGridSpec(
        num_scalar_prefetch=1, grid=(N_CHIPS,), in_specs=own + half, out_specs=tuple(half))
    blk = 3 * sum(_nbytes(g.shape[1:], BF16) for g in gots)
    return _pallas(
        body, name=name, grid_spec=grid_spec, out_shape=tuple(pltpu.HBM(g.shape, BF16) for g in gots),
        compiler_params=_params(blk, ("parallel",)))(core, *gs, *gots)


def sum_chips_into(bs, dests, layer, core, *, name):
    k = len(bs)
    steps = 2

    def body(c_ref, *refs):
        for b_ref, o_ref in zip(refs[:k], refs[2 * k:]):
            acc = b_ref[0].astype(F32)
            for p in range(1, N_CHIPS):
                acc = acc + b_ref[p].astype(F32)
            o_ref[...] = acc

    ins = [pl.BlockSpec((N_CHIPS, b.shape[1] // steps, b.shape[2]), lambda i, c: (0, i, 0)) for b in bs]
    outs = tuple(pl.BlockSpec((None, None, b.shape[1] // steps, b.shape[2]), lambda i, c: (layer, c[0], i, 0))
                 for b in bs)
    grid_spec = pltpu.PrefetchScalarGridSpec(
        num_scalar_prefetch=1, grid=(steps,), in_specs=ins + [pl.BlockSpec(memory_space=pl.ANY)] * k,
        out_specs=outs)
    blk = sum(_nbytes(b.shape, BF16) + _nbytes(b.shape[1:], F32) for b in bs) // steps
    return _pallas(
        body, name=name, grid_spec=grid_spec, out_shape=tuple(pltpu.HBM(d.shape, F32) for d in dests),
        input_output_aliases={1 + k + w: w for w in range(k)},
        compiler_params=_params(blk, ("parallel",)))(core, *bs, *dests)


_HBM = pl.BlockSpec(memory_space=pltpu.HBM)


def _place():
    x, y, c = lax.axis_index("x"), lax.axis_index("y"), lax.axis_index("c")
    chips = [(1 - x, y), (x, 1 - y), (1 - x, 1 - y)]
    return x, y, c, chips


class GatherRider:
    def __init__(self, shards):
        self.arrays = list(shards)
        self.n = n = len(shards)
        self.out_shape = tuple(pltpu.HBM((N_CHIPS,) + s.shape, s.dtype) for s in shards)
        self.scratch = [pltpu.SemaphoreType.DMA((n, 3))] * 4 + [pltpu.SemaphoreType.DMA((n,))]

    def _copies(self, ins, outs, sems):
        send_ici, recv_ici, send_d2d, recv_d2d, local = sems
        x, y, c, chips = _place()
        me = 2 * x + y

        def own(w):
            return pltpu.make_async_copy(ins[w], outs[w].at[me], local.at[w])

        def ici(w, j):
            px, py = chips[j]
            return pltpu.make_async_remote_copy(
                src_ref=ins[w].at[c], dst_ref=outs[w].at[me, c], send_sem=send_ici.at[w, j],
                recv_sem=recv_ici.at[w, j], device_id=(px, py, c), device_id_type=MESH)

        def landed(w, j, half):
            px, py = chips[j]
            return outs[w].at[2 * px + py, half]

        def d2d(w, j, half):
            return pltpu.make_async_remote_copy(
                src_ref=landed(w, j, half), dst_ref=landed(w, j, half), send_sem=send_d2d.at[w, j],
                recv_sem=recv_d2d.at[w, j], device_id=(x, y, 1 - c), device_id_type=MESH)

        def ici_arrival(w, j):
            return pltpu.make_async_remote_copy(
                src_ref=landed(w, j, c), dst_ref=landed(w, j, c), send_sem=send_ici.at[w, j],
                recv_sem=recv_ici.at[w, j], device_id=(x, y, c), device_id_type=MESH)

        return c, own, ici, d2d, ici_arrival

    def start(self, ins, outs, sems):
        c, own, ici, d2d, ici_arrival = self._copies(ins, outs, sems)
        for w in range(self.n):
            own(w).start()
            for j in range(3):
                ici(w, j).start()

    def finish(self, ins, outs, sems):
        c, own, ici, d2d, ici_arrival = self._copies(ins, outs, sems)
        for w in range(self.n):
            for j in range(3):
                ici_arrival(w, j).wait_recv()
                d2d(w, j, c).start()
        for w in range(self.n):
            for j in range(3):
                d2d(w, j, 1 - c).wait_recv()
        for w in range(self.n):
            for j in range(3):
                ici(w, j).wait_send()
                d2d(w, j, c).wait_send()
            own(w).wait()


class ScatterRider:
    def __init__(self, parts):
        self.arrays = list(parts)
        self.n = n = len(parts)
        self.out_shape = tuple(pltpu.HBM(p.shape, p.dtype) for p in parts)
        self.scratch = [pltpu.SemaphoreType.DMA((n, 3))] * 2 + [pltpu.SemaphoreType.DMA((n,))]

    def _copies(self, ins, outs, sems):
        send, recv, local = sems
        x, y, c, chips = _place()
        me = 2 * x + y

        def own(w):
            return pltpu.make_async_copy(ins[w].at[me], outs[w].at[me], local.at[w])

        def copy(w, j):
            px, py = chips[j]
            return pltpu.make_async_remote_copy(
                src_ref=ins[w].at[2 * px + py], dst_ref=outs[w].at[me], send_sem=send.at[w, j],
                recv_sem=recv.at[w, j], device_id=(px, py, c), device_id_type=MESH)

        def arrival(w, j):
            px, py = chips[j]
            blk = outs[w].at[2 * px + py]
            return pltpu.make_async_remote_copy(
                src_ref=blk, dst_ref=blk, send_sem=send.at[w, j], recv_sem=recv.at[w, j],
                device_id=(x, y, c), device_id_type=MESH)

        return own, copy, arrival

    def start(self, ins, outs, sems):
        own, copy, arrival = self._copies(ins, outs, sems)
        for w in range(self.n):
            own(w).start()
            for j in range(3):
                copy(w, j).start()

    def finish(self, ins, outs, sems):
        own, copy, arrival = self._copies(ins, outs, sems)
        for w in range(self.n):
            for j in range(3):
                arrival(w, j).wait_recv()
        for w in range(self.n):
            for j in range(3):
                copy(w, j).wait_send()
            own(w).wait()


def _carry(rider, body, n_in, n_out, first, last):
    if rider is None:
        return body
    k, m = rider.n, len(rider.scratch)

    def carried(*refs):
        ins, r_in = refs[:n_in], refs[n_in:n_in + k]
        outs, r_out = refs[n_in + k:n_in + k + n_out], refs[n_in + k + n_out:n_in + 2 * k + n_out]
        rest = refs[n_in + 2 * k + n_out:]
        scratch, sems = rest[:len(rest) - m], rest[len(rest) - m:]

        @pl.when(first())
        def _():
            rider.start(r_in, r_out, sems)

        body(*ins, *outs, *scratch)

        @pl.when(last())
        def _():
            rider.finish(r_in, r_out, sems)

    return carried


def _carry_specs(rider, in_specs, out_specs, out_shape, scratch):
    if rider is None:
        return list(in_specs), tuple(out_specs), tuple(out_shape), list(scratch)
    k = rider.n
    return (list(in_specs) + [_HBM] * k, tuple(out_specs) + (_HBM,) * k, tuple(out_shape) + rider.out_shape,
            list(scratch) + list(rider.scratch))


def run_rider(rider, *, name):
    k = rider.n

    def body(*refs):
        rider.start(refs[:k], refs[k:2 * k], refs[2 * k:])
        rider.finish(refs[:k], refs[k:2 * k], refs[2 * k:])

    return _pallas(body, name=name, out_shape=rider.out_shape, in_specs=[_HBM] * k, out_specs=(_HBM,) * k,
                   scratch_shapes=rider.scratch)(*rider.arrays)


def allgather_small(shards, *, name):
    n = len(shards)

    def body(*refs):
        ins, outs = refs[:n], refs[n:2 * n]
        send, recv, local = refs[2 * n:]
        x, y, c, chips = _place()
        me = 2 * x + y
        locals_ = [pltpu.make_async_copy(ins[w], outs[w].at[me], local.at[w]) for w in range(n)]
        for cp in locals_:
            cp.start()

        def copy(w, j):
            px, py = chips[j]
            return pltpu.make_async_remote_copy(
                src_ref=ins[w], dst_ref=outs[w].at[me], send_sem=send.at[w, j], recv_sem=recv.at[w, j],
                device_id=(px, py, c), device_id_type=MESH)

        def arrival(w, j):
            px, py = chips[j]
            blk = outs[w].at[2 * px + py]
            return pltpu.make_async_remote_copy(
                src_ref=blk, dst_ref=blk, send_sem=send.at[w, j], recv_sem=recv.at[w, j],
                device_id=(x, y, c), device_id_type=MESH)

        for w in range(n):
            for j in range(3):
                copy(w, j).start()
        for w in range(n):
            for j in range(3):
                arrival(w, j).wait_recv()
        for w in range(n):
            for j in range(3):
                copy(w, j).wait_send()
        for cp in locals_:
            cp.wait()

    out_shape = tuple(pltpu.HBM((N_CHIPS,) + s.shape, s.dtype) for s in shards)
    return _pallas(
        body, name=name, out_shape=out_shape, in_specs=[_HBM] * n, out_specs=(_HBM,) * n,
        scratch_shapes=[pltpu.SemaphoreType.DMA((n, 3))] * 2 + [pltpu.SemaphoreType.DMA((n,))],
    )(*shards)


def rs_sibling_swap(grads, *, name):
    n = len(grads)

    def body(*refs):
        ins, gots = refs[:n], refs[n:2 * n]
        send, recv = refs[2 * n:]
        x, y, c, _ = _place()
        swaps = [pltpu.make_async_remote_copy(
            src_ref=ins[w].at[:, 1 - c], dst_ref=gots[w], send_sem=send.at[w], recv_sem=recv.at[w],
            device_id=(x, y, 1 - c), device_id_type=MESH) for w in range(n)]
        for cp in swaps:
            cp.start()
        for cp in swaps:
            cp.wait_recv()
        for cp in swaps:
            cp.wait_send()

    half = tuple(pltpu.HBM((N_CHIPS,) + g.shape[2:], g.dtype) for g in grads)
    return _pallas(
        body, name=name, out_shape=half, in_specs=[_HBM] * n, out_specs=(_HBM,) * n,
        scratch_shapes=[pltpu.SemaphoreType.DMA((n,))] * 2,
    )(*grads)


def rs_sibling_share(stacked, *, name):
    n = len(stacked)

    def body(*refs):
        bufs = refs[n:2 * n]
        send, recv = refs[2 * n:]
        x, y, c, _ = _place()
        shares, arrivals = [], []
        for w in range(n):
            mine, other = bufs[w].at[:, c], bufs[w].at[:, 1 - c]
            shares.append(pltpu.make_async_remote_copy(
                src_ref=mine, dst_ref=mine, send_sem=send.at[w], recv_sem=recv.at[w],
                device_id=(x, y, 1 - c), device_id_type=MESH))
            arrivals.append(pltpu.make_async_remote_copy(
                src_ref=other, dst_ref=other, send_sem=send.at[w], recv_sem=recv.at[w],
                device_id=(x, y, c), device_id_type=MESH))
        for cp in shares:
            cp.start()
        for cp in arrivals:
            cp.wait_recv()
        for cp in shares:
            cp.wait_send()

    out_shape = tuple(pltpu.HBM(s.shape, F32) for s in stacked)
    return _pallas(
        body, name=name, out_shape=out_shape, in_specs=[_HBM] * n, out_specs=(_HBM,) * n,
        input_output_aliases={w: w for w in range(n)},
        scratch_shapes=[pltpu.SemaphoreType.DMA((n,))] * 2,
    )(*stacked)


def allreduce_small(v, *, name):
    R, C = v.shape

    def body(v_ref, o_ref, land, send, recv):
        x, y, c, _ = _place()
        me = 4 * x + 2 * y + c
        land[me] = v_ref[...]

        def flip(k):
            return (1 - x) if k & 4 else x, (1 - y) if k & 2 else y, (1 - c) if k & 1 else c

        copies = []
        for k in range(1, N_DEV):
            px, py, pc = flip(k)
            copies.append(pltpu.make_async_remote_copy(
                src_ref=v_ref, dst_ref=land.at[me], send_sem=send.at[k - 1], recv_sem=recv.at[k - 1],
                device_id=(px, py, pc), device_id_type=MESH))
        for cp in copies:
            cp.start()
        for k in range(1, N_DEV):
            px, py, pc = flip(k)
            blk = land.at[4 * px + 2 * py + pc]
            pltpu.make_async_remote_copy(
                src_ref=blk, dst_ref=blk, send_sem=send.at[k - 1], recv_sem=recv.at[k - 1],
                device_id=(x, y, c), device_id_type=MESH).wait_recv()
        for cp in copies:
            cp.wait_send()
        acc = land[0]
        for d in range(1, N_DEV):
            acc = acc + land[d]
        o_ref[...] = acc

    vm = pl.BlockSpec(memory_space=pltpu.VMEM)
    return pl.pallas_call(
        body, name=name, out_shape=jax.ShapeDtypeStruct((R, C), F32), in_specs=[vm], out_specs=vm,
        scratch_shapes=[pltpu.VMEM((N_DEV, R, C), F32), pltpu.SemaphoreType.DMA((N_DEV - 1,)),
                        pltpu.SemaphoreType.DMA((N_DEV - 1,))],
        compiler_params=pltpu.CompilerParams(vmem_limit_bytes=int(min(12 * R * C * 4 + (8 << 20), VMEM_CAP))),
    )(v)


def _pack(arrays):
    flat = jnp.concatenate([a.reshape(-1) for a in arrays])
    return flat.reshape(-1, LANES)


def _unpack(packed, shapes):
    flat = packed.reshape(-1)
    out, off = [], 0
    for s in shapes:
        n = 1
        for d in s:
            n *= d
        out.append(flat[off:off + n].reshape(s))
        off += n
    return out


def _row_tile(rows, cap=512):
    t = 1 << (cap.bit_length() - 1)
    while rows % t:
        t //= 2
    return t


def _adamw_tile(rows, cols):
    return _row_tile(rows, max(8, (1 << 20) // (4 * cols)))


def kernel(x, mem, w_in, conv_w, conv_b, conv_ln_g, conv_ln_b, w_out, ln1_g, ln1_b, mem_wq, mem_wk, mem_wv, mem_wo, ln2_g, ln2_b, ffn_up, ffn_conv_w, ffn_conv_b, ffn_down, ln3_g, ln3_b, loss_target, m_w_in, m_conv_w, m_conv_b, m_conv_ln_g, m_conv_ln_b, m_w_out, m_ln1_g, m_ln1_b, m_mem_wq, m_mem_wk, m_mem_wv, m_mem_wo, m_ln2_g, m_ln2_b, m_ffn_up, m_ffn_conv_w, m_ffn_conv_b, m_ffn_down, m_ln3_g, m_ln3_b, v_w_in, v_conv_w, v_conv_b, v_conv_ln_g, v_conv_ln_b, v_w_out, v_ln1_g, v_ln1_b, v_mem_wq, v_mem_wk, v_mem_wv, v_mem_wo, v_ln2_g, v_ln2_b, v_ffn_up, v_ffn_conv_w, v_ffn_conv_b, v_ffn_down, v_ln3_g, v_ln3_b):
    W = dict(w_in=w_in, conv_w=conv_w, conv_b=conv_b, conv_ln_g=conv_ln_g, conv_ln_b=conv_ln_b, w_out=w_out,
             ln1_g=ln1_g, ln1_b=ln1_b, mem_wq=mem_wq, mem_wk=mem_wk, mem_wv=mem_wv, mem_wo=mem_wo, ln2_g=ln2_g,
             ln2_b=ln2_b, ffn_up=ffn_up, ffn_conv_w=ffn_conv_w, ffn_conv_b=ffn_conv_b, ffn_down=ffn_down,
             ln3_g=ln3_g, ln3_b=ln3_b)
    M1 = dict(w_in=m_w_in, conv_w=m_conv_w, conv_b=m_conv_b, conv_ln_g=m_conv_ln_g, conv_ln_b=m_conv_ln_b,
              w_out=m_w_out, ln1_g=m_ln1_g, ln1_b=m_ln1_b, mem_wq=m_mem_wq, mem_wk=m_mem_wk, mem_wv=m_mem_wv,
              mem_wo=m_mem_wo, ln2_g=m_ln2_g, ln2_b=m_ln2_b, ffn_up=m_ffn_up, ffn_conv_w=m_ffn_conv_w,
              ffn_conv_b=m_ffn_conv_b, ffn_down=m_ffn_down, ln3_g=m_ln3_g, ln3_b=m_ln3_b)
    V2 = dict(w_in=v_w_in, conv_w=v_conv_w, conv_b=v_conv_b, conv_ln_g=v_conv_ln_g, conv_ln_b=v_conv_ln_b,
              w_out=v_w_out, ln1_g=v_ln1_g, ln1_b=v_ln1_b, mem_wq=v_mem_wq, mem_wk=v_mem_wk, mem_wv=v_mem_wv,
              mem_wo=v_mem_wo, ln2_g=v_ln2_g, ln2_b=v_ln2_b, ffn_up=v_ffn_up, ffn_conv_w=v_ffn_conv_w,
              ffn_conv_b=v_ffn_conv_b, ffn_down=v_ffn_down, ln3_g=v_ln3_g, ln3_b=v_ln3_b)

    L = w_in.shape[0]
    S, D = x.shape[1], x.shape[2]
    C = conv_b.shape[1]
    alpha = (2.0 * L) ** 0.25
    chip = 2 * lax.axis_index("x") + lax.axis_index("y")
    xs, mems, tgt = x[0], mem[0], loss_target[0]
    mem_bf = mems.astype(BF16)
    tm = _row_tile(S)
    tm_ffn = _row_tile(S, 256)
    tm_big = _row_tile(S, 1024)

    def shards_of(l, names):
        out = []
        for n in names:
            wl = W[n][l].astype(BF16)
            out.append(wl.reshape(2, wl.shape[0] // 2, wl.shape[1]))
        return out

    def gathered(names, got):
        layer = {}
        for n, g in zip(names, got):
            rows, cols = W[n].shape[1], W[n].shape[2]
            layer[n] = g.reshape(N_CHIPS, rows, cols) if n in COL_SHARDED else g.reshape(N_CHIPS * rows, cols)
        return layer

    full = [dict() for _ in range(L)]
    full[0].update(gathered(RIDE_IN, run_rider(GatherRider(shards_of(0, RIDE_IN)), name="allgather_w_in")))
    cw_all, fcw_all = allgather_small([conv_w, ffn_conv_w], name="allgather_small")
    cw_full = jnp.transpose(cw_all, (1, 2, 0, 3)).reshape(L, conv_w.shape[1], -1)
    fcw_full = jnp.transpose(fcw_all, (1, 2, 0, 3)).reshape(L, ffn_conv_w.shape[1], -1)

    saved = []
    h, hb = xs, xs.astype(BF16)
    for l in range(L):
        fw = full[l]
        s = dict(x=h, xb=hb)
        s['proj'] = mm_nn(hb, fw['w_in'], F32, tm=min(1024, S), tn=fw['w_in'].shape[2], name="proj")
        s['u1'], got = conv_fwd(s['proj'], cw_full[l], conv_b[l][None], name="conv_fwd",
                                rider=GatherRider(shards_of(0, RIDE_ATT)) if l == 0 else None)
        if l == 0:
            fw.update(gathered(RIDE_ATT, got))
        more = l + 1 < L
        s['o_sb'], s['ltot'], got = sb_fwd(
            s['proj'], q_col=2 * C, name="sb_fwd",
            rider=GatherRider(shards_of(l, RIDE_FFN) + (shards_of(l + 1, RIDE_IN) if more else [])))
        fw.update(gathered(RIDE_FFN, got[:len(RIDE_FFN)]))
        if more:
            full[l + 1].update(gathered(RIDE_IN, got[len(RIDE_FFN):]))
        s['ua'] = ln_silu(s['u1'], s['o_sb'], conv_ln_g[l][None], conv_ln_b[l][None], tm=tm, name="ln_silu")
        s['x1'], s['x1b'], s['zh1'], s['rs1'] = mm_ln(
            s['ua'], fw['w_out'], h, ln1_g[l][None], ln1_b[l][None], alpha, tm=tm, name="out_proj_ln")
        s['q2'] = mm_nn(s['x1b'], fw['mem_wq'], BF16, tm=min(1024, S), tn=512, name="mem_q")
        s['k2'] = mm_nn(mem_bf, fw['mem_wk'], BF16, tm=mem_bf.shape[0], tn=512, name="mem_kv")
        s['v2'] = mm_nn(mem_bf, fw['mem_wv'], BF16, tm=mem_bf.shape[0], tn=512, name="mem_kv")
        s['o2'] = xattn_fwd(s['q2'], s['k2'], s['v2'], tm=tm, name="xattn_fwd")
        s['x2'], s['x2b'], s['zh2'], s['rs2'] = mm_ln(
            s['o2'], fw['mem_wo'], s['x1'], ln2_g[l][None], ln2_b[l][None], alpha, tm=tm, name="mem_o_ln")
        s['upv'], s['upg'], s['hmid'], got = ffn_up_fwd(
            s['x2b'], fw['ffn_up'], fcw_full[l], ffn_conv_b[l][None], tm=tm_ffn, tn=fw['ffn_up'].shape[2],
            name="ffn_up_fwd", rider=GatherRider(shards_of(l + 1, RIDE_ATT)) if more else None)
        if more:
            full[l + 1].update(gathered(RIDE_ATT, got))
        h, hb, s['zh3'], s['rs3'] = mm_ln(
            s['hmid'], fw['ffn_down'], s['x2'], ln3_g[l][None], ln3_b[l][None], alpha, tm=tm, name="ffn_down_ln")
        saved.append(s)

    dx, loss_part = loss_head(h, tgt, tm=tm, name="loss_head")
    loss = lax.psum(loss_part[0, 0], ("x", "y", "c"))

    core = lax.axis_index("c").astype(jnp.int32).reshape(1)
    reduced_big = {n: lax.empty((L, 2, W[n].shape[1] // 2, W[n].shape[2]), F32) for n in BIG}
    small_grads = [None] * L

    def pre_add(g, names):
        parts = []
        for n in names:
            rows, cols = W[n].shape[1], W[n].shape[2]
            parts.append(g[n].reshape(N_CHIPS, 2, rows // 2, cols))
        got = rs_sibling_swap(parts, name="rs_sibling_swap")
        return list(add_pairs(parts, got, core, name="rs_add_pairs"))

    def reduce_into(names, scattered, layer):
        reduced_big.update(zip(names, sum_chips_into(
            list(scattered), [reduced_big[n] for n in names], layer, core, name="rs_sum_chips")))

    pending = None
    for l in reversed(range(L)):
        fw, s = full[l], saved[l]
        g = {}
        if l == L - 1:
            top = ln_bwd(dx, s['zh3'], s['rs3'], ln3_g[l][None], tm=tm, name="ln_bwd")
        dz3, dz3b, g['ln3_g'], g['ln3_b'] = top
        ftn = fw['ffn_up'].shape[2]
        (dupv, dupg, dfw_v, dfw_g, dfb_v, dfb_g), sc = ffn_mid_bwd(
            dz3b, fw['ffn_down'], s['upv'], s['upg'], fcw_full[l], ffn_conv_b[l][None], tm=tm_ffn, tn=ftn,
            name="ffn_mid_bwd", rider=ScatterRider(pending) if pending else None)
        if pending:
            reduce_into(RIDE_MIX, sc, l + 1)
        g['ffn_conv_w'] = jnp.concatenate([dfw_v, dfw_g], axis=1)
        g['ffn_conv_b'] = jnp.concatenate([dfb_v, dfb_g], axis=1)[0]
        g['ffn_down'] = mm_tn(s['hmid'], [dz3b], tk=ftn, tn=512, tmc=min(1024, S), name="grad_ffn_down")
        dz2, dz2b, g['ln2_g'], g['ln2_b'] = mm_nt_ln_bwd(
            [dupv, dupg], fw['ffn_up'], dz3, alpha, s['zh2'], s['rs2'], ln2_g[l][None], tm=tm_ffn,
            name="ffn_up_bwd")
        g['ffn_up'] = mm_tn(s['x2b'], [dupv, dupg], tk=512, tn=ftn, shard_width=ftn, tmc=min(1024, S),
                            name="grad_ffn_up")

        do2 = mm_nt([dz2b], fw['mem_wo'], BF16, tm=tm_big, tk=512, name="mem_o_bwd")
        g['mem_wo'] = mm_tn(s['o2'], [dz2b], tk=512, tn=512, name="grad_sq")
        dq2, dk2, dv2 = xattn_bwd(s['q2'], do2, s['k2'], s['v2'], tm=tm, name="xattn_bwd")
        dz1, dz1b, g['ln1_g'], g['ln1_b'] = mm_nt_ln_bwd(
            [dq2], fw['mem_wq'], dz2, alpha, s['zh1'], s['rs1'], ln1_g[l][None], tm=tm, name="mem_q_bwd")
        g['mem_wq'] = mm_tn(s['x1b'], [dq2], tk=512, tn=512, name="grad_sq")
        g['mem_wk'] = mm_tn(mem_bf, [dk2], tk=512, tn=512, name="grad_mem_kv")
        g['mem_wv'] = mm_tn(mem_bf, [dv2], tk=512, tn=512, name="grad_mem_kv")

        dua = mm_nt([dz1b], fw['w_out'], F32, tm=tm_big, tk=512, name="out_proj_bwd")
        g['w_out'] = mm_tn(s['ua'], [dz1b], tk=512, tn=512, name="grad_sq")
        dq, dk, dv, sc = sb_bwd(
            s['proj'], s['ltot'], dua, q_col=2 * C, do_col=C, name="sb_bwd",
            rider=ScatterRider(pre_add(g, RIDE_REST)))
        reduce_into(RIDE_REST, sc, l)
        du1, g['conv_ln_g'], g['conv_ln_b'] = ln_silu_bwd(
            dua, s['u1'], conv_ln_g[l][None], conv_ln_b[l][None], tm=tm, name="ln_silu_bwd")
        da, dg, g['conv_w'], dcb = conv_bwd(du1, s['proj'], cw_full[l], name="conv_bwd")
        g['conv_b'] = dcb
        dproj = jnp.concatenate([da, dg, dq, dk, dv], axis=1)
        ns_in = fw['w_in'].shape[2]
        if l > 0:
            below = saved[l - 1]
            top = mm_nt_ln_bwd([dproj], fw['w_in'], dz1, alpha, below['zh3'], below['rs3'], ln3_g[l - 1][None],
                               tm=tm, name="proj_bwd")
        else:
            dx = mm_nt([dproj], fw['w_in'], F32, tm=tm_big, tk=512, res=dz1, alpha=alpha, name="proj_bwd_x")
        g['w_in'] = mm_tn(s['xb'], [dproj], tk=512, tn=ns_in, shard_width=ns_in, name="grad_w_in")

        pending = pre_add(g, RIDE_MIX)
        small_grads[l] = {n: g[n].reshape(W[n].shape[1:-1] + (-1,)) for n in SMALL}

    grad_x = dx[None]

    reduce_into(RIDE_MIX, run_rider(ScatterRider(pending), name="rs_chip_scatter"), 0)
    shared = rs_sibling_share([reduced_big[n] for n in BIG], name="rs_sibling_share")
    G = {}
    for n, sh in zip(BIG, shared):
        G[n] = sh.reshape(W[n].shape)

    small_full_shapes = []
    small_stack = []
    for n in SMALL:
        st = jnp.stack([small_grads[l][n] for l in range(L)])
        small_stack.append(st)
        small_full_shapes.append(st.shape)
    reduced = _unpack(allreduce_small(_pack(small_stack), name="allreduce_small"), small_full_shapes)
    for n, r in zip(SMALL, reduced):
        if n in SMALL_SHARDED:
            width = W[n].shape[-1]
            r = lax.dynamic_slice_in_dim(r, chip * width, width, axis=2)
        G[n] = r

    out_g, out_d, out_m, out_v = {}, {}, {}, {}
    for n in BIG:
        shp = W[n].shape
        flat = lambda a: a.reshape(shp[0] * shp[1], shp[2])
        res = adamw(flat(W[n]), flat(G[n]), flat(M1[n]), flat(V2[n]), tr=_adamw_tile(shp[0] * shp[1], shp[2]), name="adamw")
        out_g[n], out_d[n], out_m[n], out_v[n] = [r.reshape(shp) for r in res]
    small_shapes = [W[n].shape for n in SMALL]
    packed = [_pack([d[n] for n in SMALL]) for d in (W, G, M1, V2)]
    res = adamw(*packed, tr=packed[0].shape[0], name="adamw_small")
    for d, r in zip((out_g, out_d, out_m, out_v), res):
        for n, a in zip(SMALL, _unpack(r, small_shapes)):
            d[n] = a

    return (loss, grad_x, *[out_g[n] for n in WEIGHTS], *[out_d[n] for n in WEIGHTS],
            *[out_m[n] for n in WEIGHTS], *[out_v[n] for n in WEIGHTS])
```

```python
import functools

import jax
import jax.numpy as jnp
from jax import lax
from jax.experimental import pallas as pl
from jax.experimental.pallas import tpu as pltpu

F32 = jnp.float32
BF16 = jnp.bfloat16
MESH = pl.DeviceIdType.MESH

LN_EPS = 1e-5
SB_HEADS = 8
MEM_HEADS = 4
ADAM_LR, ADAM_B1, ADAM_B2, ADAM_EPS, ADAM_WD, ADAM_STEP = 0.001, 0.9, 0.999, 1e-08, 0.01, 10

LANES = 128
V7X_VMEM_BYTES = 64 << 20
VMEM_CAP = V7X_VMEM_BYTES - (6 << 20)
N_CHIPS = 4
N_DEV = 8

BIG = ('w_in', 'w_out', 'mem_wq', 'mem_wk', 'mem_wv', 'mem_wo', 'ffn_up', 'ffn_down')
RIDE_IN = ('w_in',)
RIDE_ATT = ('w_out', 'mem_wq', 'mem_wk', 'mem_wv', 'mem_wo')
RIDE_FFN = ('ffn_up', 'ffn_down')
RIDE_MIX = ('w_in', 'w_out')
RIDE_REST = ('mem_wq', 'mem_wk', 'mem_wv', 'mem_wo', 'ffn_up', 'ffn_down')
COL_SHARDED = ('w_in', 'ffn_up')
SMALL = ('conv_w', 'conv_b', 'conv_ln_g', 'conv_ln_b', 'ln1_g', 'ln1_b', 'ln2_g', 'ln2_b',
         'ffn_conv_w', 'ffn_conv_b', 'ln3_g', 'ln3_b')
SMALL_SHARDED = ('conv_w', 'ffn_conv_w')
WEIGHTS = ('w_in', 'conv_w', 'conv_b', 'conv_ln_g', 'conv_ln_b', 'w_out', 'ln1_g', 'ln1_b',
           'mem_wq', 'mem_wk', 'mem_wv', 'mem_wo', 'ln2_g', 'ln2_b', 'ffn_up', 'ffn_conv_w',
           'ffn_conv_b', 'ffn_down', 'ln3_g', 'ln3_b')


def _params(block_bytes, semantics=None, **kw):
    limit = int(min(max(2 * block_bytes + (8 << 20), 32 << 20), VMEM_CAP))
    return pltpu.CompilerParams(dimension_semantics=semantics, vmem_limit_bytes=limit, **kw)


def _pallas(body, **kw):
    call = pl.pallas_call(body, **kw)

    def run(*args):
        return call(*[pltpu.with_memory_space_constraint(a, pltpu.HBM)
                      if jnp.issubdtype(a.dtype, jnp.floating) else a for a in args])

    return run


def _nbytes(shape, dtype):
    n = 1
    for s in shape:
        n *= s
    return n * jnp.dtype(dtype).itemsize


def _dot(a, b):
    return jnp.dot(a, b, preferred_element_type=F32)


def _dot_nt(a, b):
    return lax.dot_general(a, b, (((1,), (1,)), ((), ())), preferred_element_type=F32)


def _dot_tn(a, b):
    return lax.dot_general(a, b, (((0,), (0,)), ((), ())), preferred_element_type=F32)


def _sigmoid(x):
    return 1.0 / (1.0 + jnp.exp(-x))


def mm_nn(a, b, out_dtype, *, tm, tn, name):
    M, K = a.shape
    sharded = b.ndim == 3
    if sharded:
        nsh, _, ns = b.shape
        N, per = nsh * ns, ns // tn
        b_spec = pl.BlockSpec((None, K, tn), lambda i, j: (j // per, 0, j % per))
    else:
        N = b.shape[1]
        b_spec = pl.BlockSpec((K, tn), lambda i, j: (0, j))

    def body(a_ref, b_ref, o_ref):
        o_ref[...] = _dot(a_ref[...].astype(BF16), b_ref[...]).astype(o_ref.dtype)

    blk = _nbytes((tm, K), a.dtype) + _nbytes((K, tn), BF16) + _nbytes((tm, tn), out_dtype)
    return _pallas(
        body, name=name, out_shape=pltpu.HBM((M, N), out_dtype), grid=(M // tm, N // tn),
        in_specs=[pl.BlockSpec((tm, K), lambda i, j: (i, 0)), b_spec],
        out_specs=pl.BlockSpec((tm, tn), lambda i, j: (i, j)),
        compiler_params=_params(blk, ("parallel", "parallel")))(a, b)


def mm_ln(a, b, x, gamma, beta, alpha, *, tm, name):
    M, K = a.shape
    D = b.shape[1]

    def body(a_ref, b_ref, x_ref, g_ref, be_ref, y_ref, yb_ref, zh_ref, rs_ref):
        z = alpha * x_ref[...] + _dot(a_ref[...], b_ref[...])
        mu = jnp.mean(z, axis=-1, keepdims=True)
        zc = z - mu
        rstd = lax.rsqrt(jnp.mean(zc * zc, axis=-1, keepdims=True) + LN_EPS)
        zh = zc * rstd
        y = zh * g_ref[...] + be_ref[...]
        y_ref[...] = y
        yb_ref[...] = y.astype(BF16)
        zh_ref[...] = zh
        rs_ref[...] = rstd

    row = lambda i: (i, 0)
    fix = lambda i: (0, 0)
    blk = _nbytes((tm, K), BF16) + _nbytes((K, D), BF16) + 4 * _nbytes((tm, D), F32)
    return _pallas(
        body, name=name, grid=(M // tm,),
        out_shape=(pltpu.HBM((M, D), F32), pltpu.HBM((M, D), BF16),
                   pltpu.HBM((M, D), F32), pltpu.HBM((M, 1), F32)),
        in_specs=[pl.BlockSpec((tm, K), row), pl.BlockSpec((K, D), fix), pl.BlockSpec((tm, D), row),
                  pl.BlockSpec((1, D), fix), pl.BlockSpec((1, D), fix)],
        out_specs=(pl.BlockSpec((tm, D), row), pl.BlockSpec((tm, D), row), pl.BlockSpec((tm, D), row),
                   pl.BlockSpec((tm, 1), row)),
        compiler_params=_params(blk, ("parallel",)))(a, b, x, gamma, beta)


def ln_bwd(dy, zh, rstd, gamma, *, tm, name):
    M, D = dy.shape

    def body(dy_ref, zh_ref, rs_ref, g_ref, dz_ref, dzb_ref, dg_ref, db_ref):
        @pl.when(pl.program_id(0) == 0)
        def _():
            dg_ref[...] = jnp.zeros_like(dg_ref)
            db_ref[...] = jnp.zeros_like(db_ref)

        dyv, zhv = dy_ref[...], zh_ref[...]
        dg_ref[...] += jnp.sum(dyv * zhv, axis=0, keepdims=True)
        db_ref[...] += jnp.sum(dyv, axis=0, keepdims=True)
        dzh = dyv * g_ref[...]
        m1 = jnp.mean(dzh, axis=-1, keepdims=True)
        m2 = jnp.mean(dzh * zhv, axis=-1, keepdims=True)
        dz = rs_ref[...] * (dzh - m1 - zhv * m2)
        dz_ref[...] = dz
        dzb_ref[...] = dz.astype(BF16)

    row = lambda i: (i, 0)
    fix = lambda i: (0, 0)
    return _pallas(
        body, name=name, grid=(M // tm,),
        out_shape=(pltpu.HBM((M, D), F32), pltpu.HBM((M, D), BF16),
                   pltpu.HBM((1, D), F32), pltpu.HBM((1, D), F32)),
        in_specs=[pl.BlockSpec((tm, D), row), pl.BlockSpec((tm, D), row), pl.BlockSpec((tm, 1), row),
                  pl.BlockSpec((1, D), fix)],
        out_specs=(pl.BlockSpec((tm, D), row), pl.BlockSpec((tm, D), row), pl.BlockSpec((1, D), fix),
                   pl.BlockSpec((1, D), fix)),
        compiler_params=_params(4 * _nbytes((tm, D), F32), ("arbitrary",)))(dy, zh, rstd, gamma)


def mm_nt(a_list, b, out_dtype, *, tm, tk, name, res=None, alpha=None, rider=None):
    M = a_list[0].shape[0]
    widths = [a.shape[1] for a in a_list]
    sharded = b.ndim == 3
    if sharded:
        nsh, K, ns = b.shape
        b_spec = pl.BlockSpec((nsh, tk, ns), lambda i, j: (0, j, 0))
        for w in widths:
            assert w % ns == 0
    else:
        K, N = b.shape
        ns = None
        b_spec = pl.BlockSpec((tk, N), lambda i, j: (j, 0))
    n_a = len(a_list)

    def body(*refs):
        a_refs, b_ref = refs[:n_a], refs[n_a]
        o_ref = refs[-1]
        acc = None
        off = 0
        for a_ref, w in zip(a_refs, widths):
            if sharded:
                for p in range(w // ns):
                    t = _dot_nt(a_ref[:, p * ns:(p + 1) * ns].astype(BF16), b_ref[off // ns + p])
                    acc = t if acc is None else acc + t
            else:
                t = _dot_nt(a_ref[...].astype(BF16), b_ref[:, off:off + w])
                acc = t if acc is None else acc + t
            off += w
        if res is not None:
            acc = acc + alpha * refs[n_a + 1][...]
        o_ref[...] = acc.astype(o_ref.dtype)

    in_specs = [pl.BlockSpec((tm, w), lambda i, j: (i, 0)) for w in widths] + [b_spec]
    args = list(a_list) + [b]
    if res is not None:
        in_specs.append(pl.BlockSpec((tm, tk), lambda i, j: (i, j)))
        args.append(res)
    blk = (sum(_nbytes((tm, w), a.dtype) for a, w in zip(a_list, widths)) + _nbytes((tk, sum(widths)), BF16)
           + 2 * _nbytes((tm, tk), F32))
    in_specs, out_specs, out_shape, scratch = _carry_specs(
        rider, in_specs, (pl.BlockSpec((tm, tk), lambda i, j: (i, j)),), (pltpu.HBM((M, K), out_dtype),), [])
    first = lambda: (pl.program_id(0) == 0) & (pl.program_id(1) == 0)
    last = lambda: (pl.program_id(0) == M // tm - 1) & (pl.program_id(1) == K // tk - 1)
    res_all = _pallas(
        _carry(rider, body, len(args), 1, first, last), name=name, out_shape=out_shape, grid=(M // tm, K // tk),
        in_specs=in_specs, out_specs=out_specs, scratch_shapes=scratch,
        compiler_params=_params(blk, ("arbitrary", "arbitrary")))(*args, *(rider.arrays if rider else ()))
    return res_all[0] if rider is None else (res_all[0], list(res_all[1:]))


def mm_nt_ln_bwd(a_list, b, res, alpha, zh, rstd, gamma, *, tm, name):
    M, D = res.shape
    widths = [a.shape[1] for a in a_list]
    sharded = b.ndim == 3
    if sharded:
        nsh, _, ns = b.shape
        b_spec = pl.BlockSpec((nsh, D, ns), lambda i: (0, 0, 0))
    else:
        ns = None
        b_spec = pl.BlockSpec((D, b.shape[1]), lambda i: (0, 0))
    n_a = len(a_list)

    def body(*refs):
        a_refs, b_ref = refs[:n_a], refs[n_a]
        res_ref, zh_ref, rs_ref, g_ref = refs[n_a + 1:n_a + 5]
        dz_ref, dzb_ref, dg_ref, db_ref = refs[n_a + 5:]

        @pl.when(pl.program_id(0) == 0)
        def _():
            dg_ref[...] = jnp.zeros_like(dg_ref)
            db_ref[...] = jnp.zeros_like(db_ref)

        dy = alpha * res_ref[...]
        off = 0
        for a_ref, w in zip(a_refs, widths):
            if sharded:
                for p in range(w // ns):
                    dy = dy + _dot_nt(a_ref[:, p * ns:(p + 1) * ns], b_ref[off // ns + p])
            else:
                dy = dy + _dot_nt(a_ref[...], b_ref[:, off:off + w])
            off += w
        zhv = zh_ref[...]
        dg_ref[...] += jnp.sum(dy * zhv, axis=0, keepdims=True)
        db_ref[...] += jnp.sum(dy, axis=0, keepdims=True)
        dzh = dy * g_ref[...]
        m1 = jnp.mean(dzh, axis=-1, keepdims=True)
        m2 = jnp.mean(dzh * zhv, axis=-1, keepdims=True)
        dz = rs_ref[...] * (dzh - m1 - zhv * m2)
        dz_ref[...] = dz
        dzb_ref[...] = dz.astype(BF16)

    row = lambda i: (i, 0)
    fix = lambda i: (0, 0)
    in_specs = [pl.BlockSpec((tm, w), row) for w in widths] + [
        b_spec, pl.BlockSpec((tm, D), row), pl.BlockSpec((tm, D), row), pl.BlockSpec((tm, 1), row),
        pl.BlockSpec((1, D), fix)]
    blk = (sum(_nbytes((tm, w), BF16) for w in widths) + _nbytes((D, sum(widths)), BF16)
           + 5 * _nbytes((tm, D), F32))
    return _pallas(
        body, name=name, grid=(M // tm,),
        out_shape=(pltpu.HBM((M, D), F32), pltpu.HBM((M, D), BF16), pltpu.HBM((1, D), F32),
                   pltpu.HBM((1, D), F32)),
        in_specs=in_specs,
        out_specs=(pl.BlockSpec((tm, D), row), pl.BlockSpec((tm, D), row), pl.BlockSpec((1, D), fix),
                   pl.BlockSpec((1, D), fix)),
        compiler_params=_params(blk, ("arbitrary",)))(*a_list, b, res, zh, rstd, gamma)


def mm_tn(a, b_list, *, tk, tn, name, shard_width=None, tmc=None):
    M, K = a.shape
    tmc = M if tmc is None else tmc
    nm = M // tmc
    widths = [b.shape[1] for b in b_list]
    N = sum(widths)
    starts, s = [], 0
    for w in widths:
        assert w % tn == 0
        starts.append(s)
        s += w // tn
    n_b = len(b_list)

    def body(*refs):
        a_ref, b_refs, o_ref, acc = refs[0], refs[1:1 + n_b], refs[-2], refs[-1]
        j, m = pl.program_id(1), pl.program_id(2)
        for b_ref, st, w in zip(b_refs, starts, widths):
            @pl.when((j >= st) & (j < st + w // tn))
            def _(b_ref=b_ref):
                t = _dot_tn(a_ref[...].astype(BF16), b_ref[...].astype(BF16))
                if nm == 1:
                    o_ref[...] = t.astype(o_ref.dtype)
                else:
                    @pl.when(m == 0)
                    def _():
                        acc[...] = t

                    @pl.when(m > 0)
                    def _():
                        acc[...] += t

                    @pl.when(m == nm - 1)
                    def _():
                        o_ref[...] = acc[...].astype(o_ref.dtype)

    def b_map(st, w):
        nb = w // tn
        return lambda i, j, m: (jnp.where((j >= st) & (j < st + nb), m, 0), jnp.clip(j - st, 0, nb - 1))

    in_specs = [pl.BlockSpec((tmc, tk), lambda i, j, m: (m, i))]
    in_specs += [pl.BlockSpec((tmc, tn), b_map(st, w)) for st, w in zip(starts, widths)]
    if shard_width is None:
        out_shape = pltpu.HBM((K, N), BF16)
        out_spec = pl.BlockSpec((tk, tn), lambda i, j, m: (i, j))
    else:
        per = shard_width // tn
        out_shape = pltpu.HBM((N // shard_width, K, shard_width), BF16)
        out_spec = pl.BlockSpec((None, tk, tn), lambda i, j, m: (j // per, i, j % per))
    acc_shape = (tk, tn) if nm > 1 else (8, LANES)
    blk = (_nbytes((tmc, tk), a.dtype) + n_b * _nbytes((tmc, tn), b_list[0].dtype) + 2 * _nbytes((tk, tn), F32))
    return _pallas(
        body, name=name, out_shape=out_shape, grid=(K // tk, N // tn, nm), in_specs=in_specs, out_specs=out_spec,
        scratch_shapes=[pltpu.VMEM(acc_shape, F32)],
        compiler_params=_params(blk, ("parallel", "arbitrary", "arbitrary")))(a, *b_list)


CONV_PAD = 32
CONV_CHUNK = 128


def _rows(win, off, n, shifts):
    b, a = off % 8, off // 8
    if b not in shifts:
        shifts[b] = win if b == 0 else win[b:b + n + CONV_PAD - 8, :]
    return shifts[b][8 * a:8 * a + n, :]


def _by_residue(n_taps, offset):
    return sorted(range(n_taps), key=lambda k: (offset(k) % 8, k))


def conv_fwd(proj, conv_w, conv_b, *, name, rider=None):
    S = proj.shape[0]
    KW, C = conv_w.shape
    nct = C // LANES
    rc = min(CONV_CHUNK, S)

    def body(a_ref, g_ref, w_ref, b_ref, o_ref, pad):
        pad[0:CONV_PAD, :] = jnp.zeros((CONV_PAD, LANES), F32)
        pad[CONV_PAD:, :] = a_ref[...] * _sigmoid(g_ref[...])
        w = w_ref[...]
        bias = b_ref[...]

        def chunk(i, _):
            base = pl.multiple_of(i * rc, rc)
            win = pad[pl.ds(base, rc + CONV_PAD), :]
            acc = jnp.zeros((rc, LANES), F32) + bias
            shifts = {}
            for k in _by_residue(KW, lambda k: CONV_PAD - (KW - 1) + k):
                acc = acc + w[k:k + 1, :] * _rows(win, CONV_PAD - (KW - 1) + k, rc, shifts)
            o_ref[pl.ds(base, rc), :] = acc
            return 0

        lax.fori_loop(0, S // rc, chunk, 0)

    in_specs, out_specs, out_shape, scratch = _carry_specs(
        rider, [pl.BlockSpec((S, LANES), lambda c: (0, c)), pl.BlockSpec((S, LANES), lambda c: (0, c + nct)),
                pl.BlockSpec((KW, LANES), lambda c: (0, c)), pl.BlockSpec((1, LANES), lambda c: (0, c))],
        (pl.BlockSpec((S, LANES), lambda c: (0, c)),), (pltpu.HBM((S, C), F32),),
        [pltpu.VMEM((S + CONV_PAD, LANES), F32)])
    first = lambda: pl.program_id(0) == 0
    last = lambda: pl.program_id(0) == nct - 1
    res = _pallas(
        _carry(rider, body, 4, 1, first, last), name=name, grid=(nct,), out_shape=out_shape,
        in_specs=in_specs, out_specs=out_specs, scratch_shapes=scratch,
        compiler_params=_params(4 * _nbytes((S, LANES), F32), ("arbitrary",)))(
            proj, proj, conv_w, conv_b, *(rider.arrays if rider else ()))
    return res[0], list(res[1:])


def conv_bwd(du1, proj, conv_w, *, name):
    S = proj.shape[0]
    KW, C = conv_w.shape
    nct = C // LANES
    rc = min(CONV_CHUNK, S)

    def body(d_ref, a_ref, g_ref, w_ref, da_ref, dg_ref, dw_ref, db_ref, pad_u, pad_d, du0, dw_acc):
        dw_acc[...] = jnp.zeros_like(dw_acc)
        pad_u[0:CONV_PAD, :] = jnp.zeros((CONV_PAD, LANES), F32)
        pad_u[CONV_PAD:, :] = a_ref[...] * _sigmoid(g_ref[...])
        pad_d[0:S, :] = d_ref[...]
        pad_d[S:, :] = jnp.zeros((CONV_PAD, LANES), F32)
        w = w_ref[...]
        db_ref[...] = jnp.sum(d_ref[...], axis=0, keepdims=True)

        def chunk(i, _):
            base = pl.multiple_of(i * rc, rc)
            d = pad_d[pl.ds(base, rc), :]
            win_u = pad_u[pl.ds(base, rc + CONV_PAD), :]
            win_d = pad_d[pl.ds(base, rc + CONV_PAD), :]
            shifts = {}
            for k in _by_residue(KW, lambda k: CONV_PAD - (KW - 1) + k):
                u_k = _rows(win_u, CONV_PAD - (KW - 1) + k, rc, shifts)
                dw_acc[k:k + 1, :] += jnp.sum(d * u_k, axis=0, keepdims=True)
            acc = jnp.zeros((rc, LANES), F32)
            shifts = {}
            for k in _by_residue(KW, lambda k: KW - 1 - k):
                acc = acc + w[k:k + 1, :] * _rows(win_d, KW - 1 - k, rc, shifts)
            du0[pl.ds(base, rc), :] = acc
            return 0

        lax.fori_loop(0, S // rc, chunk, 0)
        dw_ref[...] = dw_acc[0:KW, :]
        a, sg = a_ref[...], _sigmoid(g_ref[...])
        d0 = du0[...]
        da_ref[...] = (d0 * sg).astype(BF16)
        dg_ref[...] = (d0 * a * sg * (1.0 - sg)).astype(BF16)

    col = lambda c: (0, c)
    return _pallas(
        body, name=name, grid=(nct,),
        out_shape=(pltpu.HBM((S, C), BF16), pltpu.HBM((S, C), BF16),
                   pltpu.HBM((KW, C), F32), pltpu.HBM((1, C), F32)),
        in_specs=[pl.BlockSpec((S, LANES), col), pl.BlockSpec((S, LANES), col),
                  pl.BlockSpec((S, LANES), lambda c: (0, c + nct)), pl.BlockSpec((KW, LANES), col)],
        out_specs=(pl.BlockSpec((S, LANES), col), pl.BlockSpec((S, LANES), col), pl.BlockSpec((KW, LANES), col),
                   pl.BlockSpec((1, LANES), col)),
        scratch_shapes=[pltpu.VMEM((S + CONV_PAD, LANES), F32), pltpu.VMEM((S + CONV_PAD, LANES), F32),
                        pltpu.VMEM((S, LANES), F32), pltpu.VMEM((CONV_PAD, LANES), F32)],
        compiler_params=_params(8 * _nbytes((S, LANES), F32), ("parallel",)))(du1, proj, proj, conv_w)


def ln_silu(u1, o_sb, gamma, beta, *, tm, name):
    S, C = u1.shape

    def body(u_ref, o_ref, g_ref, b_ref, out_ref):
        z = u_ref[...]
        mu = jnp.mean(z, axis=-1, keepdims=True)
        zc = z - mu
        y = zc * lax.rsqrt(jnp.mean(zc * zc, axis=-1, keepdims=True) + LN_EPS) * g_ref[...] + b_ref[...]
        out_ref[:, 0:C] = (y * _sigmoid(y)).astype(BF16)
        out_ref[:, C:] = o_ref[...].astype(BF16)

    row = lambda i: (i, 0)
    fix = lambda i: (0, 0)
    return _pallas(
        body, name=name, out_shape=pltpu.HBM((S, 2 * C), BF16), grid=(S // tm,),
        in_specs=[pl.BlockSpec((tm, C), row), pl.BlockSpec((tm, C), row), pl.BlockSpec((1, C), fix),
                  pl.BlockSpec((1, C), fix)],
        out_specs=pl.BlockSpec((tm, 2 * C), row),
        compiler_params=_params(4 * _nbytes((tm, C), F32), ("parallel",)))(u1, o_sb, gamma, beta)


def ln_silu_bwd(dua, u1, gamma, beta, *, tm, name):
    S, C = u1.shape

    def body(d_ref, u_ref, g_ref, b_ref, du1_ref, dg_ref, db_ref):
        @pl.when(pl.program_id(0) == 0)
        def _():
            dg_ref[...] = jnp.zeros_like(dg_ref)
            db_ref[...] = jnp.zeros_like(db_ref)

        z = u_ref[...]
        mu = jnp.mean(z, axis=-1, keepdims=True)
        zc = z - mu
        rstd = lax.rsqrt(jnp.mean(zc * zc, axis=-1, keepdims=True) + LN_EPS)
        zh = zc * rstd
        y = zh * g_ref[...] + b_ref[...]
        sg = _sigmoid(y)
        dy = d_ref[...] * (sg * (1.0 + y * (1.0 - sg)))
        dg_ref[...] += jnp.sum(dy * zh, axis=0, keepdims=True)
        db_ref[...] += jnp.sum(dy, axis=0, keepdims=True)
        dzh = dy * g_ref[...]
        m1 = jnp.mean(dzh, axis=-1, keepdims=True)
        m2 = jnp.mean(dzh * zh, axis=-1, keepdims=True)
        du1_ref[...] = rstd * (dzh - m1 - zh * m2)

    row = lambda i: (i, 0)
    fix = lambda i: (0, 0)
    return _pallas(
        body, name=name, grid=(S // tm,),
        out_shape=(pltpu.HBM((S, C), F32), pltpu.HBM((1, C), F32),
                   pltpu.HBM((1, C), F32)),
        in_specs=[pl.BlockSpec((tm, C), row), pl.BlockSpec((tm, C), row), pl.BlockSpec((1, C), fix),
                  pl.BlockSpec((1, C), fix)],
        out_specs=(pl.BlockSpec((tm, C), row), pl.BlockSpec((1, C), fix), pl.BlockSpec((1, C), fix)),
        compiler_params=_params(4 * _nbytes((tm, C), F32), ("arbitrary",)))(dua, u1, gamma, beta)


SB_BLOCK = 256
SB_STOP = -105.0


def _split_dot(x, tri):
    hi = x.astype(BF16)
    lo = (x - hi.astype(F32)).astype(BF16)
    return _dot(hi, tri) + _dot(lo, tri)


def _neg_softplus(z):
    return -(jnp.maximum(z, 0.0) + jnp.log(1.0 + jnp.exp(-jnp.abs(z))))


def sb_fwd(proj, *, q_col, name, rider=None):
    S = proj.shape[0]
    dh = LANES // 2
    W = SB_HEADS * dh
    npair = W // LANES
    T = min(SB_BLOCK, S)
    nblk = S // T
    scale = dh ** -0.5
    qb0 = q_col // LANES

    def body(q_ref, k_ref, v_ref, o_ref, l_ref, qs, ks, vs):
        r_i = lax.broadcasted_iota(jnp.int32, (T, T), 0)
        c_i = lax.broadcasted_iota(jnp.int32, (T, T), 1)
        tri = (r_i >= c_i).astype(BF16)
        vis = c_i < r_i
        lane = lax.broadcasted_iota(jnp.int32, (T, dh), 1)

        for hh in range(2):
            sl = slice(hh * dh, (hh + 1) * dh)
            qs[hh] = (q_ref[:, sl] * scale).astype(BF16)
            ks[hh] = k_ref[:, sl].astype(BF16)
            vs[hh] = v_ref[:, sl].astype(BF16)

        def step(qb, j0, diag, st):
            two = range(2)
            kb = [ks[hh, pl.ds(j0, T), :] for hh in two]
            vb = [vs[hh, pl.ds(j0, T), :] for hh in two]
            z = [_dot_nt(qb[hh], kb[hh]) for hh in two]
            lk = [_neg_softplus(z[hh]) for hh in two]
            if diag:
                lk = [jnp.where(vis, lk[hh], 0.0) for hh in two]
            C = [_split_dot(lk[hh], tri) for hh in two]
            A = [jnp.exp(z[hh] + C[hh] + st[2 * hh + 1]) for hh in two]
            if diag:
                A = [jnp.where(vis, A[hh], 0.0) for hh in two]
            acc = [st[2 * hh] + _dot(A[hh].astype(BF16), vb[hh]) for hh in two]
            return (acc[0], st[1] + C[0][:, 0:1], acc[1], st[3] + C[1][:, 0:1])

        def qblock(i, _):
            r0 = pl.multiple_of(i * T, T)
            qb = [qs[hh, pl.ds(r0, T), :] for hh in range(2)]
            zero = (jnp.zeros((T, dh), F32), jnp.zeros((T, 1), F32))
            state = step(qb, r0, True, zero + zero)

            def more(c):
                return (c[0] >= 0) & (jnp.max(jnp.maximum(c[2], c[4])) >= SB_STOP)

            def walk(c):
                return (c[0] - 1,) + step(qb, pl.multiple_of(c[0] * T, T), False, c[1:])

            c = lax.while_loop(more, walk, (i - 1,) + state)
            walked = (i - c[0]).astype(F32)
            for hh in range(2):
                sl = slice(hh * dh, (hh + 1) * dh)
                o_ref[pl.ds(r0, T), sl] = c[1 + 2 * hh]
                l_ref[pl.ds(r0, T), sl] = jnp.where(lane == 1, walked, c[2 + 2 * hh])
            return 0

        lax.fori_loop(0, nblk, qblock, 0)

    blk = lambda off: pl.BlockSpec((S, LANES), lambda h: (0, qb0 + off * npair + h))
    out = pl.BlockSpec((S, LANES), lambda h: (0, h))
    in_specs, out_specs, out_shape, scratch = _carry_specs(
        rider, [blk(0), blk(1), blk(2)], (out, out), (pltpu.HBM((S, W), F32), pltpu.HBM((S, W), F32)),
        [pltpu.VMEM((2, S, dh), BF16)] * 3)
    first = lambda: pl.program_id(0) == 0
    last = lambda: pl.program_id(0) == npair - 1
    res = _pallas(
        _carry(rider, body, 3, 2, first, last), name=name, grid=(npair,), out_shape=out_shape,
        in_specs=in_specs, out_specs=out_specs, scratch_shapes=scratch,
        compiler_params=_params(6 * _nbytes((S, LANES), F32), ("arbitrary",)))(
            proj, proj, proj, *(rider.arrays if rider else ()))
    return res[0], res[1], list(res[2:])


def sb_bwd(proj, ltot, dua, *, q_col, do_col, name, rider=None):
    S = proj.shape[0]
    dh = LANES // 2
    W = SB_HEADS * dh
    npair = W // LANES
    T = min(SB_BLOCK, S)
    nblk = S // T
    scale = dh ** -0.5
    qb0 = q_col // LANES
    db0 = do_col // LANES

    def body(q_ref, k_ref, v_ref, l_ref, do_ref, dq_ref, dk_ref, dv_ref, qs, ks, vs, dos, dks, dvs):
        r_i = lax.broadcasted_iota(jnp.int32, (T, T), 0)
        c_i = lax.broadcasted_iota(jnp.int32, (T, T), 1)
        tri_rev = (r_i >= c_i).astype(BF16)
        tri_fwd = (r_i <= c_i).astype(BF16)
        vis = c_i < r_i

        for hh in range(2):
            sl = slice(hh * dh, (hh + 1) * dh)
            qs[hh] = (q_ref[:, sl] * scale).astype(BF16)
            ks[hh] = k_ref[:, sl].astype(BF16)
            vs[hh] = v_ref[:, sl].astype(BF16)
            dos[hh] = do_ref[:, sl].astype(BF16)
        dks[...] = jnp.zeros_like(dks)
        dvs[...] = jnp.zeros_like(dvs)

        def step(qb, dob, Lt, j0, diag, st):
            two = range(2)
            kb = [ks[hh, pl.ds(j0, T), :] for hh in two]
            vb = [vs[hh, pl.ds(j0, T), :] for hh in two]
            z = [_dot_nt(qb[hh], kb[hh]) for hh in two]
            dA = [_dot_nt(dob[hh], vb[hh]) for hh in two]
            lk = [_neg_softplus(z[hh]) for hh in two]
            beta = [jnp.exp(z[hh] + lk[hh]) for hh in two]
            if diag:
                lk = [jnp.where(vis, lk[hh], 0.0) for hh in two]
            C = [_split_dot(lk[hh], tri_rev) for hh in two]
            rowsum = [C[hh][:, 0:1] for hh in two]
            A = [jnp.exp(z[hh] + C[hh] + (Lt[hh] - st[3 * hh + 1] - rowsum[hh])) for hh in two]
            if diag:
                A = [jnp.where(vis, A[hh], 0.0) for hh in two]
            g = [A[hh] * dA[hh] for hh in two]
            Gin = [_split_dot(g[hh], tri_fwd) for hh in two]
            dz = [g[hh] - beta[hh] * (st[3 * hh + 2] + Gin[hh]) for hh in two]
            if diag:
                dz = [jnp.where(vis, dz[hh], 0.0) for hh in two]
            dzb = [dz[hh].astype(BF16) for hh in two]
            out = ()
            for hh in two:
                dvs[hh, pl.ds(j0, T), :] += _dot_tn(A[hh].astype(BF16), dob[hh])
                dks[hh, pl.ds(j0, T), :] += _dot_tn(dzb[hh], qb[hh])
                out += (st[3 * hh] + _dot(dzb[hh], kb[hh]), st[3 * hh + 1] + rowsum[hh],
                        st[3 * hh + 2] + Gin[hh][:, T - 1:T])
            return out

        def qblock(i, _):
            r0 = pl.multiple_of(i * T, T)
            qb = [qs[hh, pl.ds(r0, T), :] for hh in range(2)]
            dob = [dos[hh, pl.ds(r0, T), :] for hh in range(2)]
            Lt = [l_ref[pl.ds(r0, T), hh * dh:hh * dh + 1] for hh in range(2)]
            walked = jnp.clip(jnp.max(l_ref[pl.ds(r0, 8), 1:2]).astype(jnp.int32), 1, i + 1)

            def inner(j, c):
                return step(qb, dob, Lt, pl.multiple_of(j * T, T), False, c)

            zero = jnp.zeros((T, 1), F32)
            init = (jnp.zeros((T, dh), F32), zero, zero)
            c = lax.fori_loop(i + 1 - walked, i, inner, init + init)
            c = step(qb, dob, Lt, r0, True, c)
            for hh in range(2):
                dq_ref[pl.ds(r0, T), hh * dh:(hh + 1) * dh] = (c[3 * hh] * scale).astype(BF16)
            return 0

        lax.fori_loop(0, nblk, qblock, 0)
        for hh in range(2):
            sl = slice(hh * dh, (hh + 1) * dh)
            dk_ref[:, sl] = dks[hh].astype(BF16)
            dv_ref[:, sl] = dvs[hh].astype(BF16)

    blk = lambda off: pl.BlockSpec((S, LANES), lambda h: (0, qb0 + off * npair + h))
    out = pl.BlockSpec((S, LANES), lambda h: (0, h))
    o_shape = pltpu.HBM((S, W), BF16)
    in_specs, out_specs, out_shape, scratch = _carry_specs(
        rider, [blk(0), blk(1), blk(2), out, pl.BlockSpec((S, LANES), lambda h: (0, db0 + h))], (out, out, out),
        (o_shape, o_shape, o_shape), [pltpu.VMEM((2, S, dh), BF16)] * 4 + [pltpu.VMEM((2, S, dh), F32)] * 2)
    first = lambda: pl.program_id(0) == 0
    last = lambda: pl.program_id(0) == npair - 1
    res = _pallas(
        _carry(rider, body, 5, 3, first, last), name=name, grid=(npair,), out_shape=out_shape,
        in_specs=in_specs, out_specs=out_specs, scratch_shapes=scratch,
        compiler_params=_params(12 * _nbytes((S, LANES), F32), ("arbitrary",)))(
            proj, proj, proj, ltot, dua, *(rider.arrays if rider else ()))
    return res[0], res[1], res[2], list(res[3:])


def xattn_fwd(q, k, v, *, tm, name):
    S, D = q.shape
    Mlen = k.shape[0]
    hd = D // MEM_HEADS
    scale = hd ** -0.5

    def body(q_ref, k_ref, v_ref, o_ref):
        for h in range(MEM_HEADS):
            sl = slice(h * hd, (h + 1) * hd)
            s = _dot_nt(q_ref[:, sl], k_ref[:, sl]) * scale
            e = jnp.exp(s - jnp.max(s, axis=-1, keepdims=True))
            p = e / jnp.sum(e, axis=-1, keepdims=True)
            o_ref[:, sl] = _dot(p.astype(BF16), v_ref[:, sl]).astype(BF16)

    row = lambda i: (i, 0)
    fix = lambda i: (0, 0)
    return _pallas(
        body, name=name, out_shape=pltpu.HBM((S, D), BF16), grid=(S // tm,),
        in_specs=[pl.BlockSpec((tm, D), row), pl.BlockSpec((Mlen, D), fix), pl.BlockSpec((Mlen, D), fix)],
        out_specs=pl.BlockSpec((tm, D), row),
        compiler_params=_params(4 * _nbytes((tm, D), F32), ("parallel",)))(q, k, v)


def xattn_bwd(q, do, k, v, *, tm, name):
    S, D = q.shape
    Mlen = k.shape[0]
    hd = D // MEM_HEADS
    scale = hd ** -0.5

    def body(q_ref, do_ref, k_ref, v_ref, dq_ref, dk_ref, dv_ref):
        @pl.when(pl.program_id(0) == 0)
        def _():
            dk_ref[...] = jnp.zeros_like(dk_ref)
            dv_ref[...] = jnp.zeros_like(dv_ref)

        for h in range(MEM_HEADS):
            sl = slice(h * hd, (h + 1) * hd)
            qh, doh, kh, vh = q_ref[:, sl], do_ref[:, sl], k_ref[:, sl], v_ref[:, sl]
            s = _dot_nt(qh, kh) * scale
            e = jnp.exp(s - jnp.max(s, axis=-1, keepdims=True))
            p = e / jnp.sum(e, axis=-1, keepdims=True)
            dp = _dot_nt(doh, vh)
            ds = (p * (dp - jnp.sum(p * dp, axis=-1, keepdims=True)) * scale).astype(BF16)
            dq_ref[:, sl] = _dot(ds, kh).astype(BF16)
            dk_ref[:, sl] += _dot_tn(ds, qh)
            dv_ref[:, sl] += _dot_tn(p.astype(BF16), doh)

    row = lambda i: (i, 0)
    fix = lambda i: (0, 0)
    return _pallas(
        body, name=name, grid=(S // tm,),
        out_shape=(pltpu.HBM((S, D), BF16), pltpu.HBM((Mlen, D), F32),
                   pltpu.HBM((Mlen, D), F32)),
        in_specs=[pl.BlockSpec((tm, D), row), pl.BlockSpec((tm, D), row), pl.BlockSpec((Mlen, D), fix),
                  pl.BlockSpec((Mlen, D), fix)],
        out_specs=(pl.BlockSpec((tm, D), row), pl.BlockSpec((Mlen, D), fix), pl.BlockSpec((Mlen, D), fix)),
        compiler_params=_params(6 * _nbytes((tm, D), F32), ("arbitrary",)))(q, do, k, v)


FFN_HALO = 8


def _conv3(ext, w, lo):
    tm = ext.shape[0] - FFN_HALO
    return (w[0:1, :] * ext[lo:lo + tm, :] + w[1:2, :] * ext[lo + 1:lo + 1 + tm, :]
            + w[2:3, :] * ext[lo + 2:lo + 2 + tm, :])


def ffn_up_fwd(xb, w_up, conv_w, conv_b, *, tm, tn, name, rider=None):
    S, D = xb.shape
    nsh, _, ns = w_up.shape
    F = nsh * ns // 2
    per = ns // tn
    ncol = F // tn
    KW = conv_w.shape[0]
    assert KW == 3

    def body(x_ref, wv_ref, wg_ref, cwv_ref, cwg_ref, cbv_ref, cbg_ref, uv_ref, ug_ref, cv_ref, cg_ref, h_ref,
             carry):
        @pl.when(pl.program_id(1) == 0)
        def _():
            carry[...] = jnp.zeros_like(carry)

        x = x_ref[...]
        uv = _dot(x, wv_ref[...])
        ug = _dot(x, wg_ref[...])
        uv_ref[...] = uv.astype(BF16)
        ug_ref[...] = ug.astype(BF16)
        lo = FFN_HALO - (KW - 1)
        cv = _conv3(jnp.concatenate([carry[0], uv], axis=0), cwv_ref[...], lo) + cbv_ref[...]
        cg = _conv3(jnp.concatenate([carry[1], ug], axis=0), cwg_ref[...], lo) + cbg_ref[...]
        carry[0] = uv[tm - FFN_HALO:, :]
        carry[1] = ug[tm - FFN_HALO:, :]
        cv_ref[...] = cv.astype(BF16)
        cg_ref[...] = cg.astype(BF16)
        h_ref[...] = (cg * _sigmoid(cg) * cv).astype(BF16)

    wspec = lambda half: pl.BlockSpec((None, D, tn), lambda j, i: (half * (nsh // 2) + j // per, 0, j % per))
    cspec = lambda rows, half: pl.BlockSpec((rows, tn), lambda j, i: (0, half * ncol + j))
    out = pl.BlockSpec((tm, tn), lambda j, i: (i, j))
    o_shape = pltpu.HBM((S, F), BF16)
    blk = _nbytes((tm, D), BF16) + 2 * _nbytes((D, tn), BF16) + 8 * _nbytes((tm, tn), F32)
    nrow = S // tm
    in_specs, out_specs, out_shape, scratch = _carry_specs(
        rider, [pl.BlockSpec((tm, D), lambda j, i: (i, 0)), wspec(0), wspec(1), cspec(KW, 0), cspec(KW, 1),
                cspec(1, 0), cspec(1, 1)], (out,) * 5, (o_shape,) * 5, [pltpu.VMEM((2, FFN_HALO, tn), F32)])
    first = lambda: (pl.program_id(0) == 0) & (pl.program_id(1) == 0)
    last = lambda: (pl.program_id(0) == ncol - 1) & (pl.program_id(1) == nrow - 1)
    res = _pallas(
        _carry(rider, body, 7, 5, first, last), name=name, grid=(ncol, nrow), out_shape=out_shape,
        in_specs=in_specs, out_specs=out_specs, scratch_shapes=scratch,
        compiler_params=_params(blk, ("arbitrary", "arbitrary")))(
            xb, w_up, w_up, conv_w, conv_w, conv_b, conv_b, *(rider.arrays if rider else ()))
    return res[:5], list(res[5:])


def ffn_mid_bwd(dzb, w_down, up_v, up_g, conv_v, conv_g, conv_w, *, tm, tn, name, rider=None):
    S, D = dzb.shape
    F = up_v.shape[1]
    ncol = F // tn
    nrow = S // tm
    KW = conv_w.shape[0]
    assert KW == 3

    def body(dz_ref, wd_ref, uv_ref, ug_ref, cv_ref, cg_ref, cwv_ref, cwg_ref,
             dv_ref, dg_ref, dwv_ref, dwg_ref, dbv_ref, dbg_ref, carry):
        @pl.when(pl.program_id(1) == 0)
        def _():
            carry[...] = jnp.zeros_like(carry)
            for r in (dwv_ref, dwg_ref, dbv_ref, dbg_ref):
                r[...] = jnp.zeros_like(r)

        cv, cg = cv_ref[...].astype(F32), cg_ref[...].astype(F32)
        dh = _dot_nt(dz_ref[...], wd_ref[...])
        sg = _sigmoid(cg)
        dcv = dh * (cg * sg)
        dcg = dh * cv * (sg * (1.0 + cg * (1.0 - sg)))

        def back(dc, u_ref, cw, slot, du_ref, dw_ref, db_ref):
            ext = jnp.concatenate([dc, carry[slot]], axis=0)
            ahead = [dc, ext[1:tm + 1, :], ext[2:tm + 2, :]]
            du = cw[2:3, :] * ahead[0] + cw[1:2, :] * ahead[1] + cw[0:1, :] * ahead[2]
            du_ref[...] = du.astype(BF16)
            carry[slot] = dc[0:FFN_HALO, :]
            u = u_ref[...].astype(F32)
            for k in range(KW):
                dw_ref[k:k + 1, :] += jnp.sum(ahead[KW - 1 - k] * u, axis=0, keepdims=True)
            db_ref[...] += jnp.sum(dc, axis=0, keepdims=True)

        back(dcv, uv_ref, cwv_ref[...], 0, dv_ref, dwv_ref, dbv_ref)
        back(dcg, ug_ref, cwg_ref[...], 1, dg_ref, dwg_ref, dbg_ref)

    rev = lambda i: nrow - 1 - i
    tile = pl.BlockSpec((tm, tn), lambda j, i: (rev(i), j))
    cspec = lambda half: pl.BlockSpec((KW, tn), lambda j, i: (0, half * ncol + j))
    acc = lambda rows: pl.BlockSpec((rows, tn), lambda j, i: (0, j))
    big = pltpu.HBM((S, F), BF16)
    blk = _nbytes((tm, D), BF16) + _nbytes((tn, D), BF16) + 10 * _nbytes((tm, tn), F32)
    in_specs, out_specs, out_shape, scratch = _carry_specs(
        rider, [pl.BlockSpec((tm, D), lambda j, i: (rev(i), 0)), pl.BlockSpec((tn, D), lambda j, i: (j, 0)),
                tile, tile, tile, tile, cspec(0), cspec(1)],
        (tile, tile, acc(KW), acc(KW), acc(1), acc(1)),
        (big, big, pltpu.HBM((KW, F), F32), pltpu.HBM((KW, F), F32), pltpu.HBM((1, F), F32),
         pltpu.HBM((1, F), F32)), [pltpu.VMEM((2, FFN_HALO, tn), F32)])
    first = lambda: (pl.program_id(0) == 0) & (pl.program_id(1) == 0)
    last = lambda: (pl.program_id(0) == ncol - 1) & (pl.program_id(1) == nrow - 1)
    res = _pallas(
        _carry(rider, body, 8, 6, first, last), name=name, grid=(ncol, nrow), out_shape=out_shape,
        in_specs=in_specs, out_specs=out_specs, scratch_shapes=scratch,
        compiler_params=_params(blk, ("arbitrary", "arbitrary")))(
            dzb, w_down, up_v, up_g, conv_v, conv_g, conv_w, conv_w, *(rider.arrays if rider else ()))
    return res[:6], list(res[6:])


def loss_head(y, target, *, tm, name):
    S, D = y.shape

    def body(y_ref, t_ref, dy_ref, l_ref):
        @pl.when(pl.program_id(0) == 0)
        def _():
            l_ref[...] = jnp.zeros_like(l_ref)

        e = y_ref[...] - t_ref[...]
        dy_ref[...] = e * (1.0 / D)
        l_ref[...] += 0.5 * jnp.sum(jnp.mean(e * e, axis=-1, keepdims=True), axis=0, keepdims=True)

    row = lambda i: (i, 0)
    return _pallas(
        body, name=name, grid=(S // tm,),
        out_shape=(pltpu.HBM((S, D), F32), pltpu.HBM((1, 1), F32)),
        in_specs=[pl.BlockSpec((tm, D), row), pl.BlockSpec((tm, D), row)],
        out_specs=(pl.BlockSpec((tm, D), row), pl.BlockSpec((1, 1), lambda i: (0, 0))),
        compiler_params=_params(3 * _nbytes((tm, D), F32), ("arbitrary",)))(y, target)


def adamw(w, g, m, v, *, tr, name):
    R, C = w.shape
    c1 = 1.0 - ADAM_B1 ** ADAM_STEP
    c2 = 1.0 - ADAM_B2 ** ADAM_STEP

    def body(w_ref, g_ref, m_ref, v_ref, go_ref, d_ref, mo_ref, vo_ref):
        gv = g_ref[...]
        mn = ADAM_B1 * m_ref[...] + (1.0 - ADAM_B1) * gv
        vn = ADAM_B2 * v_ref[...] + (1.0 - ADAM_B2) * (gv * gv)
        go_ref[...] = gv
        mo_ref[...] = mn
        vo_ref[...] = vn
        d_ref[...] = -ADAM_LR * ((mn / c1) / (jnp.sqrt(vn / c2) + ADAM_EPS) + ADAM_WD * w_ref[...])

    spec = pl.BlockSpec((tr, C), lambda i: (i, 0))
    shape = pltpu.HBM((R, C), F32)
    return _pallas(
        body, name=name, grid=(R // tr,), out_shape=(shape,) * 4, in_specs=[spec] * 4, out_specs=(spec,) * 4,
        compiler_params=_params(8 * _nbytes((tr, C), F32), ("parallel",)))(w, g, m, v)


def add_pairs(gs, gots, core, *, name):
    k = len(gs)

    def body(c_ref, *refs):
        for a_ref, b_ref, o_ref in zip(refs[:k], refs[k:2 * k], refs[2 * k:]):
            o_ref[...] = (a_ref[...].astype(F32) + b_ref[...].astype(F32)).astype(BF16)

    own = [pl.BlockSpec((None, None) + g.shape[2:], lambda i, c: (i, c[0], 0, 0)) for g in gs]
    half = [pl.BlockSpec((None,) + g.shape[1:], lambda i, c: (i, 0, 0)) for g in gots]
    grid_spec = pltpu.PrefetchScalarGridSpec(
        num_scalar_prefetch=1, grid=(N_CHIPS,), in_specs=own + half, out_specs=tuple(half))
    blk = 3 * sum(_nbytes(g.shape[1:], BF16) for g in gots)
    return _pallas(
        body, name=name, grid_spec=grid_spec, out_shape=tuple(pltpu.HBM(g.shape, BF16) for g in gots),
        compiler_params=_params(blk, ("parallel",)))(core, *gs, *gots)


def sum_chips_into(bs, dests, layer, core, *, name):
    k = len(bs)
    steps = 2

    def body(c_ref, *refs):
        for b_ref, o_ref in zip(refs[:k], refs[2 * k:]):
            acc = b_ref[0].astype(F32)
            for p in range(1, N_CHIPS):
                acc = acc + b_ref[p].astype(F32)
            o_ref[...] = acc

    ins = [pl.BlockSpec((N_CHIPS, b.shape[1] // steps, b.shape[2]), lambda i, c: (0, i, 0)) for b in bs]
    outs = tuple(pl.BlockSpec((None, None, b.shape[1] // steps, b.shape[2]), lambda i, c: (layer, c[0], i, 0))
                 for b in bs)
    grid_spec = pltpu.PrefetchScalarGridSpec(
        num_scalar_prefetch=1, grid=(steps,), in_specs=ins + [pl.BlockSpec(memory_space=pl.ANY)] * k,
        out_specs=outs)
    blk = sum(_nbytes(b.shape, BF16) + _nbytes(b.shape[1:], F32) for b in bs) // steps
    return _pallas(
        body, name=name, grid_spec=grid_spec, out_shape=tuple(pltpu.HBM(d.shape, F32) for d in dests),
        input_output_aliases={1 + k + w: w for w in range(k)},
        compiler_params=_params(blk, ("parallel",)))(core, *bs, *dests)


_HBM = pl.BlockSpec(memory_space=pltpu.HBM)


def _place():
    x, y, c = lax.axis_index("x"), lax.axis_index("y"), lax.axis_index("c")
    chips = [(1 - x, y), (x, 1 - y), (1 - x, 1 - y)]
    return x, y, c, chips


class GatherRider:
    def __init__(self, shards):
        self.arrays = list(shards)
        self.n = n = len(shards)
        self.out_shape = tuple(pltpu.HBM((N_CHIPS,) + s.shape, s.dtype) for s in shards)
        self.scratch = [pltpu.SemaphoreType.DMA((n, 3))] * 4 + [pltpu.SemaphoreType.DMA((n,))]

    def _copies(self, ins, outs, sems):
        send_ici, recv_ici, send_d2d, recv_d2d, local = sems
        x, y, c, chips = _place()
        me = 2 * x + y

        def own(w):
            return pltpu.make_async_copy(ins[w], outs[w].at[me], local.at[w])

        def ici(w, j):
            px, py = chips[j]
            return pltpu.make_async_remote_copy(
                src_ref=ins[w].at[c], dst_ref=outs[w].at[me, c], send_sem=send_ici.at[w, j],
                recv_sem=recv_ici.at[w, j], device_id=(px, py, c), device_id_type=MESH)

        def landed(w, j, half):
            px, py = chips[j]
            return outs[w].at[2 * px + py, half]

        def d2d(w, j, half):
            return pltpu.make_async_remote_copy(
                src_ref=landed(w, j, half), dst_ref=landed(w, j, half), send_sem=send_d2d.at[w, j],
                recv_sem=recv_d2d.at[w, j], device_id=(x, y, 1 - c), device_id_type=MESH)

        def ici_arrival(w, j):
            return pltpu.make_async_remote_copy(
                src_ref=landed(w, j, c), dst_ref=landed(w, j, c), send_sem=send_ici.at[w, j],
                recv_sem=recv_ici.at[w, j], device_id=(x, y, c), device_id_type=MESH)

        return c, own, ici, d2d, ici_arrival

    def start(self, ins, outs, sems):
        c, own, ici, d2d, ici_arrival = self._copies(ins, outs, sems)
        for w in range(self.n):
            own(w).start()
            for j in range(3):
                ici(w, j).start()

    def finish(self, ins, outs, sems):
        c, own, ici, d2d, ici_arrival = self._copies(ins, outs, sems)
        for w in range(self.n):
            for j in range(3):
                ici_arrival(w, j).wait_recv()
                d2d(w, j, c).start()
        for w in range(self.n):
            for j in range(3):
                d2d(w, j, 1 - c).wait_recv()
        for w in range(self.n):
            for j in range(3):
                ici(w, j).wait_send()
                d2d(w, j, c).wait_send()
            own(w).wait()


class ScatterRider:
    def __init__(self, parts):
        self.arrays = list(parts)
        self.n = n = len(parts)
        self.out_shape = tuple(pltpu.HBM(p.shape, p.dtype) for p in parts)
        self.scratch = [pltpu.SemaphoreType.DMA((n, 3))] * 2 + [pltpu.SemaphoreType.DMA((n,))]

    def _copies(self, ins, outs, sems):
        send, recv, local = sems
        x, y, c, chips = _place()
        me = 2 * x + y

        def own(w):
            return pltpu.make_async_copy(ins[w].at[me], outs[w].at[me], local.at[w])

        def copy(w, j):
            px, py = chips[j]
            return pltpu.make_async_remote_copy(
                src_ref=ins[w].at[2 * px + py], dst_ref=outs[w].at[me], send_sem=send.at[w, j],
                recv_sem=recv.at[w, j], device_id=(px, py, c), device_id_type=MESH)

        def arrival(w, j):
            px, py = chips[j]
            blk = outs[w].at[2 * px + py]
            return pltpu.make_async_remote_copy(
                src_ref=blk, dst_ref=blk, send_sem=send.at[w, j], recv_sem=recv.at[w, j],
                device_id=(x, y, c), device_id_type=MESH)

        return own, copy, arrival

    def start(self, ins, outs, sems):
        own, copy, arrival = self._copies(ins, outs, sems)
        for w in range(self.n):
            own(w).start()
            for j in range(3):
                copy(w, j).start()

    def finish(self, ins, outs, sems):
        own, copy, arrival = self._copies(ins, outs, sems)
        for w in range(self.n):
            for j in range(3):
                arrival(w, j).wait_recv()
        for w in range(self.n):
            for j in range(3):
                copy(w, j).wait_send()
            own(w).wait()


def _carry(rider, body, n_in, n_out, first, last):
    if rider is None:
        return body
    k, m = rider.n, len(rider.scratch)

    def carried(*refs):
        ins, r_in = refs[:n_in], refs[n_in:n_in + k]
        outs, r_out = refs[n_in + k:n_in + k + n_out], refs[n_in + k + n_out:n_in + 2 * k + n_out]
        rest = refs[n_in + 2 * k + n_out:]
        scratch, sems = rest[:len(rest) - m], rest[len(rest) - m:]

        @pl.when(first())
        def _():
            rider.start(r_in, r_out, sems)

        body(*ins, *outs, *scratch)

        @pl.when(last())
        def _():
            rider.finish(r_in, r_out, sems)

    return carried


def _carry_specs(rider, in_specs, out_specs, out_shape, scratch):
    if rider is None:
        return list(in_specs), tuple(out_specs), tuple(out_shape), list(scratch)
    k = rider.n
    return (list(in_specs) + [_HBM] * k, tuple(out_specs) + (_HBM,) * k, tuple(out_shape) + rider.out_shape,
            list(scratch) + list(rider.scratch))


def run_rider(rider, *, name):
    k = rider.n

    def body(*refs):
        rider.start(refs[:k], refs[k:2 * k], refs[2 * k:])
        rider.finish(refs[:k], refs[k:2 * k], refs[2 * k:])

    return _pallas(body, name=name, out_shape=rider.out_shape, in_specs=[_HBM] * k, out_specs=(_HBM,) * k,
                   scratch_shapes=rider.scratch)(*rider.arrays)


def allgather_small(shards, *, name):
    n = len(shards)

    def body(*refs):
        ins, outs = refs[:n], refs[n:2 * n]
        send, recv, local = refs[2 * n:]
        x, y, c, chips = _place()
        me = 2 * x + y
        locals_ = [pltpu.make_async_copy(ins[w], outs[w].at[me], local.at[w]) for w in range(n)]
        for cp in locals_:
            cp.start()

        def copy(w, j):
            px, py = chips[j]
            return pltpu.make_async_remote_copy(
                src_ref=ins[w], dst_ref=outs[w].at[me], send_sem=send.at[w, j], recv_sem=recv.at[w, j],
                device_id=(px, py, c), device_id_type=MESH)

        def arrival(w, j):
            px, py = chips[j]
            blk = outs[w].at[2 * px + py]
            return pltpu.make_async_remote_copy(
                src_ref=blk, dst_ref=blk, send_sem=send.at[w, j], recv_sem=recv.at[w, j],
                device_id=(x, y, c), device_id_type=MESH)

        for w in range(n):
            for j in range(3):
                copy(w, j).start()
        for w in range(n):
            for j in range(3):
                arrival(w, j).wait_recv()
        for w in range(n):
            for j in range(3):
                copy(w, j).wait_send()
        for cp in locals_:
            cp.wait()

    out_shape = tuple(pltpu.HBM((N_CHIPS,) + s.shape, s.dtype) for s in shards)
    return _pallas(
        body, name=name, out_shape=out_shape, in_specs=[_HBM] * n, out_specs=(_HBM,) * n,
        scratch_shapes=[pltpu.SemaphoreType.DMA((n, 3))] * 2 + [pltpu.SemaphoreType.DMA((n,))],
    )(*shards)


class SwapRider:
    def __init__(self, grads):
        self.arrays = list(grads)
        self.n = n = len(grads)
        self.out_shape = tuple(pltpu.HBM((N_CHIPS,) + g.shape[2:], g.dtype) for g in grads)
        self.scratch = [pltpu.SemaphoreType.DMA((n,))] * 2

    def _copies(self, ins, outs, sems):
        send, recv = sems
        x, y, c, _ = _place()
        return [pltpu.make_async_remote_copy(
            src_ref=ins[w].at[:, 1 - c], dst_ref=outs[w], send_sem=send.at[w], recv_sem=recv.at[w],
            device_id=(x, y, 1 - c), device_id_type=MESH) for w in range(self.n)]

    def start(self, ins, outs, sems):
        for cp in self._copies(ins, outs, sems):
            cp.start()

    def finish(self, ins, outs, sems):
        copies = self._copies(ins, outs, sems)
        for cp in copies:
            cp.wait_recv()
        for cp in copies:
            cp.wait_send()


def rs_sibling_share(stacked, *, name):
    n = len(stacked)

    def body(*refs):
        bufs = refs[n:2 * n]
        send, recv = refs[2 * n:]
        x, y, c, _ = _place()
        shares, arrivals = [], []
        for w in range(n):
            mine, other = bufs[w].at[:, c], bufs[w].at[:, 1 - c]
            shares.append(pltpu.make_async_remote_copy(
                src_ref=mine, dst_ref=mine, send_sem=send.at[w], recv_sem=recv.at[w],
                device_id=(x, y, 1 - c), device_id_type=MESH))
            arrivals.append(pltpu.make_async_remote_copy(
                src_ref=other, dst_ref=other, send_sem=send.at[w], recv_sem=recv.at[w],
                device_id=(x, y, c), device_id_type=MESH))
        for cp in shares:
            cp.start()
        for cp in arrivals:
            cp.wait_recv()
        for cp in shares:
            cp.wait_send()

    out_shape = tuple(pltpu.HBM(s.shape, F32) for s in stacked)
    return _pallas(
        body, name=name, out_shape=out_shape, in_specs=[_HBM] * n, out_specs=(_HBM,) * n,
        input_output_aliases={w: w for w in range(n)},
        scratch_shapes=[pltpu.SemaphoreType.DMA((n,))] * 2,
    )(*stacked)


def allreduce_small(v, *, name):
    R, C = v.shape

    def body(v_ref, o_ref, land, send, recv):
        x, y, c, _ = _place()
        me = 4 * x + 2 * y + c
        land[me] = v_ref[...]

        def flip(k):
            return (1 - x) if k & 4 else x, (1 - y) if k & 2 else y, (1 - c) if k & 1 else c

        copies = []
        for k in range(1, N_DEV):
            px, py, pc = flip(k)
            copies.append(pltpu.make_async_remote_copy(
                src_ref=v_ref, dst_ref=land.at[me], send_sem=send.at[k - 1], recv_sem=recv.at[k - 1],
                device_id=(px, py, pc), device_id_type=MESH))
        for cp in copies:
            cp.start()
        for k in range(1, N_DEV):
            px, py, pc = flip(k)
            blk = land.at[4 * px + 2 * py + pc]
            pltpu.make_async_remote_copy(
                src_ref=blk, dst_ref=blk, send_sem=send.at[k - 1], recv_sem=recv.at[k - 1],
                device_id=(x, y, c), device_id_type=MESH).wait_recv()
        for cp in copies:
            cp.wait_send()
        acc = land[0]
        for d in range(1, N_DEV):
            acc = acc + land[d]
        o_ref[...] = acc

    vm = pl.BlockSpec(memory_space=pltpu.VMEM)
    return pl.pallas_call(
        body, name=name, out_shape=jax.ShapeDtypeStruct((R, C), F32), in_specs=[vm], out_specs=vm,
        scratch_shapes=[pltpu.VMEM((N_DEV, R, C), F32), pltpu.SemaphoreType.DMA((N_DEV - 1,)),
                        pltpu.SemaphoreType.DMA((N_DEV - 1,))],
        compiler_params=pltpu.CompilerParams(vmem_limit_bytes=int(min(12 * R * C * 4 + (8 << 20), VMEM_CAP))),
    )(v)


def _pack(arrays):
    flat = jnp.concatenate([a.reshape(-1) for a in arrays])
    return flat.reshape(-1, LANES)


def _unpack(packed, shapes):
    flat = packed.reshape(-1)
    out, off = [], 0
    for s in shapes:
        n = 1
        for d in s:
            n *= d
        out.append(flat[off:off + n].reshape(s))
        off += n
    return out


def _row_tile(rows, cap=512):
    t = 1 << (cap.bit_length() - 1)
    while rows % t:
        t //= 2
    return t


def _adamw_tile(rows, cols):
    return _row_tile(rows, max(8, (1 << 20) // (4 * cols)))


def kernel(x, mem, w_in, conv_w, conv_b, conv_ln_g, conv_ln_b, w_out, ln1_g, ln1_b, mem_wq, mem_wk, mem_wv, mem_wo, ln2_g, ln2_b, ffn_up, ffn_conv_w, ffn_conv_b, ffn_down, ln3_g, ln3_b, loss_target, m_w_in, m_conv_w, m_conv_b, m_conv_ln_g, m_conv_ln_b, m_w_out, m_ln1_g, m_ln1_b, m_mem_wq, m_mem_wk, m_mem_wv, m_mem_wo, m_ln2_g, m_ln2_b, m_ffn_up, m_ffn_conv_w, m_ffn_conv_b, m_ffn_down, m_ln3_g, m_ln3_b, v_w_in, v_conv_w, v_conv_b, v_conv_ln_g, v_conv_ln_b, v_w_out, v_ln1_g, v_ln1_b, v_mem_wq, v_mem_wk, v_mem_wv, v_mem_wo, v_ln2_g, v_ln2_b, v_ffn_up, v_ffn_conv_w, v_ffn_conv_b, v_ffn_down, v_ln3_g, v_ln3_b):
    W = dict(w_in=w_in, conv_w=conv_w, conv_b=conv_b, conv_ln_g=conv_ln_g, conv_ln_b=conv_ln_b, w_out=w_out,
             ln1_g=ln1_g, ln1_b=ln1_b, mem_wq=mem_wq, mem_wk=mem_wk, mem_wv=mem_wv, mem_wo=mem_wo, ln2_g=ln2_g,
             ln2_b=ln2_b, ffn_up=ffn_up, ffn_conv_w=ffn_conv_w, ffn_conv_b=ffn_conv_b, ffn_down=ffn_down,
             ln3_g=ln3_g, ln3_b=ln3_b)
    M1 = dict(w_in=m_w_in, conv_w=m_conv_w, conv_b=m_conv_b, conv_ln_g=m_conv_ln_g, conv_ln_b=m_conv_ln_b,
              w_out=m_w_out, ln1_g=m_ln1_g, ln1_b=m_ln1_b, mem_wq=m_mem_wq, mem_wk=m_mem_wk, mem_wv=m_mem_wv,
              mem_wo=m_mem_wo, ln2_g=m_ln2_g, ln2_b=m_ln2_b, ffn_up=m_ffn_up, ffn_conv_w=m_ffn_conv_w,
              ffn_conv_b=m_ffn_conv_b, ffn_down=m_ffn_down, ln3_g=m_ln3_g, ln3_b=m_ln3_b)
    V2 = dict(w_in=v_w_in, conv_w=v_conv_w, conv_b=v_conv_b, conv_ln_g=v_conv_ln_g, conv_ln_b=v_conv_ln_b,
              w_out=v_w_out, ln1_g=v_ln1_g, ln1_b=v_ln1_b, mem_wq=v_mem_wq, mem_wk=v_mem_wk, mem_wv=v_mem_wv,
              mem_wo=v_mem_wo, ln2_g=v_ln2_g, ln2_b=v_ln2_b, ffn_up=v_ffn_up, ffn_conv_w=v_ffn_conv_w,
              ffn_conv_b=v_ffn_conv_b, ffn_down=v_ffn_down, ln3_g=v_ln3_g, ln3_b=v_ln3_b)

    L = w_in.shape[0]
    S, D = x.shape[1], x.shape[2]
    C = conv_b.shape[1]
    alpha = (2.0 * L) ** 0.25
    chip = 2 * lax.axis_index("x") + lax.axis_index("y")
    xs, mems, tgt = x[0], mem[0], loss_target[0]
    mem_bf = mems.astype(BF16)
    tm = _row_tile(S)
    tm_ffn = _row_tile(S, 256)
    tm_big = _row_tile(S, 1024)

    def shards_of(l, names):
        out = []
        for n in names:
            wl = W[n][l].astype(BF16)
            out.append(wl.reshape(2, wl.shape[0] // 2, wl.shape[1]))
        return out

    def gathered(names, got):
        layer = {}
        for n, g in zip(names, got):
            rows, cols = W[n].shape[1], W[n].shape[2]
            layer[n] = g.reshape(N_CHIPS, rows, cols) if n in COL_SHARDED else g.reshape(N_CHIPS * rows, cols)
        return layer

    full = [dict() for _ in range(L)]
    full[0].update(gathered(RIDE_IN, run_rider(GatherRider(shards_of(0, RIDE_IN)), name="allgather_w_in")))
    cw_all, fcw_all = allgather_small([conv_w, ffn_conv_w], name="allgather_small")
    cw_full = jnp.transpose(cw_all, (1, 2, 0, 3)).reshape(L, conv_w.shape[1], -1)
    fcw_full = jnp.transpose(fcw_all, (1, 2, 0, 3)).reshape(L, ffn_conv_w.shape[1], -1)

    saved = []
    h, hb = xs, xs.astype(BF16)
    for l in range(L):
        fw = full[l]
        s = dict(x=h, xb=hb)
        s['proj'] = mm_nn(hb, fw['w_in'], F32, tm=min(1024, S), tn=fw['w_in'].shape[2], name="proj")
        s['u1'], got = conv_fwd(s['proj'], cw_full[l], conv_b[l][None], name="conv_fwd",
                                rider=GatherRider(shards_of(0, RIDE_ATT)) if l == 0 else None)
        if l == 0:
            fw.update(gathered(RIDE_ATT, got))
        more = l + 1 < L
        s['o_sb'], s['ltot'], got = sb_fwd(
            s['proj'], q_col=2 * C, name="sb_fwd",
            rider=GatherRider(shards_of(l, RIDE_FFN) + (shards_of(l + 1, RIDE_IN) if more else [])))
        fw.update(gathered(RIDE_FFN, got[:len(RIDE_FFN)]))
        if more:
            full[l + 1].update(gathered(RIDE_IN, got[len(RIDE_FFN):]))
        s['ua'] = ln_silu(s['u1'], s['o_sb'], conv_ln_g[l][None], conv_ln_b[l][None], tm=tm, name="ln_silu")
        s['x1'], s['x1b'], s['zh1'], s['rs1'] = mm_ln(
            s['ua'], fw['w_out'], h, ln1_g[l][None], ln1_b[l][None], alpha, tm=tm, name="out_proj_ln")
        s['q2'] = mm_nn(s['x1b'], fw['mem_wq'], BF16, tm=min(1024, S), tn=512, name="mem_q")
        s['k2'] = mm_nn(mem_bf, fw['mem_wk'], BF16, tm=mem_bf.shape[0], tn=512, name="mem_kv")
        s['v2'] = mm_nn(mem_bf, fw['mem_wv'], BF16, tm=mem_bf.shape[0], tn=512, name="mem_kv")
        s['o2'] = xattn_fwd(s['q2'], s['k2'], s['v2'], tm=tm, name="xattn_fwd")
        s['x2'], s['x2b'], s['zh2'], s['rs2'] = mm_ln(
            s['o2'], fw['mem_wo'], s['x1'], ln2_g[l][None], ln2_b[l][None], alpha, tm=tm, name="mem_o_ln")
        (s['upv'], s['upg'], s['cv'], s['cg'], s['hmid']), got = ffn_up_fwd(
            s['x2b'], fw['ffn_up'], fcw_full[l], ffn_conv_b[l][None], tm=tm_ffn, tn=fw['ffn_up'].shape[2],
            name="ffn_up_fwd", rider=GatherRider(shards_of(l + 1, RIDE_ATT)) if more else None)
        if more:
            full[l + 1].update(gathered(RIDE_ATT, got))
        h, hb, s['zh3'], s['rs3'] = mm_ln(
            s['hmid'], fw['ffn_down'], s['x2'], ln3_g[l][None], ln3_b[l][None], alpha, tm=tm, name="ffn_down_ln")
        saved.append(s)

    dx, loss_part = loss_head(h, tgt, tm=tm, name="loss_head")
    loss = lax.psum(loss_part[0, 0], ("x", "y", "c"))

    core = lax.axis_index("c").astype(jnp.int32).reshape(1)
    reduced_big = {n: lax.empty((L, 2, W[n].shape[1] // 2, W[n].shape[2]), F32) for n in BIG}
    small_grads = [None] * L

    def row_halves(g, names):
        parts = []
        for n in names:
            rows, cols = W[n].shape[1], W[n].shape[2]
            parts.append(g[n].reshape(N_CHIPS, 2, rows // 2, cols))
        return parts

    def pre_add(g, names):
        parts = row_halves(g, names)
        got = run_rider(SwapRider(parts), name="rs_sibling_swap")
        return list(add_pairs(parts, got, core, name="rs_add_pairs"))

    def reduce_into(names, scattered, layer):
        reduced_big.update(zip(names, sum_chips_into(
            list(scattered), [reduced_big[n] for n in names], layer, core, name="rs_sum_chips")))

    pending = None
    for l in reversed(range(L)):
        fw, s = full[l], saved[l]
        g = {}
        if l == L - 1:
            top = ln_bwd(dx, s['zh3'], s['rs3'], ln3_g[l][None], tm=tm, name="ln_bwd")
        dz3, dz3b, g['ln3_g'], g['ln3_b'] = top
        ftn = fw['ffn_up'].shape[2]
        (dupv, dupg, dfw_v, dfw_g, dfb_v, dfb_g), sc = ffn_mid_bwd(
            dz3b, fw['ffn_down'], s['upv'], s['upg'], s['cv'], s['cg'], fcw_full[l], tm=tm_ffn, tn=ftn,
            name="ffn_mid_bwd", rider=ScatterRider(pending) if pending else None)
        if pending:
            reduce_into(RIDE_MIX, sc, l + 1)
        g['ffn_conv_w'] = jnp.concatenate([dfw_v, dfw_g], axis=1)
        g['ffn_conv_b'] = jnp.concatenate([dfb_v, dfb_g], axis=1)[0]
        g['ffn_down'] = mm_tn(s['hmid'], [dz3b], tk=ftn, tn=512, tmc=min(1024, S), name="grad_ffn_down")
        dz2, dz2b, g['ln2_g'], g['ln2_b'] = mm_nt_ln_bwd(
            [dupv, dupg], fw['ffn_up'], dz3, alpha, s['zh2'], s['rs2'], ln2_g[l][None], tm=tm_ffn,
            name="ffn_up_bwd")
        g['ffn_up'] = mm_tn(s['x2b'], [dupv, dupg], tk=512, tn=ftn, shard_width=ftn, tmc=min(1024, S),
                            name="grad_ffn_up")

        do2 = mm_nt([dz2b], fw['mem_wo'], BF16, tm=tm_big, tk=512, name="mem_o_bwd")
        g['mem_wo'] = mm_tn(s['o2'], [dz2b], tk=512, tn=512, name="grad_sq")
        dq2, dk2, dv2 = xattn_bwd(s['q2'], do2, s['k2'], s['v2'], tm=tm, name="xattn_bwd")
        dz1, dz1b, g['ln1_g'], g['ln1_b'] = mm_nt_ln_bwd(
            [dq2], fw['mem_wq'], dz2, alpha, s['zh1'], s['rs1'], ln1_g[l][None], tm=tm, name="mem_q_bwd")
        g['mem_wq'] = mm_tn(s['x1b'], [dq2], tk=512, tn=512, name="grad_sq")
        g['mem_wk'] = mm_tn(mem_bf, [dk2], tk=512, tn=512, name="grad_mem_kv")
        g['mem_wv'] = mm_tn(mem_bf, [dv2], tk=512, tn=512, name="grad_mem_kv")

        rest = row_halves(g, RIDE_REST)
        dua, got = mm_nt([dz1b], fw['w_out'], F32, tm=tm_big, tk=512, name="out_proj_bwd", rider=SwapRider(rest))
        rest = list(add_pairs(rest, got, core, name="rs_add_pairs"))
        g['w_out'] = mm_tn(s['ua'], [dz1b], tk=512, tn=512, name="grad_sq")
        dq, dk, dv, sc = sb_bwd(
            s['proj'], s['ltot'], dua, q_col=2 * C, do_col=C, name="sb_bwd",
            rider=ScatterRider(rest))
        reduce_into(RIDE_REST, sc, l)
        du1, g['conv_ln_g'], g['conv_ln_b'] = ln_silu_bwd(
            dua, s['u1'], conv_ln_g[l][None], conv_ln_b[l][None], tm=tm, name="ln_silu_bwd")
        da, dg, g['conv_w'], dcb = conv_bwd(du1, s['proj'], cw_full[l], name="conv_bwd")
        g['conv_b'] = dcb
        dproj = jnp.concatenate([da, dg, dq, dk, dv], axis=1)
        ns_in = fw['w_in'].shape[2]
        if l > 0:
            below = saved[l - 1]
            top = mm_nt_ln_bwd([dproj], fw['w_in'], dz1, alpha, below['zh3'], below['rs3'], ln3_g[l - 1][None],
                               tm=tm, name="proj_bwd")
        else:
            dx = mm_nt([dproj], fw['w_in'], F32, tm=tm_big, tk=512, res=dz1, alpha=alpha, name="proj_bwd_x")
        g['w_in'] = mm_tn(s['xb'], [dproj], tk=512, tn=ns_in, shard_width=ns_in, name="grad_w_in")

        pending = pre_add(g, RIDE_MIX)
        small_grads[l] = {n: g[n].reshape(W[n].shape[1:-1] + (-1,)) for n in SMALL}

    grad_x = dx[None]

    reduce_into(RIDE_MIX, run_rider(ScatterRider(pending), name="rs_chip_scatter"), 0)
    shared = rs_sibling_share([reduced_big[n] for n in BIG], name="rs_sibling_share")
    G = {}
    for n, sh in zip(BIG, shared):
        G[n] = sh.reshape(W[n].shape)

    small_full_shapes = []
    small_stack = []
    for n in SMALL:
        st = jnp.stack([small_grads[l][n] for l in range(L)])
        small_stack.append(st)
        small_full_shapes.append(st.shape)
    reduced = _unpack(allreduce_small(_pack(small_stack), name="allreduce_small"), small_full_shapes)
    for n, r in zip(SMALL, reduced):
        if n in SMALL_SHARDED:
            width = W[n].shape[-1]
            r = lax.dynamic_slice_in_dim(r, chip * width, width, axis=2)
        G[n] = r

    out_g, out_d, out_m, out_v = {}, {}, {}, {}
    for n in BIG:
        shp = W[n].shape
        flat = lambda a: a.reshape(shp[0] * shp[1], shp[2])
        res = adamw(flat(W[n]), flat(G[n]), flat(M1[n]), flat(V2[n]), tr=_adamw_tile(shp[0] * shp[1], shp[2]), name="adamw")
        out_g[n], out_d[n], out_m[n], out_v[n] = [r.reshape(shp) for r in res]
    small_shapes = [W[n].shape for n in SMALL]
    packed = [_pack([d[n] for n in SMALL]) for d in (W, G, M1, V2)]
    res = adamw(*packed, tr=packed[0].shape[0], name="adamw_small")
    for d, r in zip((out_g, out_d, out_m, out_v), res):
        for n, a in zip(SMALL, _unpack(r, small_shapes)):
            d[n] = a

    return (loss, grad_x, *[out_g[n] for n in WEIGHTS], *[out_d[n] for n in WEIGHTS],
            *[out_m[n] for n in WEIGHTS], *[out_v[n] for n in WEIGHTS])
```

```python
import functools

import jax
import jax.numpy as jnp
from jax import lax
from jax.experimental import pallas as pl
from jax.experimental.pallas import tpu as pltpu

F32 = jnp.float32
BF16 = jnp.bfloat16
MESH = pl.DeviceIdType.MESH

LN_EPS = 1e-5
SB_HEADS = 8
MEM_HEADS = 4
ADAM_LR, ADAM_B1, ADAM_B2, ADAM_EPS, ADAM_WD, ADAM_STEP = 0.001, 0.9, 0.999, 1e-08, 0.01, 10

LANES = 128
V7X_VMEM_BYTES = 64 << 20
VMEM_CAP = V7X_VMEM_BYTES - (6 << 20)
N_CHIPS = 4
N_DEV = 8

BIG = ('w_in', 'w_out', 'mem_wq', 'mem_wk', 'mem_wv', 'mem_wo', 'ffn_up', 'ffn_down')
RIDE_IN = ('w_in',)
RIDE_ATT = ('w_out', 'mem_wq', 'mem_wk', 'mem_wv', 'mem_wo')
RIDE_FFN = ('ffn_up', 'ffn_down')
RIDE_MIX = ('w_in', 'w_out')
RIDE_REST = ('mem_wq', 'mem_wk', 'mem_wv', 'mem_wo', 'ffn_up', 'ffn_down')
COL_SHARDED = ('w_in', 'ffn_up')
SMALL = ('conv_w', 'conv_b', 'conv_ln_g', 'conv_ln_b', 'ln1_g', 'ln1_b', 'ln2_g', 'ln2_b',
         'ffn_conv_w', 'ffn_conv_b', 'ln3_g', 'ln3_b')
SMALL_SHARDED = ('conv_w', 'ffn_conv_w')
WEIGHTS = ('w_in', 'conv_w', 'conv_b', 'conv_ln_g', 'conv_ln_b', 'w_out', 'ln1_g', 'ln1_b',
           'mem_wq', 'mem_wk', 'mem_wv', 'mem_wo', 'ln2_g', 'ln2_b', 'ffn_up', 'ffn_conv_w',
           'ffn_conv_b', 'ffn_down', 'ln3_g', 'ln3_b')


def _params(block_bytes, semantics=None, **kw):
    limit = int(min(max(2 * block_bytes + (8 << 20), 32 << 20), VMEM_CAP))
    return pltpu.CompilerParams(dimension_semantics=semantics, vmem_limit_bytes=limit, **kw)


def _pallas(body, **kw):
    call = pl.pallas_call(body, **kw)

    def run(*args):
        return call(*[pltpu.with_memory_space_constraint(a, pltpu.HBM)
                      if jnp.issubdtype(a.dtype, jnp.floating) else a for a in args])

    return run


def _nbytes(shape, dtype):
    n = 1
    for s in shape:
        n *= s
    return n * jnp.dtype(dtype).itemsize


def _dot(a, b):
    return jnp.dot(a, b, preferred_element_type=F32)


def _dot_nt(a, b):
    return lax.dot_general(a, b, (((1,), (1,)), ((), ())), preferred_element_type=F32)


def _dot_tn(a, b):
    return lax.dot_general(a, b, (((0,), (0,)), ((), ())), preferred_element_type=F32)


def _sigmoid(x):
    return 1.0 / (1.0 + jnp.exp(-x))


def mm_nn(a, b, out_dtype, *, tm, tn, name):
    M, K = a.shape
    sharded = b.ndim == 3
    if sharded:
        nsh, _, ns = b.shape
        N, per = nsh * ns, ns // tn
        b_spec = pl.BlockSpec((None, K, tn), lambda i, j: (j // per, 0, j % per))
    else:
        N = b.shape[1]
        b_spec = pl.BlockSpec((K, tn), lambda i, j: (0, j))

    def body(a_ref, b_ref, o_ref):
        o_ref[...] = _dot(a_ref[...].astype(BF16), b_ref[...]).astype(o_ref.dtype)

    blk = _nbytes((tm, K), a.dtype) + _nbytes((K, tn), BF16) + _nbytes((tm, tn), out_dtype)
    return _pallas(
        body, name=name, out_shape=pltpu.HBM((M, N), out_dtype), grid=(M // tm, N // tn),
        in_specs=[pl.BlockSpec((tm, K), lambda i, j: (i, 0)), b_spec],
        out_specs=pl.BlockSpec((tm, tn), lambda i, j: (i, j)),
        compiler_params=_params(blk, ("parallel", "parallel")))(a, b)


def mm_ln(a, b, x, gamma, beta, alpha, *, tm, name):
    M, K = a.shape
    D = b.shape[1]

    def body(a_ref, b_ref, x_ref, g_ref, be_ref, y_ref, yb_ref, zh_ref, rs_ref):
        z = alpha * x_ref[...] + _dot(a_ref[...], b_ref[...])
        mu = jnp.mean(z, axis=-1, keepdims=True)
        zc = z - mu
        rstd = lax.rsqrt(jnp.mean(zc * zc, axis=-1, keepdims=True) + LN_EPS)
        zh = zc * rstd
        y = zh * g_ref[...] + be_ref[...]
        y_ref[...] = y
        yb_ref[...] = y.astype(BF16)
        zh_ref[...] = zh
        rs_ref[...] = rstd

    row = lambda i: (i, 0)
    fix = lambda i: (0, 0)
    blk = _nbytes((tm, K), BF16) + _nbytes((K, D), BF16) + 4 * _nbytes((tm, D), F32)
    return _pallas(
        body, name=name, grid=(M // tm,),
        out_shape=(pltpu.HBM((M, D), F32), pltpu.HBM((M, D), BF16),
                   pltpu.HBM((M, D), F32), pltpu.HBM((M, 1), F32)),
        in_specs=[pl.BlockSpec((tm, K), row), pl.BlockSpec((K, D), fix), pl.BlockSpec((tm, D), row),
                  pl.BlockSpec((1, D), fix), pl.BlockSpec((1, D), fix)],
        out_specs=(pl.BlockSpec((tm, D), row), pl.BlockSpec((tm, D), row), pl.BlockSpec((tm, D), row),
                   pl.BlockSpec((tm, 1), row)),
        compiler_params=_params(blk, ("parallel",)))(a, b, x, gamma, beta)


def ln_bwd(dy, zh, rstd, gamma, *, tm, name):
    M, D = dy.shape

    def body(dy_ref, zh_ref, rs_ref, g_ref, dz_ref, dzb_ref, dg_ref, db_ref):
        @pl.when(pl.program_id(0) == 0)
        def _():
            dg_ref[...] = jnp.zeros_like(dg_ref)
            db_ref[...] = jnp.zeros_like(db_ref)

        dyv, zhv = dy_ref[...], zh_ref[...]
        dg_ref[...] += jnp.sum(dyv * zhv, axis=0, keepdims=True)
        db_ref[...] += jnp.sum(dyv, axis=0, keepdims=True)
        dzh = dyv * g_ref[...]
        m1 = jnp.mean(dzh, axis=-1, keepdims=True)
        m2 = jnp.mean(dzh * zhv, axis=-1, keepdims=True)
        dz = rs_ref[...] * (dzh - m1 - zhv * m2)
        dz_ref[...] = dz
        dzb_ref[...] = dz.astype(BF16)

    row = lambda i: (i, 0)
    fix = lambda i: (0, 0)
    return _pallas(
        body, name=name, grid=(M // tm,),
        out_shape=(pltpu.HBM((M, D), F32), pltpu.HBM((M, D), BF16),
                   pltpu.HBM((1, D), F32), pltpu.HBM((1, D), F32)),
        in_specs=[pl.BlockSpec((tm, D), row), pl.BlockSpec((tm, D), row), pl.BlockSpec((tm, 1), row),
                  pl.BlockSpec((1, D), fix)],
        out_specs=(pl.BlockSpec((tm, D), row), pl.BlockSpec((tm, D), row), pl.BlockSpec((1, D), fix),
                   pl.BlockSpec((1, D), fix)),
        compiler_params=_params(4 * _nbytes((tm, D), F32), ("arbitrary",)))(dy, zh, rstd, gamma)


def mm_nt(a_list, b, out_dtype, *, tm, tk, name, res=None, alpha=None, rider=None):
    M = a_list[0].shape[0]
    widths = [a.shape[1] for a in a_list]
    sharded = b.ndim == 3
    if sharded:
        nsh, K, ns = b.shape
        b_spec = pl.BlockSpec((nsh, tk, ns), lambda i, j: (0, j, 0))
        for w in widths:
            assert w % ns == 0
    else:
        K, N = b.shape
        ns = None
        b_spec = pl.BlockSpec((tk, N), lambda i, j: (j, 0))
    n_a = len(a_list)

    def body(*refs):
        a_refs, b_ref = refs[:n_a], refs[n_a]
        o_ref = refs[-1]
        acc = None
        off = 0
        for a_ref, w in zip(a_refs, widths):
            if sharded:
                for p in range(w // ns):
                    t = _dot_nt(a_ref[:, p * ns:(p + 1) * ns].astype(BF16), b_ref[off // ns + p])
                    acc = t if acc is None else acc + t
            else:
                t = _dot_nt(a_ref[...].astype(BF16), b_ref[:, off:off + w])
                acc = t if acc is None else acc + t
            off += w
        if res is not None:
            acc = acc + alpha * refs[n_a + 1][...]
        o_ref[...] = acc.astype(o_ref.dtype)

    in_specs = [pl.BlockSpec((tm, w), lambda i, j: (i, 0)) for w in widths] + [b_spec]
    args = list(a_list) + [b]
    if res is not None:
        in_specs.append(pl.BlockSpec((tm, tk), lambda i, j: (i, j)))
        args.append(res)
    blk = (sum(_nbytes((tm, w), a.dtype) for a, w in zip(a_list, widths)) + _nbytes((tk, sum(widths)), BF16)
           + 2 * _nbytes((tm, tk), F32))
    in_specs, out_specs, out_shape, scratch = _carry_specs(
        rider, in_specs, (pl.BlockSpec((tm, tk), lambda i, j: (i, j)),), (pltpu.HBM((M, K), out_dtype),), [])
    first = lambda: (pl.program_id(0) == 0) & (pl.program_id(1) == 0)
    last = lambda: (pl.program_id(0) == M // tm - 1) & (pl.program_id(1) == K // tk - 1)
    res_all = _pallas(
        _carry(rider, body, len(args), 1, first, last), name=name, out_shape=out_shape, grid=(M // tm, K // tk),
        in_specs=in_specs, out_specs=out_specs, scratch_shapes=scratch,
        compiler_params=_params(blk, ("arbitrary", "arbitrary")))(*args, *(rider.arrays if rider else ()))
    return res_all[0] if rider is None else (res_all[0], list(res_all[1:]))


def mm_nt_ln_bwd(a_list, b, res, alpha, zh, rstd, gamma, *, tm, name):
    M, D = res.shape
    widths = [a.shape[1] for a in a_list]
    sharded = b.ndim == 3
    if sharded:
        nsh, _, ns = b.shape
        b_spec = pl.BlockSpec((nsh, D, ns), lambda i: (0, 0, 0))
    else:
        ns = None
        b_spec = pl.BlockSpec((D, b.shape[1]), lambda i: (0, 0))
    n_a = len(a_list)

    def body(*refs):
        a_refs, b_ref = refs[:n_a], refs[n_a]
        res_ref, zh_ref, rs_ref, g_ref = refs[n_a + 1:n_a + 5]
        dz_ref, dzb_ref, dg_ref, db_ref = refs[n_a + 5:]

        @pl.when(pl.program_id(0) == 0)
        def _():
            dg_ref[...] = jnp.zeros_like(dg_ref)
            db_ref[...] = jnp.zeros_like(db_ref)

        dy = alpha * res_ref[...]
        off = 0
        for a_ref, w in zip(a_refs, widths):
            if sharded:
                for p in range(w // ns):
                    dy = dy + _dot_nt(a_ref[:, p * ns:(p + 1) * ns], b_ref[off // ns + p])
            else:
                dy = dy + _dot_nt(a_ref[...], b_ref[:, off:off + w])
            off += w
        zhv = zh_ref[...]
        dg_ref[...] += jnp.sum(dy * zhv, axis=0, keepdims=True)
        db_ref[...] += jnp.sum(dy, axis=0, keepdims=True)
        dzh = dy * g_ref[...]
        m1 = jnp.mean(dzh, axis=-1, keepdims=True)
        m2 = jnp.mean(dzh * zhv, axis=-1, keepdims=True)
        dz = rs_ref[...] * (dzh - m1 - zhv * m2)
        dz_ref[...] = dz
        dzb_ref[...] = dz.astype(BF16)

    row = lambda i: (i, 0)
    fix = lambda i: (0, 0)
    in_specs = [pl.BlockSpec((tm, w), row) for w in widths] + [
        b_spec, pl.BlockSpec((tm, D), row), pl.BlockSpec((tm, D), row), pl.BlockSpec((tm, 1), row),
        pl.BlockSpec((1, D), fix)]
    blk = (sum(_nbytes((tm, w), BF16) for w in widths) + _nbytes((D, sum(widths)), BF16)
           + 5 * _nbytes((tm, D), F32))
    return _pallas(
        body, name=name, grid=(M // tm,),
        out_shape=(pltpu.HBM((M, D), F32), pltpu.HBM((M, D), BF16), pltpu.HBM((1, D), F32),
                   pltpu.HBM((1, D), F32)),
        in_specs=in_specs,
        out_specs=(pl.BlockSpec((tm, D), row), pl.BlockSpec((tm, D), row), pl.BlockSpec((1, D), fix),
                   pl.BlockSpec((1, D), fix)),
        compiler_params=_params(blk, ("arbitrary",)))(*a_list, b, res, zh, rstd, gamma)


def mm_tn(a, b_list, *, tk, tn, name, shard_width=None, tmc=None):
    M, K = a.shape
    tmc = M if tmc is None else tmc
    nm = M // tmc
    widths = [b.shape[1] for b in b_list]
    N = sum(widths)
    starts, s = [], 0
    for w in widths:
        assert w % tn == 0
        starts.append(s)
        s += w // tn
    n_b = len(b_list)

    def body(*refs):
        a_ref, b_refs, o_ref, acc = refs[0], refs[1:1 + n_b], refs[-2], refs[-1]
        j, m = pl.program_id(1), pl.program_id(2)
        for b_ref, st, w in zip(b_refs, starts, widths):
            @pl.when((j >= st) & (j < st + w // tn))
            def _(b_ref=b_ref):
                t = _dot_tn(a_ref[...].astype(BF16), b_ref[...].astype(BF16))
                if nm == 1:
                    o_ref[...] = t.astype(o_ref.dtype)
                else:
                    @pl.when(m == 0)
                    def _():
                        acc[...] = t

                    @pl.when(m > 0)
                    def _():
                        acc[...] += t

                    @pl.when(m == nm - 1)
                    def _():
                        o_ref[...] = acc[...].astype(o_ref.dtype)

    def b_map(st, w):
        nb = w // tn
        return lambda i, j, m: (jnp.where((j >= st) & (j < st + nb), m, 0), jnp.clip(j - st, 0, nb - 1))

    in_specs = [pl.BlockSpec((tmc, tk), lambda i, j, m: (m, i))]
    in_specs += [pl.BlockSpec((tmc, tn), b_map(st, w)) for st, w in zip(starts, widths)]
    if shard_width is None:
        out_shape = pltpu.HBM((K, N), BF16)
        out_spec = pl.BlockSpec((tk, tn), lambda i, j, m: (i, j))
    else:
        per = shard_width // tn
        out_shape = pltpu.HBM((N // shard_width, K, shard_width), BF16)
        out_spec = pl.BlockSpec((None, tk, tn), lambda i, j, m: (j // per, i, j % per))
    acc_shape = (tk, tn) if nm > 1 else (8, LANES)
    blk = (_nbytes((tmc, tk), a.dtype) + n_b * _nbytes((tmc, tn), b_list[0].dtype) + 2 * _nbytes((tk, tn), F32))
    return _pallas(
        body, name=name, out_shape=out_shape, grid=(K // tk, N // tn, nm), in_specs=in_specs, out_specs=out_spec,
        scratch_shapes=[pltpu.VMEM(acc_shape, F32)],
        compiler_params=_params(blk, ("parallel", "arbitrary", "arbitrary")))(a, *b_list)


CONV_PAD = 32
CONV_CHUNK = 128


def _rows(win, off, n, shifts):
    b, a = off % 8, off // 8
    if b not in shifts:
        shifts[b] = win if b == 0 else win[b:b + n + CONV_PAD - 8, :]
    return shifts[b][8 * a:8 * a + n, :]


def _by_residue(n_taps, offset):
    return sorted(range(n_taps), key=lambda k: (offset(k) % 8, k))


def conv_fwd(proj, conv_w, conv_b, *, name, rider=None):
    S = proj.shape[0]
    KW, C = conv_w.shape
    nct = C // LANES
    rc = min(CONV_CHUNK, S)

    def body(a_ref, g_ref, w_ref, b_ref, o_ref, pad):
        pad[0:CONV_PAD, :] = jnp.zeros((CONV_PAD, LANES), F32)
        pad[CONV_PAD:, :] = a_ref[...] * _sigmoid(g_ref[...])
        w = w_ref[...]
        bias = b_ref[...]

        def chunk(i, _):
            base = pl.multiple_of(i * rc, rc)
            win = pad[pl.ds(base, rc + CONV_PAD), :]
            acc = jnp.zeros((rc, LANES), F32) + bias
            shifts = {}
            for k in _by_residue(KW, lambda k: CONV_PAD - (KW - 1) + k):
                acc = acc + w[k:k + 1, :] * _rows(win, CONV_PAD - (KW - 1) + k, rc, shifts)
            o_ref[pl.ds(base, rc), :] = acc
            return 0

        lax.fori_loop(0, S // rc, chunk, 0)

    in_specs, out_specs, out_shape, scratch = _carry_specs(
        rider, [pl.BlockSpec((S, LANES), lambda c: (0, c)), pl.BlockSpec((S, LANES), lambda c: (0, c + nct)),
                pl.BlockSpec((KW, LANES), lambda c: (0, c)), pl.BlockSpec((1, LANES), lambda c: (0, c))],
        (pl.BlockSpec((S, LANES), lambda c: (0, c)),), (pltpu.HBM((S, C), F32),),
        [pltpu.VMEM((S + CONV_PAD, LANES), F32)])
    first = lambda: pl.program_id(0) == 0
    last = lambda: pl.program_id(0) == nct - 1
    res = _pallas(
        _carry(rider, body, 4, 1, first, last), name=name, grid=(nct,), out_shape=out_shape,
        in_specs=in_specs, out_specs=out_specs, scratch_shapes=scratch,
        compiler_params=_params(4 * _nbytes((S, LANES), F32), ("arbitrary",)))(
            proj, proj, conv_w, conv_b, *(rider.arrays if rider else ()))
    return res[0], list(res[1:])


def conv_bwd(du1, proj, conv_w, *, name):
    S = proj.shape[0]
    KW, C = conv_w.shape
    nct = C // LANES
    rc = min(CONV_CHUNK, S)

    def body(d_ref, a_ref, g_ref, w_ref, da_ref, dg_ref, dw_ref, db_ref, pad_u, pad_d, du0, dw_acc):
        dw_acc[...] = jnp.zeros_like(dw_acc)
        pad_u[0:CONV_PAD, :] = jnp.zeros((CONV_PAD, LANES), F32)
        pad_u[CONV_PAD:, :] = a_ref[...] * _sigmoid(g_ref[...])
        pad_d[0:S, :] = d_ref[...]
        pad_d[S:, :] = jnp.zeros((CONV_PAD, LANES), F32)
        w = w_ref[...]
        db_ref[...] = jnp.sum(d_ref[...], axis=0, keepdims=True)

        def chunk(i, _):
            base = pl.multiple_of(i * rc, rc)
            d = pad_d[pl.ds(base, rc), :]
            win_u = pad_u[pl.ds(base, rc + CONV_PAD), :]
            win_d = pad_d[pl.ds(base, rc + CONV_PAD), :]
            shifts = {}
            for k in _by_residue(KW, lambda k: CONV_PAD - (KW - 1) + k):
                u_k = _rows(win_u, CONV_PAD - (KW - 1) + k, rc, shifts)
                dw_acc[k:k + 1, :] += jnp.sum(d * u_k, axis=0, keepdims=True)
            acc = jnp.zeros((rc, LANES), F32)
            shifts = {}
            for k in _by_residue(KW, lambda k: KW - 1 - k):
                acc = acc + w[k:k + 1, :] * _rows(win_d, KW - 1 - k, rc, shifts)
            du0[pl.ds(base, rc), :] = acc
            return 0

        lax.fori_loop(0, S // rc, chunk, 0)
        dw_ref[...] = dw_acc[0:KW, :]
        a, sg = a_ref[...], _sigmoid(g_ref[...])
        d0 = du0[...]
        da_ref[...] = (d0 * sg).astype(BF16)
        dg_ref[...] = (d0 * a * sg * (1.0 - sg)).astype(BF16)

    col = lambda c: (0, c)
    return _pallas(
        body, name=name, grid=(nct,),
        out_shape=(pltpu.HBM((S, C), BF16), pltpu.HBM((S, C), BF16),
                   pltpu.HBM((KW, C), F32), pltpu.HBM((1, C), F32)),
        in_specs=[pl.BlockSpec((S, LANES), col), pl.BlockSpec((S, LANES), col),
                  pl.BlockSpec((S, LANES), lambda c: (0, c + nct)), pl.BlockSpec((KW, LANES), col)],
        out_specs=(pl.BlockSpec((S, LANES), col), pl.BlockSpec((S, LANES), col), pl.BlockSpec((KW, LANES), col),
                   pl.BlockSpec((1, LANES), col)),
        scratch_shapes=[pltpu.VMEM((S + CONV_PAD, LANES), F32), pltpu.VMEM((S + CONV_PAD, LANES), F32),
                        pltpu.VMEM((S, LANES), F32), pltpu.VMEM((CONV_PAD, LANES), F32)],
        compiler_params=_params(8 * _nbytes((S, LANES), F32), ("parallel",)))(du1, proj, proj, conv_w)


def ln_silu(u1, o_sb, gamma, beta, *, tm, name):
    S, C = u1.shape

    def body(u_ref, o_ref, g_ref, b_ref, out_ref):
        z = u_ref[...]
        mu = jnp.mean(z, axis=-1, keepdims=True)
        zc = z - mu
        y = zc * lax.rsqrt(jnp.mean(zc * zc, axis=-1, keepdims=True) + LN_EPS) * g_ref[...] + b_ref[...]
        out_ref[:, 0:C] = (y * _sigmoid(y)).astype(BF16)
        out_ref[:, C:] = o_ref[...].astype(BF16)

    row = lambda i: (i, 0)
    fix = lambda i: (0, 0)
    return _pallas(
        body, name=name, out_shape=pltpu.HBM((S, 2 * C), BF16), grid=(S // tm,),
        in_specs=[pl.BlockSpec((tm, C), row), pl.BlockSpec((tm, C), row), pl.BlockSpec((1, C), fix),
                  pl.BlockSpec((1, C), fix)],
        out_specs=pl.BlockSpec((tm, 2 * C), row),
        compiler_params=_params(4 * _nbytes((tm, C), F32), ("parallel",)))(u1, o_sb, gamma, beta)


def ln_silu_bwd(dua, u1, gamma, beta, *, tm, name):
    S, C = u1.shape

    def body(d_ref, u_ref, g_ref, b_ref, du1_ref, dg_ref, db_ref):
        @pl.when(pl.program_id(0) == 0)
        def _():
            dg_ref[...] = jnp.zeros_like(dg_ref)
            db_ref[...] = jnp.zeros_like(db_ref)

        z = u_ref[...]
        mu = jnp.mean(z, axis=-1, keepdims=True)
        zc = z - mu
        rstd = lax.rsqrt(jnp.mean(zc * zc, axis=-1, keepdims=True) + LN_EPS)
        zh = zc * rstd
        y = zh * g_ref[...] + b_ref[...]
        sg = _sigmoid(y)
        dy = d_ref[...] * (sg * (1.0 + y * (1.0 - sg)))
        dg_ref[...] += jnp.sum(dy * zh, axis=0, keepdims=True)
        db_ref[...] += jnp.sum(dy, axis=0, keepdims=True)
        dzh = dy * g_ref[...]
        m1 = jnp.mean(dzh, axis=-1, keepdims=True)
        m2 = jnp.mean(dzh * zh, axis=-1, keepdims=True)
        du1_ref[...] = rstd * (dzh - m1 - zh * m2)

    row = lambda i: (i, 0)
    fix = lambda i: (0, 0)
    return _pallas(
        body, name=name, grid=(S // tm,),
        out_shape=(pltpu.HBM((S, C), F32), pltpu.HBM((1, C), F32),
                   pltpu.HBM((1, C), F32)),
        in_specs=[pl.BlockSpec((tm, C), row), pl.BlockSpec((tm, C), row), pl.BlockSpec((1, C), fix),
                  pl.BlockSpec((1, C), fix)],
        out_specs=(pl.BlockSpec((tm, C), row), pl.BlockSpec((1, C), fix), pl.BlockSpec((1, C), fix)),
        compiler_params=_params(4 * _nbytes((tm, C), F32), ("arbitrary",)))(dua, u1, gamma, beta)


SB_BLOCK = 256
SB_STOP = -105.0
SB_GROUP = 4


def _split_dot(x, tri):
    hi = x.astype(BF16)
    lo = (x - hi.astype(F32)).astype(BF16)
    return _dot(hi, tri) + _dot(lo, tri)


def _neg_softplus(z):
    return -(jnp.maximum(z, 0.0) + jnp.log(1.0 + jnp.exp(-jnp.abs(z))))


def sb_fwd(proj, *, q_col, name, rider=None):
    S = proj.shape[0]
    dh = LANES // 2
    W = SB_HEADS * dh
    BW = SB_GROUP * dh
    ngrp = W // BW
    T = min(SB_BLOCK, S)
    nblk = S // T
    scale = dh ** -0.5
    qb0 = q_col // BW
    heads = range(SB_GROUP)
    sl = [slice(h * dh, (h + 1) * dh) for h in heads]

    def body(q_ref, k_ref, v_ref, o_ref, l_ref, qs, ks, vs):
        r_i = lax.broadcasted_iota(jnp.int32, (T, T), 0)
        c_i = lax.broadcasted_iota(jnp.int32, (T, T), 1)
        tri = (r_i >= c_i).astype(BF16)
        vis = c_i < r_i
        lane = lax.broadcasted_iota(jnp.int32, (T, dh), 1)

        qs[...] = (q_ref[...] * scale).astype(BF16)
        ks[...] = k_ref[...].astype(BF16)
        vs[...] = v_ref[...].astype(BF16)

        def step(qb, j0, diag, st):
            kb = [ks[pl.ds(j0, T), sl[h]] for h in heads]
            vb = [vs[pl.ds(j0, T), sl[h]] for h in heads]
            z = [_dot_nt(qb[h], kb[h]) for h in heads]
            lk = [_neg_softplus(z[h]) for h in heads]
            if diag:
                lk = [jnp.where(vis, lk[h], 0.0) for h in heads]
            C = [_split_dot(lk[h], tri) for h in heads]
            A = [jnp.exp(z[h] + C[h] + st[2 * h + 1]) for h in heads]
            if diag:
                A = [jnp.where(vis, A[h], 0.0) for h in heads]
            acc = [st[2 * h] + _dot(A[h].astype(BF16), vb[h]) for h in heads]
            out = ()
            for h in heads:
                out += (acc[h], st[2 * h + 1] + C[h][:, 0:1])
            return out

        def qblock(i, _):
            r0 = pl.multiple_of(i * T, T)
            qb = [qs[pl.ds(r0, T), sl[h]] for h in heads]
            zero = (jnp.zeros((T, dh), F32), jnp.zeros((T, 1), F32))
            state = step(qb, r0, True, zero * SB_GROUP)

            def more(c):
                worst = c[2]
                for h in heads[1:]:
                    worst = jnp.maximum(worst, c[2 + 2 * h])
                return (c[0] >= 0) & (jnp.max(worst) >= SB_STOP)

            def walk(c):
                return (c[0] - 1,) + step(qb, pl.multiple_of(c[0] * T, T), False, c[1:])

            c = lax.while_loop(more, walk, (i - 1,) + state)
            walked = (i - c[0]).astype(F32)
            for h in heads:
                o_ref[pl.ds(r0, T), sl[h]] = c[1 + 2 * h]
                l_ref[pl.ds(r0, T), sl[h]] = jnp.where(lane == 1, walked, c[2 + 2 * h])
            return 0

        lax.fori_loop(0, nblk, qblock, 0)

    blk = lambda off: pl.BlockSpec((S, BW), lambda g: (0, qb0 + off * ngrp + g), pipeline_mode=pl.Buffered(1))
    out = pl.BlockSpec((S, BW), lambda g: (0, g))
    in_specs, out_specs, out_shape, scratch = _carry_specs(
        rider, [blk(0), blk(1), blk(2)], (out, out), (pltpu.HBM((S, W), F32), pltpu.HBM((S, W), F32)),
        [pltpu.VMEM((S, BW), BF16)] * 3)
    first = lambda: pl.program_id(0) == 0
    last = lambda: pl.program_id(0) == ngrp - 1
    res = _pallas(
        _carry(rider, body, 3, 2, first, last), name=name, grid=(ngrp,), out_shape=out_shape,
        in_specs=in_specs, out_specs=out_specs, scratch_shapes=scratch,
        compiler_params=_params(5 * _nbytes((S, BW), F32), ("arbitrary",)))(
            proj, proj, proj, *(rider.arrays if rider else ()))
    return res[0], res[1], list(res[2:])


def sb_bwd(proj, ltot, dua, *, q_col, do_col, name, rider=None):
    S = proj.shape[0]
    dh = LANES // 2
    W = SB_HEADS * dh
    BW = SB_GROUP * dh
    ngrp = W // BW
    T = min(SB_BLOCK, S)
    nblk = S // T
    scale = dh ** -0.5
    qb0 = q_col // BW
    db0 = do_col // BW
    heads = range(SB_GROUP)
    sl = [slice(h * dh, (h + 1) * dh) for h in heads]

    def body(q_ref, k_ref, v_ref, l_ref, do_ref, dq_ref, dk_ref, dv_ref, ks, vs, dks, dvs):
        r_i = lax.broadcasted_iota(jnp.int32, (T, T), 0)
        c_i = lax.broadcasted_iota(jnp.int32, (T, T), 1)
        tri_rev = (r_i >= c_i).astype(BF16)
        tri_fwd = (r_i <= c_i).astype(BF16)
        vis = c_i < r_i

        ks[...] = k_ref[...].astype(BF16)
        vs[...] = v_ref[...].astype(BF16)
        dks[...] = jnp.zeros_like(dks)
        dvs[...] = jnp.zeros_like(dvs)

        def step(qb, dob, Lt, j0, diag, st):
            kb = [ks[pl.ds(j0, T), sl[h]] for h in heads]
            vb = [vs[pl.ds(j0, T), sl[h]] for h in heads]
            z = [_dot_nt(qb[h], kb[h]) for h in heads]
            dA = [_dot_nt(dob[h], vb[h]) for h in heads]
            lk = [_neg_softplus(z[h]) for h in heads]
            beta = [jnp.exp(z[h] + lk[h]) for h in heads]
            if diag:
                lk = [jnp.where(vis, lk[h], 0.0) for h in heads]
            C = [_split_dot(lk[h], tri_rev) for h in heads]
            rowsum = [C[h][:, 0:1] for h in heads]
            A = [jnp.exp(z[h] + C[h] + (Lt[h] - st[3 * h + 1] - rowsum[h])) for h in heads]
            if diag:
                A = [jnp.where(vis, A[h], 0.0) for h in heads]
            g = [A[h] * dA[h] for h in heads]
            Gin = [_split_dot(g[h], tri_fwd) for h in heads]
            dz = [g[h] - beta[h] * (st[3 * h + 2] + Gin[h]) for h in heads]
            if diag:
                dz = [jnp.where(vis, dz[h], 0.0) for h in heads]
            dzb = [dz[h].astype(BF16) for h in heads]
            out = ()
            for h in heads:
                dvs[pl.ds(j0, T), sl[h]] += _dot_tn(A[h].astype(BF16), dob[h])
                dks[pl.ds(j0, T), sl[h]] += _dot_tn(dzb[h], qb[h])
                out += (st[3 * h] + _dot(dzb[h], kb[h]), st[3 * h + 1] + rowsum[h],
                        st[3 * h + 2] + Gin[h][:, T - 1:T])
            return out

        def qblock(i, _):
            r0 = pl.multiple_of(i * T, T)
            qb = [(q_ref[pl.ds(r0, T), sl[h]] * scale).astype(BF16) for h in heads]
            dob = [do_ref[pl.ds(r0, T), sl[h]].astype(BF16) for h in heads]
            Lt = [l_ref[pl.ds(r0, T), h * dh:h * dh + 1] for h in heads]
            walked = jnp.clip(jnp.max(l_ref[pl.ds(r0, 8), 1:2]).astype(jnp.int32), 1, i + 1)

            def inner(j, c):
                return step(qb, dob, Lt, pl.multiple_of(j * T, T), False, c)

            zero = jnp.zeros((T, 1), F32)
            init = (jnp.zeros((T, dh), F32), zero, zero)
            c = lax.fori_loop(i + 1 - walked, i, inner, init * SB_GROUP)
            c = step(qb, dob, Lt, r0, True, c)
            for h in heads:
                dq_ref[pl.ds(r0, T), sl[h]] = (c[3 * h] * scale).astype(BF16)
            return 0

        lax.fori_loop(0, nblk, qblock, 0)
        dk_ref[...] = dks[...].astype(BF16)
        dv_ref[...] = dvs[...].astype(BF16)

    once = pl.Buffered(1)
    blk = lambda off: pl.BlockSpec((S, BW), lambda g: (0, qb0 + off * ngrp + g), pipeline_mode=once)
    out = pl.BlockSpec((S, BW), lambda g: (0, g))
    o_shape = pltpu.HBM((S, W), BF16)
    in_specs, out_specs, out_shape, scratch = _carry_specs(
        rider, [blk(0), blk(1), blk(2), pl.BlockSpec((S, BW), lambda g: (0, g), pipeline_mode=once),
                pl.BlockSpec((S, BW), lambda g: (0, db0 + g), pipeline_mode=once)], (out, out, out),
        (o_shape, o_shape, o_shape), [pltpu.VMEM((S, BW), BF16)] * 2 + [pltpu.VMEM((S, BW), F32)] * 2)
    first = lambda: pl.program_id(0) == 0
    last = lambda: pl.program_id(0) == ngrp - 1
    res = _pallas(
        _carry(rider, body, 5, 3, first, last), name=name, grid=(ngrp,), out_shape=out_shape,
        in_specs=in_specs, out_specs=out_specs, scratch_shapes=scratch,
        compiler_params=_params(6 * _nbytes((S, BW), F32), ("arbitrary",)))(
            proj, proj, proj, ltot, dua, *(rider.arrays if rider else ()))
    return res[0], res[1], res[2], list(res[3:])


def xattn_fwd(q, k, v, *, tm, name):
    S, D = q.shape
    Mlen = k.shape[0]
    hd = D // MEM_HEADS
    scale = hd ** -0.5

    def body(q_ref, k_ref, v_ref, o_ref):
        for h in range(MEM_HEADS):
            sl = slice(h * hd, (h + 1) * hd)
            s = _dot_nt(q_ref[:, sl], k_ref[:, sl]) * scale
            e = jnp.exp(s - jnp.max(s, axis=-1, keepdims=True))
            p = e / jnp.sum(e, axis=-1, keepdims=True)
            o_ref[:, sl] = _dot(p.astype(BF16), v_ref[:, sl]).astype(BF16)

    row = lambda i: (i, 0)
    fix = lambda i: (0, 0)
    return _pallas(
        body, name=name, out_shape=pltpu.HBM((S, D), BF16), grid=(S // tm,),
        in_specs=[pl.BlockSpec((tm, D), row), pl.BlockSpec((Mlen, D), fix), pl.BlockSpec((Mlen, D), fix)],
        out_specs=pl.BlockSpec((tm, D), row),
        compiler_params=_params(4 * _nbytes((tm, D), F32), ("parallel",)))(q, k, v)


def xattn_bwd(q, do, k, v, *, tm, name):
    S, D = q.shape
    Mlen = k.shape[0]
    hd = D // MEM_HEADS
    scale = hd ** -0.5

    def body(q_ref, do_ref, k_ref, v_ref, dq_ref, dk_ref, dv_ref):
        @pl.when(pl.program_id(0) == 0)
        def _():
            dk_ref[...] = jnp.zeros_like(dk_ref)
            dv_ref[...] = jnp.zeros_like(dv_ref)

        for h in range(MEM_HEADS):
            sl = slice(h * hd, (h + 1) * hd)
            qh, doh, kh, vh = q_ref[:, sl], do_ref[:, sl], k_ref[:, sl], v_ref[:, sl]
            s = _dot_nt(qh, kh) * scale
            e = jnp.exp(s - jnp.max(s, axis=-1, keepdims=True))
            p = e / jnp.sum(e, axis=-1, keepdims=True)
            dp = _dot_nt(doh, vh)
            ds = (p * (dp - jnp.sum(p * dp, axis=-1, keepdims=True)) * scale).astype(BF16)
            dq_ref[:, sl] = _dot(ds, kh).astype(BF16)
            dk_ref[:, sl] += _dot_tn(ds, qh)
            dv_ref[:, sl] += _dot_tn(p.astype(BF16), doh)

    row = lambda i: (i, 0)
    fix = lambda i: (0, 0)
    return _pallas(
        body, name=name, grid=(S // tm,),
        out_shape=(pltpu.HBM((S, D), BF16), pltpu.HBM((Mlen, D), F32),
                   pltpu.HBM((Mlen, D), F32)),
        in_specs=[pl.BlockSpec((tm, D), row), pl.BlockSpec((tm, D), row), pl.BlockSpec((Mlen, D), fix),
                  pl.BlockSpec((Mlen, D), fix)],
        out_specs=(pl.BlockSpec((tm, D), row), pl.BlockSpec((Mlen, D), fix), pl.BlockSpec((Mlen, D), fix)),
        compiler_params=_params(6 * _nbytes((tm, D), F32), ("arbitrary",)))(q, do, k, v)


FFN_HALO = 8


def _conv3(ext, w, lo):
    tm = ext.shape[0] - FFN_HALO
    return (w[0:1, :] * ext[lo:lo + tm, :] + w[1:2, :] * ext[lo + 1:lo + 1 + tm, :]
            + w[2:3, :] * ext[lo + 2:lo + 2 + tm, :])


def ffn_up_fwd(xb, w_up, conv_w, conv_b, *, tm, tn, name, rider=None):
    S, D = xb.shape
    nsh, _, ns = w_up.shape
    F = nsh * ns // 2
    per = ns // tn
    ncol = F // tn
    KW = conv_w.shape[0]
    assert KW == 3

    def body(x_ref, wv_ref, wg_ref, cwv_ref, cwg_ref, cbv_ref, cbg_ref, uv_ref, ug_ref, cv_ref, cg_ref, h_ref,
             carry):
        @pl.when(pl.program_id(1) == 0)
        def _():
            carry[...] = jnp.zeros_like(carry)

        x = x_ref[...]
        uv = _dot(x, wv_ref[...])
        ug = _dot(x, wg_ref[...])
        uv_ref[...] = uv.astype(BF16)
        ug_ref[...] = ug.astype(BF16)
        lo = FFN_HALO - (KW - 1)
        cv = _conv3(jnp.concatenate([carry[0], uv], axis=0), cwv_ref[...], lo) + cbv_ref[...]
        cg = _conv3(jnp.concatenate([carry[1], ug], axis=0), cwg_ref[...], lo) + cbg_ref[...]
        carry[0] = uv[tm - FFN_HALO:, :]
        carry[1] = ug[tm - FFN_HALO:, :]
        cv_ref[...] = cv.astype(BF16)
        cg_ref[...] = cg.astype(BF16)
        h_ref[...] = (cg * _sigmoid(cg) * cv).astype(BF16)

    wspec = lambda half: pl.BlockSpec((None, D, tn), lambda j, i: (half * (nsh // 2) + j // per, 0, j % per))
    cspec = lambda rows, half: pl.BlockSpec((rows, tn), lambda j, i: (0, half * ncol + j))
    out = pl.BlockSpec((tm, tn), lambda j, i: (i, j))
    o_shape = pltpu.HBM((S, F), BF16)
    blk = _nbytes((tm, D), BF16) + 2 * _nbytes((D, tn), BF16) + 8 * _nbytes((tm, tn), F32)
    nrow = S // tm
    in_specs, out_specs, out_shape, scratch = _carry_specs(
        rider, [pl.BlockSpec((tm, D), lambda j, i: (i, 0)), wspec(0), wspec(1), cspec(KW, 0), cspec(KW, 1),
                cspec(1, 0), cspec(1, 1)], (out,) * 5, (o_shape,) * 5, [pltpu.VMEM((2, FFN_HALO, tn), F32)])
    first = lambda: (pl.program_id(0) == 0) & (pl.program_id(1) == 0)
    last = lambda: (pl.program_id(0) == ncol - 1) & (pl.program_id(1) == nrow - 1)
    res = _pallas(
        _carry(rider, body, 7, 5, first, last), name=name, grid=(ncol, nrow), out_shape=out_shape,
        in_specs=in_specs, out_specs=out_specs, scratch_shapes=scratch,
        compiler_params=_params(blk, ("arbitrary", "arbitrary")))(
            xb, w_up, w_up, conv_w, conv_w, conv_b, conv_b, *(rider.arrays if rider else ()))
    return res[:5], list(res[5:])


def ffn_mid_bwd(dzb, w_down, up_v, up_g, conv_v, conv_g, conv_w, *, tm, tn, name, rider=None):
    S, D = dzb.shape
    F = up_v.shape[1]
    ncol = F // tn
    nrow = S // tm
    KW = conv_w.shape[0]
    assert KW == 3

    def body(dz_ref, wd_ref, uv_ref, ug_ref, cv_ref, cg_ref, cwv_ref, cwg_ref,
             dv_ref, dg_ref, dwv_ref, dwg_ref, dbv_ref, dbg_ref, carry):
        @pl.when(pl.program_id(1) == 0)
        def _():
            carry[...] = jnp.zeros_like(carry)
            for r in (dwv_ref, dwg_ref, dbv_ref, dbg_ref):
                r[...] = jnp.zeros_like(r)

        cv, cg = cv_ref[...].astype(F32), cg_ref[...].astype(F32)
        dh = _dot_nt(dz_ref[...], wd_ref[...])
        sg = _sigmoid(cg)
        dcv = dh * (cg * sg)
        dcg = dh * cv * (sg * (1.0 + cg * (1.0 - sg)))

        def back(dc, u_ref, cw, slot, du_ref, dw_ref, db_ref):
            ext = jnp.concatenate([dc, carry[slot]], axis=0)
            ahead = [dc, ext[1:tm + 1, :], ext[2:tm + 2, :]]
            du = cw[2:3, :] * ahead[0] + cw[1:2, :] * ahead[1] + cw[0:1, :] * ahead[2]
            du_ref[...] = du.astype(BF16)
            carry[slot] = dc[0:FFN_HALO, :]
            u = u_ref[...].astype(F32)
            for k in range(KW):
                dw_ref[k:k + 1, :] += jnp.sum(ahead[KW - 1 - k] * u, axis=0, keepdims=True)
            db_ref[...] += jnp.sum(dc, axis=0, keepdims=True)

        back(dcv, uv_ref, cwv_ref[...], 0, dv_ref, dwv_ref, dbv_ref)
        back(dcg, ug_ref, cwg_ref[...], 1, dg_ref, dwg_ref, dbg_ref)

    rev = lambda i: nrow - 1 - i
    tile = pl.BlockSpec((tm, tn), lambda j, i: (rev(i), j))
    cspec = lambda half: pl.BlockSpec((KW, tn), lambda j, i: (0, half * ncol + j))
    acc = lambda rows: pl.BlockSpec((rows, tn), lambda j, i: (0, j))
    big = pltpu.HBM((S, F), BF16)
    blk = _nbytes((tm, D), BF16) + _nbytes((tn, D), BF16) + 10 * _nbytes((tm, tn), F32)
    in_specs, out_specs, out_shape, scratch = _carry_specs(
        rider, [pl.BlockSpec((tm, D), lambda j, i: (rev(i), 0)), pl.BlockSpec((tn, D), lambda j, i: (j, 0)),
                tile, tile, tile, tile, cspec(0), cspec(1)],
        (tile, tile, acc(KW), acc(KW), acc(1), acc(1)),
        (big, big, pltpu.HBM((KW, F), F32), pltpu.HBM((KW, F), F32), pltpu.HBM((1, F), F32),
         pltpu.HBM((1, F), F32)), [pltpu.VMEM((2, FFN_HALO, tn), F32)])
    first = lambda: (pl.program_id(0) == 0) & (pl.program_id(1) == 0)
    last = lambda: (pl.program_id(0) == ncol - 1) & (pl.program_id(1) == nrow - 1)
    res = _pallas(
        _carry(rider, body, 8, 6, first, last), name=name, grid=(ncol, nrow), out_shape=out_shape,
        in_specs=in_specs, out_specs=out_specs, scratch_shapes=scratch,
        compiler_params=_params(blk, ("arbitrary", "arbitrary")))(
            dzb, w_down, up_v, up_g, conv_v, conv_g, conv_w, conv_w, *(rider.arrays if rider else ()))
    return res[:6], list(res[6:])


def loss_head(y, target, *, tm, name):
    S, D = y.shape

    def body(y_ref, t_ref, dy_ref, l_ref):
        @pl.when(pl.program_id(0) == 0)
        def _():
            l_ref[...] = jnp.zeros_like(l_ref)

        e = y_ref[...] - t_ref[...]
        dy_ref[...] = e * (1.0 / D)
        l_ref[...] += 0.5 * jnp.sum(jnp.mean(e * e, axis=-1, keepdims=True), axis=0, keepdims=True)

    row = lambda i: (i, 0)
    return _pallas(
        body, name=name, grid=(S // tm,),
        out_shape=(pltpu.HBM((S, D), F32), pltpu.HBM((1, 1), F32)),
        in_specs=[pl.BlockSpec((tm, D), row), pl.BlockSpec((tm, D), row)],
        out_specs=(pl.BlockSpec((tm, D), row), pl.BlockSpec((1, 1), lambda i: (0, 0))),
        compiler_params=_params(3 * _nbytes((tm, D), F32), ("arbitrary",)))(y, target)


def adamw(w, g, m, v, *, tr, name):
    R, C = w.shape
    c1 = 1.0 - ADAM_B1 ** ADAM_STEP
    c2 = 1.0 - ADAM_B2 ** ADAM_STEP

    def body(w_ref, g_ref, m_ref, v_ref, go_ref, d_ref, mo_ref, vo_ref):
        gv = g_ref[...]
        mn = ADAM_B1 * m_ref[...] + (1.0 - ADAM_B1) * gv
        vn = ADAM_B2 * v_ref[...] + (1.0 - ADAM_B2) * (gv * gv)
        go_ref[...] = gv
        mo_ref[...] = mn
        vo_ref[...] = vn
        d_ref[...] = -ADAM_LR * ((mn / c1) / (jnp.sqrt(vn / c2) + ADAM_EPS) + ADAM_WD * w_ref[...])

    spec = pl.BlockSpec((tr, C), lambda i: (i, 0))
    shape = pltpu.HBM((R, C), F32)
    return _pallas(
        body, name=name, grid=(R // tr,), out_shape=(shape,) * 4, in_specs=[spec] * 4, out_specs=(spec,) * 4,
        compiler_params=_params(8 * _nbytes((tr, C), F32), ("parallel",)))(w, g, m, v)


def add_pairs(gs, gots, core, *, name):
    k = len(gs)

    def body(c_ref, *refs):
        for a_ref, b_ref, o_ref in zip(refs[:k], refs[k:2 * k], refs[2 * k:]):
            o_ref[...] = (a_ref[...].astype(F32) + b_ref[...].astype(F32)).astype(BF16)

    own = [pl.BlockSpec((None, None) + g.shape[2:], lambda i, c: (i, c[0], 0, 0)) for g in gs]
    half = [pl.BlockSpec((None,) + g.shape[1:], lambda i, c: (i, 0, 0)) for g in gots]
    grid_spec = pltpu.PrefetchScalarGridSpec(
        num_scalar_prefetch=1, grid=(N_CHIPS,), in_specs=own + half, out_specs=tuple(half))
    blk = 3 * sum(_nbytes(g.shape[1:], BF16) for g in gots)
    return _pallas(
        body, name=name, grid_spec=grid_spec, out_shape=tuple(pltpu.HBM(g.shape, BF16) for g in gots),
        compiler_params=_params(blk, ("parallel",)))(core, *gs, *gots)


def sum_chips_into(bs, dests, layer, core, *, name):
    k = len(bs)
    steps = 2

    def body(c_ref, *refs):
        for b_ref, o_ref in zip(refs[:k], refs[2 * k:]):
            acc = b_ref[0].astype(F32)
            for p in range(1, N_CHIPS):
                acc = acc + b_ref[p].astype(F32)
            o_ref[...] = acc

    ins = [pl.BlockSpec((N_CHIPS, b.shape[1] // steps, b.shape[2]), lambda i, c: (0, i, 0)) for b in bs]
    outs = tuple(pl.BlockSpec((None, None, b.shape[1] // steps, b.shape[2]), lambda i, c: (layer, c[0], i, 0))
                 for b in bs)
    grid_spec = pltpu.PrefetchScalarGridSpec(
        num_scalar_prefetch=1, grid=(steps,), in_specs=ins + [pl.BlockSpec(memory_space=pl.ANY)] * k,
        out_specs=outs)
    blk = sum(_nbytes(b.shape, BF16) + _nbytes(b.shape[1:], F32) for b in bs) // steps
    return _pallas(
        body, name=name, grid_spec=grid_spec, out_shape=tuple(pltpu.HBM(d.shape, F32) for d in dests),
        input_output_aliases={1 + k + w: w for w in range(k)},
        compiler_params=_params(blk, ("parallel",)))(core, *bs, *dests)


_HBM = pl.BlockSpec(memory_space=pltpu.HBM)


def _place():
    x, y, c = lax.axis_index("x"), lax.axis_index("y"), lax.axis_index("c")
    chips = [(1 - x, y), (x, 1 - y), (1 - x, 1 - y)]
    return x, y, c, chips


class GatherRider:
    def __init__(self, shards):
        self.arrays = list(shards)
        self.n = n = len(shards)
        self.out_shape = tuple(pltpu.HBM((N_CHIPS,) + s.shape, s.dtype) for s in shards)
        self.scratch = [pltpu.SemaphoreType.DMA((n, 3))] * 4 + [pltpu.SemaphoreType.DMA((n,))]

    def _copies(self, ins, outs, sems):
        send_ici, recv_ici, send_d2d, recv_d2d, local = sems
        x, y, c, chips = _place()
        me = 2 * x + y

        def own(w):
            return pltpu.make_async_copy(ins[w], outs[w].at[me], local.at[w])

        def ici(w, j):
            px, py = chips[j]
            return pltpu.make_async_remote_copy(
                src_ref=ins[w].at[c], dst_ref=outs[w].at[me, c], send_sem=send_ici.at[w, j],
                recv_sem=recv_ici.at[w, j], device_id=(px, py, c), device_id_type=MESH)

        def landed(w, j, half):
            px, py = chips[j]
            return outs[w].at[2 * px + py, half]

        def d2d(w, j, half):
            return pltpu.make_async_remote_copy(
                src_ref=landed(w, j, half), dst_ref=landed(w, j, half), send_sem=send_d2d.at[w, j],
                recv_sem=recv_d2d.at[w, j], device_id=(x, y, 1 - c), device_id_type=MESH)

        def ici_arrival(w, j):
            return pltpu.make_async_remote_copy(
                src_ref=landed(w, j, c), dst_ref=landed(w, j, c), send_sem=send_ici.at[w, j],
                recv_sem=recv_ici.at[w, j], device_id=(x, y, c), device_id_type=MESH)

        return c, own, ici, d2d, ici_arrival

    def start(self, ins, outs, sems):
        c, own, ici, d2d, ici_arrival = self._copies(ins, outs, sems)
        for w in range(self.n):
            own(w).start()
            for j in range(3):
                ici(w, j).start()

    def finish(self, ins, outs, sems):
        c, own, ici, d2d, ici_arrival = self._copies(ins, outs, sems)
        for w in range(self.n):
            for j in range(3):
                ici_arrival(w, j).wait_recv()
                d2d(w, j, c).start()
        for w in range(self.n):
            for j in range(3):
                d2d(w, j, 1 - c).wait_recv()
        for w in range(self.n):
            for j in range(3):
                ici(w, j).wait_send()
                d2d(w, j, c).wait_send()
            own(w).wait()


class ScatterRider:
    def __init__(self, parts):
        self.arrays = list(parts)
        self.n = n = len(parts)
        self.out_shape = tuple(pltpu.HBM(p.shape, p.dtype) for p in parts)
        self.scratch = [pltpu.SemaphoreType.DMA((n, 3))] * 2 + [pltpu.SemaphoreType.DMA((n,))]

    def _copies(self, ins, outs, sems):
        send, recv, local = sems
        x, y, c, chips = _place()
        me = 2 * x + y

        def own(w):
            return pltpu.make_async_copy(ins[w].at[me], outs[w].at[me], local.at[w])

        def copy(w, j):
            px, py = chips[j]
            return pltpu.make_async_remote_copy(
                src_ref=ins[w].at[2 * px + py], dst_ref=outs[w].at[me], send_sem=send.at[w, j],
                recv_sem=recv.at[w, j], device_id=(px, py, c), device_id_type=MESH)

        def arrival(w, j):
            px, py = chips[j]
            blk = outs[w].at[2 * px + py]
            return pltpu.make_async_remote_copy(
                src_ref=blk, dst_ref=blk, send_sem=send.at[w, j], recv_sem=recv.at[w, j],
                device_id=(x, y, c), device_id_type=MESH)

        return own, copy, arrival

    def start(self, ins, outs, sems):
        own, copy, arrival = self._copies(ins, outs, sems)
        for w in range(self.n):
            own(w).start()
            for j in range(3):
                copy(w, j).start()

    def finish(self, ins, outs, sems):
        own, copy, arrival = self._copies(ins, outs, sems)
        for w in range(self.n):
            for j in range(3):
                arrival(w, j).wait_recv()
        for w in range(self.n):
            for j in range(3):
                copy(w, j).wait_send()
            own(w).wait()


def _carry(rider, body, n_in, n_out, first, last):
    if rider is None:
        return body
    k, m = rider.n, len(rider.scratch)

    def carried(*refs):
        ins, r_in = refs[:n_in], refs[n_in:n_in + k]
        outs, r_out = refs[n_in + k:n_in + k + n_out], refs[n_in + k + n_out:n_in + 2 * k + n_out]
        rest = refs[n_in + 2 * k + n_out:]
        scratch, sems = rest[:len(rest) - m], rest[len(rest) - m:]

        @pl.when(first())
        def _():
            rider.start(r_in, r_out, sems)

        body(*ins, *outs, *scratch)

        @pl.when(last())
        def _():
            rider.finish(r_in, r_out, sems)

    return carried


def _carry_specs(rider, in_specs, out_specs, out_shape, scratch):
    if rider is None:
        return list(in_specs), tuple(out_specs), tuple(out_shape), list(scratch)
    k = rider.n
    return (list(in_specs) + [_HBM] * k, tuple(out_specs) + (_HBM,) * k, tuple(out_shape) + rider.out_shape,
            list(scratch) + list(rider.scratch))


def run_rider(rider, *, name):
    k = rider.n

    def body(*refs):
        rider.start(refs[:k], refs[k:2 * k], refs[2 * k:])
        rider.finish(refs[:k], refs[k:2 * k], refs[2 * k:])

    return _pallas(body, name=name, out_shape=rider.out_shape, in_specs=[_HBM] * k, out_specs=(_HBM,) * k,
                   scratch_shapes=rider.scratch)(*rider.arrays)


def allgather_small(shards, *, name):
    n = len(shards)

    def body(*refs):
        ins, outs = refs[:n], refs[n:2 * n]
        send, recv, local = refs[2 * n:]
        x, y, c, chips = _place()
        me = 2 * x + y
        locals_ = [pltpu.make_async_copy(ins[w], outs[w].at[me], local.at[w]) for w in range(n)]
        for cp in locals_:
            cp.start()

        def copy(w, j):
            px, py = chips[j]
            return pltpu.make_async_remote_copy(
                src_ref=ins[w], dst_ref=outs[w].at[me], send_sem=send.at[w, j], recv_sem=recv.at[w, j],
                device_id=(px, py, c), device_id_type=MESH)

        def arrival(w, j):
            px, py = chips[j]
            blk = outs[w].at[2 * px + py]
            return pltpu.make_async_remote_copy(
                src_ref=blk, dst_ref=blk, send_sem=send.at[w, j], recv_sem=recv.at[w, j],
                device_id=(x, y, c), device_id_type=MESH)

        for w in range(n):
            for j in range(3):
                copy(w, j).start()
        for w in range(n):
            for j in range(3):
                arrival(w, j).wait_recv()
        for w in range(n):
            for j in range(3):
                copy(w, j).wait_send()
        for cp in locals_:
            cp.wait()

    out_shape = tuple(pltpu.HBM((N_CHIPS,) + s.shape, s.dtype) for s in shards)
    return _pallas(
        body, name=name, out_shape=out_shape, in_specs=[_HBM] * n, out_specs=(_HBM,) * n,
        scratch_shapes=[pltpu.SemaphoreType.DMA((n, 3))] * 2 + [pltpu.SemaphoreType.DMA((n,))],
    )(*shards)


class SwapRider:
    def __init__(self, grads):
        self.arrays = list(grads)
        self.n = n = len(grads)
        self.out_shape = tuple(pltpu.HBM((N_CHIPS,) + g.shape[2:], g.dtype) for g in grads)
        self.scratch = [pltpu.SemaphoreType.DMA((n,))] * 2

    def _copies(self, ins, outs, sems):
        send, recv = sems
        x, y, c, _ = _place()
        return [pltpu.make_async_remote_copy(
            src_ref=ins[w].at[:, 1 - c], dst_ref=outs[w], send_sem=send.at[w], recv_sem=recv.at[w],
            device_id=(x, y, 1 - c), device_id_type=MESH) for w in range(self.n)]

    def start(self, ins, outs, sems):
        for cp in self._copies(ins, outs, sems):
            cp.start()

    def finish(self, ins, outs, sems):
        copies = self._copies(ins, outs, sems)
        for cp in copies:
            cp.wait_recv()
        for cp in copies:
            cp.wait_send()


def rs_sibling_share(stacked, *, name):
    n = len(stacked)

    def body(*refs):
        bufs = refs[n:2 * n]
        send, recv = refs[2 * n:]
        x, y, c, _ = _place()
        shares, arrivals = [], []
        for w in range(n):
            mine, other = bufs[w].at[:, c], bufs[w].at[:, 1 - c]
            shares.append(pltpu.make_async_remote_copy(
                src_ref=mine, dst_ref=mine, send_sem=send.at[w], recv_sem=recv.at[w],
                device_id=(x, y, 1 - c), device_id_type=MESH))
            arrivals.append(pltpu.make_async_remote_copy(
                src_ref=other, dst_ref=other, send_sem=send.at[w], recv_sem=recv.at[w],
                device_id=(x, y, c), device_id_type=MESH))
        for cp in shares:
            cp.start()
        for cp in arrivals:
            cp.wait_recv()
        for cp in shares:
            cp.wait_send()

    out_shape = tuple(pltpu.HBM(s.shape, F32) for s in stacked)
    return _pallas(
        body, name=name, out_shape=out_shape, in_specs=[_HBM] * n, out_specs=(_HBM,) * n,
        input_output_aliases={w: w for w in range(n)},
        scratch_shapes=[pltpu.SemaphoreType.DMA((n,))] * 2,
    )(*stacked)


def allreduce_small(v, *, name):
    R, C = v.shape

    def body(v_ref, o_ref, land, send, recv):
        x, y, c, _ = _place()
        me = 4 * x + 2 * y + c
        land[me] = v_ref[...]

        def flip(k):
            return (1 - x) if k & 4 else x, (1 - y) if k & 2 else y, (1 - c) if k & 1 else c

        copies = []
        for k in range(1, N_DEV):
            px, py, pc = flip(k)
            copies.append(pltpu.make_async_remote_copy(
                src_ref=v_ref, dst_ref=land.at[me], send_sem=send.at[k - 1], recv_sem=recv.at[k - 1],
                device_id=(px, py, pc), device_id_type=MESH))
        for cp in copies:
            cp.start()
        for k in range(1, N_DEV):
            px, py, pc = flip(k)
            blk = land.at[4 * px + 2 * py + pc]
            pltpu.make_async_remote_copy(
                src_ref=blk, dst_ref=blk, send_sem=send.at[k - 1], recv_sem=recv.at[k - 1],
                device_id=(x, y, c), device_id_type=MESH).wait_recv()
        for cp in copies:
            cp.wait_send()
        acc = land[0]
        for d in range(1, N_DEV):
            acc = acc + land[d]
        o_ref[...] = acc

    vm = pl.BlockSpec(memory_space=pltpu.VMEM)
    return pl.pallas_call(
        body, name=name, out_shape=jax.ShapeDtypeStruct((R, C), F32), in_specs=[vm], out_specs=vm,
        scratch_shapes=[pltpu.VMEM((N_DEV, R, C), F32), pltpu.SemaphoreType.DMA((N_DEV - 1,)),
                        pltpu.SemaphoreType.DMA((N_DEV - 1,))],
        compiler_params=pltpu.CompilerParams(vmem_limit_bytes=int(min(12 * R * C * 4 + (8 << 20), VMEM_CAP))),
    )(v)


def _pack(arrays):
    flat = jnp.concatenate([a.reshape(-1) for a in arrays])
    return flat.reshape(-1, LANES)


def _unpack(packed, shapes):
    flat = packed.reshape(-1)
    out, off = [], 0
    for s in shapes:
        n = 1
        for d in s:
            n *= d
        out.append(flat[off:off + n].reshape(s))
        off += n
    return out


def _row_tile(rows, cap=512):
    t = 1 << (cap.bit_length() - 1)
    while rows % t:
        t //= 2
    return t


def _adamw_tile(rows, cols):
    return _row_tile(rows, max(8, (1 << 20) // (4 * cols)))


def kernel(x, mem, w_in, conv_w, conv_b, conv_ln_g, conv_ln_b, w_out, ln1_g, ln1_b, mem_wq, mem_wk, mem_wv, mem_wo, ln2_g, ln2_b, ffn_up, ffn_conv_w, ffn_conv_b, ffn_down, ln3_g, ln3_b, loss_target, m_w_in, m_conv_w, m_conv_b, m_conv_ln_g, m_conv_ln_b, m_w_out, m_ln1_g, m_ln1_b, m_mem_wq, m_mem_wk, m_mem_wv, m_mem_wo, m_ln2_g, m_ln2_b, m_ffn_up, m_ffn_conv_w, m_ffn_conv_b, m_ffn_down, m_ln3_g, m_ln3_b, v_w_in, v_conv_w, v_conv_b, v_conv_ln_g, v_conv_ln_b, v_w_out, v_ln1_g, v_ln1_b, v_mem_wq, v_mem_wk, v_mem_wv, v_mem_wo, v_ln2_g, v_ln2_b, v_ffn_up, v_ffn_conv_w, v_ffn_conv_b, v_ffn_down, v_ln3_g, v_ln3_b):
    W = dict(w_in=w_in, conv_w=conv_w, conv_b=conv_b, conv_ln_g=conv_ln_g, conv_ln_b=conv_ln_b, w_out=w_out,
             ln1_g=ln1_g, ln1_b=ln1_b, mem_wq=mem_wq, mem_wk=mem_wk, mem_wv=mem_wv, mem_wo=mem_wo, ln2_g=ln2_g,
             ln2_b=ln2_b, ffn_up=ffn_up, ffn_conv_w=ffn_conv_w, ffn_conv_b=ffn_conv_b, ffn_down=ffn_down,
             ln3_g=ln3_g, ln3_b=ln3_b)
    M1 = dict(w_in=m_w_in, conv_w=m_conv_w, conv_b=m_conv_b, conv_ln_g=m_conv_ln_g, conv_ln_b=m_conv_ln_b,
              w_out=m_w_out, ln1_g=m_ln1_g, ln1_b=m_ln1_b, mem_wq=m_mem_wq, mem_wk=m_mem_wk, mem_wv=m_mem_wv,
              mem_wo=m_mem_wo, ln2_g=m_ln2_g, ln2_b=m_ln2_b, ffn_up=m_ffn_up, ffn_conv_w=m_ffn_conv_w,
              ffn_conv_b=m_ffn_conv_b, ffn_down=m_ffn_down, ln3_g=m_ln3_g, ln3_b=m_ln3_b)
    V2 = dict(w_in=v_w_in, conv_w=v_conv_w, conv_b=v_conv_b, conv_ln_g=v_conv_ln_g, conv_ln_b=v_conv_ln_b,
              w_out=v_w_out, ln1_g=v_ln1_g, ln1_b=v_ln1_b, mem_wq=v_mem_wq, mem_wk=v_mem_wk, mem_wv=v_mem_wv,
              mem_wo=v_mem_wo, ln2_g=v_ln2_g, ln2_b=v_ln2_b, ffn_up=v_ffn_up, ffn_conv_w=v_ffn_conv_w,
              ffn_conv_b=v_ffn_conv_b, ffn_down=v_ffn_down, ln3_g=v_ln3_g, ln3_b=v_ln3_b)

    L = w_in.shape[0]
    S, D = x.shape[1], x.shape[2]
    C = conv_b.shape[1]
    alpha = (2.0 * L) ** 0.25
    chip = 2 * lax.axis_index("x") + lax.axis_index("y")
    xs, mems, tgt = x[0], mem[0], loss_target[0]
    mem_bf = mems.astype(BF16)
    tm = _row_tile(S)
    tm_ffn = _row_tile(S, 256)
    tm_big = _row_tile(S, 1024)

    def shards_of(l, names):
        out = []
        for n in names:
            wl = W[n][l].astype(BF16)
            out.append(wl.reshape(2, wl.shape[0] // 2, wl.shape[1]))
        return out

    def gathered(names, got):
        layer = {}
        for n, g in zip(names, got):
            rows, cols = W[n].shape[1], W[n].shape[2]
            layer[n] = g.reshape(N_CHIPS, rows, cols) if n in COL_SHARDED else g.reshape(N_CHIPS * rows, cols)
        return layer

    full = [dict() for _ in range(L)]
    full[0].update(gathered(RIDE_IN, run_rider(GatherRider(shards_of(0, RIDE_IN)), name="allgather_w_in")))
    cw_all, fcw_all = allgather_small([conv_w, ffn_conv_w], name="allgather_small")
    cw_full = jnp.transpose(cw_all, (1, 2, 0, 3)).reshape(L, conv_w.shape[1], -1)
    fcw_full = jnp.transpose(fcw_all, (1, 2, 0, 3)).reshape(L, ffn_conv_w.shape[1], -1)

    saved = []
    h, hb = xs, xs.astype(BF16)
    for l in range(L):
        fw = full[l]
        s = dict(x=h, xb=hb)
        s['proj'] = mm_nn(hb, fw['w_in'], F32, tm=min(1024, S), tn=fw['w_in'].shape[2], name="proj")
        s['u1'], got = conv_fwd(s['proj'], cw_full[l], conv_b[l][None], name="conv_fwd",
                                rider=GatherRider(shards_of(0, RIDE_ATT)) if l == 0 else None)
        if l == 0:
            fw.update(gathered(RIDE_ATT, got))
        more = l + 1 < L
        s['o_sb'], s['ltot'], got = sb_fwd(
            s['proj'], q_col=2 * C, name="sb_fwd", rider=GatherRider(shards_of(l, RIDE_FFN)))
        fw.update(gathered(RIDE_FFN, got))
        s['ua'] = ln_silu(s['u1'], s['o_sb'], conv_ln_g[l][None], conv_ln_b[l][None], tm=tm, name="ln_silu")
        s['x1'], s['x1b'], s['zh1'], s['rs1'] = mm_ln(
            s['ua'], fw['w_out'], h, ln1_g[l][None], ln1_b[l][None], alpha, tm=tm, name="out_proj_ln")
        s['q2'] = mm_nn(s['x1b'], fw['mem_wq'], BF16, tm=min(1024, S), tn=512, name="mem_q")
        s['k2'] = mm_nn(mem_bf, fw['mem_wk'], BF16, tm=mem_bf.shape[0], tn=512, name="mem_kv")
        s['v2'] = mm_nn(mem_bf, fw['mem_wv'], BF16, tm=mem_bf.shape[0], tn=512, name="mem_kv")
        s['o2'] = xattn_fwd(s['q2'], s['k2'], s['v2'], tm=tm, name="xattn_fwd")
        s['x2'], s['x2b'], s['zh2'], s['rs2'] = mm_ln(
            s['o2'], fw['mem_wo'], s['x1'], ln2_g[l][None], ln2_b[l][None], alpha, tm=tm, name="mem_o_ln")
        (s['upv'], s['upg'], s['cv'], s['cg'], s['hmid']), got = ffn_up_fwd(
            s['x2b'], fw['ffn_up'], fcw_full[l], ffn_conv_b[l][None], tm=tm_ffn, tn=fw['ffn_up'].shape[2],
            name="ffn_up_fwd", rider=GatherRider(shards_of(l + 1, RIDE_ATT + RIDE_IN)) if more else None)
        if more:
            full[l + 1].update(gathered(RIDE_ATT + RIDE_IN, got))
        h, hb, s['zh3'], s['rs3'] = mm_ln(
            s['hmid'], fw['ffn_down'], s['x2'], ln3_g[l][None], ln3_b[l][None], alpha, tm=tm, name="ffn_down_ln")
        saved.append(s)

    dx, loss_part = loss_head(h, tgt, tm=tm, name="loss_head")
    loss = lax.psum(loss_part[0, 0], ("x", "y", "c"))

    core = lax.axis_index("c").astype(jnp.int32).reshape(1)
    reduced_big = {n: lax.empty((L, 2, W[n].shape[1] // 2, W[n].shape[2]), F32) for n in BIG}
    small_grads = [None] * L

    def row_halves(g, names):
        parts = []
        for n in names:
            rows, cols = W[n].shape[1], W[n].shape[2]
            parts.append(g[n].reshape(N_CHIPS, 2, rows // 2, cols))
        return parts

    def pre_add(g, names):
        parts = row_halves(g, names)
        got = run_rider(SwapRider(parts), name="rs_sibling_swap")
        return list(add_pairs(parts, got, core, name="rs_add_pairs"))

    def reduce_into(names, scattered, layer):
        reduced_big.update(zip(names, sum_chips_into(
            list(scattered), [reduced_big[n] for n in names], layer, core, name="rs_sum_chips")))

    pending = None
    for l in reversed(range(L)):
        fw, s = full[l], saved[l]
        g = {}
        if l == L - 1:
            top = ln_bwd(dx, s['zh3'], s['rs3'], ln3_g[l][None], tm=tm, name="ln_bwd")
        dz3, dz3b, g['ln3_g'], g['ln3_b'] = top
        ftn = fw['ffn_up'].shape[2]
        (dupv, dupg, dfw_v, dfw_g, dfb_v, dfb_g), sc = ffn_mid_bwd(
            dz3b, fw['ffn_down'], s['upv'], s['upg'], s['cv'], s['cg'], fcw_full[l], tm=tm_ffn, tn=ftn,
            name="ffn_mid_bwd", rider=ScatterRider(pending) if pending else None)
        if pending:
            reduce_into(RIDE_MIX, sc, l + 1)
        g['ffn_conv_w'] = jnp.concatenate([dfw_v, dfw_g], axis=1)
        g['ffn_conv_b'] = jnp.concatenate([dfb_v, dfb_g], axis=1)[0]
        g['ffn_down'] = mm_tn(s['hmid'], [dz3b], tk=ftn, tn=512, tmc=min(1024, S), name="grad_ffn_down")
        dz2, dz2b, g['ln2_g'], g['ln2_b'] = mm_nt_ln_bwd(
            [dupv, dupg], fw['ffn_up'], dz3, alpha, s['zh2'], s['rs2'], ln2_g[l][None], tm=tm_ffn,
            name="ffn_up_bwd")
        g['ffn_up'] = mm_tn(s['x2b'], [dupv, dupg], tk=512, tn=ftn, shard_width=ftn, tmc=min(1024, S),
                            name="grad_ffn_up")

        do2 = mm_nt([dz2b], fw['mem_wo'], BF16, tm=tm_big, tk=512, name="mem_o_bwd")
        g['mem_wo'] = mm_tn(s['o2'], [dz2b], tk=512, tn=512, name="grad_sq")
        dq2, dk2, dv2 = xattn_bwd(s['q2'], do2, s['k2'], s['v2'], tm=tm, name="xattn_bwd")
        dz1, dz1b, g['ln1_g'], g['ln1_b'] = mm_nt_ln_bwd(
            [dq2], fw['mem_wq'], dz2, alpha, s['zh1'], s['rs1'], ln1_g[l][None], tm=tm, name="mem_q_bwd")
        g['mem_wq'] = mm_tn(s['x1b'], [dq2], tk=512, tn=512, name="grad_sq")
        g['mem_wk'] = mm_tn(mem_bf, [dk2], tk=512, tn=512, name="grad_mem_kv")
        g['mem_wv'] = mm_tn(mem_bf, [dv2], tk=512, tn=512, name="grad_mem_kv")

        rest = row_halves(g, RIDE_REST)
        dua, got = mm_nt([dz1b], fw['w_out'], F32, tm=tm_big, tk=512, name="out_proj_bwd", rider=SwapRider(rest))
        rest = list(add_pairs(rest, got, core, name="rs_add_pairs"))
        g['w_out'] = mm_tn(s['ua'], [dz1b], tk=512, tn=512, name="grad_sq")
        dq, dk, dv, sc = sb_bwd(
            s['proj'], s['ltot'], dua, q_col=2 * C, do_col=C, name="sb_bwd",
            rider=ScatterRider(rest))
        reduce_into(RIDE_REST, sc, l)
        du1, g['conv_ln_g'], g['conv_ln_b'] = ln_silu_bwd(
            dua, s['u1'], conv_ln_g[l][None], conv_ln_b[l][None], tm=tm, name="ln_silu_bwd")
        da, dg, g['conv_w'], dcb = conv_bwd(du1, s['proj'], cw_full[l], name="conv_bwd")
        g['conv_b'] = dcb
        dproj = jnp.concatenate([da, dg, dq, dk, dv], axis=1)
        ns_in = fw['w_in'].shape[2]
        if l > 0:
            below = saved[l - 1]
            top = mm_nt_ln_bwd([dproj], fw['w_in'], dz1, alpha, below['zh3'], below['rs3'], ln3_g[l - 1][None],
                               tm=tm, name="proj_bwd")
        else:
            dx = mm_nt([dproj], fw['w_in'], F32, tm=tm_big, tk=512, res=dz1, alpha=alpha, name="proj_bwd_x")
        g['w_in'] = mm_tn(s['xb'], [dproj], tk=512, tn=ns_in, shard_width=ns_in, name="grad_w_in")

        pending = pre_add(g, RIDE_MIX)
        small_grads[l] = {n: g[n].reshape(W[n].shape[1:-1] + (-1,)) for n in SMALL}

    grad_x = dx[None]

    reduce_into(RIDE_MIX, run_rider(ScatterRider(pending), name="rs_chip_scatter"), 0)
    shared = rs_sibling_share([reduced_big[n] for n in BIG], name="rs_sibling_share")
    G = {}
    for n, sh in zip(BIG, shared):
        G[n] = sh.reshape(W[n].shape)

    small_full_shapes = []
    small_stack = []
    for n in SMALL:
        st = jnp.stack([small_grads[l][n] for l in range(L)])
        small_stack.append(st)
        small_full_shapes.append(st.shape)
    reduced = _unpack(allreduce_small(_pack(small_stack), name="allreduce_small"), small_full_shapes)
    for n, r in zip(SMALL, reduced):
        if n in SMALL_SHARDED:
            width = W[n].shape[-1]
            r = lax.dynamic_slice_in_dim(r, chip * width, width, axis=2)
        G[n] = r

    out_g, out_d, out_m, out_v = {}, {}, {}, {}
    for n in BIG:
        shp = W[n].shape
        flat = lambda a: a.reshape(shp[0] * shp[1], shp[2])
        res = adamw(flat(W[n]), flat(G[n]), flat(M1[n]), flat(V2[n]), tr=_adamw_tile(shp[0] * shp[1], shp[2]), name="adamw")
        out_g[n], out_d[n], out_m[n], out_v[n] = [r.reshape(shp) for r in res]
    small_shapes = [W[n].shape for n in SMALL]
    packed = [_pack([d[n] for n in SMALL]) for d in (W, G, M1, V2)]
    res = adamw(*packed, tr=packed[0].shape[0], name="adamw_small")
    for d, r in zip((out_g, out_d, out_m, out_v), res):
        for n, a in zip(SMALL, _unpack(r, small_shapes)):
            d[n] = a

    return (loss, grad_x, *[out_g[n] for n in WEIGHTS], *[out_d[n] for n in WEIGHTS],
            *[out_m[n] for n in WEIGHTS], *[out_v[n] for n in WEIGHTS])
```

```python
import functools

import jax
import jax.numpy as jnp
from jax import lax
from jax.experimental import pallas as pl
from jax.experimental.pallas import tpu as pltpu

F32 = jnp.float32
BF16 = jnp.bfloat16
MESH = pl.DeviceIdType.MESH

LN_EPS = 1e-5
SB_HEADS = 8
MEM_HEADS = 4
ADAM_LR, ADAM_B1, ADAM_B2, ADAM_EPS, ADAM_WD, ADAM_STEP = 0.001, 0.9, 0.999, 1e-08, 0.01, 10

LANES = 128
V7X_VMEM_BYTES = 64 << 20
VMEM_CAP = V7X_VMEM_BYTES - (6 << 20)
N_CHIPS = 4
N_DEV = 8

BIG = ('w_in', 'w_out', 'mem_wq', 'mem_wk', 'mem_wv', 'mem_wo', 'ffn_up', 'ffn_down')
RIDE_IN = ('w_in',)
RIDE_ATT = ('w_out', 'mem_wq', 'mem_wk', 'mem_wv', 'mem_wo')
RIDE_FFN = ('ffn_up', 'ffn_down')
RIDE_MIX = ('w_in', 'w_out')
RIDE_REST = ('mem_wq', 'mem_wk', 'mem_wv', 'mem_wo', 'ffn_up', 'ffn_down')
COL_SHARDED = ('w_in', 'ffn_up')
SMALL = ('conv_w', 'conv_b', 'conv_ln_g', 'conv_ln_b', 'ln1_g', 'ln1_b', 'ln2_g', 'ln2_b',
         'ffn_conv_w', 'ffn_conv_b', 'ln3_g', 'ln3_b')
SMALL_SHARDED = ('conv_w', 'ffn_conv_w')
WEIGHTS = ('w_in', 'conv_w', 'conv_b', 'conv_ln_g', 'conv_ln_b', 'w_out', 'ln1_g', 'ln1_b',
           'mem_wq', 'mem_wk', 'mem_wv', 'mem_wo', 'ln2_g', 'ln2_b', 'ffn_up', 'ffn_conv_w',
           'ffn_conv_b', 'ffn_down', 'ln3_g', 'ln3_b')


def _params(block_bytes, semantics=None, **kw):
    limit = int(min(max(2 * block_bytes + (8 << 20), 32 << 20), VMEM_CAP))
    return pltpu.CompilerParams(dimension_semantics=semantics, vmem_limit_bytes=limit, **kw)


def _pallas(body, **kw):
    call = pl.pallas_call(body, **kw)

    def run(*args):
        return call(*[pltpu.with_memory_space_constraint(a, pltpu.HBM)
                      if jnp.issubdtype(a.dtype, jnp.floating) else a for a in args])

    return run


def _nbytes(shape, dtype):
    n = 1
    for s in shape:
        n *= s
    return n * jnp.dtype(dtype).itemsize


def _dot(a, b):
    return jnp.dot(a, b, preferred_element_type=F32)


def _dot_nt(a, b):
    return lax.dot_general(a, b, (((1,), (1,)), ((), ())), preferred_element_type=F32)


def _dot_tn(a, b):
    return lax.dot_general(a, b, (((0,), (0,)), ((), ())), preferred_element_type=F32)


def _sigmoid(x):
    return 1.0 / (1.0 + jnp.exp(-x))


def mm_nn(a, b, out_dtype, *, tm, tn, name):
    M, K = a.shape
    sharded = b.ndim == 3
    if sharded:
        nsh, _, ns = b.shape
        N, per = nsh * ns, ns // tn
        b_spec = pl.BlockSpec((None, K, tn), lambda i, j: (j // per, 0, j % per))
    else:
        N = b.shape[1]
        b_spec = pl.BlockSpec((K, tn), lambda i, j: (0, j))

    def body(a_ref, b_ref, o_ref):
        o_ref[...] = _dot(a_ref[...].astype(BF16), b_ref[...]).astype(o_ref.dtype)

    blk = _nbytes((tm, K), a.dtype) + _nbytes((K, tn), BF16) + _nbytes((tm, tn), out_dtype)
    return _pallas(
        body, name=name, out_shape=pltpu.HBM((M, N), out_dtype), grid=(M // tm, N // tn),
        in_specs=[pl.BlockSpec((tm, K), lambda i, j: (i, 0)), b_spec],
        out_specs=pl.BlockSpec((tm, tn), lambda i, j: (i, j)),
        compiler_params=_params(blk, ("parallel", "parallel")))(a, b)


def mm_ln(a, b, x, gamma, beta, alpha, *, tm, name):
    M, K = a.shape
    D = b.shape[1]

    def body(a_ref, b_ref, x_ref, g_ref, be_ref, y_ref, yb_ref, zh_ref, rs_ref):
        z = alpha * x_ref[...] + _dot(a_ref[...], b_ref[...])
        mu = jnp.mean(z, axis=-1, keepdims=True)
        zc = z - mu
        rstd = lax.rsqrt(jnp.mean(zc * zc, axis=-1, keepdims=True) + LN_EPS)
        zh = zc * rstd
        y = zh * g_ref[...] + be_ref[...]
        y_ref[...] = y
        yb_ref[...] = y.astype(BF16)
        zh_ref[...] = zh
        rs_ref[...] = rstd

    row = lambda i: (i, 0)
    fix = lambda i: (0, 0)
    blk = _nbytes((tm, K), BF16) + _nbytes((K, D), BF16) + 4 * _nbytes((tm, D), F32)
    return _pallas(
        body, name=name, grid=(M // tm,),
        out_shape=(pltpu.HBM((M, D), F32), pltpu.HBM((M, D), BF16),
                   pltpu.HBM((M, D), F32), pltpu.HBM((M, 1), F32)),
        in_specs=[pl.BlockSpec((tm, K), row), pl.BlockSpec((K, D), fix), pl.BlockSpec((tm, D), row),
                  pl.BlockSpec((1, D), fix), pl.BlockSpec((1, D), fix)],
        out_specs=(pl.BlockSpec((tm, D), row), pl.BlockSpec((tm, D), row), pl.BlockSpec((tm, D), row),
                   pl.BlockSpec((tm, 1), row)),
        compiler_params=_params(blk, ("parallel",)))(a, b, x, gamma, beta)


def ln_bwd(dy, zh, rstd, gamma, *, tm, name):
    M, D = dy.shape

    def body(dy_ref, zh_ref, rs_ref, g_ref, dz_ref, dzb_ref, dg_ref, db_ref):
        @pl.when(pl.program_id(0) == 0)
        def _():
            dg_ref[...] = jnp.zeros_like(dg_ref)
            db_ref[...] = jnp.zeros_like(db_ref)

        dyv, zhv = dy_ref[...], zh_ref[...]
        dg_ref[...] += jnp.sum(dyv * zhv, axis=0, keepdims=True)
        db_ref[...] += jnp.sum(dyv, axis=0, keepdims=True)
        dzh = dyv * g_ref[...]
        m1 = jnp.mean(dzh, axis=-1, keepdims=True)
        m2 = jnp.mean(dzh * zhv, axis=-1, keepdims=True)
        dz = rs_ref[...] * (dzh - m1 - zhv * m2)
        dz_ref[...] = dz
        dzb_ref[...] = dz.astype(BF16)

    row = lambda i: (i, 0)
    fix = lambda i: (0, 0)
    return _pallas(
        body, name=name, grid=(M // tm,),
        out_shape=(pltpu.HBM((M, D), F32), pltpu.HBM((M, D), BF16),
                   pltpu.HBM((1, D), F32), pltpu.HBM((1, D), F32)),
        in_specs=[pl.BlockSpec((tm, D), row), pl.BlockSpec((tm, D), row), pl.BlockSpec((tm, 1), row),
                  pl.BlockSpec((1, D), fix)],
        out_specs=(pl.BlockSpec((tm, D), row), pl.BlockSpec((tm, D), row), pl.BlockSpec((1, D), fix),
                   pl.BlockSpec((1, D), fix)),
        compiler_params=_params(4 * _nbytes((tm, D), F32), ("arbitrary",)))(dy, zh, rstd, gamma)


def mm_nt(a_list, b, out_dtype, *, tm, tk, name, res=None, alpha=None, rider=None):
    M = a_list[0].shape[0]
    widths = [a.shape[1] for a in a_list]
    sharded = b.ndim == 3
    if sharded:
        nsh, K, ns = b.shape
        b_spec = pl.BlockSpec((nsh, tk, ns), lambda i, j: (0, j, 0))
        for w in widths:
            assert w % ns == 0
    else:
        K, N = b.shape
        ns = None
        b_spec = pl.BlockSpec((tk, N), lambda i, j: (j, 0))
    n_a = len(a_list)

    def body(*refs):
        a_refs, b_ref = refs[:n_a], refs[n_a]
        o_ref = refs[-1]
        acc = None
        off = 0
        for a_ref, w in zip(a_refs, widths):
            if sharded:
                for p in range(w // ns):
                    t = _dot_nt(a_ref[:, p * ns:(p + 1) * ns].astype(BF16), b_ref[off // ns + p])
                    acc = t if acc is None else acc + t
            else:
                t = _dot_nt(a_ref[...].astype(BF16), b_ref[:, off:off + w])
                acc = t if acc is None else acc + t
            off += w
        if res is not None:
            acc = acc + alpha * refs[n_a + 1][...]
        o_ref[...] = acc.astype(o_ref.dtype)

    in_specs = [pl.BlockSpec((tm, w), lambda i, j: (i, 0)) for w in widths] + [b_spec]
    args = list(a_list) + [b]
    if res is not None:
        in_specs.append(pl.BlockSpec((tm, tk), lambda i, j: (i, j)))
        args.append(res)
    blk = (sum(_nbytes((tm, w), a.dtype) for a, w in zip(a_list, widths)) + _nbytes((tk, sum(widths)), BF16)
           + 2 * _nbytes((tm, tk), F32))
    in_specs, out_specs, out_shape, scratch = _carry_specs(
        rider, in_specs, (pl.BlockSpec((tm, tk), lambda i, j: (i, j)),), (pltpu.HBM((M, K), out_dtype),), [])
    first = lambda: (pl.program_id(0) == 0) & (pl.program_id(1) == 0)
    last = lambda: (pl.program_id(0) == M // tm - 1) & (pl.program_id(1) == K // tk - 1)
    res_all = _pallas(
        _carry(rider, body, len(args), 1, first, last), name=name, out_shape=out_shape, grid=(M // tm, K // tk),
        in_specs=in_specs, out_specs=out_specs, scratch_shapes=scratch,
        compiler_params=_params(blk, ("arbitrary", "arbitrary")))(*args, *(rider.arrays if rider else ()))
    return res_all[0] if rider is None else (res_all[0], list(res_all[1:]))


def mm_nt_ln_bwd(a_list, b, res, alpha, zh, rstd, gamma, *, tm, name):
    M, D = res.shape
    widths = [a.shape[1] for a in a_list]
    sharded = b.ndim == 3
    if sharded:
        nsh, _, ns = b.shape
        b_spec = pl.BlockSpec((nsh, D, ns), lambda i: (0, 0, 0))
    else:
        ns = None
        b_spec = pl.BlockSpec((D, b.shape[1]), lambda i: (0, 0))
    n_a = len(a_list)

    def body(*refs):
        a_refs, b_ref = refs[:n_a], refs[n_a]
        res_ref, zh_ref, rs_ref, g_ref = refs[n_a + 1:n_a + 5]
        dz_ref, dzb_ref, dg_ref, db_ref = refs[n_a + 5:]

        @pl.when(pl.program_id(0) == 0)
        def _():
            dg_ref[...] = jnp.zeros_like(dg_ref)
            db_ref[...] = jnp.zeros_like(db_ref)

        dy = alpha * res_ref[...]
        off = 0
        for a_ref, w in zip(a_refs, widths):
            if sharded:
                for p in range(w // ns):
                    dy = dy + _dot_nt(a_ref[:, p * ns:(p + 1) * ns], b_ref[off // ns + p])
            else:
                dy = dy + _dot_nt(a_ref[...], b_ref[:, off:off + w])
            off += w
        zhv = zh_ref[...]
        dg_ref[...] += jnp.sum(dy * zhv, axis=0, keepdims=True)
        db_ref[...] += jnp.sum(dy, axis=0, keepdims=True)
        dzh = dy * g_ref[...]
        m1 = jnp.mean(dzh, axis=-1, keepdims=True)
        m2 = jnp.mean(dzh * zhv, axis=-1, keepdims=True)
        dz = rs_ref[...] * (dzh - m1 - zhv * m2)
        dz_ref[...] = dz
        dzb_ref[...] = dz.astype(BF16)

    row = lambda i: (i, 0)
    fix = lambda i: (0, 0)
    in_specs = [pl.BlockSpec((tm, w), row) for w in widths] + [
        b_spec, pl.BlockSpec((tm, D), row), pl.BlockSpec((tm, D), row), pl.BlockSpec((tm, 1), row),
        pl.BlockSpec((1, D), fix)]
    blk = (sum(_nbytes((tm, w), BF16) for w in widths) + _nbytes((D, sum(widths)), BF16)
           + 5 * _nbytes((tm, D), F32))
    return _pallas(
        body, name=name, grid=(M // tm,),
        out_shape=(pltpu.HBM((M, D), F32), pltpu.HBM((M, D), BF16), pltpu.HBM((1, D), F32),
                   pltpu.HBM((1, D), F32)),
        in_specs=in_specs,
        out_specs=(pl.BlockSpec((tm, D), row), pl.BlockSpec((tm, D), row), pl.BlockSpec((1, D), fix),
                   pl.BlockSpec((1, D), fix)),
        compiler_params=_params(blk, ("arbitrary",)))(*a_list, b, res, zh, rstd, gamma)


def mm_tn(a, b_list, *, tk, tn, name, shard_width=None, tmc=None):
    M, K = a.shape
    tmc = M if tmc is None else tmc
    nm = M // tmc
    widths = [b.shape[1] for b in b_list]
    N = sum(widths)
    starts, s = [], 0
    for w in widths:
        assert w % tn == 0
        starts.append(s)
        s += w // tn
    n_b = len(b_list)

    def body(*refs):
        a_ref, b_refs, o_ref, acc = refs[0], refs[1:1 + n_b], refs[-2], refs[-1]
        j, m = pl.program_id(1), pl.program_id(2)
        for b_ref, st, w in zip(b_refs, starts, widths):
            @pl.when((j >= st) & (j < st + w // tn))
            def _(b_ref=b_ref):
                t = _dot_tn(a_ref[...].astype(BF16), b_ref[...].astype(BF16))
                if nm == 1:
                    o_ref[...] = t.astype(o_ref.dtype)
                else:
                    @pl.when(m == 0)
                    def _():
                        acc[...] = t

                    @pl.when(m > 0)
                    def _():
                        acc[...] += t

                    @pl.when(m == nm - 1)
                    def _():
                        o_ref[...] = acc[...].astype(o_ref.dtype)

    def b_map(st, w):
        nb = w // tn
        return lambda i, j, m: (jnp.where((j >= st) & (j < st + nb), m, 0), jnp.clip(j - st, 0, nb - 1))

    in_specs = [pl.BlockSpec((tmc, tk), lambda i, j, m: (m, i))]
    in_specs += [pl.BlockSpec((tmc, tn), b_map(st, w)) for st, w in zip(starts, widths)]
    if shard_width is None:
        out_shape = pltpu.HBM((K, N), BF16)
        out_spec = pl.BlockSpec((tk, tn), lambda i, j, m: (i, j))
    else:
        per = shard_width // tn
        out_shape = pltpu.HBM((N // shard_width, K, shard_width), BF16)
        out_spec = pl.BlockSpec((None, tk, tn), lambda i, j, m: (j // per, i, j % per))
    acc_shape = (tk, tn) if nm > 1 else (8, LANES)
    blk = (_nbytes((tmc, tk), a.dtype) + n_b * _nbytes((tmc, tn), b_list[0].dtype) + 2 * _nbytes((tk, tn), F32))
    return _pallas(
        body, name=name, out_shape=out_shape, grid=(K // tk, N // tn, nm), in_specs=in_specs, out_specs=out_spec,
        scratch_shapes=[pltpu.VMEM(acc_shape, F32)],
        compiler_params=_params(blk, ("parallel", "arbitrary", "arbitrary")))(a, *b_list)


CONV_PAD = 32
CONV_CHUNK = 128


def _rows(win, off, n, shifts):
    b, a = off % 8, off // 8
    if b not in shifts:
        shifts[b] = win if b == 0 else win[b:b + n + CONV_PAD - 8, :]
    return shifts[b][8 * a:8 * a + n, :]


def _by_residue(n_taps, offset):
    return sorted(range(n_taps), key=lambda k: (offset(k) % 8, k))


def conv_fwd(proj, conv_w, conv_b, *, name, rider=None):
    S = proj.shape[0]
    KW, C = conv_w.shape
    nct = C // LANES
    rc = min(CONV_CHUNK, S)

    def body(a_ref, g_ref, w_ref, b_ref, o_ref, pad):
        pad[0:CONV_PAD, :] = jnp.zeros((CONV_PAD, LANES), F32)
        pad[CONV_PAD:, :] = a_ref[...] * _sigmoid(g_ref[...])
        w = w_ref[...]
        bias = b_ref[...]

        def chunk(i, _):
            base = pl.multiple_of(i * rc, rc)
            win = pad[pl.ds(base, rc + CONV_PAD), :]
            acc = jnp.zeros((rc, LANES), F32) + bias
            shifts = {}
            for k in _by_residue(KW, lambda k: CONV_PAD - (KW - 1) + k):
                acc = acc + w[k:k + 1, :] * _rows(win, CONV_PAD - (KW - 1) + k, rc, shifts)
            o_ref[pl.ds(base, rc), :] = acc
            return 0

        lax.fori_loop(0, S // rc, chunk, 0)

    in_specs, out_specs, out_shape, scratch = _carry_specs(
        rider, [pl.BlockSpec((S, LANES), lambda c: (0, c)), pl.BlockSpec((S, LANES), lambda c: (0, c + nct)),
                pl.BlockSpec((KW, LANES), lambda c: (0, c)), pl.BlockSpec((1, LANES), lambda c: (0, c))],
        (pl.BlockSpec((S, LANES), lambda c: (0, c)),), (pltpu.HBM((S, C), F32),),
        [pltpu.VMEM((S + CONV_PAD, LANES), F32)])
    first = lambda: pl.program_id(0) == 0
    last = lambda: pl.program_id(0) == nct - 1
    res = _pallas(
        _carry(rider, body, 4, 1, first, last), name=name, grid=(nct,), out_shape=out_shape,
        in_specs=in_specs, out_specs=out_specs, scratch_shapes=scratch,
        compiler_params=_params(4 * _nbytes((S, LANES), F32), ("arbitrary",)))(
            proj, proj, conv_w, conv_b, *(rider.arrays if rider else ()))
    return res[0], list(res[1:])


def conv_bwd(du1, proj, conv_w, *, name, rider=None):
    S = proj.shape[0]
    KW, C = conv_w.shape
    nct = C // LANES
    rc = min(CONV_CHUNK, S)

    def body(d_ref, a_ref, g_ref, w_ref, da_ref, dg_ref, dw_ref, db_ref, pad_u, pad_d, du0, dw_acc):
        dw_acc[...] = jnp.zeros_like(dw_acc)
        pad_u[0:CONV_PAD, :] = jnp.zeros((CONV_PAD, LANES), F32)
        pad_u[CONV_PAD:, :] = a_ref[...] * _sigmoid(g_ref[...])
        pad_d[0:S, :] = d_ref[...]
        pad_d[S:, :] = jnp.zeros((CONV_PAD, LANES), F32)
        w = w_ref[...]
        db_ref[...] = jnp.sum(d_ref[...], axis=0, keepdims=True)

        def chunk(i, _):
            base = pl.multiple_of(i * rc, rc)
            d = pad_d[pl.ds(base, rc), :]
            win_u = pad_u[pl.ds(base, rc + CONV_PAD), :]
            win_d = pad_d[pl.ds(base, rc + CONV_PAD), :]
            shifts = {}
            for k in _by_residue(KW, lambda k: CONV_PAD - (KW - 1) + k):
                u_k = _rows(win_u, CONV_PAD - (KW - 1) + k, rc, shifts)
                dw_acc[k:k + 1, :] += jnp.sum(d * u_k, axis=0, keepdims=True)
            acc = jnp.zeros((rc, LANES), F32)
            shifts = {}
            for k in _by_residue(KW, lambda k: KW - 1 - k):
                acc = acc + w[k:k + 1, :] * _rows(win_d, KW - 1 - k, rc, shifts)
            du0[pl.ds(base, rc), :] = acc
            return 0

        lax.fori_loop(0, S // rc, chunk, 0)
        dw_ref[...] = dw_acc[0:KW, :]
        a, sg = a_ref[...], _sigmoid(g_ref[...])
        d0 = du0[...]
        da_ref[...] = (d0 * sg).astype(BF16)
        dg_ref[...] = (d0 * a * sg * (1.0 - sg)).astype(BF16)

    col = lambda c: (0, c)
    in_specs, out_specs, out_shape, scratch = _carry_specs(
        rider, [pl.BlockSpec((S, LANES), col), pl.BlockSpec((S, LANES), col),
                pl.BlockSpec((S, LANES), lambda c: (0, c + nct)), pl.BlockSpec((KW, LANES), col)],
        (pl.BlockSpec((S, LANES), col), pl.BlockSpec((S, LANES), col), pl.BlockSpec((KW, LANES), col),
         pl.BlockSpec((1, LANES), col)),
        (pltpu.HBM((S, C), BF16), pltpu.HBM((S, C), BF16), pltpu.HBM((KW, C), F32), pltpu.HBM((1, C), F32)),
        [pltpu.VMEM((S + CONV_PAD, LANES), F32), pltpu.VMEM((S + CONV_PAD, LANES), F32),
         pltpu.VMEM((S, LANES), F32), pltpu.VMEM((CONV_PAD, LANES), F32)])
    first = lambda: pl.program_id(0) == 0
    last = lambda: pl.program_id(0) == nct - 1
    res = _pallas(
        _carry(rider, body, 4, 4, first, last), name=name, grid=(nct,), out_shape=out_shape,
        in_specs=in_specs, out_specs=out_specs, scratch_shapes=scratch,
        compiler_params=_params(8 * _nbytes((S, LANES), F32), ("arbitrary",)))(
            du1, proj, proj, conv_w, *(rider.arrays if rider else ()))
    return res[:4], list(res[4:])


def ln_silu(u1, o_sb, gamma, beta, *, tm, name):
    S, C = u1.shape

    def body(u_ref, o_ref, g_ref, b_ref, out_ref):
        z = u_ref[...]
        mu = jnp.mean(z, axis=-1, keepdims=True)
        zc = z - mu
        y = zc * lax.rsqrt(jnp.mean(zc * zc, axis=-1, keepdims=True) + LN_EPS) * g_ref[...] + b_ref[...]
        out_ref[:, 0:C] = (y * _sigmoid(y)).astype(BF16)
        out_ref[:, C:] = o_ref[...].astype(BF16)

    row = lambda i: (i, 0)
    fix = lambda i: (0, 0)
    return _pallas(
        body, name=name, out_shape=pltpu.HBM((S, 2 * C), BF16), grid=(S // tm,),
        in_specs=[pl.BlockSpec((tm, C), row), pl.BlockSpec((tm, C), row), pl.BlockSpec((1, C), fix),
                  pl.BlockSpec((1, C), fix)],
        out_specs=pl.BlockSpec((tm, 2 * C), row),
        compiler_params=_params(4 * _nbytes((tm, C), F32), ("parallel",)))(u1, o_sb, gamma, beta)


def ln_silu_bwd(dua, u1, gamma, beta, *, tm, name):
    S, C = u1.shape

    def body(d_ref, u_ref, g_ref, b_ref, du1_ref, dg_ref, db_ref):
        @pl.when(pl.program_id(0) == 0)
        def _():
            dg_ref[...] = jnp.zeros_like(dg_ref)
            db_ref[...] = jnp.zeros_like(db_ref)

        z = u_ref[...]
        mu = jnp.mean(z, axis=-1, keepdims=True)
        zc = z - mu
        rstd = lax.rsqrt(jnp.mean(zc * zc, axis=-1, keepdims=True) + LN_EPS)
        zh = zc * rstd
        y = zh * g_ref[...] + b_ref[...]
        sg = _sigmoid(y)
        dy = d_ref[...] * (sg * (1.0 + y * (1.0 - sg)))
        dg_ref[...] += jnp.sum(dy * zh, axis=0, keepdims=True)
        db_ref[...] += jnp.sum(dy, axis=0, keepdims=True)
        dzh = dy * g_ref[...]
        m1 = jnp.mean(dzh, axis=-1, keepdims=True)
        m2 = jnp.mean(dzh * zh, axis=-1, keepdims=True)
        du1_ref[...] = rstd * (dzh - m1 - zh * m2)

    row = lambda i: (i, 0)
    fix = lambda i: (0, 0)
    return _pallas(
        body, name=name, grid=(S // tm,),
        out_shape=(pltpu.HBM((S, C), F32), pltpu.HBM((1, C), F32),
                   pltpu.HBM((1, C), F32)),
        in_specs=[pl.BlockSpec((tm, C), row), pl.BlockSpec((tm, C), row), pl.BlockSpec((1, C), fix),
                  pl.BlockSpec((1, C), fix)],
        out_specs=(pl.BlockSpec((tm, C), row), pl.BlockSpec((1, C), fix), pl.BlockSpec((1, C), fix)),
        compiler_params=_params(4 * _nbytes((tm, C), F32), ("arbitrary",)))(dua, u1, gamma, beta)


SB_BLOCK = 256
SB_STOP = -105.0
SB_GROUP = 4


def _split_dot(x, tri):
    hi = x.astype(BF16)
    lo = (x - hi.astype(F32)).astype(BF16)
    return _dot(hi, tri) + _dot(lo, tri)


def _neg_softplus(z):
    return -(jnp.maximum(z, 0.0) + jnp.log(1.0 + jnp.exp(-jnp.abs(z))))


def sb_fwd(proj, *, q_col, name, rider=None):
    S = proj.shape[0]
    dh = LANES // 2
    W = SB_HEADS * dh
    BW = SB_GROUP * dh
    ngrp = W // BW
    T = min(SB_BLOCK, S)
    nblk = S // T
    scale = dh ** -0.5
    qb0 = q_col // BW
    heads = range(SB_GROUP)
    sl = [slice(h * dh, (h + 1) * dh) for h in heads]

    def body(q_ref, k_ref, v_ref, o_ref, l_ref, qs, ks, vs):
        r_i = lax.broadcasted_iota(jnp.int32, (T, T), 0)
        c_i = lax.broadcasted_iota(jnp.int32, (T, T), 1)
        tri = (r_i >= c_i).astype(BF16)
        vis = c_i < r_i
        lane = lax.broadcasted_iota(jnp.int32, (T, dh), 1)

        qs[...] = (q_ref[...] * scale).astype(BF16)
        ks[...] = k_ref[...].astype(BF16)
        vs[...] = v_ref[...].astype(BF16)

        def step(qb, j0, diag, st):
            kb = [ks[pl.ds(j0, T), sl[h]] for h in heads]
            vb = [vs[pl.ds(j0, T), sl[h]] for h in heads]
            z = [_dot_nt(qb[h], kb[h]) for h in heads]
            lk = [_neg_softplus(z[h]) for h in heads]
            if diag:
                lk = [jnp.where(vis, lk[h], 0.0) for h in heads]
            C = [_split_dot(lk[h], tri) for h in heads]
            A = [jnp.exp(z[h] + C[h] + st[2 * h + 1]) for h in heads]
            if diag:
                A = [jnp.where(vis, A[h], 0.0) for h in heads]
            acc = [st[2 * h] + _dot(A[h].astype(BF16), vb[h]) for h in heads]
            out = ()
            for h in heads:
                out += (acc[h], st[2 * h + 1] + C[h][:, 0:1])
            return out

        def qblock(i, _):
            r0 = pl.multiple_of(i * T, T)
            qb = [qs[pl.ds(r0, T), sl[h]] for h in heads]
            zero = (jnp.zeros((T, dh), F32), jnp.zeros((T, 1), F32))
            state = step(qb, r0, True, zero * SB_GROUP)

            def more(c):
                worst = c[2]
                for h in heads[1:]:
                    worst = jnp.maximum(worst, c[2 + 2 * h])
                return (c[0] >= 0) & (jnp.max(worst) >= SB_STOP)

            def walk(c):
                return (c[0] - 1,) + step(qb, pl.multiple_of(c[0] * T, T), False, c[1:])

            c = lax.while_loop(more, walk, (i - 1,) + state)
            walked = (i - c[0]).astype(F32)
            for h in heads:
                o_ref[pl.ds(r0, T), sl[h]] = c[1 + 2 * h]
                l_ref[pl.ds(r0, T), sl[h]] = jnp.where(lane == 1, walked, c[2 + 2 * h])
            return 0

        lax.fori_loop(0, nblk, qblock, 0)

    blk = lambda off: pl.BlockSpec((S, BW), lambda g: (0, qb0 + off * ngrp + g), pipeline_mode=pl.Buffered(1))
    out = pl.BlockSpec((S, BW), lambda g: (0, g))
    in_specs, out_specs, out_shape, scratch = _carry_specs(
        rider, [blk(0), blk(1), blk(2)], (out, out), (pltpu.HBM((S, W), F32), pltpu.HBM((S, W), F32)),
        [pltpu.VMEM((S, BW), BF16)] * 3)
    first = lambda: pl.program_id(0) == 0
    last = lambda: pl.program_id(0) == ngrp - 1
    res = _pallas(
        _carry(rider, body, 3, 2, first, last), name=name, grid=(ngrp,), out_shape=out_shape,
        in_specs=in_specs, out_specs=out_specs, scratch_shapes=scratch,
        compiler_params=_params(5 * _nbytes((S, BW), F32), ("arbitrary",)))(
            proj, proj, proj, *(rider.arrays if rider else ()))
    return res[0], res[1], list(res[2:])


def sb_bwd(proj, ltot, dua, *, q_col, do_col, name, rider=None):
    S = proj.shape[0]
    dh = LANES // 2
    W = SB_HEADS * dh
    BW = SB_GROUP * dh
    ngrp = W // BW
    T = min(SB_BLOCK, S)
    nblk = S // T
    scale = dh ** -0.5
    qb0 = q_col // BW
    db0 = do_col // BW
    heads = range(SB_GROUP)
    sl = [slice(h * dh, (h + 1) * dh) for h in heads]

    def body(q_ref, k_ref, v_ref, l_ref, do_ref, dq_ref, dk_ref, dv_ref, ks, vs, dks, dvs):
        r_i = lax.broadcasted_iota(jnp.int32, (T, T), 0)
        c_i = lax.broadcasted_iota(jnp.int32, (T, T), 1)
        tri_rev = (r_i >= c_i).astype(BF16)
        tri_fwd = (r_i <= c_i).astype(BF16)
        vis = c_i < r_i

        ks[...] = k_ref[...].astype(BF16)
        vs[...] = v_ref[...].astype(BF16)
        dks[...] = jnp.zeros_like(dks)
        dvs[...] = jnp.zeros_like(dvs)

        def step(qb, dob, Lt, j0, diag, st):
            kb = [ks[pl.ds(j0, T), sl[h]] for h in heads]
            vb = [vs[pl.ds(j0, T), sl[h]] for h in heads]
            z = [_dot_nt(qb[h], kb[h]) for h in heads]
            dA = [_dot_nt(dob[h], vb[h]) for h in heads]
            lk = [_neg_softplus(z[h]) for h in heads]
            beta = [jnp.exp(z[h] + lk[h]) for h in heads]
            if diag:
                lk = [jnp.where(vis, lk[h], 0.0) for h in heads]
            C = [_split_dot(lk[h], tri_rev) for h in heads]
            rowsum = [C[h][:, 0:1] for h in heads]
            A = [jnp.exp(z[h] + C[h] + (Lt[h] - st[3 * h + 1] - rowsum[h])) for h in heads]
            if diag:
                A = [jnp.where(vis, A[h], 0.0) for h in heads]
            g = [A[h] * dA[h] for h in heads]
            Gin = [_split_dot(g[h], tri_fwd) for h in heads]
            dz = [g[h] - beta[h] * (st[3 * h + 2] + Gin[h]) for h in heads]
            if diag:
                dz = [jnp.where(vis, dz[h], 0.0) for h in heads]
            dzb = [dz[h].astype(BF16) for h in heads]
            out = ()
            for h in heads:
                dvs[pl.ds(j0, T), sl[h]] += _dot_tn(A[h].astype(BF16), dob[h])
                dks[pl.ds(j0, T), sl[h]] += _dot_tn(dzb[h], qb[h])
                out += (st[3 * h] + _dot(dzb[h], kb[h]), st[3 * h + 1] + rowsum[h],
                        st[3 * h + 2] + Gin[h][:, T - 1:T])
            return out

        def qblock(i, _):
            r0 = pl.multiple_of(i * T, T)
            qb = [(q_ref[pl.ds(r0, T), sl[h]] * scale).astype(BF16) for h in heads]
            dob = [do_ref[pl.ds(r0, T), sl[h]].astype(BF16) for h in heads]
            Lt = [l_ref[pl.ds(r0, T), h * dh:h * dh + 1] for h in heads]
            walked = jnp.clip(jnp.max(l_ref[pl.ds(r0, 8), 1:2]).astype(jnp.int32), 1, i + 1)

            def inner(j, c):
                return step(qb, dob, Lt, pl.multiple_of(j * T, T), False, c)

            zero = jnp.zeros((T, 1), F32)
            init = (jnp.zeros((T, dh), F32), zero, zero)
            c = lax.fori_loop(i + 1 - walked, i, inner, init * SB_GROUP)
            c = step(qb, dob, Lt, r0, True, c)
            for h in heads:
                dq_ref[pl.ds(r0, T), sl[h]] = (c[3 * h] * scale).astype(BF16)
            return 0

        lax.fori_loop(0, nblk, qblock, 0)
        dk_ref[...] = dks[...].astype(BF16)
        dv_ref[...] = dvs[...].astype(BF16)

    once = pl.Buffered(1)
    blk = lambda off: pl.BlockSpec((S, BW), lambda g: (0, qb0 + off * ngrp + g), pipeline_mode=once)
    out = pl.BlockSpec((S, BW), lambda g: (0, g))
    o_shape = pltpu.HBM((S, W), BF16)
    in_specs, out_specs, out_shape, scratch = _carry_specs(
        rider, [blk(0), blk(1), blk(2), pl.BlockSpec((S, BW), lambda g: (0, g), pipeline_mode=once),
                pl.BlockSpec((S, BW), lambda g: (0, db0 + g), pipeline_mode=once)], (out, out, out),
        (o_shape, o_shape, o_shape), [pltpu.VMEM((S, BW), BF16)] * 2 + [pltpu.VMEM((S, BW), F32)] * 2)
    first = lambda: pl.program_id(0) == 0
    last = lambda: pl.program_id(0) == ngrp - 1
    res = _pallas(
        _carry(rider, body, 5, 3, first, last), name=name, grid=(ngrp,), out_shape=out_shape,
        in_specs=in_specs, out_specs=out_specs, scratch_shapes=scratch,
        compiler_params=_params(6 * _nbytes((S, BW), F32), ("arbitrary",)))(
            proj, proj, proj, ltot, dua, *(rider.arrays if rider else ()))
    return res[0], res[1], res[2], list(res[3:])


def xattn_fwd(q, k, v, *, tm, name):
    S, D = q.shape
    Mlen = k.shape[0]
    hd = D // MEM_HEADS
    scale = hd ** -0.5

    def body(q_ref, k_ref, v_ref, o_ref):
        for h in range(MEM_HEADS):
            sl = slice(h * hd, (h + 1) * hd)
            s = _dot_nt(q_ref[:, sl], k_ref[:, sl]) * scale
            e = jnp.exp(s - jnp.max(s, axis=-1, keepdims=True))
            p = e / jnp.sum(e, axis=-1, keepdims=True)
            o_ref[:, sl] = _dot(p.astype(BF16), v_ref[:, sl]).astype(BF16)

    row = lambda i: (i, 0)
    fix = lambda i: (0, 0)
    return _pallas(
        body, name=name, out_shape=pltpu.HBM((S, D), BF16), grid=(S // tm,),
        in_specs=[pl.BlockSpec((tm, D), row), pl.BlockSpec((Mlen, D), fix), pl.BlockSpec((Mlen, D), fix)],
        out_specs=pl.BlockSpec((tm, D), row),
        compiler_params=_params(4 * _nbytes((tm, D), F32), ("parallel",)))(q, k, v)


def xattn_bwd(q, do, k, v, *, tm, name):
    S, D = q.shape
    Mlen = k.shape[0]
    hd = D // MEM_HEADS
    scale = hd ** -0.5

    def body(q_ref, do_ref, k_ref, v_ref, dq_ref, dk_ref, dv_ref):
        @pl.when(pl.program_id(0) == 0)
        def _():
            dk_ref[...] = jnp.zeros_like(dk_ref)
            dv_ref[...] = jnp.zeros_like(dv_ref)

        for h in range(MEM_HEADS):
            sl = slice(h * hd, (h + 1) * hd)
            qh, doh, kh, vh = q_ref[:, sl], do_ref[:, sl], k_ref[:, sl], v_ref[:, sl]
            s = _dot_nt(qh, kh) * scale
            e = jnp.exp(s - jnp.max(s, axis=-1, keepdims=True))
            p = e / jnp.sum(e, axis=-1, keepdims=True)
            dp = _dot_nt(doh, vh)
            ds = (p * (dp - jnp.sum(p * dp, axis=-1, keepdims=True)) * scale).astype(BF16)
            dq_ref[:, sl] = _dot(ds, kh).astype(BF16)
            dk_ref[:, sl] += _dot_tn(ds, qh)
            dv_ref[:, sl] += _dot_tn(p.astype(BF16), doh)

    row = lambda i: (i, 0)
    fix = lambda i: (0, 0)
    return _pallas(
        body, name=name, grid=(S // tm,),
        out_shape=(pltpu.HBM((S, D), BF16), pltpu.HBM((Mlen, D), F32),
                   pltpu.HBM((Mlen, D), F32)),
        in_specs=[pl.BlockSpec((tm, D), row), pl.BlockSpec((tm, D), row), pl.BlockSpec((Mlen, D), fix),
                  pl.BlockSpec((Mlen, D), fix)],
        out_specs=(pl.BlockSpec((tm, D), row), pl.BlockSpec((Mlen, D), fix), pl.BlockSpec((Mlen, D), fix)),
        compiler_params=_params(6 * _nbytes((tm, D), F32), ("arbitrary",)))(q, do, k, v)


FFN_HALO = 8


def _conv3(ext, w, lo):
    tm = ext.shape[0] - FFN_HALO
    return (w[0:1, :] * ext[lo:lo + tm, :] + w[1:2, :] * ext[lo + 1:lo + 1 + tm, :]
            + w[2:3, :] * ext[lo + 2:lo + 2 + tm, :])


def ffn_up_fwd(xb, w_up, conv_w, conv_b, *, tm, tn, name, rider=None):
    S, D = xb.shape
    nsh, _, ns = w_up.shape
    F = nsh * ns // 2
    per = ns // tn
    ncol = F // tn
    KW = conv_w.shape[0]
    assert KW == 3

    def body(x_ref, wv_ref, wg_ref, cwv_ref, cwg_ref, cbv_ref, cbg_ref, uv_ref, ug_ref, cv_ref, cg_ref, h_ref,
             carry):
        @pl.when(pl.program_id(1) == 0)
        def _():
            carry[...] = jnp.zeros_like(carry)

        x = x_ref[...]
        uv = _dot(x, wv_ref[...])
        ug = _dot(x, wg_ref[...])
        uv_ref[...] = uv.astype(BF16)
        ug_ref[...] = ug.astype(BF16)
        lo = FFN_HALO - (KW - 1)
        cv = _conv3(jnp.concatenate([carry[0], uv], axis=0), cwv_ref[...], lo) + cbv_ref[...]
        cg = _conv3(jnp.concatenate([carry[1], ug], axis=0), cwg_ref[...], lo) + cbg_ref[...]
        carry[0] = uv[tm - FFN_HALO:, :]
        carry[1] = ug[tm - FFN_HALO:, :]
        cv_ref[...] = cv.astype(BF16)
        cg_ref[...] = cg.astype(BF16)
        h_ref[...] = (cg * _sigmoid(cg) * cv).astype(BF16)

    wspec = lambda half: pl.BlockSpec((None, D, tn), lambda j, i: (half * (nsh // 2) + j // per, 0, j % per))
    cspec = lambda rows, half: pl.BlockSpec((rows, tn), lambda j, i: (0, half * ncol + j))
    out = pl.BlockSpec((tm, tn), lambda j, i: (i, j))
    o_shape = pltpu.HBM((S, F), BF16)
    blk = _nbytes((tm, D), BF16) + 2 * _nbytes((D, tn), BF16) + 8 * _nbytes((tm, tn), F32)
    nrow = S // tm
    in_specs, out_specs, out_shape, scratch = _carry_specs(
        rider, [pl.BlockSpec((tm, D), lambda j, i: (i, 0)), wspec(0), wspec(1), cspec(KW, 0), cspec(KW, 1),
                cspec(1, 0), cspec(1, 1)], (out,) * 5, (o_shape,) * 5, [pltpu.VMEM((2, FFN_HALO, tn), F32)])
    first = lambda: (pl.program_id(0) == 0) & (pl.program_id(1) == 0)
    last = lambda: (pl.program_id(0) == ncol - 1) & (pl.program_id(1) == nrow - 1)
    res = _pallas(
        _carry(rider, body, 7, 5, first, last), name=name, grid=(ncol, nrow), out_shape=out_shape,
        in_specs=in_specs, out_specs=out_specs, scratch_shapes=scratch,
        compiler_params=_params(blk, ("arbitrary", "arbitrary")))(
            xb, w_up, w_up, conv_w, conv_w, conv_b, conv_b, *(rider.arrays if rider else ()))
    return res[:5], list(res[5:])


def ffn_mid_bwd(dzb, w_down, up_v, up_g, conv_v, conv_g, conv_w, *, tm, tn, name, rider=None):
    S, D = dzb.shape
    F = up_v.shape[1]
    ncol = F // tn
    nrow = S // tm
    KW = conv_w.shape[0]
    assert KW == 3

    def body(dz_ref, wd_ref, uv_ref, ug_ref, cv_ref, cg_ref, cwv_ref, cwg_ref,
             dv_ref, dg_ref, dwv_ref, dwg_ref, dbv_ref, dbg_ref, carry):
        @pl.when(pl.program_id(1) == 0)
        def _():
            carry[...] = jnp.zeros_like(carry)
            for r in (dwv_ref, dwg_ref, dbv_ref, dbg_ref):
                r[...] = jnp.zeros_like(r)

        cv, cg = cv_ref[...].astype(F32), cg_ref[...].astype(F32)
        dh = _dot_nt(dz_ref[...], wd_ref[...])
        sg = _sigmoid(cg)
        dcv = dh * (cg * sg)
        dcg = dh * cv * (sg * (1.0 + cg * (1.0 - sg)))

        def back(dc, u_ref, cw, slot, du_ref, dw_ref, db_ref):
            ext = jnp.concatenate([dc, carry[slot]], axis=0)
            ahead = [dc, ext[1:tm + 1, :], ext[2:tm + 2, :]]
            du = cw[2:3, :] * ahead[0] + cw[1:2, :] * ahead[1] + cw[0:1, :] * ahead[2]
            du_ref[...] = du.astype(BF16)
            carry[slot] = dc[0:FFN_HALO, :]
            u = u_ref[...].astype(F32)
            for k in range(KW):
                dw_ref[k:k + 1, :] += jnp.sum(ahead[KW - 1 - k] * u, axis=0, keepdims=True)
            db_ref[...] += jnp.sum(dc, axis=0, keepdims=True)

        back(dcv, uv_ref, cwv_ref[...], 0, dv_ref, dwv_ref, dbv_ref)
        back(dcg, ug_ref, cwg_ref[...], 1, dg_ref, dwg_ref, dbg_ref)

    rev = lambda i: nrow - 1 - i
    tile = pl.BlockSpec((tm, tn), lambda j, i: (rev(i), j))
    cspec = lambda half: pl.BlockSpec((KW, tn), lambda j, i: (0, half * ncol + j))
    acc = lambda rows: pl.BlockSpec((rows, tn), lambda j, i: (0, j))
    big = pltpu.HBM((S, F), BF16)
    blk = _nbytes((tm, D), BF16) + _nbytes((tn, D), BF16) + 10 * _nbytes((tm, tn), F32)
    in_specs, out_specs, out_shape, scratch = _carry_specs(
        rider, [pl.BlockSpec((tm, D), lambda j, i: (rev(i), 0)), pl.BlockSpec((tn, D), lambda j, i: (j, 0)),
                tile, tile, tile, tile, cspec(0), cspec(1)],
        (tile, tile, acc(KW), acc(KW), acc(1), acc(1)),
        (big, big, pltpu.HBM((KW, F), F32), pltpu.HBM((KW, F), F32), pltpu.HBM((1, F), F32),
         pltpu.HBM((1, F), F32)), [pltpu.VMEM((2, FFN_HALO, tn), F32)])
    first = lambda: (pl.program_id(0) == 0) & (pl.program_id(1) == 0)
    last = lambda: (pl.program_id(0) == ncol - 1) & (pl.program_id(1) == nrow - 1)
    res = _pallas(
        _carry(rider, body, 8, 6, first, last), name=name, grid=(ncol, nrow), out_shape=out_shape,
        in_specs=in_specs, out_specs=out_specs, scratch_shapes=scratch,
        compiler_params=_params(blk, ("arbitrary", "arbitrary")))(
            dzb, w_down, up_v, up_g, conv_v, conv_g, conv_w, conv_w, *(rider.arrays if rider else ()))
    return res[:6], list(res[6:])


def loss_head(y, target, *, tm, name):
    S, D = y.shape

    def body(y_ref, t_ref, dy_ref, l_ref):
        @pl.when(pl.program_id(0) == 0)
        def _():
            l_ref[...] = jnp.zeros_like(l_ref)

        e = y_ref[...] - t_ref[...]
        dy_ref[...] = e * (1.0 / D)
        l_ref[...] += 0.5 * jnp.sum(jnp.mean(e * e, axis=-1, keepdims=True), axis=0, keepdims=True)

    row = lambda i: (i, 0)
    return _pallas(
        body, name=name, grid=(S // tm,),
        out_shape=(pltpu.HBM((S, D), F32), pltpu.HBM((1, 1), F32)),
        in_specs=[pl.BlockSpec((tm, D), row), pl.BlockSpec((tm, D), row)],
        out_specs=(pl.BlockSpec((tm, D), row), pl.BlockSpec((1, 1), lambda i: (0, 0))),
        compiler_params=_params(3 * _nbytes((tm, D), F32), ("arbitrary",)))(y, target)


def adamw(w, g, m, v, *, tr, name):
    R, C = w.shape
    c1 = 1.0 - ADAM_B1 ** ADAM_STEP
    c2 = 1.0 - ADAM_B2 ** ADAM_STEP

    def body(w_ref, g_ref, m_ref, v_ref, go_ref, d_ref, mo_ref, vo_ref):
        gv = g_ref[...]
        mn = ADAM_B1 * m_ref[...] + (1.0 - ADAM_B1) * gv
        vn = ADAM_B2 * v_ref[...] + (1.0 - ADAM_B2) * (gv * gv)
        go_ref[...] = gv
        mo_ref[...] = mn
        vo_ref[...] = vn
        d_ref[...] = -ADAM_LR * ((mn / c1) / (jnp.sqrt(vn / c2) + ADAM_EPS) + ADAM_WD * w_ref[...])

    spec = pl.BlockSpec((tr, C), lambda i: (i, 0))
    shape = pltpu.HBM((R, C), F32)
    return _pallas(
        body, name=name, grid=(R // tr,), out_shape=(shape,) * 4, in_specs=[spec] * 4, out_specs=(spec,) * 4,
        compiler_params=_params(8 * _nbytes((tr, C), F32), ("parallel",)))(w, g, m, v)


def add_pairs(gs, gots, core, *, name):
    k = len(gs)

    def body(c_ref, *refs):
        for a_ref, b_ref, o_ref in zip(refs[:k], refs[k:2 * k], refs[2 * k:]):
            o_ref[...] = (a_ref[...].astype(F32) + b_ref[...].astype(F32)).astype(BF16)

    own = [pl.BlockSpec((None, None) + g.shape[2:], lambda i, c: (i, c[0], 0, 0)) for g in gs]
    half = [pl.BlockSpec((None,) + g.shape[1:], lambda i, c: (i, 0, 0)) for g in gots]
    grid_spec = pltpu.PrefetchScalarGridSpec(
        num_scalar_prefetch=1, grid=(N_CHIPS,), in_specs=own + half, out_specs=tuple(half))
    blk = 3 * sum(_nbytes(g.shape[1:], BF16) for g in gots)
    return _pallas(
        body, name=name, grid_spec=grid_spec, out_shape=tuple(pltpu.HBM(g.shape, BF16) for g in gots),
        compiler_params=_params(blk, ("parallel",)))(core, *gs, *gots)


def sum_chips_into(bs, dests, layer, core, *, name):
    k = len(bs)
    steps = 2

    def body(c_ref, *refs):
        for b_ref, o_ref in zip(refs[:k], refs[2 * k:]):
            acc = b_ref[0].astype(F32)
            for p in range(1, N_CHIPS):
                acc = acc + b_ref[p].astype(F32)
            o_ref[...] = acc

    ins = [pl.BlockSpec((N_CHIPS, b.shape[1] // steps, b.shape[2]), lambda i, c: (0, i, 0)) for b in bs]
    outs = tuple(pl.BlockSpec((None, None, b.shape[1] // steps, b.shape[2]), lambda i, c: (layer, c[0], i, 0))
                 for b in bs)
    grid_spec = pltpu.PrefetchScalarGridSpec(
        num_scalar_prefetch=1, grid=(steps,), in_specs=ins + [pl.BlockSpec(memory_space=pl.ANY)] * k,
        out_specs=outs)
    blk = sum(_nbytes(b.shape, BF16) + _nbytes(b.shape[1:], F32) for b in bs) // steps
    return _pallas(
        body, name=name, grid_spec=grid_spec, out_shape=tuple(pltpu.HBM(d.shape, F32) for d in dests),
        input_output_aliases={1 + k + w: w for w in range(k)},
        compiler_params=_params(blk, ("parallel",)))(core, *bs, *dests)


_HBM = pl.BlockSpec(memory_space=pltpu.HBM)


def _place():
    x, y, c = lax.axis_index("x"), lax.axis_index("y"), lax.axis_index("c")
    chips = [(1 - x, y), (x, 1 - y), (1 - x, 1 - y)]
    return x, y, c, chips


class GatherRider:
    def __init__(self, shards):
        self.arrays = list(shards)
        self.n = n = len(shards)
        self.out_shape = tuple(pltpu.HBM((N_CHIPS,) + s.shape, s.dtype) for s in shards)
        self.scratch = [pltpu.SemaphoreType.DMA((n, 3))] * 4 + [pltpu.SemaphoreType.DMA((n,))]

    def _copies(self, ins, outs, sems):
        send_ici, recv_ici, send_d2d, recv_d2d, local = sems
        x, y, c, chips = _place()
        me = 2 * x + y

        def own(w):
            return pltpu.make_async_copy(ins[w], outs[w].at[me], local.at[w])

        def ici(w, j):
            px, py = chips[j]
            return pltpu.make_async_remote_copy(
                src_ref=ins[w].at[c], dst_ref=outs[w].at[me, c], send_sem=send_ici.at[w, j],
                recv_sem=recv_ici.at[w, j], device_id=(px, py, c), device_id_type=MESH)

        def landed(w, j, half):
            px, py = chips[j]
            return outs[w].at[2 * px + py, half]

        def d2d(w, j, half):
            return pltpu.make_async_remote_copy(
                src_ref=landed(w, j, half), dst_ref=landed(w, j, half), send_sem=send_d2d.at[w, j],
                recv_sem=recv_d2d.at[w, j], device_id=(x, y, 1 - c), device_id_type=MESH)

        def ici_arrival(w, j):
            return pltpu.make_async_remote_copy(
                src_ref=landed(w, j, c), dst_ref=landed(w, j, c), send_sem=send_ici.at[w, j],
                recv_sem=recv_ici.at[w, j], device_id=(x, y, c), device_id_type=MESH)

        return c, own, ici, d2d, ici_arrival

    def start(self, ins, outs, sems):
        c, own, ici, d2d, ici_arrival = self._copies(ins, outs, sems)
        for w in range(self.n):
            own(w).start()
            for j in range(3):
                ici(w, j).start()

    def finish(self, ins, outs, sems):
        c, own, ici, d2d, ici_arrival = self._copies(ins, outs, sems)
        for w in range(self.n):
            for j in range(3):
                ici_arrival(w, j).wait_recv()
                d2d(w, j, c).start()
        for w in range(self.n):
            for j in range(3):
                d2d(w, j, 1 - c).wait_recv()
        for w in range(self.n):
            for j in range(3):
                ici(w, j).wait_send()
                d2d(w, j, c).wait_send()
            own(w).wait()


class ScatterRider:
    def __init__(self, parts):
        self.arrays = list(parts)
        self.n = n = len(parts)
        self.out_shape = tuple(pltpu.HBM(p.shape, p.dtype) for p in parts)
        self.scratch = [pltpu.SemaphoreType.DMA((n, 3))] * 2 + [pltpu.SemaphoreType.DMA((n,))]

    def _copies(self, ins, outs, sems):
        send, recv, local = sems
        x, y, c, chips = _place()
        me = 2 * x + y

        def own(w):
            return pltpu.make_async_copy(ins[w].at[me], outs[w].at[me], local.at[w])

        def copy(w, j):
            px, py = chips[j]
            return pltpu.make_async_remote_copy(
                src_ref=ins[w].at[2 * px + py], dst_ref=outs[w].at[me], send_sem=send.at[w, j],
                recv_sem=recv.at[w, j], device_id=(px, py, c), device_id_type=MESH)

        def arrival(w, j):
            px, py = chips[j]
            blk = outs[w].at[2 * px + py]
            return pltpu.make_async_remote_copy(
                src_ref=blk, dst_ref=blk, send_sem=send.at[w, j], recv_sem=recv.at[w, j],
                device_id=(x, y, c), device_id_type=MESH)

        return own, copy, arrival

    def start(self, ins, outs, sems):
        own, copy, arrival = self._copies(ins, outs, sems)
        for w in range(self.n):
            own(w).start()
            for j in range(3):
                copy(w, j).start()

    def finish(self, ins, outs, sems):
        own, copy, arrival = self._copies(ins, outs, sems)
        for w in range(self.n):
            for j in range(3):
                arrival(w, j).wait_recv()
        for w in range(self.n):
            for j in range(3):
                copy(w, j).wait_send()
            own(w).wait()


def _carry(rider, body, n_in, n_out, first, last):
    if rider is None:
        return body
    k, m = rider.n, len(rider.scratch)

    def carried(*refs):
        ins, r_in = refs[:n_in], refs[n_in:n_in + k]
        outs, r_out = refs[n_in + k:n_in + k + n_out], refs[n_in + k + n_out:n_in + 2 * k + n_out]
        rest = refs[n_in + 2 * k + n_out:]
        scratch, sems = rest[:len(rest) - m], rest[len(rest) - m:]

        @pl.when(first())
        def _():
            rider.start(r_in, r_out, sems)

        body(*ins, *outs, *scratch)

        @pl.when(last())
        def _():
            rider.finish(r_in, r_out, sems)

    return carried


def _carry_specs(rider, in_specs, out_specs, out_shape, scratch):
    if rider is None:
        return list(in_specs), tuple(out_specs), tuple(out_shape), list(scratch)
    k = rider.n
    return (list(in_specs) + [_HBM] * k, tuple(out_specs) + (_HBM,) * k, tuple(out_shape) + rider.out_shape,
            list(scratch) + list(rider.scratch))


def run_rider(rider, *, name):
    k = rider.n

    def body(*refs):
        rider.start(refs[:k], refs[k:2 * k], refs[2 * k:])
        rider.finish(refs[:k], refs[k:2 * k], refs[2 * k:])

    return _pallas(body, name=name, out_shape=rider.out_shape, in_specs=[_HBM] * k, out_specs=(_HBM,) * k,
                   scratch_shapes=rider.scratch)(*rider.arrays)


def allgather_small(shards, *, name):
    n = len(shards)

    def body(*refs):
        ins, outs = refs[:n], refs[n:2 * n]
        send, recv, local = refs[2 * n:]
        x, y, c, chips = _place()
        me = 2 * x + y
        locals_ = [pltpu.make_async_copy(ins[w], outs[w].at[me], local.at[w]) for w in range(n)]
        for cp in locals_:
            cp.start()

        def copy(w, j):
            px, py = chips[j]
            return pltpu.make_async_remote_copy(
                src_ref=ins[w], dst_ref=outs[w].at[me], send_sem=send.at[w, j], recv_sem=recv.at[w, j],
                device_id=(px, py, c), device_id_type=MESH)

        def arrival(w, j):
            px, py = chips[j]
            blk = outs[w].at[2 * px + py]
            return pltpu.make_async_remote_copy(
                src_ref=blk, dst_ref=blk, send_sem=send.at[w, j], recv_sem=recv.at[w, j],
                device_id=(x, y, c), device_id_type=MESH)

        for w in range(n):
            for j in range(3):
                copy(w, j).start()
        for w in range(n):
            for j in range(3):
                arrival(w, j).wait_recv()
        for w in range(n):
            for j in range(3):
                copy(w, j).wait_send()
        for cp in locals_:
            cp.wait()

    out_shape = tuple(pltpu.HBM((N_CHIPS,) + s.shape, s.dtype) for s in shards)
    return _pallas(
        body, name=name, out_shape=out_shape, in_specs=[_HBM] * n, out_specs=(_HBM,) * n,
        scratch_shapes=[pltpu.SemaphoreType.DMA((n, 3))] * 2 + [pltpu.SemaphoreType.DMA((n,))],
    )(*shards)


class SwapRider:
    def __init__(self, grads):
        self.arrays = list(grads)
        self.n = n = len(grads)
        self.out_shape = tuple(pltpu.HBM((N_CHIPS,) + g.shape[2:], g.dtype) for g in grads)
        self.scratch = [pltpu.SemaphoreType.DMA((n,))] * 2

    def _copies(self, ins, outs, sems):
        send, recv = sems
        x, y, c, _ = _place()
        return [pltpu.make_async_remote_copy(
            src_ref=ins[w].at[:, 1 - c], dst_ref=outs[w], send_sem=send.at[w], recv_sem=recv.at[w],
            device_id=(x, y, 1 - c), device_id_type=MESH) for w in range(self.n)]

    def start(self, ins, outs, sems):
        for cp in self._copies(ins, outs, sems):
            cp.start()

    def finish(self, ins, outs, sems):
        copies = self._copies(ins, outs, sems)
        for cp in copies:
            cp.wait_recv()
        for cp in copies:
            cp.wait_send()


def rs_sibling_share(stacked, *, name):
    n = len(stacked)

    def body(*refs):
        bufs = refs[n:2 * n]
        send, recv = refs[2 * n:]
        x, y, c, _ = _place()
        shares, arrivals = [], []
        for w in range(n):
            mine, other = bufs[w].at[:, c], bufs[w].at[:, 1 - c]
            shares.append(pltpu.make_async_remote_copy(
                src_ref=mine, dst_ref=mine, send_sem=send.at[w], recv_sem=recv.at[w],
                device_id=(x, y, 1 - c), device_id_type=MESH))
            arrivals.append(pltpu.make_async_remote_copy(
                src_ref=other, dst_ref=other, send_sem=send.at[w], recv_sem=recv.at[w],
                device_id=(x, y, c), device_id_type=MESH))
        for cp in shares:
            cp.start()
        for cp in arrivals:
            cp.wait_recv()
        for cp in shares:
            cp.wait_send()

    out_shape = tuple(pltpu.HBM(s.shape, F32) for s in stacked)
    return _pallas(
        body, name=name, out_shape=out_shape, in_specs=[_HBM] * n, out_specs=(_HBM,) * n,
        input_output_aliases={w: w for w in range(n)},
        scratch_shapes=[pltpu.SemaphoreType.DMA((n,))] * 2,
    )(*stacked)


def allreduce_small(v, *, name):
    R, C = v.shape

    def body(v_ref, o_ref, land, send, recv):
        x, y, c, _ = _place()
        me = 4 * x + 2 * y + c
        land[me] = v_ref[...]

        def flip(k):
            return (1 - x) if k & 4 else x, (1 - y) if k & 2 else y, (1 - c) if k & 1 else c

        copies = []
        for k in range(1, N_DEV):
            px, py, pc = flip(k)
            copies.append(pltpu.make_async_remote_copy(
                src_ref=v_ref, dst_ref=land.at[me], send_sem=send.at[k - 1], recv_sem=recv.at[k - 1],
                device_id=(px, py, pc), device_id_type=MESH))
        for cp in copies:
            cp.start()
        for k in range(1, N_DEV):
            px, py, pc = flip(k)
            blk = land.at[4 * px + 2 * py + pc]
            pltpu.make_async_remote_copy(
                src_ref=blk, dst_ref=blk, send_sem=send.at[k - 1], recv_sem=recv.at[k - 1],
                device_id=(x, y, c), device_id_type=MESH).wait_recv()
        for cp in copies:
            cp.wait_send()
        acc = land[0]
        for d in range(1, N_DEV):
            acc = acc + land[d]
        o_ref[...] = acc

    vm = pl.BlockSpec(memory_space=pltpu.VMEM)
    return pl.pallas_call(
        body, name=name, out_shape=jax.ShapeDtypeStruct((R, C), F32), in_specs=[vm], out_specs=vm,
        scratch_shapes=[pltpu.VMEM((N_DEV, R, C), F32), pltpu.SemaphoreType.DMA((N_DEV - 1,)),
                        pltpu.SemaphoreType.DMA((N_DEV - 1,))],
        compiler_params=pltpu.CompilerParams(vmem_limit_bytes=int(min(12 * R * C * 4 + (8 << 20), VMEM_CAP))),
    )(v)


def _pack(arrays):
    flat = jnp.concatenate([a.reshape(-1) for a in arrays])
    return flat.reshape(-1, LANES)


def _unpack(packed, shapes):
    flat = packed.reshape(-1)
    out, off = [], 0
    for s in shapes:
        n = 1
        for d in s:
            n *= d
        out.append(flat[off:off + n].reshape(s))
        off += n
    return out


def _row_tile(rows, cap=512):
    t = 1 << (cap.bit_length() - 1)
    while rows % t:
        t //= 2
    return t


def _adamw_tile(rows, cols):
    return _row_tile(rows, max(8, (1 << 20) // (4 * cols)))


def kernel(x, mem, w_in, conv_w, conv_b, conv_ln_g, conv_ln_b, w_out, ln1_g, ln1_b, mem_wq, mem_wk, mem_wv, mem_wo, ln2_g, ln2_b, ffn_up, ffn_conv_w, ffn_conv_b, ffn_down, ln3_g, ln3_b, loss_target, m_w_in, m_conv_w, m_conv_b, m_conv_ln_g, m_conv_ln_b, m_w_out, m_ln1_g, m_ln1_b, m_mem_wq, m_mem_wk, m_mem_wv, m_mem_wo, m_ln2_g, m_ln2_b, m_ffn_up, m_ffn_conv_w, m_ffn_conv_b, m_ffn_down, m_ln3_g, m_ln3_b, v_w_in, v_conv_w, v_conv_b, v_conv_ln_g, v_conv_ln_b, v_w_out, v_ln1_g, v_ln1_b, v_mem_wq, v_mem_wk, v_mem_wv, v_mem_wo, v_ln2_g, v_ln2_b, v_ffn_up, v_ffn_conv_w, v_ffn_conv_b, v_ffn_down, v_ln3_g, v_ln3_b):
    W = dict(w_in=w_in, conv_w=conv_w, conv_b=conv_b, conv_ln_g=conv_ln_g, conv_ln_b=conv_ln_b, w_out=w_out,
             ln1_g=ln1_g, ln1_b=ln1_b, mem_wq=mem_wq, mem_wk=mem_wk, mem_wv=mem_wv, mem_wo=mem_wo, ln2_g=ln2_g,
             ln2_b=ln2_b, ffn_up=ffn_up, ffn_conv_w=ffn_conv_w, ffn_conv_b=ffn_conv_b, ffn_down=ffn_down,
             ln3_g=ln3_g, ln3_b=ln3_b)
    M1 = dict(w_in=m_w_in, conv_w=m_conv_w, conv_b=m_conv_b, conv_ln_g=m_conv_ln_g, conv_ln_b=m_conv_ln_b,
              w_out=m_w_out, ln1_g=m_ln1_g, ln1_b=m_ln1_b, mem_wq=m_mem_wq, mem_wk=m_mem_wk, mem_wv=m_mem_wv,
              mem_wo=m_mem_wo, ln2_g=m_ln2_g, ln2_b=m_ln2_b, ffn_up=m_ffn_up, ffn_conv_w=m_ffn_conv_w,
              ffn_conv_b=m_ffn_conv_b, ffn_down=m_ffn_down, ln3_g=m_ln3_g, ln3_b=m_ln3_b)
    V2 = dict(w_in=v_w_in, conv_w=v_conv_w, conv_b=v_conv_b, conv_ln_g=v_conv_ln_g, conv_ln_b=v_conv_ln_b,
              w_out=v_w_out, ln1_g=v_ln1_g, ln1_b=v_ln1_b, mem_wq=v_mem_wq, mem_wk=v_mem_wk, mem_wv=v_mem_wv,
              mem_wo=v_mem_wo, ln2_g=v_ln2_g, ln2_b=v_ln2_b, ffn_up=v_ffn_up, ffn_conv_w=v_ffn_conv_w,
              ffn_conv_b=v_ffn_conv_b, ffn_down=v_ffn_down, ln3_g=v_ln3_g, ln3_b=v_ln3_b)

    L = w_in.shape[0]
    S, D = x.shape[1], x.shape[2]
    C = conv_b.shape[1]
    alpha = (2.0 * L) ** 0.25
    chip = 2 * lax.axis_index("x") + lax.axis_index("y")
    xs, mems, tgt = x[0], mem[0], loss_target[0]
    mem_bf = mems.astype(BF16)
    tm = _row_tile(S)
    tm_ffn = _row_tile(S, 256)
    tm_big = _row_tile(S, 1024)

    def shards_of(l, names):
        out = []
        for n in names:
            wl = W[n][l].astype(BF16)
            out.append(wl.reshape(2, wl.shape[0] // 2, wl.shape[1]))
        return out

    def gathered(names, got):
        layer = {}
        for n, g in zip(names, got):
            rows, cols = W[n].shape[1], W[n].shape[2]
            layer[n] = g.reshape(N_CHIPS, rows, cols) if n in COL_SHARDED else g.reshape(N_CHIPS * rows, cols)
        return layer

    full = [dict() for _ in range(L)]
    full[0].update(gathered(RIDE_IN, run_rider(GatherRider(shards_of(0, RIDE_IN)), name="allgather_w_in")))
    cw_all, fcw_all = allgather_small([conv_w, ffn_conv_w], name="allgather_small")
    cw_full = jnp.transpose(cw_all, (1, 2, 0, 3)).reshape(L, conv_w.shape[1], -1)
    fcw_full = jnp.transpose(fcw_all, (1, 2, 0, 3)).reshape(L, ffn_conv_w.shape[1], -1)

    saved = []
    h, hb = xs, xs.astype(BF16)
    for l in range(L):
        fw = full[l]
        s = dict(x=h, xb=hb)
        s['proj'] = mm_nn(hb, fw['w_in'], F32, tm=min(1024, S), tn=fw['w_in'].shape[2], name="proj")
        on_conv = RIDE_ATT if l == 0 else RIDE_FFN[1:]
        on_sb = RIDE_FFN if l == 0 else RIDE_FFN[:1]
        s['u1'], got = conv_fwd(s['proj'], cw_full[l], conv_b[l][None], name="conv_fwd",
                                rider=GatherRider(shards_of(l, on_conv)))
        fw.update(gathered(on_conv, got))
        more = l + 1 < L
        s['o_sb'], s['ltot'], got = sb_fwd(
            s['proj'], q_col=2 * C, name="sb_fwd", rider=GatherRider(shards_of(l, on_sb)))
        fw.update(gathered(on_sb, got))
        s['ua'] = ln_silu(s['u1'], s['o_sb'], conv_ln_g[l][None], conv_ln_b[l][None], tm=tm, name="ln_silu")
        s['x1'], s['x1b'], s['zh1'], s['rs1'] = mm_ln(
            s['ua'], fw['w_out'], h, ln1_g[l][None], ln1_b[l][None], alpha, tm=tm, name="out_proj_ln")
        s['q2'] = mm_nn(s['x1b'], fw['mem_wq'], BF16, tm=min(1024, S), tn=512, name="mem_q")
        s['k2'] = mm_nn(mem_bf, fw['mem_wk'], BF16, tm=mem_bf.shape[0], tn=512, name="mem_kv")
        s['v2'] = mm_nn(mem_bf, fw['mem_wv'], BF16, tm=mem_bf.shape[0], tn=512, name="mem_kv")
        s['o2'] = xattn_fwd(s['q2'], s['k2'], s['v2'], tm=tm, name="xattn_fwd")
        s['x2'], s['x2b'], s['zh2'], s['rs2'] = mm_ln(
            s['o2'], fw['mem_wo'], s['x1'], ln2_g[l][None], ln2_b[l][None], alpha, tm=tm, name="mem_o_ln")
        (s['upv'], s['upg'], s['cv'], s['cg'], s['hmid']), got = ffn_up_fwd(
            s['x2b'], fw['ffn_up'], fcw_full[l], ffn_conv_b[l][None], tm=tm_ffn, tn=fw['ffn_up'].shape[2],
            name="ffn_up_fwd", rider=GatherRider(shards_of(l + 1, RIDE_ATT + RIDE_IN)) if more else None)
        if more:
            full[l + 1].update(gathered(RIDE_ATT + RIDE_IN, got))
        h, hb, s['zh3'], s['rs3'] = mm_ln(
            s['hmid'], fw['ffn_down'], s['x2'], ln3_g[l][None], ln3_b[l][None], alpha, tm=tm, name="ffn_down_ln")
        saved.append(s)

    dx, loss_part = loss_head(h, tgt, tm=tm, name="loss_head")
    loss = lax.psum(loss_part[0, 0], ("x", "y", "c"))

    core = lax.axis_index("c").astype(jnp.int32).reshape(1)
    reduced_big = {n: lax.empty((L, 2, W[n].shape[1] // 2, W[n].shape[2]), F32) for n in BIG}
    small_grads = [None] * L

    def row_halves(g, names):
        parts = []
        for n in names:
            rows, cols = W[n].shape[1], W[n].shape[2]
            parts.append(g[n].reshape(N_CHIPS, 2, rows // 2, cols))
        return parts

    def pre_add(g, names):
        parts = row_halves(g, names)
        got = run_rider(SwapRider(parts), name="rs_sibling_swap")
        return list(add_pairs(parts, got, core, name="rs_add_pairs"))

    def reduce_into(names, scattered, layer):
        reduced_big.update(zip(names, sum_chips_into(
            list(scattered), [reduced_big[n] for n in names], layer, core, name="rs_sum_chips")))

    pending = None
    for l in reversed(range(L)):
        fw, s = full[l], saved[l]
        g = {}
        if l == L - 1:
            top = ln_bwd(dx, s['zh3'], s['rs3'], ln3_g[l][None], tm=tm, name="ln_bwd")
        dz3, dz3b, g['ln3_g'], g['ln3_b'] = top
        ftn = fw['ffn_up'].shape[2]
        (dupv, dupg, dfw_v, dfw_g, dfb_v, dfb_g), sc = ffn_mid_bwd(
            dz3b, fw['ffn_down'], s['upv'], s['upg'], s['cv'], s['cg'], fcw_full[l], tm=tm_ffn, tn=ftn,
            name="ffn_mid_bwd", rider=ScatterRider(pending) if pending else None)
        if pending:
            reduce_into(RIDE_MIX, sc, l + 1)
        g['ffn_conv_w'] = jnp.concatenate([dfw_v, dfw_g], axis=1)
        g['ffn_conv_b'] = jnp.concatenate([dfb_v, dfb_g], axis=1)[0]
        g['ffn_down'] = mm_tn(s['hmid'], [dz3b], tk=ftn, tn=512, tmc=min(1024, S), name="grad_ffn_down")
        dz2, dz2b, g['ln2_g'], g['ln2_b'] = mm_nt_ln_bwd(
            [dupv, dupg], fw['ffn_up'], dz3, alpha, s['zh2'], s['rs2'], ln2_g[l][None], tm=tm_ffn,
            name="ffn_up_bwd")
        g['ffn_up'] = mm_tn(s['x2b'], [dupv, dupg], tk=512, tn=ftn, shard_width=ftn, tmc=min(1024, S),
                            name="grad_ffn_up")

        do2 = mm_nt([dz2b], fw['mem_wo'], BF16, tm=tm_big, tk=512, name="mem_o_bwd")
        g['mem_wo'] = mm_tn(s['o2'], [dz2b], tk=512, tn=512, name="grad_sq")
        dq2, dk2, dv2 = xattn_bwd(s['q2'], do2, s['k2'], s['v2'], tm=tm, name="xattn_bwd")
        dz1, dz1b, g['ln1_g'], g['ln1_b'] = mm_nt_ln_bwd(
            [dq2], fw['mem_wq'], dz2, alpha, s['zh1'], s['rs1'], ln1_g[l][None], tm=tm, name="mem_q_bwd")
        g['mem_wq'] = mm_tn(s['x1b'], [dq2], tk=512, tn=512, name="grad_sq")
        g['mem_wk'] = mm_tn(mem_bf, [dk2], tk=512, tn=512, name="grad_mem_kv")
        g['mem_wv'] = mm_tn(mem_bf, [dv2], tk=512, tn=512, name="grad_mem_kv")

        rest = row_halves(g, RIDE_REST)
        dua, got = mm_nt([dz1b], fw['w_out'], F32, tm=tm_big, tk=512, name="out_proj_bwd", rider=SwapRider(rest))
        rest = list(add_pairs(rest, got, core, name="rs_add_pairs"))
        g['w_out'] = mm_tn(s['ua'], [dz1b], tk=512, tn=512, name="grad_sq")
        dq, dk, dv, sc = sb_bwd(
            s['proj'], s['ltot'], dua, q_col=2 * C, do_col=C, name="sb_bwd",
            rider=ScatterRider(rest[:-1]))
        reduce_into(RIDE_REST[:-1], sc, l)
        du1, g['conv_ln_g'], g['conv_ln_b'] = ln_silu_bwd(
            dua, s['u1'], conv_ln_g[l][None], conv_ln_b[l][None], tm=tm, name="ln_silu_bwd")
        (da, dg, g['conv_w'], dcb), sc = conv_bwd(du1, s['proj'], cw_full[l], name="conv_bwd",
                                                  rider=ScatterRider(rest[-1:]))
        reduce_into(RIDE_REST[-1:], sc, l)
        g['conv_b'] = dcb
        dproj = jnp.concatenate([da, dg, dq, dk, dv], axis=1)
        ns_in = fw['w_in'].shape[2]
        if l > 0:
            below = saved[l - 1]
            top = mm_nt_ln_bwd([dproj], fw['w_in'], dz1, alpha, below['zh3'], below['rs3'], ln3_g[l - 1][None],
                               tm=tm, name="proj_bwd")
        else:
            dx = mm_nt([dproj], fw['w_in'], F32, tm=tm_big, tk=512, res=dz1, alpha=alpha, name="proj_bwd_x")
        g['w_in'] = mm_tn(s['xb'], [dproj], tk=512, tn=ns_in, shard_width=ns_in, name="grad_w_in")

        pending = pre_add(g, RIDE_MIX)
        small_grads[l] = {n: g[n].reshape(W[n].shape[1:-1] + (-1,)) for n in SMALL}

    grad_x = dx[None]

    reduce_into(RIDE_MIX, run_rider(ScatterRider(pending), name="rs_chip_scatter"), 0)
    shared = rs_sibling_share([reduced_big[n] for n in BIG], name="rs_sibling_share")
    G = {}
    for n, sh in zip(BIG, shared):
        G[n] = sh.reshape(W[n].shape)

    small_full_shapes = []
    small_stack = []
    for n in SMALL:
        st = jnp.stack([small_grads[l][n] for l in range(L)])
        small_stack.append(st)
        small_full_shapes.append(st.shape)
    reduced = _unpack(allreduce_small(_pack(small_stack), name="allreduce_small"), small_full_shapes)
    for n, r in zip(SMALL, reduced):
        if n in SMALL_SHARDED:
            width = W[n].shape[-1]
            r = lax.dynamic_slice_in_dim(r, chip * width, width, axis=2)
        G[n] = r

    out_g, out_d, out_m, out_v = {}, {}, {}, {}
    for n in BIG:
        shp = W[n].shape
        flat = lambda a: a.reshape(shp[0] * shp[1], shp[2])
        res = adamw(flat(W[n]), flat(G[n]), flat(M1[n]), flat(V2[n]), tr=_adamw_tile(shp[0] * shp[1], shp[2]), name="adamw")
        out_g[n], out_d[n], out_m[n], out_v[n] = [r.reshape(shp) for r in res]
    small_shapes = [W[n].shape for n in SMALL]
    packed = [_pack([d[n] for n in SMALL]) for d in (W, G, M1, V2)]
    res = adamw(*packed, tr=packed[0].shape[0], name="adamw_small")
    for d, r in zip((out_g, out_d, out_m, out_v), res):
        for n, a in zip(SMALL, _unpack(r, small_shapes)):
            d[n] = a

    return (loss, grad_x, *[out_g[n] for n in WEIGHTS], *[out_d[n] for n in WEIGHTS],
            *[out_m[n] for n in WEIGHTS], *[out_v[n] for n in WEIGHTS])
```

```python
import functools

import jax
import jax.numpy as jnp
from jax import lax
from jax.experimental import pallas as pl
from jax.experimental.pallas import tpu as pltpu

F32 = jnp.float32
BF16 = jnp.bfloat16
MESH = pl.DeviceIdType.MESH

LN_EPS = 1e-5
SB_HEADS = 8
MEM_HEADS = 4
ADAM_LR, ADAM_B1, ADAM_B2, ADAM_EPS, ADAM_WD, ADAM_STEP = 0.001, 0.9, 0.999, 1e-08, 0.01, 10

LANES = 128
V7X_VMEM_BYTES = 64 << 20
VMEM_CAP = V7X_VMEM_BYTES - (6 << 20)
N_CHIPS = 4
N_DEV = 8

BIG = ('w_in', 'w_out', 'mem_wq', 'mem_wk', 'mem_wv', 'mem_wo', 'ffn_up', 'ffn_down')
RIDE_IN = ('w_in',)
RIDE_ATT = ('w_out', 'mem_wq', 'mem_wk', 'mem_wv', 'mem_wo')
RIDE_FFN = ('ffn_up', 'ffn_down')
RIDE_MIX = ('w_in', 'w_out')
RIDE_REST = ('mem_wq', 'mem_wk', 'mem_wv', 'mem_wo', 'ffn_up', 'ffn_down')
COL_SHARDED = ('w_in', 'ffn_up')
SMALL = ('conv_w', 'conv_b', 'conv_ln_g', 'conv_ln_b', 'ln1_g', 'ln1_b', 'ln2_g', 'ln2_b',
         'ffn_conv_w', 'ffn_conv_b', 'ln3_g', 'ln3_b')
SMALL_SHARDED = ('conv_w', 'ffn_conv_w')
WEIGHTS = ('w_in', 'conv_w', 'conv_b', 'conv_ln_g', 'conv_ln_b', 'w_out', 'ln1_g', 'ln1_b',
           'mem_wq', 'mem_wk', 'mem_wv', 'mem_wo', 'ln2_g', 'ln2_b', 'ffn_up', 'ffn_conv_w',
           'ffn_conv_b', 'ffn_down', 'ln3_g', 'ln3_b')


def _params(block_bytes, semantics=None, **kw):
    limit = int(min(max(2 * block_bytes + (8 << 20), 32 << 20), VMEM_CAP))
    return pltpu.CompilerParams(dimension_semantics=semantics, vmem_limit_bytes=limit, **kw)


def _pallas(body, **kw):
    call = pl.pallas_call(body, **kw)

    def run(*args):
        return call(*[pltpu.with_memory_space_constraint(a, pltpu.HBM)
                      if jnp.issubdtype(a.dtype, jnp.floating) else a for a in args])

    return run


def _nbytes(shape, dtype):
    n = 1
    for s in shape:
        n *= s
    return n * jnp.dtype(dtype).itemsize


def _dot(a, b):
    return jnp.dot(a, b, preferred_element_type=F32)


def _dot_nt(a, b):
    return lax.dot_general(a, b, (((1,), (1,)), ((), ())), preferred_element_type=F32)


def _dot_tn(a, b):
    return lax.dot_general(a, b, (((0,), (0,)), ((), ())), preferred_element_type=F32)


def _sigmoid(x):
    return 1.0 / (1.0 + jnp.exp(-x))


def mm_nn(a, b, out_dtype, *, tm, tn, name):
    M, K = a.shape
    sharded = b.ndim == 3
    if sharded:
        nsh, _, ns = b.shape
        N, per = nsh * ns, ns // tn
        b_spec = pl.BlockSpec((None, K, tn), lambda i, j: (j // per, 0, j % per))
    else:
        N = b.shape[1]
        b_spec = pl.BlockSpec((K, tn), lambda i, j: (0, j))

    def body(a_ref, b_ref, o_ref):
        o_ref[...] = _dot(a_ref[...].astype(BF16), b_ref[...]).astype(o_ref.dtype)

    blk = _nbytes((tm, K), a.dtype) + _nbytes((K, tn), BF16) + _nbytes((tm, tn), out_dtype)
    return _pallas(
        body, name=name, out_shape=pltpu.HBM((M, N), out_dtype), grid=(M // tm, N // tn),
        in_specs=[pl.BlockSpec((tm, K), lambda i, j: (i, 0)), b_spec],
        out_specs=pl.BlockSpec((tm, tn), lambda i, j: (i, j)),
        compiler_params=_params(blk, ("parallel", "parallel")))(a, b)


def mm_ln(a, b, x, gamma, beta, alpha, *, tm, name):
    M, K = a.shape
    D = b.shape[1]

    def body(a_ref, b_ref, x_ref, g_ref, be_ref, y_ref, yb_ref, zh_ref, rs_ref):
        z = alpha * x_ref[...] + _dot(a_ref[...], b_ref[...])
        mu = jnp.mean(z, axis=-1, keepdims=True)
        zc = z - mu
        rstd = lax.rsqrt(jnp.mean(zc * zc, axis=-1, keepdims=True) + LN_EPS)
        zh = zc * rstd
        y = zh * g_ref[...] + be_ref[...]
        y_ref[...] = y
        yb_ref[...] = y.astype(BF16)
        zh_ref[...] = zh
        rs_ref[...] = rstd

    row = lambda i: (i, 0)
    fix = lambda i: (0, 0)
    blk = _nbytes((tm, K), BF16) + _nbytes((K, D), BF16) + 4 * _nbytes((tm, D), F32)
    return _pallas(
        body, name=name, grid=(M // tm,),
        out_shape=(pltpu.HBM((M, D), F32), pltpu.HBM((M, D), BF16),
                   pltpu.HBM((M, D), F32), pltpu.HBM((M, 1), F32)),
        in_specs=[pl.BlockSpec((tm, K), row), pl.BlockSpec((K, D), fix), pl.BlockSpec((tm, D), row),
                  pl.BlockSpec((1, D), fix), pl.BlockSpec((1, D), fix)],
        out_specs=(pl.BlockSpec((tm, D), row), pl.BlockSpec((tm, D), row), pl.BlockSpec((tm, D), row),
                   pl.BlockSpec((tm, 1), row)),
        compiler_params=_params(blk, ("parallel",)))(a, b, x, gamma, beta)


def ln_bwd(dy, zh, rstd, gamma, *, tm, name):
    M, D = dy.shape

    def body(dy_ref, zh_ref, rs_ref, g_ref, dz_ref, dzb_ref, dg_ref, db_ref):
        @pl.when(pl.program_id(0) == 0)
        def _():
            dg_ref[...] = jnp.zeros_like(dg_ref)
            db_ref[...] = jnp.zeros_like(db_ref)

        dyv, zhv = dy_ref[...], zh_ref[...]
        dg_ref[...] += jnp.sum(dyv * zhv, axis=0, keepdims=True)
        db_ref[...] += jnp.sum(dyv, axis=0, keepdims=True)
        dzh = dyv * g_ref[...]
        m1 = jnp.mean(dzh, axis=-1, keepdims=True)
        m2 = jnp.mean(dzh * zhv, axis=-1, keepdims=True)
        dz = rs_ref[...] * (dzh - m1 - zhv * m2)
        dz_ref[...] = dz
        dzb_ref[...] = dz.astype(BF16)

    row = lambda i: (i, 0)
    fix = lambda i: (0, 0)
    return _pallas(
        body, name=name, grid=(M // tm,),
        out_shape=(pltpu.HBM((M, D), F32), pltpu.HBM((M, D), BF16),
                   pltpu.HBM((1, D), F32), pltpu.HBM((1, D), F32)),
        in_specs=[pl.BlockSpec((tm, D), row), pl.BlockSpec((tm, D), row), pl.BlockSpec((tm, 1), row),
                  pl.BlockSpec((1, D), fix)],
        out_specs=(pl.BlockSpec((tm, D), row), pl.BlockSpec((tm, D), row), pl.BlockSpec((1, D), fix),
                   pl.BlockSpec((1, D), fix)),
        compiler_params=_params(4 * _nbytes((tm, D), F32), ("arbitrary",)))(dy, zh, rstd, gamma)


def mm_nt(a_list, b, out_dtype, *, tm, tk, name, res=None, alpha=None, rider=None):
    M = a_list[0].shape[0]
    widths = [a.shape[1] for a in a_list]
    sharded = b.ndim == 3
    if sharded:
        nsh, K, ns = b.shape
        b_spec = pl.BlockSpec((nsh, tk, ns), lambda i, j: (0, j, 0))
        for w in widths:
            assert w % ns == 0
    else:
        K, N = b.shape
        ns = None
        b_spec = pl.BlockSpec((tk, N), lambda i, j: (j, 0))
    n_a = len(a_list)

    def body(*refs):
        a_refs, b_ref = refs[:n_a], refs[n_a]
        o_ref = refs[-1]
        acc = None
        off = 0
        for a_ref, w in zip(a_refs, widths):
            if sharded:
                for p in range(w // ns):
                    t = _dot_nt(a_ref[:, p * ns:(p + 1) * ns].astype(BF16), b_ref[off // ns + p])
                    acc = t if acc is None else acc + t
            else:
                t = _dot_nt(a_ref[...].astype(BF16), b_ref[:, off:off + w])
                acc = t if acc is None else acc + t
            off += w
        if res is not None:
            acc = acc + alpha * refs[n_a + 1][...]
        o_ref[...] = acc.astype(o_ref.dtype)

    in_specs = [pl.BlockSpec((tm, w), lambda i, j: (i, 0)) for w in widths] + [b_spec]
    args = list(a_list) + [b]
    if res is not None:
        in_specs.append(pl.BlockSpec((tm, tk), lambda i, j: (i, j)))
        args.append(res)
    blk = (sum(_nbytes((tm, w), a.dtype) for a, w in zip(a_list, widths)) + _nbytes((tk, sum(widths)), BF16)
           + 2 * _nbytes((tm, tk), F32))
    in_specs, out_specs, out_shape, scratch = _carry_specs(
        rider, in_specs, (pl.BlockSpec((tm, tk), lambda i, j: (i, j)),), (pltpu.HBM((M, K), out_dtype),), [])
    first = lambda: (pl.program_id(0) == 0) & (pl.program_id(1) == 0)
    last = lambda: (pl.program_id(0) == M // tm - 1) & (pl.program_id(1) == K // tk - 1)
    res_all = _pallas(
        _carry(rider, body, len(args), 1, first, last), name=name, out_shape=out_shape, grid=(M // tm, K // tk),
        in_specs=in_specs, out_specs=out_specs, scratch_shapes=scratch,
        compiler_params=_params(blk, ("arbitrary", "arbitrary")))(*args, *(rider.arrays if rider else ()))
    return res_all[0] if rider is None else (res_all[0], list(res_all[1:]))


def mm_nt_ln_bwd(a_list, b, res, alpha, zh, rstd, gamma, *, tm, name):
    M, D = res.shape
    widths = [a.shape[1] for a in a_list]
    sharded = b.ndim == 3
    if sharded:
        nsh, _, ns = b.shape
        b_spec = pl.BlockSpec((nsh, D, ns), lambda i: (0, 0, 0))
    else:
        ns = None
        b_spec = pl.BlockSpec((D, b.shape[1]), lambda i: (0, 0))
    n_a = len(a_list)

    def body(*refs):
        a_refs, b_ref = refs[:n_a], refs[n_a]
        res_ref, zh_ref, rs_ref, g_ref = refs[n_a + 1:n_a + 5]
        dz_ref, dzb_ref, dg_ref, db_ref = refs[n_a + 5:]

        @pl.when(pl.program_id(0) == 0)
        def _():
            dg_ref[...] = jnp.zeros_like(dg_ref)
            db_ref[...] = jnp.zeros_like(db_ref)

        dy = alpha * res_ref[...]
        off = 0
        for a_ref, w in zip(a_refs, widths):
            if sharded:
                for p in range(w // ns):
                    dy = dy + _dot_nt(a_ref[:, p * ns:(p + 1) * ns], b_ref[off // ns + p])
            else:
                dy = dy + _dot_nt(a_ref[...], b_ref[:, off:off + w])
            off += w
        zhv = zh_ref[...]
        dg_ref[...] += jnp.sum(dy * zhv, axis=0, keepdims=True)
        db_ref[...] += jnp.sum(dy, axis=0, keepdims=True)
        dzh = dy * g_ref[...]
        m1 = jnp.mean(dzh, axis=-1, keepdims=True)
        m2 = jnp.mean(dzh * zhv, axis=-1, keepdims=True)
        dz = rs_ref[...] * (dzh - m1 - zhv * m2)
        dz_ref[...] = dz
        dzb_ref[...] = dz.astype(BF16)

    row = lambda i: (i, 0)
    fix = lambda i: (0, 0)
    in_specs = [pl.BlockSpec((tm, w), row) for w in widths] + [
        b_spec, pl.BlockSpec((tm, D), row), pl.BlockSpec((tm, D), row), pl.BlockSpec((tm, 1), row),
        pl.BlockSpec((1, D), fix)]
    blk = (sum(_nbytes((tm, w), BF16) for w in widths) + _nbytes((D, sum(widths)), BF16)
           + 5 * _nbytes((tm, D), F32))
    return _pallas(
        body, name=name, grid=(M // tm,),
        out_shape=(pltpu.HBM((M, D), F32), pltpu.HBM((M, D), BF16), pltpu.HBM((1, D), F32),
                   pltpu.HBM((1, D), F32)),
        in_specs=in_specs,
        out_specs=(pl.BlockSpec((tm, D), row), pl.BlockSpec((tm, D), row), pl.BlockSpec((1, D), fix),
                   pl.BlockSpec((1, D), fix)),
        compiler_params=_params(blk, ("arbitrary",)))(*a_list, b, res, zh, rstd, gamma)


def mm_tn(a, b_list, *, tk, tn, name, shard_width=None, tmc=None):
    M, K = a.shape
    tmc = M if tmc is None else tmc
    nm = M // tmc
    widths = [b.shape[1] for b in b_list]
    N = sum(widths)
    starts, s = [], 0
    for w in widths:
        assert w % tn == 0
        starts.append(s)
        s += w // tn
    n_b = len(b_list)

    def body(*refs):
        a_ref, b_refs, o_ref, acc = refs[0], refs[1:1 + n_b], refs[-2], refs[-1]
        j, m = pl.program_id(1), pl.program_id(2)
        for b_ref, st, w in zip(b_refs, starts, widths):
            @pl.when((j >= st) & (j < st + w // tn))
            def _(b_ref=b_ref):
                t = _dot_tn(a_ref[...].astype(BF16), b_ref[...].astype(BF16))
                if nm == 1:
                    o_ref[...] = t.astype(o_ref.dtype)
                else:
                    @pl.when(m == 0)
                    def _():
                        acc[...] = t

                    @pl.when(m > 0)
                    def _():
                        acc[...] += t

                    @pl.when(m == nm - 1)
                    def _():
                        o_ref[...] = acc[...].astype(o_ref.dtype)

    def b_map(st, w):
        nb = w // tn
        return lambda i, j, m: (jnp.where((j >= st) & (j < st + nb), m, 0), jnp.clip(j - st, 0, nb - 1))

    in_specs = [pl.BlockSpec((tmc, tk), lambda i, j, m: (m, i))]
    in_specs += [pl.BlockSpec((tmc, tn), b_map(st, w)) for st, w in zip(starts, widths)]
    if shard_width is None:
        out_shape = pltpu.HBM((K, N), BF16)
        out_spec = pl.BlockSpec((tk, tn), lambda i, j, m: (i, j))
    else:
        per = shard_width // tn
        out_shape = pltpu.HBM((N // shard_width, K, shard_width), BF16)
        out_spec = pl.BlockSpec((None, tk, tn), lambda i, j, m: (j // per, i, j % per))
    acc_shape = (tk, tn) if nm > 1 else (8, LANES)
    blk = (_nbytes((tmc, tk), a.dtype) + n_b * _nbytes((tmc, tn), b_list[0].dtype) + 2 * _nbytes((tk, tn), F32))
    return _pallas(
        body, name=name, out_shape=out_shape, grid=(K // tk, N // tn, nm), in_specs=in_specs, out_specs=out_spec,
        scratch_shapes=[pltpu.VMEM(acc_shape, F32)],
        compiler_params=_params(blk, ("parallel", "arbitrary", "arbitrary")))(a, *b_list)


CONV_PAD = 32
CONV_CHUNK = 128


def _rows(win, off, n, shifts):
    b, a = off % 8, off // 8
    if b not in shifts:
        shifts[b] = win if b == 0 else win[b:b + n + CONV_PAD - 8, :]
    return shifts[b][8 * a:8 * a + n, :]


def _by_residue(n_taps, offset):
    return sorted(range(n_taps), key=lambda k: (offset(k) % 8, k))


def conv_fwd(proj, conv_w, conv_b, *, name, rider=None):
    S = proj.shape[0]
    KW, C = conv_w.shape
    nct = C // LANES
    rc = min(CONV_CHUNK, S)

    def body(a_ref, g_ref, w_ref, b_ref, o_ref, pad):
        pad[0:CONV_PAD, :] = jnp.zeros((CONV_PAD, LANES), F32)
        pad[CONV_PAD:, :] = a_ref[...] * _sigmoid(g_ref[...])
        w = w_ref[...]
        bias = b_ref[...]

        def chunk(i, _):
            base = pl.multiple_of(i * rc, rc)
            win = pad[pl.ds(base, rc + CONV_PAD), :]
            acc = jnp.zeros((rc, LANES), F32) + bias
            shifts = {}
            for k in _by_residue(KW, lambda k: CONV_PAD - (KW - 1) + k):
                acc = acc + w[k:k + 1, :] * _rows(win, CONV_PAD - (KW - 1) + k, rc, shifts)
            o_ref[pl.ds(base, rc), :] = acc
            return 0

        lax.fori_loop(0, S // rc, chunk, 0)

    in_specs, out_specs, out_shape, scratch = _carry_specs(
        rider, [pl.BlockSpec((S, LANES), lambda c: (0, c)), pl.BlockSpec((S, LANES), lambda c: (0, c + nct)),
                pl.BlockSpec((KW, LANES), lambda c: (0, c)), pl.BlockSpec((1, LANES), lambda c: (0, c))],
        (pl.BlockSpec((S, LANES), lambda c: (0, c)),), (pltpu.HBM((S, C), F32),),
        [pltpu.VMEM((S + CONV_PAD, LANES), F32)])
    first = lambda: pl.program_id(0) == 0
    last = lambda: pl.program_id(0) == nct - 1
    res = _pallas(
        _carry(rider, body, 4, 1, first, last), name=name, grid=(nct,), out_shape=out_shape,
        in_specs=in_specs, out_specs=out_specs, scratch_shapes=scratch,
        compiler_params=_params(4 * _nbytes((S, LANES), F32), ("arbitrary",)))(
            proj, proj, conv_w, conv_b, *(rider.arrays if rider else ()))
    return res[0], list(res[1:])


def conv_bwd(du1, proj, conv_w, *, name, rider=None):
    S = proj.shape[0]
    KW, C = conv_w.shape
    nct = C // LANES
    rc = min(CONV_CHUNK, S)

    def body(d_ref, a_ref, g_ref, w_ref, da_ref, dg_ref, dw_ref, db_ref, pad_u, pad_d, du0, dw_acc):
        dw_acc[...] = jnp.zeros_like(dw_acc)
        pad_u[0:CONV_PAD, :] = jnp.zeros((CONV_PAD, LANES), F32)
        pad_u[CONV_PAD:, :] = a_ref[...] * _sigmoid(g_ref[...])
        pad_d[0:S, :] = d_ref[...]
        pad_d[S:, :] = jnp.zeros((CONV_PAD, LANES), F32)
        w = w_ref[...]
        db_ref[...] = jnp.sum(d_ref[...], axis=0, keepdims=True)

        def chunk(i, _):
            base = pl.multiple_of(i * rc, rc)
            d = pad_d[pl.ds(base, rc), :]
            win_u = pad_u[pl.ds(base, rc + CONV_PAD), :]
            win_d = pad_d[pl.ds(base, rc + CONV_PAD), :]
            shifts = {}
            for k in _by_residue(KW, lambda k: CONV_PAD - (KW - 1) + k):
                u_k = _rows(win_u, CONV_PAD - (KW - 1) + k, rc, shifts)
                dw_acc[k:k + 1, :] += jnp.sum(d * u_k, axis=0, keepdims=True)
            acc = jnp.zeros((rc, LANES), F32)
            shifts = {}
            for k in _by_residue(KW, lambda k: KW - 1 - k):
                acc = acc + w[k:k + 1, :] * _rows(win_d, KW - 1 - k, rc, shifts)
            du0[pl.ds(base, rc), :] = acc
            return 0

        lax.fori_loop(0, S // rc, chunk, 0)
        dw_ref[...] = dw_acc[0:KW, :]
        a, sg = a_ref[...], _sigmoid(g_ref[...])
        d0 = du0[...]
        da_ref[...] = (d0 * sg).astype(BF16)
        dg_ref[...] = (d0 * a * sg * (1.0 - sg)).astype(BF16)

    col = lambda c: (0, c)
    in_specs, out_specs, out_shape, scratch = _carry_specs(
        rider, [pl.BlockSpec((S, LANES), col), pl.BlockSpec((S, LANES), col),
                pl.BlockSpec((S, LANES), lambda c: (0, c + nct)), pl.BlockSpec((KW, LANES), col)],
        (pl.BlockSpec((S, LANES), col), pl.BlockSpec((S, LANES), col), pl.BlockSpec((KW, LANES), col),
         pl.BlockSpec((1, LANES), col)),
        (pltpu.HBM((S, C), BF16), pltpu.HBM((S, C), BF16), pltpu.HBM((KW, C), F32), pltpu.HBM((1, C), F32)),
        [pltpu.VMEM((S + CONV_PAD, LANES), F32), pltpu.VMEM((S + CONV_PAD, LANES), F32),
         pltpu.VMEM((S, LANES), F32), pltpu.VMEM((CONV_PAD, LANES), F32)])
    first = lambda: pl.program_id(0) == 0
    last = lambda: pl.program_id(0) == nct - 1
    res = _pallas(
        _carry(rider, body, 4, 4, first, last), name=name, grid=(nct,), out_shape=out_shape,
        in_specs=in_specs, out_specs=out_specs, scratch_shapes=scratch,
        compiler_params=_params(8 * _nbytes((S, LANES), F32), ("arbitrary",)))(
            du1, proj, proj, conv_w, *(rider.arrays if rider else ()))
    return res[:4], list(res[4:])


def ln_silu(u1, o_sb, gamma, beta, *, tm, name):
    S, C = u1.shape

    def body(u_ref, o_ref, g_ref, b_ref, out_ref):
        z = u_ref[...]
        mu = jnp.mean(z, axis=-1, keepdims=True)
        zc = z - mu
        y = zc * lax.rsqrt(jnp.mean(zc * zc, axis=-1, keepdims=True) + LN_EPS) * g_ref[...] + b_ref[...]
        out_ref[:, 0:C] = (y * _sigmoid(y)).astype(BF16)
        out_ref[:, C:] = o_ref[...].astype(BF16)

    row = lambda i: (i, 0)
    fix = lambda i: (0, 0)
    return _pallas(
        body, name=name, out_shape=pltpu.HBM((S, 2 * C), BF16), grid=(S // tm,),
        in_specs=[pl.BlockSpec((tm, C), row), pl.BlockSpec((tm, C), row), pl.BlockSpec((1, C), fix),
                  pl.BlockSpec((1, C), fix)],
        out_specs=pl.BlockSpec((tm, 2 * C), row),
        compiler_params=_params(4 * _nbytes((tm, C), F32), ("parallel",)))(u1, o_sb, gamma, beta)


def ln_silu_bwd(dua, u1, gamma, beta, *, tm, name):
    S, C = u1.shape

    def body(d_ref, u_ref, g_ref, b_ref, du1_ref, dg_ref, db_ref):
        @pl.when(pl.program_id(0) == 0)
        def _():
            dg_ref[...] = jnp.zeros_like(dg_ref)
            db_ref[...] = jnp.zeros_like(db_ref)

        z = u_ref[...]
        mu = jnp.mean(z, axis=-1, keepdims=True)
        zc = z - mu
        rstd = lax.rsqrt(jnp.mean(zc * zc, axis=-1, keepdims=True) + LN_EPS)
        zh = zc * rstd
        y = zh * g_ref[...] + b_ref[...]
        sg = _sigmoid(y)
        dy = d_ref[...] * (sg * (1.0 + y * (1.0 - sg)))
        dg_ref[...] += jnp.sum(dy * zh, axis=0, keepdims=True)
        db_ref[...] += jnp.sum(dy, axis=0, keepdims=True)
        dzh = dy * g_ref[...]
        m1 = jnp.mean(dzh, axis=-1, keepdims=True)
        m2 = jnp.mean(dzh * zh, axis=-1, keepdims=True)
        du1_ref[...] = rstd * (dzh - m1 - zh * m2)

    row = lambda i: (i, 0)
    fix = lambda i: (0, 0)
    return _pallas(
        body, name=name, grid=(S // tm,),
        out_shape=(pltpu.HBM((S, C), F32), pltpu.HBM((1, C), F32),
                   pltpu.HBM((1, C), F32)),
        in_specs=[pl.BlockSpec((tm, C), row), pl.BlockSpec((tm, C), row), pl.BlockSpec((1, C), fix),
                  pl.BlockSpec((1, C), fix)],
        out_specs=(pl.BlockSpec((tm, C), row), pl.BlockSpec((1, C), fix), pl.BlockSpec((1, C), fix)),
        compiler_params=_params(4 * _nbytes((tm, C), F32), ("arbitrary",)))(dua, u1, gamma, beta)


SB_BLOCK = 256
SB_STOP = -105.0
SB_GROUP = 4


def _split_dot(x, tri):
    hi = x.astype(BF16)
    lo = (x - hi.astype(F32)).astype(BF16)
    return _dot(hi, tri) + _dot(lo, tri)


def _neg_softplus(z):
    return -(jnp.maximum(z, 0.0) + jnp.log(1.0 + jnp.exp(-jnp.abs(z))))


def sb_fwd(proj, *, q_col, name, rider=None):
    S = proj.shape[0]
    dh = LANES // 2
    W = SB_HEADS * dh
    BW = SB_GROUP * dh
    ngrp = W // BW
    T = min(SB_BLOCK, S)
    nblk = S // T
    scale = dh ** -0.5
    qb0 = q_col // BW
    heads = range(SB_GROUP)
    sl = [slice(h * dh, (h + 1) * dh) for h in heads]

    def body(q_ref, k_ref, v_ref, o_ref, l_ref, qs, ks, vs):
        r_i = lax.broadcasted_iota(jnp.int32, (T, T), 0)
        c_i = lax.broadcasted_iota(jnp.int32, (T, T), 1)
        tri = (r_i >= c_i).astype(BF16)
        vis = c_i < r_i
        lane = lax.broadcasted_iota(jnp.int32, (T, dh), 1)

        qs[...] = (q_ref[...] * scale).astype(BF16)
        ks[...] = k_ref[...].astype(BF16)
        vs[...] = v_ref[...].astype(BF16)

        def step(qb, blocks, st):
            nb = range(len(blocks))
            kb = [[ks[pl.ds(j0, T), sl[h]] for h in heads] for j0, _ in blocks]
            vb = [[vs[pl.ds(j0, T), sl[h]] for h in heads] for j0, _ in blocks]
            z = [[_dot_nt(qb[h], kb[b][h]) for h in heads] for b in nb]
            lk = [[_neg_softplus(z[b][h]) for h in heads] for b in nb]
            lk = [[jnp.where(vis, lk[b][h], 0.0) if blocks[b][1] else lk[b][h] for h in heads] for b in nb]
            C = [[_split_dot(lk[b][h], tri) for h in heads] for b in nb]
            R = [[st[2 * h + 1] for h in heads]]
            for b in nb:
                R.append([R[b][h] + C[b][h][:, 0:1] for h in heads])
            A = [[jnp.exp(z[b][h] + C[b][h] + R[b][h]) for h in heads] for b in nb]
            A = [[jnp.where(vis, A[b][h], 0.0) if blocks[b][1] else A[b][h] for h in heads] for b in nb]
            out = ()
            for h in heads:
                acc = st[2 * h]
                for b in nb:
                    acc = acc + _dot(A[b][h].astype(BF16), vb[b][h])
                out += (acc, R[-1][h])
            return out

        zero = (jnp.zeros((T, dh), F32), jnp.zeros((T, 1), F32))

        def finish(r0, i, c):
            walked = jnp.asarray(i - c[0]).astype(F32)
            for h in heads:
                o_ref[pl.ds(r0, T), sl[h]] = c[1 + 2 * h]
                l_ref[pl.ds(r0, T), sl[h]] = jnp.where(lane == 1, walked, c[2 + 2 * h])

        finish(0, 0, (-1,) + step([qs[0:T, sl[h]] for h in heads], [(0, True)], zero * SB_GROUP))

        def qblock(i, _):
            r0 = pl.multiple_of(i * T, T)
            qb = [qs[pl.ds(r0, T), sl[h]] for h in heads]
            state = step(qb, [(r0, True), (pl.multiple_of(r0 - T, T), False)], zero * SB_GROUP)

            def more(c):
                worst = c[2]
                for h in heads[1:]:
                    worst = jnp.maximum(worst, c[2 + 2 * h])
                return (c[0] >= 0) & (jnp.max(worst) >= SB_STOP)

            def walk(c):
                return (c[0] - 1,) + step(qb, [(pl.multiple_of(c[0] * T, T), False)], c[1:])

            finish(r0, i, lax.while_loop(more, walk, (i - 2,) + state))
            return 0

        lax.fori_loop(1, nblk, qblock, 0)

    blk = lambda off: pl.BlockSpec((S, BW), lambda g: (0, qb0 + off * ngrp + g), pipeline_mode=pl.Buffered(1))
    out = pl.BlockSpec((S, BW), lambda g: (0, g))
    in_specs, out_specs, out_shape, scratch = _carry_specs(
        rider, [blk(0), blk(1), blk(2)], (out, out), (pltpu.HBM((S, W), F32), pltpu.HBM((S, W), F32)),
        [pltpu.VMEM((S, BW), BF16)] * 3)
    first = lambda: pl.program_id(0) == 0
    last = lambda: pl.program_id(0) == ngrp - 1
    res = _pallas(
        _carry(rider, body, 3, 2, first, last), name=name, grid=(ngrp,), out_shape=out_shape,
        in_specs=in_specs, out_specs=out_specs, scratch_shapes=scratch,
        compiler_params=_params(5 * _nbytes((S, BW), F32), ("arbitrary",)))(
            proj, proj, proj, *(rider.arrays if rider else ()))
    return res[0], res[1], list(res[2:])


def sb_bwd(proj, ltot, dua, *, q_col, do_col, name, rider=None):
    S = proj.shape[0]
    dh = LANES // 2
    W = SB_HEADS * dh
    BW = SB_GROUP * dh
    ngrp = W // BW
    T = min(SB_BLOCK, S)
    nblk = S // T
    scale = dh ** -0.5
    qb0 = q_col // BW
    db0 = do_col // BW
    heads = range(SB_GROUP)
    sl = [slice(h * dh, (h + 1) * dh) for h in heads]

    def body(q_ref, k_ref, v_ref, l_ref, do_ref, dq_ref, dk_ref, dv_ref, dks, dvs):
        r_i = lax.broadcasted_iota(jnp.int32, (T, T), 0)
        c_i = lax.broadcasted_iota(jnp.int32, (T, T), 1)
        tri_rev = (r_i >= c_i).astype(BF16)
        tri_fwd = (r_i <= c_i).astype(BF16)
        vis = c_i < r_i

        dks[...] = jnp.zeros_like(dks)
        dvs[...] = jnp.zeros_like(dvs)

        def step(qb, dob, Lt, blocks, st):
            nb = range(len(blocks))
            kb = [[k_ref[pl.ds(j0, T), sl[h]].astype(BF16) for h in heads] for j0, _ in blocks]
            vb = [[v_ref[pl.ds(j0, T), sl[h]].astype(BF16) for h in heads] for j0, _ in blocks]
            z = [[_dot_nt(qb[h], kb[b][h]) for h in heads] for b in nb]
            dA =[[_dot_nt(dob[h], vb[b][h]) for h in heads] for b in nb]
            lk = [[_neg_softplus(z[b][h]) for h in heads] for b in nb]
            beta = [[jnp.exp(z[b][h] + lk[b][h]) for h in heads] for b in nb]
            lk = [[jnp.where(vis, lk[b][h], 0.0) if blocks[b][1] else lk[b][h] for h in heads] for b in nb]
            C = [[_split_dot(lk[b][h], tri_rev) for h in heads] for b in nb]
            P = [[st[3 * h + 1] for h in heads]]
            for b in nb:
                P.append([P[b][h] + C[b][h][:, 0:1] for h in heads])
            A = [[jnp.exp(z[b][h] + C[b][h] + (Lt[h] - P[b + 1][h])) for h in heads] for b in nb]
            A = [[jnp.where(vis, A[b][h], 0.0) if blocks[b][1] else A[b][h] for h in heads] for b in nb]
            g = [[A[b][h] * dA[b][h] for h in heads] for b in nb]
            Gin = [[_split_dot(g[b][h], tri_fwd) for h in heads] for b in nb]
            Gp = [[st[3 * h + 2] for h in heads]]
            for b in nb:
                Gp.append([Gp[b][h] + Gin[b][h][:, T - 1:T] for h in heads])
            dz = [[g[b][h] - beta[b][h] * (Gp[b][h] + Gin[b][h]) for h in heads] for b in nb]
            dz = [[jnp.where(vis, dz[b][h], 0.0) if blocks[b][1] else dz[b][h] for h in heads] for b in nb]
            dzb = [[dz[b][h].astype(BF16) for h in heads] for b in nb]
            out = ()
            for h in heads:
                dq = st[3 * h]
                for b in nb:
                    j0 = blocks[b][0]
                    dvs[pl.ds(j0, T), sl[h]] += _dot_tn(A[b][h].astype(BF16), dob[h])
                    dks[pl.ds(j0, T), sl[h]] += _dot_tn(dzb[b][h], qb[h])
                    dq = dq + _dot(dzb[b][h], kb[b][h])
                out += (dq, P[-1][h], Gp[-1][h])
            return out

        zero = jnp.zeros((T, 1), F32)
        init = (jnp.zeros((T, dh), F32), zero, zero)

        def operands(r0):
            return ([(q_ref[pl.ds(r0, T), sl[h]] * scale).astype(BF16) for h in heads],
                    [do_ref[pl.ds(r0, T), sl[h]].astype(BF16) for h in heads],
                    [l_ref[pl.ds(r0, T), h * dh:h * dh + 1] for h in heads])

        def finish(r0, c):
            for h in heads:
                dq_ref[pl.ds(r0, T), sl[h]] = (c[3 * h] * scale).astype(BF16)

        def qblock(i, _):
            r0 = pl.multiple_of(i * T, T)
            qb, dob, Lt = operands(r0)
            walked = jnp.clip(jnp.max(l_ref[pl.ds(r0, 8), 1:2]).astype(jnp.int32), 1, i + 1)

            def inner(j, c):
                return step(qb, dob, Lt, [(pl.multiple_of(j * T, T), False)], c)

            c = lax.fori_loop(i + 1 - walked, i, inner, init * SB_GROUP)
            finish(r0, step(qb, dob, Lt, [(r0, True)], c))
            return 0

        lax.fori_loop(0, nblk, qblock, 0)
        dk_ref[...] = dks[...].astype(BF16)
        dv_ref[...] = dvs[...].astype(BF16)

    once = pl.Buffered(1)
    blk = lambda off: pl.BlockSpec((S, BW), lambda g: (0, qb0 + off * ngrp + g), pipeline_mode=once)
    out = pl.BlockSpec((S, BW), lambda g: (0, g))
    o_shape = pltpu.HBM((S, W), BF16)
    in_specs, out_specs, out_shape, scratch = _carry_specs(
        rider, [blk(0), blk(1), blk(2), pl.BlockSpec((S, BW), lambda g: (0, g), pipeline_mode=once),
                pl.BlockSpec((S, BW), lambda g: (0, db0 + g), pipeline_mode=once)], (out, out, out),
        (o_shape, o_shape, o_shape), [pltpu.VMEM((S, BW), F32)] * 2)
    first = lambda: pl.program_id(0) == 0
    last = lambda: pl.program_id(0) == ngrp - 1
    res = _pallas(
        _carry(rider, body, 5, 3, first, last), name=name, grid=(ngrp,), out_shape=out_shape,
        in_specs=in_specs, out_specs=out_specs, scratch_shapes=scratch,
        compiler_params=_params(6 * _nbytes((S, BW), F32), ("arbitrary",)))(
            proj, proj, proj, ltot, dua, *(rider.arrays if rider else ()))
    return res[0], res[1], res[2], list(res[3:])


def xattn_fwd(q, k, v, *, tm, name):
    S, D = q.shape
    Mlen = k.shape[0]
    hd = D // MEM_HEADS
    scale = hd ** -0.5

    def body(q_ref, k_ref, v_ref, o_ref):
        for h in range(MEM_HEADS):
            sl = slice(h * hd, (h + 1) * hd)
            s = _dot_nt(q_ref[:, sl], k_ref[:, sl]) * scale
            e = jnp.exp(s - jnp.max(s, axis=-1, keepdims=True))
            p = e / jnp.sum(e, axis=-1, keepdims=True)
            o_ref[:, sl] = _dot(p.astype(BF16), v_ref[:, sl]).astype(BF16)

    row = lambda i: (i, 0)
    fix = lambda i: (0, 0)
    return _pallas(
        body, name=name, out_shape=pltpu.HBM((S, D), BF16), grid=(S // tm,),
        in_specs=[pl.BlockSpec((tm, D), row), pl.BlockSpec((Mlen, D), fix), pl.BlockSpec((Mlen, D), fix)],
        out_specs=pl.BlockSpec((tm, D), row),
        compiler_params=_params(4 * _nbytes((tm, D), F32), ("parallel",)))(q, k, v)


def xattn_bwd(q, do, k, v, *, tm, name):
    S, D = q.shape
    Mlen = k.shape[0]
    hd = D // MEM_HEADS
    scale = hd ** -0.5

    def body(q_ref, do_ref, k_ref, v_ref, dq_ref, dk_ref, dv_ref):
        @pl.when(pl.program_id(0) == 0)
        def _():
            dk_ref[...] = jnp.zeros_like(dk_ref)
            dv_ref[...] = jnp.zeros_like(dv_ref)

        for h in range(MEM_HEADS):
            sl = slice(h * hd, (h + 1) * hd)
            qh, doh, kh, vh = q_ref[:, sl], do_ref[:, sl], k_ref[:, sl], v_ref[:, sl]
            s = _dot_nt(qh, kh) * scale
            e = jnp.exp(s - jnp.max(s, axis=-1, keepdims=True))
            p = e / jnp.sum(e, axis=-1, keepdims=True)
            dp = _dot_nt(doh, vh)
            ds = (p * (dp - jnp.sum(p * dp, axis=-1, keepdims=True)) * scale).astype(BF16)
            dq_ref[:, sl] = _dot(ds, kh).astype(BF16)
            dk_ref[:, sl] += _dot_tn(ds, qh)
            dv_ref[:, sl] += _dot_tn(p.astype(BF16), doh)

    row = lambda i: (i, 0)
    fix = lambda i: (0, 0)
    return _pallas(
        body, name=name, grid=(S // tm,),
        out_shape=(pltpu.HBM((S, D), BF16), pltpu.HBM((Mlen, D), F32),
                   pltpu.HBM((Mlen, D), F32)),
        in_specs=[pl.BlockSpec((tm, D), row), pl.BlockSpec((tm, D), row), pl.BlockSpec((Mlen, D), fix),
                  pl.BlockSpec((Mlen, D), fix)],
        out_specs=(pl.BlockSpec((tm, D), row), pl.BlockSpec((Mlen, D), fix), pl.BlockSpec((Mlen, D), fix)),
        compiler_params=_params(6 * _nbytes((tm, D), F32), ("arbitrary",)))(q, do, k, v)


FFN_HALO = 8


def _conv3(ext, w, lo):
    tm = ext.shape[0] - FFN_HALO
    return (w[0:1, :] * ext[lo:lo + tm, :] + w[1:2, :] * ext[lo + 1:lo + 1 + tm, :]
            + w[2:3, :] * ext[lo + 2:lo + 2 + tm, :])


def ffn_up_fwd(xb, w_up, conv_w, conv_b, *, tm, tn, name, rider=None):
    S, D = xb.shape
    nsh, _, ns = w_up.shape
    F = nsh * ns // 2
    per = ns // tn
    ncol = F // tn
    KW = conv_w.shape[0]
    assert KW == 3

    def body(x_ref, wv_ref, wg_ref, cwv_ref, cwg_ref, cbv_ref, cbg_ref, uv_ref, ug_ref, cv_ref, cg_ref, h_ref,
             carry):
        @pl.when(pl.program_id(1) == 0)
        def _():
            carry[...] = jnp.zeros_like(carry)

        x = x_ref[...]
        uv = _dot(x, wv_ref[...])
        ug = _dot(x, wg_ref[...])
        uv_ref[...] = uv.astype(BF16)
        ug_ref[...] = ug.astype(BF16)
        lo = FFN_HALO - (KW - 1)
        cv = _conv3(jnp.concatenate([carry[0], uv], axis=0), cwv_ref[...], lo) + cbv_ref[...]
        cg = _conv3(jnp.concatenate([carry[1], ug], axis=0), cwg_ref[...], lo) + cbg_ref[...]
        carry[0] = uv[tm - FFN_HALO:, :]
        carry[1] = ug[tm - FFN_HALO:, :]
        cv_ref[...] = cv.astype(BF16)
        cg_ref[...] = cg.astype(BF16)
        h_ref[...] = (cg * _sigmoid(cg) * cv).astype(BF16)

    wspec = lambda half: pl.BlockSpec((None, D, tn), lambda j, i: (half * (nsh // 2) + j // per, 0, j % per))
    cspec = lambda rows, half: pl.BlockSpec((rows, tn), lambda j, i: (0, half * ncol + j))
    out = pl.BlockSpec((tm, tn), lambda j, i: (i, j))
    o_shape = pltpu.HBM((S, F), BF16)
    blk = _nbytes((tm, D), BF16) + 2 * _nbytes((D, tn), BF16) + 8 * _nbytes((tm, tn), F32)
    nrow = S // tm
    in_specs, out_specs, out_shape, scratch = _carry_specs(
        rider, [pl.BlockSpec((tm, D), lambda j, i: (i, 0)), wspec(0), wspec(1), cspec(KW, 0), cspec(KW, 1),
                cspec(1, 0), cspec(1, 1)], (out,) * 5, (o_shape,) * 5, [pltpu.VMEM((2, FFN_HALO, tn), F32)])
    first = lambda: (pl.program_id(0) == 0) & (pl.program_id(1) == 0)
    last = lambda: (pl.program_id(0) == ncol - 1) & (pl.program_id(1) == nrow - 1)
    res = _pallas(
        _carry(rider, body, 7, 5, first, last), name=name, grid=(ncol, nrow), out_shape=out_shape,
        in_specs=in_specs, out_specs=out_specs, scratch_shapes=scratch,
        compiler_params=_params(blk, ("arbitrary", "arbitrary")))(
            xb, w_up, w_up, conv_w, conv_w, conv_b, conv_b, *(rider.arrays if rider else ()))
    return res[:5], list(res[5:])


def ffn_mid_bwd(dzb, w_down, up_v, up_g, conv_v, conv_g, conv_w, *, tm, tn, name, rider=None):
    S, D = dzb.shape
    F = up_v.shape[1]
    ncol = F // tn
    nrow = S // tm
    KW = conv_w.shape[0]
    assert KW == 3

    def body(dz_ref, wd_ref, uv_ref, ug_ref, cv_ref, cg_ref, cwv_ref, cwg_ref,
             dv_ref, dg_ref, dwv_ref, dwg_ref, dbv_ref, dbg_ref, carry):
        @pl.when(pl.program_id(1) == 0)
        def _():
            carry[...] = jnp.zeros_like(carry)
            for r in (dwv_ref, dwg_ref, dbv_ref, dbg_ref):
                r[...] = jnp.zeros_like(r)

        cv, cg = cv_ref[...].astype(F32), cg_ref[...].astype(F32)
        dh = _dot_nt(dz_ref[...], wd_ref[...])
        sg = _sigmoid(cg)
        dcv = dh * (cg * sg)
        dcg = dh * cv * (sg * (1.0 + cg * (1.0 - sg)))

        def back(dc, u_ref, cw, slot, du_ref, dw_ref, db_ref):
            ext = jnp.concatenate([dc, carry[slot]], axis=0)
            ahead = [dc, ext[1:tm + 1, :], ext[2:tm + 2, :]]
            du = cw[2:3, :] * ahead[0] + cw[1:2, :] * ahead[1] + cw[0:1, :] * ahead[2]
            du_ref[...] = du.astype(BF16)
            carry[slot] = dc[0:FFN_HALO, :]
            u = u_ref[...].astype(F32)
            for k in range(KW):
                dw_ref[k:k + 1, :] += jnp.sum(ahead[KW - 1 - k] * u, axis=0, keepdims=True)
            db_ref[...] += jnp.sum(dc, axis=0, keepdims=True)

        back(dcv, uv_ref, cwv_ref[...], 0, dv_ref, dwv_ref, dbv_ref)
        back(dcg, ug_ref, cwg_ref[...], 1, dg_ref, dwg_ref, dbg_ref)

    rev = lambda i: nrow - 1 - i
    tile = pl.BlockSpec((tm, tn), lambda j, i: (rev(i), j))
    cspec = lambda half: pl.BlockSpec((KW, tn), lambda j, i: (0, half * ncol + j))
    acc = lambda rows: pl.BlockSpec((rows, tn), lambda j, i: (0, j))
    big = pltpu.HBM((S, F), BF16)
    blk = _nbytes((tm, D), BF16) + _nbytes((tn, D), BF16) + 10 * _nbytes((tm, tn), F32)
    in_specs, out_specs, out_shape, scratch = _carry_specs(
        rider, [pl.BlockSpec((tm, D), lambda j, i: (rev(i), 0)), pl.BlockSpec((tn, D), lambda j, i: (j, 0)),
                tile, tile, tile, tile, cspec(0), cspec(1)],
        (tile, tile, acc(KW), acc(KW), acc(1), acc(1)),
        (big, big, pltpu.HBM((KW, F), F32), pltpu.HBM((KW, F), F32), pltpu.HBM((1, F), F32),
         pltpu.HBM((1, F), F32)), [pltpu.VMEM((2, FFN_HALO, tn), F32)])
    first = lambda: (pl.program_id(0) == 0) & (pl.program_id(1) == 0)
    last = lambda: (pl.program_id(0) == ncol - 1) & (pl.program_id(1) == nrow - 1)
    res = _pallas(
        _carry(rider, body, 8, 6, first, last), name=name, grid=(ncol, nrow), out_shape=out_shape,
        in_specs=in_specs, out_specs=out_specs, scratch_shapes=scratch,
        compiler_params=_params(blk, ("arbitrary", "arbitrary")))(
            dzb, w_down, up_v, up_g, conv_v, conv_g, conv_w, conv_w, *(rider.arrays if rider else ()))
    return res[:6], list(res[6:])


def loss_head(y, target, *, tm, name):
    S, D = y.shape

    def body(y_ref, t_ref, dy_ref, l_ref):
        @pl.when(pl.program_id(0) == 0)
        def _():
            l_ref[...] = jnp.zeros_like(l_ref)

        e = y_ref[...] - t_ref[...]
        dy_ref[...] = e * (1.0 / D)
        l_ref[...] += 0.5 * jnp.sum(jnp.mean(e * e, axis=-1, keepdims=True), axis=0, keepdims=True)

    row = lambda i: (i, 0)
    return _pallas(
        body, name=name, grid=(S // tm,),
        out_shape=(pltpu.HBM((S, D), F32), pltpu.HBM((1, 1), F32)),
        in_specs=[pl.BlockSpec((tm, D), row), pl.BlockSpec((tm, D), row)],
        out_specs=(pl.BlockSpec((tm, D), row), pl.BlockSpec((1, 1), lambda i: (0, 0))),
        compiler_params=_params(3 * _nbytes((tm, D), F32), ("arbitrary",)))(y, target)


def adamw(w, g, m, v, *, tr, name):
    R, C = w.shape
    c1 = 1.0 - ADAM_B1 ** ADAM_STEP
    c2 = 1.0 - ADAM_B2 ** ADAM_STEP

    def body(w_ref, g_ref, m_ref, v_ref, go_ref, d_ref, mo_ref, vo_ref):
        gv = g_ref[...]
        mn = ADAM_B1 * m_ref[...] + (1.0 - ADAM_B1) * gv
        vn = ADAM_B2 * v_ref[...] + (1.0 - ADAM_B2) * (gv * gv)
        go_ref[...] = gv
        mo_ref[...] = mn
        vo_ref[...] = vn
        d_ref[...] = -ADAM_LR * ((mn / c1) / (jnp.sqrt(vn / c2) + ADAM_EPS) + ADAM_WD * w_ref[...])

    spec = pl.BlockSpec((tr, C), lambda i: (i, 0))
    shape = pltpu.HBM((R, C), F32)
    return _pallas(
        body, name=name, grid=(R // tr,), out_shape=(shape,) * 4, in_specs=[spec] * 4, out_specs=(spec,) * 4,
        compiler_params=_params(8 * _nbytes((tr, C), F32), ("parallel",)))(w, g, m, v)


def add_pairs(gs, gots, core, *, name):
    k = len(gs)

    def body(c_ref, *refs):
        for a_ref, b_ref, o_ref in zip(refs[:k], refs[k:2 * k], refs[2 * k:]):
            o_ref[...] = (a_ref[...].astype(F32) + b_ref[...].astype(F32)).astype(BF16)

    own = [pl.BlockSpec((None, None) + g.shape[2:], lambda i, c: (i, c[0], 0, 0)) for g in gs]
    half = [pl.BlockSpec((None,) + g.shape[1:], lambda i, c: (i, 0, 0)) for g in gots]
    grid_spec = pltpu.PrefetchScalarGridSpec(
        num_scalar_prefetch=1, grid=(N_CHIPS,), in_specs=own + half, out_specs=tuple(half))
    blk = 3 * sum(_nbytes(g.shape[1:], BF16) for g in gots)
    return _pallas(
        body, name=name, grid_spec=grid_spec, out_shape=tuple(pltpu.HBM(g.shape, BF16) for g in gots),
        compiler_params=_params(blk, ("parallel",)))(core, *gs, *gots)


def sum_chips_into(bs, dests, layer, core, *, name):
    k = len(bs)
    steps = 2

    def body(c_ref, *refs):
        for b_ref, o_ref in zip(refs[:k], refs[2 * k:]):
            acc = b_ref[0].astype(F32)
            for p in range(1, N_CHIPS):
                acc = acc + b_ref[p].astype(F32)
            o_ref[...] = acc

    ins = [pl.BlockSpec((N_CHIPS, b.shape[1] // steps, b.shape[2]), lambda i, c: (0, i, 0)) for b in bs]
    outs = tuple(pl.BlockSpec((None, None, b.shape[1] // steps, b.shape[2]), lambda i, c: (layer, c[0], i, 0))
                 for b in bs)
    grid_spec = pltpu.PrefetchScalarGridSpec(
        num_scalar_prefetch=1, grid=(steps,), in_specs=ins + [pl.BlockSpec(memory_space=pl.ANY)] * k,
        out_specs=outs)
    blk = sum(_nbytes(b.shape, BF16) + _nbytes(b.shape[1:], F32) for b in bs) // steps
    return _pallas(
        body, name=name, grid_spec=grid_spec, out_shape=tuple(pltpu.HBM(d.shape, F32) for d in dests),
        input_output_aliases={1 + k + w: w for w in range(k)},
        compiler_params=_params(blk, ("parallel",)))(core, *bs, *dests)


_HBM = pl.BlockSpec(memory_space=pltpu.HBM)


def _place():
    x, y, c = lax.axis_index("x"), lax.axis_index("y"), lax.axis_index("c")
    chips = [(1 - x, y), (x, 1 - y), (1 - x, 1 - y)]
    return x, y, c, chips


class GatherRider:
    def __init__(self, shards):
        self.arrays = list(shards)
        self.n = n = len(shards)
        self.out_shape = tuple(pltpu.HBM((N_CHIPS,) + s.shape, s.dtype) for s in shards)
        self.scratch = [pltpu.SemaphoreType.DMA((n, 3))] * 4 + [pltpu.SemaphoreType.DMA((n,))]

    def _copies(self, ins, outs, sems):
        send_ici, recv_ici, send_d2d, recv_d2d, local = sems
        x, y, c, chips = _place()
        me = 2 * x + y

        def own(w):
            return pltpu.make_async_copy(ins[w], outs[w].at[me], local.at[w])

        def ici(w, j):
            px, py = chips[j]
            return pltpu.make_async_remote_copy(
                src_ref=ins[w].at[c], dst_ref=outs[w].at[me, c], send_sem=send_ici.at[w, j],
                recv_sem=recv_ici.at[w, j], device_id=(px, py, c), device_id_type=MESH)

        def landed(w, j, half):
            px, py = chips[j]
            return outs[w].at[2 * px + py, half]

        def d2d(w, j, half):
            return pltpu.make_async_remote_copy(
                src_ref=landed(w, j, half), dst_ref=landed(w, j, half), send_sem=send_d2d.at[w, j],
                recv_sem=recv_d2d.at[w, j], device_id=(x, y, 1 - c), device_id_type=MESH)

        def ici_arrival(w, j):
            return pltpu.make_async_remote_copy(
                src_ref=landed(w, j, c), dst_ref=landed(w, j, c), send_sem=send_ici.at[w, j],
                recv_sem=recv_ici.at[w, j], device_id=(x, y, c), device_id_type=MESH)

        return c, own, ici, d2d, ici_arrival

    def start(self, ins, outs, sems):
        c, own, ici, d2d, ici_arrival = self._copies(ins, outs, sems)
        for w in range(self.n):
            own(w).start()
            for j in range(3):
                ici(w, j).start()

    def finish(self, ins, outs, sems):
        c, own, ici, d2d, ici_arrival = self._copies(ins, outs, sems)
        for w in range(self.n):
            for j in range(3):
                ici_arrival(w, j).wait_recv()
                d2d(w, j, c).start()
        for w in range(self.n):
            for j in range(3):
                d2d(w, j, 1 - c).wait_recv()
        for w in range(self.n):
            for j in range(3):
                ici(w, j).wait_send()
                d2d(w, j, c).wait_send()
            own(w).wait()


class ScatterRider:
    def __init__(self, parts):
        self.arrays = list(parts)
        self.n = n = len(parts)
        self.out_shape = tuple(pltpu.HBM(p.shape, p.dtype) for p in parts)
        self.scratch = [pltpu.SemaphoreType.DMA((n, 3))] * 2 + [pltpu.SemaphoreType.DMA((n,))]

    def _copies(self, ins, outs, sems):
        send, recv, local = sems
        x, y, c, chips = _place()
        me = 2 * x + y

        def own(w):
            return pltpu.make_async_copy(ins[w].at[me], outs[w].at[me], local.at[w])

        def copy(w, j):
            px, py = chips[j]
            return pltpu.make_async_remote_copy(
                src_ref=ins[w].at[2 * px + py], dst_ref=outs[w].at[me], send_sem=send.at[w, j],
                recv_sem=recv.at[w, j], device_id=(px, py, c), device_id_type=MESH)

        def arrival(w, j):
            px, py = chips[j]
            blk = outs[w].at[2 * px + py]
            return pltpu.make_async_remote_copy(
                src_ref=blk, dst_ref=blk, send_sem=send.at[w, j], recv_sem=recv.at[w, j],
                device_id=(x, y, c), device_id_type=MESH)

        return own, copy, arrival

    def start(self, ins, outs, sems):
        own, copy, arrival = self._copies(ins, outs, sems)
        for w in range(self.n):
            own(w).start()
            for j in range(3):
                copy(w, j).start()

    def finish(self, ins, outs, sems):
        own, copy, arrival = self._copies(ins, outs, sems)
        for w in range(self.n):
            for j in range(3):
                arrival(w, j).wait_recv()
        for w in range(self.n):
            for j in range(3):
                copy(w, j).wait_send()
            own(w).wait()


def _carry(rider, body, n_in, n_out, first, last):
    if rider is None:
        return body
    k, m = rider.n, len(rider.scratch)

    def carried(*refs):
        ins, r_in = refs[:n_in], refs[n_in:n_in + k]
        outs, r_out = refs[n_in + k:n_in + k + n_out], refs[n_in + k + n_out:n_in + 2 * k + n_out]
        rest = refs[n_in + 2 * k + n_out:]
        scratch, sems = rest[:len(rest) - m], rest[len(rest) - m:]

        @pl.when(first())
        def _():
            rider.start(r_in, r_out, sems)

        body(*ins, *outs, *scratch)

        @pl.when(last())
        def _():
            rider.finish(r_in, r_out, sems)

    return carried


def _carry_specs(rider, in_specs, out_specs, out_shape, scratch):
    if rider is None:
        return list(in_specs), tuple(out_specs), tuple(out_shape), list(scratch)
    k = rider.n
    return (list(in_specs) + [_HBM] * k, tuple(out_specs) + (_HBM,) * k, tuple(out_shape) + rider.out_shape,
            list(scratch) + list(rider.scratch))


def run_rider(rider, *, name):
    k = rider.n

    def body(*refs):
        rider.start(refs[:k], refs[k:2 * k], refs[2 * k:])
        rider.finish(refs[:k], refs[k:2 * k], refs[2 * k:])

    return _pallas(body, name=name, out_shape=rider.out_shape, in_specs=[_HBM] * k, out_specs=(_HBM,) * k,
                   scratch_shapes=rider.scratch)(*rider.arrays)


def allgather_small(shards, *, name):
    n = len(shards)

    def body(*refs):
        ins, outs = refs[:n], refs[n:2 * n]
        send, recv, local = refs[2 * n:]
        x, y, c, chips = _place()
        me = 2 * x + y
        locals_ = [pltpu.make_async_copy(ins[w], outs[w].at[me], local.at[w]) for w in range(n)]
        for cp in locals_:
            cp.start()

        def copy(w, j):
            px, py = chips[j]
            return pltpu.make_async_remote_copy(
                src_ref=ins[w], dst_ref=outs[w].at[me], send_sem=send.at[w, j], recv_sem=recv.at[w, j],
                device_id=(px, py, c), device_id_type=MESH)

        def arrival(w, j):
            px, py = chips[j]
            blk = outs[w].at[2 * px + py]
            return pltpu.make_async_remote_copy(
                src_ref=blk, dst_ref=blk, send_sem=send.at[w, j], recv_sem=recv.at[w, j],
                device_id=(x, y, c), device_id_type=MESH)

        for w in range(n):
            for j in range(3):
                copy(w, j).start()
        for w in range(n):
            for j in range(3):
                arrival(w, j).wait_recv()
        for w in range(n):
            for j in range(3):
                copy(w, j).wait_send()
        for cp in locals_:
            cp.wait()

    out_shape = tuple(pltpu.HBM((N_CHIPS,) + s.shape, s.dtype) for s in shards)
    return _pallas(
        body, name=name, out_shape=out_shape, in_specs=[_HBM] * n, out_specs=(_HBM,) * n,
        scratch_shapes=[pltpu.SemaphoreType.DMA((n, 3))] * 2 + [pltpu.SemaphoreType.DMA((n,))],
    )(*shards)


class SwapRider:
    def __init__(self, grads):
        self.arrays = list(grads)
        self.n = n = len(grads)
        self.out_shape = tuple(pltpu.HBM((N_CHIPS,) + g.shape[2:], g.dtype) for g in grads)
        self.scratch = [pltpu.SemaphoreType.DMA((n,))] * 2

    def _copies(self, ins, outs, sems):
        send, recv = sems
        x, y, c, _ = _place()
        return [pltpu.make_async_remote_copy(
            src_ref=ins[w].at[:, 1 - c], dst_ref=outs[w], send_sem=send.at[w], recv_sem=recv.at[w],
            device_id=(x, y, 1 - c), device_id_type=MESH) for w in range(self.n)]

    def start(self, ins, outs, sems):
        for cp in self._copies(ins, outs, sems):
            cp.start()

    def finish(self, ins, outs, sems):
        copies = self._copies(ins, outs, sems)
        for cp in copies:
            cp.wait_recv()
        for cp in copies:
            cp.wait_send()


def rs_sibling_share(stacked, *, name):
    n = len(stacked)

    def body(*refs):
        bufs = refs[n:2 * n]
        send, recv = refs[2 * n:]
        x, y, c, _ = _place()
        shares, arrivals = [], []
        for w in range(n):
            mine, other = bufs[w].at[:, c], bufs[w].at[:, 1 - c]
            shares.append(pltpu.make_async_remote_copy(
                src_ref=mine, dst_ref=mine, send_sem=send.at[w], recv_sem=recv.at[w],
                device_id=(x, y, 1 - c), device_id_type=MESH))
            arrivals.append(pltpu.make_async_remote_copy(
                src_ref=other, dst_ref=other, send_sem=send.at[w], recv_sem=recv.at[w],
                device_id=(x, y, c), device_id_type=MESH))
        for cp in shares:
            cp.start()
        for cp in arrivals:
            cp.wait_recv()
        for cp in shares:
            cp.wait_send()

    out_shape = tuple(pltpu.HBM(s.shape, F32) for s in stacked)
    return _pallas(
        body, name=name, out_shape=out_shape, in_specs=[_HBM] * n, out_specs=(_HBM,) * n,
        input_output_aliases={w: w for w in range(n)},
        scratch_shapes=[pltpu.SemaphoreType.DMA((n,))] * 2,
    )(*stacked)


def allreduce_small(v, *, name):
    R, C = v.shape

    def body(v_ref, o_ref, land, send, recv):
        x, y, c, _ = _place()
        me = 4 * x + 2 * y + c
        land[me] = v_ref[...]

        def flip(k):
            return (1 - x) if k & 4 else x, (1 - y) if k & 2 else y, (1 - c) if k & 1 else c

        copies = []
        for k in range(1, N_DEV):
            px, py, pc = flip(k)
            copies.append(pltpu.make_async_remote_copy(
                src_ref=v_ref, dst_ref=land.at[me], send_sem=send.at[k - 1], recv_sem=recv.at[k - 1],
                device_id=(px, py, pc), device_id_type=MESH))
        for cp in copies:
            cp.start()
        for k in range(1, N_DEV):
            px, py, pc = flip(k)
            blk = land.at[4 * px + 2 * py + pc]
            pltpu.make_async_remote_copy(
                src_ref=blk, dst_ref=blk, send_sem=send.at[k - 1], recv_sem=recv.at[k - 1],
                device_id=(x, y, c), device_id_type=MESH).wait_recv()
        for cp in copies:
            cp.wait_send()
        acc = land[0]
        for d in range(1, N_DEV):
            acc = acc + land[d]
        o_ref[...] = acc

    vm = pl.BlockSpec(memory_space=pltpu.VMEM)
    return pl.pallas_call(
        body, name=name, out_shape=jax.ShapeDtypeStruct((R, C), F32), in_specs=[vm], out_specs=vm,
        scratch_shapes=[pltpu.VMEM((N_DEV, R, C), F32), pltpu.SemaphoreType.DMA((N_DEV - 1,)),
                        pltpu.SemaphoreType.DMA((N_DEV - 1,))],
        compiler_params=pltpu.CompilerParams(vmem_limit_bytes=int(min(12 * R * C * 4 + (8 << 20), VMEM_CAP))),
    )(v)


def _pack(arrays):
    flat = jnp.concatenate([a.reshape(-1) for a in arrays])
    return flat.reshape(-1, LANES)


def _unpack(packed, shapes):
    flat = packed.reshape(-1)
    out, off = [], 0
    for s in shapes:
        n = 1
        for d in s:
            n *= d
        out.append(flat[off:off + n].reshape(s))
        off += n
    return out


def _row_tile(rows, cap=512):
    t = 1 << (cap.bit_length() - 1)
    while rows % t:
        t //= 2
    return t


def _adamw_tile(rows, cols):
    return _row_tile(rows, max(8, (1 << 20) // (4 * cols)))


def kernel(x, mem, w_in, conv_w, conv_b, conv_ln_g, conv_ln_b, w_out, ln1_g, ln1_b, mem_wq, mem_wk, mem_wv, mem_wo, ln2_g, ln2_b, ffn_up, ffn_conv_w, ffn_conv_b, ffn_down, ln3_g, ln3_b, loss_target, m_w_in, m_conv_w, m_conv_b, m_conv_ln_g, m_conv_ln_b, m_w_out, m_ln1_g, m_ln1_b, m_mem_wq, m_mem_wk, m_mem_wv, m_mem_wo, m_ln2_g, m_ln2_b, m_ffn_up, m_ffn_conv_w, m_ffn_conv_b, m_ffn_down, m_ln3_g, m_ln3_b, v_w_in, v_conv_w, v_conv_b, v_conv_ln_g, v_conv_ln_b, v_w_out, v_ln1_g, v_ln1_b, v_mem_wq, v_mem_wk, v_mem_wv, v_mem_wo, v_ln2_g, v_ln2_b, v_ffn_up, v_ffn_conv_w, v_ffn_conv_b, v_ffn_down, v_ln3_g, v_ln3_b):
    W = dict(w_in=w_in, conv_w=conv_w, conv_b=conv_b, conv_ln_g=conv_ln_g, conv_ln_b=conv_ln_b, w_out=w_out,
             ln1_g=ln1_g, ln1_b=ln1_b, mem_wq=mem_wq, mem_wk=mem_wk, mem_wv=mem_wv, mem_wo=mem_wo, ln2_g=ln2_g,
             ln2_b=ln2_b, ffn_up=ffn_up, ffn_conv_w=ffn_conv_w, ffn_conv_b=ffn_conv_b, ffn_down=ffn_down,
             ln3_g=ln3_g, ln3_b=ln3_b)
    M1 = dict(w_in=m_w_in, conv_w=m_conv_w, conv_b=m_conv_b, conv_ln_g=m_conv_ln_g, conv_ln_b=m_conv_ln_b,
              w_out=m_w_out, ln1_g=m_ln1_g, ln1_b=m_ln1_b, mem_wq=m_mem_wq, mem_wk=m_mem_wk, mem_wv=m_mem_wv,
              mem_wo=m_mem_wo, ln2_g=m_ln2_g, ln2_b=m_ln2_b, ffn_up=m_ffn_up, ffn_conv_w=m_ffn_conv_w,
              ffn_conv_b=m_ffn_conv_b, ffn_down=m_ffn_down, ln3_g=m_ln3_g, ln3_b=m_ln3_b)
    V2 = dict(w_in=v_w_in, conv_w=v_conv_w, conv_b=v_conv_b, conv_ln_g=v_conv_ln_g, conv_ln_b=v_conv_ln_b,
              w_out=v_w_out, ln1_g=v_ln1_g, ln1_b=v_ln1_b, mem_wq=v_mem_wq, mem_wk=v_mem_wk, mem_wv=v_mem_wv,
              mem_wo=v_mem_wo, ln2_g=v_ln2_g, ln2_b=v_ln2_b, ffn_up=v_ffn_up, ffn_conv_w=v_ffn_conv_w,
              ffn_conv_b=v_ffn_conv_b, ffn_down=v_ffn_down, ln3_g=v_ln3_g, ln3_b=v_ln3_b)

    L = w_in.shape[0]
    S, D = x.shape[1], x.shape[2]
    C = conv_b.shape[1]
    alpha = (2.0 * L) ** 0.25
    chip = 2 * lax.axis_index("x") + lax.axis_index("y")
    xs, mems, tgt = x[0], mem[0], loss_target[0]
    mem_bf = mems.astype(BF16)
    tm = _row_tile(S)
    tm_ffn = _row_tile(S, 256)
    tm_big = _row_tile(S, 1024)

    def shards_of(l, names):
        out = []
        for n in names:
            wl = W[n][l].astype(BF16)
            out.append(wl.reshape(2, wl.shape[0] // 2, wl.shape[1]))
        return out

    def gathered(names, got):
        layer = {}
        for n, g in zip(names, got):
            rows, cols = W[n].shape[1], W[n].shape[2]
            layer[n] = g.reshape(N_CHIPS, rows, cols) if n in COL_SHARDED else g.reshape(N_CHIPS * rows, cols)
        return layer

    full = [dict() for _ in range(L)]
    full[0].update(gathered(RIDE_IN, run_rider(GatherRider(shards_of(0, RIDE_IN)), name="allgather_w_in")))
    cw_all, fcw_all = allgather_small([conv_w, ffn_conv_w], name="allgather_small")
    cw_full = jnp.transpose(cw_all, (1, 2, 0, 3)).reshape(L, conv_w.shape[1], -1)
    fcw_full = jnp.transpose(fcw_all, (1, 2, 0, 3)).reshape(L, ffn_conv_w.shape[1], -1)

    saved = []
    h, hb = xs, xs.astype(BF16)
    for l in range(L):
        fw = full[l]
        s = dict(x=h, xb=hb)
        s['proj'] = mm_nn(hb, fw['w_in'], F32, tm=min(1024, S), tn=fw['w_in'].shape[2], name="proj")
        on_conv = RIDE_ATT if l == 0 else RIDE_FFN[1:]
        on_sb = RIDE_FFN if l == 0 else RIDE_FFN[:1]
        s['u1'], got = conv_fwd(s['proj'], cw_full[l], conv_b[l][None], name="conv_fwd",
                                rider=GatherRider(shards_of(l, on_conv)))
        fw.update(gathered(on_conv, got))
        more = l + 1 < L
        s['o_sb'], s['ltot'], got = sb_fwd(
            s['proj'], q_col=2 * C, name="sb_fwd", rider=GatherRider(shards_of(l, on_sb)))
        fw.update(gathered(on_sb, got))
        s['ua'] = ln_silu(s['u1'], s['o_sb'], conv_ln_g[l][None], conv_ln_b[l][None], tm=tm, name="ln_silu")
        s['x1'], s['x1b'], s['zh1'], s['rs1'] = mm_ln(
            s['ua'], fw['w_out'], h, ln1_g[l][None], ln1_b[l][None], alpha, tm=tm, name="out_proj_ln")
        s['q2'] = mm_nn(s['x1b'], fw['mem_wq'], BF16, tm=min(1024, S), tn=512, name="mem_q")
        s['k2'] = mm_nn(mem_bf, fw['mem_wk'], BF16, tm=mem_bf.shape[0], tn=512, name="mem_kv")
        s['v2'] = mm_nn(mem_bf, fw['mem_wv'], BF16, tm=mem_bf.shape[0], tn=512, name="mem_kv")
        s['o2'] = xattn_fwd(s['q2'], s['k2'], s['v2'], tm=tm, name="xattn_fwd")
        s['x2'], s['x2b'], s['zh2'], s['rs2'] = mm_ln(
            s['o2'], fw['mem_wo'], s['x1'], ln2_g[l][None], ln2_b[l][None], alpha, tm=tm, name="mem_o_ln")
        (s['upv'], s['upg'], s['cv'], s['cg'], s['hmid']), got = ffn_up_fwd(
            s['x2b'], fw['ffn_up'], fcw_full[l], ffn_conv_b[l][None], tm=tm_ffn, tn=fw['ffn_up'].shape[2],
            name="ffn_up_fwd", rider=GatherRider(shards_of(l + 1, RIDE_ATT + RIDE_IN)) if more else None)
        if more:
            full[l + 1].update(gathered(RIDE_ATT + RIDE_IN, got))
        h, hb, s['zh3'], s['rs3'] = mm_ln(
            s['hmid'], fw['ffn_down'], s['x2'], ln3_g[l][None], ln3_b[l][None], alpha, tm=tm, name="ffn_down_ln")
        saved.append(s)

    dx, loss_part = loss_head(h, tgt, tm=tm, name="loss_head")
    loss = lax.psum(loss_part[0, 0], ("x", "y", "c"))

    core = lax.axis_index("c").astype(jnp.int32).reshape(1)
    reduced_big = {n: lax.empty((L, 2, W[n].shape[1] // 2, W[n].shape[2]), F32) for n in BIG}
    small_grads = [None] * L

    def row_halves(g, names):
        parts = []
        for n in names:
            rows, cols = W[n].shape[1], W[n].shape[2]
            parts.append(g[n].reshape(N_CHIPS, 2, rows // 2, cols))
        return parts

    def pre_add(g, names):
        parts = row_halves(g, names)
        got = run_rider(SwapRider(parts), name="rs_sibling_swap")
        return list(add_pairs(parts, got, core, name="rs_add_pairs"))

    def reduce_into(names, scattered, layer):
        reduced_big.update(zip(names, sum_chips_into(
            list(scattered), [reduced_big[n] for n in names], layer, core, name="rs_sum_chips")))

    pending = None
    for l in reversed(range(L)):
        fw, s = full[l], saved[l]
        g = {}
        if l == L - 1:
            top = ln_bwd(dx, s['zh3'], s['rs3'], ln3_g[l][None], tm=tm, name="ln_bwd")
        dz3, dz3b, g['ln3_g'], g['ln3_b'] = top
        ftn = fw['ffn_up'].shape[2]
        (dupv, dupg, dfw_v, dfw_g, dfb_v, dfb_g), sc = ffn_mid_bwd(
            dz3b, fw['ffn_down'], s['upv'], s['upg'], s['cv'], s['cg'], fcw_full[l], tm=tm_ffn, tn=ftn,
            name="ffn_mid_bwd", rider=ScatterRider(pending) if pending else None)
        if pending:
            reduce_into(RIDE_MIX, sc, l + 1)
        g['ffn_conv_w'] = jnp.concatenate([dfw_v, dfw_g], axis=1)
        g['ffn_conv_b'] = jnp.concatenate([dfb_v, dfb_g], axis=1)[0]
        g['ffn_down'] = mm_tn(s['hmid'], [dz3b], tk=ftn, tn=512, tmc=min(1024, S), name="grad_ffn_down")
        dz2, dz2b, g['ln2_g'], g['ln2_b'] = mm_nt_ln_bwd(
            [dupv, dupg], fw['ffn_up'], dz3, alpha, s['zh2'], s['rs2'], ln2_g[l][None], tm=tm_ffn,
            name="ffn_up_bwd")
        g['ffn_up'] = mm_tn(s['x2b'], [dupv, dupg], tk=512, tn=ftn, shard_width=ftn, tmc=min(1024, S),
                            name="grad_ffn_up")

        do2 = mm_nt([dz2b], fw['mem_wo'], BF16, tm=tm_big, tk=512, name="mem_o_bwd")
        g['mem_wo'] = mm_tn(s['o2'], [dz2b], tk=512, tn=512, name="grad_sq")
        dq2, dk2, dv2 = xattn_bwd(s['q2'], do2, s['k2'], s['v2'], tm=tm, name="xattn_bwd")
        dz1, dz1b, g['ln1_g'], g['ln1_b'] = mm_nt_ln_bwd(
            [dq2], fw['mem_wq'], dz2, alpha, s['zh1'], s['rs1'], ln1_g[l][None], tm=tm, name="mem_q_bwd")
        g['mem_wq'] = mm_tn(s['x1b'], [dq2], tk=512, tn=512, name="grad_sq")
        g['mem_wk'] = mm_tn(mem_bf, [dk2], tk=512, tn=512, name="grad_mem_kv")
        g['mem_wv'] = mm_tn(mem_bf, [dv2], tk=512, tn=512, name="grad_mem_kv")

        rest = row_halves(g, RIDE_REST)
        dua, got = mm_nt([dz1b], fw['w_out'], F32, tm=tm_big, tk=512, name="out_proj_bwd", rider=SwapRider(rest))
        rest = list(add_pairs(rest, got, core, name="rs_add_pairs"))
        g['w_out'] = mm_tn(s['ua'], [dz1b], tk=512, tn=512, name="grad_sq")
        dq, dk, dv, sc = sb_bwd(
            s['proj'], s['ltot'], dua, q_col=2 * C, do_col=C, name="sb_bwd",
            rider=ScatterRider(rest[:-1]))
        reduce_into(RIDE_REST[:-1], sc, l)
        du1, g['conv_ln_g'], g['conv_ln_b'] = ln_silu_bwd(
            dua, s['u1'], conv_ln_g[l][None], conv_ln_b[l][None], tm=tm, name="ln_silu_bwd")
        (da, dg, g['conv_w'], dcb), sc = conv_bwd(du1, s['proj'], cw_full[l], name="conv_bwd",
                                                  rider=ScatterRider(rest[-1:]))
        reduce_into(RIDE_REST[-1:], sc, l)
        g['conv_b'] = dcb
        dproj = jnp.concatenate([da, dg, dq, dk, dv], axis=1)
        ns_in = fw['w_in'].shape[2]
        if l > 0:
            below = saved[l - 1]
            top = mm_nt_ln_bwd([dproj], fw['w_in'], dz1, alpha, below['zh3'], below['rs3'], ln3_g[l - 1][None],
                               tm=tm, name="proj_bwd")
        else:
            dx = mm_nt([dproj], fw['w_in'], F32, tm=tm_big, tk=512, res=dz1, alpha=alpha, name="proj_bwd_x")
        g['w_in'] = mm_tn(s['xb'], [dproj], tk=512, tn=ns_in, shard_width=ns_in, name="grad_w_in")

        pending = pre_add(g, RIDE_MIX)
        small_grads[l] = {n: g[n].reshape(W[n].shape[1:-1] + (-1,)) for n in SMALL}

    grad_x = dx[None]

    reduce_into(RIDE_MIX, run_rider(ScatterRider(pending), name="rs_chip_scatter"), 0)
    shared = rs_sibling_share([reduced_big[n] for n in BIG], name="rs_sibling_share")
    G = {}
    for n, sh in zip(BIG, shared):
        G[n] = sh.reshape(W[n].shape)

    small_full_shapes = []
    small_stack = []
    for n in SMALL:
        st = jnp.stack([small_grads[l][n] for l in range(L)])
        small_stack.append(st)
        small_full_shapes.append(st.shape)
    reduced = _unpack(allreduce_small(_pack(small_stack), name="allreduce_small"), small_full_shapes)
    for n, r in zip(SMALL, reduced):
        if n in SMALL_SHARDED:
            width = W[n].shape[-1]
            r = lax.dynamic_slice_in_dim(r, chip * width, width, axis=2)
        G[n] = r

    out_g, out_d, out_m, out_v = {}, {}, {}, {}
    for n in BIG:
        shp = W[n].shape
        flat = lambda a: a.reshape(shp[0] * shp[1], shp[2])
        res = adamw(flat(W[n]), flat(G[n]), flat(M1[n]), flat(V2[n]), tr=_adamw_tile(shp[0] * shp[1], shp[2]), name="adamw")
        out_g[n], out_d[n], out_m[n], out_v[n] = [r.reshape(shp) for r in res]
    small_shapes = [W[n].shape for n in SMALL]
    packed = [_pack([d[n] for n in SMALL]) for d in (W, G, M1, V2)]
    res = adamw(*packed, tr=packed[0].shape[0], name="adamw_small")
    for d, r in zip((out_g, out_d, out_m, out_v), res):
        for n, a in zip(SMALL, _unpack(r, small_shapes)):
            d[n] = a

    return (loss, grad_x, *[out_g[n] for n in WEIGHTS], *[out_d[n] for n in WEIGHTS],
            *[out_m[n] for n in WEIGHTS], *[out_v[n] for n in WEIGHTS])
```

```python
import functools

import jax
import jax.numpy as jnp
from jax import lax
from jax.experimental import pallas as pl
from jax.experimental.pallas import tpu as pltpu

F32 = jnp.float32
BF16 = jnp.bfloat16
MESH = pl.DeviceIdType.MESH

LN_EPS = 1e-5
SB_HEADS = 8
MEM_HEADS = 4
ADAM_LR, ADAM_B1, ADAM_B2, ADAM_EPS, ADAM_WD, ADAM_STEP = 0.001, 0.9, 0.999, 1e-08, 0.01, 10

LANES = 128
V7X_VMEM_BYTES = 64 << 20
VMEM_CAP = V7X_VMEM_BYTES - (6 << 20)
N_CHIPS = 4
N_DEV = 8

BIG = ('w_in', 'w_out', 'mem_wq', 'mem_wk', 'mem_wv', 'mem_wo', 'ffn_up', 'ffn_down')
RIDE_IN = ('w_in',)
RIDE_ATT = ('w_out', 'mem_wq', 'mem_wk', 'mem_wv', 'mem_wo')
RIDE_FFN = ('ffn_up', 'ffn_down')
RIDE_MIX = ('w_in', 'w_out')
RIDE_REST = ('mem_wq', 'mem_wk', 'mem_wv', 'mem_wo', 'ffn_up', 'ffn_down')
COL_SHARDED = ('w_in', 'ffn_up')
SMALL = ('conv_w', 'conv_b', 'conv_ln_g', 'conv_ln_b', 'ln1_g', 'ln1_b', 'ln2_g', 'ln2_b',
         'ffn_conv_w', 'ffn_conv_b', 'ln3_g', 'ln3_b')
SMALL_SHARDED = ('conv_w', 'ffn_conv_w')
WEIGHTS = ('w_in', 'conv_w', 'conv_b', 'conv_ln_g', 'conv_ln_b', 'w_out', 'ln1_g', 'ln1_b',
           'mem_wq', 'mem_wk', 'mem_wv', 'mem_wo', 'ln2_g', 'ln2_b', 'ffn_up', 'ffn_conv_w',
           'ffn_conv_b', 'ffn_down', 'ln3_g', 'ln3_b')


def _params(block_bytes, semantics=None, **kw):
    limit = int(min(max(2 * block_bytes + (8 << 20), 32 << 20), VMEM_CAP))
    return pltpu.CompilerParams(dimension_semantics=semantics, vmem_limit_bytes=limit, **kw)


def _pallas(body, **kw):
    call = pl.pallas_call(body, **kw)

    def run(*args):
        return call(*[pltpu.with_memory_space_constraint(a, pltpu.HBM)
                      if jnp.issubdtype(a.dtype, jnp.floating) else a for a in args])

    return run


def _nbytes(shape, dtype):
    n = 1
    for s in shape:
        n *= s
    return n * jnp.dtype(dtype).itemsize


def _dot(a, b):
    return jnp.dot(a, b, preferred_element_type=F32)


def _dot_nt(a, b):
    return lax.dot_general(a, b, (((1,), (1,)), ((), ())), preferred_element_type=F32)


def _dot_tn(a, b):
    return lax.dot_general(a, b, (((0,), (0,)), ((), ())), preferred_element_type=F32)


def _sigmoid(x):
    return 1.0 / (1.0 + jnp.exp(-x))


def mm_nn(a, b, out_dtype, *, tm, tn, name):
    M, K = a.shape
    sharded = b.ndim == 3
    if sharded:
        nsh, _, ns = b.shape
        N, per = nsh * ns, ns // tn
        b_spec = pl.BlockSpec((None, K, tn), lambda i, j: (j // per, 0, j % per))
    else:
        N = b.shape[1]
        b_spec = pl.BlockSpec((K, tn), lambda i, j: (0, j))

    def body(a_ref, b_ref, o_ref):
        o_ref[...] = _dot(a_ref[...].astype(BF16), b_ref[...]).astype(o_ref.dtype)

    blk = _nbytes((tm, K), a.dtype) + _nbytes((K, tn), BF16) + _nbytes((tm, tn), out_dtype)
    return _pallas(
        body, name=name, out_shape=pltpu.HBM((M, N), out_dtype), grid=(M // tm, N // tn),
        in_specs=[pl.BlockSpec((tm, K), lambda i, j: (i, 0)), b_spec],
        out_specs=pl.BlockSpec((tm, tn), lambda i, j: (i, j)),
        compiler_params=_params(blk, ("parallel", "parallel")))(a, b)


def proj_split(a, b, n_f32, *, tm, name):
    M, K = a.shape
    nsh, _, ns = b.shape
    N = nsh * ns

    def body(a_ref, b_ref, lo_ref, hi_ref):
        acc = _dot(a_ref[...], b_ref[...])
        j = pl.program_id(1)
        for s in range(nsh):
            c0, c1 = s * ns, (s + 1) * ns
            cut = min(max(n_f32 - c0, 0), ns)

            @pl.when(j == s)
            def _(c0=c0, c1=c1, cut=cut):
                if cut > 0:
                    lo_ref[:, c0:c0 + cut] = acc[:, 0:cut]
                if cut < ns:
                    hi_ref[:, c0 + cut - n_f32:c1 - n_f32] = acc[:, cut:ns].astype(BF16)

    blk = _nbytes((tm, K), BF16) + _nbytes((K, ns), BF16) + _nbytes((tm, N), F32)
    return _pallas(
        body, name=name, grid=(M // tm, nsh),
        out_shape=(pltpu.HBM((M, n_f32), F32), pltpu.HBM((M, N - n_f32), BF16)),
        in_specs=[pl.BlockSpec((tm, K), lambda i, j: (i, 0)), pl.BlockSpec((None, K, ns), lambda i, j: (j, 0, 0))],
        out_specs=(pl.BlockSpec((tm, n_f32), lambda i, j: (i, 0)), pl.BlockSpec((tm, N - n_f32), lambda i, j: (i, 0))),
        compiler_params=_params(blk, ("parallel", "arbitrary")))(a, b)


def mm_ln(a, b, x, gamma, beta, alpha, *, tm, name):
    M, K = a.shape
    D = b.shape[1]

    def body(a_ref, b_ref, x_ref, g_ref, be_ref, y_ref, yb_ref, zh_ref, rs_ref):
        z = alpha * x_ref[...] + _dot(a_ref[...], b_ref[...])
        mu = jnp.mean(z, axis=-1, keepdims=True)
        zc = z - mu
        rstd = lax.rsqrt(jnp.mean(zc * zc, axis=-1, keepdims=True) + LN_EPS)
        zh = zc * rstd
        y = zh * g_ref[...] + be_ref[...]
        y_ref[...] = y
        yb_ref[...] = y.astype(BF16)
        zh_ref[...] = zh
        rs_ref[...] = rstd

    row = lambda i: (i, 0)
    fix = lambda i: (0, 0)
    blk = _nbytes((tm, K), BF16) + _nbytes((K, D), BF16) + 4 * _nbytes((tm, D), F32)
    return _pallas(
        body, name=name, grid=(M // tm,),
        out_shape=(pltpu.HBM((M, D), F32), pltpu.HBM((M, D), BF16),
                   pltpu.HBM((M, D), F32), pltpu.HBM((M, 1), F32)),
        in_specs=[pl.BlockSpec((tm, K), row), pl.BlockSpec((K, D), fix), pl.BlockSpec((tm, D), row),
                  pl.BlockSpec((1, D), fix), pl.BlockSpec((1, D), fix)],
        out_specs=(pl.BlockSpec((tm, D), row), pl.BlockSpec((tm, D), row), pl.BlockSpec((tm, D), row),
                   pl.BlockSpec((tm, 1), row)),
        compiler_params=_params(blk, ("parallel",)))(a, b, x, gamma, beta)


def ln_bwd(dy, zh, rstd, gamma, *, tm, name):
    M, D = dy.shape

    def body(dy_ref, zh_ref, rs_ref, g_ref, dz_ref, dzb_ref, dg_ref, db_ref):
        @pl.when(pl.program_id(0) == 0)
        def _():
            dg_ref[...] = jnp.zeros_like(dg_ref)
            db_ref[...] = jnp.zeros_like(db_ref)

        dyv, zhv = dy_ref[...], zh_ref[...]
        dg_ref[...] += jnp.sum(dyv * zhv, axis=0, keepdims=True)
        db_ref[...] += jnp.sum(dyv, axis=0, keepdims=True)
        dzh = dyv * g_ref[...]
        m1 = jnp.mean(dzh, axis=-1, keepdims=True)
        m2 = jnp.mean(dzh * zhv, axis=-1, keepdims=True)
        dz = rs_ref[...] * (dzh - m1 - zhv * m2)
        dz_ref[...] = dz
        dzb_ref[...] = dz.astype(BF16)

    row = lambda i: (i, 0)
    fix = lambda i: (0, 0)
    return _pallas(
        body, name=name, grid=(M // tm,),
        out_shape=(pltpu.HBM((M, D), F32), pltpu.HBM((M, D), BF16),
                   pltpu.HBM((1, D), F32), pltpu.HBM((1, D), F32)),
        in_specs=[pl.BlockSpec((tm, D), row), pl.BlockSpec((tm, D), row), pl.BlockSpec((tm, 1), row),
                  pl.BlockSpec((1, D), fix)],
        out_specs=(pl.BlockSpec((tm, D), row), pl.BlockSpec((tm, D), row), pl.BlockSpec((1, D), fix),
                   pl.BlockSpec((1, D), fix)),
        compiler_params=_params(4 * _nbytes((tm, D), F32), ("arbitrary",)))(dy, zh, rstd, gamma)


def mm_nt(a_list, b, out_dtype, *, tm, tk, name, res=None, alpha=None, rider=None):
    M = a_list[0].shape[0]
    widths = [a.shape[1] for a in a_list]
    sharded = b.ndim == 3
    if sharded:
        nsh, K, ns = b.shape
        b_spec = pl.BlockSpec((nsh, tk, ns), lambda i, j: (0, j, 0))
        for w in widths:
            assert w % ns == 0
    else:
        K, N = b.shape
        ns = None
        b_spec = pl.BlockSpec((tk, N), lambda i, j: (j, 0))
    n_a = len(a_list)

    def body(*refs):
        a_refs, b_ref = refs[:n_a], refs[n_a]
        o_ref = refs[-1]
        acc = None
        off = 0
        for a_ref, w in zip(a_refs, widths):
            if sharded:
                for p in range(w // ns):
                    t = _dot_nt(a_ref[:, p * ns:(p + 1) * ns].astype(BF16), b_ref[off // ns + p])
                    acc = t if acc is None else acc + t
            else:
                t = _dot_nt(a_ref[...].astype(BF16), b_ref[:, off:off + w])
                acc = t if acc is None else acc + t
            off += w
        if res is not None:
            acc = acc + alpha * refs[n_a + 1][...]
        o_ref[...] = acc.astype(o_ref.dtype)

    in_specs = [pl.BlockSpec((tm, w), lambda i, j: (i, 0)) for w in widths] + [b_spec]
    args = list(a_list) + [b]
    if res is not None:
        in_specs.append(pl.BlockSpec((tm, tk), lambda i, j: (i, j)))
        args.append(res)
    blk = (sum(_nbytes((tm, w), a.dtype) for a, w in zip(a_list, widths)) + _nbytes((tk, sum(widths)), BF16)
           + 2 * _nbytes((tm, tk), F32))
    in_specs, out_specs, out_shape, scratch = _carry_specs(
        rider, in_specs, (pl.BlockSpec((tm, tk), lambda i, j: (i, j)),), (pltpu.HBM((M, K), out_dtype),), [])
    first = lambda: (pl.program_id(0) == 0) & (pl.program_id(1) == 0)
    last = lambda: (pl.program_id(0) == M // tm - 1) & (pl.program_id(1) == K // tk - 1)
    res_all = _pallas(
        _carry(rider, body, len(args), 1, first, last), name=name, out_shape=out_shape, grid=(M // tm, K // tk),
        in_specs=in_specs, out_specs=out_specs, scratch_shapes=scratch,
        compiler_params=_params(blk, ("arbitrary", "arbitrary")))(*args, *(rider.arrays if rider else ()))
    return res_all[0] if rider is None else (res_all[0], list(res_all[1:]))


def mm_nt_ln_bwd(a_list, b, res, alpha, zh, rstd, gamma, *, tm, name):
    M, D = res.shape
    widths = [a.shape[1] for a in a_list]
    sharded = b.ndim == 3
    if sharded:
        nsh, _, ns = b.shape
        b_spec = pl.BlockSpec((nsh, D, ns), lambda i: (0, 0, 0))
    else:
        ns = None
        b_spec = pl.BlockSpec((D, b.shape[1]), lambda i: (0, 0))
    n_a = len(a_list)

    def body(*refs):
        a_refs, b_ref = refs[:n_a], refs[n_a]
        res_ref, zh_ref, rs_ref, g_ref = refs[n_a + 1:n_a + 5]
        dz_ref, dzb_ref, dg_ref, db_ref = refs[n_a + 5:]

        @pl.when(pl.program_id(0) == 0)
        def _():
            dg_ref[...] = jnp.zeros_like(dg_ref)
            db_ref[...] = jnp.zeros_like(db_ref)

        dy = alpha * res_ref[...]
        off = 0
        for a_ref, w in zip(a_refs, widths):
            if sharded:
                for p in range(w // ns):
                    dy = dy + _dot_nt(a_ref[:, p * ns:(p + 1) * ns], b_ref[off // ns + p])
            else:
                dy = dy + _dot_nt(a_ref[...], b_ref[:, off:off + w])
            off += w
        zhv = zh_ref[...]
        dg_ref[...] += jnp.sum(dy * zhv, axis=0, keepdims=True)
        db_ref[...] += jnp.sum(dy, axis=0, keepdims=True)
        dzh = dy * g_ref[...]
        m1 = jnp.mean(dzh, axis=-1, keepdims=True)
        m2 = jnp.mean(dzh * zhv, axis=-1, keepdims=True)
        dz = rs_ref[...] * (dzh - m1 - zhv * m2)
        dz_ref[...] = dz
        dzb_ref[...] = dz.astype(BF16)

    row = lambda i: (i, 0)
    fix = lambda i: (0, 0)
    in_specs = [pl.BlockSpec((tm, w), row) for w in widths] + [
        b_spec, pl.BlockSpec((tm, D), row), pl.BlockSpec((tm, D), row), pl.BlockSpec((tm, 1), row),
        pl.BlockSpec((1, D), fix)]
    blk = (sum(_nbytes((tm, w), BF16) for w in widths) + _nbytes((D, sum(widths)), BF16)
           + 5 * _nbytes((tm, D), F32))
    return _pallas(
        body, name=name, grid=(M // tm,),
        out_shape=(pltpu.HBM((M, D), F32), pltpu.HBM((M, D), BF16), pltpu.HBM((1, D), F32),
                   pltpu.HBM((1, D), F32)),
        in_specs=in_specs,
        out_specs=(pl.BlockSpec((tm, D), row), pl.BlockSpec((tm, D), row), pl.BlockSpec((1, D), fix),
                   pl.BlockSpec((1, D), fix)),
        compiler_params=_params(blk, ("arbitrary",)))(*a_list, b, res, zh, rstd, gamma)


def mm_tn(a, b_list, *, tk, tn, name, shard_width=None, tmc=None):
    M, K = a.shape
    tmc = M if tmc is None else tmc
    nm = M // tmc
    widths = [b.shape[1] for b in b_list]
    N = sum(widths)
    starts, s = [], 0
    for w in widths:
        assert w % tn == 0
        starts.append(s)
        s += w // tn
    n_b = len(b_list)

    def body(*refs):
        a_ref, b_refs, o_ref, acc = refs[0], refs[1:1 + n_b], refs[-2], refs[-1]
        j, m = pl.program_id(1), pl.program_id(2)
        for b_ref, st, w in zip(b_refs, starts, widths):
            @pl.when((j >= st) & (j < st + w // tn))
            def _(b_ref=b_ref):
                t = _dot_tn(a_ref[...].astype(BF16), b_ref[...].astype(BF16))
                if nm == 1:
                    o_ref[...] = t.astype(o_ref.dtype)
                else:
                    @pl.when(m == 0)
                    def _():
                        acc[...] = t

                    @pl.when(m > 0)
                    def _():
                        acc[...] += t

                    @pl.when(m == nm - 1)
                    def _():
                        o_ref[...] = acc[...].astype(o_ref.dtype)

    def b_map(st, w):
        nb = w // tn
        return lambda i, j, m: (jnp.where((j >= st) & (j < st + nb), m, 0), jnp.clip(j - st, 0, nb - 1))

    in_specs = [pl.BlockSpec((tmc, tk), lambda i, j, m: (m, i))]
    in_specs += [pl.BlockSpec((tmc, tn), b_map(st, w)) for st, w in zip(starts, widths)]
    if shard_width is None:
        out_shape = pltpu.HBM((K, N), BF16)
        out_spec = pl.BlockSpec((tk, tn), lambda i, j, m: (i, j))
    else:
        per = shard_width // tn
        out_shape = pltpu.HBM((N // shard_width, K, shard_width), BF16)
        out_spec = pl.BlockSpec((None, tk, tn), lambda i, j, m: (j // per, i, j % per))
    acc_shape = (tk, tn) if nm > 1 else (8, LANES)
    blk = (_nbytes((tmc, tk), a.dtype) + n_b * _nbytes((tmc, tn), b_list[0].dtype) + 2 * _nbytes((tk, tn), F32))
    return _pallas(
        body, name=name, out_shape=out_shape, grid=(K // tk, N // tn, nm), in_specs=in_specs, out_specs=out_spec,
        scratch_shapes=[pltpu.VMEM(acc_shape, F32)],
        compiler_params=_params(blk, ("parallel", "arbitrary", "arbitrary")))(a, *b_list)


CONV_PAD = 32
CONV_CHUNK = 128


def _rows(win, off, n, shifts):
    b, a = off % 8, off // 8
    if b not in shifts:
        shifts[b] = win if b == 0 else win[b:b + n + CONV_PAD - 8, :]
    return shifts[b][8 * a:8 * a + n, :]


def _by_residue(n_taps, offset):
    return sorted(range(n_taps), key=lambda k: (offset(k) % 8, k))


def conv_fwd(proj, conv_w, conv_b, *, name, rider=None):
    S = proj.shape[0]
    KW, C = conv_w.shape
    nct = C // LANES
    rc = min(CONV_CHUNK, S)

    def body(a_ref, g_ref, w_ref, b_ref, o_ref, pad):
        pad[0:CONV_PAD, :] = jnp.zeros((CONV_PAD, LANES), F32)
        pad[CONV_PAD:, :] = a_ref[...] * _sigmoid(g_ref[...])
        w = w_ref[...]
        bias = b_ref[...]

        def chunk(i, _):
            base = pl.multiple_of(i * rc, rc)
            win = pad[pl.ds(base, rc + CONV_PAD), :]
            acc = jnp.zeros((rc, LANES), F32) + bias
            shifts = {}
            for k in _by_residue(KW, lambda k: CONV_PAD - (KW - 1) + k):
                acc = acc + w[k:k + 1, :] * _rows(win, CONV_PAD - (KW - 1) + k, rc, shifts)
            o_ref[pl.ds(base, rc), :] = acc
            return 0

        lax.fori_loop(0, S // rc, chunk, 0)

    in_specs, out_specs, out_shape, scratch = _carry_specs(
        rider, [pl.BlockSpec((S, LANES), lambda c: (0, c)), pl.BlockSpec((S, LANES), lambda c: (0, c + nct)),
                pl.BlockSpec((KW, LANES), lambda c: (0, c)), pl.BlockSpec((1, LANES), lambda c: (0, c))],
        (pl.BlockSpec((S, LANES), lambda c: (0, c)),), (pltpu.HBM((S, C), F32),),
        [pltpu.VMEM((S + CONV_PAD, LANES), F32)])
    first = lambda: pl.program_id(0) == 0
    last = lambda: pl.program_id(0) == nct - 1
    res = _pallas(
        _carry(rider, body, 4, 1, first, last), name=name, grid=(nct,), out_shape=out_shape,
        in_specs=in_specs, out_specs=out_specs, scratch_shapes=scratch,
        compiler_params=_params(4 * _nbytes((S, LANES), F32), ("arbitrary",)))(
            proj, proj, conv_w, conv_b, *(rider.arrays if rider else ()))
    return res[0], list(res[1:])


def conv_bwd(du1, proj, conv_w, *, name, rider=None):
    S = proj.shape[0]
    KW, C = conv_w.shape
    nct = C // LANES
    rc = min(CONV_CHUNK, S)

    def body(d_ref, a_ref, g_ref, w_ref, da_ref, dg_ref, dw_ref, db_ref, pad_u, pad_d, du0, dw_acc):
        dw_acc[...] = jnp.zeros_like(dw_acc)
        pad_u[0:CONV_PAD, :] = jnp.zeros((CONV_PAD, LANES), F32)
        pad_u[CONV_PAD:, :] = a_ref[...] * _sigmoid(g_ref[...])
        pad_d[0:S, :] = d_ref[...]
        pad_d[S:, :] = jnp.zeros((CONV_PAD, LANES), F32)
        w = w_ref[...]
        db_ref[...] = jnp.sum(d_ref[...], axis=0, keepdims=True)

        def chunk(i, _):
            base = pl.multiple_of(i * rc, rc)
            d = pad_d[pl.ds(base, rc), :]
            win_u = pad_u[pl.ds(base, rc + CONV_PAD), :]
            win_d = pad_d[pl.ds(base, rc + CONV_PAD), :]
            shifts = {}
            for k in _by_residue(KW, lambda k: CONV_PAD - (KW - 1) + k):
                u_k = _rows(win_u, CONV_PAD - (KW - 1) + k, rc, shifts)
                dw_acc[k:k + 1, :] += jnp.sum(d * u_k, axis=0, keepdims=True)
            acc = jnp.zeros((rc, LANES), F32)
            shifts = {}
            for k in _by_residue(KW, lambda k: KW - 1 - k):
                acc = acc + w[k:k + 1, :] * _rows(win_d, KW - 1 - k, rc, shifts)
            du0[pl.ds(base, rc), :] = acc
            return 0

        lax.fori_loop(0, S // rc, chunk, 0)
        dw_ref[...] = dw_acc[0:KW, :]
        a, sg = a_ref[...], _sigmoid(g_ref[...])
        d0 = du0[...]
        da_ref[...] = (d0 * sg).astype(BF16)
        dg_ref[...] = (d0 * a * sg * (1.0 - sg)).astype(BF16)

    col = lambda c: (0, c)
    in_specs, out_specs, out_shape, scratch = _carry_specs(
        rider, [pl.BlockSpec((S, LANES), col), pl.BlockSpec((S, LANES), col),
                pl.BlockSpec((S, LANES), lambda c: (0, c + nct)), pl.BlockSpec((KW, LANES), col)],
        (pl.BlockSpec((S, LANES), col), pl.BlockSpec((S, LANES), col), pl.BlockSpec((KW, LANES), col),
         pl.BlockSpec((1, LANES), col)),
        (pltpu.HBM((S, C), BF16), pltpu.HBM((S, C), BF16), pltpu.HBM((KW, C), F32), pltpu.HBM((1, C), F32)),
        [pltpu.VMEM((S + CONV_PAD, LANES), F32), pltpu.VMEM((S + CONV_PAD, LANES), F32),
         pltpu.VMEM((S, LANES), F32), pltpu.VMEM((CONV_PAD, LANES), F32)])
    first = lambda: pl.program_id(0) == 0
    last = lambda: pl.program_id(0) == nct - 1
    res = _pallas(
        _carry(rider, body, 4, 4, first, last), name=name, grid=(nct,), out_shape=out_shape,
        in_specs=in_specs, out_specs=out_specs, scratch_shapes=scratch,
        compiler_params=_params(8 * _nbytes((S, LANES), F32), ("arbitrary",)))(
            du1, proj, proj, conv_w, *(rider.arrays if rider else ()))
    return res[:4], list(res[4:])


def ln_silu(u1, o_sb, gamma, beta, *, tm, name):
    S, C = u1.shape

    def body(u_ref, o_ref, g_ref, b_ref, out_ref):
        z = u_ref[...]
        mu = jnp.mean(z, axis=-1, keepdims=True)
        zc = z - mu
        y = zc * lax.rsqrt(jnp.mean(zc * zc, axis=-1, keepdims=True) + LN_EPS) * g_ref[...] + b_ref[...]
        out_ref[:, 0:C] = (y * _sigmoid(y)).astype(BF16)
        out_ref[:, C:] = o_ref[...].astype(BF16)

    row = lambda i: (i, 0)
    fix = lambda i: (0, 0)
    return _pallas(
        body, name=name, out_shape=pltpu.HBM((S, 2 * C), BF16), grid=(S // tm,),
        in_specs=[pl.BlockSpec((tm, C), row), pl.BlockSpec((tm, C), row), pl.BlockSpec((1, C), fix),
                  pl.BlockSpec((1, C), fix)],
        out_specs=pl.BlockSpec((tm, 2 * C), row),
        compiler_params=_params(4 * _nbytes((tm, C), F32), ("parallel",)))(u1, o_sb, gamma, beta)


def ln_silu_bwd(dua, u1, gamma, beta, *, tm, name):
    S, C = u1.shape

    def body(d_ref, u_ref, g_ref, b_ref, du1_ref, dg_ref, db_ref):
        @pl.when(pl.program_id(0) == 0)
        def _():
            dg_ref[...] = jnp.zeros_like(dg_ref)
            db_ref[...] = jnp.zeros_like(db_ref)

        z = u_ref[...]
        mu = jnp.mean(z, axis=-1, keepdims=True)
        zc = z - mu
        rstd = lax.rsqrt(jnp.mean(zc * zc, axis=-1, keepdims=True) + LN_EPS)
        zh = zc * rstd
        y = zh * g_ref[...] + b_ref[...]
        sg = _sigmoid(y)
        dy = d_ref[...] * (sg * (1.0 + y * (1.0 - sg)))
        dg_ref[...] += jnp.sum(dy * zh, axis=0, keepdims=True)
        db_ref[...] += jnp.sum(dy, axis=0, keepdims=True)
        dzh = dy * g_ref[...]
        m1 = jnp.mean(dzh, axis=-1, keepdims=True)
        m2 = jnp.mean(dzh * zh, axis=-1, keepdims=True)
        du1_ref[...] = rstd * (dzh - m1 - zh * m2)

    row = lambda i: (i, 0)
    fix = lambda i: (0, 0)
    return _pallas(
        body, name=name, grid=(S // tm,),
        out_shape=(pltpu.HBM((S, C), F32), pltpu.HBM((1, C), F32),
                   pltpu.HBM((1, C), F32)),
        in_specs=[pl.BlockSpec((tm, C), row), pl.BlockSpec((tm, C), row), pl.BlockSpec((1, C), fix),
                  pl.BlockSpec((1, C), fix)],
        out_specs=(pl.BlockSpec((tm, C), row), pl.BlockSpec((1, C), fix), pl.BlockSpec((1, C), fix)),
        compiler_params=_params(4 * _nbytes((tm, C), F32), ("arbitrary",)))(dua, u1, gamma, beta)


SB_BLOCK = 256
SB_STOP = -105.0
SB_GROUP = 4


def _split_dot(x, tri):
    hi = x.astype(BF16)
    lo = (x - hi.astype(F32)).astype(BF16)
    return _dot(hi, tri) + _dot(lo, tri)


def _neg_softplus(z):
    return -(jnp.maximum(z, 0.0) + jnp.log(1.0 + jnp.exp(-jnp.abs(z))))


def sb_fwd(proj, *, q_col, name, rider=None):
    S = proj.shape[0]
    dh = LANES // 2
    W = SB_HEADS * dh
    BW = SB_GROUP * dh
    ngrp = W // BW
    T = min(SB_BLOCK, S)
    nblk = S // T
    scale = dh ** -0.5
    qb0 = q_col // BW
    heads = range(SB_GROUP)
    sl = [slice(h * dh, (h + 1) * dh) for h in heads]

    def body(q_ref, k_ref, v_ref, o_ref, l_ref, qs):
        r_i = lax.broadcasted_iota(jnp.int32, (T, T), 0)
        c_i = lax.broadcasted_iota(jnp.int32, (T, T), 1)
        tri = (r_i >= c_i).astype(BF16)
        vis = c_i < r_i
        lane = lax.broadcasted_iota(jnp.int32, (T, dh), 1)

        qs[...] = (q_ref[...] * scale).astype(BF16)

        def step(qb, blocks, st):
            nb = range(len(blocks))
            kb = [[k_ref[pl.ds(j0, T), sl[h]].astype(BF16) for h in heads] for j0, _ in blocks]
            vb = [[v_ref[pl.ds(j0, T), sl[h]].astype(BF16) for h in heads] for j0, _ in blocks]
            z = [[_dot_nt(qb[h], kb[b][h]) for h in heads] for b in nb]
            lk = [[_neg_softplus(z[b][h]) for h in heads] for b in nb]
            lk = [[jnp.where(vis, lk[b][h], 0.0) if blocks[b][1] else lk[b][h] for h in heads] for b in nb]
            C = [[_split_dot(lk[b][h], tri) for h in heads] for b in nb]
            R = [[st[2 * h + 1] for h in heads]]
            for b in nb:
                R.append([R[b][h] + C[b][h][:, 0:1] for h in heads])
            A = [[jnp.exp(z[b][h] + C[b][h] + R[b][h]) for h in heads] for b in nb]
            A = [[jnp.where(vis, A[b][h], 0.0) if blocks[b][1] else A[b][h] for h in heads] for b in nb]
            out = ()
            for h in heads:
                acc = st[2 * h]
                for b in nb:
                    acc = acc + _dot(A[b][h].astype(BF16), vb[b][h])
                out += (acc, R[-1][h])
            return out

        zero = (jnp.zeros((T, dh), F32), jnp.zeros((T, 1), F32))

        def finish(r0, i, c):
            walked = jnp.asarray(i - c[0]).astype(F32)
            for h in heads:
                o_ref[pl.ds(r0, T), sl[h]] = c[1 + 2 * h]
                l_ref[pl.ds(r0, T), sl[h]] = jnp.where(lane == 1, walked, c[2 + 2 * h])

        finish(0, 0, (-1,) + step([qs[0:T, sl[h]] for h in heads], [(0, True)], zero * SB_GROUP))

        def qblock(i, _):
            r0 = pl.multiple_of(i * T, T)
            qb = [qs[pl.ds(r0, T), sl[h]] for h in heads]
            state = step(qb, [(r0, True), (pl.multiple_of(r0 - T, T), False)], zero * SB_GROUP)

            def more(c):
                worst = c[2]
                for h in heads[1:]:
                    worst = jnp.maximum(worst, c[2 + 2 * h])
                return (c[0] >= 0) & (jnp.max(worst) >= SB_STOP)

            def walk(c):
                return (c[0] - 1,) + step(qb, [(pl.multiple_of(c[0] * T, T), False)], c[1:])

            finish(r0, i, lax.while_loop(more, walk, (i - 2,) + state))
            return 0

        lax.fori_loop(1, nblk, qblock, 0)

    blk = lambda off: pl.BlockSpec((S, BW), lambda g: (0, qb0 + off * ngrp + g), pipeline_mode=pl.Buffered(1))
    out = pl.BlockSpec((S, BW), lambda g: (0, g))
    in_specs, out_specs, out_shape, scratch = _carry_specs(
        rider, [blk(0), blk(1), blk(2)], (out, out), (pltpu.HBM((S, W), F32), pltpu.HBM((S, W), F32)),
        [pltpu.VMEM((S, BW), BF16)])
    first = lambda: pl.program_id(0) == 0
    last = lambda: pl.program_id(0) == ngrp - 1
    res = _pallas(
        _carry(rider, body, 3, 2, first, last), name=name, grid=(ngrp,), out_shape=out_shape,
        in_specs=in_specs, out_specs=out_specs, scratch_shapes=scratch,
        compiler_params=_params(5 * _nbytes((S, BW), F32), ("arbitrary",)))(
            proj, proj, proj, *(rider.arrays if rider else ()))
    return res[0], res[1], list(res[2:])


def sb_bwd(proj, ltot, dua, *, q_col, do_col, name, rider=None):
    S = proj.shape[0]
    dh = LANES // 2
    W = SB_HEADS * dh
    BW = SB_GROUP * dh
    ngrp = W // BW
    T = min(SB_BLOCK, S)
    nblk = S // T
    scale = dh ** -0.5
    qb0 = q_col // BW
    db0 = do_col // BW
    heads = range(SB_GROUP)
    sl = [slice(h * dh, (h + 1) * dh) for h in heads]

    def body(q_ref, k_ref, v_ref, l_ref, do_ref, dq_ref, dk_ref, dv_ref, dks, dvs):
        r_i = lax.broadcasted_iota(jnp.int32, (T, T), 0)
        c_i = lax.broadcasted_iota(jnp.int32, (T, T), 1)
        tri_rev = (r_i >= c_i).astype(BF16)
        tri_fwd = (r_i <= c_i).astype(BF16)
        vis = c_i < r_i

        dks[...] = jnp.zeros_like(dks)
        dvs[...] = jnp.zeros_like(dvs)

        def step(qb, dob, Lt, blocks, st):
            nb = range(len(blocks))
            kb = [[k_ref[pl.ds(j0, T), sl[h]].astype(BF16) for h in heads] for j0, _ in blocks]
            vb = [[v_ref[pl.ds(j0, T), sl[h]].astype(BF16) for h in heads] for j0, _ in blocks]
            z = [[_dot_nt(qb[h], kb[b][h]) for h in heads] for b in nb]
            dA =[[_dot_nt(dob[h], vb[b][h]) for h in heads] for b in nb]
            lk = [[_neg_softplus(z[b][h]) for h in heads] for b in nb]
            beta = [[jnp.exp(z[b][h] + lk[b][h]) for h in heads] for b in nb]
            lk = [[jnp.where(vis, lk[b][h], 0.0) if blocks[b][1] else lk[b][h] for h in heads] for b in nb]
            C = [[_split_dot(lk[b][h], tri_rev) for h in heads] for b in nb]
            P = [[st[3 * h + 1] for h in heads]]
            for b in nb:
                P.append([P[b][h] + C[b][h][:, 0:1] for h in heads])
            A = [[jnp.exp(z[b][h] + C[b][h] + (Lt[h] - P[b + 1][h])) for h in heads] for b in nb]
            A = [[jnp.where(vis, A[b][h], 0.0) if blocks[b][1] else A[b][h] for h in heads] for b in nb]
            g = [[A[b][h] * dA[b][h] for h in heads] for b in nb]
            Gin = [[_split_dot(g[b][h], tri_fwd) for h in heads] for b in nb]
            Gp = [[st[3 * h + 2] for h in heads]]
            for b in nb:
                Gp.append([Gp[b][h] + Gin[b][h][:, T - 1:T] for h in heads])
            dz = [[g[b][h] - beta[b][h] * (Gp[b][h] + Gin[b][h]) for h in heads] for b in nb]
            dz = [[jnp.where(vis, dz[b][h], 0.0) if blocks[b][1] else dz[b][h] for h in heads] for b in nb]
            dzb = [[dz[b][h].astype(BF16) for h in heads] for b in nb]
            out = ()
            for h in heads:
                dq = st[3 * h]
                for b in nb:
                    j0 = blocks[b][0]
                    dvs[pl.ds(j0, T), sl[h]] += _dot_tn(A[b][h].astype(BF16), dob[h])
                    dks[pl.ds(j0, T), sl[h]] += _dot_tn(dzb[b][h], qb[h])
                    dq = dq + _dot(dzb[b][h], kb[b][h])
                out += (dq, P[-1][h], Gp[-1][h])
            return out

        zero = jnp.zeros((T, 1), F32)
        init = (jnp.zeros((T, dh), F32), zero, zero)

        def operands(r0):
            return ([(q_ref[pl.ds(r0, T), sl[h]] * scale).astype(BF16) for h in heads],
                    [do_ref[pl.ds(r0, T), sl[h]].astype(BF16) for h in heads],
                    [l_ref[pl.ds(r0, T), h * dh:h * dh + 1] for h in heads])

        def finish(r0, c):
            for h in heads:
                dq_ref[pl.ds(r0, T), sl[h]] = (c[3 * h] * scale).astype(BF16)

        finish(0, step(*operands(0), [(0, True)], init * SB_GROUP))

        def qblock(i, _):
            r0 = pl.multiple_of(i * T, T)
            qb, dob, Lt = operands(r0)
            walked = jnp.clip(jnp.max(l_ref[pl.ds(r0, 8), 1:2]).astype(jnp.int32), 2, i + 1)

            def inner(j, c):
                return step(qb, dob, Lt, [(pl.multiple_of(j * T, T), False)], c)

            c = lax.fori_loop(i + 1 - walked, i - 1, inner, init * SB_GROUP)
            finish(r0, step(qb, dob, Lt, [(pl.multiple_of(r0 - T, T), False), (r0, True)], c))
            return 0

        lax.fori_loop(1, nblk, qblock, 0)
        dk_ref[...] = dks[...].astype(BF16)
        dv_ref[...] = dvs[...].astype(BF16)

    once = pl.Buffered(1)
    blk = lambda off: pl.BlockSpec((S, BW), lambda g: (0, qb0 + off * ngrp + g), pipeline_mode=once)
    out = pl.BlockSpec((S, BW), lambda g: (0, g))
    o_shape = pltpu.HBM((S, W), BF16)
    in_specs, out_specs, out_shape, scratch = _carry_specs(
        rider, [blk(0), blk(1), blk(2), pl.BlockSpec((S, BW), lambda g: (0, g), pipeline_mode=once),
                pl.BlockSpec((S, BW), lambda g: (0, db0 + g), pipeline_mode=once)], (out, out, out),
        (o_shape, o_shape, o_shape), [pltpu.VMEM((S, BW), F32)] * 2)
    first = lambda: pl.program_id(0) == 0
    last = lambda: pl.program_id(0) == ngrp - 1
    res = _pallas(
        _carry(rider, body, 5, 3, first, last), name=name, grid=(ngrp,), out_shape=out_shape,
        in_specs=in_specs, out_specs=out_specs, scratch_shapes=scratch,
        compiler_params=_params(6 * _nbytes((S, BW), F32), ("arbitrary",)))(
            proj, proj, proj, ltot, dua, *(rider.arrays if rider else ()))
    return res[0], res[1], res[2], list(res[3:])


def xattn_fwd(q, k, v, *, tm, name):
    S, D = q.shape
    Mlen = k.shape[0]
    hd = D // MEM_HEADS
    scale = hd ** -0.5

    def body(q_ref, k_ref, v_ref, o_ref):
        for h in range(MEM_HEADS):
            sl = slice(h * hd, (h + 1) * hd)
            s = _dot_nt(q_ref[:, sl], k_ref[:, sl]) * scale
            e = jnp.exp(s - jnp.max(s, axis=-1, keepdims=True))
            p = e / jnp.sum(e, axis=-1, keepdims=True)
            o_ref[:, sl] = _dot(p.astype(BF16), v_ref[:, sl]).astype(BF16)

    row = lambda i: (i, 0)
    fix = lambda i: (0, 0)
    return _pallas(
        body, name=name, out_shape=pltpu.HBM((S, D), BF16), grid=(S // tm,),
        in_specs=[pl.BlockSpec((tm, D), row), pl.BlockSpec((Mlen, D), fix), pl.BlockSpec((Mlen, D), fix)],
        out_specs=pl.BlockSpec((tm, D), row),
        compiler_params=_params(4 * _nbytes((tm, D), F32), ("parallel",)))(q, k, v)


def xattn_bwd(q, do, k, v, *, tm, name):
    S, D = q.shape
    Mlen = k.shape[0]
    hd = D // MEM_HEADS
    scale = hd ** -0.5

    def body(q_ref, do_ref, k_ref, v_ref, dq_ref, dk_ref, dv_ref):
        @pl.when(pl.program_id(0) == 0)
        def _():
            dk_ref[...] = jnp.zeros_like(dk_ref)
            dv_ref[...] = jnp.zeros_like(dv_ref)

        for h in range(MEM_HEADS):
            sl = slice(h * hd, (h + 1) * hd)
            qh, doh, kh, vh = q_ref[:, sl], do_ref[:, sl], k_ref[:, sl], v_ref[:, sl]
            s = _dot_nt(qh, kh) * scale
            e = jnp.exp(s - jnp.max(s, axis=-1, keepdims=True))
            p = e / jnp.sum(e, axis=-1, keepdims=True)
            dp = _dot_nt(doh, vh)
            ds = (p * (dp - jnp.sum(p * dp, axis=-1, keepdims=True)) * scale).astype(BF16)
            dq_ref[:, sl] = _dot(ds, kh).astype(BF16)
            dk_ref[:, sl] += _dot_tn(ds, qh)
            dv_ref[:, sl] += _dot_tn(p.astype(BF16), doh)

    row = lambda i: (i, 0)
    fix = lambda i: (0, 0)
    return _pallas(
        body, name=name, grid=(S // tm,),
        out_shape=(pltpu.HBM((S, D), BF16), pltpu.HBM((Mlen, D), F32),
                   pltpu.HBM((Mlen, D), F32)),
        in_specs=[pl.BlockSpec((tm, D), row), pl.BlockSpec((tm, D), row), pl.BlockSpec((Mlen, D), fix),
                  pl.BlockSpec((Mlen, D), fix)],
        out_specs=(pl.BlockSpec((tm, D), row), pl.BlockSpec((Mlen, D), fix), pl.BlockSpec((Mlen, D), fix)),
        compiler_params=_params(6 * _nbytes((tm, D), F32), ("arbitrary",)))(q, do, k, v)


FFN_HALO = 8


def _conv3(ext, w, lo):
    tm = ext.shape[0] - FFN_HALO
    return (w[0:1, :] * ext[lo:lo + tm, :] + w[1:2, :] * ext[lo + 1:lo + 1 + tm, :]
            + w[2:3, :] * ext[lo + 2:lo + 2 + tm, :])


def ffn_up_fwd(xb, w_up, conv_w, conv_b, *, tm, tn, name, rider=None):
    S, D = xb.shape
    nsh, _, ns = w_up.shape
    F = nsh * ns // 2
    per = ns // tn
    ncol = F // tn
    KW = conv_w.shape[0]
    assert KW == 3

    def body(x_ref, wv_ref, wg_ref, cwv_ref, cwg_ref, cbv_ref, cbg_ref, uv_ref, ug_ref, cv_ref, cg_ref, h_ref,
             carry):
        @pl.when(pl.program_id(1) == 0)
        def _():
            carry[...] = jnp.zeros_like(carry)

        x = x_ref[...]
        uv = _dot(x, wv_ref[...])
        ug = _dot(x, wg_ref[...])
        uv_ref[...] = uv.astype(BF16)
        ug_ref[...] = ug.astype(BF16)
        lo = FFN_HALO - (KW - 1)
        cv = _conv3(jnp.concatenate([carry[0], uv], axis=0), cwv_ref[...], lo) + cbv_ref[...]
        cg = _conv3(jnp.concatenate([carry[1], ug], axis=0), cwg_ref[...], lo) + cbg_ref[...]
        carry[0] = uv[tm - FFN_HALO:, :]
        carry[1] = ug[tm - FFN_HALO:, :]
        cv_ref[...] = cv.astype(BF16)
        cg_ref[...] = cg.astype(BF16)
        h_ref[...] = (cg * _sigmoid(cg) * cv).astype(BF16)

    wspec = lambda half: pl.BlockSpec((None, D, tn), lambda j, i: (half * (nsh // 2) + j // per, 0, j % per))
    cspec = lambda rows, half: pl.BlockSpec((rows, tn), lambda j, i: (0, half * ncol + j))
    out = pl.BlockSpec((tm, tn), lambda j, i: (i, j))
    o_shape = pltpu.HBM((S, F), BF16)
    blk = _nbytes((tm, D), BF16) + 2 * _nbytes((D, tn), BF16) + 8 * _nbytes((tm, tn), F32)
    nrow = S // tm
    in_specs, out_specs, out_shape, scratch = _carry_specs(
        rider, [pl.BlockSpec((tm, D), lambda j, i: (i, 0)), wspec(0), wspec(1), cspec(KW, 0), cspec(KW, 1),
                cspec(1, 0), cspec(1, 1)], (out,) * 5, (o_shape,) * 5, [pltpu.VMEM((2, FFN_HALO, tn), F32)])
    first = lambda: (pl.program_id(0) == 0) & (pl.program_id(1) == 0)
    last = lambda: (pl.program_id(0) == ncol - 1) & (pl.program_id(1) == nrow - 1)
    res = _pallas(
        _carry(rider, body, 7, 5, first, last), name=name, grid=(ncol, nrow), out_shape=out_shape,
        in_specs=in_specs, out_specs=out_specs, scratch_shapes=scratch,
        compiler_params=_params(blk, ("arbitrary", "arbitrary")))(
            xb, w_up, w_up, conv_w, conv_w, conv_b, conv_b, *(rider.arrays if rider else ()))
    return res[:5], list(res[5:])


def ffn_mid_bwd(dzb, w_down, up_v, up_g, conv_v, conv_g, conv_w, *, tm, tn, name, rider=None):
    S, D = dzb.shape
    F = up_v.shape[1]
    ncol = F // tn
    nrow = S // tm
    KW = conv_w.shape[0]
    assert KW == 3

    def body(dz_ref, wd_ref, uv_ref, ug_ref, cv_ref, cg_ref, cwv_ref, cwg_ref,
             dv_ref, dg_ref, dwv_ref, dwg_ref, dbv_ref, dbg_ref, carry):
        @pl.when(pl.program_id(1) == 0)
        def _():
            carry[...] = jnp.zeros_like(carry)
            for r in (dwv_ref, dwg_ref, dbv_ref, dbg_ref):
                r[...] = jnp.zeros_like(r)

        cv, cg = cv_ref[...].astype(F32), cg_ref[...].astype(F32)
        dh = _dot_nt(dz_ref[...], wd_ref[...])
        sg = _sigmoid(cg)
        dcv = dh * (cg * sg)
        dcg = dh * cv * (sg * (1.0 + cg * (1.0 - sg)))

        def back(dc, u_ref, cw, slot, du_ref, dw_ref, db_ref):
            ext = jnp.concatenate([dc, carry[slot]], axis=0)
            ahead = [dc, ext[1:tm + 1, :], ext[2:tm + 2, :]]
            du = cw[2:3, :] * ahead[0] + cw[1:2, :] * ahead[1] + cw[0:1, :] * ahead[2]
            du_ref[...] = du.astype(BF16)
            carry[slot] = dc[0:FFN_HALO, :]
            u = u_ref[...].astype(F32)
            for k in range(KW):
                dw_ref[k:k + 1, :] += jnp.sum(ahead[KW - 1 - k] * u, axis=0, keepdims=True)
            db_ref[...] += jnp.sum(dc, axis=0, keepdims=True)

        back(dcv, uv_ref, cwv_ref[...], 0, dv_ref, dwv_ref, dbv_ref)
        back(dcg, ug_ref, cwg_ref[...], 1, dg_ref, dwg_ref, dbg_ref)

    rev = lambda i: nrow - 1 - i
    tile = pl.BlockSpec((tm, tn), lambda j, i: (rev(i), j))
    cspec = lambda half: pl.BlockSpec((KW, tn), lambda j, i: (0, half * ncol + j))
    acc = lambda rows: pl.BlockSpec((rows, tn), lambda j, i: (0, j))
    big = pltpu.HBM((S, F), BF16)
    blk = _nbytes((tm, D), BF16) + _nbytes((tn, D), BF16) + 10 * _nbytes((tm, tn), F32)
    in_specs, out_specs, out_shape, scratch = _carry_specs(
        rider, [pl.BlockSpec((tm, D), lambda j, i: (rev(i), 0)), pl.BlockSpec((tn, D), lambda j, i: (j, 0)),
                tile, tile, tile, tile, cspec(0), cspec(1)],
        (tile, tile, acc(KW), acc(KW), acc(1), acc(1)),
        (big, big, pltpu.HBM((KW, F), F32), pltpu.HBM((KW, F), F32), pltpu.HBM((1, F), F32),
         pltpu.HBM((1, F), F32)), [pltpu.VMEM((2, FFN_HALO, tn), F32)])
    first = lambda: (pl.program_id(0) == 0) & (pl.program_id(1) == 0)
    last = lambda: (pl.program_id(0) == ncol - 1) & (pl.program_id(1) == nrow - 1)
    res = _pallas(
        _carry(rider, body, 8, 6, first, last), name=name, grid=(ncol, nrow), out_shape=out_shape,
        in_specs=in_specs, out_specs=out_specs, scratch_shapes=scratch,
        compiler_params=_params(blk, ("arbitrary", "arbitrary")))(
            dzb, w_down, up_v, up_g, conv_v, conv_g, conv_w, conv_w, *(rider.arrays if rider else ()))
    return res[:6], list(res[6:])


def loss_head(y, target, *, tm, name):
    S, D = y.shape

    def body(y_ref, t_ref, dy_ref, l_ref):
        @pl.when(pl.program_id(0) == 0)
        def _():
            l_ref[...] = jnp.zeros_like(l_ref)

        e = y_ref[...] - t_ref[...]
        dy_ref[...] = e * (1.0 / D)
        l_ref[...] += 0.5 * jnp.sum(jnp.mean(e * e, axis=-1, keepdims=True), axis=0, keepdims=True)

    row = lambda i: (i, 0)
    return _pallas(
        body, name=name, grid=(S // tm,),
        out_shape=(pltpu.HBM((S, D), F32), pltpu.HBM((1, 1), F32)),
        in_specs=[pl.BlockSpec((tm, D), row), pl.BlockSpec((tm, D), row)],
        out_specs=(pl.BlockSpec((tm, D), row), pl.BlockSpec((1, 1), lambda i: (0, 0))),
        compiler_params=_params(3 * _nbytes((tm, D), F32), ("arbitrary",)))(y, target)


def adamw(w, g, m, v, *, tr, name):
    R, C = w.shape
    c1 = 1.0 - ADAM_B1 ** ADAM_STEP
    c2 = 1.0 - ADAM_B2 ** ADAM_STEP

    def body(w_ref, g_ref, m_ref, v_ref, go_ref, d_ref, mo_ref, vo_ref):
        gv = g_ref[...]
        mn = ADAM_B1 * m_ref[...] + (1.0 - ADAM_B1) * gv
        vn = ADAM_B2 * v_ref[...] + (1.0 - ADAM_B2) * (gv * gv)
        go_ref[...] = gv
        mo_ref[...] = mn
        vo_ref[...] = vn
        d_ref[...] = -ADAM_LR * ((mn / c1) / (jnp.sqrt(vn / c2) + ADAM_EPS) + ADAM_WD * w_ref[...])

    spec = pl.BlockSpec((tr, C), lambda i: (i, 0))
    shape = pltpu.HBM((R, C), F32)
    return _pallas(
        body, name=name, grid=(R // tr,), out_shape=(shape,) * 4, in_specs=[spec] * 4, out_specs=(spec,) * 4,
        compiler_params=_params(8 * _nbytes((tr, C), F32), ("parallel",)))(w, g, m, v)


def add_pairs(gs, gots, core, *, name):
    k = len(gs)

    def body(c_ref, *refs):
        for a_ref, b_ref, o_ref in zip(refs[:k], refs[k:2 * k], refs[2 * k:]):
            o_ref[...] = (a_ref[...].astype(F32) + b_ref[...].astype(F32)).astype(BF16)

    own = [pl.BlockSpec((None, None) + g.shape[2:], lambda i, c: (i, c[0], 0, 0)) for g in gs]
    half = [pl.BlockSpec((None,) + g.shape[1:], lambda i, c: (i, 0, 0)) for g in gots]
    grid_spec = pltpu.PrefetchScalarGridSpec(
        num_scalar_prefetch=1, grid=(N_CHIPS,), in_specs=own + half, out_specs=tuple(half))
    blk = 3 * sum(_nbytes(g.shape[1:], BF16) for g in gots)
    return _pallas(
        body, name=name, grid_spec=grid_spec, out_shape=tuple(pltpu.HBM(g.shape, BF16) for g in gots),
        compiler_params=_params(blk, ("parallel",)))(core, *gs, *gots)


def sum_chips_into(bs, dests, layer, core, *, name):
    k = len(bs)
    steps = 2

    def body(c_ref, *refs):
        for b_ref, o_ref in zip(refs[:k], refs[2 * k:]):
            acc = b_ref[0].astype(F32)
            for p in range(1, N_CHIPS):
                acc = acc + b_ref[p].astype(F32)
            o_ref[...] = acc

    ins = [pl.BlockSpec((N_CHIPS, b.shape[1] // steps, b.shape[2]), lambda i, c: (0, i, 0)) for b in bs]
    outs = tuple(pl.BlockSpec((None, None, b.shape[1] // steps, b.shape[2]), lambda i, c: (layer, c[0], i, 0))
                 for b in bs)
    grid_spec = pltpu.PrefetchScalarGridSpec(
        num_scalar_prefetch=1, grid=(steps,), in_specs=ins + [pl.BlockSpec(memory_space=pl.ANY)] * k,
        out_specs=outs)
    blk = sum(_nbytes(b.shape, BF16) + _nbytes(b.shape[1:], F32) for b in bs) // steps
    return _pallas(
        body, name=name, grid_spec=grid_spec, out_shape=tuple(pltpu.HBM(d.shape, F32) for d in dests),
        input_output_aliases={1 + k + w: w for w in range(k)},
        compiler_params=_params(blk, ("parallel",)))(core, *bs, *dests)


_HBM = pl.BlockSpec(memory_space=pltpu.HBM)


def _place():
    x, y, c = lax.axis_index("x"), lax.axis_index("y"), lax.axis_index("c")
    chips = [(1 - x, y), (x, 1 - y), (1 - x, 1 - y)]
    return x, y, c, chips


class GatherRider:
    def __init__(self, shards):
        self.arrays = list(shards)
        self.n = n = len(shards)
        self.out_shape = tuple(pltpu.HBM((N_CHIPS,) + s.shape, s.dtype) for s in shards)
        self.scratch = [pltpu.SemaphoreType.DMA((n, 3))] * 4 + [pltpu.SemaphoreType.DMA((n,))]

    def _copies(self, ins, outs, sems):
        send_ici, recv_ici, send_d2d, recv_d2d, local = sems
        x, y, c, chips = _place()
        me = 2 * x + y

        def own(w):
            return pltpu.make_async_copy(ins[w], outs[w].at[me], local.at[w])

        def ici(w, j):
            px, py = chips[j]
            return pltpu.make_async_remote_copy(
                src_ref=ins[w].at[c], dst_ref=outs[w].at[me, c], send_sem=send_ici.at[w, j],
                recv_sem=recv_ici.at[w, j], device_id=(px, py, c), device_id_type=MESH)

        def landed(w, j, half):
            px, py = chips[j]
            return outs[w].at[2 * px + py, half]

        def d2d(w, j, half):
            return pltpu.make_async_remote_copy(
                src_ref=landed(w, j, half), dst_ref=landed(w, j, half), send_sem=send_d2d.at[w, j],
                recv_sem=recv_d2d.at[w, j], device_id=(x, y, 1 - c), device_id_type=MESH)

        def ici_arrival(w, j):
            return pltpu.make_async_remote_copy(
                src_ref=landed(w, j, c), dst_ref=landed(w, j, c), send_sem=send_ici.at[w, j],
                recv_sem=recv_ici.at[w, j], device_id=(x, y, c), device_id_type=MESH)

        return c, own, ici, d2d, ici_arrival

    def start(self, ins, outs, sems):
        c, own, ici, d2d, ici_arrival = self._copies(ins, outs, sems)
        for w in range(self.n):
            own(w).start()
            for j in range(3):
                ici(w, j).start()

    def finish(self, ins, outs, sems):
        c, own, ici, d2d, ici_arrival = self._copies(ins, outs, sems)
        for w in range(self.n):
            for j in range(3):
                ici_arrival(w, j).wait_recv()
                d2d(w, j, c).start()
        for w in range(self.n):
            for j in range(3):
                d2d(w, j, 1 - c).wait_recv()
        for w in range(self.n):
            for j in range(3):
                ici(w, j).wait_send()
                d2d(w, j, c).wait_send()
            own(w).wait()


class ScatterRider:
    def __init__(self, parts):
        self.arrays = list(parts)
        self.n = n = len(parts)
        self.out_shape = tuple(pltpu.HBM(p.shape, p.dtype) for p in parts)
        self.scratch = [pltpu.SemaphoreType.DMA((n, 3))] * 2 + [pltpu.SemaphoreType.DMA((n,))]

    def _copies(self, ins, outs, sems):
        send, recv, local = sems
        x, y, c, chips = _place()
        me = 2 * x + y

        def own(w):
            return pltpu.make_async_copy(ins[w].at[me], outs[w].at[me], local.at[w])

        def copy(w, j):
            px, py = chips[j]
            return pltpu.make_async_remote_copy(
                src_ref=ins[w].at[2 * px + py], dst_ref=outs[w].at[me], send_sem=send.at[w, j],
                recv_sem=recv.at[w, j], device_id=(px, py, c), device_id_type=MESH)

        def arrival(w, j):
            px, py = chips[j]
            blk = outs[w].at[2 * px + py]
            return pltpu.make_async_remote_copy(
                src_ref=blk, dst_ref=blk, send_sem=send.at[w, j], recv_sem=recv.at[w, j],
                device_id=(x, y, c), device_id_type=MESH)

        return own, copy, arrival

    def start(self, ins, outs, sems):
        own, copy, arrival = self._copies(ins, outs, sems)
        for w in range(self.n):
            own(w).start()
            for j in range(3):
                copy(w, j).start()

    def finish(self, ins, outs, sems):
        own, copy, arrival = self._copies(ins, outs, sems)
        for w in range(self.n):
            for j in range(3):
                arrival(w, j).wait_recv()
        for w in range(self.n):
            for j in range(3):
                copy(w, j).wait_send()
            own(w).wait()


def _carry(rider, body, n_in, n_out, first, last):
    if rider is None:
        return body
    k, m = rider.n, len(rider.scratch)

    def carried(*refs):
        ins, r_in = refs[:n_in], refs[n_in:n_in + k]
        outs, r_out = refs[n_in + k:n_in + k + n_out], refs[n_in + k + n_out:n_in + 2 * k + n_out]
        rest = refs[n_in + 2 * k + n_out:]
        scratch, sems = rest[:len(rest) - m], rest[len(rest) - m:]

        @pl.when(first())
        def _():
            rider.start(r_in, r_out, sems)

        body(*ins, *outs, *scratch)

        @pl.when(last())
        def _():
            rider.finish(r_in, r_out, sems)

    return carried


def _carry_specs(rider, in_specs, out_specs, out_shape, scratch):
    if rider is None:
        return list(in_specs), tuple(out_specs), tuple(out_shape), list(scratch)
    k = rider.n
    return (list(in_specs) + [_HBM] * k, tuple(out_specs) + (_HBM,) * k, tuple(out_shape) + rider.out_shape,
            list(scratch) + list(rider.scratch))


def run_rider(rider, *, name):
    k = rider.n

    def body(*refs):
        rider.start(refs[:k], refs[k:2 * k], refs[2 * k:])
        rider.finish(refs[:k], refs[k:2 * k], refs[2 * k:])

    return _pallas(body, name=name, out_shape=rider.out_shape, in_specs=[_HBM] * k, out_specs=(_HBM,) * k,
                   scratch_shapes=rider.scratch)(*rider.arrays)


def allgather_small(shards, *, name):
    n = len(shards)

    def body(*refs):
        ins, outs = refs[:n], refs[n:2 * n]
        send, recv, local = refs[2 * n:]
        x, y, c, chips = _place()
        me = 2 * x + y
        locals_ = [pltpu.make_async_copy(ins[w], outs[w].at[me], local.at[w]) for w in range(n)]
        for cp in locals_:
            cp.start()

        def copy(w, j):
            px, py = chips[j]
            return pltpu.make_async_remote_copy(
                src_ref=ins[w], dst_ref=outs[w].at[me], send_sem=send.at[w, j], recv_sem=recv.at[w, j],
                device_id=(px, py, c), device_id_type=MESH)

        def arrival(w, j):
            px, py = chips[j]
            blk = outs[w].at[2 * px + py]
            return pltpu.make_async_remote_copy(
                src_ref=blk, dst_ref=blk, send_sem=send.at[w, j], recv_sem=recv.at[w, j],
                device_id=(x, y, c), device_id_type=MESH)

        for w in range(n):
            for j in range(3):
                copy(w, j).start()
        for w in range(n):
            for j in range(3):
                arrival(w, j).wait_recv()
        for w in range(n):
            for j in range(3):
                copy(w, j).wait_send()
        for cp in locals_:
            cp.wait()

    out_shape = tuple(pltpu.HBM((N_CHIPS,) + s.shape, s.dtype) for s in shards)
    return _pallas(
        body, name=name, out_shape=out_shape, in_specs=[_HBM] * n, out_specs=(_HBM,) * n,
        scratch_shapes=[pltpu.SemaphoreType.DMA((n, 3))] * 2 + [pltpu.SemaphoreType.DMA((n,))],
    )(*shards)


class SwapRider:
    def __init__(self, grads):
        self.arrays = list(grads)
        self.n = n = len(grads)
        self.out_shape = tuple(pltpu.HBM((N_CHIPS,) + g.shape[2:], g.dtype) for g in grads)
        self.scratch = [pltpu.SemaphoreType.DMA((n,))] * 2

    def _copies(self, ins, outs, sems):
        send, recv = sems
        x, y, c, _ = _place()
        return [pltpu.make_async_remote_copy(
            src_ref=ins[w].at[:, 1 - c], dst_ref=outs[w], send_sem=send.at[w], recv_sem=recv.at[w],
            device_id=(x, y, 1 - c), device_id_type=MESH) for w in range(self.n)]

    def start(self, ins, outs, sems):
        for cp in self._copies(ins, outs, sems):
            cp.start()

    def finish(self, ins, outs, sems):
        copies = self._copies(ins, outs, sems)
        for cp in copies:
            cp.wait_recv()
        for cp in copies:
            cp.wait_send()


def rs_sibling_share(stacked, *, name):
    n = len(stacked)

    def body(*refs):
        bufs = refs[n:2 * n]
        send, recv = refs[2 * n:]
        x, y, c, _ = _place()
        shares, arrivals = [], []
        for w in range(n):
            mine, other = bufs[w].at[:, c], bufs[w].at[:, 1 - c]
            shares.append(pltpu.make_async_remote_copy(
                src_ref=mine, dst_ref=mine, send_sem=send.at[w], recv_sem=recv.at[w],
                device_id=(x, y, 1 - c), device_id_type=MESH))
            arrivals.append(pltpu.make_async_remote_copy(
                src_ref=other, dst_ref=other, send_sem=send.at[w], recv_sem=recv.at[w],
                device_id=(x, y, c), device_id_type=MESH))
        for cp in shares:
            cp.start()
        for cp in arrivals:
            cp.wait_recv()
        for cp in shares:
            cp.wait_send()

    out_shape = tuple(pltpu.HBM(s.shape, F32) for s in stacked)
    return _pallas(
        body, name=name, out_shape=out_shape, in_specs=[_HBM] * n, out_specs=(_HBM,) * n,
        input_output_aliases={w: w for w in range(n)},
        scratch_shapes=[pltpu.SemaphoreType.DMA((n,))] * 2,
    )(*stacked)


def allreduce_small(v, *, name):
    R, C = v.shape

    def body(v_ref, o_ref, land, send, recv):
        x, y, c, _ = _place()
        me = 4 * x + 2 * y + c
        land[me] = v_ref[...]

        def flip(k):
            return (1 - x) if k & 4 else x, (1 - y) if k & 2 else y, (1 - c) if k & 1 else c

        copies = []
        for k in range(1, N_DEV):
            px, py, pc = flip(k)
            copies.append(pltpu.make_async_remote_copy(
                src_ref=v_ref, dst_ref=land.at[me], send_sem=send.at[k - 1], recv_sem=recv.at[k - 1],
                device_id=(px, py, pc), device_id_type=MESH))
        for cp in copies:
            cp.start()
        for k in range(1, N_DEV):
            px, py, pc = flip(k)
            blk = land.at[4 * px + 2 * py + pc]
            pltpu.make_async_remote_copy(
                src_ref=blk, dst_ref=blk, send_sem=send.at[k - 1], recv_sem=recv.at[k - 1],
                device_id=(x, y, c), device_id_type=MESH).wait_recv()
        for cp in copies:
            cp.wait_send()
        acc = land[0]
        for d in range(1, N_DEV):
            acc = acc + land[d]
        o_ref[...] = acc

    vm = pl.BlockSpec(memory_space=pltpu.VMEM)
    return pl.pallas_call(
        body, name=name, out_shape=jax.ShapeDtypeStruct((R, C), F32), in_specs=[vm], out_specs=vm,
        scratch_shapes=[pltpu.VMEM((N_DEV, R, C), F32), pltpu.SemaphoreType.DMA((N_DEV - 1,)),
                        pltpu.SemaphoreType.DMA((N_DEV - 1,))],
        compiler_params=pltpu.CompilerParams(vmem_limit_bytes=int(min(12 * R * C * 4 + (8 << 20), VMEM_CAP))),
    )(v)


def _pack(arrays):
    flat = jnp.concatenate([a.reshape(-1) for a in arrays])
    return flat.reshape(-1, LANES)


def _unpack(packed, shapes):
    flat = packed.reshape(-1)
    out, off = [], 0
    for s in shapes:
        n = 1
        for d in s:
            n *= d
        out.append(flat[off:off + n].reshape(s))
        off += n
    return out


def _row_tile(rows, cap=512):
    t = 1 << (cap.bit_length() - 1)
    while rows % t:
        t //= 2
    return t


def _adamw_tile(rows, cols):
    return _row_tile(rows, max(8, (1 << 20) // (4 * cols)))


def kernel(x, mem, w_in, conv_w, conv_b, conv_ln_g, conv_ln_b, w_out, ln1_g, ln1_b, mem_wq, mem_wk, mem_wv, mem_wo, ln2_g, ln2_b, ffn_up, ffn_conv_w, ffn_conv_b, ffn_down, ln3_g, ln3_b, loss_target, m_w_in, m_conv_w, m_conv_b, m_conv_ln_g, m_conv_ln_b, m_w_out, m_ln1_g, m_ln1_b, m_mem_wq, m_mem_wk, m_mem_wv, m_mem_wo, m_ln2_g, m_ln2_b, m_ffn_up, m_ffn_conv_w, m_ffn_conv_b, m_ffn_down, m_ln3_g, m_ln3_b, v_w_in, v_conv_w, v_conv_b, v_conv_ln_g, v_conv_ln_b, v_w_out, v_ln1_g, v_ln1_b, v_mem_wq, v_mem_wk, v_mem_wv, v_mem_wo, v_ln2_g, v_ln2_b, v_ffn_up, v_ffn_conv_w, v_ffn_conv_b, v_ffn_down, v_ln3_g, v_ln3_b):
    W = dict(w_in=w_in, conv_w=conv_w, conv_b=conv_b, conv_ln_g=conv_ln_g, conv_ln_b=conv_ln_b, w_out=w_out,
             ln1_g=ln1_g, ln1_b=ln1_b, mem_wq=mem_wq, mem_wk=mem_wk, mem_wv=mem_wv, mem_wo=mem_wo, ln2_g=ln2_g,
             ln2_b=ln2_b, ffn_up=ffn_up, ffn_conv_w=ffn_conv_w, ffn_conv_b=ffn_conv_b, ffn_down=ffn_down,
             ln3_g=ln3_g, ln3_b=ln3_b)
    M1 = dict(w_in=m_w_in, conv_w=m_conv_w, conv_b=m_conv_b, conv_ln_g=m_conv_ln_g, conv_ln_b=m_conv_ln_b,
              w_out=m_w_out, ln1_g=m_ln1_g, ln1_b=m_ln1_b, mem_wq=m_mem_wq, mem_wk=m_mem_wk, mem_wv=m_mem_wv,
              mem_wo=m_mem_wo, ln2_g=m_ln2_g, ln2_b=m_ln2_b, ffn_up=m_ffn_up, ffn_conv_w=m_ffn_conv_w,
              ffn_conv_b=m_ffn_conv_b, ffn_down=m_ffn_down, ln3_g=m_ln3_g, ln3_b=m_ln3_b)
    V2 = dict(w_in=v_w_in, conv_w=v_conv_w, conv_b=v_conv_b, conv_ln_g=v_conv_ln_g, conv_ln_b=v_conv_ln_b,
              w_out=v_w_out, ln1_g=v_ln1_g, ln1_b=v_ln1_b, mem_wq=v_mem_wq, mem_wk=v_mem_wk, mem_wv=v_mem_wv,
              mem_wo=v_mem_wo, ln2_g=v_ln2_g, ln2_b=v_ln2_b, ffn_up=v_ffn_up, ffn_conv_w=v_ffn_conv_w,
              ffn_conv_b=v_ffn_conv_b, ffn_down=v_ffn_down, ln3_g=v_ln3_g, ln3_b=v_ln3_b)

    L = w_in.shape[0]
    S, D = x.shape[1], x.shape[2]
    C = conv_b.shape[1]
    alpha = (2.0 * L) ** 0.25
    chip = 2 * lax.axis_index("x") + lax.axis_index("y")
    xs, mems, tgt = x[0], mem[0], loss_target[0]
    mem_bf = mems.astype(BF16)
    tm = _row_tile(S)
    tm_ffn = _row_tile(S, 256)
    tm_big = _row_tile(S, 1024)

    def shards_of(l, names):
        out = []
        for n in names:
            wl = W[n][l].astype(BF16)
            out.append(wl.reshape(2, wl.shape[0] // 2, wl.shape[1]))
        return out

    def gathered(names, got):
        layer = {}
        for n, g in zip(names, got):
            rows, cols = W[n].shape[1], W[n].shape[2]
            layer[n] = g.reshape(N_CHIPS, rows, cols) if n in COL_SHARDED else g.reshape(N_CHIPS * rows, cols)
        return layer

    full = [dict() for _ in range(L)]
    full[0].update(gathered(RIDE_IN, run_rider(GatherRider(shards_of(0, RIDE_IN)), name="allgather_w_in")))
    cw_all, fcw_all = allgather_small([conv_w, ffn_conv_w], name="allgather_small")
    cw_full = jnp.transpose(cw_all, (1, 2, 0, 3)).reshape(L, conv_w.shape[1], -1)
    fcw_full = jnp.transpose(fcw_all, (1, 2, 0, 3)).reshape(L, ffn_conv_w.shape[1], -1)

    saved = []
    h, hb = xs, xs.astype(BF16)
    for l in range(L):
        fw = full[l]
        s = dict(x=h, xb=hb)
        s['glu'], s['qkv'] = proj_split(hb, fw['w_in'], 2 * C, tm=tm_big, name="proj")
        on_conv = RIDE_ATT if l == 0 else RIDE_FFN[1:]
        on_sb = RIDE_FFN if l == 0 else RIDE_FFN[:1]
        s['u1'], got = conv_fwd(s['glu'], cw_full[l], conv_b[l][None], name="conv_fwd",
                                rider=GatherRider(shards_of(l, on_conv)))
        fw.update(gathered(on_conv, got))
        more = l + 1 < L
        s['o_sb'], s['ltot'], got = sb_fwd(
            s['qkv'], q_col=0, name="sb_fwd", rider=GatherRider(shards_of(l, on_sb)))
        fw.update(gathered(on_sb, got))
        s['ua'] = ln_silu(s['u1'], s['o_sb'], conv_ln_g[l][None], conv_ln_b[l][None], tm=tm, name="ln_silu")
        s['x1'], s['x1b'], s['zh1'], s['rs1'] = mm_ln(
            s['ua'], fw['w_out'], h, ln1_g[l][None], ln1_b[l][None], alpha, tm=tm, name="out_proj_ln")
        s['q2'] = mm_nn(s['x1b'], fw['mem_wq'], BF16, tm=min(1024, S), tn=512, name="mem_q")
        s['k2'] = mm_nn(mem_bf, fw['mem_wk'], BF16, tm=mem_bf.shape[0], tn=512, name="mem_kv")
        s['v2'] = mm_nn(mem_bf, fw['mem_wv'], BF16, tm=mem_bf.shape[0], tn=512, name="mem_kv")
        s['o2'] = xattn_fwd(s['q2'], s['k2'], s['v2'], tm=tm, name="xattn_fwd")
        s['x2'], s['x2b'], s['zh2'], s['rs2'] = mm_ln(
            s['o2'], fw['mem_wo'], s['x1'], ln2_g[l][None], ln2_b[l][None], alpha, tm=tm, name="mem_o_ln")
        (s['upv'], s['upg'], s['cv'], s['cg'], s['hmid']), got = ffn_up_fwd(
            s['x2b'], fw['ffn_up'], fcw_full[l], ffn_conv_b[l][None], tm=tm_ffn, tn=fw['ffn_up'].shape[2],
            name="ffn_up_fwd", rider=GatherRider(shards_of(l + 1, RIDE_ATT + RIDE_IN)) if more else None)
        if more:
            full[l + 1].update(gathered(RIDE_ATT + RIDE_IN, got))
        h, hb, s['zh3'], s['rs3'] = mm_ln(
            s['hmid'], fw['ffn_down'], s['x2'], ln3_g[l][None], ln3_b[l][None], alpha, tm=tm, name="ffn_down_ln")
        saved.append(s)

    dx, loss_part = loss_head(h, tgt, tm=tm, name="loss_head")
    loss = lax.psum(loss_part[0, 0], ("x", "y", "c"))

    core = lax.axis_index("c").astype(jnp.int32).reshape(1)
    reduced_big = {n: lax.empty((L, 2, W[n].shape[1] // 2, W[n].shape[2]), F32) for n in BIG}
    small_grads = [None] * L

    def row_halves(g, names):
        parts = []
        for n in names:
            rows, cols = W[n].shape[1], W[n].shape[2]
            parts.append(g[n].reshape(N_CHIPS, 2, rows // 2, cols))
        return parts

    def pre_add(g, names):
        parts = row_halves(g, names)
        got = run_rider(SwapRider(parts), name="rs_sibling_swap")
        return list(add_pairs(parts, got, core, name="rs_add_pairs"))

    def reduce_into(names, scattered, layer):
        reduced_big.update(zip(names, sum_chips_into(
            list(scattered), [reduced_big[n] for n in names], layer, core, name="rs_sum_chips")))

    pending = None
    for l in reversed(range(L)):
        fw, s = full[l], saved[l]
        g = {}
        if l == L - 1:
            top = ln_bwd(dx, s['zh3'], s['rs3'], ln3_g[l][None], tm=tm, name="ln_bwd")
        dz3, dz3b, g['ln3_g'], g['ln3_b'] = top
        ftn = fw['ffn_up'].shape[2]
        (dupv, dupg, dfw_v, dfw_g, dfb_v, dfb_g), sc = ffn_mid_bwd(
            dz3b, fw['ffn_down'], s['upv'], s['upg'], s['cv'], s['cg'], fcw_full[l], tm=tm_ffn, tn=ftn,
            name="ffn_mid_bwd", rider=ScatterRider(pending) if pending else None)
        if pending:
            reduce_into(RIDE_MIX, sc, l + 1)
        g['ffn_conv_w'] = jnp.concatenate([dfw_v, dfw_g], axis=1)
        g['ffn_conv_b'] = jnp.concatenate([dfb_v, dfb_g], axis=1)[0]
        g['ffn_down'] = mm_tn(s['hmid'], [dz3b], tk=ftn, tn=512, tmc=min(1024, S), name="grad_ffn_down")
        dz2, dz2b, g['ln2_g'], g['ln2_b'] = mm_nt_ln_bwd(
            [dupv, dupg], fw['ffn_up'], dz3, alpha, s['zh2'], s['rs2'], ln2_g[l][None], tm=tm_ffn,
            name="ffn_up_bwd")
        g['ffn_up'] = mm_tn(s['x2b'], [dupv, dupg], tk=512, tn=ftn, shard_width=ftn, tmc=min(1024, S),
                            name="grad_ffn_up")

        do2 = mm_nt([dz2b], fw['mem_wo'], BF16, tm=tm_big, tk=512, name="mem_o_bwd")
        g['mem_wo'] = mm_tn(s['o2'], [dz2b], tk=512, tn=512, name="grad_sq")
        dq2, dk2, dv2 = xattn_bwd(s['q2'], do2, s['k2'], s['v2'], tm=tm, name="xattn_bwd")
        dz1, dz1b, g['ln1_g'], g['ln1_b'] = mm_nt_ln_bwd(
            [dq2], fw['mem_wq'], dz2, alpha, s['zh1'], s['rs1'], ln1_g[l][None], tm=tm, name="mem_q_bwd")
        g['mem_wq'] = mm_tn(s['x1b'], [dq2], tk=512, tn=512, name="grad_sq")
        g['mem_wk'] = mm_tn(mem_bf, [dk2], tk=512, tn=512, name="grad_mem_kv")
        g['mem_wv'] = mm_tn(mem_bf, [dv2], tk=512, tn=512, name="grad_mem_kv")

        rest = row_halves(g, RIDE_REST)
        dua, got = mm_nt([dz1b], fw['w_out'], F32, tm=tm_big, tk=512, name="out_proj_bwd", rider=SwapRider(rest))
        rest = list(add_pairs(rest, got, core, name="rs_add_pairs"))
        g['w_out'] = mm_tn(s['ua'], [dz1b], tk=512, tn=512, name="grad_sq")
        dq, dk, dv, sc = sb_bwd(
            s['qkv'], s['ltot'], dua, q_col=0, do_col=C, name="sb_bwd",
            rider=ScatterRider(rest[:-1]))
        reduce_into(RIDE_REST[:-1], sc, l)
        du1, g['conv_ln_g'], g['conv_ln_b'] = ln_silu_bwd(
            dua, s['u1'], conv_ln_g[l][None], conv_ln_b[l][None], tm=tm, name="ln_silu_bwd")
        (da, dg, g['conv_w'], dcb), sc = conv_bwd(du1, s['glu'], cw_full[l], name="conv_bwd",
                                                  rider=ScatterRider(rest[-1:]))
        reduce_into(RIDE_REST[-1:], sc, l)
        g['conv_b'] = dcb
        dproj = jnp.concatenate([da, dg, dq, dk, dv], axis=1)
        ns_in = fw['w_in'].shape[2]
        if l > 0:
            below = saved[l - 1]
            top = mm_nt_ln_bwd([dproj], fw['w_in'], dz1, alpha, below['zh3'], below['rs3'], ln3_g[l - 1][None],
                               tm=tm, name="proj_bwd")
        else:
            dx = mm_nt([dproj], fw['w_in'], F32, tm=tm_big, tk=512, res=dz1, alpha=alpha, name="proj_bwd_x")
        g['w_in'] = mm_tn(s['xb'], [dproj], tk=512, tn=ns_in, shard_width=ns_in, name="grad_w_in")

        pending = pre_add(g, RIDE_MIX)
        small_grads[l] = {n: g[n].reshape(W[n].shape[1:-1] + (-1,)) for n in SMALL}

    grad_x = dx[None]

    reduce_into(RIDE_MIX, run_rider(ScatterRider(pending), name="rs_chip_scatter"), 0)
    shared = rs_sibling_share([reduced_big[n] for n in BIG], name="rs_sibling_share")
    G = {}
    for n, sh in zip(BIG, shared):
        G[n] = sh.reshape(W[n].shape)

    small_full_shapes = []
    small_stack = []
    for n in SMALL:
        st = jnp.stack([small_grads[l][n] for l in range(L)])
        small_stack.append(st)
        small_full_shapes.append(st.shape)
    reduced = _unpack(allreduce_small(_pack(small_stack), name="allreduce_small"), small_full_shapes)
    for n, r in zip(SMALL, reduced):
        if n in SMALL_SHARDED:
            width = W[n].shape[-1]
            r = lax.dynamic_slice_in_dim(r, chip * width, width, axis=2)
        G[n] = r

    out_g, out_d, out_m, out_v = {}, {}, {}, {}
    for n in BIG:
        shp = W[n].shape
        flat = lambda a: a.reshape(shp[0] * shp[1], shp[2])
        res = adamw(flat(W[n]), flat(G[n]), flat(M1[n]), flat(V2[n]), tr=_adamw_tile(shp[0] * shp[1], shp[2]), name="adamw")
        out_g[n], out_d[n], out_m[n], out_v[n] = [r.reshape(shp) for r in res]
    small_shapes = [W[n].shape for n in SMALL]
    packed = [_pack([d[n] for n in SMALL]) for d in (W, G, M1, V2)]
    res = adamw(*packed, tr=packed[0].shape[0], name="adamw_small")
    for d, r in zip((out_g, out_d, out_m, out_v), res):
        for n, a in zip(SMALL, _unpack(r, small_shapes)):
            d[n] = a

    return (loss, grad_x, *[out_g[n] for n in WEIGHTS], *[out_d[n] for n in WEIGHTS],
            *[out_m[n] for n in WEIGHTS], *[out_v[n] for n in WEIGHTS])
```

```python
import functools

import jax
import jax.numpy as jnp
from jax import lax
from jax.experimental import pallas as pl
from jax.experimental.pallas import tpu as pltpu

F32 = jnp.float32
BF16 = jnp.bfloat16
MESH = pl.DeviceIdType.MESH

LN_EPS = 1e-5
SB_HEADS = 8
MEM_HEADS = 4
ADAM_LR, ADAM_B1, ADAM_B2, ADAM_EPS, ADAM_WD, ADAM_STEP = 0.001, 0.9, 0.999, 1e-08, 0.01, 10

LANES = 128
V7X_VMEM_BYTES = 64 << 20
VMEM_CAP = V7X_VMEM_BYTES - (6 << 20)
N_CHIPS = 4
N_DEV = 8

BIG = ('w_in', 'w_out', 'mem_wq', 'mem_wk', 'mem_wv', 'mem_wo', 'ffn_up', 'ffn_down')
RIDE_IN = ('w_in',)
RIDE_ATT = ('w_out', 'mem_wq', 'mem_wk', 'mem_wv', 'mem_wo')
RIDE_FFN = ('ffn_up', 'ffn_down')
RIDE_MIX = ('w_in', 'w_out')
RIDE_REST = ('mem_wq', 'mem_wk', 'mem_wv', 'mem_wo', 'ffn_up', 'ffn_down')
COL_SHARDED = ('w_in', 'ffn_up')
SMALL = ('conv_w', 'conv_b', 'conv_ln_g', 'conv_ln_b', 'ln1_g', 'ln1_b', 'ln2_g', 'ln2_b',
         'ffn_conv_w', 'ffn_conv_b', 'ln3_g', 'ln3_b')
SMALL_SHARDED = ('conv_w', 'ffn_conv_w')
WEIGHTS = ('w_in', 'conv_w', 'conv_b', 'conv_ln_g', 'conv_ln_b', 'w_out', 'ln1_g', 'ln1_b',
           'mem_wq', 'mem_wk', 'mem_wv', 'mem_wo', 'ln2_g', 'ln2_b', 'ffn_up', 'ffn_conv_w',
           'ffn_conv_b', 'ffn_down', 'ln3_g', 'ln3_b')


def _params(block_bytes, semantics=None, **kw):
    limit = int(min(max(2 * block_bytes + (8 << 20), 32 << 20), VMEM_CAP))
    return pltpu.CompilerParams(dimension_semantics=semantics, vmem_limit_bytes=limit, **kw)


def _pallas(body, **kw):
    call = pl.pallas_call(body, **kw)

    def run(*args):
        return call(*[pltpu.with_memory_space_constraint(a, pltpu.HBM)
                      if jnp.issubdtype(a.dtype, jnp.floating) else a for a in args])

    return run


def _nbytes(shape, dtype):
    n = 1
    for s in shape:
        n *= s
    return n * jnp.dtype(dtype).itemsize


def _dot(a, b):
    return jnp.dot(a, b, preferred_element_type=F32)


def _dot_nt(a, b):
    return lax.dot_general(a, b, (((1,), (1,)), ((), ())), preferred_element_type=F32)


def _dot_tn(a, b):
    return lax.dot_general(a, b, (((0,), (0,)), ((), ())), preferred_element_type=F32)


def _sigmoid(x):
    return 1.0 / (1.0 + jnp.exp(-x))


def mm_nn(a, b, out_dtype, *, tm, tn, name):
    M, K = a.shape
    sharded = b.ndim == 3
    if sharded:
        nsh, _, ns = b.shape
        N, per = nsh * ns, ns // tn
        b_spec = pl.BlockSpec((None, K, tn), lambda i, j: (j // per, 0, j % per))
    else:
        N = b.shape[1]
        b_spec = pl.BlockSpec((K, tn), lambda i, j: (0, j))

    def body(a_ref, b_ref, o_ref):
        o_ref[...] = _dot(a_ref[...].astype(BF16), b_ref[...]).astype(o_ref.dtype)

    blk = _nbytes((tm, K), a.dtype) + _nbytes((K, tn), BF16) + _nbytes((tm, tn), out_dtype)
    return _pallas(
        body, name=name, out_shape=pltpu.HBM((M, N), out_dtype), grid=(M // tm, N // tn),
        in_specs=[pl.BlockSpec((tm, K), lambda i, j: (i, 0)), b_spec],
        out_specs=pl.BlockSpec((tm, tn), lambda i, j: (i, j)),
        compiler_params=_params(blk, ("parallel", "parallel")))(a, b)


def proj_split(a, b, n_f32, *, tm, name):
    M, K = a.shape
    nsh, _, ns = b.shape
    N = nsh * ns

    def body(a_ref, b_ref, lo_ref, hi_ref):
        acc = _dot(a_ref[...], b_ref[...])
        j = pl.program_id(1)
        for s in range(nsh):
            c0, c1 = s * ns, (s + 1) * ns
            cut = min(max(n_f32 - c0, 0), ns)

            @pl.when(j == s)
            def _(c0=c0, c1=c1, cut=cut):
                if cut > 0:
                    lo_ref[:, c0:c0 + cut] = acc[:, 0:cut]
                if cut < ns:
                    hi_ref[:, c0 + cut - n_f32:c1 - n_f32] = acc[:, cut:ns].astype(BF16)

    blk = _nbytes((tm, K), BF16) + _nbytes((K, ns), BF16) + _nbytes((tm, N), F32)
    return _pallas(
        body, name=name, grid=(M // tm, nsh),
        out_shape=(pltpu.HBM((M, n_f32), F32), pltpu.HBM((M, N - n_f32), BF16)),
        in_specs=[pl.BlockSpec((tm, K), lambda i, j: (i, 0)), pl.BlockSpec((None, K, ns), lambda i, j: (j, 0, 0))],
        out_specs=(pl.BlockSpec((tm, n_f32), lambda i, j: (i, 0)), pl.BlockSpec((tm, N - n_f32), lambda i, j: (i, 0))),
        compiler_params=_params(blk, ("parallel", "arbitrary")))(a, b)


def mm_ln(a, b, x, gamma, beta, alpha, *, tm, name):
    M, K = a.shape
    D = b.shape[1]

    def body(a_ref, b_ref, x_ref, g_ref, be_ref, y_ref, yb_ref, zh_ref, rs_ref):
        z = alpha * x_ref[...] + _dot(a_ref[...], b_ref[...])
        mu = jnp.mean(z, axis=-1, keepdims=True)
        zc = z - mu
        rstd = lax.rsqrt(jnp.mean(zc * zc, axis=-1, keepdims=True) + LN_EPS)
        zh = zc * rstd
        y = zh * g_ref[...] + be_ref[...]
        y_ref[...] = y
        yb_ref[...] = y.astype(BF16)
        zh_ref[...] = zh
        rs_ref[...] = rstd

    row = lambda i: (i, 0)
    fix = lambda i: (0, 0)
    blk = _nbytes((tm, K), BF16) + _nbytes((K, D), BF16) + 4 * _nbytes((tm, D), F32)
    return _pallas(
        body, name=name, grid=(M // tm,),
        out_shape=(pltpu.HBM((M, D), F32), pltpu.HBM((M, D), BF16),
                   pltpu.HBM((M, D), F32), pltpu.HBM((M, 1), F32)),
        in_specs=[pl.BlockSpec((tm, K), row), pl.BlockSpec((K, D), fix), pl.BlockSpec((tm, D), row),
                  pl.BlockSpec((1, D), fix), pl.BlockSpec((1, D), fix)],
        out_specs=(pl.BlockSpec((tm, D), row), pl.BlockSpec((tm, D), row), pl.BlockSpec((tm, D), row),
                   pl.BlockSpec((tm, 1), row)),
        compiler_params=_params(blk, ("parallel",)))(a, b, x, gamma, beta)


def ln_bwd(dy, zh, rstd, gamma, *, tm, name):
    M, D = dy.shape

    def body(dy_ref, zh_ref, rs_ref, g_ref, dz_ref, dzb_ref, dg_ref, db_ref):
        @pl.when(pl.program_id(0) == 0)
        def _():
            dg_ref[...] = jnp.zeros_like(dg_ref)
            db_ref[...] = jnp.zeros_like(db_ref)

        dyv, zhv = dy_ref[...], zh_ref[...]
        dg_ref[...] += jnp.sum(dyv * zhv, axis=0, keepdims=True)
        db_ref[...] += jnp.sum(dyv, axis=0, keepdims=True)
        dzh = dyv * g_ref[...]
        m1 = jnp.mean(dzh, axis=-1, keepdims=True)
        m2 = jnp.mean(dzh * zhv, axis=-1, keepdims=True)
        dz = rs_ref[...] * (dzh - m1 - zhv * m2)
        dz_ref[...] = dz
        dzb_ref[...] = dz.astype(BF16)

    row = lambda i: (i, 0)
    fix = lambda i: (0, 0)
    return _pallas(
        body, name=name, grid=(M // tm,),
        out_shape=(pltpu.HBM((M, D), F32), pltpu.HBM((M, D), BF16),
                   pltpu.HBM((1, D), F32), pltpu.HBM((1, D), F32)),
        in_specs=[pl.BlockSpec((tm, D), row), pl.BlockSpec((tm, D), row), pl.BlockSpec((tm, 1), row),
                  pl.BlockSpec((1, D), fix)],
        out_specs=(pl.BlockSpec((tm, D), row), pl.BlockSpec((tm, D), row), pl.BlockSpec((1, D), fix),
                   pl.BlockSpec((1, D), fix)),
        compiler_params=_params(4 * _nbytes((tm, D), F32), ("arbitrary",)))(dy, zh, rstd, gamma)


def mm_nt(a_list, b, out_dtype, *, tm, tk, name, res=None, alpha=None, rider=None):
    M = a_list[0].shape[0]
    widths = [a.shape[1] for a in a_list]
    sharded = b.ndim == 3
    if sharded:
        nsh, K, ns = b.shape
        b_spec = pl.BlockSpec((nsh, tk, ns), lambda i, j: (0, j, 0))
        for w in widths:
            assert w % ns == 0
    else:
        K, N = b.shape
        ns = None
        b_spec = pl.BlockSpec((tk, N), lambda i, j: (j, 0))
    n_a = len(a_list)

    def body(*refs):
        a_refs, b_ref = refs[:n_a], refs[n_a]
        o_ref = refs[-1]
        acc = None
        off = 0
        for a_ref, w in zip(a_refs, widths):
            if sharded:
                for p in range(w // ns):
                    t = _dot_nt(a_ref[:, p * ns:(p + 1) * ns].astype(BF16), b_ref[off // ns + p])
                    acc = t if acc is None else acc + t
            else:
                t = _dot_nt(a_ref[...].astype(BF16), b_ref[:, off:off + w])
                acc = t if acc is None else acc + t
            off += w
        if res is not None:
            acc = acc + alpha * refs[n_a + 1][...]
        o_ref[...] = acc.astype(o_ref.dtype)

    in_specs = [pl.BlockSpec((tm, w), lambda i, j: (i, 0)) for w in widths] + [b_spec]
    args = list(a_list) + [b]
    if res is not None:
        in_specs.append(pl.BlockSpec((tm, tk), lambda i, j: (i, j)))
        args.append(res)
    blk = (sum(_nbytes((tm, w), a.dtype) for a, w in zip(a_list, widths)) + _nbytes((tk, sum(widths)), BF16)
           + 2 * _nbytes((tm, tk), F32))
    in_specs, out_specs, out_shape, scratch = _carry_specs(
        rider, in_specs, (pl.BlockSpec((tm, tk), lambda i, j: (i, j)),), (pltpu.HBM((M, K), out_dtype),), [])
    first = lambda: (pl.program_id(0) == 0) & (pl.program_id(1) == 0)
    last = lambda: (pl.program_id(0) == M // tm - 1) & (pl.program_id(1) == K // tk - 1)
    res_all = _pallas(
        _carry(rider, body, len(args), 1, first, last), name=name, out_shape=out_shape, grid=(M // tm, K // tk),
        in_specs=in_specs, out_specs=out_specs, scratch_shapes=scratch,
        compiler_params=_params(blk, ("arbitrary", "arbitrary")))(*args, *(rider.arrays if rider else ()))
    return res_all[0] if rider is None else (res_all[0], list(res_all[1:]))


def mm_nt_ln_bwd(a_list, b, res, alpha, zh, rstd, gamma, *, tm, name):
    M, D = res.shape
    widths = [a.shape[1] for a in a_list]
    sharded = b.ndim == 3
    if sharded:
        nsh, _, ns = b.shape
        b_spec = pl.BlockSpec((nsh, D, ns), lambda i: (0, 0, 0))
    else:
        ns = None
        b_spec = pl.BlockSpec((D, b.shape[1]), lambda i: (0, 0))
    n_a = len(a_list)

    def body(*refs):
        a_refs, b_ref = refs[:n_a], refs[n_a]
        res_ref, zh_ref, rs_ref, g_ref = refs[n_a + 1:n_a + 5]
        dz_ref, dzb_ref, dg_ref, db_ref = refs[n_a + 5:]

        @pl.when(pl.program_id(0) == 0)
        def _():
            dg_ref[...] = jnp.zeros_like(dg_ref)
            db_ref[...] = jnp.zeros_like(db_ref)

        dy = alpha * res_ref[...]
        off = 0
        for a_ref, w in zip(a_refs, widths):
            if sharded:
                for p in range(w // ns):
                    dy = dy + _dot_nt(a_ref[:, p * ns:(p + 1) * ns], b_ref[off // ns + p])
            else:
                dy = dy + _dot_nt(a_ref[...], b_ref[:, off:off + w])
            off += w
        zhv = zh_ref[...]
        dg_ref[...] += jnp.sum(dy * zhv, axis=0, keepdims=True)
        db_ref[...] += jnp.sum(dy, axis=0, keepdims=True)
        dzh = dy * g_ref[...]
        m1 = jnp.mean(dzh, axis=-1, keepdims=True)
        m2 = jnp.mean(dzh * zhv, axis=-1, keepdims=True)
        dz = rs_ref[...] * (dzh - m1 - zhv * m2)
        dz_ref[...] = dz
        dzb_ref[...] = dz.astype(BF16)

    row = lambda i: (i, 0)
    fix = lambda i: (0, 0)
    in_specs = [pl.BlockSpec((tm, w), row) for w in widths] + [
        b_spec, pl.BlockSpec((tm, D), row), pl.BlockSpec((tm, D), row), pl.BlockSpec((tm, 1), row),
        pl.BlockSpec((1, D), fix)]
    blk = (sum(_nbytes((tm, w), BF16) for w in widths) + _nbytes((D, sum(widths)), BF16)
           + 5 * _nbytes((tm, D), F32))
    return _pallas(
        body, name=name, grid=(M // tm,),
        out_shape=(pltpu.HBM((M, D), F32), pltpu.HBM((M, D), BF16), pltpu.HBM((1, D), F32),
                   pltpu.HBM((1, D), F32)),
        in_specs=in_specs,
        out_specs=(pl.BlockSpec((tm, D), row), pl.BlockSpec((tm, D), row), pl.BlockSpec((1, D), fix),
                   pl.BlockSpec((1, D), fix)),
        compiler_params=_params(blk, ("arbitrary",)))(*a_list, b, res, zh, rstd, gamma)


def mm_tn(a, b_list, *, tk, tn, name, shard_width=None, tmc=None):
    M, K = a.shape
    tmc = M if tmc is None else tmc
    nm = M // tmc
    widths = [b.shape[1] for b in b_list]
    N = sum(widths)
    starts, s = [], 0
    for w in widths:
        assert w % tn == 0
        starts.append(s)
        s += w // tn
    n_b = len(b_list)

    def body(*refs):
        a_ref, b_refs, o_ref, acc = refs[0], refs[1:1 + n_b], refs[-2], refs[-1]
        j, m = pl.program_id(1), pl.program_id(2)
        for b_ref, st, w in zip(b_refs, starts, widths):
            @pl.when((j >= st) & (j < st + w // tn))
            def _(b_ref=b_ref):
                t = _dot_tn(a_ref[...].astype(BF16), b_ref[...].astype(BF16))
                if nm == 1:
                    o_ref[...] = t.astype(o_ref.dtype)
                else:
                    @pl.when(m == 0)
                    def _():
                        acc[...] = t

                    @pl.when(m > 0)
                    def _():
                        acc[...] += t

                    @pl.when(m == nm - 1)
                    def _():
                        o_ref[...] = acc[...].astype(o_ref.dtype)

    def b_map(st, w):
        nb = w // tn
        return lambda i, j, m: (jnp.where((j >= st) & (j < st + nb), m, 0), jnp.clip(j - st, 0, nb - 1))

    in_specs = [pl.BlockSpec((tmc, tk), lambda i, j, m: (m, i))]
    in_specs += [pl.BlockSpec((tmc, tn), b_map(st, w)) for st, w in zip(starts, widths)]
    if shard_width is None:
        out_shape = pltpu.HBM((K, N), BF16)
        out_spec = pl.BlockSpec((tk, tn), lambda i, j, m: (i, j))
    else:
        per = shard_width // tn
        out_shape = pltpu.HBM((N // shard_width, K, shard_width), BF16)
        out_spec = pl.BlockSpec((None, tk, tn), lambda i, j, m: (j // per, i, j % per))
    acc_shape = (tk, tn) if nm > 1 else (8, LANES)
    blk = (_nbytes((tmc, tk), a.dtype) + n_b * _nbytes((tmc, tn), b_list[0].dtype) + 2 * _nbytes((tk, tn), F32))
    return _pallas(
        body, name=name, out_shape=out_shape, grid=(K // tk, N // tn, nm), in_specs=in_specs, out_specs=out_spec,
        scratch_shapes=[pltpu.VMEM(acc_shape, F32)],
        compiler_params=_params(blk, ("parallel", "arbitrary", "arbitrary")))(a, *b_list)


CONV_PAD = 32
CONV_CHUNK = 128


def _rows(win, off, n, shifts):
    b, a = off % 8, off // 8
    if b not in shifts:
        shifts[b] = win if b == 0 else win[b:b + n + CONV_PAD - 8, :]
    return shifts[b][8 * a:8 * a + n, :]


def _by_residue(n_taps, offset):
    return sorted(range(n_taps), key=lambda k: (offset(k) % 8, k))


def conv_fwd(proj, conv_w, conv_b, *, name, rider=None):
    S = proj.shape[0]
    KW, C = conv_w.shape
    nct = C // LANES
    rc = min(CONV_CHUNK, S)

    def body(a_ref, g_ref, w_ref, b_ref, o_ref, pad):
        pad[0:CONV_PAD, :] = jnp.zeros((CONV_PAD, LANES), F32)
        pad[CONV_PAD:, :] = a_ref[...] * _sigmoid(g_ref[...])
        w = w_ref[...]
        bias = b_ref[...]

        def chunk(i, _):
            base = pl.multiple_of(i * rc, rc)
            win = pad[pl.ds(base, rc + CONV_PAD), :]
            acc = jnp.zeros((rc, LANES), F32) + bias
            shifts = {}
            for k in _by_residue(KW, lambda k: CONV_PAD - (KW - 1) + k):
                acc = acc + w[k:k + 1, :] * _rows(win, CONV_PAD - (KW - 1) + k, rc, shifts)
            o_ref[pl.ds(base, rc), :] = acc
            return 0

        lax.fori_loop(0, S // rc, chunk, 0)

    in_specs, out_specs, out_shape, scratch = _carry_specs(
        rider, [pl.BlockSpec((S, LANES), lambda c: (0, c)), pl.BlockSpec((S, LANES), lambda c: (0, c + nct)),
                pl.BlockSpec((KW, LANES), lambda c: (0, c)), pl.BlockSpec((1, LANES), lambda c: (0, c))],
        (pl.BlockSpec((S, LANES), lambda c: (0, c)),), (pltpu.HBM((S, C), F32),),
        [pltpu.VMEM((S + CONV_PAD, LANES), F32)])
    first = lambda: pl.program_id(0) == 0
    last = lambda: pl.program_id(0) == nct - 1
    res = _pallas(
        _carry(rider, body, 4, 1, first, last), name=name, grid=(nct,), out_shape=out_shape,
        in_specs=in_specs, out_specs=out_specs, scratch_shapes=scratch,
        compiler_params=_params(4 * _nbytes((S, LANES), F32), ("arbitrary",)))(
            proj, proj, conv_w, conv_b, *(rider.arrays if rider else ()))
    return res[0], list(res[1:])


def conv_bwd(du1, proj, conv_w, *, name, rider=None):
    S = proj.shape[0]
    KW, C = conv_w.shape
    nct = C // LANES
    rc = min(CONV_CHUNK, S)

    def body(d_ref, a_ref, g_ref, w_ref, da_ref, dg_ref, dw_ref, db_ref, pad_u, pad_d, du0, dw_acc):
        dw_acc[...] = jnp.zeros_like(dw_acc)
        pad_u[0:CONV_PAD, :] = jnp.zeros((CONV_PAD, LANES), F32)
        pad_u[CONV_PAD:, :] = a_ref[...] * _sigmoid(g_ref[...])
        pad_d[0:S, :] = d_ref[...]
        pad_d[S:, :] = jnp.zeros((CONV_PAD, LANES), F32)
        w = w_ref[...]
        db_ref[...] = jnp.sum(d_ref[...], axis=0, keepdims=True)

        def chunk(i, _):
            base = pl.multiple_of(i * rc, rc)
            d = pad_d[pl.ds(base, rc), :]
            win_u = pad_u[pl.ds(base, rc + CONV_PAD), :]
            win_d = pad_d[pl.ds(base, rc + CONV_PAD), :]
            shifts = {}
            for k in _by_residue(KW, lambda k: CONV_PAD - (KW - 1) + k):
                u_k = _rows(win_u, CONV_PAD - (KW - 1) + k, rc, shifts)
                dw_acc[k:k + 1, :] += jnp.sum(d * u_k, axis=0, keepdims=True)
            acc = jnp.zeros((rc, LANES), F32)
            shifts = {}
            for k in _by_residue(KW, lambda k: KW - 1 - k):
                acc = acc + w[k:k + 1, :] * _rows(win_d, KW - 1 - k, rc, shifts)
            du0[pl.ds(base, rc), :] = acc
            return 0

        lax.fori_loop(0, S // rc, chunk, 0)
        dw_ref[...] = dw_acc[0:KW, :]
        a, sg = a_ref[...], _sigmoid(g_ref[...])
        d0 = du0[...]
        da_ref[...] = (d0 * sg).astype(BF16)
        dg_ref[...] = (d0 * a * sg * (1.0 - sg)).astype(BF16)

    col = lambda c: (0, c)
    in_specs, out_specs, out_shape, scratch = _carry_specs(
        rider, [pl.BlockSpec((S, LANES), col), pl.BlockSpec((S, LANES), col),
                pl.BlockSpec((S, LANES), lambda c: (0, c + nct)), pl.BlockSpec((KW, LANES), col)],
        (pl.BlockSpec((S, LANES), col), pl.BlockSpec((S, LANES), col), pl.BlockSpec((KW, LANES), col),
         pl.BlockSpec((1, LANES), col)),
        (pltpu.HBM((S, C), BF16), pltpu.HBM((S, C), BF16), pltpu.HBM((KW, C), F32), pltpu.HBM((1, C), F32)),
        [pltpu.VMEM((S + CONV_PAD, LANES), F32), pltpu.VMEM((S + CONV_PAD, LANES), F32),
         pltpu.VMEM((S, LANES), F32), pltpu.VMEM((CONV_PAD, LANES), F32)])
    first = lambda: pl.program_id(0) == 0
    last = lambda: pl.program_id(0) == nct - 1
    res = _pallas(
        _carry(rider, body, 4, 4, first, last), name=name, grid=(nct,), out_shape=out_shape,
        in_specs=in_specs, out_specs=out_specs, scratch_shapes=scratch,
        compiler_params=_params(8 * _nbytes((S, LANES), F32), ("arbitrary",)))(
            du1, proj, proj, conv_w, *(rider.arrays if rider else ()))
    return res[:4], list(res[4:])


def ln_silu(u1, o_sb, gamma, beta, *, tm, name):
    S, C = u1.shape

    def body(u_ref, o_ref, g_ref, b_ref, out_ref):
        z = u_ref[...]
        mu = jnp.mean(z, axis=-1, keepdims=True)
        zc = z - mu
        y = zc * lax.rsqrt(jnp.mean(zc * zc, axis=-1, keepdims=True) + LN_EPS) * g_ref[...] + b_ref[...]
        out_ref[:, 0:C] = (y * _sigmoid(y)).astype(BF16)
        out_ref[:, C:] = o_ref[...].astype(BF16)

    row = lambda i: (i, 0)
    fix = lambda i: (0, 0)
    return _pallas(
        body, name=name, out_shape=pltpu.HBM((S, 2 * C), BF16), grid=(S // tm,),
        in_specs=[pl.BlockSpec((tm, C), row), pl.BlockSpec((tm, C), row), pl.BlockSpec((1, C), fix),
                  pl.BlockSpec((1, C), fix)],
        out_specs=pl.BlockSpec((tm, 2 * C), row),
        compiler_params=_params(4 * _nbytes((tm, C), F32), ("parallel",)))(u1, o_sb, gamma, beta)


def ln_silu_bwd(dua, u1, gamma, beta, *, tm, name):
    S, C = u1.shape

    def body(d_ref, u_ref, g_ref, b_ref, du1_ref, dg_ref, db_ref):
        @pl.when(pl.program_id(0) == 0)
        def _():
            dg_ref[...] = jnp.zeros_like(dg_ref)
            db_ref[...] = jnp.zeros_like(db_ref)

        z = u_ref[...]
        mu = jnp.mean(z, axis=-1, keepdims=True)
        zc = z - mu
        rstd = lax.rsqrt(jnp.mean(zc * zc, axis=-1, keepdims=True) + LN_EPS)
        zh = zc * rstd
        y = zh * g_ref[...] + b_ref[...]
        sg = _sigmoid(y)
        dy = d_ref[...] * (sg * (1.0 + y * (1.0 - sg)))
        dg_ref[...] += jnp.sum(dy * zh, axis=0, keepdims=True)
        db_ref[...] += jnp.sum(dy, axis=0, keepdims=True)
        dzh = dy * g_ref[...]
        m1 = jnp.mean(dzh, axis=-1, keepdims=True)
        m2 = jnp.mean(dzh * zh, axis=-1, keepdims=True)
        du1_ref[...] = rstd * (dzh - m1 - zh * m2)

    row = lambda i: (i, 0)
    fix = lambda i: (0, 0)
    return _pallas(
        body, name=name, grid=(S // tm,),
        out_shape=(pltpu.HBM((S, C), F32), pltpu.HBM((1, C), F32),
                   pltpu.HBM((1, C), F32)),
        in_specs=[pl.BlockSpec((tm, C), row), pl.BlockSpec((tm, C), row), pl.BlockSpec((1, C), fix),
                  pl.BlockSpec((1, C), fix)],
        out_specs=(pl.BlockSpec((tm, C), row), pl.BlockSpec((1, C), fix), pl.BlockSpec((1, C), fix)),
        compiler_params=_params(4 * _nbytes((tm, C), F32), ("arbitrary",)))(dua, u1, gamma, beta)


SB_BLOCK = 256
SB_STOP = -105.0
SB_GROUP = 4


def _split_dot(x, tri):
    hi = x.astype(BF16)
    lo = (x - hi.astype(F32)).astype(BF16)
    return _dot(hi, tri) + _dot(lo, tri)


def _neg_softplus(z):
    return -(jnp.maximum(z, 0.0) + jnp.log(1.0 + jnp.exp(-jnp.abs(z))))


def sb_fwd(proj, *, q_col, name, rider=None):
    S = proj.shape[0]
    dh = LANES // 2
    W = SB_HEADS * dh
    BW = SB_GROUP * dh
    ngrp = W // BW
    T = min(SB_BLOCK, S)
    nblk = S // T
    scale = dh ** -0.5
    qb0 = q_col // BW
    heads = range(SB_GROUP)
    sl = [slice(h * dh, (h + 1) * dh) for h in heads]

    def body(q_ref, k_ref, v_ref, o_ref, l_ref, qs):
        r_i = lax.broadcasted_iota(jnp.int32, (T, T), 0)
        c_i = lax.broadcasted_iota(jnp.int32, (T, T), 1)
        tri = (r_i >= c_i).astype(BF16)
        vis = c_i < r_i
        lane = lax.broadcasted_iota(jnp.int32, (T, dh), 1)

        qs[...] = (q_ref[...] * scale).astype(BF16)

        def step(qb, blocks, st):
            nb = range(len(blocks))
            kb = [[k_ref[pl.ds(j0, T), sl[h]].astype(BF16) for h in heads] for j0, _ in blocks]
            vb = [[v_ref[pl.ds(j0, T), sl[h]].astype(BF16) for h in heads] for j0, _ in blocks]
            z = [[_dot_nt(qb[h], kb[b][h]) for h in heads] for b in nb]
            lk = [[_neg_softplus(z[b][h]) for h in heads] for b in nb]
            lk = [[jnp.where(vis, lk[b][h], 0.0) if blocks[b][1] else lk[b][h] for h in heads] for b in nb]
            C = [[_split_dot(lk[b][h], tri) for h in heads] for b in nb]
            R = [[st[2 * h + 1] for h in heads]]
            for b in nb:
                R.append([R[b][h] + C[b][h][:, 0:1] for h in heads])
            A = [[jnp.exp(z[b][h] + C[b][h] + R[b][h]) for h in heads] for b in nb]
            A = [[jnp.where(vis, A[b][h], 0.0) if blocks[b][1] else A[b][h] for h in heads] for b in nb]
            out = ()
            for h in heads:
                acc = st[2 * h]
                for b in nb:
                    acc = acc + _dot(A[b][h].astype(BF16), vb[b][h])
                out += (acc, R[-1][h])
            return out

        zero = (jnp.zeros((T, dh), F32), jnp.zeros((T, 1), F32))

        def finish(r0, i, c):
            walked = jnp.asarray(i - c[0]).astype(F32)
            for h in heads:
                o_ref[pl.ds(r0, T), sl[h]] = c[1 + 2 * h]
                l_ref[pl.ds(r0, T), sl[h]] = jnp.where(lane == 1, walked, c[2 + 2 * h])

        finish(0, 0, (-1,) + step([qs[0:T, sl[h]] for h in heads], [(0, True)], zero * SB_GROUP))

        def qblock(i, _):
            r0 = pl.multiple_of(i * T, T)
            qb = [qs[pl.ds(r0, T), sl[h]] for h in heads]
            state = step(qb, [(r0, True), (pl.multiple_of(r0 - T, T), False)], zero * SB_GROUP)

            def more(c):
                worst = c[2]
                for h in heads[1:]:
                    worst = jnp.maximum(worst, c[2 + 2 * h])
                return (c[0] >= 0) & (jnp.max(worst) >= SB_STOP)

            def walk(c):
                return (c[0] - 1,) + step(qb, [(pl.multiple_of(c[0] * T, T), False)], c[1:])

            finish(r0, i, lax.while_loop(more, walk, (i - 2,) + state))
            return 0

        lax.fori_loop(1, nblk, qblock, 0)

    blk = lambda off: pl.BlockSpec((S, BW), lambda g: (0, qb0 + off * ngrp + g), pipeline_mode=pl.Buffered(1))
    out = pl.BlockSpec((S, BW), lambda g: (0, g))
    in_specs, out_specs, out_shape, scratch = _carry_specs(
        rider, [blk(0), blk(1), blk(2)], (out, out), (pltpu.HBM((S, W), F32), pltpu.HBM((S, W), F32)),
        [pltpu.VMEM((S, BW), BF16)])
    first = lambda: pl.program_id(0) == 0
    last = lambda: pl.program_id(0) == ngrp - 1
    res = _pallas(
        _carry(rider, body, 3, 2, first, last), name=name, grid=(ngrp,), out_shape=out_shape,
        in_specs=in_specs, out_specs=out_specs, scratch_shapes=scratch,
        compiler_params=_params(5 * _nbytes((S, BW), F32), ("arbitrary",)))(
            proj, proj, proj, *(rider.arrays if rider else ()))
    return res[0], res[1], list(res[2:])


def sb_bwd(proj, ltot, dua, *, q_col, do_col, name, rider=None):
    S = proj.shape[0]
    dh = LANES // 2
    W = SB_HEADS * dh
    BW = SB_GROUP * dh
    ngrp = W // BW
    T = min(SB_BLOCK, S)
    nblk = S // T
    scale = dh ** -0.5
    qb0 = q_col // BW
    db0 = do_col // BW
    heads = range(SB_GROUP)
    sl = [slice(h * dh, (h + 1) * dh) for h in heads]

    def body(q_ref, k_ref, v_ref, l_ref, do_ref, dq_ref, dk_ref, dv_ref, dks, dvs):
        r_i = lax.broadcasted_iota(jnp.int32, (T, T), 0)
        c_i = lax.broadcasted_iota(jnp.int32, (T, T), 1)
        tri_rev = (r_i >= c_i).astype(BF16)
        tri_fwd = (r_i <= c_i).astype(BF16)
        vis = c_i < r_i

        dks[...] = jnp.zeros_like(dks)
        dvs[...] = jnp.zeros_like(dvs)

        def step(qb, dob, Lt, blocks, st):
            nb = range(len(blocks))
            kb = [[k_ref[pl.ds(j0, T), sl[h]].astype(BF16) for h in heads] for j0, _ in blocks]
            vb = [[v_ref[pl.ds(j0, T), sl[h]].astype(BF16) for h in heads] for j0, _ in blocks]
            z = [[_dot_nt(qb[h], kb[b][h]) for h in heads] for b in nb]
            dA =[[_dot_nt(dob[h], vb[b][h]) for h in heads] for b in nb]
            lk = [[_neg_softplus(z[b][h]) for h in heads] for b in nb]
            beta = [[jnp.exp(z[b][h] + lk[b][h]) for h in heads] for b in nb]
            lk = [[jnp.where(vis, lk[b][h], 0.0) if blocks[b][1] else lk[b][h] for h in heads] for b in nb]
            C = [[_split_dot(lk[b][h], tri_rev) for h in heads] for b in nb]
            P = [[st[3 * h + 1] for h in heads]]
            for b in nb:
                P.append([P[b][h] + C[b][h][:, 0:1] for h in heads])
            A = [[jnp.exp(z[b][h] + C[b][h] + (Lt[h] - P[b + 1][h])) for h in heads] for b in nb]
            A = [[jnp.where(vis, A[b][h], 0.0) if blocks[b][1] else A[b][h] for h in heads] for b in nb]
            g = [[A[b][h] * dA[b][h] for h in heads] for b in nb]
            Gin = [[_split_dot(g[b][h], tri_fwd) for h in heads] for b in nb]
            Gp = [[st[3 * h + 2] for h in heads]]
            for b in nb:
                Gp.append([Gp[b][h] + Gin[b][h][:, T - 1:T] for h in heads])
            dz = [[g[b][h] - beta[b][h] * (Gp[b][h] + Gin[b][h]) for h in heads] for b in nb]
            dz = [[jnp.where(vis, dz[b][h], 0.0) if blocks[b][1] else dz[b][h] for h in heads] for b in nb]
            dzb = [[dz[b][h].astype(BF16) for h in heads] for b in nb]
            out = ()
            for h in heads:
                dq = st[3 * h]
                for b in nb:
                    j0 = blocks[b][0]
                    dvs[pl.ds(j0, T), sl[h]] += _dot_tn(A[b][h].astype(BF16), dob[h])
                    dks[pl.ds(j0, T), sl[h]] += _dot_tn(dzb[b][h], qb[h])
                    dq = dq + _dot(dzb[b][h], kb[b][h])
                out += (dq, P[-1][h], Gp[-1][h])
            return out

        zero = jnp.zeros((T, 1), F32)
        init = (jnp.zeros((T, dh), F32), zero, zero)

        def operands(r0):
            return ([(q_ref[pl.ds(r0, T), sl[h]] * scale).astype(BF16) for h in heads],
                    [do_ref[pl.ds(r0, T), sl[h]].astype(BF16) for h in heads],
                    [l_ref[pl.ds(r0, T), h * dh:h * dh + 1] for h in heads])

        def finish(r0, c):
            for h in heads:
                dq_ref[pl.ds(r0, T), sl[h]] = (c[3 * h] * scale).astype(BF16)

        finish(0, step(*operands(0), [(0, True)], init * SB_GROUP))

        def qblock(i, _):
            r0 = pl.multiple_of(i * T, T)
            qb, dob, Lt = operands(r0)
            walked = jnp.clip(jnp.max(l_ref[pl.ds(r0, 8), 1:2]).astype(jnp.int32), 2, i + 1)

            def inner(j, c):
                return step(qb, dob, Lt, [(pl.multiple_of(j * T, T), False)], c)

            c = lax.fori_loop(i + 1 - walked, i - 1, inner, init * SB_GROUP)
            finish(r0, step(qb, dob, Lt, [(pl.multiple_of(r0 - T, T), False), (r0, True)], c))
            return 0

        lax.fori_loop(1, nblk, qblock, 0)
        dk_ref[...] = dks[...].astype(BF16)
        dv_ref[...] = dvs[...].astype(BF16)

    once = pl.Buffered(1)
    blk = lambda off: pl.BlockSpec((S, BW), lambda g: (0, qb0 + off * ngrp + g), pipeline_mode=once)
    out = pl.BlockSpec((S, BW), lambda g: (0, g))
    o_shape = pltpu.HBM((S, W), BF16)
    in_specs, out_specs, out_shape, scratch = _carry_specs(
        rider, [blk(0), blk(1), blk(2), pl.BlockSpec((S, BW), lambda g: (0, g), pipeline_mode=once),
                pl.BlockSpec((S, BW), lambda g: (0, db0 + g), pipeline_mode=once)], (out, out, out),
        (o_shape, o_shape, o_shape), [pltpu.VMEM((S, BW), F32)] * 2)
    first = lambda: pl.program_id(0) == 0
    last = lambda: pl.program_id(0) == ngrp - 1
    res = _pallas(
        _carry(rider, body, 5, 3, first, last), name=name, grid=(ngrp,), out_shape=out_shape,
        in_specs=in_specs, out_specs=out_specs, scratch_shapes=scratch,
        compiler_params=_params(6 * _nbytes((S, BW), F32), ("arbitrary",)))(
            proj, proj, proj, ltot, dua, *(rider.arrays if rider else ()))
    return res[0], res[1], res[2], list(res[3:])


def xattn_fwd(q, k, v, *, tm, name):
    S, D = q.shape
    Mlen = k.shape[0]
    hd = D // MEM_HEADS
    scale = hd ** -0.5

    def body(q_ref, k_ref, v_ref, o_ref):
        for h in range(MEM_HEADS):
            sl = slice(h * hd, (h + 1) * hd)
            s = _dot_nt(q_ref[:, sl], k_ref[:, sl]) * scale
            e = jnp.exp(s - jnp.max(s, axis=-1, keepdims=True))
            p = e / jnp.sum(e, axis=-1, keepdims=True)
            o_ref[:, sl] = _dot(p.astype(BF16), v_ref[:, sl]).astype(BF16)

    row = lambda i: (i, 0)
    fix = lambda i: (0, 0)
    return _pallas(
        body, name=name, out_shape=pltpu.HBM((S, D), BF16), grid=(S // tm,),
        in_specs=[pl.BlockSpec((tm, D), row), pl.BlockSpec((Mlen, D), fix), pl.BlockSpec((Mlen, D), fix)],
        out_specs=pl.BlockSpec((tm, D), row),
        compiler_params=_params(4 * _nbytes((tm, D), F32), ("parallel",)))(q, k, v)


def xattn_bwd(q, do, k, v, *, tm, name):
    S, D = q.shape
    Mlen = k.shape[0]
    hd = D // MEM_HEADS
    scale = hd ** -0.5

    def body(q_ref, do_ref, k_ref, v_ref, dq_ref, dk_ref, dv_ref):
        @pl.when(pl.program_id(0) == 0)
        def _():
            dk_ref[...] = jnp.zeros_like(dk_ref)
            dv_ref[...] = jnp.zeros_like(dv_ref)

        for h in range(MEM_HEADS):
            sl = slice(h * hd, (h + 1) * hd)
            qh, doh, kh, vh = q_ref[:, sl], do_ref[:, sl], k_ref[:, sl], v_ref[:, sl]
            s = _dot_nt(qh, kh) * scale
            e = jnp.exp(s - jnp.max(s, axis=-1, keepdims=True))
            p = e / jnp.sum(e, axis=-1, keepdims=True)
            dp = _dot_nt(doh, vh)
            ds = (p * (dp - jnp.sum(p * dp, axis=-1, keepdims=True)) * scale).astype(BF16)
            dq_ref[:, sl] = _dot(ds, kh).astype(BF16)
            dk_ref[:, sl] += _dot_tn(ds, qh)
            dv_ref[:, sl] += _dot_tn(p.astype(BF16), doh)

    row = lambda i: (i, 0)
    fix = lambda i: (0, 0)
    return _pallas(
        body, name=name, grid=(S // tm,),
        out_shape=(pltpu.HBM((S, D), BF16), pltpu.HBM((Mlen, D), F32),
                   pltpu.HBM((Mlen, D), F32)),
        in_specs=[pl.BlockSpec((tm, D), row), pl.BlockSpec((tm, D), row), pl.BlockSpec((Mlen, D), fix),
                  pl.BlockSpec((Mlen, D), fix)],
        out_specs=(pl.BlockSpec((tm, D), row), pl.BlockSpec((Mlen, D), fix), pl.BlockSpec((Mlen, D), fix)),
        compiler_params=_params(6 * _nbytes((tm, D), F32), ("arbitrary",)))(q, do, k, v)


FFN_HALO = 8


def _conv3(ext, w, lo):
    tm = ext.shape[0] - FFN_HALO
    return (w[0:1, :] * ext[lo:lo + tm, :] + w[1:2, :] * ext[lo + 1:lo + 1 + tm, :]
            + w[2:3, :] * ext[lo + 2:lo + 2 + tm, :])


def ffn_up_fwd(xb, w_up, conv_w, conv_b, *, tm, tn, name, rider=None):
    S, D = xb.shape
    nsh, _, ns = w_up.shape
    F = nsh * ns // 2
    per = ns // tn
    ncol = F // tn
    KW = conv_w.shape[0]
    assert KW == 3

    def body(x_ref, wv_ref, wg_ref, cwv_ref, cwg_ref, cbv_ref, cbg_ref, uv_ref, ug_ref, mv_ref, mg_ref, h_ref,
             carry):
        @pl.when(pl.program_id(1) == 0)
        def _():
            carry[...] = jnp.zeros_like(carry)

        x = x_ref[...]
        uv = _dot(x, wv_ref[...])
        ug = _dot(x, wg_ref[...])
        uv_ref[...] = uv.astype(BF16)
        ug_ref[...] = ug.astype(BF16)
        lo = FFN_HALO - (KW - 1)
        cv = _conv3(jnp.concatenate([carry[0], uv], axis=0), cwv_ref[...], lo) + cbv_ref[...]
        cg = _conv3(jnp.concatenate([carry[1], ug], axis=0), cwg_ref[...], lo) + cbg_ref[...]
        carry[0] = uv[tm - FFN_HALO:, :]
        carry[1] = ug[tm - FFN_HALO:, :]
        sg = _sigmoid(cg)
        act = cg * sg
        mv_ref[...] = act.astype(BF16)
        mg_ref[...] = (cv * (sg + act * (1.0 - sg))).astype(BF16)
        h_ref[...] = (act * cv).astype(BF16)

    wspec = lambda half: pl.BlockSpec((None, D, tn), lambda j, i: (half * (nsh // 2) + j // per, 0, j % per))
    cspec = lambda rows, half: pl.BlockSpec((rows, tn), lambda j, i: (0, half * ncol + j))
    out = pl.BlockSpec((tm, tn), lambda j, i: (i, j))
    o_shape = pltpu.HBM((S, F), BF16)
    blk = _nbytes((tm, D), BF16) + 2 * _nbytes((D, tn), BF16) + 8 * _nbytes((tm, tn), F32)
    nrow = S // tm
    in_specs, out_specs, out_shape, scratch = _carry_specs(
        rider, [pl.BlockSpec((tm, D), lambda j, i: (i, 0)), wspec(0), wspec(1), cspec(KW, 0), cspec(KW, 1),
                cspec(1, 0), cspec(1, 1)], (out,) * 5, (o_shape,) * 5, [pltpu.VMEM((2, FFN_HALO, tn), F32)])
    first = lambda: (pl.program_id(0) == 0) & (pl.program_id(1) == 0)
    last = lambda: (pl.program_id(0) == ncol - 1) & (pl.program_id(1) == nrow - 1)
    res = _pallas(
        _carry(rider, body, 7, 5, first, last), name=name, grid=(ncol, nrow), out_shape=out_shape,
        in_specs=in_specs, out_specs=out_specs, scratch_shapes=scratch,
        compiler_params=_params(blk, ("arbitrary", "arbitrary")))(
            xb, w_up, w_up, conv_w, conv_w, conv_b, conv_b, *(rider.arrays if rider else ()))
    return res[:5], list(res[5:])


def ffn_mid_bwd(dzb, w_down, up_v, up_g, mult_v, mult_g, conv_w, *, tm, tn, name, rider=None):
    S, D = dzb.shape
    F = up_v.shape[1]
    ncol = F // tn
    nrow = S // tm
    KW = conv_w.shape[0]
    assert KW == 3

    def body(dz_ref, wd_ref, uv_ref, ug_ref, mv_ref, mg_ref, cwv_ref, cwg_ref,
             dv_ref, dg_ref, dwv_ref, dwg_ref, dbv_ref, dbg_ref, carry):
        @pl.when(pl.program_id(1) == 0)
        def _():
            carry[...] = jnp.zeros_like(carry)
            for r in (dwv_ref, dwg_ref, dbv_ref, dbg_ref):
                r[...] = jnp.zeros_like(r)

        dh = _dot_nt(dz_ref[...], wd_ref[...])
        dcv = dh * mv_ref[...].astype(F32)
        dcg = dh * mg_ref[...].astype(F32)

        def back(dc, u_ref, cw, slot, du_ref, dw_ref, db_ref):
            ext = jnp.concatenate([dc, carry[slot]], axis=0)
            ahead = [dc, ext[1:tm + 1, :], ext[2:tm + 2, :]]
            du = cw[2:3, :] * ahead[0] + cw[1:2, :] * ahead[1] + cw[0:1, :] * ahead[2]
            du_ref[...] = du.astype(BF16)
            carry[slot] = dc[0:FFN_HALO, :]
            u = u_ref[...].astype(F32)
            for k in range(KW):
                dw_ref[k:k + 1, :] += jnp.sum(ahead[KW - 1 - k] * u, axis=0, keepdims=True)
            db_ref[...] += jnp.sum(dc, axis=0, keepdims=True)

        back(dcv, uv_ref, cwv_ref[...], 0, dv_ref, dwv_ref, dbv_ref)
        back(dcg, ug_ref, cwg_ref[...], 1, dg_ref, dwg_ref, dbg_ref)

    rev = lambda i: nrow - 1 - i
    tile = pl.BlockSpec((tm, tn), lambda j, i: (rev(i), j))
    cspec = lambda half: pl.BlockSpec((KW, tn), lambda j, i: (0, half * ncol + j))
    acc = lambda rows: pl.BlockSpec((rows, tn), lambda j, i: (0, j))
    big = pltpu.HBM((S, F), BF16)
    blk = _nbytes((tm, D), BF16) + _nbytes((tn, D), BF16) + 10 * _nbytes((tm, tn), F32)
    in_specs, out_specs, out_shape, scratch = _carry_specs(
        rider, [pl.BlockSpec((tm, D), lambda j, i: (rev(i), 0)), pl.BlockSpec((tn, D), lambda j, i: (j, 0)),
                tile, tile, tile, tile, cspec(0), cspec(1)],
        (tile, tile, acc(KW), acc(KW), acc(1), acc(1)),
        (big, big, pltpu.HBM((KW, F), F32), pltpu.HBM((KW, F), F32), pltpu.HBM((1, F), F32),
         pltpu.HBM((1, F), F32)), [pltpu.VMEM((2, FFN_HALO, tn), F32)])
    first = lambda: (pl.program_id(0) == 0) & (pl.program_id(1) == 0)
    last = lambda: (pl.program_id(0) == ncol - 1) & (pl.program_id(1) == nrow - 1)
    res = _pallas(
        _carry(rider, body, 8, 6, first, last), name=name, grid=(ncol, nrow), out_shape=out_shape,
        in_specs=in_specs, out_specs=out_specs, scratch_shapes=scratch,
        compiler_params=_params(blk, ("arbitrary", "arbitrary")))(
            dzb, w_down, up_v, up_g, mult_v, mult_g, conv_w, conv_w, *(rider.arrays if rider else ()))
    return res[:6], list(res[6:])


def loss_head(y, target, *, tm, name):
    S, D = y.shape

    def body(y_ref, t_ref, dy_ref, l_ref):
        @pl.when(pl.program_id(0) == 0)
        def _():
            l_ref[...] = jnp.zeros_like(l_ref)

        e = y_ref[...] - t_ref[...]
        dy_ref[...] = e * (1.0 / D)
        l_ref[...] += 0.5 * jnp.sum(jnp.mean(e * e, axis=-1, keepdims=True), axis=0, keepdims=True)

    row = lambda i: (i, 0)
    return _pallas(
        body, name=name, grid=(S // tm,),
        out_shape=(pltpu.HBM((S, D), F32), pltpu.HBM((1, 1), F32)),
        in_specs=[pl.BlockSpec((tm, D), row), pl.BlockSpec((tm, D), row)],
        out_specs=(pl.BlockSpec((tm, D), row), pl.BlockSpec((1, 1), lambda i: (0, 0))),
        compiler_params=_params(3 * _nbytes((tm, D), F32), ("arbitrary",)))(y, target)


def adamw(w, g, m, v, *, tr, name):
    R, C = w.shape
    c1 = 1.0 - ADAM_B1 ** ADAM_STEP
    c2 = 1.0 - ADAM_B2 ** ADAM_STEP

    def body(w_ref, g_ref, m_ref, v_ref, go_ref, d_ref, mo_ref, vo_ref):
        gv = g_ref[...]
        mn = ADAM_B1 * m_ref[...] + (1.0 - ADAM_B1) * gv
        vn = ADAM_B2 * v_ref[...] + (1.0 - ADAM_B2) * (gv * gv)
        go_ref[...] = gv
        mo_ref[...] = mn
        vo_ref[...] = vn
        d_ref[...] = -ADAM_LR * ((mn / c1) / (jnp.sqrt(vn / c2) + ADAM_EPS) + ADAM_WD * w_ref[...])

    spec = pl.BlockSpec((tr, C), lambda i: (i, 0))
    shape = pltpu.HBM((R, C), F32)
    return _pallas(
        body, name=name, grid=(R // tr,), out_shape=(shape,) * 4, in_specs=[spec] * 4, out_specs=(spec,) * 4,
        compiler_params=_params(8 * _nbytes((tr, C), F32), ("parallel",)))(w, g, m, v)


def add_pairs(gs, gots, core, *, name):
    k = len(gs)

    def body(c_ref, *refs):
        for a_ref, b_ref, o_ref in zip(refs[:k], refs[k:2 * k], refs[2 * k:]):
            o_ref[...] = (a_ref[...].astype(F32) + b_ref[...].astype(F32)).astype(BF16)

    own = [pl.BlockSpec((None, None) + g.shape[2:], lambda i, c: (i, c[0], 0, 0)) for g in gs]
    half = [pl.BlockSpec((None,) + g.shape[1:], lambda i, c: (i, 0, 0)) for g in gots]
    grid_spec = pltpu.PrefetchScalarGridSpec(
        num_scalar_prefetch=1, grid=(N_CHIPS,), in_specs=own + half, out_specs=tuple(half))
    blk = 3 * sum(_nbytes(g.shape[1:], BF16) for g in gots)
    return _pallas(
        body, name=name, grid_spec=grid_spec, out_shape=tuple(pltpu.HBM(g.shape, BF16) for g in gots),
        compiler_params=_params(blk, ("parallel",)))(core, *gs, *gots)


def sum_chips_into(bs, dests, layer, core, *, name):
    k = len(bs)
    steps = 2

    def body(c_ref, *refs):
        for b_ref, o_ref in zip(refs[:k], refs[2 * k:]):
            acc = b_ref[0].astype(F32)
            for p in range(1, N_CHIPS):
                acc = acc + b_ref[p].astype(F32)
            o_ref[...] = acc

    ins = [pl.BlockSpec((N_CHIPS, b.shape[1] // steps, b.shape[2]), lambda i, c: (0, i, 0)) for b in bs]
    outs = tuple(pl.BlockSpec((None, None, b.shape[1] // steps, b.shape[2]), lambda i, c: (layer, c[0], i, 0))
                 for b in bs)
    grid_spec = pltpu.PrefetchScalarGridSpec(
        num_scalar_prefetch=1, grid=(steps,), in_specs=ins + [pl.BlockSpec(memory_space=pl.ANY)] * k,
        out_specs=outs)
    blk = sum(_nbytes(b.shape, BF16) + _nbytes(b.shape[1:], F32) for b in bs) // steps
    return _pallas(
        body, name=name, grid_spec=grid_spec, out_shape=tuple(pltpu.HBM(d.shape, F32) for d in dests),
        input_output_aliases={1 + k + w: w for w in range(k)},
        compiler_params=_params(blk, ("parallel",)))(core, *bs, *dests)


_HBM = pl.BlockSpec(memory_space=pltpu.HBM)


def _place():
    x, y, c = lax.axis_index("x"), lax.axis_index("y"), lax.axis_index("c")
    chips = [(1 - x, y), (x, 1 - y), (1 - x, 1 - y)]
    return x, y, c, chips


class GatherRider:
    def __init__(self, shards):
        self.arrays = list(shards)
        self.n = n = len(shards)
        self.out_shape = tuple(pltpu.HBM((N_CHIPS,) + s.shape, s.dtype) for s in shards)
        self.scratch = [pltpu.SemaphoreType.DMA((n, 3))] * 4 + [pltpu.SemaphoreType.DMA((n,))]

    def _copies(self, ins, outs, sems):
        send_ici, recv_ici, send_d2d, recv_d2d, local = sems
        x, y, c, chips = _place()
        me = 2 * x + y

        def own(w):
            return pltpu.make_async_copy(ins[w], outs[w].at[me], local.at[w])

        def ici(w, j):
            px, py = chips[j]
            return pltpu.make_async_remote_copy(
                src_ref=ins[w].at[c], dst_ref=outs[w].at[me, c], send_sem=send_ici.at[w, j],
                recv_sem=recv_ici.at[w, j], device_id=(px, py, c), device_id_type=MESH)

        def landed(w, j, half):
            px, py = chips[j]
            return outs[w].at[2 * px + py, half]

        def d2d(w, j, half):
            return pltpu.make_async_remote_copy(
                src_ref=landed(w, j, half), dst_ref=landed(w, j, half), send_sem=send_d2d.at[w, j],
                recv_sem=recv_d2d.at[w, j], device_id=(x, y, 1 - c), device_id_type=MESH)

        def ici_arrival(w, j):
            return pltpu.make_async_remote_copy(
                src_ref=landed(w, j, c), dst_ref=landed(w, j, c), send_sem=send_ici.at[w, j],
                recv_sem=recv_ici.at[w, j], device_id=(x, y, c), device_id_type=MESH)

        return c, own, ici, d2d, ici_arrival

    def start(self, ins, outs, sems):
        c, own, ici, d2d, ici_arrival = self._copies(ins, outs, sems)
        for w in range(self.n):
            own(w).start()
            for j in range(3):
                ici(w, j).start()

    def finish(self, ins, outs, sems):
        c, own, ici, d2d, ici_arrival = self._copies(ins, outs, sems)
        for w in range(self.n):
            for j in range(3):
                ici_arrival(w, j).wait_recv()
                d2d(w, j, c).start()
        for w in range(self.n):
            for j in range(3):
                d2d(w, j, 1 - c).wait_recv()
        for w in range(self.n):
            for j in range(3):
                ici(w, j).wait_send()
                d2d(w, j, c).wait_send()
            own(w).wait()


class ScatterRider:
    def __init__(self, parts):
        self.arrays = list(parts)
        self.n = n = len(parts)
        self.out_shape = tuple(pltpu.HBM(p.shape, p.dtype) for p in parts)
        self.scratch = [pltpu.SemaphoreType.DMA((n, 3))] * 2 + [pltpu.SemaphoreType.DMA((n,))]

    def _copies(self, ins, outs, sems):
        send, recv, local = sems
        x, y, c, chips = _place()
        me = 2 * x + y

        def own(w):
            return pltpu.make_async_copy(ins[w].at[me], outs[w].at[me], local.at[w])

        def copy(w, j):
            px, py = chips[j]
            return pltpu.make_async_remote_copy(
                src_ref=ins[w].at[2 * px + py], dst_ref=outs[w].at[me], send_sem=send.at[w, j],
                recv_sem=recv.at[w, j], device_id=(px, py, c), device_id_type=MESH)

        def arrival(w, j):
            px, py = chips[j]
            blk = outs[w].at[2 * px + py]
            return pltpu.make_async_remote_copy(
                src_ref=blk, dst_ref=blk, send_sem=send.at[w, j], recv_sem=recv.at[w, j],
                device_id=(x, y, c), device_id_type=MESH)

        return own, copy, arrival

    def start(self, ins, outs, sems):
        own, copy, arrival = self._copies(ins, outs, sems)
        for w in range(self.n):
            own(w).start()
            for j in range(3):
                copy(w, j).start()

    def finish(self, ins, outs, sems):
        own, copy, arrival = self._copies(ins, outs, sems)
        for w in range(self.n):
            for j in range(3):
                arrival(w, j).wait_recv()
        for w in range(self.n):
            for j in range(3):
                copy(w, j).wait_send()
            own(w).wait()


def _carry(rider, body, n_in, n_out, first, last):
    if rider is None:
        return body
    k, m = rider.n, len(rider.scratch)

    def carried(*refs):
        ins, r_in = refs[:n_in], refs[n_in:n_in + k]
        outs, r_out = refs[n_in + k:n_in + k + n_out], refs[n_in + k + n_out:n_in + 2 * k + n_out]
        rest = refs[n_in + 2 * k + n_out:]
        scratch, sems = rest[:len(rest) - m], rest[len(rest) - m:]

        @pl.when(first())
        def _():
            rider.start(r_in, r_out, sems)

        body(*ins, *outs, *scratch)

        @pl.when(last())
        def _():
            rider.finish(r_in, r_out, sems)

    return carried


def _carry_specs(rider, in_specs, out_specs, out_shape, scratch):
    if rider is None:
        return list(in_specs), tuple(out_specs), tuple(out_shape), list(scratch)
    k = rider.n
    return (list(in_specs) + [_HBM] * k, tuple(out_specs) + (_HBM,) * k, tuple(out_shape) + rider.out_shape,
            list(scratch) + list(rider.scratch))


def run_rider(rider, *, name):
    k = rider.n

    def body(*refs):
        rider.start(refs[:k], refs[k:2 * k], refs[2 * k:])
        rider.finish(refs[:k], refs[k:2 * k], refs[2 * k:])

    return _pallas(body, name=name, out_shape=rider.out_shape, in_specs=[_HBM] * k, out_specs=(_HBM,) * k,
                   scratch_shapes=rider.scratch)(*rider.arrays)


def allgather_small(shards, *, name):
    n = len(shards)

    def body(*refs):
        ins, outs = refs[:n], refs[n:2 * n]
        send, recv, local = refs[2 * n:]
        x, y, c, chips = _place()
        me = 2 * x + y
        locals_ = [pltpu.make_async_copy(ins[w], outs[w].at[me], local.at[w]) for w in range(n)]
        for cp in locals_:
            cp.start()

        def copy(w, j):
            px, py = chips[j]
            return pltpu.make_async_remote_copy(
                src_ref=ins[w], dst_ref=outs[w].at[me], send_sem=send.at[w, j], recv_sem=recv.at[w, j],
                device_id=(px, py, c), device_id_type=MESH)

        def arrival(w, j):
            px, py = chips[j]
            blk = outs[w].at[2 * px + py]
            return pltpu.make_async_remote_copy(
                src_ref=blk, dst_ref=blk, send_sem=send.at[w, j], recv_sem=recv.at[w, j],
                device_id=(x, y, c), device_id_type=MESH)

        for w in range(n):
            for j in range(3):
                copy(w, j).start()
        for w in range(n):
            for j in range(3):
                arrival(w, j).wait_recv()
        for w in range(n):
            for j in range(3):
                copy(w, j).wait_send()
        for cp in locals_:
            cp.wait()

    out_shape = tuple(pltpu.HBM((N_CHIPS,) + s.shape, s.dtype) for s in shards)
    return _pallas(
        body, name=name, out_shape=out_shape, in_specs=[_HBM] * n, out_specs=(_HBM,) * n,
        scratch_shapes=[pltpu.SemaphoreType.DMA((n, 3))] * 2 + [pltpu.SemaphoreType.DMA((n,))],
    )(*shards)


class SwapRider:
    def __init__(self, grads):
        self.arrays = list(grads)
        self.n = n = len(grads)
        self.out_shape = tuple(pltpu.HBM((N_CHIPS,) + g.shape[2:], g.dtype) for g in grads)
        self.scratch = [pltpu.SemaphoreType.DMA((n,))] * 2

    def _copies(self, ins, outs, sems):
        send, recv = sems
        x, y, c, _ = _place()
        return [pltpu.make_async_remote_copy(
            src_ref=ins[w].at[:, 1 - c], dst_ref=outs[w], send_sem=send.at[w], recv_sem=recv.at[w],
            device_id=(x, y, 1 - c), device_id_type=MESH) for w in range(self.n)]

    def start(self, ins, outs, sems):
        for cp in self._copies(ins, outs, sems):
            cp.start()

    def finish(self, ins, outs, sems):
        copies = self._copies(ins, outs, sems)
        for cp in copies:
            cp.wait_recv()
        for cp in copies:
            cp.wait_send()


def rs_sibling_share(stacked, *, name):
    n = len(stacked)

    def body(*refs):
        bufs = refs[n:2 * n]
        send, recv = refs[2 * n:]
        x, y, c, _ = _place()
        shares, arrivals = [], []
        for w in range(n):
            mine, other = bufs[w].at[:, c], bufs[w].at[:, 1 - c]
            shares.append(pltpu.make_async_remote_copy(
                src_ref=mine, dst_ref=mine, send_sem=send.at[w], recv_sem=recv.at[w],
                device_id=(x, y, 1 - c), device_id_type=MESH))
            arrivals.append(pltpu.make_async_remote_copy(
                src_ref=other, dst_ref=other, send_sem=send.at[w], recv_sem=recv.at[w],
                device_id=(x, y, c), device_id_type=MESH))
        for cp in shares:
            cp.start()
        for cp in arrivals:
            cp.wait_recv()
        for cp in shares:
            cp.wait_send()

    out_shape = tuple(pltpu.HBM(s.shape, F32) for s in stacked)
    return _pallas(
        body, name=name, out_shape=out_shape, in_specs=[_HBM] * n, out_specs=(_HBM,) * n,
        input_output_aliases={w: w for w in range(n)},
        scratch_shapes=[pltpu.SemaphoreType.DMA((n,))] * 2,
    )(*stacked)


def allreduce_small(v, *, name):
    R, C = v.shape

    def body(v_ref, o_ref, land, send, recv):
        x, y, c, _ = _place()
        me = 4 * x + 2 * y + c
        land[me] = v_ref[...]

        def flip(k):
            return (1 - x) if k & 4 else x, (1 - y) if k & 2 else y, (1 - c) if k & 1 else c

        copies = []
        for k in range(1, N_DEV):
            px, py, pc = flip(k)
            copies.append(pltpu.make_async_remote_copy(
                src_ref=v_ref, dst_ref=land.at[me], send_sem=send.at[k - 1], recv_sem=recv.at[k - 1],
                device_id=(px, py, pc), device_id_type=MESH))
        for cp in copies:
            cp.start()
        for k in range(1, N_DEV):
            px, py, pc = flip(k)
            blk = land.at[4 * px + 2 * py + pc]
            pltpu.make_async_remote_copy(
                src_ref=blk, dst_ref=blk, send_sem=send.at[k - 1], recv_sem=recv.at[k - 1],
                device_id=(x, y, c), device_id_type=MESH).wait_recv()
        for cp in copies:
            cp.wait_send()
        acc = land[0]
        for d in range(1, N_DEV):
            acc = acc + land[d]
        o_ref[...] = acc

    vm = pl.BlockSpec(memory_space=pltpu.VMEM)
    return pl.pallas_call(
        body, name=name, out_shape=jax.ShapeDtypeStruct((R, C), F32), in_specs=[vm], out_specs=vm,
        scratch_shapes=[pltpu.VMEM((N_DEV, R, C), F32), pltpu.SemaphoreType.DMA((N_DEV - 1,)),
                        pltpu.SemaphoreType.DMA((N_DEV - 1,))],
        compiler_params=pltpu.CompilerParams(vmem_limit_bytes=int(min(12 * R * C * 4 + (8 << 20), VMEM_CAP))),
    )(v)


def _pack(arrays):
    flat = jnp.concatenate([a.reshape(-1) for a in arrays])
    return flat.reshape(-1, LANES)


def _unpack(packed, shapes):
    flat = packed.reshape(-1)
    out, off = [], 0
    for s in shapes:
        n = 1
        for d in s:
            n *= d
        out.append(flat[off:off + n].reshape(s))
        off += n
    return out


def _row_tile(rows, cap=512):
    t = 1 << (cap.bit_length() - 1)
    while rows % t:
        t //= 2
    return t


def _adamw_tile(rows, cols):
    return _row_tile(rows, max(8, (1 << 20) // (4 * cols)))


def kernel(x, mem, w_in, conv_w, conv_b, conv_ln_g, conv_ln_b, w_out, ln1_g, ln1_b, mem_wq, mem_wk, mem_wv, mem_wo, ln2_g, ln2_b, ffn_up, ffn_conv_w, ffn_conv_b, ffn_down, ln3_g, ln3_b, loss_target, m_w_in, m_conv_w, m_conv_b, m_conv_ln_g, m_conv_ln_b, m_w_out, m_ln1_g, m_ln1_b, m_mem_wq, m_mem_wk, m_mem_wv, m_mem_wo, m_ln2_g, m_ln2_b, m_ffn_up, m_ffn_conv_w, m_ffn_conv_b, m_ffn_down, m_ln3_g, m_ln3_b, v_w_in, v_conv_w, v_conv_b, v_conv_ln_g, v_conv_ln_b, v_w_out, v_ln1_g, v_ln1_b, v_mem_wq, v_mem_wk, v_mem_wv, v_mem_wo, v_ln2_g, v_ln2_b, v_ffn_up, v_ffn_conv_w, v_ffn_conv_b, v_ffn_down, v_ln3_g, v_ln3_b):
    W = dict(w_in=w_in, conv_w=conv_w, conv_b=conv_b, conv_ln_g=conv_ln_g, conv_ln_b=conv_ln_b, w_out=w_out,
             ln1_g=ln1_g, ln1_b=ln1_b, mem_wq=mem_wq, mem_wk=mem_wk, mem_wv=mem_wv, mem_wo=mem_wo, ln2_g=ln2_g,
             ln2_b=ln2_b, ffn_up=ffn_up, ffn_conv_w=ffn_conv_w, ffn_conv_b=ffn_conv_b, ffn_down=ffn_down,
             ln3_g=ln3_g, ln3_b=ln3_b)
    M1 = dict(w_in=m_w_in, conv_w=m_conv_w, conv_b=m_conv_b, conv_ln_g=m_conv_ln_g, conv_ln_b=m_conv_ln_b,
              w_out=m_w_out, ln1_g=m_ln1_g, ln1_b=m_ln1_b, mem_wq=m_mem_wq, mem_wk=m_mem_wk, mem_wv=m_mem_wv,
              mem_wo=m_mem_wo, ln2_g=m_ln2_g, ln2_b=m_ln2_b, ffn_up=m_ffn_up, ffn_conv_w=m_ffn_conv_w,
              ffn_conv_b=m_ffn_conv_b, ffn_down=m_ffn_down, ln3_g=m_ln3_g, ln3_b=m_ln3_b)
    V2 = dict(w_in=v_w_in, conv_w=v_conv_w, conv_b=v_conv_b, conv_ln_g=v_conv_ln_g, conv_ln_b=v_conv_ln_b,
              w_out=v_w_out, ln1_g=v_ln1_g, ln1_b=v_ln1_b, mem_wq=v_mem_wq, mem_wk=v_mem_wk, mem_wv=v_mem_wv,
              mem_wo=v_mem_wo, ln2_g=v_ln2_g, ln2_b=v_ln2_b, ffn_up=v_ffn_up, ffn_conv_w=v_ffn_conv_w,
              ffn_conv_b=v_ffn_conv_b, ffn_down=v_ffn_down, ln3_g=v_ln3_g, ln3_b=v_ln3_b)

    L = w_in.shape[0]
    S, D = x.shape[1], x.shape[2]
    C = conv_b.shape[1]
    alpha = (2.0 * L) ** 0.25
    chip = 2 * lax.axis_index("x") + lax.axis_index("y")
    xs, mems, tgt = x[0], mem[0], loss_target[0]
    mem_bf = mems.astype(BF16)
    tm = _row_tile(S)
    tm_ffn = _row_tile(S, 256)
    tm_big = _row_tile(S, 1024)

    def shards_of(l, names):
        out = []
        for n in names:
            wl = W[n][l].astype(BF16)
            out.append(wl.reshape(2, wl.shape[0] // 2, wl.shape[1]))
        return out

    def gathered(names, got):
        layer = {}
        for n, g in zip(names, got):
            rows, cols = W[n].shape[1], W[n].shape[2]
            layer[n] = g.reshape(N_CHIPS, rows, cols) if n in COL_SHARDED else g.reshape(N_CHIPS * rows, cols)
        return layer

    full = [dict() for _ in range(L)]
    full[0].update(gathered(RIDE_IN, run_rider(GatherRider(shards_of(0, RIDE_IN)), name="allgather_w_in")))
    cw_all, fcw_all = allgather_small([conv_w, ffn_conv_w], name="allgather_small")
    cw_full = jnp.transpose(cw_all, (1, 2, 0, 3)).reshape(L, conv_w.shape[1], -1)
    fcw_full = jnp.transpose(fcw_all, (1, 2, 0, 3)).reshape(L, ffn_conv_w.shape[1], -1)

    saved = []
    h, hb = xs, xs.astype(BF16)
    for l in range(L):
        fw = full[l]
        s = dict(x=h, xb=hb)
        s['glu'], s['qkv'] = proj_split(hb, fw['w_in'], 2 * C, tm=tm_big, name="proj")
        on_conv = RIDE_ATT if l == 0 else RIDE_FFN[1:]
        on_sb = RIDE_FFN if l == 0 else RIDE_FFN[:1]
        s['u1'], got = conv_fwd(s['glu'], cw_full[l], conv_b[l][None], name="conv_fwd",
                                rider=GatherRider(shards_of(l, on_conv)))
        fw.update(gathered(on_conv, got))
        more = l + 1 < L
        s['o_sb'], s['ltot'], got = sb_fwd(
            s['qkv'], q_col=0, name="sb_fwd", rider=GatherRider(shards_of(l, on_sb)))
        fw.update(gathered(on_sb, got))
        s['ua'] = ln_silu(s['u1'], s['o_sb'], conv_ln_g[l][None], conv_ln_b[l][None], tm=tm, name="ln_silu")
        s['x1'], s['x1b'], s['zh1'], s['rs1'] = mm_ln(
            s['ua'], fw['w_out'], h, ln1_g[l][None], ln1_b[l][None], alpha, tm=tm, name="out_proj_ln")
        s['q2'] = mm_nn(s['x1b'], fw['mem_wq'], BF16, tm=min(1024, S), tn=512, name="mem_q")
        s['k2'] = mm_nn(mem_bf, fw['mem_wk'], BF16, tm=mem_bf.shape[0], tn=512, name="mem_kv")
        s['v2'] = mm_nn(mem_bf, fw['mem_wv'], BF16, tm=mem_bf.shape[0], tn=512, name="mem_kv")
        s['o2'] = xattn_fwd(s['q2'], s['k2'], s['v2'], tm=tm, name="xattn_fwd")
        s['x2'], s['x2b'], s['zh2'], s['rs2'] = mm_ln(
            s['o2'], fw['mem_wo'], s['x1'], ln2_g[l][None], ln2_b[l][None], alpha, tm=tm, name="mem_o_ln")
        (s['upv'], s['upg'], s['mv'], s['mg'], s['hmid']), got = ffn_up_fwd(
            s['x2b'], fw['ffn_up'], fcw_full[l], ffn_conv_b[l][None], tm=tm_ffn, tn=fw['ffn_up'].shape[2],
            name="ffn_up_fwd", rider=GatherRider(shards_of(l + 1, RIDE_ATT + RIDE_IN)) if more else None)
        if more:
            full[l + 1].update(gathered(RIDE_ATT + RIDE_IN, got))
        h, hb, s['zh3'], s['rs3'] = mm_ln(
            s['hmid'], fw['ffn_down'], s['x2'], ln3_g[l][None], ln3_b[l][None], alpha, tm=tm, name="ffn_down_ln")
        saved.append(s)

    dx, loss_part = loss_head(h, tgt, tm=tm, name="loss_head")
    loss = lax.psum(loss_part[0, 0], ("x", "y", "c"))

    core = lax.axis_index("c").astype(jnp.int32).reshape(1)
    reduced_big = {n: lax.empty((L, 2, W[n].shape[1] // 2, W[n].shape[2]), F32) for n in BIG}
    small_grads = [None] * L

    def row_halves(g, names):
        parts = []
        for n in names:
            rows, cols = W[n].shape[1], W[n].shape[2]
            parts.append(g[n].reshape(N_CHIPS, 2, rows // 2, cols))
        return parts

    def pre_add(g, names):
        parts = row_halves(g, names)
        got = run_rider(SwapRider(parts), name="rs_sibling_swap")
        return list(add_pairs(parts, got, core, name="rs_add_pairs"))

    def reduce_into(names, scattered, layer):
        reduced_big.update(zip(names, sum_chips_into(
            list(scattered), [reduced_big[n] for n in names], layer, core, name="rs_sum_chips")))

    pending = None
    for l in reversed(range(L)):
        fw, s = full[l], saved[l]
        g = {}
        if l == L - 1:
            top = ln_bwd(dx, s['zh3'], s['rs3'], ln3_g[l][None], tm=tm, name="ln_bwd")
        dz3, dz3b, g['ln3_g'], g['ln3_b'] = top
        ftn = fw['ffn_up'].shape[2]
        (dupv, dupg, dfw_v, dfw_g, dfb_v, dfb_g), sc = ffn_mid_bwd(
            dz3b, fw['ffn_down'], s['upv'], s['upg'], s['mv'], s['mg'], fcw_full[l], tm=tm_ffn, tn=ftn,
            name="ffn_mid_bwd", rider=ScatterRider(pending) if pending else None)
        if pending:
            reduce_into(RIDE_MIX, sc, l + 1)
        g['ffn_conv_w'] = jnp.concatenate([dfw_v, dfw_g], axis=1)
        g['ffn_conv_b'] = jnp.concatenate([dfb_v, dfb_g], axis=1)[0]
        g['ffn_down'] = mm_tn(s['hmid'], [dz3b], tk=ftn, tn=512, tmc=min(1024, S), name="grad_ffn_down")
        dz2, dz2b, g['ln2_g'], g['ln2_b'] = mm_nt_ln_bwd(
            [dupv, dupg], fw['ffn_up'], dz3, alpha, s['zh2'], s['rs2'], ln2_g[l][None], tm=tm_ffn,
            name="ffn_up_bwd")
        g['ffn_up'] = mm_tn(s['x2b'], [dupv, dupg], tk=512, tn=ftn, shard_width=ftn, tmc=min(1024, S),
                            name="grad_ffn_up")

        do2 = mm_nt([dz2b], fw['mem_wo'], BF16, tm=tm_big, tk=512, name="mem_o_bwd")
        g['mem_wo'] = mm_tn(s['o2'], [dz2b], tk=512, tn=512, name="grad_sq")
        dq2, dk2, dv2 = xattn_bwd(s['q2'], do2, s['k2'], s['v2'], tm=tm, name="xattn_bwd")
        dz1, dz1b, g['ln1_g'], g['ln1_b'] = mm_nt_ln_bwd(
            [dq2], fw['mem_wq'], dz2, alpha, s['zh1'], s['rs1'], ln1_g[l][None], tm=tm, name="mem_q_bwd")
        g['mem_wq'] = mm_tn(s['x1b'], [dq2], tk=512, tn=512, name="grad_sq")
        g['mem_wk'] = mm_tn(mem_bf, [dk2], tk=512, tn=512, name="grad_mem_kv")
        g['mem_wv'] = mm_tn(mem_bf, [dv2], tk=512, tn=512, name="grad_mem_kv")

        rest = row_halves(g, RIDE_REST)
        dua, got = mm_nt([dz1b], fw['w_out'], F32, tm=tm_big, tk=512, name="out_proj_bwd", rider=SwapRider(rest))
        rest = list(add_pairs(rest, got, core, name="rs_add_pairs"))
        g['w_out'] = mm_tn(s['ua'], [dz1b], tk=512, tn=512, name="grad_sq")
        dq, dk, dv, sc = sb_bwd(
            s['qkv'], s['ltot'], dua, q_col=0, do_col=C, name="sb_bwd",
            rider=ScatterRider(rest[:-1]))
        reduce_into(RIDE_REST[:-1], sc, l)
        du1, g['conv_ln_g'], g['conv_ln_b'] = ln_silu_bwd(
            dua, s['u1'], conv_ln_g[l][None], conv_ln_b[l][None], tm=tm, name="ln_silu_bwd")
        (da, dg, g['conv_w'], dcb), sc = conv_bwd(du1, s['glu'], cw_full[l], name="conv_bwd",
                                                  rider=ScatterRider(rest[-1:]))
        reduce_into(RIDE_REST[-1:], sc, l)
        g['conv_b'] = dcb
        dproj = jnp.concatenate([da, dg, dq, dk, dv], axis=1)
        ns_in = fw['w_in'].shape[2]
        if l > 0:
            below = saved[l - 1]
            top = mm_nt_ln_bwd([dproj], fw['w_in'], dz1, alpha, below['zh3'], below['rs3'], ln3_g[l - 1][None],
                               tm=tm, name="proj_bwd")
        else:
            dx = mm_nt([dproj], fw['w_in'], F32, tm=tm_big, tk=512, res=dz1, alpha=alpha, name="proj_bwd_x")
        g['w_in'] = mm_tn(s['xb'], [dproj], tk=512, tn=ns_in, shard_width=ns_in, name="grad_w_in")

        pending = pre_add(g, RIDE_MIX)
        small_grads[l] = {n: g[n].reshape(W[n].shape[1:-1] + (-1,)) for n in SMALL}

    grad_x = dx[None]

    reduce_into(RIDE_MIX, run_rider(ScatterRider(pending), name="rs_chip_scatter"), 0)
    shared = rs_sibling_share([reduced_big[n] for n in BIG], name="rs_sibling_share")
    G = {}
    for n, sh in zip(BIG, shared):
        G[n] = sh.reshape(W[n].shape)

    small_full_shapes = []
    small_stack = []
    for n in SMALL:
        st = jnp.stack([small_grads[l][n] for l in range(L)])
        small_stack.append(st)
        small_full_shapes.append(st.shape)
    reduced = _unpack(allreduce_small(_pack(small_stack), name="allreduce_small"), small_full_shapes)
    for n, r in zip(SMALL, reduced):
        if n in SMALL_SHARDED:
            width = W[n].shape[-1]
            r = lax.dynamic_slice_in_dim(r, chip * width, width, axis=2)
        G[n] = r

    out_g, out_d, out_m, out_v = {}, {}, {}, {}
    for n in BIG:
        shp = W[n].shape
        flat = lambda a: a.reshape(shp[0] * shp[1], shp[2])
        res = adamw(flat(W[n]), flat(G[n]), flat(M1[n]), flat(V2[n]), tr=_adamw_tile(shp[0] * shp[1], shp[2]), name="adamw")
        out_g[n], out_d[n], out_m[n], out_v[n] = [r.reshape(shp) for r in res]
    small_shapes = [W[n].shape for n in SMALL]
    packed = [_pack([d[n] for n in SMALL]) for d in (W, G, M1, V2)]
    res = adamw(*packed, tr=packed[0].shape[0], name="adamw_small")
    for d, r in zip((out_g, out_d, out_m, out_v), res):
        for n, a in zip(SMALL, _unpack(r, small_shapes)):
            d[n] = a

    return (loss, grad_x, *[out_g[n] for n in WEIGHTS], *[out_d[n] for n in WEIGHTS],
            *[out_m[n] for n in WEIGHTS], *[out_v[n] for n in WEIGHTS])
```

```python
import functools

import jax
import jax.numpy as jnp
from jax import lax
from jax.experimental import pallas as pl
from jax.experimental.pallas import tpu as pltpu

F32 = jnp.float32
BF16 = jnp.bfloat16
MESH = pl.DeviceIdType.MESH

LN_EPS = 1e-5
SB_HEADS = 8
MEM_HEADS = 4
ADAM_LR, ADAM_B1, ADAM_B2, ADAM_EPS, ADAM_WD, ADAM_STEP = 0.001, 0.9, 0.999, 1e-08, 0.01, 10

LANES = 128
V7X_VMEM_BYTES = 64 << 20
VMEM_CAP = V7X_VMEM_BYTES - (6 << 20)
N_CHIPS = 4
N_DEV = 8

BIG = ('w_in', 'w_out', 'mem_wq', 'mem_wk', 'mem_wv', 'mem_wo', 'ffn_up', 'ffn_down')
RIDE_IN = ('w_in',)
RIDE_ATT = ('w_out', 'mem_wq', 'mem_wk', 'mem_wv', 'mem_wo')
RIDE_FFN = ('ffn_up', 'ffn_down')
RIDE_MIX = ('w_in', 'w_out')
RIDE_REST = ('mem_wq', 'mem_wk', 'mem_wv', 'mem_wo', 'ffn_up', 'ffn_down')
COL_SHARDED = ('w_in', 'ffn_up')
SMALL = ('conv_w', 'conv_b', 'conv_ln_g', 'conv_ln_b', 'ln1_g', 'ln1_b', 'ln2_g', 'ln2_b',
         'ffn_conv_w', 'ffn_conv_b', 'ln3_g', 'ln3_b')
SMALL_SHARDED = ('conv_w', 'ffn_conv_w')
WEIGHTS = ('w_in', 'conv_w', 'conv_b', 'conv_ln_g', 'conv_ln_b', 'w_out', 'ln1_g', 'ln1_b',
           'mem_wq', 'mem_wk', 'mem_wv', 'mem_wo', 'ln2_g', 'ln2_b', 'ffn_up', 'ffn_conv_w',
           'ffn_conv_b', 'ffn_down', 'ln3_g', 'ln3_b')


def _params(block_bytes, semantics=None, **kw):
    limit = int(min(max(2 * block_bytes + (8 << 20), 32 << 20), VMEM_CAP))
    return pltpu.CompilerParams(dimension_semantics=semantics, vmem_limit_bytes=limit, **kw)


def _pallas(body, **kw):
    call = pl.pallas_call(body, **kw)

    def run(*args):
        return call(*[pltpu.with_memory_space_constraint(a, pltpu.HBM)
                      if jnp.issubdtype(a.dtype, jnp.floating) else a for a in args])

    return run


def _nbytes(shape, dtype):
    n = 1
    for s in shape:
        n *= s
    return n * jnp.dtype(dtype).itemsize


def _dot(a, b):
    return jnp.dot(a, b, preferred_element_type=F32)


def _dot_nt(a, b):
    return lax.dot_general(a, b, (((1,), (1,)), ((), ())), preferred_element_type=F32)


def _dot_tn(a, b):
    return lax.dot_general(a, b, (((0,), (0,)), ((), ())), preferred_element_type=F32)


def _sigmoid(x):
    return 1.0 / (1.0 + jnp.exp(-x))


def mm_nn(a, b, out_dtype, *, tm, tn, name):
    M, K = a.shape
    sharded = b.ndim == 3
    if sharded:
        nsh, _, ns = b.shape
        N, per = nsh * ns, ns // tn
        b_spec = pl.BlockSpec((None, K, tn), lambda i, j: (j // per, 0, j % per))
    else:
        N = b.shape[1]
        b_spec = pl.BlockSpec((K, tn), lambda i, j: (0, j))

    def body(a_ref, b_ref, o_ref):
        o_ref[...] = _dot(a_ref[...].astype(BF16), b_ref[...]).astype(o_ref.dtype)

    blk = _nbytes((tm, K), a.dtype) + _nbytes((K, tn), BF16) + _nbytes((tm, tn), out_dtype)
    return _pallas(
        body, name=name, out_shape=pltpu.HBM((M, N), out_dtype), grid=(M // tm, N // tn),
        in_specs=[pl.BlockSpec((tm, K), lambda i, j: (i, 0)), b_spec],
        out_specs=pl.BlockSpec((tm, tn), lambda i, j: (i, j)),
        compiler_params=_params(blk, ("parallel", "parallel")))(a, b)


def proj_split(a, b, n_f32, *, tm, name):
    M, K = a.shape
    nsh, _, ns = b.shape
    N = nsh * ns

    def body(a_ref, b_ref, lo_ref, hi_ref):
        acc = _dot(a_ref[...], b_ref[...])
        j = pl.program_id(1)
        for s in range(nsh):
            c0, c1 = s * ns, (s + 1) * ns
            cut = min(max(n_f32 - c0, 0), ns)

            @pl.when(j == s)
            def _(c0=c0, c1=c1, cut=cut):
                if cut > 0:
                    lo_ref[:, c0:c0 + cut] = acc[:, 0:cut]
                if cut < ns:
                    hi_ref[:, c0 + cut - n_f32:c1 - n_f32] = acc[:, cut:ns].astype(BF16)

    blk = _nbytes((tm, K), BF16) + _nbytes((K, ns), BF16) + _nbytes((tm, N), F32)
    return _pallas(
        body, name=name, grid=(M // tm, nsh),
        out_shape=(pltpu.HBM((M, n_f32), F32), pltpu.HBM((M, N - n_f32), BF16)),
        in_specs=[pl.BlockSpec((tm, K), lambda i, j: (i, 0)), pl.BlockSpec((None, K, ns), lambda i, j: (j, 0, 0))],
        out_specs=(pl.BlockSpec((tm, n_f32), lambda i, j: (i, 0)), pl.BlockSpec((tm, N - n_f32), lambda i, j: (i, 0))),
        compiler_params=_params(blk, ("parallel", "arbitrary")))(a, b)


def mm_ln(a, b, x, gamma, beta, alpha, *, tm, name):
    M, K = a.shape
    D = b.shape[1]

    def body(a_ref, b_ref, x_ref, g_ref, be_ref, y_ref, yb_ref, zh_ref, rs_ref):
        z = alpha * x_ref[...] + _dot(a_ref[...], b_ref[...])
        mu = jnp.mean(z, axis=-1, keepdims=True)
        zc = z - mu
        rstd = lax.rsqrt(jnp.mean(zc * zc, axis=-1, keepdims=True) + LN_EPS)
        zh = zc * rstd
        y = zh * g_ref[...] + be_ref[...]
        y_ref[...] = y
        yb_ref[...] = y.astype(BF16)
        zh_ref[...] = zh
        rs_ref[...] = rstd

    row = lambda i: (i, 0)
    fix = lambda i: (0, 0)
    blk = _nbytes((tm, K), BF16) + _nbytes((K, D), BF16) + 4 * _nbytes((tm, D), F32)
    return _pallas(
        body, name=name, grid=(M // tm,),
        out_shape=(pltpu.HBM((M, D), F32), pltpu.HBM((M, D), BF16),
                   pltpu.HBM((M, D), F32), pltpu.HBM((M, 1), F32)),
        in_specs=[pl.BlockSpec((tm, K), row), pl.BlockSpec((K, D), fix), pl.BlockSpec((tm, D), row),
                  pl.BlockSpec((1, D), fix), pl.BlockSpec((1, D), fix)],
        out_specs=(pl.BlockSpec((tm, D), row), pl.BlockSpec((tm, D), row), pl.BlockSpec((tm, D), row),
                   pl.BlockSpec((tm, 1), row)),
        compiler_params=_params(blk, ("parallel",)))(a, b, x, gamma, beta)


def ln_bwd(dy, zh, rstd, gamma, *, tm, name):
    M, D = dy.shape

    def body(dy_ref, zh_ref, rs_ref, g_ref, dz_ref, dzb_ref, dg_ref, db_ref):
        @pl.when(pl.program_id(0) == 0)
        def _():
            dg_ref[...] = jnp.zeros_like(dg_ref)
            db_ref[...] = jnp.zeros_like(db_ref)

        dyv, zhv = dy_ref[...], zh_ref[...]
        dg_ref[...] += jnp.sum(dyv * zhv, axis=0, keepdims=True)
        db_ref[...] += jnp.sum(dyv, axis=0, keepdims=True)
        dzh = dyv * g_ref[...]
        m1 = jnp.mean(dzh, axis=-1, keepdims=True)
        m2 = jnp.mean(dzh * zhv, axis=-1, keepdims=True)
        dz = rs_ref[...] * (dzh - m1 - zhv * m2)
        dz_ref[...] = dz
        dzb_ref[...] = dz.astype(BF16)

    row = lambda i: (i, 0)
    fix = lambda i: (0, 0)
    return _pallas(
        body, name=name, grid=(M // tm,),
        out_shape=(pltpu.HBM((M, D), F32), pltpu.HBM((M, D), BF16),
                   pltpu.HBM((1, D), F32), pltpu.HBM((1, D), F32)),
        in_specs=[pl.BlockSpec((tm, D), row), pl.BlockSpec((tm, D), row), pl.BlockSpec((tm, 1), row),
                  pl.BlockSpec((1, D), fix)],
        out_specs=(pl.BlockSpec((tm, D), row), pl.BlockSpec((tm, D), row), pl.BlockSpec((1, D), fix),
                   pl.BlockSpec((1, D), fix)),
        compiler_params=_params(4 * _nbytes((tm, D), F32), ("arbitrary",)))(dy, zh, rstd, gamma)


def mm_nt(a_list, b, out_dtype, *, tm, tk, name, res=None, alpha=None, rider=None):
    M = a_list[0].shape[0]
    widths = [a.shape[1] for a in a_list]
    sharded = b.ndim == 3
    if sharded:
        nsh, K, ns = b.shape
        b_spec = pl.BlockSpec((nsh, tk, ns), lambda i, j: (0, j, 0))
        for w in widths:
            assert w % ns == 0
    else:
        K, N = b.shape
        ns = None
        b_spec = pl.BlockSpec((tk, N), lambda i, j: (j, 0))
    n_a = len(a_list)

    def body(*refs):
        a_refs, b_ref = refs[:n_a], refs[n_a]
        o_ref = refs[-1]
        acc = None
        off = 0
        for a_ref, w in zip(a_refs, widths):
            if sharded:
                for p in range(w // ns):
                    t = _dot_nt(a_ref[:, p * ns:(p + 1) * ns].astype(BF16), b_ref[off // ns + p])
                    acc = t if acc is None else acc + t
            else:
                t = _dot_nt(a_ref[...].astype(BF16), b_ref[:, off:off + w])
                acc = t if acc is None else acc + t
            off += w
        if res is not None:
            acc = acc + alpha * refs[n_a + 1][...]
        o_ref[...] = acc.astype(o_ref.dtype)

    in_specs = [pl.BlockSpec((tm, w), lambda i, j: (i, 0)) for w in widths] + [b_spec]
    args = list(a_list) + [b]
    if res is not None:
        in_specs.append(pl.BlockSpec((tm, tk), lambda i, j: (i, j)))
        args.append(res)
    blk = (sum(_nbytes((tm, w), a.dtype) for a, w in zip(a_list, widths)) + _nbytes((tk, sum(widths)), BF16)
           + 2 * _nbytes((tm, tk), F32))
    in_specs, out_specs, out_shape, scratch = _carry_specs(
        rider, in_specs, (pl.BlockSpec((tm, tk), lambda i, j: (i, j)),), (pltpu.HBM((M, K), out_dtype),), [])
    first = lambda: (pl.program_id(0) == 0) & (pl.program_id(1) == 0)
    last = lambda: (pl.program_id(0) == M // tm - 1) & (pl.program_id(1) == K // tk - 1)
    res_all = _pallas(
        _carry(rider, body, len(args), 1, first, last), name=name, out_shape=out_shape, grid=(M // tm, K // tk),
        in_specs=in_specs, out_specs=out_specs, scratch_shapes=scratch,
        compiler_params=_params(blk, ("arbitrary", "arbitrary")))(*args, *(rider.arrays if rider else ()))
    return res_all[0] if rider is None else (res_all[0], list(res_all[1:]))


def mm_nt_ln_bwd(a_list, b, res, alpha, zh, rstd, gamma, *, tm, name):
    M, D = res.shape
    widths = [a.shape[1] for a in a_list]
    sharded = b.ndim == 3
    if sharded:
        nsh, _, ns = b.shape
        b_spec = pl.BlockSpec((nsh, D, ns), lambda i: (0, 0, 0))
    else:
        ns = None
        b_spec = pl.BlockSpec((D, b.shape[1]), lambda i: (0, 0))
    n_a = len(a_list)

    def body(*refs):
        a_refs, b_ref = refs[:n_a], refs[n_a]
        res_ref, zh_ref, rs_ref, g_ref = refs[n_a + 1:n_a + 5]
        dz_ref, dzb_ref, dg_ref, db_ref = refs[n_a + 5:]

        @pl.when(pl.program_id(0) == 0)
        def _():
            dg_ref[...] = jnp.zeros_like(dg_ref)
            db_ref[...] = jnp.zeros_like(db_ref)

        dy = alpha * res_ref[...]
        off = 0
        for a_ref, w in zip(a_refs, widths):
            if sharded:
                for p in range(w // ns):
                    dy = dy + _dot_nt(a_ref[:, p * ns:(p + 1) * ns], b_ref[off // ns + p])
            else:
                dy = dy + _dot_nt(a_ref[...], b_ref[:, off:off + w])
            off += w
        zhv = zh_ref[...]
        dg_ref[...] += jnp.sum(dy * zhv, axis=0, keepdims=True)
        db_ref[...] += jnp.sum(dy, axis=0, keepdims=True)
        dzh = dy * g_ref[...]
        m1 = jnp.mean(dzh, axis=-1, keepdims=True)
        m2 = jnp.mean(dzh * zhv, axis=-1, keepdims=True)
        dz = rs_ref[...] * (dzh - m1 - zhv * m2)
        dz_ref[...] = dz
        dzb_ref[...] = dz.astype(BF16)

    row = lambda i: (i, 0)
    fix = lambda i: (0, 0)
    in_specs = [pl.BlockSpec((tm, w), row) for w in widths] + [
        b_spec, pl.BlockSpec((tm, D), row), pl.BlockSpec((tm, D), row), pl.BlockSpec((tm, 1), row),
        pl.BlockSpec((1, D), fix)]
    blk = (sum(_nbytes((tm, w), BF16) for w in widths) + _nbytes((D, sum(widths)), BF16)
           + 5 * _nbytes((tm, D), F32))
    return _pallas(
        body, name=name, grid=(M // tm,),
        out_shape=(pltpu.HBM((M, D), F32), pltpu.HBM((M, D), BF16), pltpu.HBM((1, D), F32),
                   pltpu.HBM((1, D), F32)),
        in_specs=in_specs,
        out_specs=(pl.BlockSpec((tm, D), row), pl.BlockSpec((tm, D), row), pl.BlockSpec((1, D), fix),
                   pl.BlockSpec((1, D), fix)),
        compiler_params=_params(blk, ("arbitrary",)))(*a_list, b, res, zh, rstd, gamma)


def mm_tn(a, b_list, *, tk, tn, name, shard_width=None, tmc=None):
    M, K = a.shape
    tmc = M if tmc is None else tmc
    nm = M // tmc
    widths = [b.shape[1] for b in b_list]
    N = sum(widths)
    starts, s = [], 0
    for w in widths:
        assert w % tn == 0
        starts.append(s)
        s += w // tn
    n_b = len(b_list)

    def body(*refs):
        a_ref, b_refs, o_ref, acc = refs[0], refs[1:1 + n_b], refs[-2], refs[-1]
        j, m = pl.program_id(1), pl.program_id(2)
        for b_ref, st, w in zip(b_refs, starts, widths):
            @pl.when((j >= st) & (j < st + w // tn))
            def _(b_ref=b_ref):
                t = _dot_tn(a_ref[...].astype(BF16), b_ref[...].astype(BF16))
                if nm == 1:
                    o_ref[...] = t.astype(o_ref.dtype)
                else:
                    @pl.when(m == 0)
                    def _():
                        acc[...] = t

                    @pl.when(m > 0)
                    def _():
                        acc[...] += t

                    @pl.when(m == nm - 1)
                    def _():
                        o_ref[...] = acc[...].astype(o_ref.dtype)

    def b_map(st, w):
        nb = w // tn
        return lambda i, j, m: (jnp.where((j >= st) & (j < st + nb), m, 0), jnp.clip(j - st, 0, nb - 1))

    in_specs = [pl.BlockSpec((tmc, tk), lambda i, j, m: (m, i))]
    in_specs += [pl.BlockSpec((tmc, tn), b_map(st, w)) for st, w in zip(starts, widths)]
    if shard_width is None:
        out_shape = pltpu.HBM((K, N), BF16)
        out_spec = pl.BlockSpec((tk, tn), lambda i, j, m: (i, j))
    else:
        per = shard_width // tn
        out_shape = pltpu.HBM((N // shard_width, K, shard_width), BF16)
        out_spec = pl.BlockSpec((None, tk, tn), lambda i, j, m: (j // per, i, j % per))
    acc_shape = (tk, tn) if nm > 1 else (8, LANES)
    blk = (_nbytes((tmc, tk), a.dtype) + n_b * _nbytes((tmc, tn), b_list[0].dtype) + 2 * _nbytes((tk, tn), F32))
    return _pallas(
        body, name=name, out_shape=out_shape, grid=(K // tk, N // tn, nm), in_specs=in_specs, out_specs=out_spec,
        scratch_shapes=[pltpu.VMEM(acc_shape, F32)],
        compiler_params=_params(blk, ("parallel", "arbitrary", "arbitrary")))(a, *b_list)


CONV_PAD = 32
CONV_CHUNK = 128


def _rows(win, off, n, shifts):
    b, a = off % 8, off // 8
    if b not in shifts:
        shifts[b] = win if b == 0 else win[b:b + n + CONV_PAD - 8, :]
    return shifts[b][8 * a:8 * a + n, :]


def _by_residue(n_taps, offset):
    return sorted(range(n_taps), key=lambda k: (offset(k) % 8, k))


def conv_fwd(proj, conv_w, conv_b, *, name, rider=None):
    S = proj.shape[0]
    KW, C = conv_w.shape
    nct = C // LANES
    rc = min(CONV_CHUNK, S)

    def body(a_ref, g_ref, w_ref, b_ref, o_ref, pad):
        pad[0:CONV_PAD, :] = jnp.zeros((CONV_PAD, LANES), F32)
        pad[CONV_PAD:, :] = a_ref[...] * _sigmoid(g_ref[...])
        w = w_ref[...]
        bias = b_ref[...]

        def chunk(i, _):
            base = pl.multiple_of(i * rc, rc)
            win = pad[pl.ds(base, rc + CONV_PAD), :]
            acc = jnp.zeros((rc, LANES), F32) + bias
            shifts = {}
            for k in _by_residue(KW, lambda k: CONV_PAD - (KW - 1) + k):
                acc = acc + w[k:k + 1, :] * _rows(win, CONV_PAD - (KW - 1) + k, rc, shifts)
            o_ref[pl.ds(base, rc), :] = acc
            return 0

        lax.fori_loop(0, S // rc, chunk, 0)

    in_specs, out_specs, out_shape, scratch = _carry_specs(
        rider, [pl.BlockSpec((S, LANES), lambda c: (0, c)), pl.BlockSpec((S, LANES), lambda c: (0, c + nct)),
                pl.BlockSpec((KW, LANES), lambda c: (0, c)), pl.BlockSpec((1, LANES), lambda c: (0, c))],
        (pl.BlockSpec((S, LANES), lambda c: (0, c)),), (pltpu.HBM((S, C), F32),),
        [pltpu.VMEM((S + CONV_PAD, LANES), F32)])
    first = lambda: pl.program_id(0) == 0
    last = lambda: pl.program_id(0) == nct - 1
    res = _pallas(
        _carry(rider, body, 4, 1, first, last), name=name, grid=(nct,), out_shape=out_shape,
        in_specs=in_specs, out_specs=out_specs, scratch_shapes=scratch,
        compiler_params=_params(4 * _nbytes((S, LANES), F32), ("arbitrary",)))(
            proj, proj, conv_w, conv_b, *(rider.arrays if rider else ()))
    return res[0], list(res[1:])


def conv_bwd(du1, proj, conv_w, *, name, rider=None):
    S = proj.shape[0]
    KW, C = conv_w.shape
    nct = C // LANES
    rc = min(CONV_CHUNK, S)

    def body(d_ref, a_ref, g_ref, w_ref, da_ref, dg_ref, dw_ref, db_ref, pad_u, pad_d, du0, dw_acc):
        dw_acc[...] = jnp.zeros_like(dw_acc)
        pad_u[0:CONV_PAD, :] = jnp.zeros((CONV_PAD, LANES), F32)
        pad_u[CONV_PAD:, :] = a_ref[...] * _sigmoid(g_ref[...])
        pad_d[0:S, :] = d_ref[...]
        pad_d[S:, :] = jnp.zeros((CONV_PAD, LANES), F32)
        w = w_ref[...]
        db_ref[...] = jnp.sum(d_ref[...], axis=0, keepdims=True)

        def chunk(i, _):
            base = pl.multiple_of(i * rc, rc)
            d = pad_d[pl.ds(base, rc), :]
            win_u = pad_u[pl.ds(base, rc + CONV_PAD), :]
            win_d = pad_d[pl.ds(base, rc + CONV_PAD), :]
            shifts = {}
            for k in _by_residue(KW, lambda k: CONV_PAD - (KW - 1) + k):
                u_k = _rows(win_u, CONV_PAD - (KW - 1) + k, rc, shifts)
                dw_acc[k:k + 1, :] += jnp.sum(d * u_k, axis=0, keepdims=True)
            acc = jnp.zeros((rc, LANES), F32)
            shifts = {}
            for k in _by_residue(KW, lambda k: KW - 1 - k):
                acc = acc + w[k:k + 1, :] * _rows(win_d, KW - 1 - k, rc, shifts)
            du0[pl.ds(base, rc), :] = acc
            return 0

        lax.fori_loop(0, S // rc, chunk, 0)
        dw_ref[...] = dw_acc[0:KW, :]
        a, sg = a_ref[...], _sigmoid(g_ref[...])
        d0 = du0[...]
        da_ref[...] = (d0 * sg).astype(BF16)
        dg_ref[...] = (d0 * a * sg * (1.0 - sg)).astype(BF16)

    col = lambda c: (0, c)
    in_specs, out_specs, out_shape, scratch = _carry_specs(
        rider, [pl.BlockSpec((S, LANES), col), pl.BlockSpec((S, LANES), col),
                pl.BlockSpec((S, LANES), lambda c: (0, c + nct)), pl.BlockSpec((KW, LANES), col)],
        (pl.BlockSpec((S, LANES), col), pl.BlockSpec((S, LANES), col), pl.BlockSpec((KW, LANES), col),
         pl.BlockSpec((1, LANES), col)),
        (pltpu.HBM((S, C), BF16), pltpu.HBM((S, C), BF16), pltpu.HBM((KW, C), F32), pltpu.HBM((1, C), F32)),
        [pltpu.VMEM((S + CONV_PAD, LANES), F32), pltpu.VMEM((S + CONV_PAD, LANES), F32),
         pltpu.VMEM((S, LANES), F32), pltpu.VMEM((CONV_PAD, LANES), F32)])
    first = lambda: pl.program_id(0) == 0
    last = lambda: pl.program_id(0) == nct - 1
    res = _pallas(
        _carry(rider, body, 4, 4, first, last), name=name, grid=(nct,), out_shape=out_shape,
        in_specs=in_specs, out_specs=out_specs, scratch_shapes=scratch,
        compiler_params=_params(8 * _nbytes((S, LANES), F32), ("arbitrary",)))(
            du1, proj, proj, conv_w, *(rider.arrays if rider else ()))
    return res[:4], list(res[4:])


def ln_silu(u1, o_sb, gamma, beta, *, tm, name):
    S, C = u1.shape

    def body(u_ref, o_ref, g_ref, b_ref, out_ref):
        z = u_ref[...]
        mu = jnp.mean(z, axis=-1, keepdims=True)
        zc = z - mu
        y = zc * lax.rsqrt(jnp.mean(zc * zc, axis=-1, keepdims=True) + LN_EPS) * g_ref[...] + b_ref[...]
        out_ref[:, 0:C] = (y * _sigmoid(y)).astype(BF16)
        out_ref[:, C:] = o_ref[...].astype(BF16)

    row = lambda i: (i, 0)
    fix = lambda i: (0, 0)
    return _pallas(
        body, name=name, out_shape=pltpu.HBM((S, 2 * C), BF16), grid=(S // tm,),
        in_specs=[pl.BlockSpec((tm, C), row), pl.BlockSpec((tm, C), row), pl.BlockSpec((1, C), fix),
                  pl.BlockSpec((1, C), fix)],
        out_specs=pl.BlockSpec((tm, 2 * C), row),
        compiler_params=_params(4 * _nbytes((tm, C), F32), ("parallel",)))(u1, o_sb, gamma, beta)


def ln_silu_bwd(dua, u1, gamma, beta, *, tm, name):
    S, C = u1.shape

    def body(d_ref, u_ref, g_ref, b_ref, du1_ref, dg_ref, db_ref):
        @pl.when(pl.program_id(0) == 0)
        def _():
            dg_ref[...] = jnp.zeros_like(dg_ref)
            db_ref[...] = jnp.zeros_like(db_ref)

        z = u_ref[...]
        mu = jnp.mean(z, axis=-1, keepdims=True)
        zc = z - mu
        rstd = lax.rsqrt(jnp.mean(zc * zc, axis=-1, keepdims=True) + LN_EPS)
        zh = zc * rstd
        y = zh * g_ref[...] + b_ref[...]
        sg = _sigmoid(y)
        dy = d_ref[...] * (sg * (1.0 + y * (1.0 - sg)))
        dg_ref[...] += jnp.sum(dy * zh, axis=0, keepdims=True)
        db_ref[...] += jnp.sum(dy, axis=0, keepdims=True)
        dzh = dy * g_ref[...]
        m1 = jnp.mean(dzh, axis=-1, keepdims=True)
        m2 = jnp.mean(dzh * zh, axis=-1, keepdims=True)
        du1_ref[...] = rstd * (dzh - m1 - zh * m2)

    row = lambda i: (i, 0)
    fix = lambda i: (0, 0)
    return _pallas(
        body, name=name, grid=(S // tm,),
        out_shape=(pltpu.HBM((S, C), F32), pltpu.HBM((1, C), F32),
                   pltpu.HBM((1, C), F32)),
        in_specs=[pl.BlockSpec((tm, C), row), pl.BlockSpec((tm, C), row), pl.BlockSpec((1, C), fix),
                  pl.BlockSpec((1, C), fix)],
        out_specs=(pl.BlockSpec((tm, C), row), pl.BlockSpec((1, C), fix), pl.BlockSpec((1, C), fix)),
        compiler_params=_params(4 * _nbytes((tm, C), F32), ("arbitrary",)))(dua, u1, gamma, beta)


SB_BLOCK = 256
SB_STOP = -105.0
SB_GROUP = 4


def _split_dot(x, tri):
    hi = x.astype(BF16)
    lo = (x - hi.astype(F32)).astype(BF16)
    return _dot(hi, tri) + _dot(lo, tri)


def _neg_softplus(z):
    return -(jnp.maximum(z, 0.0) + jnp.log(1.0 + jnp.exp(-jnp.abs(z))))


def sb_fwd(proj, *, q_col, name, rider=None):
    S = proj.shape[0]
    dh = LANES // 2
    W = SB_HEADS * dh
    BW = SB_GROUP * dh
    ngrp = W // BW
    T = min(SB_BLOCK, S)
    nblk = S // T
    scale = dh ** -0.5
    qb0 = q_col // BW
    heads = range(SB_GROUP)
    sl = [slice(h * dh, (h + 1) * dh) for h in heads]

    def body(q_ref, k_ref, v_ref, o_ref, l_ref, qs):
        r_i = lax.broadcasted_iota(jnp.int32, (T, T), 0)
        c_i = lax.broadcasted_iota(jnp.int32, (T, T), 1)
        tri = (r_i >= c_i).astype(BF16)
        vis = c_i < r_i
        lane = lax.broadcasted_iota(jnp.int32, (T, dh), 1)

        qs[...] = (q_ref[...] * scale).astype(BF16)

        def step(qb, blocks, st):
            nb = range(len(blocks))
            kb = [[k_ref[pl.ds(j0, T), sl[h]].astype(BF16) for h in heads] for j0, _ in blocks]
            vb = [[v_ref[pl.ds(j0, T), sl[h]].astype(BF16) for h in heads] for j0, _ in blocks]
            z = [[_dot_nt(qb[h], kb[b][h]) for h in heads] for b in nb]
            lk = [[_neg_softplus(z[b][h]) for h in heads] for b in nb]
            lk = [[jnp.where(vis, lk[b][h], 0.0) if blocks[b][1] else lk[b][h] for h in heads] for b in nb]
            C = [[_split_dot(lk[b][h], tri) for h in heads] for b in nb]
            R = [[st[2 * h + 1] for h in heads]]
            for b in nb:
                R.append([R[b][h] + C[b][h][:, 0:1] for h in heads])
            A = [[jnp.exp(z[b][h] + C[b][h] + R[b][h]) for h in heads] for b in nb]
            A = [[jnp.where(vis, A[b][h], 0.0) if blocks[b][1] else A[b][h] for h in heads] for b in nb]
            out = ()
            for h in heads:
                acc = st[2 * h]
                for b in nb:
                    acc = acc + _dot(A[b][h].astype(BF16), vb[b][h])
                out += (acc, R[-1][h])
            return out

        zero = (jnp.zeros((T, dh), F32), jnp.zeros((T, 1), F32))

        def finish(r0, i, c):
            walked = jnp.asarray(i - c[0]).astype(F32)
            for h in heads:
                o_ref[pl.ds(r0, T), sl[h]] = c[1 + 2 * h]
                l_ref[pl.ds(r0, T), sl[h]] = jnp.where(lane == 1, walked, c[2 + 2 * h])

        finish(0, 0, (-1,) + step([qs[0:T, sl[h]] for h in heads], [(0, True)], zero * SB_GROUP))

        def qblock(i, _):
            r0 = pl.multiple_of(i * T, T)
            qb = [qs[pl.ds(r0, T), sl[h]] for h in heads]
            state = step(qb, [(r0, True), (pl.multiple_of(r0 - T, T), False)], zero * SB_GROUP)

            def more(c):
                worst = c[2]
                for h in heads[1:]:
                    worst = jnp.maximum(worst, c[2 + 2 * h])
                return (c[0] >= 0) & (jnp.max(worst) >= SB_STOP)

            def walk(c):
                return (c[0] - 1,) + step(qb, [(pl.multiple_of(c[0] * T, T), False)], c[1:])

            finish(r0, i, lax.while_loop(more, walk, (i - 2,) + state))
            return 0

        lax.fori_loop(1, nblk, qblock, 0)

    blk = lambda off: pl.BlockSpec((S, BW), lambda g: (0, qb0 + off * ngrp + g), pipeline_mode=pl.Buffered(1))
    out = pl.BlockSpec((S, BW), lambda g: (0, g))
    in_specs, out_specs, out_shape, scratch = _carry_specs(
        rider, [blk(0), blk(1), blk(2)], (out, out), (pltpu.HBM((S, W), F32), pltpu.HBM((S, W), F32)),
        [pltpu.VMEM((S, BW), BF16)])
    first = lambda: pl.program_id(0) == 0
    last = lambda: pl.program_id(0) == ngrp - 1
    res = _pallas(
        _carry(rider, body, 3, 2, first, last), name=name, grid=(ngrp,), out_shape=out_shape,
        in_specs=in_specs, out_specs=out_specs, scratch_shapes=scratch,
        compiler_params=_params(5 * _nbytes((S, BW), F32), ("arbitrary",)))(
            proj, proj, proj, *(rider.arrays if rider else ()))
    return res[0], res[1], list(res[2:])


def sb_bwd(proj, ltot, dua, *, q_col, do_col, name, rider=None):
    S = proj.shape[0]
    dh = LANES // 2
    W = SB_HEADS * dh
    BW = SB_GROUP * dh
    ngrp = W // BW
    T = min(SB_BLOCK, S)
    nblk = S // T
    scale = dh ** -0.5
    qb0 = q_col // BW
    db0 = do_col // BW
    heads = range(SB_GROUP)
    sl = [slice(h * dh, (h + 1) * dh) for h in heads]

    def body(q_ref, k_ref, v_ref, l_ref, do_ref, dq_ref, dk_ref, dv_ref, dks, dvs):
        r_i = lax.broadcasted_iota(jnp.int32, (T, T), 0)
        c_i = lax.broadcasted_iota(jnp.int32, (T, T), 1)
        tri_rev = (r_i >= c_i).astype(BF16)
        tri_fwd = (r_i <= c_i).astype(BF16)
        vis = c_i < r_i

        dks[...] = jnp.zeros_like(dks)
        dvs[...] = jnp.zeros_like(dvs)

        def step(qb, dob, Lt, blocks, st):
            nb = range(len(blocks))
            kb = [[k_ref[pl.ds(j0, T), sl[h]].astype(BF16) for h in heads] for j0, _ in blocks]
            vb = [[v_ref[pl.ds(j0, T), sl[h]].astype(BF16) for h in heads] for j0, _ in blocks]
            z = [[_dot_nt(qb[h], kb[b][h]) for h in heads] for b in nb]
            dA =[[_dot_nt(dob[h], vb[b][h]) for h in heads] for b in nb]
            lk = [[_neg_softplus(z[b][h]) for h in heads] for b in nb]
            beta = [[jnp.exp(z[b][h] + lk[b][h]) for h in heads] for b in nb]
            lk = [[jnp.where(vis, lk[b][h], 0.0) if blocks[b][1] else lk[b][h] for h in heads] for b in nb]
            C = [[_split_dot(lk[b][h], tri_rev) for h in heads] for b in nb]
            P = [[st[3 * h + 1] for h in heads]]
            for b in nb:
                P.append([P[b][h] + C[b][h][:, 0:1] for h in heads])
            A = [[jnp.exp(z[b][h] + C[b][h] + (Lt[h] - P[b + 1][h])) for h in heads] for b in nb]
            A = [[jnp.where(vis, A[b][h], 0.0) if blocks[b][1] else A[b][h] for h in heads] for b in nb]
            g = [[A[b][h] * dA[b][h] for h in heads] for b in nb]
            Gin = [[_split_dot(g[b][h], tri_fwd) for h in heads] for b in nb]
            Gp = [[st[3 * h + 2] for h in heads]]
            for b in nb:
                Gp.append([Gp[b][h] + Gin[b][h][:, T - 1:T] for h in heads])
            dz = [[g[b][h] - beta[b][h] * (Gp[b][h] + Gin[b][h]) for h in heads] for b in nb]
            dz = [[jnp.where(vis, dz[b][h], 0.0) if blocks[b][1] else dz[b][h] for h in heads] for b in nb]
            dzb = [[dz[b][h].astype(BF16) for h in heads] for b in nb]
            out = ()
            for h in heads:
                dq = st[3 * h]
                for b in nb:
                    j0 = blocks[b][0]
                    dvs[pl.ds(j0, T), sl[h]] += _dot_tn(A[b][h].astype(BF16), dob[h])
                    dks[pl.ds(j0, T), sl[h]] += _dot_tn(dzb[b][h], qb[h])
                    dq = dq + _dot(dzb[b][h], kb[b][h])
                out += (dq, P[-1][h], Gp[-1][h])
            return out

        zero = jnp.zeros((T, 1), F32)
        init = (jnp.zeros((T, dh), F32), zero, zero)

        def operands(r0):
            return ([(q_ref[pl.ds(r0, T), sl[h]] * scale).astype(BF16) for h in heads],
                    [do_ref[pl.ds(r0, T), sl[h]].astype(BF16) for h in heads],
                    [l_ref[pl.ds(r0, T), h * dh:h * dh + 1] for h in heads])

        def finish(r0, c):
            for h in heads:
                dq_ref[pl.ds(r0, T), sl[h]] = (c[3 * h] * scale).astype(BF16)

        finish(0, step(*operands(0), [(0, True)], init * SB_GROUP))

        def qblock(i, _):
            r0 = pl.multiple_of(i * T, T)
            qb, dob, Lt = operands(r0)
            walked = jnp.clip(jnp.max(l_ref[pl.ds(r0, 8), 1:2]).astype(jnp.int32), 2, i + 1)

            def inner(j, c):
                return step(qb, dob, Lt, [(pl.multiple_of(j * T, T), False)], c)

            c = lax.fori_loop(i + 1 - walked, i - 1, inner, init * SB_GROUP)
            finish(r0, step(qb, dob, Lt, [(pl.multiple_of(r0 - T, T), False), (r0, True)], c))
            return 0

        lax.fori_loop(1, nblk, qblock, 0)
        dk_ref[...] = dks[...].astype(BF16)
        dv_ref[...] = dvs[...].astype(BF16)

    once = pl.Buffered(1)
    blk = lambda off: pl.BlockSpec((S, BW), lambda g: (0, qb0 + off * ngrp + g), pipeline_mode=once)
    out = pl.BlockSpec((S, BW), lambda g: (0, g))
    o_shape = pltpu.HBM((S, W), BF16)
    in_specs, out_specs, out_shape, scratch = _carry_specs(
        rider, [blk(0), blk(1), blk(2), pl.BlockSpec((S, BW), lambda g: (0, g), pipeline_mode=once),
                pl.BlockSpec((S, BW), lambda g: (0, db0 + g), pipeline_mode=once)], (out, out, out),
        (o_shape, o_shape, o_shape), [pltpu.VMEM((S, BW), F32)] * 2)
    first = lambda: pl.program_id(0) == 0
    last = lambda: pl.program_id(0) == ngrp - 1
    res = _pallas(
        _carry(rider, body, 5, 3, first, last), name=name, grid=(ngrp,), out_shape=out_shape,
        in_specs=in_specs, out_specs=out_specs, scratch_shapes=scratch,
        compiler_params=_params(6 * _nbytes((S, BW), F32), ("arbitrary",)))(
            proj, proj, proj, ltot, dua, *(rider.arrays if rider else ()))
    return res[0], res[1], res[2], list(res[3:])


def xattn_fwd(q, k, v, *, tm, name):
    S, D = q.shape
    Mlen = k.shape[0]
    hd = D // MEM_HEADS
    scale = hd ** -0.5

    def body(q_ref, k_ref, v_ref, o_ref):
        for h in range(MEM_HEADS):
            sl = slice(h * hd, (h + 1) * hd)
            s = _dot_nt(q_ref[:, sl], k_ref[:, sl]) * scale
            e = jnp.exp(s - jnp.max(s, axis=-1, keepdims=True))
            p = e / jnp.sum(e, axis=-1, keepdims=True)
            o_ref[:, sl] = _dot(p.astype(BF16), v_ref[:, sl]).astype(BF16)

    row = lambda i: (i, 0)
    fix = lambda i: (0, 0)
    return _pallas(
        body, name=name, out_shape=pltpu.HBM((S, D), BF16), grid=(S // tm,),
        in_specs=[pl.BlockSpec((tm, D), row), pl.BlockSpec((Mlen, D), fix), pl.BlockSpec((Mlen, D), fix)],
        out_specs=pl.BlockSpec((tm, D), row),
        compiler_params=_params(4 * _nbytes((tm, D), F32), ("parallel",)))(q, k, v)


def xattn_bwd(q, do, k, v, *, tm, name):
    S, D = q.shape
    Mlen = k.shape[0]
    hd = D // MEM_HEADS
    scale = hd ** -0.5

    def body(q_ref, do_ref, k_ref, v_ref, dq_ref, dk_ref, dv_ref):
        @pl.when(pl.program_id(0) == 0)
        def _():
            dk_ref[...] = jnp.zeros_like(dk_ref)
            dv_ref[...] = jnp.zeros_like(dv_ref)

        for h in range(MEM_HEADS):
            sl = slice(h * hd, (h + 1) * hd)
            qh, doh, kh, vh = q_ref[:, sl], do_ref[:, sl], k_ref[:, sl], v_ref[:, sl]
            s = _dot_nt(qh, kh) * scale
            e = jnp.exp(s - jnp.max(s, axis=-1, keepdims=True))
            p = e / jnp.sum(e, axis=-1, keepdims=True)
            dp = _dot_nt(doh, vh)
            ds = (p * (dp - jnp.sum(p * dp, axis=-1, keepdims=True)) * scale).astype(BF16)
            dq_ref[:, sl] = _dot(ds, kh).astype(BF16)
            dk_ref[:, sl] += _dot_tn(ds, qh)
            dv_ref[:, sl] += _dot_tn(p.astype(BF16), doh)

    row = lambda i: (i, 0)
    fix = lambda i: (0, 0)
    return _pallas(
        body, name=name, grid=(S // tm,),
        out_shape=(pltpu.HBM((S, D), BF16), pltpu.HBM((Mlen, D), F32),
                   pltpu.HBM((Mlen, D), F32)),
        in_specs=[pl.BlockSpec((tm, D), row), pl.BlockSpec((tm, D), row), pl.BlockSpec((Mlen, D), fix),
                  pl.BlockSpec((Mlen, D), fix)],
        out_specs=(pl.BlockSpec((tm, D), row), pl.BlockSpec((Mlen, D), fix), pl.BlockSpec((Mlen, D), fix)),
        compiler_params=_params(6 * _nbytes((tm, D), F32), ("arbitrary",)))(q, do, k, v)


FFN_HALO = 8


def _conv3(ext, w, lo):
    tm = ext.shape[0] - FFN_HALO
    return (w[0:1, :] * ext[lo:lo + tm, :] + w[1:2, :] * ext[lo + 1:lo + 1 + tm, :]
            + w[2:3, :] * ext[lo + 2:lo + 2 + tm, :])


def ffn_up_fwd(xb, w_up, conv_w, conv_b, *, tm, tn, name, rider=None):
    S, D = xb.shape
    nsh, _, ns = w_up.shape
    F = nsh * ns // 2
    per = ns // tn
    ncol = F // tn
    KW = conv_w.shape[0]
    assert KW == 3

    def body(x_ref, wv_ref, wg_ref, cwv_ref, cwg_ref, cbv_ref, cbg_ref, uv_ref, ug_ref, mv_ref, mg_ref, h_ref,
             carry):
        @pl.when(pl.program_id(1) == 0)
        def _():
            carry[...] = jnp.zeros_like(carry)

        x = x_ref[...]
        uv = _dot(x, wv_ref[...])
        ug = _dot(x, wg_ref[...])
        uv_ref[...] = uv.astype(BF16)
        ug_ref[...] = ug.astype(BF16)
        lo = FFN_HALO - (KW - 1)
        cv = _conv3(jnp.concatenate([carry[0], uv], axis=0), cwv_ref[...], lo) + cbv_ref[...]
        cg = _conv3(jnp.concatenate([carry[1], ug], axis=0), cwg_ref[...], lo) + cbg_ref[...]
        carry[0] = uv[tm - FFN_HALO:, :]
        carry[1] = ug[tm - FFN_HALO:, :]
        sg = _sigmoid(cg)
        act = cg * sg
        mv_ref[...] = act.astype(BF16)
        mg_ref[...] = (cv * (sg + act * (1.0 - sg))).astype(BF16)
        h_ref[...] = (act * cv).astype(BF16)

    wspec = lambda half: pl.BlockSpec((None, D, tn), lambda j, i: (half * (nsh // 2) + j // per, 0, j % per))
    cspec = lambda rows, half: pl.BlockSpec((rows, tn), lambda j, i: (0, half * ncol + j))
    out = pl.BlockSpec((tm, tn), lambda j, i: (i, j))
    o_shape = pltpu.HBM((S, F), BF16)
    blk = _nbytes((tm, D), BF16) + 2 * _nbytes((D, tn), BF16) + 8 * _nbytes((tm, tn), F32)
    nrow = S // tm
    in_specs, out_specs, out_shape, scratch = _carry_specs(
        rider, [pl.BlockSpec((tm, D), lambda j, i: (i, 0)), wspec(0), wspec(1), cspec(KW, 0), cspec(KW, 1),
                cspec(1, 0), cspec(1, 1)], (out,) * 5, (o_shape,) * 5, [pltpu.VMEM((2, FFN_HALO, tn), F32)])
    first = lambda: (pl.program_id(0) == 0) & (pl.program_id(1) == 0)
    last = lambda: (pl.program_id(0) == ncol - 1) & (pl.program_id(1) == nrow - 1)
    res = _pallas(
        _carry(rider, body, 7, 5, first, last), name=name, grid=(ncol, nrow), out_shape=out_shape,
        in_specs=in_specs, out_specs=out_specs, scratch_shapes=scratch,
        compiler_params=_params(blk, ("arbitrary", "arbitrary")))(
            xb, w_up, w_up, conv_w, conv_w, conv_b, conv_b, *(rider.arrays if rider else ()))
    return res[:5], list(res[5:])


def ffn_mid_bwd(dzb, w_down, up_v, up_g, mult_v, mult_g, conv_w, *, tm, tn, name, rider=None):
    S, D = dzb.shape
    F = up_v.shape[1]
    ncol = F // tn
    nrow = S // tm
    KW = conv_w.shape[0]
    assert KW == 3

    def body(dz_ref, wd_ref, uv_ref, ug_ref, mv_ref, mg_ref, cwv_ref, cwg_ref,
             dv_ref, dg_ref, dwv_ref, dwg_ref, dbv_ref, dbg_ref, carry):
        @pl.when(pl.program_id(1) == 0)
        def _():
            carry[...] = jnp.zeros_like(carry)
            for r in (dwv_ref, dwg_ref, dbv_ref, dbg_ref):
                r[...] = jnp.zeros_like(r)

        dh = _dot_nt(dz_ref[...], wd_ref[...])
        dcv = dh * mv_ref[...].astype(F32)
        dcg = dh * mg_ref[...].astype(F32)

        def back(dc, u_ref, cw, slot, du_ref, dw_ref, db_ref):
            ext = jnp.concatenate([dc, carry[slot]], axis=0)
            ahead = [dc, ext[1:tm + 1, :], ext[2:tm + 2, :]]
            du = cw[2:3, :] * ahead[0] + cw[1:2, :] * ahead[1] + cw[0:1, :] * ahead[2]
            du_ref[...] = du.astype(BF16)
            carry[slot] = dc[0:FFN_HALO, :]
            u = u_ref[...].astype(F32)
            for k in range(KW):
                dw_ref[k:k + 1, :] += jnp.sum(ahead[KW - 1 - k] * u, axis=0, keepdims=True)
            db_ref[...] += jnp.sum(dc, axis=0, keepdims=True)

        back(dcv, uv_ref, cwv_ref[...], 0, dv_ref, dwv_ref, dbv_ref)
        back(dcg, ug_ref, cwg_ref[...], 1, dg_ref, dwg_ref, dbg_ref)

    rev = lambda i: nrow - 1 - i
    tile = pl.BlockSpec((tm, tn), lambda j, i: (rev(i), j))
    cspec = lambda half: pl.BlockSpec((KW, tn), lambda j, i: (0, half * ncol + j))
    acc = lambda rows: pl.BlockSpec((rows, tn), lambda j, i: (0, j))
    big = pltpu.HBM((S, F), BF16)
    blk = _nbytes((tm, D), BF16) + _nbytes((tn, D), BF16) + 10 * _nbytes((tm, tn), F32)
    in_specs, out_specs, out_shape, scratch = _carry_specs(
        rider, [pl.BlockSpec((tm, D), lambda j, i: (rev(i), 0)), pl.BlockSpec((tn, D), lambda j, i: (j, 0)),
                tile, tile, tile, tile, cspec(0), cspec(1)],
        (tile, tile, acc(KW), acc(KW), acc(1), acc(1)),
        (big, big, pltpu.HBM((KW, F), F32), pltpu.HBM((KW, F), F32), pltpu.HBM((1, F), F32),
         pltpu.HBM((1, F), F32)), [pltpu.VMEM((2, FFN_HALO, tn), F32)])
    first = lambda: (pl.program_id(0) == 0) & (pl.program_id(1) == 0)
    last = lambda: (pl.program_id(0) == ncol - 1) & (pl.program_id(1) == nrow - 1)
    res = _pallas(
        _carry(rider, body, 8, 6, first, last), name=name, grid=(ncol, nrow), out_shape=out_shape,
        in_specs=in_specs, out_specs=out_specs, scratch_shapes=scratch,
        compiler_params=_params(blk, ("arbitrary", "arbitrary")))(
            dzb, w_down, up_v, up_g, mult_v, mult_g, conv_w, conv_w, *(rider.arrays if rider else ()))
    return res[:6], list(res[6:])


def loss_head(y, target, *, tm, name):
    S, D = y.shape

    def body(y_ref, t_ref, dy_ref, l_ref):
        @pl.when(pl.program_id(0) == 0)
        def _():
            l_ref[...] = jnp.zeros_like(l_ref)

        e = y_ref[...] - t_ref[...]
        dy_ref[...] = e * (1.0 / D)
        l_ref[...] += 0.5 * jnp.sum(jnp.mean(e * e, axis=-1, keepdims=True), axis=0, keepdims=True)

    row = lambda i: (i, 0)
    return _pallas(
        body, name=name, grid=(S // tm,),
        out_shape=(pltpu.HBM((S, D), F32), pltpu.HBM((1, 1), F32)),
        in_specs=[pl.BlockSpec((tm, D), row), pl.BlockSpec((tm, D), row)],
        out_specs=(pl.BlockSpec((tm, D), row), pl.BlockSpec((1, 1), lambda i: (0, 0))),
        compiler_params=_params(3 * _nbytes((tm, D), F32), ("arbitrary",)))(y, target)


def adamw(w, g, m, v, *, tr, name):
    R, C = w.shape
    c1 = 1.0 - ADAM_B1 ** ADAM_STEP
    c2 = 1.0 - ADAM_B2 ** ADAM_STEP

    def body(w_ref, g_ref, m_ref, v_ref, go_ref, d_ref, mo_ref, vo_ref):
        gv = g_ref[...]
        mn = ADAM_B1 * m_ref[...] + (1.0 - ADAM_B1) * gv
        vn = ADAM_B2 * v_ref[...] + (1.0 - ADAM_B2) * (gv * gv)
        go_ref[...] = gv
        mo_ref[...] = mn
        vo_ref[...] = vn
        d_ref[...] = -ADAM_LR * ((mn / c1) / (jnp.sqrt(vn / c2) + ADAM_EPS) + ADAM_WD * w_ref[...])

    spec = pl.BlockSpec((tr, C), lambda i: (i, 0))
    shape = pltpu.HBM((R, C), F32)
    return _pallas(
        body, name=name, grid=(R // tr,), out_shape=(shape,) * 4, in_specs=[spec] * 4, out_specs=(spec,) * 4,
        compiler_params=_params(8 * _nbytes((tr, C), F32), ("parallel",)))(w, g, m, v)


def add_pairs(gs, gots, core, *, name):
    k = len(gs)

    def body(c_ref, *refs):
        for a_ref, b_ref, o_ref in zip(refs[:k], refs[k:2 * k], refs[2 * k:]):
            o_ref[...] = (a_ref[...].astype(F32) + b_ref[...].astype(F32)).astype(BF16)

    own = [pl.BlockSpec((None, None) + g.shape[2:], lambda i, c: (i, c[0], 0, 0)) for g in gs]
    half = [pl.BlockSpec((None,) + g.shape[1:], lambda i, c: (i, 0, 0)) for g in gots]
    grid_spec = pltpu.PrefetchScalarGridSpec(
        num_scalar_prefetch=1, grid=(N_CHIPS,), in_specs=own + half, out_specs=tuple(half))
    blk = 3 * sum(_nbytes(g.shape[1:], BF16) for g in gots)
    return _pallas(
        body, name=name, grid_spec=grid_spec, out_shape=tuple(pltpu.HBM(g.shape, BF16) for g in gots),
        compiler_params=_params(blk, ("parallel",)))(core, *gs, *gots)


def sum_chips_into(bs, dests, layer, core, *, name):
    k = len(bs)
    steps = 2

    def body(c_ref, *refs):
        for b_ref, o_ref in zip(refs[:k], refs[2 * k:]):
            acc = b_ref[0].astype(F32)
            for p in range(1, N_CHIPS):
                acc = acc + b_ref[p].astype(F32)
            o_ref[...] = acc

    ins = [pl.BlockSpec((N_CHIPS, b.shape[1] // steps, b.shape[2]), lambda i, c: (0, i, 0)) for b in bs]
    outs = tuple(pl.BlockSpec((None, None, b.shape[1] // steps, b.shape[2]), lambda i, c: (layer, c[0], i, 0))
                 for b in bs)
    grid_spec = pltpu.PrefetchScalarGridSpec(
        num_scalar_prefetch=1, grid=(steps,), in_specs=ins + [pl.BlockSpec(memory_space=pl.ANY)] * k,
        out_specs=outs)
    blk = sum(_nbytes(b.shape, BF16) + _nbytes(b.shape[1:], F32) for b in bs) // steps
    return _pallas(
        body, name=name, grid_spec=grid_spec, out_shape=tuple(pltpu.HBM(d.shape, F32) for d in dests),
        input_output_aliases={1 + k + w: w for w in range(k)},
        compiler_params=_params(blk, ("parallel",)))(core, *bs, *dests)


_HBM = pl.BlockSpec(memory_space=pltpu.HBM)


def _place():
    x, y, c = lax.axis_index("x"), lax.axis_index("y"), lax.axis_index("c")
    chips = [(1 - x, y), (x, 1 - y), (1 - x, 1 - y)]
    return x, y, c, chips


class GatherRider:
    def __init__(self, shards):
        self.arrays = list(shards)
        self.n = n = len(shards)
        self.out_shape = tuple(pltpu.HBM((N_CHIPS,) + s.shape, s.dtype) for s in shards)
        self.scratch = [pltpu.SemaphoreType.DMA((n, 3))] * 4 + [pltpu.SemaphoreType.DMA((n,))]

    def _copies(self, ins, outs, sems):
        send_ici, recv_ici, send_d2d, recv_d2d, local = sems
        x, y, c, chips = _place()
        me = 2 * x + y

        def own(w):
            return pltpu.make_async_copy(ins[w], outs[w].at[me], local.at[w])

        def ici(w, j):
            px, py = chips[j]
            return pltpu.make_async_remote_copy(
                src_ref=ins[w].at[c], dst_ref=outs[w].at[me, c], send_sem=send_ici.at[w, j],
                recv_sem=recv_ici.at[w, j], device_id=(px, py, c), device_id_type=MESH)

        def landed(w, j, half):
            px, py = chips[j]
            return outs[w].at[2 * px + py, half]

        def d2d(w, j, half):
            return pltpu.make_async_remote_copy(
                src_ref=landed(w, j, half), dst_ref=landed(w, j, half), send_sem=send_d2d.at[w, j],
                recv_sem=recv_d2d.at[w, j], device_id=(x, y, 1 - c), device_id_type=MESH)

        def ici_arrival(w, j):
            return pltpu.make_async_remote_copy(
                src_ref=landed(w, j, c), dst_ref=landed(w, j, c), send_sem=send_ici.at[w, j],
                recv_sem=recv_ici.at[w, j], device_id=(x, y, c), device_id_type=MESH)

        return c, own, ici, d2d, ici_arrival

    def start(self, ins, outs, sems):
        c, own, ici, d2d, ici_arrival = self._copies(ins, outs, sems)
        for w in range(self.n):
            own(w).start()
            for j in range(3):
                ici(w, j).start()

    def finish(self, ins, outs, sems):
        c, own, ici, d2d, ici_arrival = self._copies(ins, outs, sems)
        for w in range(self.n):
            for j in range(3):
                ici_arrival(w, j).wait_recv()
                d2d(w, j, c).start()
        for w in range(self.n):
            for j in range(3):
                d2d(w, j, 1 - c).wait_recv()
        for w in range(self.n):
            for j in range(3):
                ici(w, j).wait_send()
                d2d(w, j, c).wait_send()
            own(w).wait()


class ScatterRider:
    def __init__(self, parts):
        self.arrays = list(parts)
        self.n = n = len(parts)
        self.out_shape = tuple(pltpu.HBM(p.shape, p.dtype) for p in parts)
        self.scratch = [pltpu.SemaphoreType.DMA((n, 3))] * 2 + [pltpu.SemaphoreType.DMA((n,))]

    def _copies(self, ins, outs, sems):
        send, recv, local = sems
        x, y, c, chips = _place()
        me = 2 * x + y

        def own(w):
            return pltpu.make_async_copy(ins[w].at[me], outs[w].at[me], local.at[w])

        def copy(w, j):
            px, py = chips[j]
            return pltpu.make_async_remote_copy(
                src_ref=ins[w].at[2 * px + py], dst_ref=outs[w].at[me], send_sem=send.at[w, j],
                recv_sem=recv.at[w, j], device_id=(px, py, c), device_id_type=MESH)

        def arrival(w, j):
            px, py = chips[j]
            blk = outs[w].at[2 * px + py]
            return pltpu.make_async_remote_copy(
                src_ref=blk, dst_ref=blk, send_sem=send.at[w, j], recv_sem=recv.at[w, j],
                device_id=(x, y, c), device_id_type=MESH)

        return own, copy, arrival

    def start(self, ins, outs, sems):
        own, copy, arrival = self._copies(ins, outs, sems)
        for w in range(self.n):
            own(w).start()
            for j in range(3):
                copy(w, j).start()

    def finish(self, ins, outs, sems):
        own, copy, arrival = self._copies(ins, outs, sems)
        for w in range(self.n):
            for j in range(3):
                arrival(w, j).wait_recv()
        for w in range(self.n):
            for j in range(3):
                copy(w, j).wait_send()
            own(w).wait()


def _carry(rider, body, n_in, n_out, first, last):
    if rider is None:
        return body
    k, m = rider.n, len(rider.scratch)

    def carried(*refs):
        ins, r_in = refs[:n_in], refs[n_in:n_in + k]
        outs, r_out = refs[n_in + k:n_in + k + n_out], refs[n_in + k + n_out:n_in + 2 * k + n_out]
        rest = refs[n_in + 2 * k + n_out:]
        scratch, sems = rest[:len(rest) - m], rest[len(rest) - m:]

        @pl.when(first())
        def _():
            rider.start(r_in, r_out, sems)

        body(*ins, *outs, *scratch)

        @pl.when(last())
        def _():
            rider.finish(r_in, r_out, sems)

    return carried


def _carry_specs(rider, in_specs, out_specs, out_shape, scratch):
    if rider is None:
        return list(in_specs), tuple(out_specs), tuple(out_shape), list(scratch)
    k = rider.n
    return (list(in_specs) + [_HBM] * k, tuple(out_specs) + (_HBM,) * k, tuple(out_shape) + rider.out_shape,
            list(scratch) + list(rider.scratch))


def run_rider(rider, *, name):
    k = rider.n

    def body(*refs):
        rider.start(refs[:k], refs[k:2 * k], refs[2 * k:])
        rider.finish(refs[:k], refs[k:2 * k], refs[2 * k:])

    return _pallas(body, name=name, out_shape=rider.out_shape, in_specs=[_HBM] * k, out_specs=(_HBM,) * k,
                   scratch_shapes=rider.scratch)(*rider.arrays)


def allgather_small(shards, *, name):
    n = len(shards)

    def body(*refs):
        ins, outs = refs[:n], refs[n:2 * n]
        send, recv, local = refs[2 * n:]
        x, y, c, chips = _place()
        me = 2 * x + y
        locals_ = [pltpu.make_async_copy(ins[w], outs[w].at[me], local.at[w]) for w in range(n)]
        for cp in locals_:
            cp.start()

        def copy(w, j):
            px, py = chips[j]
            return pltpu.make_async_remote_copy(
                src_ref=ins[w], dst_ref=outs[w].at[me], send_sem=send.at[w, j], recv_sem=recv.at[w, j],
                device_id=(px, py, c), device_id_type=MESH)

        def arrival(w, j):
            px, py = chips[j]
            blk = outs[w].at[2 * px + py]
            return pltpu.make_async_remote_copy(
                src_ref=blk, dst_ref=blk, send_sem=send.at[w, j], recv_sem=recv.at[w, j],
                device_id=(x, y, c), device_id_type=MESH)

        for w in range(n):
            for j in range(3):
                copy(w, j).start()
        for w in range(n):
            for j in range(3):
                arrival(w, j).wait_recv()
        for w in range(n):
            for j in range(3):
                copy(w, j).wait_send()
        for cp in locals_:
            cp.wait()

    out_shape = tuple(pltpu.HBM((N_CHIPS,) + s.shape, s.dtype) for s in shards)
    return _pallas(
        body, name=name, out_shape=out_shape, in_specs=[_HBM] * n, out_specs=(_HBM,) * n,
        scratch_shapes=[pltpu.SemaphoreType.DMA((n, 3))] * 2 + [pltpu.SemaphoreType.DMA((n,))],
    )(*shards)


class SwapRider:
    def __init__(self, grads):
        self.arrays = list(grads)
        self.n = n = len(grads)
        self.out_shape = tuple(pltpu.HBM((N_CHIPS,) + g.shape[2:], g.dtype) for g in grads)
        self.scratch = [pltpu.SemaphoreType.DMA((n,))] * 2

    def _copies(self, ins, outs, sems):
        send, recv = sems
        x, y, c, _ = _place()
        return [pltpu.make_async_remote_copy(
            src_ref=ins[w].at[:, 1 - c], dst_ref=outs[w], send_sem=send.at[w], recv_sem=recv.at[w],
            device_id=(x, y, 1 - c), device_id_type=MESH) for w in range(self.n)]

    def start(self, ins, outs, sems):
        for cp in self._copies(ins, outs, sems):
            cp.start()

    def finish(self, ins, outs, sems):
        copies = self._copies(ins, outs, sems)
        for cp in copies:
            cp.wait_recv()
        for cp in copies:
            cp.wait_send()


def rs_sibling_share(stacked, *, name):
    n = len(stacked)

    def body(*refs):
        bufs = refs[n:2 * n]
        send, recv = refs[2 * n:]
        x, y, c, _ = _place()
        shares, arrivals = [], []
        for w in range(n):
            mine, other = bufs[w].at[:, c], bufs[w].at[:, 1 - c]
            shares.append(pltpu.make_async_remote_copy(
                src_ref=mine, dst_ref=mine, send_sem=send.at[w], recv_sem=recv.at[w],
                device_id=(x, y, 1 - c), device_id_type=MESH))
            arrivals.append(pltpu.make_async_remote_copy(
                src_ref=other, dst_ref=other, send_sem=send.at[w], recv_sem=recv.at[w],
                device_id=(x, y, c), device_id_type=MESH))
        for cp in shares:
            cp.start()
        for cp in arrivals:
            cp.wait_recv()
        for cp in shares:
            cp.wait_send()

    out_shape = tuple(pltpu.HBM(s.shape, F32) for s in stacked)
    return _pallas(
        body, name=name, out_shape=out_shape, in_specs=[_HBM] * n, out_specs=(_HBM,) * n,
        input_output_aliases={w: w for w in range(n)},
        scratch_shapes=[pltpu.SemaphoreType.DMA((n,))] * 2,
    )(*stacked)


def allreduce_small(v, *, name):
    R, C = v.shape

    def body(v_ref, o_ref, land, send, recv):
        x, y, c, _ = _place()
        me = 4 * x + 2 * y + c
        land[me] = v_ref[...]

        def flip(k):
            return (1 - x) if k & 4 else x, (1 - y) if k & 2 else y, (1 - c) if k & 1 else c

        copies = []
        for k in range(1, N_DEV):
            px, py, pc = flip(k)
            copies.append(pltpu.make_async_remote_copy(
                src_ref=v_ref, dst_ref=land.at[me], send_sem=send.at[k - 1], recv_sem=recv.at[k - 1],
                device_id=(px, py, pc), device_id_type=MESH))
        for cp in copies:
            cp.start()
        for k in range(1, N_DEV):
            px, py, pc = flip(k)
            blk = land.at[4 * px + 2 * py + pc]
            pltpu.make_async_remote_copy(
                src_ref=blk, dst_ref=blk, send_sem=send.at[k - 1], recv_sem=recv.at[k - 1],
                device_id=(x, y, c), device_id_type=MESH).wait_recv()
        for cp in copies:
            cp.wait_send()
        acc = land[0]
        for d in range(1, N_DEV):
            acc = acc + land[d]
        o_ref[...] = acc

    vm = pl.BlockSpec(memory_space=pltpu.VMEM)
    return pl.pallas_call(
        body, name=name, out_shape=jax.ShapeDtypeStruct((R, C), F32), in_specs=[vm], out_specs=vm,
        scratch_shapes=[pltpu.VMEM((N_DEV, R, C), F32), pltpu.SemaphoreType.DMA((N_DEV - 1,)),
                        pltpu.SemaphoreType.DMA((N_DEV - 1,))],
        compiler_params=pltpu.CompilerParams(vmem_limit_bytes=int(min(12 * R * C * 4 + (8 << 20), VMEM_CAP))),
    )(v)


def _pack(arrays):
    flat = jnp.concatenate([a.reshape(-1) for a in arrays])
    return flat.reshape(-1, LANES)


def _unpack(packed, shapes):
    flat = packed.reshape(-1)
    out, off = [], 0
    for s in shapes:
        n = 1
        for d in s:
            n *= d
        out.append(flat[off:off + n].reshape(s))
        off += n
    return out


def _row_tile(rows, cap=512):
    t = 1 << (cap.bit_length() - 1)
    while rows % t:
        t //= 2
    return t


def _adamw_tile(rows, cols):
    return _row_tile(rows, max(8, (1 << 20) // (4 * cols)))


def kernel(x, mem, w_in, conv_w, conv_b, conv_ln_g, conv_ln_b, w_out, ln1_g, ln1_b, mem_wq, mem_wk, mem_wv, mem_wo, ln2_g, ln2_b, ffn_up, ffn_conv_w, ffn_conv_b, ffn_down, ln3_g, ln3_b, loss_target, m_w_in, m_conv_w, m_conv_b, m_conv_ln_g, m_conv_ln_b, m_w_out, m_ln1_g, m_ln1_b, m_mem_wq, m_mem_wk, m_mem_wv, m_mem_wo, m_ln2_g, m_ln2_b, m_ffn_up, m_ffn_conv_w, m_ffn_conv_b, m_ffn_down, m_ln3_g, m_ln3_b, v_w_in, v_conv_w, v_conv_b, v_conv_ln_g, v_conv_ln_b, v_w_out, v_ln1_g, v_ln1_b, v_mem_wq, v_mem_wk, v_mem_wv, v_mem_wo, v_ln2_g, v_ln2_b, v_ffn_up, v_ffn_conv_w, v_ffn_conv_b, v_ffn_down, v_ln3_g, v_ln3_b):
    W = dict(w_in=w_in, conv_w=conv_w, conv_b=conv_b, conv_ln_g=conv_ln_g, conv_ln_b=conv_ln_b, w_out=w_out,
             ln1_g=ln1_g, ln1_b=ln1_b, mem_wq=mem_wq, mem_wk=mem_wk, mem_wv=mem_wv, mem_wo=mem_wo, ln2_g=ln2_g,
             ln2_b=ln2_b, ffn_up=ffn_up, ffn_conv_w=ffn_conv_w, ffn_conv_b=ffn_conv_b, ffn_down=ffn_down,
             ln3_g=ln3_g, ln3_b=ln3_b)
    M1 = dict(w_in=m_w_in, conv_w=m_conv_w, conv_b=m_conv_b, conv_ln_g=m_conv_ln_g, conv_ln_b=m_conv_ln_b,
              w_out=m_w_out, ln1_g=m_ln1_g, ln1_b=m_ln1_b, mem_wq=m_mem_wq, mem_wk=m_mem_wk, mem_wv=m_mem_wv,
              mem_wo=m_mem_wo, ln2_g=m_ln2_g, ln2_b=m_ln2_b, ffn_up=m_ffn_up, ffn_conv_w=m_ffn_conv_w,
              ffn_conv_b=m_ffn_conv_b, ffn_down=m_ffn_down, ln3_g=m_ln3_g, ln3_b=m_ln3_b)
    V2 = dict(w_in=v_w_in, conv_w=v_conv_w, conv_b=v_conv_b, conv_ln_g=v_conv_ln_g, conv_ln_b=v_conv_ln_b,
              w_out=v_w_out, ln1_g=v_ln1_g, ln1_b=v_ln1_b, mem_wq=v_mem_wq, mem_wk=v_mem_wk, mem_wv=v_mem_wv,
              mem_wo=v_mem_wo, ln2_g=v_ln2_g, ln2_b=v_ln2_b, ffn_up=v_ffn_up, ffn_conv_w=v_ffn_conv_w,
              ffn_conv_b=v_ffn_conv_b, ffn_down=v_ffn_down, ln3_g=v_ln3_g, ln3_b=v_ln3_b)

    L = w_in.shape[0]
    S, D = x.shape[1], x.shape[2]
    C = conv_b.shape[1]
    alpha = (2.0 * L) ** 0.25
    chip = 2 * lax.axis_index("x") + lax.axis_index("y")
    xs, mems, tgt = x[0], mem[0], loss_target[0]
    mem_bf = mems.astype(BF16)
    tm = _row_tile(S)
    tm_ffn = _row_tile(S, 256)
    tm_big = _row_tile(S, 1024)
    tm_half = _row_tile(S, 2048)

    def shards_of(l, names):
        out = []
        for n in names:
            wl = W[n][l].astype(BF16)
            out.append(wl.reshape(2, wl.shape[0] // 2, wl.shape[1]))
        return out

    def gathered(names, got):
        layer = {}
        for n, g in zip(names, got):
            rows, cols = W[n].shape[1], W[n].shape[2]
            layer[n] = g.reshape(N_CHIPS, rows, cols) if n in COL_SHARDED else g.reshape(N_CHIPS * rows, cols)
        return layer

    full = [dict() for _ in range(L)]
    full[0].update(gathered(RIDE_IN, run_rider(GatherRider(shards_of(0, RIDE_IN)), name="allgather_w_in")))
    cw_all, fcw_all = allgather_small([conv_w, ffn_conv_w], name="allgather_small")
    cw_full = jnp.transpose(cw_all, (1, 2, 0, 3)).reshape(L, conv_w.shape[1], -1)
    fcw_full = jnp.transpose(fcw_all, (1, 2, 0, 3)).reshape(L, ffn_conv_w.shape[1], -1)

    saved = []
    h, hb = xs, xs.astype(BF16)
    for l in range(L):
        fw = full[l]
        s = dict(x=h, xb=hb)
        s['glu'], s['qkv'] = proj_split(hb, fw['w_in'], 2 * C, tm=tm_big, name="proj")
        on_conv = RIDE_ATT if l == 0 else RIDE_FFN[1:]
        on_sb = RIDE_FFN if l == 0 else RIDE_FFN[:1]
        s['u1'], got = conv_fwd(s['glu'], cw_full[l], conv_b[l][None], name="conv_fwd",
                                rider=GatherRider(shards_of(l, on_conv)))
        fw.update(gathered(on_conv, got))
        more = l + 1 < L
        s['o_sb'], s['ltot'], got = sb_fwd(
            s['qkv'], q_col=0, name="sb_fwd", rider=GatherRider(shards_of(l, on_sb)))
        fw.update(gathered(on_sb, got))
        s['ua'] = ln_silu(s['u1'], s['o_sb'], conv_ln_g[l][None], conv_ln_b[l][None], tm=tm, name="ln_silu")
        s['x1'], s['x1b'], s['zh1'], s['rs1'] = mm_ln(
            s['ua'], fw['w_out'], h, ln1_g[l][None], ln1_b[l][None], alpha, tm=tm, name="out_proj_ln")
        s['q2'] = mm_nn(s['x1b'], fw['mem_wq'], BF16, tm=min(1024, S), tn=512, name="mem_q")
        s['k2'] = mm_nn(mem_bf, fw['mem_wk'], BF16, tm=mem_bf.shape[0], tn=512, name="mem_kv")
        s['v2'] = mm_nn(mem_bf, fw['mem_wv'], BF16, tm=mem_bf.shape[0], tn=512, name="mem_kv")
        s['o2'] = xattn_fwd(s['q2'], s['k2'], s['v2'], tm=tm, name="xattn_fwd")
        s['x2'], s['x2b'], s['zh2'], s['rs2'] = mm_ln(
            s['o2'], fw['mem_wo'], s['x1'], ln2_g[l][None], ln2_b[l][None], alpha, tm=tm, name="mem_o_ln")
        (s['upv'], s['upg'], s['mv'], s['mg'], s['hmid']), got = ffn_up_fwd(
            s['x2b'], fw['ffn_up'], fcw_full[l], ffn_conv_b[l][None], tm=tm_ffn, tn=fw['ffn_up'].shape[2],
            name="ffn_up_fwd", rider=GatherRider(shards_of(l + 1, RIDE_ATT + RIDE_IN)) if more else None)
        if more:
            full[l + 1].update(gathered(RIDE_ATT + RIDE_IN, got))
        h, hb, s['zh3'], s['rs3'] = mm_ln(
            s['hmid'], fw['ffn_down'], s['x2'], ln3_g[l][None], ln3_b[l][None], alpha, tm=tm, name="ffn_down_ln")
        saved.append(s)

    dx, loss_part = loss_head(h, tgt, tm=tm, name="loss_head")
    loss = lax.psum(loss_part[0, 0], ("x", "y", "c"))

    core = lax.axis_index("c").astype(jnp.int32).reshape(1)
    reduced_big = {n: lax.empty((L, 2, W[n].shape[1] // 2, W[n].shape[2]), F32) for n in BIG}
    small_grads = [None] * L

    def row_halves(g, names):
        parts = []
        for n in names:
            rows, cols = W[n].shape[1], W[n].shape[2]
            parts.append(g[n].reshape(N_CHIPS, 2, rows // 2, cols))
        return parts

    def pre_add(g, names):
        parts = row_halves(g, names)
        got = run_rider(SwapRider(parts), name="rs_sibling_swap")
        return list(add_pairs(parts, got, core, name="rs_add_pairs"))

    def reduce_into(names, scattered, layer):
        reduced_big.update(zip(names, sum_chips_into(
            list(scattered), [reduced_big[n] for n in names], layer, core, name="rs_sum_chips")))

    pending = None
    for l in reversed(range(L)):
        fw, s = full[l], saved[l]
        g = {}
        if l == L - 1:
            top = ln_bwd(dx, s['zh3'], s['rs3'], ln3_g[l][None], tm=tm, name="ln_bwd")
        dz3, dz3b, g['ln3_g'], g['ln3_b'] = top
        ftn = fw['ffn_up'].shape[2]
        (dupv, dupg, dfw_v, dfw_g, dfb_v, dfb_g), sc = ffn_mid_bwd(
            dz3b, fw['ffn_down'], s['upv'], s['upg'], s['mv'], s['mg'], fcw_full[l], tm=tm_ffn, tn=ftn,
            name="ffn_mid_bwd", rider=ScatterRider(pending) if pending else None)
        if pending:
            reduce_into(RIDE_MIX, sc, l + 1)
        g['ffn_conv_w'] = jnp.concatenate([dfw_v, dfw_g], axis=1)
        g['ffn_conv_b'] = jnp.concatenate([dfb_v, dfb_g], axis=1)[0]
        g['ffn_down'] = mm_tn(s['hmid'], [dz3b], tk=ftn, tn=D, tmc=min(1024, S), name="grad_ffn_down")
        dz2, dz2b, g['ln2_g'], g['ln2_b'] = mm_nt_ln_bwd(
            [dupv, dupg], fw['ffn_up'], dz3, alpha, s['zh2'], s['rs2'], ln2_g[l][None], tm=tm_ffn,
            name="ffn_up_bwd")
        g['ffn_up'] = mm_tn(s['x2b'], [dupv, dupg], tk=D, tn=ftn, shard_width=ftn, tmc=min(1024, S),
                            name="grad_ffn_up")

        do2 = mm_nt([dz2b], fw['mem_wo'], BF16, tm=tm_big, tk=512, name="mem_o_bwd")
        g['mem_wo'] = mm_tn(s['o2'], [dz2b], tk=D, tn=D, tmc=tm_half, name="grad_sq")
        dq2, dk2, dv2 = xattn_bwd(s['q2'], do2, s['k2'], s['v2'], tm=tm, name="xattn_bwd")
        dz1, dz1b, g['ln1_g'], g['ln1_b'] = mm_nt_ln_bwd(
            [dq2], fw['mem_wq'], dz2, alpha, s['zh1'], s['rs1'], ln1_g[l][None], tm=tm, name="mem_q_bwd")
        g['mem_wq'] = mm_tn(s['x1b'], [dq2], tk=D, tn=D, tmc=tm_half, name="grad_sq")
        g['mem_wk'] = mm_tn(mem_bf, [dk2], tk=D, tn=D, name="grad_mem_kv")
        g['mem_wv'] = mm_tn(mem_bf, [dv2], tk=D, tn=D, name="grad_mem_kv")

        rest = row_halves(g, RIDE_REST)
        dua, got = mm_nt([dz1b], fw['w_out'], F32, tm=tm_big, tk=512, name="out_proj_bwd", rider=SwapRider(rest))
        rest = list(add_pairs(rest, got, core, name="rs_add_pairs"))
        g['w_out'] = mm_tn(s['ua'], [dz1b], tk=D, tn=D, tmc=tm_half, name="grad_sq")
        dq, dk, dv, sc = sb_bwd(
            s['qkv'], s['ltot'], dua, q_col=0, do_col=C, name="sb_bwd",
            rider=ScatterRider(rest[:-1]))
        reduce_into(RIDE_REST[:-1], sc, l)
        du1, g['conv_ln_g'], g['conv_ln_b'] = ln_silu_bwd(
            dua, s['u1'], conv_ln_g[l][None], conv_ln_b[l][None], tm=tm, name="ln_silu_bwd")
        (da, dg, g['conv_w'], dcb), sc = conv_bwd(du1, s['glu'], cw_full[l], name="conv_bwd",
                                                  rider=ScatterRider(rest[-1:]))
        reduce_into(RIDE_REST[-1:], sc, l)
        g['conv_b'] = dcb
        dproj = jnp.concatenate([da, dg, dq, dk, dv], axis=1)
        ns_in = fw['w_in'].shape[2]
        if l > 0:
            below = saved[l - 1]
            top = mm_nt_ln_bwd([dproj], fw['w_in'], dz1, alpha, below['zh3'], below['rs3'], ln3_g[l - 1][None],
                               tm=tm, name="proj_bwd")
        else:
            dx = mm_nt([dproj], fw['w_in'], F32, tm=tm_big, tk=512, res=dz1, alpha=alpha, name="proj_bwd_x")
        g['w_in'] = mm_tn(s['xb'], [dproj], tk=D, tn=ns_in, shard_width=ns_in, tmc=tm_half, name="grad_w_in")

        pending = pre_add(g, RIDE_MIX)
        small_grads[l] = {n: g[n].reshape(W[n].shape[1:-1] + (-1,)) for n in SMALL}

    grad_x = dx[None]

    reduce_into(RIDE_MIX, run_rider(ScatterRider(pending), name="rs_chip_scatter"), 0)
    shared = rs_sibling_share([reduced_big[n] for n in BIG], name="rs_sibling_share")
    G = {}
    for n, sh in zip(BIG, shared):
        G[n] = sh.reshape(W[n].shape)

    small_full_shapes = []
    small_stack = []
    for n in SMALL:
        st = jnp.stack([small_grads[l][n] for l in range(L)])
        small_stack.append(st)
        small_full_shapes.append(st.shape)
    reduced = _unpack(allreduce_small(_pack(small_stack), name="allreduce_small"), small_full_shapes)
    for n, r in zip(SMALL, reduced):
        if n in SMALL_SHARDED:
            width = W[n].shape[-1]
            r = lax.dynamic_slice_in_dim(r, chip * width, width, axis=2)
        G[n] = r

    out_g, out_d, out_m, out_v = {}, {}, {}, {}
    for n in BIG:
        shp = W[n].shape
        flat = lambda a: a.reshape(shp[0] * shp[1], shp[2])
        res = adamw(flat(W[n]), flat(G[n]), flat(M1[n]), flat(V2[n]), tr=_adamw_tile(shp[0] * shp[1], shp[2]), name="adamw")
        out_g[n], out_d[n], out_m[n], out_v[n] = [r.reshape(shp) for r in res]
    small_shapes = [W[n].shape for n in SMALL]
    packed = [_pack([d[n] for n in SMALL]) for d in (W, G, M1, V2)]
    res = adamw(*packed, tr=packed[0].shape[0], name="adamw_small")
    for d, r in zip((out_g, out_d, out_m, out_v), res):
        for n, a in zip(SMALL, _unpack(r, small_shapes)):
            d[n] = a

    return (loss, grad_x, *[out_g[n] for n in WEIGHTS], *[out_d[n] for n in WEIGHTS],
            *[out_m[n] for n in WEIGHTS], *[out_v[n] for n in WEIGHTS])
```

```python
import functools

import jax
import jax.numpy as jnp
from jax import lax
from jax.experimental import pallas as pl
from jax.experimental.pallas import tpu as pltpu

F32 = jnp.float32
BF16 = jnp.bfloat16
MESH = pl.DeviceIdType.MESH

LN_EPS = 1e-5
SB_HEADS = 8
MEM_HEADS = 4
ADAM_LR, ADAM_B1, ADAM_B2, ADAM_EPS, ADAM_WD, ADAM_STEP = 0.001, 0.9, 0.999, 1e-08, 0.01, 10

LANES = 128
V7X_VMEM_BYTES = 64 << 20
VMEM_CAP = V7X_VMEM_BYTES - (6 << 20)
N_CHIPS = 4
N_DEV = 8

BIG = ('w_in', 'w_out', 'mem_wq', 'mem_wk', 'mem_wv', 'mem_wo', 'ffn_up', 'ffn_down')
RIDE_IN = ('w_in',)
RIDE_ATT = ('w_out', 'mem_wq', 'mem_wk', 'mem_wv', 'mem_wo')
RIDE_FFN = ('ffn_up', 'ffn_down')
RIDE_MIX = ('w_in',)
RIDE_REST = ('w_out', 'mem_wq', 'mem_wk', 'mem_wv', 'mem_wo', 'ffn_up', 'ffn_down')
COL_SHARDED = ('w_in', 'ffn_up')
SMALL = ('conv_w', 'conv_b', 'conv_ln_g', 'conv_ln_b', 'ln1_g', 'ln1_b', 'ln2_g', 'ln2_b',
         'ffn_conv_w', 'ffn_conv_b', 'ln3_g', 'ln3_b')
SMALL_SHARDED = ('conv_w', 'ffn_conv_w')
WEIGHTS = ('w_in', 'conv_w', 'conv_b', 'conv_ln_g', 'conv_ln_b', 'w_out', 'ln1_g', 'ln1_b',
           'mem_wq', 'mem_wk', 'mem_wv', 'mem_wo', 'ln2_g', 'ln2_b', 'ffn_up', 'ffn_conv_w',
           'ffn_conv_b', 'ffn_down', 'ln3_g', 'ln3_b')


def _params(block_bytes, semantics=None, **kw):
    limit = int(min(max(2 * block_bytes + (8 << 20), 32 << 20), VMEM_CAP))
    return pltpu.CompilerParams(dimension_semantics=semantics, vmem_limit_bytes=limit, **kw)


def _pallas(body, **kw):
    call = pl.pallas_call(body, **kw)

    def run(*args):
        return call(*[pltpu.with_memory_space_constraint(a, pltpu.HBM)
                      if jnp.issubdtype(a.dtype, jnp.floating) else a for a in args])

    return run


def _nbytes(shape, dtype):
    n = 1
    for s in shape:
        n *= s
    return n * jnp.dtype(dtype).itemsize


def _dot(a, b):
    return jnp.dot(a, b, preferred_element_type=F32)


def _dot_nt(a, b):
    return lax.dot_general(a, b, (((1,), (1,)), ((), ())), preferred_element_type=F32)


def _dot_tn(a, b):
    return lax.dot_general(a, b, (((0,), (0,)), ((), ())), preferred_element_type=F32)


def _sigmoid(x):
    return 1.0 / (1.0 + jnp.exp(-x))


def mm_nn(a, b, out_dtype, *, tm, tn, name):
    M, K = a.shape
    sharded = b.ndim == 3
    if sharded:
        nsh, _, ns = b.shape
        N, per = nsh * ns, ns // tn
        b_spec = pl.BlockSpec((None, K, tn), lambda i, j: (j // per, 0, j % per))
    else:
        N = b.shape[1]
        b_spec = pl.BlockSpec((K, tn), lambda i, j: (0, j))

    def body(a_ref, b_ref, o_ref):
        o_ref[...] = _dot(a_ref[...].astype(BF16), b_ref[...]).astype(o_ref.dtype)

    blk = _nbytes((tm, K), a.dtype) + _nbytes((K, tn), BF16) + _nbytes((tm, tn), out_dtype)
    return _pallas(
        body, name=name, out_shape=pltpu.HBM((M, N), out_dtype), grid=(M // tm, N // tn),
        in_specs=[pl.BlockSpec((tm, K), lambda i, j: (i, 0)), b_spec],
        out_specs=pl.BlockSpec((tm, tn), lambda i, j: (i, j)),
        compiler_params=_params(blk, ("parallel", "parallel")))(a, b)


def proj_split(a, b, n_f32, *, tm, name):
    M, K = a.shape
    nsh, _, ns = b.shape
    N = nsh * ns

    def body(a_ref, b_ref, lo_ref, hi_ref):
        acc = _dot(a_ref[...], b_ref[...])
        j = pl.program_id(1)
        for s in range(nsh):
            c0, c1 = s * ns, (s + 1) * ns
            cut = min(max(n_f32 - c0, 0), ns)

            @pl.when(j == s)
            def _(c0=c0, c1=c1, cut=cut):
                if cut > 0:
                    lo_ref[:, c0:c0 + cut] = acc[:, 0:cut]
                if cut < ns:
                    hi_ref[:, c0 + cut - n_f32:c1 - n_f32] = acc[:, cut:ns].astype(BF16)

    blk = _nbytes((tm, K), BF16) + _nbytes((K, ns), BF16) + _nbytes((tm, N), F32)
    return _pallas(
        body, name=name, grid=(M // tm, nsh),
        out_shape=(pltpu.HBM((M, n_f32), F32), pltpu.HBM((M, N - n_f32), BF16)),
        in_specs=[pl.BlockSpec((tm, K), lambda i, j: (i, 0)), pl.BlockSpec((None, K, ns), lambda i, j: (j, 0, 0))],
        out_specs=(pl.BlockSpec((tm, n_f32), lambda i, j: (i, 0)), pl.BlockSpec((tm, N - n_f32), lambda i, j: (i, 0))),
        compiler_params=_params(blk, ("parallel", "arbitrary")))(a, b)


def mm_ln(a, b, x, gamma, beta, alpha, *, tm, name):
    M, K = a.shape
    D = b.shape[1]

    def body(a_ref, b_ref, x_ref, g_ref, be_ref, y_ref, yb_ref, zh_ref, rs_ref):
        z = alpha * x_ref[...] + _dot(a_ref[...], b_ref[...])
        mu = jnp.mean(z, axis=-1, keepdims=True)
        zc = z - mu
        rstd = lax.rsqrt(jnp.mean(zc * zc, axis=-1, keepdims=True) + LN_EPS)
        zh = zc * rstd
        y = zh * g_ref[...] + be_ref[...]
        y_ref[...] = y
        yb_ref[...] = y.astype(BF16)
        zh_ref[...] = zh
        rs_ref[...] = rstd

    row = lambda i: (i, 0)
    fix = lambda i: (0, 0)
    blk = _nbytes((tm, K), BF16) + _nbytes((K, D), BF16) + 4 * _nbytes((tm, D), F32)
    return _pallas(
        body, name=name, grid=(M // tm,),
        out_shape=(pltpu.HBM((M, D), F32), pltpu.HBM((M, D), BF16),
                   pltpu.HBM((M, D), F32), pltpu.HBM((M, 1), F32)),
        in_specs=[pl.BlockSpec((tm, K), row), pl.BlockSpec((K, D), fix), pl.BlockSpec((tm, D), row),
                  pl.BlockSpec((1, D), fix), pl.BlockSpec((1, D), fix)],
        out_specs=(pl.BlockSpec((tm, D), row), pl.BlockSpec((tm, D), row), pl.BlockSpec((tm, D), row),
                   pl.BlockSpec((tm, 1), row)),
        compiler_params=_params(blk, ("parallel",)))(a, b, x, gamma, beta)


def ln_bwd(dy, zh, rstd, gamma, *, tm, name):
    M, D = dy.shape

    def body(dy_ref, zh_ref, rs_ref, g_ref, dz_ref, dzb_ref, dg_ref, db_ref):
        @pl.when(pl.program_id(0) == 0)
        def _():
            dg_ref[...] = jnp.zeros_like(dg_ref)
            db_ref[...] = jnp.zeros_like(db_ref)

        dyv, zhv = dy_ref[...], zh_ref[...]
        dg_ref[...] += jnp.sum(dyv * zhv, axis=0, keepdims=True)
        db_ref[...] += jnp.sum(dyv, axis=0, keepdims=True)
        dzh = dyv * g_ref[...]
        m1 = jnp.mean(dzh, axis=-1, keepdims=True)
        m2 = jnp.mean(dzh * zhv, axis=-1, keepdims=True)
        dz = rs_ref[...] * (dzh - m1 - zhv * m2)
        dz_ref[...] = dz
        dzb_ref[...] = dz.astype(BF16)

    row = lambda i: (i, 0)
    fix = lambda i: (0, 0)
    return _pallas(
        body, name=name, grid=(M // tm,),
        out_shape=(pltpu.HBM((M, D), F32), pltpu.HBM((M, D), BF16),
                   pltpu.HBM((1, D), F32), pltpu.HBM((1, D), F32)),
        in_specs=[pl.BlockSpec((tm, D), row), pl.BlockSpec((tm, D), row), pl.BlockSpec((tm, 1), row),
                  pl.BlockSpec((1, D), fix)],
        out_specs=(pl.BlockSpec((tm, D), row), pl.BlockSpec((tm, D), row), pl.BlockSpec((1, D), fix),
                   pl.BlockSpec((1, D), fix)),
        compiler_params=_params(4 * _nbytes((tm, D), F32), ("arbitrary",)))(dy, zh, rstd, gamma)


def mm_nt(a_list, b, out_dtype, *, tm, tk, name, res=None, alpha=None, rider=None):
    M = a_list[0].shape[0]
    widths = [a.shape[1] for a in a_list]
    sharded = b.ndim == 3
    if sharded:
        nsh, K, ns = b.shape
        b_spec = pl.BlockSpec((nsh, tk, ns), lambda i, j: (0, j, 0))
        for w in widths:
            assert w % ns == 0
    else:
        K, N = b.shape
        ns = None
        b_spec = pl.BlockSpec((tk, N), lambda i, j: (j, 0))
    n_a = len(a_list)

    def body(*refs):
        a_refs, b_ref = refs[:n_a], refs[n_a]
        o_ref = refs[-1]
        acc = None
        off = 0
        for a_ref, w in zip(a_refs, widths):
            if sharded:
                for p in range(w // ns):
                    t = _dot_nt(a_ref[:, p * ns:(p + 1) * ns].astype(BF16), b_ref[off // ns + p])
                    acc = t if acc is None else acc + t
            else:
                t = _dot_nt(a_ref[...].astype(BF16), b_ref[:, off:off + w])
                acc = t if acc is None else acc + t
            off += w
        if res is not None:
            acc = acc + alpha * refs[n_a + 1][...]
        o_ref[...] = acc.astype(o_ref.dtype)

    in_specs = [pl.BlockSpec((tm, w), lambda i, j: (i, 0)) for w in widths] + [b_spec]
    args = list(a_list) + [b]
    if res is not None:
        in_specs.append(pl.BlockSpec((tm, tk), lambda i, j: (i, j)))
        args.append(res)
    blk = (sum(_nbytes((tm, w), a.dtype) for a, w in zip(a_list, widths)) + _nbytes((tk, sum(widths)), BF16)
           + 2 * _nbytes((tm, tk), F32))
    in_specs, out_specs, out_shape, scratch = _carry_specs(
        rider, in_specs, (pl.BlockSpec((tm, tk), lambda i, j: (i, j)),), (pltpu.HBM((M, K), out_dtype),), [])
    first = lambda: (pl.program_id(0) == 0) & (pl.program_id(1) == 0)
    last = lambda: (pl.program_id(0) == M // tm - 1) & (pl.program_id(1) == K // tk - 1)
    res_all = _pallas(
        _carry(rider, body, len(args), 1, first, last), name=name, out_shape=out_shape, grid=(M // tm, K // tk),
        in_specs=in_specs, out_specs=out_specs, scratch_shapes=scratch,
        compiler_params=_params(blk, ("arbitrary", "arbitrary")))(*args, *(rider.arrays if rider else ()))
    return res_all[0] if rider is None else (res_all[0], list(res_all[1:]))


def mm_nt_ln_bwd(a_list, b, res, alpha, zh, rstd, gamma, *, tm, name):
    M, D = res.shape
    widths = [a.shape[1] for a in a_list]
    sharded = b.ndim == 3
    if sharded:
        nsh, _, ns = b.shape
        b_spec = pl.BlockSpec((nsh, D, ns), lambda i: (0, 0, 0))
    else:
        ns = None
        b_spec = pl.BlockSpec((D, b.shape[1]), lambda i: (0, 0))
    n_a = len(a_list)

    def body(*refs):
        a_refs, b_ref = refs[:n_a], refs[n_a]
        res_ref, zh_ref, rs_ref, g_ref = refs[n_a + 1:n_a + 5]
        dz_ref, dzb_ref, dg_ref, db_ref = refs[n_a + 5:]

        @pl.when(pl.program_id(0) == 0)
        def _():
            dg_ref[...] = jnp.zeros_like(dg_ref)
            db_ref[...] = jnp.zeros_like(db_ref)

        dy = alpha * res_ref[...]
        off = 0
        for a_ref, w in zip(a_refs, widths):
            if sharded:
                for p in range(w // ns):
                    dy = dy + _dot_nt(a_ref[:, p * ns:(p + 1) * ns], b_ref[off // ns + p])
            else:
                dy = dy + _dot_nt(a_ref[...], b_ref[:, off:off + w])
            off += w
        zhv = zh_ref[...]
        dg_ref[...] += jnp.sum(dy * zhv, axis=0, keepdims=True)
        db_ref[...] += jnp.sum(dy, axis=0, keepdims=True)
        dzh = dy * g_ref[...]
        m1 = jnp.mean(dzh, axis=-1, keepdims=True)
        m2 = jnp.mean(dzh * zhv, axis=-1, keepdims=True)
        dz = rs_ref[...] * (dzh - m1 - zhv * m2)
        dz_ref[...] = dz
        dzb_ref[...] = dz.astype(BF16)

    row = lambda i: (i, 0)
    fix = lambda i: (0, 0)
    in_specs = [pl.BlockSpec((tm, w), row) for w in widths] + [
        b_spec, pl.BlockSpec((tm, D), row), pl.BlockSpec((tm, D), row), pl.BlockSpec((tm, 1), row),
        pl.BlockSpec((1, D), fix)]
    blk = (sum(_nbytes((tm, w), BF16) for w in widths) + _nbytes((D, sum(widths)), BF16)
           + 5 * _nbytes((tm, D), F32))
    return _pallas(
        body, name=name, grid=(M // tm,),
        out_shape=(pltpu.HBM((M, D), F32), pltpu.HBM((M, D), BF16), pltpu.HBM((1, D), F32),
                   pltpu.HBM((1, D), F32)),
        in_specs=in_specs,
        out_specs=(pl.BlockSpec((tm, D), row), pl.BlockSpec((tm, D), row), pl.BlockSpec((1, D), fix),
                   pl.BlockSpec((1, D), fix)),
        compiler_params=_params(blk, ("arbitrary",)))(*a_list, b, res, zh, rstd, gamma)


def mm_tn(a, b_list, *, tk, tn, name, shard_width=None, tmc=None):
    M, K = a.shape
    tmc = M if tmc is None else tmc
    nm = M // tmc
    widths = [b.shape[1] for b in b_list]
    N = sum(widths)
    starts, s = [], 0
    for w in widths:
        assert w % tn == 0
        starts.append(s)
        s += w // tn
    n_b = len(b_list)

    def body(*refs):
        a_ref, b_refs, o_ref, acc = refs[0], refs[1:1 + n_b], refs[-2], refs[-1]
        j, m = pl.program_id(1), pl.program_id(2)
        for b_ref, st, w in zip(b_refs, starts, widths):
            @pl.when((j >= st) & (j < st + w // tn))
            def _(b_ref=b_ref):
                t = _dot_tn(a_ref[...].astype(BF16), b_ref[...].astype(BF16))
                if nm == 1:
                    o_ref[...] = t.astype(o_ref.dtype)
                else:
                    @pl.when(m == 0)
                    def _():
                        acc[...] = t

                    @pl.when(m > 0)
                    def _():
                        acc[...] += t

                    @pl.when(m == nm - 1)
                    def _():
                        o_ref[...] = acc[...].astype(o_ref.dtype)

    def b_map(st, w):
        nb = w // tn
        return lambda i, j, m: (jnp.where((j >= st) & (j < st + nb), m, 0), jnp.clip(j - st, 0, nb - 1))

    in_specs = [pl.BlockSpec((tmc, tk), lambda i, j, m: (m, i))]
    in_specs += [pl.BlockSpec((tmc, tn), b_map(st, w)) for st, w in zip(starts, widths)]
    if shard_width is None:
        out_shape = pltpu.HBM((K, N), BF16)
        out_spec = pl.BlockSpec((tk, tn), lambda i, j, m: (i, j))
    else:
        per = shard_width // tn
        out_shape = pltpu.HBM((N // shard_width, K, shard_width), BF16)
        out_spec = pl.BlockSpec((None, tk, tn), lambda i, j, m: (j // per, i, j % per))
    acc_shape = (tk, tn) if nm > 1 else (8, LANES)
    blk = (_nbytes((tmc, tk), a.dtype) + n_b * _nbytes((tmc, tn), b_list[0].dtype) + 2 * _nbytes((tk, tn), F32))
    return _pallas(
        body, name=name, out_shape=out_shape, grid=(K // tk, N // tn, nm), in_specs=in_specs, out_specs=out_spec,
        scratch_shapes=[pltpu.VMEM(acc_shape, F32)],
        compiler_params=_params(blk, ("parallel", "arbitrary", "arbitrary")))(a, *b_list)


CONV_PAD = 32
CONV_CHUNK = 128


def _rows(win, off, n, shifts):
    b, a = off % 8, off // 8
    if b not in shifts:
        shifts[b] = win if b == 0 else win[b:b + n + CONV_PAD - 8, :]
    return shifts[b][8 * a:8 * a + n, :]


def _by_residue(n_taps, offset):
    return sorted(range(n_taps), key=lambda k: (offset(k) % 8, k))


def conv_fwd(proj, conv_w, conv_b, *, name, rider=None):
    S = proj.shape[0]
    KW, C = conv_w.shape
    nct = C // LANES
    rc = min(CONV_CHUNK, S)

    def body(a_ref, g_ref, w_ref, b_ref, o_ref, pad):
        pad[0:CONV_PAD, :] = jnp.zeros((CONV_PAD, LANES), F32)
        pad[CONV_PAD:, :] = a_ref[...] * _sigmoid(g_ref[...])
        w = w_ref[...]
        bias = b_ref[...]

        def chunk(i, _):
            base = pl.multiple_of(i * rc, rc)
            win = pad[pl.ds(base, rc + CONV_PAD), :]
            acc = jnp.zeros((rc, LANES), F32) + bias
            shifts = {}
            for k in _by_residue(KW, lambda k: CONV_PAD - (KW - 1) + k):
                acc = acc + w[k:k + 1, :] * _rows(win, CONV_PAD - (KW - 1) + k, rc, shifts)
            o_ref[pl.ds(base, rc), :] = acc
            return 0

        lax.fori_loop(0, S // rc, chunk, 0)

    in_specs, out_specs, out_shape, scratch = _carry_specs(
        rider, [pl.BlockSpec((S, LANES), lambda c: (0, c)), pl.BlockSpec((S, LANES), lambda c: (0, c + nct)),
                pl.BlockSpec((KW, LANES), lambda c: (0, c)), pl.BlockSpec((1, LANES), lambda c: (0, c))],
        (pl.BlockSpec((S, LANES), lambda c: (0, c)),), (pltpu.HBM((S, C), F32),),
        [pltpu.VMEM((S + CONV_PAD, LANES), F32)])
    first = lambda: pl.program_id(0) == 0
    last = lambda: pl.program_id(0) == nct - 1
    res = _pallas(
        _carry(rider, body, 4, 1, first, last), name=name, grid=(nct,), out_shape=out_shape,
        in_specs=in_specs, out_specs=out_specs, scratch_shapes=scratch,
        compiler_params=_params(4 * _nbytes((S, LANES), F32), ("arbitrary",)))(
            proj, proj, conv_w, conv_b, *(rider.arrays if rider else ()))
    return res[0], list(res[1:])


def conv_bwd(du1, proj, conv_w, *, name, rider=None):
    S = proj.shape[0]
    KW, C = conv_w.shape
    nct = C // LANES
    rc = min(CONV_CHUNK, S)

    def body(d_ref, a_ref, g_ref, w_ref, da_ref, dg_ref, dw_ref, db_ref, pad_u, pad_d, du0, dw_acc):
        dw_acc[...] = jnp.zeros_like(dw_acc)
        pad_u[0:CONV_PAD, :] = jnp.zeros((CONV_PAD, LANES), F32)
        pad_u[CONV_PAD:, :] = a_ref[...] * _sigmoid(g_ref[...])
        pad_d[0:S, :] = d_ref[...]
        pad_d[S:, :] = jnp.zeros((CONV_PAD, LANES), F32)
        w = w_ref[...]
        db_ref[...] = jnp.sum(d_ref[...], axis=0, keepdims=True)

        def chunk(i, _):
            base = pl.multiple_of(i * rc, rc)
            d = pad_d[pl.ds(base, rc), :]
            win_u = pad_u[pl.ds(base, rc + CONV_PAD), :]
            win_d = pad_d[pl.ds(base, rc + CONV_PAD), :]
            shifts = {}
            for k in _by_residue(KW, lambda k: CONV_PAD - (KW - 1) + k):
                u_k = _rows(win_u, CONV_PAD - (KW - 1) + k, rc, shifts)
                dw_acc[k:k + 1, :] += jnp.sum(d * u_k, axis=0, keepdims=True)
            acc = jnp.zeros((rc, LANES), F32)
            shifts = {}
            for k in _by_residue(KW, lambda k: KW - 1 - k):
                acc = acc + w[k:k + 1, :] * _rows(win_d, KW - 1 - k, rc, shifts)
            du0[pl.ds(base, rc), :] = acc
            return 0

        lax.fori_loop(0, S // rc, chunk, 0)
        dw_ref[...] = dw_acc[0:KW, :]
        a, sg = a_ref[...], _sigmoid(g_ref[...])
        d0 = du0[...]
        da_ref[...] = (d0 * sg).astype(BF16)
        dg_ref[...] = (d0 * a * sg * (1.0 - sg)).astype(BF16)

    col = lambda c: (0, c)
    in_specs, out_specs, out_shape, scratch = _carry_specs(
        rider, [pl.BlockSpec((S, LANES), col), pl.BlockSpec((S, LANES), col),
                pl.BlockSpec((S, LANES), lambda c: (0, c + nct)), pl.BlockSpec((KW, LANES), col)],
        (pl.BlockSpec((S, LANES), col), pl.BlockSpec((S, LANES), col), pl.BlockSpec((KW, LANES), col),
         pl.BlockSpec((1, LANES), col)),
        (pltpu.HBM((S, C), BF16), pltpu.HBM((S, C), BF16), pltpu.HBM((KW, C), F32), pltpu.HBM((1, C), F32)),
        [pltpu.VMEM((S + CONV_PAD, LANES), F32), pltpu.VMEM((S + CONV_PAD, LANES), F32),
         pltpu.VMEM((S, LANES), F32), pltpu.VMEM((CONV_PAD, LANES), F32)])
    first = lambda: pl.program_id(0) == 0
    last = lambda: pl.program_id(0) == nct - 1
    res = _pallas(
        _carry(rider, body, 4, 4, first, last), name=name, grid=(nct,), out_shape=out_shape,
        in_specs=in_specs, out_specs=out_specs, scratch_shapes=scratch,
        compiler_params=_params(8 * _nbytes((S, LANES), F32), ("arbitrary",)))(
            du1, proj, proj, conv_w, *(rider.arrays if rider else ()))
    return res[:4], list(res[4:])


def ln_silu(u1, o_sb, gamma, beta, *, tm, name):
    S, C = u1.shape

    def body(u_ref, o_ref, g_ref, b_ref, out_ref):
        z = u_ref[...]
        mu = jnp.mean(z, axis=-1, keepdims=True)
        zc = z - mu
        y = zc * lax.rsqrt(jnp.mean(zc * zc, axis=-1, keepdims=True) + LN_EPS) * g_ref[...] + b_ref[...]
        out_ref[:, 0:C] = (y * _sigmoid(y)).astype(BF16)
        out_ref[:, C:] = o_ref[...].astype(BF16)

    row = lambda i: (i, 0)
    fix = lambda i: (0, 0)
    return _pallas(
        body, name=name, out_shape=pltpu.HBM((S, 2 * C), BF16), grid=(S // tm,),
        in_specs=[pl.BlockSpec((tm, C), row), pl.BlockSpec((tm, C), row), pl.BlockSpec((1, C), fix),
                  pl.BlockSpec((1, C), fix)],
        out_specs=pl.BlockSpec((tm, 2 * C), row),
        compiler_params=_params(4 * _nbytes((tm, C), F32), ("parallel",)))(u1, o_sb, gamma, beta)


def ln_silu_bwd(dua, u1, gamma, beta, *, tm, name):
    S, C = u1.shape

    def body(d_ref, u_ref, g_ref, b_ref, du1_ref, dg_ref, db_ref):
        @pl.when(pl.program_id(0) == 0)
        def _():
            dg_ref[...] = jnp.zeros_like(dg_ref)
            db_ref[...] = jnp.zeros_like(db_ref)

        z = u_ref[...]
        mu = jnp.mean(z, axis=-1, keepdims=True)
        zc = z - mu
        rstd = lax.rsqrt(jnp.mean(zc * zc, axis=-1, keepdims=True) + LN_EPS)
        zh = zc * rstd
        y = zh * g_ref[...] + b_ref[...]
        sg = _sigmoid(y)
        dy = d_ref[...] * (sg * (1.0 + y * (1.0 - sg)))
        dg_ref[...] += jnp.sum(dy * zh, axis=0, keepdims=True)
        db_ref[...] += jnp.sum(dy, axis=0, keepdims=True)
        dzh = dy * g_ref[...]
        m1 = jnp.mean(dzh, axis=-1, keepdims=True)
        m2 = jnp.mean(dzh * zh, axis=-1, keepdims=True)
        du1_ref[...] = rstd * (dzh - m1 - zh * m2)

    row = lambda i: (i, 0)
    fix = lambda i: (0, 0)
    return _pallas(
        body, name=name, grid=(S // tm,),
        out_shape=(pltpu.HBM((S, C), F32), pltpu.HBM((1, C), F32),
                   pltpu.HBM((1, C), F32)),
        in_specs=[pl.BlockSpec((tm, C), row), pl.BlockSpec((tm, C), row), pl.BlockSpec((1, C), fix),
                  pl.BlockSpec((1, C), fix)],
        out_specs=(pl.BlockSpec((tm, C), row), pl.BlockSpec((1, C), fix), pl.BlockSpec((1, C), fix)),
        compiler_params=_params(4 * _nbytes((tm, C), F32), ("arbitrary",)))(dua, u1, gamma, beta)


SB_BLOCK = 256
SB_STOP = -105.0
SB_GROUP = 4


def _split_dot(x, tri):
    hi = x.astype(BF16)
    lo = (x - hi.astype(F32)).astype(BF16)
    return _dot(hi, tri) + _dot(lo, tri)


def _neg_softplus(z):
    return -(jnp.maximum(z, 0.0) + jnp.log(1.0 + jnp.exp(-jnp.abs(z))))


def sb_fwd(proj, *, q_col, name, rider=None):
    S = proj.shape[0]
    dh = LANES // 2
    W = SB_HEADS * dh
    BW = SB_GROUP * dh
    ngrp = W // BW
    T = min(SB_BLOCK, S)
    nblk = S // T
    scale = dh ** -0.5
    qb0 = q_col // BW
    heads = range(SB_GROUP)
    sl = [slice(h * dh, (h + 1) * dh) for h in heads]

    def body(q_ref, k_ref, v_ref, o_ref, l_ref, qs):
        r_i = lax.broadcasted_iota(jnp.int32, (T, T), 0)
        c_i = lax.broadcasted_iota(jnp.int32, (T, T), 1)
        tri = (r_i >= c_i).astype(BF16)
        vis = c_i < r_i
        lane = lax.broadcasted_iota(jnp.int32, (T, dh), 1)

        qs[...] = (q_ref[...] * scale).astype(BF16)

        def step(qb, blocks, st):
            nb = range(len(blocks))
            kb = [[k_ref[pl.ds(j0, T), sl[h]].astype(BF16) for h in heads] for j0, _ in blocks]
            vb = [[v_ref[pl.ds(j0, T), sl[h]].astype(BF16) for h in heads] for j0, _ in blocks]
            z = [[_dot_nt(qb[h], kb[b][h]) for h in heads] for b in nb]
            lk = [[_neg_softplus(z[b][h]) for h in heads] for b in nb]
            lk = [[jnp.where(vis, lk[b][h], 0.0) if blocks[b][1] else lk[b][h] for h in heads] for b in nb]
            C = [[_split_dot(lk[b][h], tri) for h in heads] for b in nb]
            R = [[st[2 * h + 1] for h in heads]]
            for b in nb:
                R.append([R[b][h] + C[b][h][:, 0:1] for h in heads])
            A = [[jnp.exp(z[b][h] + C[b][h] + R[b][h]) for h in heads] for b in nb]
            A = [[jnp.where(vis, A[b][h], 0.0) if blocks[b][1] else A[b][h] for h in heads] for b in nb]
            out = ()
            for h in heads:
                acc = st[2 * h]
                for b in nb:
                    acc = acc + _dot(A[b][h].astype(BF16), vb[b][h])
                out += (acc, R[-1][h])
            return out

        zero = (jnp.zeros((T, dh), F32), jnp.zeros((T, 1), F32))

        def finish(r0, i, c):
            walked = jnp.asarray(i - c[0]).astype(F32)
            for h in heads:
                o_ref[pl.ds(r0, T), sl[h]] = c[1 + 2 * h]
                l_ref[pl.ds(r0, T), sl[h]] = jnp.where(lane == 1, walked, c[2 + 2 * h])

        finish(0, 0, (-1,) + step([qs[0:T, sl[h]] for h in heads], [(0, True)], zero * SB_GROUP))

        def qblock(i, _):
            r0 = pl.multiple_of(i * T, T)
            qb = [qs[pl.ds(r0, T), sl[h]] for h in heads]
            state = step(qb, [(r0, True), (pl.multiple_of(r0 - T, T), False)], zero * SB_GROUP)

            def more(c):
                worst = c[2]
                for h in heads[1:]:
                    worst = jnp.maximum(worst, c[2 + 2 * h])
                return (c[0] >= 0) & (jnp.max(worst) >= SB_STOP)

            def walk(c):
                return (c[0] - 1,) + step(qb, [(pl.multiple_of(c[0] * T, T), False)], c[1:])

            finish(r0, i, lax.while_loop(more, walk, (i - 2,) + state))
            return 0

        lax.fori_loop(1, nblk, qblock, 0)

    blk = lambda off: pl.BlockSpec((S, BW), lambda g: (0, qb0 + off * ngrp + g), pipeline_mode=pl.Buffered(1))
    out = pl.BlockSpec((S, BW), lambda g: (0, g))
    in_specs, out_specs, out_shape, scratch = _carry_specs(
        rider, [blk(0), blk(1), blk(2)], (out, out), (pltpu.HBM((S, W), F32), pltpu.HBM((S, W), F32)),
        [pltpu.VMEM((S, BW), BF16)])
    first = lambda: pl.program_id(0) == 0
    last = lambda: pl.program_id(0) == ngrp - 1
    res = _pallas(
        _carry(rider, body, 3, 2, first, last), name=name, grid=(ngrp,), out_shape=out_shape,
        in_specs=in_specs, out_specs=out_specs, scratch_shapes=scratch,
        compiler_params=_params(5 * _nbytes((S, BW), F32), ("arbitrary",)))(
            proj, proj, proj, *(rider.arrays if rider else ()))
    return res[0], res[1], list(res[2:])


def sb_bwd(proj, ltot, dua, *, q_col, do_col, name, rider=None):
    S = proj.shape[0]
    dh = LANES // 2
    W = SB_HEADS * dh
    BW = SB_GROUP * dh
    ngrp = W // BW
    T = min(SB_BLOCK, S)
    nblk = S // T
    scale = dh ** -0.5
    qb0 = q_col // BW
    db0 = do_col // BW
    heads = range(SB_GROUP)
    sl = [slice(h * dh, (h + 1) * dh) for h in heads]

    def body(q_ref, k_ref, v_ref, l_ref, do_ref, dq_ref, dk_ref, dv_ref, dks, dvs):
        r_i = lax.broadcasted_iota(jnp.int32, (T, T), 0)
        c_i = lax.broadcasted_iota(jnp.int32, (T, T), 1)
        tri_rev = (r_i >= c_i).astype(BF16)
        tri_fwd = (r_i <= c_i).astype(BF16)
        vis = c_i < r_i

        dks[...] = jnp.zeros_like(dks)
        dvs[...] = jnp.zeros_like(dvs)

        def step(qb, dob, Lt, blocks, st):
            nb = range(len(blocks))
            kb = [[k_ref[pl.ds(j0, T), sl[h]].astype(BF16) for h in heads] for j0, _ in blocks]
            vb = [[v_ref[pl.ds(j0, T), sl[h]].astype(BF16) for h in heads] for j0, _ in blocks]
            z = [[_dot_nt(qb[h], kb[b][h]) for h in heads] for b in nb]
            dA =[[_dot_nt(dob[h], vb[b][h]) for h in heads] for b in nb]
            lk = [[_neg_softplus(z[b][h]) for h in heads] for b in nb]
            beta = [[jnp.exp(z[b][h] + lk[b][h]) for h in heads] for b in nb]
            lk = [[jnp.where(vis, lk[b][h], 0.0) if blocks[b][1] else lk[b][h] for h in heads] for b in nb]
            C = [[_split_dot(lk[b][h], tri_rev) for h in heads] for b in nb]
            P = [[st[3 * h + 1] for h in heads]]
            for b in nb:
                P.append([P[b][h] + C[b][h][:, 0:1] for h in heads])
            A = [[jnp.exp(z[b][h] + C[b][h] + (Lt[h] - P[b + 1][h])) for h in heads] for b in nb]
            A = [[jnp.where(vis, A[b][h], 0.0) if blocks[b][1] else A[b][h] for h in heads] for b in nb]
            g = [[A[b][h] * dA[b][h] for h in heads] for b in nb]
            Gin = [[_split_dot(g[b][h], tri_fwd) for h in heads] for b in nb]
            Gp = [[st[3 * h + 2] for h in heads]]
            for b in nb:
                Gp.append([Gp[b][h] + Gin[b][h][:, T - 1:T] for h in heads])
            dz = [[g[b][h] - beta[b][h] * (Gp[b][h] + Gin[b][h]) for h in heads] for b in nb]
            dz = [[jnp.where(vis, dz[b][h], 0.0) if blocks[b][1] else dz[b][h] for h in heads] for b in nb]
            dzb = [[dz[b][h].astype(BF16) for h in heads] for b in nb]
            out = ()
            for h in heads:
                dq = st[3 * h]
                for b in nb:
                    j0 = blocks[b][0]
                    dvs[pl.ds(j0, T), sl[h]] += _dot_tn(A[b][h].astype(BF16), dob[h])
                    dks[pl.ds(j0, T), sl[h]] += _dot_tn(dzb[b][h], qb[h])
                    dq = dq + _dot(dzb[b][h], kb[b][h])
                out += (dq, P[-1][h], Gp[-1][h])
            return out

        zero = jnp.zeros((T, 1), F32)
        init = (jnp.zeros((T, dh), F32), zero, zero)

        def operands(r0):
            return ([(q_ref[pl.ds(r0, T), sl[h]] * scale).astype(BF16) for h in heads],
                    [do_ref[pl.ds(r0, T), sl[h]].astype(BF16) for h in heads],
                    [l_ref[pl.ds(r0, T), h * dh:h * dh + 1] for h in heads])

        def finish(r0, c):
            for h in heads:
                dq_ref[pl.ds(r0, T), sl[h]] = (c[3 * h] * scale).astype(BF16)

        finish(0, step(*operands(0), [(0, True)], init * SB_GROUP))

        def qblock(i, _):
            r0 = pl.multiple_of(i * T, T)
            qb, dob, Lt = operands(r0)
            walked = jnp.clip(jnp.max(l_ref[pl.ds(r0, 8), 1:2]).astype(jnp.int32), 2, i + 1)

            def inner(j, c):
                return step(qb, dob, Lt, [(pl.multiple_of(j * T, T), False)], c)

            c = lax.fori_loop(i + 1 - walked, i - 1, inner, init * SB_GROUP)
            finish(r0, step(qb, dob, Lt, [(pl.multiple_of(r0 - T, T), False), (r0, True)], c))
            return 0

        lax.fori_loop(1, nblk, qblock, 0)
        dk_ref[...] = dks[...].astype(BF16)
        dv_ref[...] = dvs[...].astype(BF16)

    once = pl.Buffered(1)
    blk = lambda off: pl.BlockSpec((S, BW), lambda g: (0, qb0 + off * ngrp + g), pipeline_mode=once)
    out = pl.BlockSpec((S, BW), lambda g: (0, g))
    o_shape = pltpu.HBM((S, W), BF16)
    in_specs, out_specs, out_shape, scratch = _carry_specs(
        rider, [blk(0), blk(1), blk(2), pl.BlockSpec((S, BW), lambda g: (0, g), pipeline_mode=once),
                pl.BlockSpec((S, BW), lambda g: (0, db0 + g), pipeline_mode=once)], (out, out, out),
        (o_shape, o_shape, o_shape), [pltpu.VMEM((S, BW), F32)] * 2)
    first = lambda: pl.program_id(0) == 0
    last = lambda: pl.program_id(0) == ngrp - 1
    res = _pallas(
        _carry(rider, body, 5, 3, first, last), name=name, grid=(ngrp,), out_shape=out_shape,
        in_specs=in_specs, out_specs=out_specs, scratch_shapes=scratch,
        compiler_params=_params(6 * _nbytes((S, BW), F32), ("arbitrary",)))(
            proj, proj, proj, ltot, dua, *(rider.arrays if rider else ()))
    return res[0], res[1], res[2], list(res[3:])


def xattn_fwd(q, k, v, *, tm, name):
    S, D = q.shape
    Mlen = k.shape[0]
    hd = D // MEM_HEADS
    scale = hd ** -0.5

    def body(q_ref, k_ref, v_ref, o_ref):
        for h in range(MEM_HEADS):
            sl = slice(h * hd, (h + 1) * hd)
            s = _dot_nt(q_ref[:, sl], k_ref[:, sl]) * scale
            e = jnp.exp(s - jnp.max(s, axis=-1, keepdims=True))
            p = e / jnp.sum(e, axis=-1, keepdims=True)
            o_ref[:, sl] = _dot(p.astype(BF16), v_ref[:, sl]).astype(BF16)

    row = lambda i: (i, 0)
    fix = lambda i: (0, 0)
    return _pallas(
        body, name=name, out_shape=pltpu.HBM((S, D), BF16), grid=(S // tm,),
        in_specs=[pl.BlockSpec((tm, D), row), pl.BlockSpec((Mlen, D), fix), pl.BlockSpec((Mlen, D), fix)],
        out_specs=pl.BlockSpec((tm, D), row),
        compiler_params=_params(4 * _nbytes((tm, D), F32), ("parallel",)))(q, k, v)


def xattn_bwd(q, do, k, v, *, tm, name):
    S, D = q.shape
    Mlen = k.shape[0]
    hd = D // MEM_HEADS
    scale = hd ** -0.5

    def body(q_ref, do_ref, k_ref, v_ref, dq_ref, dk_ref, dv_ref):
        @pl.when(pl.program_id(0) == 0)
        def _():
            dk_ref[...] = jnp.zeros_like(dk_ref)
            dv_ref[...] = jnp.zeros_like(dv_ref)

        for h in range(MEM_HEADS):
            sl = slice(h * hd, (h + 1) * hd)
            qh, doh, kh, vh = q_ref[:, sl], do_ref[:, sl], k_ref[:, sl], v_ref[:, sl]
            s = _dot_nt(qh, kh) * scale
            e = jnp.exp(s - jnp.max(s, axis=-1, keepdims=True))
            p = e / jnp.sum(e, axis=-1, keepdims=True)
            dp = _dot_nt(doh, vh)
            ds = (p * (dp - jnp.sum(p * dp, axis=-1, keepdims=True)) * scale).astype(BF16)
            dq_ref[:, sl] = _dot(ds, kh).astype(BF16)
            dk_ref[:, sl] += _dot_tn(ds, qh)
            dv_ref[:, sl] += _dot_tn(p.astype(BF16), doh)

    row = lambda i: (i, 0)
    fix = lambda i: (0, 0)
    return _pallas(
        body, name=name, grid=(S // tm,),
        out_shape=(pltpu.HBM((S, D), BF16), pltpu.HBM((Mlen, D), F32),
                   pltpu.HBM((Mlen, D), F32)),
        in_specs=[pl.BlockSpec((tm, D), row), pl.BlockSpec((tm, D), row), pl.BlockSpec((Mlen, D), fix),
                  pl.BlockSpec((Mlen, D), fix)],
        out_specs=(pl.BlockSpec((tm, D), row), pl.BlockSpec((Mlen, D), fix), pl.BlockSpec((Mlen, D), fix)),
        compiler_params=_params(6 * _nbytes((tm, D), F32), ("arbitrary",)))(q, do, k, v)


FFN_HALO = 8


def _conv3(ext, w, lo):
    tm = ext.shape[0] - FFN_HALO
    return (w[0:1, :] * ext[lo:lo + tm, :] + w[1:2, :] * ext[lo + 1:lo + 1 + tm, :]
            + w[2:3, :] * ext[lo + 2:lo + 2 + tm, :])


def ffn_up_fwd(xb, w_up, conv_w, conv_b, *, tm, tn, name, rider=None):
    S, D = xb.shape
    nsh, _, ns = w_up.shape
    F = nsh * ns // 2
    per = ns // tn
    ncol = F // tn
    KW = conv_w.shape[0]
    assert KW == 3

    def body(x_ref, wv_ref, wg_ref, cwv_ref, cwg_ref, cbv_ref, cbg_ref, uv_ref, ug_ref, mv_ref, mg_ref, h_ref,
             carry):
        @pl.when(pl.program_id(1) == 0)
        def _():
            carry[...] = jnp.zeros_like(carry)

        x = x_ref[...]
        uv = _dot(x, wv_ref[...])
        ug = _dot(x, wg_ref[...])
        uv_ref[...] = uv.astype(BF16)
        ug_ref[...] = ug.astype(BF16)
        lo = FFN_HALO - (KW - 1)
        cv = _conv3(jnp.concatenate([carry[0], uv], axis=0), cwv_ref[...], lo) + cbv_ref[...]
        cg = _conv3(jnp.concatenate([carry[1], ug], axis=0), cwg_ref[...], lo) + cbg_ref[...]
        carry[0] = uv[tm - FFN_HALO:, :]
        carry[1] = ug[tm - FFN_HALO:, :]
        sg = _sigmoid(cg)
        act = cg * sg
        mv_ref[...] = act.astype(BF16)
        mg_ref[...] = (cv * (sg + act * (1.0 - sg))).astype(BF16)
        h_ref[...] = (act * cv).astype(BF16)

    wspec = lambda half: pl.BlockSpec((None, D, tn), lambda j, i: (half * (nsh // 2) + j // per, 0, j % per))
    cspec = lambda rows, half: pl.BlockSpec((rows, tn), lambda j, i: (0, half * ncol + j))
    out = pl.BlockSpec((tm, tn), lambda j, i: (i, j))
    o_shape = pltpu.HBM((S, F), BF16)
    blk = _nbytes((tm, D), BF16) + 2 * _nbytes((D, tn), BF16) + 8 * _nbytes((tm, tn), F32)
    nrow = S // tm
    in_specs, out_specs, out_shape, scratch = _carry_specs(
        rider, [pl.BlockSpec((tm, D), lambda j, i: (i, 0)), wspec(0), wspec(1), cspec(KW, 0), cspec(KW, 1),
                cspec(1, 0), cspec(1, 1)], (out,) * 5, (o_shape,) * 5, [pltpu.VMEM((2, FFN_HALO, tn), F32)])
    first = lambda: (pl.program_id(0) == 0) & (pl.program_id(1) == 0)
    last = lambda: (pl.program_id(0) == ncol - 1) & (pl.program_id(1) == nrow - 1)
    res = _pallas(
        _carry(rider, body, 7, 5, first, last), name=name, grid=(ncol, nrow), out_shape=out_shape,
        in_specs=in_specs, out_specs=out_specs, scratch_shapes=scratch,
        compiler_params=_params(blk, ("arbitrary", "arbitrary")))(
            xb, w_up, w_up, conv_w, conv_w, conv_b, conv_b, *(rider.arrays if rider else ()))
    return res[:5], list(res[5:])


def ffn_mid_bwd(dzb, w_down, up_v, up_g, mult_v, mult_g, conv_w, *, tm, tn, name, rider=None):
    S, D = dzb.shape
    F = up_v.shape[1]
    ncol = F // tn
    nrow = S // tm
    KW = conv_w.shape[0]
    assert KW == 3

    def body(dz_ref, wd_ref, uv_ref, ug_ref, mv_ref, mg_ref, cwv_ref, cwg_ref,
             dv_ref, dg_ref, dwv_ref, dwg_ref, dbv_ref, dbg_ref, carry):
        @pl.when(pl.program_id(1) == 0)
        def _():
            carry[...] = jnp.zeros_like(carry)
            for r in (dwv_ref, dwg_ref, dbv_ref, dbg_ref):
                r[...] = jnp.zeros_like(r)

        dh = _dot_nt(dz_ref[...], wd_ref[...])
        dcv = dh * mv_ref[...].astype(F32)
        dcg = dh * mg_ref[...].astype(F32)

        def back(dc, u_ref, cw, slot, du_ref, dw_ref, db_ref):
            ext = jnp.concatenate([dc, carry[slot]], axis=0)
            ahead = [dc, ext[1:tm + 1, :], ext[2:tm + 2, :]]
            du = cw[2:3, :] * ahead[0] + cw[1:2, :] * ahead[1] + cw[0:1, :] * ahead[2]
            du_ref[...] = du.astype(BF16)
            carry[slot] = dc[0:FFN_HALO, :]
            u = u_ref[...].astype(F32)
            for k in range(KW):
                dw_ref[k:k + 1, :] += jnp.sum(ahead[KW - 1 - k] * u, axis=0, keepdims=True)
            db_ref[...] += jnp.sum(dc, axis=0, keepdims=True)

        back(dcv, uv_ref, cwv_ref[...], 0, dv_ref, dwv_ref, dbv_ref)
        back(dcg, ug_ref, cwg_ref[...], 1, dg_ref, dwg_ref, dbg_ref)

    rev = lambda i: nrow - 1 - i
    tile = pl.BlockSpec((tm, tn), lambda j, i: (rev(i), j))
    cspec = lambda half: pl.BlockSpec((KW, tn), lambda j, i: (0, half * ncol + j))
    acc = lambda rows: pl.BlockSpec((rows, tn), lambda j, i: (0, j))
    big = pltpu.HBM((S, F), BF16)
    blk = _nbytes((tm, D), BF16) + _nbytes((tn, D), BF16) + 10 * _nbytes((tm, tn), F32)
    in_specs, out_specs, out_shape, scratch = _carry_specs(
        rider, [pl.BlockSpec((tm, D), lambda j, i: (rev(i), 0)), pl.BlockSpec((tn, D), lambda j, i: (j, 0)),
                tile, tile, tile, tile, cspec(0), cspec(1)],
        (tile, tile, acc(KW), acc(KW), acc(1), acc(1)),
        (big, big, pltpu.HBM((KW, F), F32), pltpu.HBM((KW, F), F32), pltpu.HBM((1, F), F32),
         pltpu.HBM((1, F), F32)), [pltpu.VMEM((2, FFN_HALO, tn), F32)])
    first = lambda: (pl.program_id(0) == 0) & (pl.program_id(1) == 0)
    last = lambda: (pl.program_id(0) == ncol - 1) & (pl.program_id(1) == nrow - 1)
    res = _pallas(
        _carry(rider, body, 8, 6, first, last), name=name, grid=(ncol, nrow), out_shape=out_shape,
        in_specs=in_specs, out_specs=out_specs, scratch_shapes=scratch,
        compiler_params=_params(blk, ("arbitrary", "arbitrary")))(
            dzb, w_down, up_v, up_g, mult_v, mult_g, conv_w, conv_w, *(rider.arrays if rider else ()))
    return res[:6], list(res[6:])


def loss_head(y, target, *, tm, name):
    S, D = y.shape

    def body(y_ref, t_ref, dy_ref, l_ref):
        @pl.when(pl.program_id(0) == 0)
        def _():
            l_ref[...] = jnp.zeros_like(l_ref)

        e = y_ref[...] - t_ref[...]
        dy_ref[...] = e * (1.0 / D)
        l_ref[...] += 0.5 * jnp.sum(jnp.mean(e * e, axis=-1, keepdims=True), axis=0, keepdims=True)

    row = lambda i: (i, 0)
    return _pallas(
        body, name=name, grid=(S // tm,),
        out_shape=(pltpu.HBM((S, D), F32), pltpu.HBM((1, 1), F32)),
        in_specs=[pl.BlockSpec((tm, D), row), pl.BlockSpec((tm, D), row)],
        out_specs=(pl.BlockSpec((tm, D), row), pl.BlockSpec((1, 1), lambda i: (0, 0))),
        compiler_params=_params(3 * _nbytes((tm, D), F32), ("arbitrary",)))(y, target)


def adamw(w, g, m, v, *, tr, name):
    R, C = w.shape
    c1 = 1.0 - ADAM_B1 ** ADAM_STEP
    c2 = 1.0 - ADAM_B2 ** ADAM_STEP

    def body(w_ref, g_ref, m_ref, v_ref, go_ref, d_ref, mo_ref, vo_ref):
        gv = g_ref[...]
        mn = ADAM_B1 * m_ref[...] + (1.0 - ADAM_B1) * gv
        vn = ADAM_B2 * v_ref[...] + (1.0 - ADAM_B2) * (gv * gv)
        go_ref[...] = gv
        mo_ref[...] = mn
        vo_ref[...] = vn
        d_ref[...] = -ADAM_LR * ((mn / c1) / (jnp.sqrt(vn / c2) + ADAM_EPS) + ADAM_WD * w_ref[...])

    spec = pl.BlockSpec((tr, C), lambda i: (i, 0))
    shape = pltpu.HBM((R, C), F32)
    return _pallas(
        body, name=name, grid=(R // tr,), out_shape=(shape,) * 4, in_specs=[spec] * 4, out_specs=(spec,) * 4,
        compiler_params=_params(8 * _nbytes((tr, C), F32), ("parallel",)))(w, g, m, v)


def add_pairs(gs, gots, core, *, name):
    k = len(gs)

    def body(c_ref, *refs):
        for a_ref, b_ref, o_ref in zip(refs[:k], refs[k:2 * k], refs[2 * k:]):
            o_ref[...] = (a_ref[...].astype(F32) + b_ref[...].astype(F32)).astype(BF16)

    own = [pl.BlockSpec((None, None) + g.shape[2:], lambda i, c: (i, c[0], 0, 0)) for g in gs]
    half = [pl.BlockSpec((None,) + g.shape[1:], lambda i, c: (i, 0, 0)) for g in gots]
    grid_spec = pltpu.PrefetchScalarGridSpec(
        num_scalar_prefetch=1, grid=(N_CHIPS,), in_specs=own + half, out_specs=tuple(half))
    blk = 3 * sum(_nbytes(g.shape[1:], BF16) for g in gots)
    return _pallas(
        body, name=name, grid_spec=grid_spec, out_shape=tuple(pltpu.HBM(g.shape, BF16) for g in gots),
        compiler_params=_params(blk, ("parallel",)))(core, *gs, *gots)


def sum_chips_into(bs, dests, layer, core, *, name):
    k = len(bs)
    steps = 2

    def body(c_ref, *refs):
        for b_ref, o_ref in zip(refs[:k], refs[2 * k:]):
            acc = b_ref[0].astype(F32)
            for p in range(1, N_CHIPS):
                acc = acc + b_ref[p].astype(F32)
            o_ref[...] = acc

    ins = [pl.BlockSpec((N_CHIPS, b.shape[1] // steps, b.shape[2]), lambda i, c: (0, i, 0)) for b in bs]
    outs = tuple(pl.BlockSpec((None, None, b.shape[1] // steps, b.shape[2]), lambda i, c: (layer, c[0], i, 0))
                 for b in bs)
    grid_spec = pltpu.PrefetchScalarGridSpec(
        num_scalar_prefetch=1, grid=(steps,), in_specs=ins + [pl.BlockSpec(memory_space=pl.ANY)] * k,
        out_specs=outs)
    blk = sum(_nbytes(b.shape, BF16) + _nbytes(b.shape[1:], F32) for b in bs) // steps
    return _pallas(
        body, name=name, grid_spec=grid_spec, out_shape=tuple(pltpu.HBM(d.shape, F32) for d in dests),
        input_output_aliases={1 + k + w: w for w in range(k)},
        compiler_params=_params(blk, ("parallel",)))(core, *bs, *dests)


_HBM = pl.BlockSpec(memory_space=pltpu.HBM)


def _place():
    x, y, c = lax.axis_index("x"), lax.axis_index("y"), lax.axis_index("c")
    chips = [(1 - x, y), (x, 1 - y), (1 - x, 1 - y)]
    return x, y, c, chips


class GatherRider:
    def __init__(self, shards):
        self.arrays = list(shards)
        self.n = n = len(shards)
        self.out_shape = tuple(pltpu.HBM((N_CHIPS,) + s.shape, s.dtype) for s in shards)
        self.scratch = [pltpu.SemaphoreType.DMA((n, 3))] * 4 + [pltpu.SemaphoreType.DMA((n,))]

    def _copies(self, ins, outs, sems):
        send_ici, recv_ici, send_d2d, recv_d2d, local = sems
        x, y, c, chips = _place()
        me = 2 * x + y

        def own(w):
            return pltpu.make_async_copy(ins[w], outs[w].at[me], local.at[w])

        def ici(w, j):
            px, py = chips[j]
            return pltpu.make_async_remote_copy(
                src_ref=ins[w].at[c], dst_ref=outs[w].at[me, c], send_sem=send_ici.at[w, j],
                recv_sem=recv_ici.at[w, j], device_id=(px, py, c), device_id_type=MESH)

        def landed(w, j, half):
            px, py = chips[j]
            return outs[w].at[2 * px + py, half]

        def d2d(w, j, half):
            return pltpu.make_async_remote_copy(
                src_ref=landed(w, j, half), dst_ref=landed(w, j, half), send_sem=send_d2d.at[w, j],
                recv_sem=recv_d2d.at[w, j], device_id=(x, y, 1 - c), device_id_type=MESH)

        def ici_arrival(w, j):
            return pltpu.make_async_remote_copy(
                src_ref=landed(w, j, c), dst_ref=landed(w, j, c), send_sem=send_ici.at[w, j],
                recv_sem=recv_ici.at[w, j], device_id=(x, y, c), device_id_type=MESH)

        return c, own, ici, d2d, ici_arrival

    def start(self, ins, outs, sems):
        c, own, ici, d2d, ici_arrival = self._copies(ins, outs, sems)
        for w in range(self.n):
            own(w).start()
            for j in range(3):
                ici(w, j).start()

    def finish(self, ins, outs, sems):
        c, own, ici, d2d, ici_arrival = self._copies(ins, outs, sems)
        for w in range(self.n):
            for j in range(3):
                ici_arrival(w, j).wait_recv()
                d2d(w, j, c).start()
        for w in range(self.n):
            for j in range(3):
                d2d(w, j, 1 - c).wait_recv()
        for w in range(self.n):
            for j in range(3):
                ici(w, j).wait_send()
                d2d(w, j, c).wait_send()
            own(w).wait()


class ScatterRider:
    def __init__(self, parts):
        self.arrays = list(parts)
        self.n = n = len(parts)
        self.out_shape = tuple(pltpu.HBM(p.shape, p.dtype) for p in parts)
        self.scratch = [pltpu.SemaphoreType.DMA((n, 3))] * 2 + [pltpu.SemaphoreType.DMA((n,))]

    def _copies(self, ins, outs, sems):
        send, recv, local = sems
        x, y, c, chips = _place()
        me = 2 * x + y

        def own(w):
            return pltpu.make_async_copy(ins[w].at[me], outs[w].at[me], local.at[w])

        def copy(w, j):
            px, py = chips[j]
            return pltpu.make_async_remote_copy(
                src_ref=ins[w].at[2 * px + py], dst_ref=outs[w].at[me], send_sem=send.at[w, j],
                recv_sem=recv.at[w, j], device_id=(px, py, c), device_id_type=MESH)

        def arrival(w, j):
            px, py = chips[j]
            blk = outs[w].at[2 * px + py]
            return pltpu.make_async_remote_copy(
                src_ref=blk, dst_ref=blk, send_sem=send.at[w, j], recv_sem=recv.at[w, j],
                device_id=(x, y, c), device_id_type=MESH)

        return own, copy, arrival

    def start(self, ins, outs, sems):
        own, copy, arrival = self._copies(ins, outs, sems)
        for w in range(self.n):
            own(w).start()
            for j in range(3):
                copy(w, j).start()

    def finish(self, ins, outs, sems):
        own, copy, arrival = self._copies(ins, outs, sems)
        for w in range(self.n):
            for j in range(3):
                arrival(w, j).wait_recv()
        for w in range(self.n):
            for j in range(3):
                copy(w, j).wait_send()
            own(w).wait()


def _carry(rider, body, n_in, n_out, first, last):
    if rider is None:
        return body
    k, m = rider.n, len(rider.scratch)

    def carried(*refs):
        ins, r_in = refs[:n_in], refs[n_in:n_in + k]
        outs, r_out = refs[n_in + k:n_in + k + n_out], refs[n_in + k + n_out:n_in + 2 * k + n_out]
        rest = refs[n_in + 2 * k + n_out:]
        scratch, sems = rest[:len(rest) - m], rest[len(rest) - m:]

        @pl.when(first())
        def _():
            rider.start(r_in, r_out, sems)

        body(*ins, *outs, *scratch)

        @pl.when(last())
        def _():
            rider.finish(r_in, r_out, sems)

    return carried


def _carry_specs(rider, in_specs, out_specs, out_shape, scratch):
    if rider is None:
        return list(in_specs), tuple(out_specs), tuple(out_shape), list(scratch)
    k = rider.n
    return (list(in_specs) + [_HBM] * k, tuple(out_specs) + (_HBM,) * k, tuple(out_shape) + rider.out_shape,
            list(scratch) + list(rider.scratch))


def run_riders(riders, *, name):
    ks = [r.n for r in riders]
    ms = [len(r.scratch) for r in riders]
    k_all = sum(ks)

    def body(*refs):
        parts, i0, o0, s0 = [], 0, k_all, 2 * k_all
        for k, m in zip(ks, ms):
            parts.append((refs[i0:i0 + k], refs[o0:o0 + k], refs[s0:s0 + m]))
            i0, o0, s0 = i0 + k, o0 + k, s0 + m
        for r, p in zip(riders, parts):
            r.start(*p)
        for r, p in zip(riders, parts):
            r.finish(*p)

    res = _pallas(
        body, name=name, out_shape=tuple(o for r in riders for o in r.out_shape), in_specs=[_HBM] * k_all,
        out_specs=(_HBM,) * k_all, scratch_shapes=[s for r in riders for s in r.scratch],
    )(*[a for r in riders for a in r.arrays])
    out, o0 = [], 0
    for k in ks:
        out.append(list(res[o0:o0 + k]))
        o0 += k
    return out


def run_rider(rider, *, name):
    return run_riders([rider], name=name)[0]


def allgather_small(shards, *, name):
    n = len(shards)

    def body(*refs):
        ins, outs = refs[:n], refs[n:2 * n]
        send, recv, local = refs[2 * n:]
        x, y, c, chips = _place()
        me = 2 * x + y
        locals_ = [pltpu.make_async_copy(ins[w], outs[w].at[me], local.at[w]) for w in range(n)]
        for cp in locals_:
            cp.start()

        def copy(w, j):
            px, py = chips[j]
            return pltpu.make_async_remote_copy(
                src_ref=ins[w], dst_ref=outs[w].at[me], send_sem=send.at[w, j], recv_sem=recv.at[w, j],
                device_id=(px, py, c), device_id_type=MESH)

        def arrival(w, j):
            px, py = chips[j]
            blk = outs[w].at[2 * px + py]
            return pltpu.make_async_remote_copy(
                src_ref=blk, dst_ref=blk, send_sem=send.at[w, j], recv_sem=recv.at[w, j],
                device_id=(x, y, c), device_id_type=MESH)

        for w in range(n):
            for j in range(3):
                copy(w, j).start()
        for w in range(n):
            for j in range(3):
                arrival(w, j).wait_recv()
        for w in range(n):
            for j in range(3):
                copy(w, j).wait_send()
        for cp in locals_:
            cp.wait()

    out_shape = tuple(pltpu.HBM((N_CHIPS,) + s.shape, s.dtype) for s in shards)
    return _pallas(
        body, name=name, out_shape=out_shape, in_specs=[_HBM] * n, out_specs=(_HBM,) * n,
        scratch_shapes=[pltpu.SemaphoreType.DMA((n, 3))] * 2 + [pltpu.SemaphoreType.DMA((n,))],
    )(*shards)


class SwapRider:
    def __init__(self, grads):
        self.arrays = list(grads)
        self.n = n = len(grads)
        self.out_shape = tuple(pltpu.HBM((N_CHIPS,) + g.shape[2:], g.dtype) for g in grads)
        self.scratch = [pltpu.SemaphoreType.DMA((n,))] * 2

    def _copies(self, ins, outs, sems):
        send, recv = sems
        x, y, c, _ = _place()
        return [pltpu.make_async_remote_copy(
            src_ref=ins[w].at[:, 1 - c], dst_ref=outs[w], send_sem=send.at[w], recv_sem=recv.at[w],
            device_id=(x, y, 1 - c), device_id_type=MESH) for w in range(self.n)]

    def start(self, ins, outs, sems):
        for cp in self._copies(ins, outs, sems):
            cp.start()

    def finish(self, ins, outs, sems):
        copies = self._copies(ins, outs, sems)
        for cp in copies:
            cp.wait_recv()
        for cp in copies:
            cp.wait_send()


def rs_sibling_share(stacked, *, name):
    n = len(stacked)

    def body(*refs):
        bufs = refs[n:2 * n]
        send, recv = refs[2 * n:]
        x, y, c, _ = _place()
        shares, arrivals = [], []
        for w in range(n):
            mine, other = bufs[w].at[:, c], bufs[w].at[:, 1 - c]
            shares.append(pltpu.make_async_remote_copy(
                src_ref=mine, dst_ref=mine, send_sem=send.at[w], recv_sem=recv.at[w],
                device_id=(x, y, 1 - c), device_id_type=MESH))
            arrivals.append(pltpu.make_async_remote_copy(
                src_ref=other, dst_ref=other, send_sem=send.at[w], recv_sem=recv.at[w],
                device_id=(x, y, c), device_id_type=MESH))
        for cp in shares:
            cp.start()
        for cp in arrivals:
            cp.wait_recv()
        for cp in shares:
            cp.wait_send()

    out_shape = tuple(pltpu.HBM(s.shape, F32) for s in stacked)
    return _pallas(
        body, name=name, out_shape=out_shape, in_specs=[_HBM] * n, out_specs=(_HBM,) * n,
        input_output_aliases={w: w for w in range(n)},
        scratch_shapes=[pltpu.SemaphoreType.DMA((n,))] * 2,
    )(*stacked)


class EveryoneRider:
    def __init__(self, v):
        self.arrays = [v]
        self.n = 1
        self.out_shape = (pltpu.HBM((N_DEV,) + v.shape, v.dtype),)
        self.scratch = [pltpu.SemaphoreType.DMA((N_DEV - 1,))] * 2 + [pltpu.SemaphoreType.DMA(())]

    def _copies(self, ins, outs, sems):
        send, recv, local = sems
        x, y, c, _ = _place()
        me = 4 * x + 2 * y + c

        def flip(k):
            return (1 - x) if k & 4 else x, (1 - y) if k & 2 else y, (1 - c) if k & 1 else c

        own = pltpu.make_async_copy(ins[0], outs[0].at[me], local)
        sends, arrivals = [], []
        for k in range(1, N_DEV):
            px, py, pc = flip(k)
            sends.append(pltpu.make_async_remote_copy(
                src_ref=ins[0], dst_ref=outs[0].at[me], send_sem=send.at[k - 1], recv_sem=recv.at[k - 1],
                device_id=(px, py, pc), device_id_type=MESH))
            blk = outs[0].at[4 * px + 2 * py + pc]
            arrivals.append(pltpu.make_async_remote_copy(
                src_ref=blk, dst_ref=blk, send_sem=send.at[k - 1], recv_sem=recv.at[k - 1],
                device_id=(x, y, c), device_id_type=MESH))
        return own, sends, arrivals

    def start(self, ins, outs, sems):
        own, sends, _ = self._copies(ins, outs, sems)
        own.start()
        for cp in sends:
            cp.start()

    def finish(self, ins, outs, sems):
        own, sends, arrivals = self._copies(ins, outs, sems)
        for cp in arrivals:
            cp.wait_recv()
        for cp in sends:
            cp.wait_send()
        own.wait()


def sum_devices(land, *, name):
    n, R, C = land.shape

    def body(l_ref, o_ref):
        acc = l_ref[0]
        for d in range(1, n):
            acc = acc + l_ref[d]
        o_ref[...] = acc

    return _pallas(
        body, name=name, grid=(1,), out_shape=pltpu.HBM((R, C), F32),
        in_specs=[pl.BlockSpec((n, R, C), lambda i: (0, 0, 0))], out_specs=pl.BlockSpec((R, C), lambda i: (0, 0)),
        compiler_params=_params(_nbytes(land.shape, F32), ("arbitrary",)))(land)


def _pack(arrays):
    flat = jnp.concatenate([a.reshape(-1) for a in arrays])
    return flat.reshape(-1, LANES)


def _unpack(packed, shapes):
    flat = packed.reshape(-1)
    out, off = [], 0
    for s in shapes:
        n = 1
        for d in s:
            n *= d
        out.append(flat[off:off + n].reshape(s))
        off += n
    return out


def _row_tile(rows, cap=512):
    t = 1 << (cap.bit_length() - 1)
    while rows % t:
        t //= 2
    return t


def _adamw_tile(rows, cols):
    return _row_tile(rows, max(8, (1 << 20) // (4 * cols)))


def kernel(x, mem, w_in, conv_w, conv_b, conv_ln_g, conv_ln_b, w_out, ln1_g, ln1_b, mem_wq, mem_wk, mem_wv, mem_wo, ln2_g, ln2_b, ffn_up, ffn_conv_w, ffn_conv_b, ffn_down, ln3_g, ln3_b, loss_target, m_w_in, m_conv_w, m_conv_b, m_conv_ln_g, m_conv_ln_b, m_w_out, m_ln1_g, m_ln1_b, m_mem_wq, m_mem_wk, m_mem_wv, m_mem_wo, m_ln2_g, m_ln2_b, m_ffn_up, m_ffn_conv_w, m_ffn_conv_b, m_ffn_down, m_ln3_g, m_ln3_b, v_w_in, v_conv_w, v_conv_b, v_conv_ln_g, v_conv_ln_b, v_w_out, v_ln1_g, v_ln1_b, v_mem_wq, v_mem_wk, v_mem_wv, v_mem_wo, v_ln2_g, v_ln2_b, v_ffn_up, v_ffn_conv_w, v_ffn_conv_b, v_ffn_down, v_ln3_g, v_ln3_b):
    W = dict(w_in=w_in, conv_w=conv_w, conv_b=conv_b, conv_ln_g=conv_ln_g, conv_ln_b=conv_ln_b, w_out=w_out,
             ln1_g=ln1_g, ln1_b=ln1_b, mem_wq=mem_wq, mem_wk=mem_wk, mem_wv=mem_wv, mem_wo=mem_wo, ln2_g=ln2_g,
             ln2_b=ln2_b, ffn_up=ffn_up, ffn_conv_w=ffn_conv_w, ffn_conv_b=ffn_conv_b, ffn_down=ffn_down,
             ln3_g=ln3_g, ln3_b=ln3_b)
    M1 = dict(w_in=m_w_in, conv_w=m_conv_w, conv_b=m_conv_b, conv_ln_g=m_conv_ln_g, conv_ln_b=m_conv_ln_b,
              w_out=m_w_out, ln1_g=m_ln1_g, ln1_b=m_ln1_b, mem_wq=m_mem_wq, mem_wk=m_mem_wk, mem_wv=m_mem_wv,
              mem_wo=m_mem_wo, ln2_g=m_ln2_g, ln2_b=m_ln2_b, ffn_up=m_ffn_up, ffn_conv_w=m_ffn_conv_w,
              ffn_conv_b=m_ffn_conv_b, ffn_down=m_ffn_down, ln3_g=m_ln3_g, ln3_b=m_ln3_b)
    V2 = dict(w_in=v_w_in, conv_w=v_conv_w, conv_b=v_conv_b, conv_ln_g=v_conv_ln_g, conv_ln_b=v_conv_ln_b,
              w_out=v_w_out, ln1_g=v_ln1_g, ln1_b=v_ln1_b, mem_wq=v_mem_wq, mem_wk=v_mem_wk, mem_wv=v_mem_wv,
              mem_wo=v_mem_wo, ln2_g=v_ln2_g, ln2_b=v_ln2_b, ffn_up=v_ffn_up, ffn_conv_w=v_ffn_conv_w,
              ffn_conv_b=v_ffn_conv_b, ffn_down=v_ffn_down, ln3_g=v_ln3_g, ln3_b=v_ln3_b)

    L = w_in.shape[0]
    S, D = x.shape[1], x.shape[2]
    C = conv_b.shape[1]
    alpha = (2.0 * L) ** 0.25
    chip = 2 * lax.axis_index("x") + lax.axis_index("y")
    xs, mems, tgt = x[0], mem[0], loss_target[0]
    mem_bf = mems.astype(BF16)
    tm = _row_tile(S)
    tm_ffn = _row_tile(S, 256)
    tm_big = _row_tile(S, 1024)
    tm_half = _row_tile(S, 2048)

    def shards_of(l, names):
        out = []
        for n in names:
            wl = W[n][l].astype(BF16)
            out.append(wl.reshape(2, wl.shape[0] // 2, wl.shape[1]))
        return out

    def gathered(names, got):
        layer = {}
        for n, g in zip(names, got):
            rows, cols = W[n].shape[1], W[n].shape[2]
            layer[n] = g.reshape(N_CHIPS, rows, cols) if n in COL_SHARDED else g.reshape(N_CHIPS * rows, cols)
        return layer

    full = [dict() for _ in range(L)]
    full[0].update(gathered(RIDE_IN, run_rider(GatherRider(shards_of(0, RIDE_IN)), name="allgather_w_in")))
    cw_all, fcw_all = allgather_small([conv_w, ffn_conv_w], name="allgather_small")
    cw_full = jnp.transpose(cw_all, (1, 2, 0, 3)).reshape(L, conv_w.shape[1], -1)
    fcw_full = jnp.transpose(fcw_all, (1, 2, 0, 3)).reshape(L, ffn_conv_w.shape[1], -1)

    saved = []
    h, hb = xs, xs.astype(BF16)
    for l in range(L):
        fw = full[l]
        s = dict(x=h, xb=hb)
        s['glu'], s['qkv'] = proj_split(hb, fw['w_in'], 2 * C, tm=tm_big, name="proj")
        on_conv = RIDE_ATT if l == 0 else RIDE_FFN[1:]
        on_sb = RIDE_FFN if l == 0 else RIDE_FFN[:1]
        s['u1'], got = conv_fwd(s['glu'], cw_full[l], conv_b[l][None], name="conv_fwd",
                                rider=GatherRider(shards_of(l, on_conv)))
        fw.update(gathered(on_conv, got))
        more = l + 1 < L
        s['o_sb'], s['ltot'], got = sb_fwd(
            s['qkv'], q_col=0, name="sb_fwd", rider=GatherRider(shards_of(l, on_sb)))
        fw.update(gathered(on_sb, got))
        s['ua'] = ln_silu(s['u1'], s['o_sb'], conv_ln_g[l][None], conv_ln_b[l][None], tm=tm, name="ln_silu")
        s['x1'], s['x1b'], s['zh1'], s['rs1'] = mm_ln(
            s['ua'], fw['w_out'], h, ln1_g[l][None], ln1_b[l][None], alpha, tm=tm, name="out_proj_ln")
        s['q2'] = mm_nn(s['x1b'], fw['mem_wq'], BF16, tm=min(1024, S), tn=512, name="mem_q")
        s['k2'] = mm_nn(mem_bf, fw['mem_wk'], BF16, tm=mem_bf.shape[0], tn=512, name="mem_kv")
        s['v2'] = mm_nn(mem_bf, fw['mem_wv'], BF16, tm=mem_bf.shape[0], tn=512, name="mem_kv")
        s['o2'] = xattn_fwd(s['q2'], s['k2'], s['v2'], tm=tm, name="xattn_fwd")
        s['x2'], s['x2b'], s['zh2'], s['rs2'] = mm_ln(
            s['o2'], fw['mem_wo'], s['x1'], ln2_g[l][None], ln2_b[l][None], alpha, tm=tm, name="mem_o_ln")
        (s['upv'], s['upg'], s['mv'], s['mg'], s['hmid']), got = ffn_up_fwd(
            s['x2b'], fw['ffn_up'], fcw_full[l], ffn_conv_b[l][None], tm=tm_ffn, tn=fw['ffn_up'].shape[2],
            name="ffn_up_fwd", rider=GatherRider(shards_of(l + 1, RIDE_ATT + RIDE_IN)) if more else None)
        if more:
            full[l + 1].update(gathered(RIDE_ATT + RIDE_IN, got))
        h, hb, s['zh3'], s['rs3'] = mm_ln(
            s['hmid'], fw['ffn_down'], s['x2'], ln3_g[l][None], ln3_b[l][None], alpha, tm=tm, name="ffn_down_ln")
        saved.append(s)

    dx, loss_part = loss_head(h, tgt, tm=tm, name="loss_head")
    loss = lax.psum(loss_part[0, 0], ("x", "y", "c"))

    core = lax.axis_index("c").astype(jnp.int32).reshape(1)
    reduced_big = {n: lax.empty((L, 2, W[n].shape[1] // 2, W[n].shape[2]), F32) for n in BIG}
    small_grads = [None] * L

    def row_halves(g, names):
        parts = []
        for n in names:
            rows, cols = W[n].shape[1], W[n].shape[2]
            parts.append(g[n].reshape(N_CHIPS, 2, rows // 2, cols))
        return parts

    def pre_add(g, names):
        parts = row_halves(g, names)
        got = run_rider(SwapRider(parts), name="rs_sibling_swap")
        return list(add_pairs(parts, got, core, name="rs_add_pairs"))

    def reduce_into(names, scattered, layer):
        reduced_big.update(zip(names, sum_chips_into(
            list(scattered), [reduced_big[n] for n in names], layer, core, name="rs_sum_chips")))

    pending = None
    for l in reversed(range(L)):
        fw, s = full[l], saved[l]
        g = {}
        if l == L - 1:
            top = ln_bwd(dx, s['zh3'], s['rs3'], ln3_g[l][None], tm=tm, name="ln_bwd")
        dz3, dz3b, g['ln3_g'], g['ln3_b'] = top
        ftn = fw['ffn_up'].shape[2]
        (dupv, dupg, dfw_v, dfw_g, dfb_v, dfb_g), sc = ffn_mid_bwd(
            dz3b, fw['ffn_down'], s['upv'], s['upg'], s['mv'], s['mg'], fcw_full[l], tm=tm_ffn, tn=ftn,
            name="ffn_mid_bwd", rider=ScatterRider(pending) if pending else None)
        if pending:
            reduce_into(RIDE_MIX, sc, l + 1)
        g['ffn_conv_w'] = jnp.concatenate([dfw_v, dfw_g], axis=1)
        g['ffn_conv_b'] = jnp.concatenate([dfb_v, dfb_g], axis=1)[0]
        g['ffn_down'] = mm_tn(s['hmid'], [dz3b], tk=ftn, tn=D, tmc=min(1024, S), name="grad_ffn_down")
        dz2, dz2b, g['ln2_g'], g['ln2_b'] = mm_nt_ln_bwd(
            [dupv, dupg], fw['ffn_up'], dz3, alpha, s['zh2'], s['rs2'], ln2_g[l][None], tm=tm_ffn,
            name="ffn_up_bwd")
        g['ffn_up'] = mm_tn(s['x2b'], [dupv, dupg], tk=D, tn=ftn, shard_width=ftn, tmc=min(1024, S),
                            name="grad_ffn_up")

        do2 = mm_nt([dz2b], fw['mem_wo'], BF16, tm=tm_big, tk=512, name="mem_o_bwd")
        g['mem_wo'] = mm_tn(s['o2'], [dz2b], tk=D, tn=D, tmc=tm_half, name="grad_sq")
        dq2, dk2, dv2 = xattn_bwd(s['q2'], do2, s['k2'], s['v2'], tm=tm, name="xattn_bwd")
        dz1, dz1b, g['ln1_g'], g['ln1_b'] = mm_nt_ln_bwd(
            [dq2], fw['mem_wq'], dz2, alpha, s['zh1'], s['rs1'], ln1_g[l][None], tm=tm, name="mem_q_bwd")
        g['mem_wq'] = mm_tn(s['x1b'], [dq2], tk=D, tn=D, tmc=tm_half, name="grad_sq")
        g['mem_wk'] = mm_tn(mem_bf, [dk2], tk=D, tn=D, name="grad_mem_kv")
        g['mem_wv'] = mm_tn(mem_bf, [dv2], tk=D, tn=D, name="grad_mem_kv")

        g['w_out'] = mm_tn(s['ua'], [dz1b], tk=D, tn=D, tmc=tm_half, name="grad_sq")
        rest = row_halves(g, RIDE_REST)
        dua, got = mm_nt([dz1b], fw['w_out'], F32, tm=tm_big, tk=512, name="out_proj_bwd", rider=SwapRider(rest))
        rest = list(add_pairs(rest, got, core, name="rs_add_pairs"))
        dq, dk, dv, sc = sb_bwd(
            s['qkv'], s['ltot'], dua, q_col=0, do_col=C, name="sb_bwd",
            rider=ScatterRider(rest[:-1]))
        reduce_into(RIDE_REST[:-1], sc, l)
        du1, g['conv_ln_g'], g['conv_ln_b'] = ln_silu_bwd(
            dua, s['u1'], conv_ln_g[l][None], conv_ln_b[l][None], tm=tm, name="ln_silu_bwd")
        (da, dg, g['conv_w'], dcb), sc = conv_bwd(du1, s['glu'], cw_full[l], name="conv_bwd",
                                                  rider=ScatterRider(rest[-1:]))
        reduce_into(RIDE_REST[-1:], sc, l)
        g['conv_b'] = dcb
        dproj = jnp.concatenate([da, dg, dq, dk, dv], axis=1)
        ns_in = fw['w_in'].shape[2]
        if l > 0:
            below = saved[l - 1]
            top = mm_nt_ln_bwd([dproj], fw['w_in'], dz1, alpha, below['zh3'], below['rs3'], ln3_g[l - 1][None],
                               tm=tm, name="proj_bwd")
        else:
            dx = mm_nt([dproj], fw['w_in'], F32, tm=tm_big, tk=512, res=dz1, alpha=alpha, name="proj_bwd_x")
        g['w_in'] = mm_tn(s['xb'], [dproj], tk=512, tn=ns_in, shard_width=ns_in, name="grad_w_in")

        pending = pre_add(g, RIDE_MIX)
        small_grads[l] = {n: g[n].reshape(W[n].shape[1:-1] + (-1,)) for n in SMALL}

    grad_x = dx[None]

    small_full_shapes = []
    small_stack = []
    for n in SMALL:
        st = jnp.stack([small_grads[l][n] for l in range(L)])
        small_stack.append(st)
        small_full_shapes.append(st.shape)
    scattered, everyone = run_riders([ScatterRider(pending), EveryoneRider(_pack(small_stack))],
                                     name="rs_tail_exchange")
    reduce_into(RIDE_MIX, scattered, 0)
    shared = rs_sibling_share([reduced_big[n] for n in BIG], name="rs_sibling_share")
    G = {}
    for n, sh in zip(BIG, shared):
        G[n] = sh.reshape(W[n].shape)
    reduced = _unpack(sum_devices(everyone[0], name="sum_small"), small_full_shapes)
    for n, r in zip(SMALL, reduced):
        if n in SMALL_SHARDED:
            width = W[n].shape[-1]
            r = lax.dynamic_slice_in_dim(r, chip * width, width, axis=2)
        G[n] = r

    out_g, out_d, out_m, out_v = {}, {}, {}, {}
    for n in BIG:
        shp = W[n].shape
        flat = lambda a: a.reshape(shp[0] * shp[1], shp[2])
        res = adamw(flat(W[n]), flat(G[n]), flat(M1[n]), flat(V2[n]), tr=_adamw_tile(shp[0] * shp[1], shp[2]), name="adamw")
        out_g[n], out_d[n], out_m[n], out_v[n] = [r.reshape(shp) for r in res]
    small_shapes = [W[n].shape for n in SMALL]
    packed = [_pack([d[n] for n in SMALL]) for d in (W, G, M1, V2)]
    res = adamw(*packed, tr=packed[0].shape[0], name="adamw_small")
    for d, r in zip((out_g, out_d, out_m, out_v), res):
        for n, a in zip(SMALL, _unpack(r, small_shapes)):
            d[n] = a

    return (loss, grad_x, *[out_g[n] for n in WEIGHTS], *[out_d[n] for n in WEIGHTS],
            *[out_m[n] for n in WEIGHTS], *[out_v[n] for n in WEIGHTS])
```

```python
import functools

import jax
import jax.numpy as jnp
from jax import lax
from jax.experimental import pallas as pl
from jax.experimental.pallas import tpu as pltpu

F32 = jnp.float32
BF16 = jnp.bfloat16
MESH = pl.DeviceIdType.MESH

LN_EPS = 1e-5
SB_HEADS = 8
MEM_HEADS = 4
ADAM_LR, ADAM_B1, ADAM_B2, ADAM_EPS, ADAM_WD, ADAM_STEP = 0.001, 0.9, 0.999, 1e-08, 0.01, 10

LANES = 128
V7X_VMEM_BYTES = 64 << 20
VMEM_CAP = V7X_VMEM_BYTES - (6 << 20)
N_CHIPS = 4
N_DEV = 8

BIG = ('w_in', 'w_out', 'mem_wq', 'mem_wk', 'mem_wv', 'mem_wo', 'ffn_up', 'ffn_down')
RIDE_IN = ('w_in',)
RIDE_ATT = ('w_out', 'mem_wq', 'mem_wk', 'mem_wv', 'mem_wo')
RIDE_FFN = ('ffn_up', 'ffn_down')
RIDE_MIX = ('w_in',)
RIDE_REST = ('w_out', 'mem_wq', 'mem_wk', 'mem_wv', 'mem_wo', 'ffn_up', 'ffn_down')
COL_SHARDED = ('w_in', 'ffn_up')
SMALL = ('conv_w', 'conv_b', 'conv_ln_g', 'conv_ln_b', 'ln1_g', 'ln1_b', 'ln2_g', 'ln2_b',
         'ffn_conv_w', 'ffn_conv_b', 'ln3_g', 'ln3_b')
SMALL_SHARDED = ('conv_w', 'ffn_conv_w')
WEIGHTS = ('w_in', 'conv_w', 'conv_b', 'conv_ln_g', 'conv_ln_b', 'w_out', 'ln1_g', 'ln1_b',
           'mem_wq', 'mem_wk', 'mem_wv', 'mem_wo', 'ln2_g', 'ln2_b', 'ffn_up', 'ffn_conv_w',
           'ffn_conv_b', 'ffn_down', 'ln3_g', 'ln3_b')


def _params(block_bytes, semantics=None, **kw):
    limit = int(min(max(2 * block_bytes + (8 << 20), 32 << 20), VMEM_CAP))
    return pltpu.CompilerParams(dimension_semantics=semantics, vmem_limit_bytes=limit, **kw)


def _pallas(body, **kw):
    call = pl.pallas_call(body, **kw)

    def run(*args):
        return call(*[pltpu.with_memory_space_constraint(a, pltpu.HBM)
                      if jnp.issubdtype(a.dtype, jnp.floating) else a for a in args])

    return run


def _nbytes(shape, dtype):
    n = 1
    for s in shape:
        n *= s
    return n * jnp.dtype(dtype).itemsize


def _dot(a, b):
    return jnp.dot(a, b, preferred_element_type=F32)


def _dot_nt(a, b):
    return lax.dot_general(a, b, (((1,), (1,)), ((), ())), preferred_element_type=F32)


def _dot_tn(a, b):
    return lax.dot_general(a, b, (((0,), (0,)), ((), ())), preferred_element_type=F32)


def _sigmoid(x):
    return 1.0 / (1.0 + jnp.exp(-x))


def mm_nn(a, b, out_dtype, *, tm, tn, name):
    M, K = a.shape
    sharded = b.ndim == 3
    if sharded:
        nsh, _, ns = b.shape
        N, per = nsh * ns, ns // tn
        b_spec = pl.BlockSpec((None, K, tn), lambda i, j: (j // per, 0, j % per))
    else:
        N = b.shape[1]
        b_spec = pl.BlockSpec((K, tn), lambda i, j: (0, j))

    def body(a_ref, b_ref, o_ref):
        o_ref[...] = _dot(a_ref[...].astype(BF16), b_ref[...]).astype(o_ref.dtype)

    blk = _nbytes((tm, K), a.dtype) + _nbytes((K, tn), BF16) + _nbytes((tm, tn), out_dtype)
    return _pallas(
        body, name=name, out_shape=pltpu.HBM((M, N), out_dtype), grid=(M // tm, N // tn),
        in_specs=[pl.BlockSpec((tm, K), lambda i, j: (i, 0)), b_spec],
        out_specs=pl.BlockSpec((tm, tn), lambda i, j: (i, j)),
        compiler_params=_params(blk, ("parallel", "parallel")))(a, b)


def mm_pair(a, b1, b2, *, name):
    M, K = a.shape
    N = b1.shape[1]

    def body(a_ref, b1_ref, b2_ref, o1_ref, o2_ref):
        av = a_ref[...]
        o1_ref[...] = _dot(av, b1_ref[...]).astype(BF16)
        o2_ref[...] = _dot(av, b2_ref[...]).astype(BF16)

    whole = lambda r, c: pl.BlockSpec((r, c), lambda i: (0, 0))
    return _pallas(
        body, name=name, grid=(1,), out_shape=(pltpu.HBM((M, N), BF16), pltpu.HBM((M, N), BF16)),
        in_specs=[whole(M, K), whole(K, N), whole(K, N)], out_specs=(whole(M, N), whole(M, N)),
        compiler_params=_params(3 * _nbytes((K, N), BF16), ("arbitrary",)))(a, b1, b2)


def mm_tn_pair(a, b1, b2, *, name):
    M, K = a.shape
    N = b1.shape[1]

    def body(a_ref, b1_ref, b2_ref, o1_ref, o2_ref):
        av = a_ref[...]
        o1_ref[...] = _dot_tn(av, b1_ref[...].astype(BF16)).astype(BF16)
        o2_ref[...] = _dot_tn(av, b2_ref[...].astype(BF16)).astype(BF16)

    whole = lambda r, c: pl.BlockSpec((r, c), lambda i: (0, 0))
    return _pallas(
        body, name=name, grid=(1,), out_shape=(pltpu.HBM((K, N), BF16), pltpu.HBM((K, N), BF16)),
        in_specs=[whole(M, K), whole(M, N), whole(M, N)], out_specs=(whole(K, N), whole(K, N)),
        compiler_params=_params(4 * _nbytes((K, N), BF16), ("arbitrary",)))(a, b1, b2)


def proj_split(a, b, n_f32, *, tm, name):
    M, K = a.shape
    nsh, _, ns = b.shape
    N = nsh * ns

    def body(a_ref, b_ref, lo_ref, hi_ref):
        av = a_ref[...]
        for s in range(nsh):
            acc = _dot(av, b_ref[s])
            c0, c1 = s * ns, (s + 1) * ns
            cut = min(max(n_f32 - c0, 0), ns)
            if cut > 0:
                lo_ref[:, c0:c0 + cut] = acc[:, 0:cut]
            if cut < ns:
                hi_ref[:, c0 + cut - n_f32:c1 - n_f32] = acc[:, cut:ns].astype(BF16)

    blk = _nbytes((tm, K), BF16) + _nbytes((K, N), BF16) + _nbytes((tm, N), F32)
    return _pallas(
        body, name=name, grid=(M // tm,),
        out_shape=(pltpu.HBM((M, n_f32), F32), pltpu.HBM((M, N - n_f32), BF16)),
        in_specs=[pl.BlockSpec((tm, K), lambda i: (i, 0)), pl.BlockSpec((nsh, K, ns), lambda i: (0, 0, 0))],
        out_specs=(pl.BlockSpec((tm, n_f32), lambda i: (i, 0)), pl.BlockSpec((tm, N - n_f32), lambda i: (i, 0))),
        compiler_params=_params(blk, ("parallel",)))(a, b)


def mm_ln(a, b, x, gamma, beta, alpha, *, tm, name):
    M, K = a.shape
    D = b.shape[1]

    def body(a_ref, b_ref, x_ref, g_ref, be_ref, y_ref, yb_ref, zh_ref, rs_ref):
        z = alpha * x_ref[...] + _dot(a_ref[...], b_ref[...])
        mu = jnp.mean(z, axis=-1, keepdims=True)
        zc = z - mu
        rstd = lax.rsqrt(jnp.mean(zc * zc, axis=-1, keepdims=True) + LN_EPS)
        zh = zc * rstd
        y = zh * g_ref[...] + be_ref[...]
        y_ref[...] = y
        yb_ref[...] = y.astype(BF16)
        zh_ref[...] = zh
        rs_ref[...] = rstd

    row = lambda i: (i, 0)
    fix = lambda i: (0, 0)
    blk = _nbytes((tm, K), BF16) + _nbytes((K, D), BF16) + 4 * _nbytes((tm, D), F32)
    return _pallas(
        body, name=name, grid=(M // tm,),
        out_shape=(pltpu.HBM((M, D), F32), pltpu.HBM((M, D), BF16),
                   pltpu.HBM((M, D), F32), pltpu.HBM((M, 1), F32)),
        in_specs=[pl.BlockSpec((tm, K), row), pl.BlockSpec((K, D), fix), pl.BlockSpec((tm, D), row),
                  pl.BlockSpec((1, D), fix), pl.BlockSpec((1, D), fix)],
        out_specs=(pl.BlockSpec((tm, D), row), pl.BlockSpec((tm, D), row), pl.BlockSpec((tm, D), row),
                   pl.BlockSpec((tm, 1), row)),
        compiler_params=_params(blk, ("parallel",)))(a, b, x, gamma, beta)


def ln_bwd(dy, zh, rstd, gamma, *, tm, name):
    M, D = dy.shape

    def body(dy_ref, zh_ref, rs_ref, g_ref, dz_ref, dzb_ref, dg_ref, db_ref):
        @pl.when(pl.program_id(0) == 0)
        def _():
            dg_ref[...] = jnp.zeros_like(dg_ref)
            db_ref[...] = jnp.zeros_like(db_ref)

        dyv, zhv = dy_ref[...], zh_ref[...]
        dg_ref[...] += jnp.sum(dyv * zhv, axis=0, keepdims=True)
        db_ref[...] += jnp.sum(dyv, axis=0, keepdims=True)
        dzh = dyv * g_ref[...]
        m1 = jnp.mean(dzh, axis=-1, keepdims=True)
        m2 = jnp.mean(dzh * zhv, axis=-1, keepdims=True)
        dz = rs_ref[...] * (dzh - m1 - zhv * m2)
        dz_ref[...] = dz
        dzb_ref[...] = dz.astype(BF16)

    row = lambda i: (i, 0)
    fix = lambda i: (0, 0)
    return _pallas(
        body, name=name, grid=(M // tm,),
        out_shape=(pltpu.HBM((M, D), F32), pltpu.HBM((M, D), BF16),
                   pltpu.HBM((1, D), F32), pltpu.HBM((1, D), F32)),
        in_specs=[pl.BlockSpec((tm, D), row), pl.BlockSpec((tm, D), row), pl.BlockSpec((tm, 1), row),
                  pl.BlockSpec((1, D), fix)],
        out_specs=(pl.BlockSpec((tm, D), row), pl.BlockSpec((tm, D), row), pl.BlockSpec((1, D), fix),
                   pl.BlockSpec((1, D), fix)),
        compiler_params=_params(4 * _nbytes((tm, D), F32), ("arbitrary",)))(dy, zh, rstd, gamma)


def mm_nt(a_list, b, out_dtype, *, tm, tk, name, res=None, alpha=None, rider=None):
    M = a_list[0].shape[0]
    widths = [a.shape[1] for a in a_list]
    sharded = b.ndim == 3
    if sharded:
        nsh, K, ns = b.shape
        b_spec = pl.BlockSpec((nsh, tk, ns), lambda i, j: (0, j, 0))
        for w in widths:
            assert w % ns == 0
    else:
        K, N = b.shape
        ns = None
        b_spec = pl.BlockSpec((tk, N), lambda i, j: (j, 0))
    n_a = len(a_list)

    def body(*refs):
        a_refs, b_ref = refs[:n_a], refs[n_a]
        o_ref = refs[-1]
        acc = None
        off = 0
        for a_ref, w in zip(a_refs, widths):
            if sharded:
                for p in range(w // ns):
                    t = _dot_nt(a_ref[:, p * ns:(p + 1) * ns].astype(BF16), b_ref[off // ns + p])
                    acc = t if acc is None else acc + t
            else:
                t = _dot_nt(a_ref[...].astype(BF16), b_ref[:, off:off + w])
                acc = t if acc is None else acc + t
            off += w
        if res is not None:
            acc = acc + alpha * refs[n_a + 1][...]
        o_ref[...] = acc.astype(o_ref.dtype)

    in_specs = [pl.BlockSpec((tm, w), lambda i, j: (i, 0)) for w in widths] + [b_spec]
    args = list(a_list) + [b]
    if res is not None:
        in_specs.append(pl.BlockSpec((tm, tk), lambda i, j: (i, j)))
        args.append(res)
    blk = (sum(_nbytes((tm, w), a.dtype) for a, w in zip(a_list, widths)) + _nbytes((tk, sum(widths)), BF16)
           + 2 * _nbytes((tm, tk), F32))
    in_specs, out_specs, out_shape, scratch = _carry_specs(
        rider, in_specs, (pl.BlockSpec((tm, tk), lambda i, j: (i, j)),), (pltpu.HBM((M, K), out_dtype),), [])
    first = lambda: (pl.program_id(0) == 0) & (pl.program_id(1) == 0)
    last = lambda: (pl.program_id(0) == M // tm - 1) & (pl.program_id(1) == K // tk - 1)
    res_all = _pallas(
        _carry(rider, body, len(args), 1, first, last), name=name, out_shape=out_shape, grid=(M // tm, K // tk),
        in_specs=in_specs, out_specs=out_specs, scratch_shapes=scratch,
        compiler_params=_params(blk, ("arbitrary", "arbitrary")))(*args, *(rider.arrays if rider else ()))
    return res_all[0] if rider is None else (res_all[0], list(res_all[1:]))


def mm_nt_ln_bwd(a_list, b, res, alpha, zh, rstd, gamma, *, tm, name):
    M, D = res.shape
    widths = [a.shape[1] for a in a_list]
    sharded = b.ndim == 3
    if sharded:
        nsh, _, ns = b.shape
        b_spec = pl.BlockSpec((nsh, D, ns), lambda i: (0, 0, 0))
    else:
        ns = None
        b_spec = pl.BlockSpec((D, b.shape[1]), lambda i: (0, 0))
    n_a = len(a_list)

    def body(*refs):
        a_refs, b_ref = refs[:n_a], refs[n_a]
        res_ref, zh_ref, rs_ref, g_ref = refs[n_a + 1:n_a + 5]
        dz_ref, dzb_ref, dg_ref, db_ref = refs[n_a + 5:]

        @pl.when(pl.program_id(0) == 0)
        def _():
            dg_ref[...] = jnp.zeros_like(dg_ref)
            db_ref[...] = jnp.zeros_like(db_ref)

        dy = alpha * res_ref[...]
        off = 0
        for a_ref, w in zip(a_refs, widths):
            if sharded:
                for p in range(w // ns):
                    dy = dy + _dot_nt(a_ref[:, p * ns:(p + 1) * ns], b_ref[off // ns + p])
            else:
                dy = dy + _dot_nt(a_ref[...], b_ref[:, off:off + w])
            off += w
        zhv = zh_ref[...]
        dg_ref[...] += jnp.sum(dy * zhv, axis=0, keepdims=True)
        db_ref[...] += jnp.sum(dy, axis=0, keepdims=True)
        dzh = dy * g_ref[...]
        m1 = jnp.mean(dzh, axis=-1, keepdims=True)
        m2 = jnp.mean(dzh * zhv, axis=-1, keepdims=True)
        dz = rs_ref[...] * (dzh - m1 - zhv * m2)
        dz_ref[...] = dz
        dzb_ref[...] = dz.astype(BF16)

    row = lambda i: (i, 0)
    fix = lambda i: (0, 0)
    in_specs = [pl.BlockSpec((tm, w), row) for w in widths] + [
        b_spec, pl.BlockSpec((tm, D), row), pl.BlockSpec((tm, D), row), pl.BlockSpec((tm, 1), row),
        pl.BlockSpec((1, D), fix)]
    blk = (sum(_nbytes((tm, w), BF16) for w in widths) + _nbytes((D, sum(widths)), BF16)
           + 5 * _nbytes((tm, D), F32))
    return _pallas(
        body, name=name, grid=(M // tm,),
        out_shape=(pltpu.HBM((M, D), F32), pltpu.HBM((M, D), BF16), pltpu.HBM((1, D), F32),
                   pltpu.HBM((1, D), F32)),
        in_specs=in_specs,
        out_specs=(pl.BlockSpec((tm, D), row), pl.BlockSpec((tm, D), row), pl.BlockSpec((1, D), fix),
                   pl.BlockSpec((1, D), fix)),
        compiler_params=_params(blk, ("arbitrary",)))(*a_list, b, res, zh, rstd, gamma)


def mm_tn(a, b_list, *, tk, tn, name, shard_width=None, tmc=None):
    M, K = a.shape
    tmc = M if tmc is None else tmc
    nm = M // tmc
    widths = [b.shape[1] for b in b_list]
    N = sum(widths)
    starts, s = [], 0
    for w in widths:
        assert w % tn == 0
        starts.append(s)
        s += w // tn
    n_b = len(b_list)

    def body(*refs):
        a_ref, b_refs, o_ref, acc = refs[0], refs[1:1 + n_b], refs[-2], refs[-1]
        j, m = pl.program_id(1), pl.program_id(2)
        for b_ref, st, w in zip(b_refs, starts, widths):
            @pl.when((j >= st) & (j < st + w // tn))
            def _(b_ref=b_ref):
                t = _dot_tn(a_ref[...].astype(BF16), b_ref[...].astype(BF16))
                if nm == 1:
                    o_ref[...] = t.astype(o_ref.dtype)
                else:
                    @pl.when(m == 0)
                    def _():
                        acc[...] = t

                    @pl.when(m > 0)
                    def _():
                        acc[...] += t

                    @pl.when(m == nm - 1)
                    def _():
                        o_ref[...] = acc[...].astype(o_ref.dtype)

    def b_map(st, w):
        nb = w // tn
        return lambda i, j, m: (jnp.where((j >= st) & (j < st + nb), m, 0), jnp.clip(j - st, 0, nb - 1))

    in_specs = [pl.BlockSpec((tmc, tk), lambda i, j, m: (m, i))]
    in_specs += [pl.BlockSpec((tmc, tn), b_map(st, w)) for st, w in zip(starts, widths)]
    if shard_width is None:
        out_shape = pltpu.HBM((K, N), BF16)
        out_spec = pl.BlockSpec((tk, tn), lambda i, j, m: (i, j))
    else:
        per = shard_width // tn
        out_shape = pltpu.HBM((N // shard_width, K, shard_width), BF16)
        out_spec = pl.BlockSpec((None, tk, tn), lambda i, j, m: (j // per, i, j % per))
    acc_shape = (tk, tn) if nm > 1 else (8, LANES)
    blk = (_nbytes((tmc, tk), a.dtype) + n_b * _nbytes((tmc, tn), b_list[0].dtype) + 2 * _nbytes((tk, tn), F32))
    return _pallas(
        body, name=name, out_shape=out_shape, grid=(K // tk, N // tn, nm), in_specs=in_specs, out_specs=out_spec,
        scratch_shapes=[pltpu.VMEM(acc_shape, F32)],
        compiler_params=_params(blk, ("parallel", "arbitrary", "arbitrary")))(a, *b_list)


CONV_PAD = 32
CONV_CHUNK = 128


def _rows(win, off, n, shifts):
    b, a = off % 8, off // 8
    if b not in shifts:
        shifts[b] = win if b == 0 else win[b:b + n + CONV_PAD - 8, :]
    return shifts[b][8 * a:8 * a + n, :]


def _by_residue(n_taps, offset):
    return sorted(range(n_taps), key=lambda k: (offset(k) % 8, k))


def conv_fwd(proj, conv_w, conv_b, *, name, rider=None):
    S = proj.shape[0]
    KW, C = conv_w.shape
    nct = C // LANES
    rc = min(CONV_CHUNK, S)

    def body(a_ref, g_ref, w_ref, b_ref, o_ref, pad):
        pad[0:CONV_PAD, :] = jnp.zeros((CONV_PAD, LANES), F32)
        pad[CONV_PAD:, :] = a_ref[...] * _sigmoid(g_ref[...])
        w = w_ref[...]
        bias = b_ref[...]

        def chunk(i, _):
            base = pl.multiple_of(i * rc, rc)
            win = pad[pl.ds(base, rc + CONV_PAD), :]
            acc = jnp.zeros((rc, LANES), F32) + bias
            shifts = {}
            for k in _by_residue(KW, lambda k: CONV_PAD - (KW - 1) + k):
                acc = acc + w[k:k + 1, :] * _rows(win, CONV_PAD - (KW - 1) + k, rc, shifts)
            o_ref[pl.ds(base, rc), :] = acc
            return 0

        lax.fori_loop(0, S // rc, chunk, 0)

    in_specs, out_specs, out_shape, scratch = _carry_specs(
        rider, [pl.BlockSpec((S, LANES), lambda c: (0, c)), pl.BlockSpec((S, LANES), lambda c: (0, c + nct)),
                pl.BlockSpec((KW, LANES), lambda c: (0, c)), pl.BlockSpec((1, LANES), lambda c: (0, c))],
        (pl.BlockSpec((S, LANES), lambda c: (0, c)),), (pltpu.HBM((S, C), F32),),
        [pltpu.VMEM((S + CONV_PAD, LANES), F32)])
    first = lambda: pl.program_id(0) == 0
    last = lambda: pl.program_id(0) == nct - 1
    res = _pallas(
        _carry(rider, body, 4, 1, first, last), name=name, grid=(nct,), out_shape=out_shape,
        in_specs=in_specs, out_specs=out_specs, scratch_shapes=scratch,
        compiler_params=_params(4 * _nbytes((S, LANES), F32), ("arbitrary",)))(
            proj, proj, conv_w, conv_b, *(rider.arrays if rider else ()))
    return res[0], list(res[1:])


def conv_bwd(du1, proj, conv_w, *, name, rider=None):
    S = proj.shape[0]
    KW, C = conv_w.shape
    nct = C // LANES
    rc = min(CONV_CHUNK, S)

    def body(d_ref, a_ref, g_ref, w_ref, da_ref, dg_ref, dw_ref, db_ref, pad_u, pad_d, du0, dw_acc):
        dw_acc[...] = jnp.zeros_like(dw_acc)
        pad_u[0:CONV_PAD, :] = jnp.zeros((CONV_PAD, LANES), F32)
        pad_u[CONV_PAD:, :] = a_ref[...] * _sigmoid(g_ref[...])
        pad_d[0:S, :] = d_ref[...]
        pad_d[S:, :] = jnp.zeros((CONV_PAD, LANES), F32)
        w = w_ref[...]
        db_ref[...] = jnp.sum(d_ref[...], axis=0, keepdims=True)

        def chunk(i, _):
            base = pl.multiple_of(i * rc, rc)
            d = pad_d[pl.ds(base, rc), :]
            win_u = pad_u[pl.ds(base, rc + CONV_PAD), :]
            win_d = pad_d[pl.ds(base, rc + CONV_PAD), :]
            shifts = {}
            for k in _by_residue(KW, lambda k: CONV_PAD - (KW - 1) + k):
                u_k = _rows(win_u, CONV_PAD - (KW - 1) + k, rc, shifts)
                dw_acc[k:k + 1, :] += jnp.sum(d * u_k, axis=0, keepdims=True)
            acc = jnp.zeros((rc, LANES), F32)
            shifts = {}
            for k in _by_residue(KW, lambda k: KW - 1 - k):
                acc = acc + w[k:k + 1, :] * _rows(win_d, KW - 1 - k, rc, shifts)
            du0[pl.ds(base, rc), :] = acc
            return 0

        lax.fori_loop(0, S // rc, chunk, 0)
        dw_ref[...] = dw_acc[0:KW, :]
        a, sg = a_ref[...], _sigmoid(g_ref[...])
        d0 = du0[...]
        da_ref[...] = (d0 * sg).astype(BF16)
        dg_ref[...] = (d0 * a * sg * (1.0 - sg)).astype(BF16)

    col = lambda c: (0, c)
    in_specs, out_specs, out_shape, scratch = _carry_specs(
        rider, [pl.BlockSpec((S, LANES), col), pl.BlockSpec((S, LANES), col),
                pl.BlockSpec((S, LANES), lambda c: (0, c + nct)), pl.BlockSpec((KW, LANES), col)],
        (pl.BlockSpec((S, LANES), col), pl.BlockSpec((S, LANES), col), pl.BlockSpec((KW, LANES), col),
         pl.BlockSpec((1, LANES), col)),
        (pltpu.HBM((S, C), BF16), pltpu.HBM((S, C), BF16), pltpu.HBM((KW, C), F32), pltpu.HBM((1, C), F32)),
        [pltpu.VMEM((S + CONV_PAD, LANES), F32), pltpu.VMEM((S + CONV_PAD, LANES), F32),
         pltpu.VMEM((S, LANES), F32), pltpu.VMEM((CONV_PAD, LANES), F32)])
    first = lambda: pl.program_id(0) == 0
    last = lambda: pl.program_id(0) == nct - 1
    res = _pallas(
        _carry(rider, body, 4, 4, first, last), name=name, grid=(nct,), out_shape=out_shape,
        in_specs=in_specs, out_specs=out_specs, scratch_shapes=scratch,
        compiler_params=_params(8 * _nbytes((S, LANES), F32), ("arbitrary",)))(
            du1, proj, proj, conv_w, *(rider.arrays if rider else ()))
    return res[:4], list(res[4:])


def ln_silu(u1, o_sb, gamma, beta, *, tm, name):
    S, C = u1.shape

    def body(u_ref, o_ref, g_ref, b_ref, out_ref):
        z = u_ref[...]
        mu = jnp.mean(z, axis=-1, keepdims=True)
        zc = z - mu
        y = zc * lax.rsqrt(jnp.mean(zc * zc, axis=-1, keepdims=True) + LN_EPS) * g_ref[...] + b_ref[...]
        out_ref[:, 0:C] = (y * _sigmoid(y)).astype(BF16)
        out_ref[:, C:] = o_ref[...].astype(BF16)

    row = lambda i: (i, 0)
    fix = lambda i: (0, 0)
    return _pallas(
        body, name=name, out_shape=pltpu.HBM((S, 2 * C), BF16), grid=(S // tm,),
        in_specs=[pl.BlockSpec((tm, C), row), pl.BlockSpec((tm, C), row), pl.BlockSpec((1, C), fix),
                  pl.BlockSpec((1, C), fix)],
        out_specs=pl.BlockSpec((tm, 2 * C), row),
        compiler_params=_params(4 * _nbytes((tm, C), F32), ("parallel",)))(u1, o_sb, gamma, beta)


def ln_silu_bwd(dua, u1, gamma, beta, *, tm, name):
    S, C = u1.shape

    def body(d_ref, u_ref, g_ref, b_ref, du1_ref, dg_ref, db_ref):
        @pl.when(pl.program_id(0) == 0)
        def _():
            dg_ref[...] = jnp.zeros_like(dg_ref)
            db_ref[...] = jnp.zeros_like(db_ref)

        z = u_ref[...]
        mu = jnp.mean(z, axis=-1, keepdims=True)
        zc = z - mu
        rstd = lax.rsqrt(jnp.mean(zc * zc, axis=-1, keepdims=True) + LN_EPS)
        zh = zc * rstd
        y = zh * g_ref[...] + b_ref[...]
        sg = _sigmoid(y)
        dy = d_ref[...] * (sg * (1.0 + y * (1.0 - sg)))
        dg_ref[...] += jnp.sum(dy * zh, axis=0, keepdims=True)
        db_ref[...] += jnp.sum(dy, axis=0, keepdims=True)
        dzh = dy * g_ref[...]
        m1 = jnp.mean(dzh, axis=-1, keepdims=True)
        m2 = jnp.mean(dzh * zh, axis=-1, keepdims=True)
        du1_ref[...] = rstd * (dzh - m1 - zh * m2)

    row = lambda i: (i, 0)
    fix = lambda i: (0, 0)
    return _pallas(
        body, name=name, grid=(S // tm,),
        out_shape=(pltpu.HBM((S, C), F32), pltpu.HBM((1, C), F32),
                   pltpu.HBM((1, C), F32)),
        in_specs=[pl.BlockSpec((tm, C), row), pl.BlockSpec((tm, C), row), pl.BlockSpec((1, C), fix),
                  pl.BlockSpec((1, C), fix)],
        out_specs=(pl.BlockSpec((tm, C), row), pl.BlockSpec((1, C), fix), pl.BlockSpec((1, C), fix)),
        compiler_params=_params(4 * _nbytes((tm, C), F32), ("arbitrary",)))(dua, u1, gamma, beta)


SB_BLOCK = 256
SB_STOP = -105.0
SB_GROUP = 4


def _split_dot(x, tri):
    hi = x.astype(BF16)
    lo = (x - hi.astype(F32)).astype(BF16)
    return _dot(hi, tri) + _dot(lo, tri)


def _neg_softplus(z):
    return -(jnp.maximum(z, 0.0) + jnp.log(1.0 + jnp.exp(-jnp.abs(z))))


def sb_fwd(proj, *, q_col, name, rider=None):
    S = proj.shape[0]
    dh = LANES // 2
    W = SB_HEADS * dh
    BW = SB_GROUP * dh
    ngrp = W // BW
    T = min(SB_BLOCK, S)
    nblk = S // T
    scale = dh ** -0.5
    qb0 = q_col // BW
    heads = range(SB_GROUP)
    sl = [slice(h * dh, (h + 1) * dh) for h in heads]

    def body(q_ref, k_ref, v_ref, o_ref, l_ref, qs):
        r_i = lax.broadcasted_iota(jnp.int32, (T, T), 0)
        c_i = lax.broadcasted_iota(jnp.int32, (T, T), 1)
        tri = (r_i >= c_i).astype(BF16)
        vis = c_i < r_i
        lane = lax.broadcasted_iota(jnp.int32, (T, dh), 1)

        qs[...] = (q_ref[...] * scale).astype(BF16)

        def step(qb, blocks, st):
            nb = range(len(blocks))
            kb = [[k_ref[pl.ds(j0, T), sl[h]].astype(BF16) for h in heads] for j0, _ in blocks]
            vb = [[v_ref[pl.ds(j0, T), sl[h]].astype(BF16) for h in heads] for j0, _ in blocks]
            z = [[_dot_nt(qb[h], kb[b][h]) for h in heads] for b in nb]
            lk = [[_neg_softplus(z[b][h]) for h in heads] for b in nb]
            lk = [[jnp.where(vis, lk[b][h], 0.0) if blocks[b][1] else lk[b][h] for h in heads] for b in nb]
            C = [[_split_dot(lk[b][h], tri) for h in heads] for b in nb]
            R = [[st[2 * h + 1] for h in heads]]
            for b in nb:
                R.append([R[b][h] + C[b][h][:, 0:1] for h in heads])
            A = [[jnp.exp(z[b][h] + C[b][h] + R[b][h]) for h in heads] for b in nb]
            A = [[jnp.where(vis, A[b][h], 0.0) if blocks[b][1] else A[b][h] for h in heads] for b in nb]
            out = ()
            for h in heads:
                acc = st[2 * h]
                for b in nb:
                    acc = acc + _dot(A[b][h].astype(BF16), vb[b][h])
                out += (acc, R[-1][h])
            return out

        zero = (jnp.zeros((T, dh), F32), jnp.zeros((T, 1), F32))

        def finish(r0, i, c):
            walked = jnp.asarray(i - c[0]).astype(F32)
            for h in heads:
                o_ref[pl.ds(r0, T), sl[h]] = c[1 + 2 * h]
                l_ref[pl.ds(r0, T), sl[h]] = jnp.where(lane == 1, walked, c[2 + 2 * h])

        finish(0, 0, (-1,) + step([qs[0:T, sl[h]] for h in heads], [(0, True)], zero * SB_GROUP))

        def qblock(i, _):
            r0 = pl.multiple_of(i * T, T)
            qb = [qs[pl.ds(r0, T), sl[h]] for h in heads]
            state = step(qb, [(r0, True), (pl.multiple_of(r0 - T, T), False)], zero * SB_GROUP)

            def more(c):
                worst = c[2]
                for h in heads[1:]:
                    worst = jnp.maximum(worst, c[2 + 2 * h])
                return (c[0] >= 0) & (jnp.max(worst) >= SB_STOP)

            def walk(c):
                return (c[0] - 1,) + step(qb, [(pl.multiple_of(c[0] * T, T), False)], c[1:])

            finish(r0, i, lax.while_loop(more, walk, (i - 2,) + state))
            return 0

        lax.fori_loop(1, nblk, qblock, 0)

    blk = lambda off: pl.BlockSpec((S, BW), lambda g: (0, qb0 + off * ngrp + g), pipeline_mode=pl.Buffered(1))
    out = pl.BlockSpec((S, BW), lambda g: (0, g))
    in_specs, out_specs, out_shape, scratch = _carry_specs(
        rider, [blk(0), blk(1), blk(2)], (out, out), (pltpu.HBM((S, W), F32), pltpu.HBM((S, W), F32)),
        [pltpu.VMEM((S, BW), BF16)])
    first = lambda: pl.program_id(0) == 0
    last = lambda: pl.program_id(0) == ngrp - 1
    res = _pallas(
        _carry(rider, body, 3, 2, first, last), name=name, grid=(ngrp,), out_shape=out_shape,
        in_specs=in_specs, out_specs=out_specs, scratch_shapes=scratch,
        compiler_params=_params(5 * _nbytes((S, BW), F32), ("arbitrary",)))(
            proj, proj, proj, *(rider.arrays if rider else ()))
    return res[0], res[1], list(res[2:])


def sb_bwd(proj, ltot, dua, *, q_col, do_col, name, rider=None):
    S = proj.shape[0]
    dh = LANES // 2
    W = SB_HEADS * dh
    BW = SB_GROUP * dh
    ngrp = W // BW
    T = min(SB_BLOCK, S)
    nblk = S // T
    scale = dh ** -0.5
    qb0 = q_col // BW
    db0 = do_col // BW
    heads = range(SB_GROUP)
    sl = [slice(h * dh, (h + 1) * dh) for h in heads]

    def body(q_ref, k_ref, v_ref, l_ref, do_ref, dq_ref, dk_ref, dv_ref, dks, dvs):
        r_i = lax.broadcasted_iota(jnp.int32, (T, T), 0)
        c_i = lax.broadcasted_iota(jnp.int32, (T, T), 1)
        tri_rev = (r_i >= c_i).astype(BF16)
        tri_fwd = (r_i <= c_i).astype(BF16)
        vis = c_i < r_i

        dks[...] = jnp.zeros_like(dks)
        dvs[...] = jnp.zeros_like(dvs)

        def step(qb, dob, Lt, blocks, st):
            nb = range(len(blocks))
            kb = [[k_ref[pl.ds(j0, T), sl[h]].astype(BF16) for h in heads] for j0, _ in blocks]
            vb = [[v_ref[pl.ds(j0, T), sl[h]].astype(BF16) for h in heads] for j0, _ in blocks]
            z = [[_dot_nt(qb[h], kb[b][h]) for h in heads] for b in nb]
            dA =[[_dot_nt(dob[h], vb[b][h]) for h in heads] for b in nb]
            lk = [[_neg_softplus(z[b][h]) for h in heads] for b in nb]
            beta = [[jnp.exp(z[b][h] + lk[b][h]) for h in heads] for b in nb]
            lk = [[jnp.where(vis, lk[b][h], 0.0) if blocks[b][1] else lk[b][h] for h in heads] for b in nb]
            C = [[_split_dot(lk[b][h], tri_rev) for h in heads] for b in nb]
            P = [[st[3 * h + 1] for h in heads]]
            for b in nb:
                P.append([P[b][h] + C[b][h][:, 0:1] for h in heads])
            A = [[jnp.exp(z[b][h] + C[b][h] + (Lt[h] - P[b + 1][h])) for h in heads] for b in nb]
            A = [[jnp.where(vis, A[b][h], 0.0) if blocks[b][1] else A[b][h] for h in heads] for b in nb]
            g = [[A[b][h] * dA[b][h] for h in heads] for b in nb]
            Gin = [[_split_dot(g[b][h], tri_fwd) for h in heads] for b in nb]
            Gp = [[st[3 * h + 2] for h in heads]]
            for b in nb:
                Gp.append([Gp[b][h] + Gin[b][h][:, T - 1:T] for h in heads])
            dz = [[g[b][h] - beta[b][h] * (Gp[b][h] + Gin[b][h]) for h in heads] for b in nb]
            dz = [[jnp.where(vis, dz[b][h], 0.0) if blocks[b][1] else dz[b][h] for h in heads] for b in nb]
            dzb = [[dz[b][h].astype(BF16) for h in heads] for b in nb]
            out = ()
            for h in heads:
                dq = st[3 * h]
                for b in nb:
                    j0 = blocks[b][0]
                    dvs[pl.ds(j0, T), sl[h]] += _dot_tn(A[b][h].astype(BF16), dob[h])
                    dks[pl.ds(j0, T), sl[h]] += _dot_tn(dzb[b][h], qb[h])
                    dq = dq + _dot(dzb[b][h], kb[b][h])
                out += (dq, P[-1][h], Gp[-1][h])
            return out

        zero = jnp.zeros((T, 1), F32)
        init = (jnp.zeros((T, dh), F32), zero, zero)

        def operands(r0):
            return ([(q_ref[pl.ds(r0, T), sl[h]] * scale).astype(BF16) for h in heads],
                    [do_ref[pl.ds(r0, T), sl[h]].astype(BF16) for h in heads],
                    [l_ref[pl.ds(r0, T), h * dh:h * dh + 1] for h in heads])

        def finish(r0, c):
            for h in heads:
                dq_ref[pl.ds(r0, T), sl[h]] = (c[3 * h] * scale).astype(BF16)

        finish(0, step(*operands(0), [(0, True)], init * SB_GROUP))

        def qblock(i, _):
            r0 = pl.multiple_of(i * T, T)
            qb, dob, Lt = operands(r0)
            walked = jnp.clip(jnp.max(l_ref[pl.ds(r0, 8), 1:2]).astype(jnp.int32), 2, i + 1)

            def inner(j, c):
                return step(qb, dob, Lt, [(pl.multiple_of(j * T, T), False)], c)

            c = lax.fori_loop(i + 1 - walked, i - 1, inner, init * SB_GROUP)
            finish(r0, step(qb, dob, Lt, [(pl.multiple_of(r0 - T, T), False), (r0, True)], c))
            return 0

        lax.fori_loop(1, nblk, qblock, 0)
        dk_ref[...] = dks[...].astype(BF16)
        dv_ref[...] = dvs[...].astype(BF16)

    once = pl.Buffered(1)
    blk = lambda off: pl.BlockSpec((S, BW), lambda g: (0, qb0 + off * ngrp + g), pipeline_mode=once)
    out = pl.BlockSpec((S, BW), lambda g: (0, g))
    o_shape = pltpu.HBM((S, W), BF16)
    in_specs, out_specs, out_shape, scratch = _carry_specs(
        rider, [blk(0), blk(1), blk(2), pl.BlockSpec((S, BW), lambda g: (0, g), pipeline_mode=once),
                pl.BlockSpec((S, BW), lambda g: (0, db0 + g), pipeline_mode=once)], (out, out, out),
        (o_shape, o_shape, o_shape), [pltpu.VMEM((S, BW), F32)] * 2)
    first = lambda: pl.program_id(0) == 0
    last = lambda: pl.program_id(0) == ngrp - 1
    res = _pallas(
        _carry(rider, body, 5, 3, first, last), name=name, grid=(ngrp,), out_shape=out_shape,
        in_specs=in_specs, out_specs=out_specs, scratch_shapes=scratch,
        compiler_params=_params(6 * _nbytes((S, BW), F32), ("arbitrary",)))(
            proj, proj, proj, ltot, dua, *(rider.arrays if rider else ()))
    return res[0], res[1], res[2], list(res[3:])


def xattn_fwd(q, k, v, *, tm, name):
    S, D = q.shape
    Mlen = k.shape[0]
    hd = D // MEM_HEADS
    scale = hd ** -0.5

    def body(q_ref, k_ref, v_ref, o_ref):
        for h in range(MEM_HEADS):
            sl = slice(h * hd, (h + 1) * hd)
            s = _dot_nt(q_ref[:, sl], k_ref[:, sl]) * scale
            e = jnp.exp(s - jnp.max(s, axis=-1, keepdims=True))
            p = e / jnp.sum(e, axis=-1, keepdims=True)
            o_ref[:, sl] = _dot(p.astype(BF16), v_ref[:, sl]).astype(BF16)

    row = lambda i: (i, 0)
    fix = lambda i: (0, 0)
    return _pallas(
        body, name=name, out_shape=pltpu.HBM((S, D), BF16), grid=(S // tm,),
        in_specs=[pl.BlockSpec((tm, D), row), pl.BlockSpec((Mlen, D), fix), pl.BlockSpec((Mlen, D), fix)],
        out_specs=pl.BlockSpec((tm, D), row),
        compiler_params=_params(4 * _nbytes((tm, D), F32), ("parallel",)))(q, k, v)


def xattn_bwd(q, do, k, v, *, tm, name):
    S, D = q.shape
    Mlen = k.shape[0]
    hd = D // MEM_HEADS
    scale = hd ** -0.5

    def body(q_ref, do_ref, k_ref, v_ref, dq_ref, dk_ref, dv_ref):
        @pl.when(pl.program_id(0) == 0)
        def _():
            dk_ref[...] = jnp.zeros_like(dk_ref)
            dv_ref[...] = jnp.zeros_like(dv_ref)

        for h in range(MEM_HEADS):
            sl = slice(h * hd, (h + 1) * hd)
            qh, doh, kh, vh = q_ref[:, sl], do_ref[:, sl], k_ref[:, sl], v_ref[:, sl]
            s = _dot_nt(qh, kh) * scale
            e = jnp.exp(s - jnp.max(s, axis=-1, keepdims=True))
            p = e / jnp.sum(e, axis=-1, keepdims=True)
            dp = _dot_nt(doh, vh)
            ds = (p * (dp - jnp.sum(p * dp, axis=-1, keepdims=True)) * scale).astype(BF16)
            dq_ref[:, sl] = _dot(ds, kh).astype(BF16)
            dk_ref[:, sl] += _dot_tn(ds, qh)
            dv_ref[:, sl] += _dot_tn(p.astype(BF16), doh)

    row = lambda i: (i, 0)
    fix = lambda i: (0, 0)
    return _pallas(
        body, name=name, grid=(S // tm,),
        out_shape=(pltpu.HBM((S, D), BF16), pltpu.HBM((Mlen, D), F32),
                   pltpu.HBM((Mlen, D), F32)),
        in_specs=[pl.BlockSpec((tm, D), row), pl.BlockSpec((tm, D), row), pl.BlockSpec((Mlen, D), fix),
                  pl.BlockSpec((Mlen, D), fix)],
        out_specs=(pl.BlockSpec((tm, D), row), pl.BlockSpec((Mlen, D), fix), pl.BlockSpec((Mlen, D), fix)),
        compiler_params=_params(6 * _nbytes((tm, D), F32), ("arbitrary",)))(q, do, k, v)


FFN_HALO = 8


def _conv3(ext, w, lo):
    tm = ext.shape[0] - FFN_HALO
    return (w[0:1, :] * ext[lo:lo + tm, :] + w[1:2, :] * ext[lo + 1:lo + 1 + tm, :]
            + w[2:3, :] * ext[lo + 2:lo + 2 + tm, :])


def ffn_up_fwd(xb, w_up, conv_w, conv_b, *, tm, tn, name, rider=None):
    S, D = xb.shape
    nsh, _, ns = w_up.shape
    F = nsh * ns // 2
    per = ns // tn
    ncol = F // tn
    KW = conv_w.shape[0]
    assert KW == 3

    def body(x_ref, wv_ref, wg_ref, cwv_ref, cwg_ref, cbv_ref, cbg_ref, uv_ref, ug_ref, mv_ref, mg_ref, h_ref,
             carry):
        @pl.when(pl.program_id(1) == 0)
        def _():
            carry[...] = jnp.zeros_like(carry)

        x = x_ref[...]
        uv = _dot(x, wv_ref[...])
        ug = _dot(x, wg_ref[...])
        uv_ref[...] = uv.astype(BF16)
        ug_ref[...] = ug.astype(BF16)
        lo = FFN_HALO - (KW - 1)
        cv = _conv3(jnp.concatenate([carry[0], uv], axis=0), cwv_ref[...], lo) + cbv_ref[...]
        cg = _conv3(jnp.concatenate([carry[1], ug], axis=0), cwg_ref[...], lo) + cbg_ref[...]
        carry[0] = uv[tm - FFN_HALO:, :]
        carry[1] = ug[tm - FFN_HALO:, :]
        sg = _sigmoid(cg)
        act = cg * sg
        mv_ref[...] = act.astype(BF16)
        mg_ref[...] = (cv * (sg + act * (1.0 - sg))).astype(BF16)
        h_ref[...] = (act * cv).astype(BF16)

    wspec = lambda half: pl.BlockSpec((None, D, tn), lambda j, i: (half * (nsh // 2) + j // per, 0, j % per))
    cspec = lambda rows, half: pl.BlockSpec((rows, tn), lambda j, i: (0, half * ncol + j))
    out = pl.BlockSpec((tm, tn), lambda j, i: (i, j))
    o_shape = pltpu.HBM((S, F), BF16)
    blk = _nbytes((tm, D), BF16) + 2 * _nbytes((D, tn), BF16) + 8 * _nbytes((tm, tn), F32)
    nrow = S // tm
    in_specs, out_specs, out_shape, scratch = _carry_specs(
        rider, [pl.BlockSpec((tm, D), lambda j, i: (i, 0)), wspec(0), wspec(1), cspec(KW, 0), cspec(KW, 1),
                cspec(1, 0), cspec(1, 1)], (out,) * 5, (o_shape,) * 5, [pltpu.VMEM((2, FFN_HALO, tn), F32)])
    first = lambda: (pl.program_id(0) == 0) & (pl.program_id(1) == 0)
    last = lambda: (pl.program_id(0) == ncol - 1) & (pl.program_id(1) == nrow - 1)
    res = _pallas(
        _carry(rider, body, 7, 5, first, last), name=name, grid=(ncol, nrow), out_shape=out_shape,
        in_specs=in_specs, out_specs=out_specs, scratch_shapes=scratch,
        compiler_params=_params(blk, ("arbitrary", "arbitrary")))(
            xb, w_up, w_up, conv_w, conv_w, conv_b, conv_b, *(rider.arrays if rider else ()))
    return res[:5], list(res[5:])


def ffn_mid_bwd(dzb, w_down, up_v, up_g, mult_v, mult_g, conv_w, *, tm, tn, name, rider=None):
    S, D = dzb.shape
    F = up_v.shape[1]
    ncol = F // tn
    nrow = S // tm
    KW = conv_w.shape[0]
    assert KW == 3

    def body(dz_ref, wd_ref, uv_ref, ug_ref, mv_ref, mg_ref, cwv_ref, cwg_ref,
             dv_ref, dg_ref, dwv_ref, dwg_ref, dbv_ref, dbg_ref, carry):
        @pl.when(pl.program_id(1) == 0)
        def _():
            carry[...] = jnp.zeros_like(carry)
            for r in (dwv_ref, dwg_ref, dbv_ref, dbg_ref):
                r[...] = jnp.zeros_like(r)

        dh = _dot_nt(dz_ref[...], wd_ref[...])
        dcv = dh * mv_ref[...].astype(F32)
        dcg = dh * mg_ref[...].astype(F32)

        def back(dc, u_ref, cw, slot, du_ref, dw_ref, db_ref):
            ext = jnp.concatenate([dc, carry[slot]], axis=0)
            ahead = [dc, ext[1:tm + 1, :], ext[2:tm + 2, :]]
            du = cw[2:3, :] * ahead[0] + cw[1:2, :] * ahead[1] + cw[0:1, :] * ahead[2]
            du_ref[...] = du.astype(BF16)
            carry[slot] = dc[0:FFN_HALO, :]
            u = u_ref[...].astype(F32)
            for k in range(KW):
                dw_ref[k:k + 1, :] += jnp.sum(ahead[KW - 1 - k] * u, axis=0, keepdims=True)
            db_ref[...] += jnp.sum(dc, axis=0, keepdims=True)

        back(dcv, uv_ref, cwv_ref[...], 0, dv_ref, dwv_ref, dbv_ref)
        back(dcg, ug_ref, cwg_ref[...], 1, dg_ref, dwg_ref, dbg_ref)

    rev = lambda i: nrow - 1 - i
    tile = pl.BlockSpec((tm, tn), lambda j, i: (rev(i), j))
    cspec = lambda half: pl.BlockSpec((KW, tn), lambda j, i: (0, half * ncol + j))
    acc = lambda rows: pl.BlockSpec((rows, tn), lambda j, i: (0, j))
    big = pltpu.HBM((S, F), BF16)
    blk = _nbytes((tm, D), BF16) + _nbytes((tn, D), BF16) + 10 * _nbytes((tm, tn), F32)
    in_specs, out_specs, out_shape, scratch = _carry_specs(
        rider, [pl.BlockSpec((tm, D), lambda j, i: (rev(i), 0)), pl.BlockSpec((tn, D), lambda j, i: (j, 0)),
                tile, tile, tile, tile, cspec(0), cspec(1)],
        (tile, tile, acc(KW), acc(KW), acc(1), acc(1)),
        (big, big, pltpu.HBM((KW, F), F32), pltpu.HBM((KW, F), F32), pltpu.HBM((1, F), F32),
         pltpu.HBM((1, F), F32)), [pltpu.VMEM((2, FFN_HALO, tn), F32)])
    first = lambda: (pl.program_id(0) == 0) & (pl.program_id(1) == 0)
    last = lambda: (pl.program_id(0) == ncol - 1) & (pl.program_id(1) == nrow - 1)
    res = _pallas(
        _carry(rider, body, 8, 6, first, last), name=name, grid=(ncol, nrow), out_shape=out_shape,
        in_specs=in_specs, out_specs=out_specs, scratch_shapes=scratch,
        compiler_params=_params(blk, ("arbitrary", "arbitrary")))(
            dzb, w_down, up_v, up_g, mult_v, mult_g, conv_w, conv_w, *(rider.arrays if rider else ()))
    return res[:6], list(res[6:])


def loss_head(y, target, *, tm, name):
    S, D = y.shape

    def body(y_ref, t_ref, dy_ref, l_ref):
        @pl.when(pl.program_id(0) == 0)
        def _():
            l_ref[...] = jnp.zeros_like(l_ref)

        e = y_ref[...] - t_ref[...]
        dy_ref[...] = e * (1.0 / D)
        l_ref[...] += 0.5 * jnp.sum(jnp.mean(e * e, axis=-1, keepdims=True), axis=0, keepdims=True)

    row = lambda i: (i, 0)
    return _pallas(
        body, name=name, grid=(S // tm,),
        out_shape=(pltpu.HBM((S, D), F32), pltpu.HBM((1, 1), F32)),
        in_specs=[pl.BlockSpec((tm, D), row), pl.BlockSpec((tm, D), row)],
        out_specs=(pl.BlockSpec((tm, D), row), pl.BlockSpec((1, 1), lambda i: (0, 0))),
        compiler_params=_params(3 * _nbytes((tm, D), F32), ("arbitrary",)))(y, target)


def adamw(w, g, m, v, *, tr, name):
    R, C = w.shape
    c1 = 1.0 - ADAM_B1 ** ADAM_STEP
    c2 = 1.0 - ADAM_B2 ** ADAM_STEP

    def body(w_ref, g_ref, m_ref, v_ref, go_ref, d_ref, mo_ref, vo_ref):
        gv = g_ref[...]
        mn = ADAM_B1 * m_ref[...] + (1.0 - ADAM_B1) * gv
        vn = ADAM_B2 * v_ref[...] + (1.0 - ADAM_B2) * (gv * gv)
        go_ref[...] = gv
        mo_ref[...] = mn
        vo_ref[...] = vn
        d_ref[...] = -ADAM_LR * ((mn / c1) / (jnp.sqrt(vn / c2) + ADAM_EPS) + ADAM_WD * w_ref[...])

    spec = pl.BlockSpec((tr, C), lambda i: (i, 0))
    shape = pltpu.HBM((R, C), F32)
    return _pallas(
        body, name=name, grid=(R // tr,), out_shape=(shape,) * 4, in_specs=[spec] * 4, out_specs=(spec,) * 4,
        compiler_params=_params(8 * _nbytes((tr, C), F32), ("parallel",)))(w, g, m, v)


def add_pairs(gs, gots, core, *, name):
    k = len(gs)

    def body(c_ref, *refs):
        for a_ref, b_ref, o_ref in zip(refs[:k], refs[k:2 * k], refs[2 * k:]):
            o_ref[...] = (a_ref[...].astype(F32) + b_ref[...].astype(F32)).astype(BF16)

    own = [pl.BlockSpec((None, None) + g.shape[2:], lambda i, c: (i, c[0], 0, 0)) for g in gs]
    half = [pl.BlockSpec((None,) + g.shape[1:], lambda i, c: (i, 0, 0)) for g in gots]
    grid_spec = pltpu.PrefetchScalarGridSpec(
        num_scalar_prefetch=1, grid=(N_CHIPS,), in_specs=own + half, out_specs=tuple(half))
    blk = 3 * sum(_nbytes(g.shape[1:], BF16) for g in gots)
    return _pallas(
        body, name=name, grid_spec=grid_spec, out_shape=tuple(pltpu.HBM(g.shape, BF16) for g in gots),
        compiler_params=_params(blk, ("parallel",)))(core, *gs, *gots)


def sum_chips_into(bs, dests, layer, core, *, name):
    k = len(bs)
    steps = 2

    def body(c_ref, *refs):
        for b_ref, o_ref in zip(refs[:k], refs[2 * k:]):
            acc = b_ref[0].astype(F32)
            for p in range(1, N_CHIPS):
                acc = acc + b_ref[p].astype(F32)
            o_ref[...] = acc

    ins = [pl.BlockSpec((N_CHIPS, b.shape[1] // steps, b.shape[2]), lambda i, c: (0, i, 0)) for b in bs]
    outs = tuple(pl.BlockSpec((None, None, b.shape[1] // steps, b.shape[2]), lambda i, c: (layer, c[0], i, 0))
                 for b in bs)
    grid_spec = pltpu.PrefetchScalarGridSpec(
        num_scalar_prefetch=1, grid=(steps,), in_specs=ins + [pl.BlockSpec(memory_space=pl.ANY)] * k,
        out_specs=outs)
    blk = sum(_nbytes(b.shape, BF16) + _nbytes(b.shape[1:], F32) for b in bs) // steps
    return _pallas(
        body, name=name, grid_spec=grid_spec, out_shape=tuple(pltpu.HBM(d.shape, F32) for d in dests),
        input_output_aliases={1 + k + w: w for w in range(k)},
        compiler_params=_params(blk, ("parallel",)))(core, *bs, *dests)


_HBM = pl.BlockSpec(memory_space=pltpu.HBM)


def _place():
    x, y, c = lax.axis_index("x"), lax.axis_index("y"), lax.axis_index("c")
    chips = [(1 - x, y), (x, 1 - y), (1 - x, 1 - y)]
    return x, y, c, chips


class GatherRider:
    def __init__(self, shards):
        self.arrays = list(shards)
        self.n = n = len(shards)
        self.out_shape = tuple(pltpu.HBM((N_CHIPS,) + s.shape, s.dtype) for s in shards)
        self.scratch = [pltpu.SemaphoreType.DMA((n, 3))] * 4 + [pltpu.SemaphoreType.DMA((n,))]

    def _copies(self, ins, outs, sems):
        send_ici, recv_ici, send_d2d, recv_d2d, local = sems
        x, y, c, chips = _place()
        me = 2 * x + y

        def own(w):
            return pltpu.make_async_copy(ins[w], outs[w].at[me], local.at[w])

        def ici(w, j):
            px, py = chips[j]
            return pltpu.make_async_remote_copy(
                src_ref=ins[w].at[c], dst_ref=outs[w].at[me, c], send_sem=send_ici.at[w, j],
                recv_sem=recv_ici.at[w, j], device_id=(px, py, c), device_id_type=MESH)

        def landed(w, j, half):
            px, py = chips[j]
            return outs[w].at[2 * px + py, half]

        def d2d(w, j, half):
            return pltpu.make_async_remote_copy(
                src_ref=landed(w, j, half), dst_ref=landed(w, j, half), send_sem=send_d2d.at[w, j],
                recv_sem=recv_d2d.at[w, j], device_id=(x, y, 1 - c), device_id_type=MESH)

        def ici_arrival(w, j):
            return pltpu.make_async_remote_copy(
                src_ref=landed(w, j, c), dst_ref=landed(w, j, c), send_sem=send_ici.at[w, j],
                recv_sem=recv_ici.at[w, j], device_id=(x, y, c), device_id_type=MESH)

        return c, own, ici, d2d, ici_arrival

    def start(self, ins, outs, sems):
        c, own, ici, d2d, ici_arrival = self._copies(ins, outs, sems)
        for w in range(self.n):
            own(w).start()
            for j in range(3):
                ici(w, j).start()

    def finish(self, ins, outs, sems):
        c, own, ici, d2d, ici_arrival = self._copies(ins, outs, sems)
        for w in range(self.n):
            for j in range(3):
                ici_arrival(w, j).wait_recv()
                d2d(w, j, c).start()
        for w in range(self.n):
            for j in range(3):
                d2d(w, j, 1 - c).wait_recv()
        for w in range(self.n):
            for j in range(3):
                ici(w, j).wait_send()
                d2d(w, j, c).wait_send()
            own(w).wait()


class ScatterRider:
    def __init__(self, parts):
        self.arrays = list(parts)
        self.n = n = len(parts)
        self.out_shape = tuple(pltpu.HBM(p.shape, p.dtype) for p in parts)
        self.scratch = [pltpu.SemaphoreType.DMA((n, 3))] * 2 + [pltpu.SemaphoreType.DMA((n,))]

    def _copies(self, ins, outs, sems):
        send, recv, local = sems
        x, y, c, chips = _place()
        me = 2 * x + y

        def own(w):
            return pltpu.make_async_copy(ins[w].at[me], outs[w].at[me], local.at[w])

        def copy(w, j):
            px, py = chips[j]
            return pltpu.make_async_remote_copy(
                src_ref=ins[w].at[2 * px + py], dst_ref=outs[w].at[me], send_sem=send.at[w, j],
                recv_sem=recv.at[w, j], device_id=(px, py, c), device_id_type=MESH)

        def arrival(w, j):
            px, py = chips[j]
            blk = outs[w].at[2 * px + py]
            return pltpu.make_async_remote_copy(
                src_ref=blk, dst_ref=blk, send_sem=send.at[w, j], recv_sem=recv.at[w, j],
                device_id=(x, y, c), device_id_type=MESH)

        return own, copy, arrival

    def start(self, ins, outs, sems):
        own, copy, arrival = self._copies(ins, outs, sems)
        for w in range(self.n):
            own(w).start()
            for j in range(3):
                copy(w, j).start()

    def finish(self, ins, outs, sems):
        own, copy, arrival = self._copies(ins, outs, sems)
        for w in range(self.n):
            for j in range(3):
                arrival(w, j).wait_recv()
        for w in range(self.n):
            for j in range(3):
                copy(w, j).wait_send()
            own(w).wait()


def _carry(rider, body, n_in, n_out, first, last):
    if rider is None:
        return body
    k, m = rider.n, len(rider.scratch)

    def carried(*refs):
        ins, r_in = refs[:n_in], refs[n_in:n_in + k]
        outs, r_out = refs[n_in + k:n_in + k + n_out], refs[n_in + k + n_out:n_in + 2 * k + n_out]
        rest = refs[n_in + 2 * k + n_out:]
        scratch, sems = rest[:len(rest) - m], rest[len(rest) - m:]

        @pl.when(first())
        def _():
            rider.start(r_in, r_out, sems)

        body(*ins, *outs, *scratch)

        @pl.when(last())
        def _():
            rider.finish(r_in, r_out, sems)

    return carried


def _carry_specs(rider, in_specs, out_specs, out_shape, scratch):
    if rider is None:
        return list(in_specs), tuple(out_specs), tuple(out_shape), list(scratch)
    k = rider.n
    return (list(in_specs) + [_HBM] * k, tuple(out_specs) + (_HBM,) * k, tuple(out_shape) + rider.out_shape,
            list(scratch) + list(rider.scratch))


def run_riders(riders, *, name):
    ks = [r.n for r in riders]
    ms = [len(r.scratch) for r in riders]
    k_all = sum(ks)

    def body(*refs):
        parts, i0, o0, s0 = [], 0, k_all, 2 * k_all
        for k, m in zip(ks, ms):
            parts.append((refs[i0:i0 + k], refs[o0:o0 + k], refs[s0:s0 + m]))
            i0, o0, s0 = i0 + k, o0 + k, s0 + m
        for r, p in zip(riders, parts):
            r.start(*p)
        for r, p in zip(riders, parts):
            r.finish(*p)

    res = _pallas(
        body, name=name, out_shape=tuple(o for r in riders for o in r.out_shape), in_specs=[_HBM] * k_all,
        out_specs=(_HBM,) * k_all, scratch_shapes=[s for r in riders for s in r.scratch],
    )(*[a for r in riders for a in r.arrays])
    out, o0 = [], 0
    for k in ks:
        out.append(list(res[o0:o0 + k]))
        o0 += k
    return out


def run_rider(rider, *, name):
    return run_riders([rider], name=name)[0]


class SmallGatherRider:
    def __init__(self, shards):
        self.arrays = list(shards)
        self.n = n = len(shards)
        self.out_shape = tuple(pltpu.HBM((N_CHIPS,) + s.shape, s.dtype) for s in shards)
        self.scratch = [pltpu.SemaphoreType.DMA((n, 3))] * 2 + [pltpu.SemaphoreType.DMA((n,))]

    def _copies(self, ins, outs, sems):
        send, recv, local = sems
        x, y, c, chips = _place()
        me = 2 * x + y

        def own(w):
            return pltpu.make_async_copy(ins[w], outs[w].at[me], local.at[w])

        def copy(w, j):
            px, py = chips[j]
            return pltpu.make_async_remote_copy(
                src_ref=ins[w], dst_ref=outs[w].at[me], send_sem=send.at[w, j], recv_sem=recv.at[w, j],
                device_id=(px, py, c), device_id_type=MESH)

        def arrival(w, j):
            px, py = chips[j]
            blk = outs[w].at[2 * px + py]
            return pltpu.make_async_remote_copy(
                src_ref=blk, dst_ref=blk, send_sem=send.at[w, j], recv_sem=recv.at[w, j],
                device_id=(x, y, c), device_id_type=MESH)

        return own, copy, arrival

    def start(self, ins, outs, sems):
        own, copy, arrival = self._copies(ins, outs, sems)
        for w in range(self.n):
            own(w).start()
            for j in range(3):
                copy(w, j).start()

    def finish(self, ins, outs, sems):
        own, copy, arrival = self._copies(ins, outs, sems)
        for w in range(self.n):
            for j in range(3):
                arrival(w, j).wait_recv()
        for w in range(self.n):
            for j in range(3):
                copy(w, j).wait_send()
            own(w).wait()


class SwapRider:
    def __init__(self, grads):
        self.arrays = list(grads)
        self.n = n = len(grads)
        self.out_shape = tuple(pltpu.HBM((N_CHIPS,) + g.shape[2:], g.dtype) for g in grads)
        self.scratch = [pltpu.SemaphoreType.DMA((n,))] * 2

    def _copies(self, ins, outs, sems):
        send, recv = sems
        x, y, c, _ = _place()
        return [pltpu.make_async_remote_copy(
            src_ref=ins[w].at[:, 1 - c], dst_ref=outs[w], send_sem=send.at[w], recv_sem=recv.at[w],
            device_id=(x, y, 1 - c), device_id_type=MESH) for w in range(self.n)]

    def start(self, ins, outs, sems):
        for cp in self._copies(ins, outs, sems):
            cp.start()

    def finish(self, ins, outs, sems):
        copies = self._copies(ins, outs, sems)
        for cp in copies:
            cp.wait_recv()
        for cp in copies:
            cp.wait_send()


def rs_sibling_share(stacked, *, name):
    n = len(stacked)

    def body(*refs):
        bufs = refs[n:2 * n]
        send, recv = refs[2 * n:]
        x, y, c, _ = _place()
        shares, arrivals = [], []
        for w in range(n):
            mine, other = bufs[w].at[:, c], bufs[w].at[:, 1 - c]
            shares.append(pltpu.make_async_remote_copy(
                src_ref=mine, dst_ref=mine, send_sem=send.at[w], recv_sem=recv.at[w],
                device_id=(x, y, 1 - c), device_id_type=MESH))
            arrivals.append(pltpu.make_async_remote_copy(
                src_ref=other, dst_ref=other, send_sem=send.at[w], recv_sem=recv.at[w],
                device_id=(x, y, c), device_id_type=MESH))
        for cp in shares:
            cp.start()
        for cp in arrivals:
            cp.wait_recv()
        for cp in shares:
            cp.wait_send()

    out_shape = tuple(pltpu.HBM(s.shape, F32) for s in stacked)
    return _pallas(
        body, name=name, out_shape=out_shape, in_specs=[_HBM] * n, out_specs=(_HBM,) * n,
        input_output_aliases={w: w for w in range(n)},
        scratch_shapes=[pltpu.SemaphoreType.DMA((n,))] * 2,
    )(*stacked)


class EveryoneRider:
    def __init__(self, v):
        self.arrays = [v]
        self.n = 1
        self.out_shape = (pltpu.HBM((N_DEV,) + v.shape, v.dtype),)
        self.scratch = [pltpu.SemaphoreType.DMA((N_DEV - 1,))] * 2 + [pltpu.SemaphoreType.DMA(())]

    def _copies(self, ins, outs, sems):
        send, recv, local = sems
        x, y, c, _ = _place()
        me = 4 * x + 2 * y + c

        def flip(k):
            return (1 - x) if k & 4 else x, (1 - y) if k & 2 else y, (1 - c) if k & 1 else c

        own = pltpu.make_async_copy(ins[0], outs[0].at[me], local)
        sends, arrivals = [], []
        for k in range(1, N_DEV):
            px, py, pc = flip(k)
            sends.append(pltpu.make_async_remote_copy(
                src_ref=ins[0], dst_ref=outs[0].at[me], send_sem=send.at[k - 1], recv_sem=recv.at[k - 1],
                device_id=(px, py, pc), device_id_type=MESH))
            blk = outs[0].at[4 * px + 2 * py + pc]
            arrivals.append(pltpu.make_async_remote_copy(
                src_ref=blk, dst_ref=blk, send_sem=send.at[k - 1], recv_sem=recv.at[k - 1],
                device_id=(x, y, c), device_id_type=MESH))
        return own, sends, arrivals

    def start(self, ins, outs, sems):
        own, sends, _ = self._copies(ins, outs, sems)
        own.start()
        for cp in sends:
            cp.start()

    def finish(self, ins, outs, sems):
        own, sends, arrivals = self._copies(ins, outs, sems)
        for cp in arrivals:
            cp.wait_recv()
        for cp in sends:
            cp.wait_send()
        own.wait()


def sum_devices(land, *, name):
    n, R, C = land.shape

    def body(l_ref, o_ref):
        acc = l_ref[0]
        for d in range(1, n):
            acc = acc + l_ref[d]
        o_ref[...] = acc

    return _pallas(
        body, name=name, grid=(1,), out_shape=pltpu.HBM((R, C), F32),
        in_specs=[pl.BlockSpec((n, R, C), lambda i: (0, 0, 0))], out_specs=pl.BlockSpec((R, C), lambda i: (0, 0)),
        compiler_params=_params(_nbytes(land.shape, F32), ("arbitrary",)))(land)


def _pack(arrays):
    flat = jnp.concatenate([a.reshape(-1) for a in arrays])
    return flat.reshape(-1, LANES)


def _unpack(packed, shapes):
    flat = packed.reshape(-1)
    out, off = [], 0
    for s in shapes:
        n = 1
        for d in s:
            n *= d
        out.append(flat[off:off + n].reshape(s))
        off += n
    return out


def _row_tile(rows, cap=512):
    t = 1 << (cap.bit_length() - 1)
    while rows % t:
        t //= 2
    return t


def _adamw_tile(rows, cols):
    return _row_tile(rows, max(8, (1 << 20) // (4 * cols)))


def kernel(x, mem, w_in, conv_w, conv_b, conv_ln_g, conv_ln_b, w_out, ln1_g, ln1_b, mem_wq, mem_wk, mem_wv, mem_wo, ln2_g, ln2_b, ffn_up, ffn_conv_w, ffn_conv_b, ffn_down, ln3_g, ln3_b, loss_target, m_w_in, m_conv_w, m_conv_b, m_conv_ln_g, m_conv_ln_b, m_w_out, m_ln1_g, m_ln1_b, m_mem_wq, m_mem_wk, m_mem_wv, m_mem_wo, m_ln2_g, m_ln2_b, m_ffn_up, m_ffn_conv_w, m_ffn_conv_b, m_ffn_down, m_ln3_g, m_ln3_b, v_w_in, v_conv_w, v_conv_b, v_conv_ln_g, v_conv_ln_b, v_w_out, v_ln1_g, v_ln1_b, v_mem_wq, v_mem_wk, v_mem_wv, v_mem_wo, v_ln2_g, v_ln2_b, v_ffn_up, v_ffn_conv_w, v_ffn_conv_b, v_ffn_down, v_ln3_g, v_ln3_b):
    W = dict(w_in=w_in, conv_w=conv_w, conv_b=conv_b, conv_ln_g=conv_ln_g, conv_ln_b=conv_ln_b, w_out=w_out,
             ln1_g=ln1_g, ln1_b=ln1_b, mem_wq=mem_wq, mem_wk=mem_wk, mem_wv=mem_wv, mem_wo=mem_wo, ln2_g=ln2_g,
             ln2_b=ln2_b, ffn_up=ffn_up, ffn_conv_w=ffn_conv_w, ffn_conv_b=ffn_conv_b, ffn_down=ffn_down,
             ln3_g=ln3_g, ln3_b=ln3_b)
    M1 = dict(w_in=m_w_in, conv_w=m_conv_w, conv_b=m_conv_b, conv_ln_g=m_conv_ln_g, conv_ln_b=m_conv_ln_b,
              w_out=m_w_out, ln1_g=m_ln1_g, ln1_b=m_ln1_b, mem_wq=m_mem_wq, mem_wk=m_mem_wk, mem_wv=m_mem_wv,
              mem_wo=m_mem_wo, ln2_g=m_ln2_g, ln2_b=m_ln2_b, ffn_up=m_ffn_up, ffn_conv_w=m_ffn_conv_w,
              ffn_conv_b=m_ffn_conv_b, ffn_down=m_ffn_down, ln3_g=m_ln3_g, ln3_b=m_ln3_b)
    V2 = dict(w_in=v_w_in, conv_w=v_conv_w, conv_b=v_conv_b, conv_ln_g=v_conv_ln_g, conv_ln_b=v_conv_ln_b,
              w_out=v_w_out, ln1_g=v_ln1_g, ln1_b=v_ln1_b, mem_wq=v_mem_wq, mem_wk=v_mem_wk, mem_wv=v_mem_wv,
              mem_wo=v_mem_wo, ln2_g=v_ln2_g, ln2_b=v_ln2_b, ffn_up=v_ffn_up, ffn_conv_w=v_ffn_conv_w,
              ffn_conv_b=v_ffn_conv_b, ffn_down=v_ffn_down, ln3_g=v_ln3_g, ln3_b=v_ln3_b)

    L = w_in.shape[0]
    S, D = x.shape[1], x.shape[2]
    C = conv_b.shape[1]
    alpha = (2.0 * L) ** 0.25
    chip = 2 * lax.axis_index("x") + lax.axis_index("y")
    xs, mems, tgt = x[0], mem[0], loss_target[0]
    mem_bf = mems.astype(BF16)
    tm = _row_tile(S)
    tm_ffn = _row_tile(S, 256)
    tm_big = _row_tile(S, 1024)
    tm_half = _row_tile(S, 2048)

    def shards_of(l, names):
        out = []
        for n in names:
            wl = W[n][l].astype(BF16)
            out.append(wl.reshape(2, wl.shape[0] // 2, wl.shape[1]))
        return out

    def gathered(names, got):
        layer = {}
        for n, g in zip(names, got):
            rows, cols = W[n].shape[1], W[n].shape[2]
            layer[n] = g.reshape(N_CHIPS, rows, cols) if n in COL_SHARDED else g.reshape(N_CHIPS * rows, cols)
        return layer

    full = [dict() for _ in range(L)]
    got, (cw_all, fcw_all) = run_riders(
        [GatherRider(shards_of(0, RIDE_IN)), SmallGatherRider([conv_w, ffn_conv_w])], name="allgather_first")
    full[0].update(gathered(RIDE_IN, got))
    cw_full = jnp.transpose(cw_all, (1, 2, 0, 3)).reshape(L, conv_w.shape[1], -1)
    fcw_full = jnp.transpose(fcw_all, (1, 2, 0, 3)).reshape(L, ffn_conv_w.shape[1], -1)

    saved = []
    h, hb = xs, xs.astype(BF16)
    for l in range(L):
        fw = full[l]
        s = dict(x=h, xb=hb)
        s['glu'], s['qkv'] = proj_split(hb, fw['w_in'], 2 * C, tm=tm, name="proj")
        on_conv = RIDE_ATT if l == 0 else RIDE_FFN[1:]
        on_sb = RIDE_FFN if l == 0 else RIDE_FFN[:1]
        s['u1'], got = conv_fwd(s['glu'], cw_full[l], conv_b[l][None], name="conv_fwd",
                                rider=GatherRider(shards_of(l, on_conv)))
        fw.update(gathered(on_conv, got))
        more = l + 1 < L
        s['o_sb'], s['ltot'], got = sb_fwd(
            s['qkv'], q_col=0, name="sb_fwd", rider=GatherRider(shards_of(l, on_sb)))
        fw.update(gathered(on_sb, got))
        s['ua'] = ln_silu(s['u1'], s['o_sb'], conv_ln_g[l][None], conv_ln_b[l][None], tm=tm, name="ln_silu")
        s['x1'], s['x1b'], s['zh1'], s['rs1'] = mm_ln(
            s['ua'], fw['w_out'], h, ln1_g[l][None], ln1_b[l][None], alpha, tm=tm, name="out_proj_ln")
        s['q2'] = mm_nn(s['x1b'], fw['mem_wq'], BF16, tm=min(1024, S), tn=512, name="mem_q")
        s['k2'], s['v2'] = mm_pair(mem_bf, fw['mem_wk'], fw['mem_wv'], name="mem_kv")
        s['o2'] = xattn_fwd(s['q2'], s['k2'], s['v2'], tm=tm, name="xattn_fwd")
        s['x2'], s['x2b'], s['zh2'], s['rs2'] = mm_ln(
            s['o2'], fw['mem_wo'], s['x1'], ln2_g[l][None], ln2_b[l][None], alpha, tm=tm, name="mem_o_ln")
        (s['upv'], s['upg'], s['mv'], s['mg'], s['hmid']), got = ffn_up_fwd(
            s['x2b'], fw['ffn_up'], fcw_full[l], ffn_conv_b[l][None], tm=tm_ffn, tn=fw['ffn_up'].shape[2],
            name="ffn_up_fwd", rider=GatherRider(shards_of(l + 1, RIDE_ATT + RIDE_IN)) if more else None)
        if more:
            full[l + 1].update(gathered(RIDE_ATT + RIDE_IN, got))
        h, hb, s['zh3'], s['rs3'] = mm_ln(
            s['hmid'], fw['ffn_down'], s['x2'], ln3_g[l][None], ln3_b[l][None], alpha, tm=tm, name="ffn_down_ln")
        saved.append(s)

    dx, loss_part = loss_head(h, tgt, tm=tm, name="loss_head")
    loss = lax.psum(loss_part[0, 0], ("x", "y", "c"))

    core = lax.axis_index("c").astype(jnp.int32).reshape(1)
    reduced_big = {n: lax.empty((L, 2, W[n].shape[1] // 2, W[n].shape[2]), F32) for n in BIG}
    small_grads = [None] * L

    def row_halves(g, names):
        parts = []
        for n in names:
            rows, cols = W[n].shape[1], W[n].shape[2]
            parts.append(g[n].reshape(N_CHIPS, 2, rows // 2, cols))
        return parts

    def pre_add(g, names):
        parts = row_halves(g, names)
        got = run_rider(SwapRider(parts), name="rs_sibling_swap")
        return list(add_pairs(parts, got, core, name="rs_add_pairs"))

    def reduce_into(names, scattered, layer):
        reduced_big.update(zip(names, sum_chips_into(
            list(scattered), [reduced_big[n] for n in names], layer, core, name="rs_sum_chips")))

    pending = None
    for l in reversed(range(L)):
        fw, s = full[l], saved[l]
        g = {}
        if l == L - 1:
            top = ln_bwd(dx, s['zh3'], s['rs3'], ln3_g[l][None], tm=tm, name="ln_bwd")
        dz3, dz3b, g['ln3_g'], g['ln3_b'] = top
        ftn = fw['ffn_up'].shape[2]
        (dupv, dupg, dfw_v, dfw_g, dfb_v, dfb_g), sc = ffn_mid_bwd(
            dz3b, fw['ffn_down'], s['upv'], s['upg'], s['mv'], s['mg'], fcw_full[l], tm=tm_ffn, tn=ftn,
            name="ffn_mid_bwd", rider=ScatterRider(pending) if pending else None)
        if pending:
            reduce_into(RIDE_MIX, sc, l + 1)
        g['ffn_conv_w'] = jnp.concatenate([dfw_v, dfw_g], axis=1)
        g['ffn_conv_b'] = jnp.concatenate([dfb_v, dfb_g], axis=1)[0]
        g['ffn_down'] = mm_tn(s['hmid'], [dz3b], tk=ftn, tn=D, tmc=min(1024, S), name="grad_ffn_down")
        dz2, dz2b, g['ln2_g'], g['ln2_b'] = mm_nt_ln_bwd(
            [dupv, dupg], fw['ffn_up'], dz3, alpha, s['zh2'], s['rs2'], ln2_g[l][None], tm=tm_ffn,
            name="ffn_up_bwd")
        g['ffn_up'] = mm_tn(s['x2b'], [dupv, dupg], tk=D, tn=ftn, shard_width=ftn, tmc=min(1024, S),
                            name="grad_ffn_up")

        do2 = mm_nt([dz2b], fw['mem_wo'], BF16, tm=tm_big, tk=512, name="mem_o_bwd")
        g['mem_wo'] = mm_tn(s['o2'], [dz2b], tk=D, tn=D, tmc=tm_half, name="grad_sq")
        dq2, dk2, dv2 = xattn_bwd(s['q2'], do2, s['k2'], s['v2'], tm=tm, name="xattn_bwd")
        dz1, dz1b, g['ln1_g'], g['ln1_b'] = mm_nt_ln_bwd(
            [dq2], fw['mem_wq'], dz2, alpha, s['zh1'], s['rs1'], ln1_g[l][None], tm=tm, name="mem_q_bwd")
        g['mem_wq'] = mm_tn(s['x1b'], [dq2], tk=D, tn=D, tmc=tm_half, name="grad_sq")
        g['mem_wk'], g['mem_wv'] = mm_tn_pair(mem_bf, dk2, dv2, name="grad_mem_kv")

        g['w_out'] = mm_tn(s['ua'], [dz1b], tk=D, tn=D, tmc=tm_half, name="grad_sq")
        rest = row_halves(g, RIDE_REST)
        dua, got = mm_nt([dz1b], fw['w_out'], F32, tm=tm_big, tk=512, name="out_proj_bwd", rider=SwapRider(rest))
        rest = list(add_pairs(rest, got, core, name="rs_add_pairs"))
        dq, dk, dv, sc = sb_bwd(
            s['qkv'], s['ltot'], dua, q_col=0, do_col=C, name="sb_bwd",
            rider=ScatterRider(rest[:-1]))
        reduce_into(RIDE_REST[:-1], sc, l)
        du1, g['conv_ln_g'], g['conv_ln_b'] = ln_silu_bwd(
            dua, s['u1'], conv_ln_g[l][None], conv_ln_b[l][None], tm=tm, name="ln_silu_bwd")
        (da, dg, g['conv_w'], dcb), sc = conv_bwd(du1, s['glu'], cw_full[l], name="conv_bwd",
                                                  rider=ScatterRider(rest[-1:]))
        reduce_into(RIDE_REST[-1:], sc, l)
        g['conv_b'] = dcb
        dproj = jnp.concatenate([da, dg, dq, dk, dv], axis=1)
        ns_in = fw['w_in'].shape[2]
        if l > 0:
            below = saved[l - 1]
            top = mm_nt_ln_bwd([dproj], fw['w_in'], dz1, alpha, below['zh3'], below['rs3'], ln3_g[l - 1][None],
                               tm=tm, name="proj_bwd")
        else:
            dx = mm_nt([dproj], fw['w_in'], F32, tm=tm_big, tk=512, res=dz1, alpha=alpha, name="proj_bwd_x")
        g['w_in'] = mm_tn(s['xb'], [dproj], tk=512, tn=ns_in, shard_width=ns_in, name="grad_w_in")

        pending = pre_add(g, RIDE_MIX)
        small_grads[l] = {n: g[n].reshape(W[n].shape[1:-1] + (-1,)) for n in SMALL}

    grad_x = dx[None]

    small_full_shapes = []
    small_stack = []
    for n in SMALL:
        st = jnp.stack([small_grads[l][n] for l in range(L)])
        small_stack.append(st)
        small_full_shapes.append(st.shape)
    scattered, everyone = run_riders([ScatterRider(pending), EveryoneRider(_pack(small_stack))],
                                     name="rs_tail_exchange")
    reduce_into(RIDE_MIX, scattered, 0)
    shared = rs_sibling_share([reduced_big[n] for n in BIG], name="rs_sibling_share")
    G = {}
    for n, sh in zip(BIG, shared):
        G[n] = sh.reshape(W[n].shape)
    reduced = _unpack(sum_devices(everyone[0], name="sum_small"), small_full_shapes)
    for n, r in zip(SMALL, reduced):
        if n in SMALL_SHARDED:
            width = W[n].shape[-1]
            r = lax.dynamic_slice_in_dim(r, chip * width, width, axis=2)
        G[n] = r

    out_g, out_d, out_m, out_v = {}, {}, {}, {}
    for n in BIG:
        shp = W[n].shape
        flat = lambda a: a.reshape(shp[0] * shp[1], shp[2])
        res = adamw(flat(W[n]), flat(G[n]), flat(M1[n]), flat(V2[n]), tr=_adamw_tile(shp[0] * shp[1], shp[2]), name="adamw")
        out_g[n], out_d[n], out_m[n], out_v[n] = [r.reshape(shp) for r in res]
    small_shapes = [W[n].shape for n in SMALL]
    packed = [_pack([d[n] for n in SMALL]) for d in (W, G, M1, V2)]
    res = adamw(*packed, tr=packed[0].shape[0], name="adamw_small")
    for d, r in zip((out_g, out_d, out_m, out_v), res):
        for n, a in zip(SMALL, _unpack(r, small_shapes)):
            d[n] = a

    return (loss, grad_x, *[out_g[n] for n in WEIGHTS], *[out_d[n] for n in WEIGHTS],
            *[out_m[n] for n in WEIGHTS], *[out_v[n] for n in WEIGHTS])
```

```python
import functools

import jax
import jax.numpy as jnp
from jax import lax
from jax.experimental import pallas as pl
from jax.experimental.pallas import tpu as pltpu

F32 = jnp.float32
BF16 = jnp.bfloat16
MESH = pl.DeviceIdType.MESH

LN_EPS = 1e-5
SB_HEADS = 8
MEM_HEADS = 4
ADAM_LR, ADAM_B1, ADAM_B2, ADAM_EPS, ADAM_WD, ADAM_STEP = 0.001, 0.9, 0.999, 1e-08, 0.01, 10

LANES = 128
V7X_VMEM_BYTES = 64 << 20
VMEM_CAP = V7X_VMEM_BYTES - (6 << 20)
N_CHIPS = 4
N_DEV = 8

BIG = ('w_in', 'w_out', 'mem_wq', 'mem_wk', 'mem_wv', 'mem_wo', 'ffn_up', 'ffn_down')
RIDE_IN = ('w_in',)
RIDE_ATT = ('w_out', 'mem_wq', 'mem_wk', 'mem_wv', 'mem_wo')
RIDE_FFN = ('ffn_up', 'ffn_down')
RIDE_MIX = ('w_in',)
RIDE_REST = ('w_out', 'mem_wq', 'mem_wk', 'mem_wv', 'mem_wo', 'ffn_up', 'ffn_down')
COL_SHARDED = ('w_in', 'ffn_up')
SMALL = ('conv_w', 'conv_b', 'conv_ln_g', 'conv_ln_b', 'ln1_g', 'ln1_b', 'ln2_g', 'ln2_b',
         'ffn_conv_w', 'ffn_conv_b', 'ln3_g', 'ln3_b')
SMALL_SHARDED = ('conv_w', 'ffn_conv_w')
WEIGHTS = ('w_in', 'conv_w', 'conv_b', 'conv_ln_g', 'conv_ln_b', 'w_out', 'ln1_g', 'ln1_b',
           'mem_wq', 'mem_wk', 'mem_wv', 'mem_wo', 'ln2_g', 'ln2_b', 'ffn_up', 'ffn_conv_w',
           'ffn_conv_b', 'ffn_down', 'ln3_g', 'ln3_b')


def _params(block_bytes, semantics=None, **kw):
    limit = int(min(max(2 * block_bytes + (8 << 20), 32 << 20), VMEM_CAP))
    return pltpu.CompilerParams(dimension_semantics=semantics, vmem_limit_bytes=limit, **kw)


def _pallas(body, **kw):
    call = pl.pallas_call(body, **kw)

    def run(*args):
        return call(*[pltpu.with_memory_space_constraint(a, pltpu.HBM)
                      if jnp.issubdtype(a.dtype, jnp.floating) else a for a in args])

    return run


def _nbytes(shape, dtype):
    n = 1
    for s in shape:
        n *= s
    return n * jnp.dtype(dtype).itemsize


def _dot(a, b):
    return jnp.dot(a, b, preferred_element_type=F32)


def _dot_nt(a, b):
    return lax.dot_general(a, b, (((1,), (1,)), ((), ())), preferred_element_type=F32)


def _dot_tn(a, b):
    return lax.dot_general(a, b, (((0,), (0,)), ((), ())), preferred_element_type=F32)


def _sigmoid(x):
    return 1.0 / (1.0 + jnp.exp(-x))


def mm_nn(a, b, out_dtype, *, tm, tn, name):
    M, K = a.shape
    sharded = b.ndim == 3
    if sharded:
        nsh, _, ns = b.shape
        N, per = nsh * ns, ns // tn
        b_spec = pl.BlockSpec((None, K, tn), lambda i, j: (j // per, 0, j % per))
    else:
        N = b.shape[1]
        b_spec = pl.BlockSpec((K, tn), lambda i, j: (0, j))

    def body(a_ref, b_ref, o_ref):
        o_ref[...] = _dot(a_ref[...].astype(BF16), b_ref[...]).astype(o_ref.dtype)

    blk = _nbytes((tm, K), a.dtype) + _nbytes((K, tn), BF16) + _nbytes((tm, tn), out_dtype)
    return _pallas(
        body, name=name, out_shape=pltpu.HBM((M, N), out_dtype), grid=(M // tm, N // tn),
        in_specs=[pl.BlockSpec((tm, K), lambda i, j: (i, 0)), b_spec],
        out_specs=pl.BlockSpec((tm, tn), lambda i, j: (i, j)),
        compiler_params=_params(blk, ("parallel", "parallel")))(a, b)


def mm_pair(a, b1, b2, *, name):
    M, K = a.shape
    N = b1.shape[1]

    def body(a_ref, b1_ref, b2_ref, o1_ref, o2_ref):
        av = a_ref[...]
        o1_ref[...] = _dot(av, b1_ref[...]).astype(BF16)
        o2_ref[...] = _dot(av, b2_ref[...]).astype(BF16)

    whole = lambda r, c: pl.BlockSpec((r, c), lambda i: (0, 0))
    return _pallas(
        body, name=name, grid=(1,), out_shape=(pltpu.HBM((M, N), BF16), pltpu.HBM((M, N), BF16)),
        in_specs=[whole(M, K), whole(K, N), whole(K, N)], out_specs=(whole(M, N), whole(M, N)),
        compiler_params=_params(3 * _nbytes((K, N), BF16), ("arbitrary",)))(a, b1, b2)


def mm_tn_pair(a, b1, b2, *, name):
    M, K = a.shape
    N = b1.shape[1]

    def body(a_ref, b1_ref, b2_ref, o1_ref, o2_ref):
        av = a_ref[...]
        o1_ref[...] = _dot_tn(av, b1_ref[...].astype(BF16)).astype(BF16)
        o2_ref[...] = _dot_tn(av, b2_ref[...].astype(BF16)).astype(BF16)

    whole = lambda r, c: pl.BlockSpec((r, c), lambda i: (0, 0))
    return _pallas(
        body, name=name, grid=(1,), out_shape=(pltpu.HBM((K, N), BF16), pltpu.HBM((K, N), BF16)),
        in_specs=[whole(M, K), whole(M, N), whole(M, N)], out_specs=(whole(K, N), whole(K, N)),
        compiler_params=_params(4 * _nbytes((K, N), BF16), ("arbitrary",)))(a, b1, b2)


def proj_split(a, b, n_f32, *, tm, name):
    M, K = a.shape
    nsh, _, ns = b.shape
    N = nsh * ns

    def body(a_ref, b_ref, lo_ref, hi_ref):
        av = a_ref[...]
        for s in range(nsh):
            acc = _dot(av, b_ref[s])
            c0, c1 = s * ns, (s + 1) * ns
            cut = min(max(n_f32 - c0, 0), ns)
            if cut > 0:
                lo_ref[:, c0:c0 + cut] = acc[:, 0:cut]
            if cut < ns:
                hi_ref[:, c0 + cut - n_f32:c1 - n_f32] = acc[:, cut:ns].astype(BF16)

    blk = _nbytes((tm, K), BF16) + _nbytes((K, N), BF16) + _nbytes((tm, N), F32)
    return _pallas(
        body, name=name, grid=(M // tm,),
        out_shape=(pltpu.HBM((M, n_f32), F32), pltpu.HBM((M, N - n_f32), BF16)),
        in_specs=[pl.BlockSpec((tm, K), lambda i: (i, 0)), pl.BlockSpec((nsh, K, ns), lambda i: (0, 0, 0))],
        out_specs=(pl.BlockSpec((tm, n_f32), lambda i: (i, 0)), pl.BlockSpec((tm, N - n_f32), lambda i: (i, 0))),
        compiler_params=_params(blk, ("parallel",)))(a, b)


def mm_ln(a, b, x, gamma, beta, alpha, *, tm, name):
    M, K = a.shape
    D = b.shape[1]

    def body(a_ref, b_ref, x_ref, g_ref, be_ref, y_ref, yb_ref, zh_ref, rs_ref):
        z = alpha * x_ref[...] + _dot(a_ref[...], b_ref[...])
        mu = jnp.mean(z, axis=-1, keepdims=True)
        zc = z - mu
        rstd = lax.rsqrt(jnp.mean(zc * zc, axis=-1, keepdims=True) + LN_EPS)
        zh = zc * rstd
        y = zh * g_ref[...] + be_ref[...]
        y_ref[...] = y
        yb_ref[...] = y.astype(BF16)
        zh_ref[...] = zh
        rs_ref[...] = rstd

    row = lambda i: (i, 0)
    fix = lambda i: (0, 0)
    blk = _nbytes((tm, K), BF16) + _nbytes((K, D), BF16) + 4 * _nbytes((tm, D), F32)
    return _pallas(
        body, name=name, grid=(M // tm,),
        out_shape=(pltpu.HBM((M, D), F32), pltpu.HBM((M, D), BF16),
                   pltpu.HBM((M, D), F32), pltpu.HBM((M, 1), F32)),
        in_specs=[pl.BlockSpec((tm, K), row), pl.BlockSpec((K, D), fix), pl.BlockSpec((tm, D), row),
                  pl.BlockSpec((1, D), fix), pl.BlockSpec((1, D), fix)],
        out_specs=(pl.BlockSpec((tm, D), row), pl.BlockSpec((tm, D), row), pl.BlockSpec((tm, D), row),
                   pl.BlockSpec((tm, 1), row)),
        compiler_params=_params(blk, ("parallel",)))(a, b, x, gamma, beta)


def ln_bwd(dy, zh, rstd, gamma, *, tm, name):
    M, D = dy.shape

    def body(dy_ref, zh_ref, rs_ref, g_ref, dz_ref, dzb_ref, dg_ref, db_ref):
        @pl.when(pl.program_id(0) == 0)
        def _():
            dg_ref[...] = jnp.zeros_like(dg_ref)
            db_ref[...] = jnp.zeros_like(db_ref)

        dyv, zhv = dy_ref[...], zh_ref[...]
        dg_ref[...] += jnp.sum(dyv * zhv, axis=0, keepdims=True)
        db_ref[...] += jnp.sum(dyv, axis=0, keepdims=True)
        dzh = dyv * g_ref[...]
        m1 = jnp.mean(dzh, axis=-1, keepdims=True)
        m2 = jnp.mean(dzh * zhv, axis=-1, keepdims=True)
        dz = rs_ref[...] * (dzh - m1 - zhv * m2)
        dz_ref[...] = dz
        dzb_ref[...] = dz.astype(BF16)

    row = lambda i: (i, 0)
    fix = lambda i: (0, 0)
    return _pallas(
        body, name=name, grid=(M // tm,),
        out_shape=(pltpu.HBM((M, D), F32), pltpu.HBM((M, D), BF16),
                   pltpu.HBM((1, D), F32), pltpu.HBM((1, D), F32)),
        in_specs=[pl.BlockSpec((tm, D), row), pl.BlockSpec((tm, D), row), pl.BlockSpec((tm, 1), row),
                  pl.BlockSpec((1, D), fix)],
        out_specs=(pl.BlockSpec((tm, D), row), pl.BlockSpec((tm, D), row), pl.BlockSpec((1, D), fix),
                   pl.BlockSpec((1, D), fix)),
        compiler_params=_params(4 * _nbytes((tm, D), F32), ("arbitrary",)))(dy, zh, rstd, gamma)


def mm_nt(a_list, b, out_dtype, *, tm, tk, name, res=None, alpha=None, rider=None):
    M = a_list[0].shape[0]
    widths = [a.shape[1] for a in a_list]
    sharded = b.ndim == 3
    if sharded:
        nsh, K, ns = b.shape
        b_spec = pl.BlockSpec((nsh, tk, ns), lambda i, j: (0, j, 0))
        for w in widths:
            assert w % ns == 0
    else:
        K, N = b.shape
        ns = None
        b_spec = pl.BlockSpec((tk, N), lambda i, j: (j, 0))
    n_a = len(a_list)

    def body(*refs):
        a_refs, b_ref = refs[:n_a], refs[n_a]
        o_ref = refs[-1]
        acc = None
        off = 0
        for a_ref, w in zip(a_refs, widths):
            if sharded:
                for p in range(w // ns):
                    t = _dot_nt(a_ref[:, p * ns:(p + 1) * ns].astype(BF16), b_ref[off // ns + p])
                    acc = t if acc is None else acc + t
            else:
                t = _dot_nt(a_ref[...].astype(BF16), b_ref[:, off:off + w])
                acc = t if acc is None else acc + t
            off += w
        if res is not None:
            acc = acc + alpha * refs[n_a + 1][...]
        o_ref[...] = acc.astype(o_ref.dtype)

    in_specs = [pl.BlockSpec((tm, w), lambda i, j: (i, 0)) for w in widths] + [b_spec]
    args = list(a_list) + [b]
    if res is not None:
        in_specs.append(pl.BlockSpec((tm, tk), lambda i, j: (i, j)))
        args.append(res)
    blk = (sum(_nbytes((tm, w), a.dtype) for a, w in zip(a_list, widths)) + _nbytes((tk, sum(widths)), BF16)
           + 2 * _nbytes((tm, tk), F32))
    in_specs, out_specs, out_shape, scratch = _carry_specs(
        rider, in_specs, (pl.BlockSpec((tm, tk), lambda i, j: (i, j)),), (pltpu.HBM((M, K), out_dtype),), [])
    first = lambda: (pl.program_id(0) == 0) & (pl.program_id(1) == 0)
    last = lambda: (pl.program_id(0) == M // tm - 1) & (pl.program_id(1) == K // tk - 1)
    res_all = _pallas(
        _carry(rider, body, len(args), 1, first, last), name=name, out_shape=out_shape, grid=(M // tm, K // tk),
        in_specs=in_specs, out_specs=out_specs, scratch_shapes=scratch,
        compiler_params=_params(blk, ("arbitrary", "arbitrary")))(*args, *(rider.arrays if rider else ()))
    return res_all[0] if rider is None else (res_all[0], list(res_all[1:]))


def mm_nt_ln_bwd(a_list, b, res, alpha, zh, rstd, gamma, *, tm, name):
    M, D = res.shape
    widths = [a.shape[1] for a in a_list]
    sharded = b.ndim == 3
    if sharded:
        nsh, _, ns = b.shape
        b_spec = pl.BlockSpec((nsh, D, ns), lambda i: (0, 0, 0))
    else:
        ns = None
        b_spec = pl.BlockSpec((D, b.shape[1]), lambda i: (0, 0))
    n_a = len(a_list)

    def body(*refs):
        a_refs, b_ref = refs[:n_a], refs[n_a]
        res_ref, zh_ref, rs_ref, g_ref = refs[n_a + 1:n_a + 5]
        dz_ref, dzb_ref, dg_ref, db_ref = refs[n_a + 5:]

        @pl.when(pl.program_id(0) == 0)
        def _():
            dg_ref[...] = jnp.zeros_like(dg_ref)
            db_ref[...] = jnp.zeros_like(db_ref)

        dy = alpha * res_ref[...]
        off = 0
        for a_ref, w in zip(a_refs, widths):
            if sharded:
                for p in range(w // ns):
                    dy = dy + _dot_nt(a_ref[:, p * ns:(p + 1) * ns], b_ref[off // ns + p])
            else:
                dy = dy + _dot_nt(a_ref[...], b_ref[:, off:off + w])
            off += w
        zhv = zh_ref[...]
        dg_ref[...] += jnp.sum(dy * zhv, axis=0, keepdims=True)
        db_ref[...] += jnp.sum(dy, axis=0, keepdims=True)
        dzh = dy * g_ref[...]
        m1 = jnp.mean(dzh, axis=-1, keepdims=True)
        m2 = jnp.mean(dzh * zhv, axis=-1, keepdims=True)
        dz = rs_ref[...] * (dzh - m1 - zhv * m2)
        dz_ref[...] = dz
        dzb_ref[...] = dz.astype(BF16)

    row = lambda i: (i, 0)
    fix = lambda i: (0, 0)
    in_specs = [pl.BlockSpec((tm, w), row) for w in widths] + [
        b_spec, pl.BlockSpec((tm, D), row), pl.BlockSpec((tm, D), row), pl.BlockSpec((tm, 1), row),
        pl.BlockSpec((1, D), fix)]
    blk = (sum(_nbytes((tm, w), BF16) for w in widths) + _nbytes((D, sum(widths)), BF16)
           + 5 * _nbytes((tm, D), F32))
    return _pallas(
        body, name=name, grid=(M // tm,),
        out_shape=(pltpu.HBM((M, D), F32), pltpu.HBM((M, D), BF16), pltpu.HBM((1, D), F32),
                   pltpu.HBM((1, D), F32)),
        in_specs=in_specs,
        out_specs=(pl.BlockSpec((tm, D), row), pl.BlockSpec((tm, D), row), pl.BlockSpec((1, D), fix),
                   pl.BlockSpec((1, D), fix)),
        compiler_params=_params(blk, ("arbitrary",)))(*a_list, b, res, zh, rstd, gamma)


def mm_tn(a, b_list, *, tk, tn, name, shard_width=None, tmc=None):
    M, K = a.shape
    tmc = M if tmc is None else tmc
    nm = M // tmc
    widths = [b.shape[1] for b in b_list]
    N = sum(widths)
    starts, s = [], 0
    for w in widths:
        assert w % tn == 0
        starts.append(s)
        s += w // tn
    n_b = len(b_list)

    def body(*refs):
        a_ref, b_refs, o_ref, acc = refs[0], refs[1:1 + n_b], refs[-2], refs[-1]
        j, m = pl.program_id(1), pl.program_id(2)
        for b_ref, st, w in zip(b_refs, starts, widths):
            @pl.when((j >= st) & (j < st + w // tn))
            def _(b_ref=b_ref):
                t = _dot_tn(a_ref[...].astype(BF16), b_ref[...].astype(BF16))
                if nm == 1:
                    o_ref[...] = t.astype(o_ref.dtype)
                else:
                    @pl.when(m == 0)
                    def _():
                        acc[...] = t

                    @pl.when(m > 0)
                    def _():
                        acc[...] += t

                    @pl.when(m == nm - 1)
                    def _():
                        o_ref[...] = acc[...].astype(o_ref.dtype)

    def b_map(st, w):
        nb = w // tn
        return lambda i, j, m: (jnp.where((j >= st) & (j < st + nb), m, 0), jnp.clip(j - st, 0, nb - 1))

    in_specs = [pl.BlockSpec((tmc, tk), lambda i, j, m: (m, i))]
    in_specs += [pl.BlockSpec((tmc, tn), b_map(st, w)) for st, w in zip(starts, widths)]
    if shard_width is None:
        out_shape = pltpu.HBM((K, N), BF16)
        out_spec = pl.BlockSpec((tk, tn), lambda i, j, m: (i, j))
    else:
        per = shard_width // tn
        out_shape = pltpu.HBM((N // shard_width, K, shard_width), BF16)
        out_spec = pl.BlockSpec((None, tk, tn), lambda i, j, m: (j // per, i, j % per))
    acc_shape = (tk, tn) if nm > 1 else (8, LANES)
    blk = (_nbytes((tmc, tk), a.dtype) + n_b * _nbytes((tmc, tn), b_list[0].dtype) + 2 * _nbytes((tk, tn), F32))
    return _pallas(
        body, name=name, out_shape=out_shape, grid=(K // tk, N // tn, nm), in_specs=in_specs, out_specs=out_spec,
        scratch_shapes=[pltpu.VMEM(acc_shape, F32)],
        compiler_params=_params(blk, ("parallel", "arbitrary", "arbitrary")))(a, *b_list)


CONV_PAD = 32
CONV_CHUNK = 128


def _rows(win, off, n, shifts):
    b, a = off % 8, off // 8
    if b not in shifts:
        shifts[b] = win if b == 0 else win[b:b + n + CONV_PAD - 8, :]
    return shifts[b][8 * a:8 * a + n, :]


def _by_residue(n_taps, offset):
    return sorted(range(n_taps), key=lambda k: (offset(k) % 8, k))


def conv_fwd(proj, conv_w, conv_b, *, name, rider=None):
    S = proj.shape[0]
    KW, C = conv_w.shape
    nct = C // LANES
    rc = min(CONV_CHUNK, S)

    def body(a_ref, g_ref, w_ref, b_ref, o_ref, pad):
        pad[0:CONV_PAD, :] = jnp.zeros((CONV_PAD, LANES), F32)
        pad[CONV_PAD:, :] = a_ref[...] * _sigmoid(g_ref[...])
        w = w_ref[...]
        bias = b_ref[...]

        def chunk(i, _):
            base = pl.multiple_of(i * rc, rc)
            win = pad[pl.ds(base, rc + CONV_PAD), :]
            acc = jnp.zeros((rc, LANES), F32) + bias
            shifts = {}
            for k in _by_residue(KW, lambda k: CONV_PAD - (KW - 1) + k):
                acc = acc + w[k:k + 1, :] * _rows(win, CONV_PAD - (KW - 1) + k, rc, shifts)
            o_ref[pl.ds(base, rc), :] = acc
            return 0

        lax.fori_loop(0, S // rc, chunk, 0)

    in_specs, out_specs, out_shape, scratch = _carry_specs(
        rider, [pl.BlockSpec((S, LANES), lambda c: (0, c)), pl.BlockSpec((S, LANES), lambda c: (0, c + nct)),
                pl.BlockSpec((KW, LANES), lambda c: (0, c)), pl.BlockSpec((1, LANES), lambda c: (0, c))],
        (pl.BlockSpec((S, LANES), lambda c: (0, c)),), (pltpu.HBM((S, C), F32),),
        [pltpu.VMEM((S + CONV_PAD, LANES), F32)])
    first = lambda: pl.program_id(0) == 0
    last = lambda: pl.program_id(0) == nct - 1
    res = _pallas(
        _carry(rider, body, 4, 1, first, last), name=name, grid=(nct,), out_shape=out_shape,
        in_specs=in_specs, out_specs=out_specs, scratch_shapes=scratch,
        compiler_params=_params(4 * _nbytes((S, LANES), F32), ("arbitrary",)))(
            proj, proj, conv_w, conv_b, *(rider.arrays if rider else ()))
    return res[0], list(res[1:])


def conv_bwd(du1, proj, conv_w, *, name, rider=None):
    S = proj.shape[0]
    KW, C = conv_w.shape
    nct = C // LANES
    rc = min(CONV_CHUNK, S)

    def body(d_ref, a_ref, g_ref, w_ref, da_ref, dg_ref, dw_ref, db_ref, pad_u, pad_d, du0, dw_acc):
        dw_acc[...] = jnp.zeros_like(dw_acc)
        pad_u[0:CONV_PAD, :] = jnp.zeros((CONV_PAD, LANES), F32)
        pad_u[CONV_PAD:, :] = a_ref[...] * _sigmoid(g_ref[...])
        pad_d[0:S, :] = d_ref[...]
        pad_d[S:, :] = jnp.zeros((CONV_PAD, LANES), F32)
        w = w_ref[...]
        db_ref[...] = jnp.sum(d_ref[...], axis=0, keepdims=True)

        def chunk(i, _):
            base = pl.multiple_of(i * rc, rc)
            d = pad_d[pl.ds(base, rc), :]
            win_u = pad_u[pl.ds(base, rc + CONV_PAD), :]
            win_d = pad_d[pl.ds(base, rc + CONV_PAD), :]
            shifts = {}
            for k in _by_residue(KW, lambda k: CONV_PAD - (KW - 1) + k):
                u_k = _rows(win_u, CONV_PAD - (KW - 1) + k, rc, shifts)
                dw_acc[k:k + 1, :] += jnp.sum(d * u_k, axis=0, keepdims=True)
            acc = jnp.zeros((rc, LANES), F32)
            shifts = {}
            for k in _by_residue(KW, lambda k: KW - 1 - k):
                acc = acc + w[k:k + 1, :] * _rows(win_d, KW - 1 - k, rc, shifts)
            du0[pl.ds(base, rc), :] = acc
            return 0

        lax.fori_loop(0, S // rc, chunk, 0)
        dw_ref[...] = dw_acc[0:KW, :]
        a, sg = a_ref[...], _sigmoid(g_ref[...])
        d0 = du0[...]
        da_ref[...] = (d0 * sg).astype(BF16)
        dg_ref[...] = (d0 * a * sg * (1.0 - sg)).astype(BF16)

    col = lambda c: (0, c)
    in_specs, out_specs, out_shape, scratch = _carry_specs(
        rider, [pl.BlockSpec((S, LANES), col), pl.BlockSpec((S, LANES), col),
                pl.BlockSpec((S, LANES), lambda c: (0, c + nct)), pl.BlockSpec((KW, LANES), col)],
        (pl.BlockSpec((S, LANES), col), pl.BlockSpec((S, LANES), col), pl.BlockSpec((KW, LANES), col),
         pl.BlockSpec((1, LANES), col)),
        (pltpu.HBM((S, C), BF16), pltpu.HBM((S, C), BF16), pltpu.HBM((KW, C), F32), pltpu.HBM((1, C), F32)),
        [pltpu.VMEM((S + CONV_PAD, LANES), F32), pltpu.VMEM((S + CONV_PAD, LANES), F32),
         pltpu.VMEM((S, LANES), F32), pltpu.VMEM((CONV_PAD, LANES), F32)])
    first = lambda: pl.program_id(0) == 0
    last = lambda: pl.program_id(0) == nct - 1
    res = _pallas(
        _carry(rider, body, 4, 4, first, last), name=name, grid=(nct,), out_shape=out_shape,
        in_specs=in_specs, out_specs=out_specs, scratch_shapes=scratch,
        compiler_params=_params(8 * _nbytes((S, LANES), F32), ("arbitrary",)))(
            du1, proj, proj, conv_w, *(rider.arrays if rider else ()))
    return res[:4], list(res[4:])


def ln_silu(u1, o_sb, gamma, beta, *, tm, name):
    S, C = u1.shape

    def body(u_ref, o_ref, g_ref, b_ref, out_ref):
        z = u_ref[...]
        mu = jnp.mean(z, axis=-1, keepdims=True)
        zc = z - mu
        y = zc * lax.rsqrt(jnp.mean(zc * zc, axis=-1, keepdims=True) + LN_EPS) * g_ref[...] + b_ref[...]
        out_ref[:, 0:C] = (y * _sigmoid(y)).astype(BF16)
        out_ref[:, C:] = o_ref[...].astype(BF16)

    row = lambda i: (i, 0)
    fix = lambda i: (0, 0)
    return _pallas(
        body, name=name, out_shape=pltpu.HBM((S, 2 * C), BF16), grid=(S // tm,),
        in_specs=[pl.BlockSpec((tm, C), row), pl.BlockSpec((tm, C), row), pl.BlockSpec((1, C), fix),
                  pl.BlockSpec((1, C), fix)],
        out_specs=pl.BlockSpec((tm, 2 * C), row),
        compiler_params=_params(4 * _nbytes((tm, C), F32), ("parallel",)))(u1, o_sb, gamma, beta)


def ln_silu_bwd(dua, u1, gamma, beta, *, tm, name):
    S, C = u1.shape

    def body(d_ref, u_ref, g_ref, b_ref, du1_ref, dg_ref, db_ref):
        @pl.when(pl.program_id(0) == 0)
        def _():
            dg_ref[...] = jnp.zeros_like(dg_ref)
            db_ref[...] = jnp.zeros_like(db_ref)

        z = u_ref[...]
        mu = jnp.mean(z, axis=-1, keepdims=True)
        zc = z - mu
        rstd = lax.rsqrt(jnp.mean(zc * zc, axis=-1, keepdims=True) + LN_EPS)
        zh = zc * rstd
        y = zh * g_ref[...] + b_ref[...]
        sg = _sigmoid(y)
        dy = d_ref[...] * (sg * (1.0 + y * (1.0 - sg)))
        dg_ref[...] += jnp.sum(dy * zh, axis=0, keepdims=True)
        db_ref[...] += jnp.sum(dy, axis=0, keepdims=True)
        dzh = dy * g_ref[...]
        m1 = jnp.mean(dzh, axis=-1, keepdims=True)
        m2 = jnp.mean(dzh * zh, axis=-1, keepdims=True)
        du1_ref[...] = rstd * (dzh - m1 - zh * m2)

    row = lambda i: (i, 0)
    fix = lambda i: (0, 0)
    return _pallas(
        body, name=name, grid=(S // tm,),
        out_shape=(pltpu.HBM((S, C), F32), pltpu.HBM((1, C), F32),
                   pltpu.HBM((1, C), F32)),
        in_specs=[pl.BlockSpec((tm, C), row), pl.BlockSpec((tm, C), row), pl.BlockSpec((1, C), fix),
                  pl.BlockSpec((1, C), fix)],
        out_specs=(pl.BlockSpec((tm, C), row), pl.BlockSpec((1, C), fix), pl.BlockSpec((1, C), fix)),
        compiler_params=_params(4 * _nbytes((tm, C), F32), ("arbitrary",)))(dua, u1, gamma, beta)


SB_BLOCK = 256
SB_STOP = -105.0
SB_GROUP = 4


def _split_dot(x, tri):
    hi = x.astype(BF16)
    lo = (x - hi.astype(F32)).astype(BF16)
    return _dot(hi, tri) + _dot(lo, tri)


def _neg_softplus(z):
    return -(jnp.maximum(z, 0.0) + jnp.log(1.0 + jnp.exp(-jnp.abs(z))))


def sb_fwd(proj, *, q_col, name, rider=None):
    S = proj.shape[0]
    dh = LANES // 2
    W = SB_HEADS * dh
    BW = SB_GROUP * dh
    ngrp = W // BW
    T = min(SB_BLOCK, S)
    nblk = S // T
    scale = dh ** -0.5
    qb0 = q_col // BW
    heads = range(SB_GROUP)
    sl = [slice(h * dh, (h + 1) * dh) for h in heads]

    def body(q_ref, k_ref, v_ref, o_ref, l_ref, qs):
        r_i = lax.broadcasted_iota(jnp.int32, (T, T), 0)
        c_i = lax.broadcasted_iota(jnp.int32, (T, T), 1)
        tri = (r_i >= c_i).astype(BF16)
        vis = c_i < r_i
        lane = lax.broadcasted_iota(jnp.int32, (T, dh), 1)

        qs[...] = (q_ref[...] * scale).astype(BF16)

        def step(qb, blocks, st):
            nb = range(len(blocks))
            kb = [[k_ref[pl.ds(j0, T), sl[h]].astype(BF16) for h in heads] for j0, _ in blocks]
            vb = [[v_ref[pl.ds(j0, T), sl[h]].astype(BF16) for h in heads] for j0, _ in blocks]
            z = [[_dot_nt(qb[h], kb[b][h]) for h in heads] for b in nb]
            lk = [[_neg_softplus(z[b][h]) for h in heads] for b in nb]
            lk = [[jnp.where(vis, lk[b][h], 0.0) if blocks[b][1] else lk[b][h] for h in heads] for b in nb]
            C = [[_split_dot(lk[b][h], tri) for h in heads] for b in nb]
            R = [[st[2 * h + 1] for h in heads]]
            for b in nb:
                R.append([R[b][h] + C[b][h][:, 0:1] for h in heads])
            A = [[jnp.exp(z[b][h] + C[b][h] + R[b][h]) for h in heads] for b in nb]
            A = [[jnp.where(vis, A[b][h], 0.0) if blocks[b][1] else A[b][h] for h in heads] for b in nb]
            out = ()
            for h in heads:
                acc = st[2 * h]
                for b in nb:
                    acc = acc + _dot(A[b][h].astype(BF16), vb[b][h])
                out += (acc, R[-1][h])
            return out

        zero = (jnp.zeros((T, dh), F32), jnp.zeros((T, 1), F32))

        def finish(r0, i, c):
            walked = jnp.asarray(i - c[0]).astype(F32)
            for h in heads:
                o_ref[pl.ds(r0, T), sl[h]] = c[1 + 2 * h]
                l_ref[pl.ds(r0, T), sl[h]] = jnp.where(lane == 1, walked, c[2 + 2 * h])

        finish(0, 0, (-1,) + step([qs[0:T, sl[h]] for h in heads], [(0, True)], zero * SB_GROUP))

        def qblock(i, _):
            r0 = pl.multiple_of(i * T, T)
            qb = [qs[pl.ds(r0, T), sl[h]] for h in heads]
            state = step(qb, [(r0, True), (pl.multiple_of(r0 - T, T), False)], zero * SB_GROUP)

            def more(c):
                worst = c[2]
                for h in heads[1:]:
                    worst = jnp.maximum(worst, c[2 + 2 * h])
                return (c[0] >= 0) & (jnp.max(worst) >= SB_STOP)

            def walk(c):
                return (c[0] - 1,) + step(qb, [(pl.multiple_of(c[0] * T, T), False)], c[1:])

            finish(r0, i, lax.while_loop(more, walk, (i - 2,) + state))
            return 0

        lax.fori_loop(1, nblk, qblock, 0)

    blk = lambda off: pl.BlockSpec((S, BW), lambda g: (0, qb0 + off * ngrp + g), pipeline_mode=pl.Buffered(1))
    out = pl.BlockSpec((S, BW), lambda g: (0, g))
    in_specs, out_specs, out_shape, scratch = _carry_specs(
        rider, [blk(0), blk(1), blk(2)], (out, out), (pltpu.HBM((S, W), F32), pltpu.HBM((S, W), F32)),
        [pltpu.VMEM((S, BW), BF16)])
    first = lambda: pl.program_id(0) == 0
    last = lambda: pl.program_id(0) == ngrp - 1
    res = _pallas(
        _carry(rider, body, 3, 2, first, last), name=name, grid=(ngrp,), out_shape=out_shape,
        in_specs=in_specs, out_specs=out_specs, scratch_shapes=scratch,
        compiler_params=_params(5 * _nbytes((S, BW), F32), ("arbitrary",)))(
            proj, proj, proj, *(rider.arrays if rider else ()))
    return res[0], res[1], list(res[2:])


def sb_bwd(proj, ltot, dua, *, q_col, do_col, name, rider=None):
    S = proj.shape[0]
    dh = LANES // 2
    W = SB_HEADS * dh
    BW = SB_GROUP * dh
    ngrp = W // BW
    T = min(SB_BLOCK, S)
    nblk = S // T
    scale = dh ** -0.5
    qb0 = q_col // BW
    db0 = do_col // BW
    heads = range(SB_GROUP)
    sl = [slice(h * dh, (h + 1) * dh) for h in heads]

    def body(q_ref, k_ref, v_ref, l_ref, do_ref, dq_ref, dk_ref, dv_ref, dks, dvs):
        r_i = lax.broadcasted_iota(jnp.int32, (T, T), 0)
        c_i = lax.broadcasted_iota(jnp.int32, (T, T), 1)
        tri_rev = (r_i >= c_i).astype(BF16)
        tri_fwd = (r_i <= c_i).astype(BF16)
        vis = c_i < r_i

        dks[...] = jnp.zeros_like(dks)
        dvs[...] = jnp.zeros_like(dvs)

        def step(qb, dob, Lt, blocks, st):
            nb = range(len(blocks))
            kb = [[k_ref[pl.ds(j0, T), sl[h]].astype(BF16) for h in heads] for j0, _ in blocks]
            vb = [[v_ref[pl.ds(j0, T), sl[h]].astype(BF16) for h in heads] for j0, _ in blocks]
            z = [[_dot_nt(qb[h], kb[b][h]) for h in heads] for b in nb]
            dA =[[_dot_nt(dob[h], vb[b][h]) for h in heads] for b in nb]
            lk = [[_neg_softplus(z[b][h]) for h in heads] for b in nb]
            beta = [[jnp.exp(z[b][h] + lk[b][h]) for h in heads] for b in nb]
            lk = [[jnp.where(vis, lk[b][h], 0.0) if blocks[b][1] else lk[b][h] for h in heads] for b in nb]
            C = [[_split_dot(lk[b][h], tri_rev) for h in heads] for b in nb]
            P = [[st[3 * h + 1] for h in heads]]
            for b in nb:
                P.append([P[b][h] + C[b][h][:, 0:1] for h in heads])
            A = [[jnp.exp(z[b][h] + C[b][h] + (Lt[h] - P[b + 1][h])) for h in heads] for b in nb]
            A = [[jnp.where(vis, A[b][h], 0.0) if blocks[b][1] else A[b][h] for h in heads] for b in nb]
            g = [[A[b][h] * dA[b][h] for h in heads] for b in nb]
            Gin = [[_split_dot(g[b][h], tri_fwd) for h in heads] for b in nb]
            Gp = [[st[3 * h + 2] for h in heads]]
            for b in nb:
                Gp.append([Gp[b][h] + Gin[b][h][:, T - 1:T] for h in heads])
            dz = [[g[b][h] - beta[b][h] * (Gp[b][h] + Gin[b][h]) for h in heads] for b in nb]
            dz = [[jnp.where(vis, dz[b][h], 0.0) if blocks[b][1] else dz[b][h] for h in heads] for b in nb]
            dzb = [[dz[b][h].astype(BF16) for h in heads] for b in nb]
            out = ()
            for h in heads:
                dq = st[3 * h]
                for b in nb:
                    j0 = blocks[b][0]
                    dvs[pl.ds(j0, T), sl[h]] += _dot_tn(A[b][h].astype(BF16), dob[h])
                    dks[pl.ds(j0, T), sl[h]] += _dot_tn(dzb[b][h], qb[h])
                    dq = dq + _dot(dzb[b][h], kb[b][h])
                out += (dq, P[-1][h], Gp[-1][h])
            return out

        zero = jnp.zeros((T, 1), F32)
        init = (jnp.zeros((T, dh), F32), zero, zero)

        def operands(r0):
            return ([(q_ref[pl.ds(r0, T), sl[h]] * scale).astype(BF16) for h in heads],
                    [do_ref[pl.ds(r0, T), sl[h]].astype(BF16) for h in heads],
                    [l_ref[pl.ds(r0, T), h * dh:h * dh + 1] for h in heads])

        def finish(r0, c):
            for h in heads:
                dq_ref[pl.ds(r0, T), sl[h]] = (c[3 * h] * scale).astype(BF16)

        finish(0, step(*operands(0), [(0, True)], init * SB_GROUP))

        def qblock(i, _):
            r0 = pl.multiple_of(i * T, T)
            qb, dob, Lt = operands(r0)
            walked = jnp.clip(jnp.max(l_ref[pl.ds(r0, 8), 1:2]).astype(jnp.int32), 2, i + 1)

            def inner(j, c):
                return step(qb, dob, Lt, [(pl.multiple_of(j * T, T), False)], c)

            c = lax.fori_loop(i + 1 - walked, i - 1, inner, init * SB_GROUP)
            finish(r0, step(qb, dob, Lt, [(pl.multiple_of(r0 - T, T), False), (r0, True)], c))
            return 0

        lax.fori_loop(1, nblk, qblock, 0)
        dk_ref[...] = dks[...].astype(BF16)
        dv_ref[...] = dvs[...].astype(BF16)

    once = pl.Buffered(1)
    blk = lambda off: pl.BlockSpec((S, BW), lambda g: (0, qb0 + off * ngrp + g), pipeline_mode=once)
    out = pl.BlockSpec((S, BW), lambda g: (0, g))
    o_shape = pltpu.HBM((S, W), BF16)
    in_specs, out_specs, out_shape, scratch = _carry_specs(
        rider, [blk(0), blk(1), blk(2), pl.BlockSpec((S, BW), lambda g: (0, g), pipeline_mode=once),
                pl.BlockSpec((S, BW), lambda g: (0, db0 + g), pipeline_mode=once)], (out, out, out),
        (o_shape, o_shape, o_shape), [pltpu.VMEM((S, BW), F32)] * 2)
    first = lambda: pl.program_id(0) == 0
    last = lambda: pl.program_id(0) == ngrp - 1
    res = _pallas(
        _carry(rider, body, 5, 3, first, last), name=name, grid=(ngrp,), out_shape=out_shape,
        in_specs=in_specs, out_specs=out_specs, scratch_shapes=scratch,
        compiler_params=_params(6 * _nbytes((S, BW), F32), ("arbitrary",)))(
            proj, proj, proj, ltot, dua, *(rider.arrays if rider else ()))
    return res[0], res[1], res[2], list(res[3:])


def xattn_fwd(q, k, v, *, tm, name):
    S, D = q.shape
    Mlen = k.shape[0]
    hd = D // MEM_HEADS
    scale = hd ** -0.5

    def body(q_ref, k_ref, v_ref, o_ref):
        for h in range(MEM_HEADS):
            sl = slice(h * hd, (h + 1) * hd)
            s = _dot_nt(q_ref[:, sl], k_ref[:, sl]) * scale
            e = jnp.exp(s - jnp.max(s, axis=-1, keepdims=True))
            p = e / jnp.sum(e, axis=-1, keepdims=True)
            o_ref[:, sl] = _dot(p.astype(BF16), v_ref[:, sl]).astype(BF16)

    row = lambda i: (i, 0)
    fix = lambda i: (0, 0)
    return _pallas(
        body, name=name, out_shape=pltpu.HBM((S, D), BF16), grid=(S // tm,),
        in_specs=[pl.BlockSpec((tm, D), row), pl.BlockSpec((Mlen, D), fix), pl.BlockSpec((Mlen, D), fix)],
        out_specs=pl.BlockSpec((tm, D), row),
        compiler_params=_params(4 * _nbytes((tm, D), F32), ("parallel",)))(q, k, v)


def xattn_bwd(q, do, k, v, *, tm, name):
    S, D = q.shape
    Mlen = k.shape[0]
    hd = D // MEM_HEADS
    scale = hd ** -0.5

    def body(q_ref, do_ref, k_ref, v_ref, dq_ref, dk_ref, dv_ref):
        @pl.when(pl.program_id(0) == 0)
        def _():
            dk_ref[...] = jnp.zeros_like(dk_ref)
            dv_ref[...] = jnp.zeros_like(dv_ref)

        for h in range(MEM_HEADS):
            sl = slice(h * hd, (h + 1) * hd)
            qh, doh, kh, vh = q_ref[:, sl], do_ref[:, sl], k_ref[:, sl], v_ref[:, sl]
            s = _dot_nt(qh, kh) * scale
            e = jnp.exp(s - jnp.max(s, axis=-1, keepdims=True))
            p = e / jnp.sum(e, axis=-1, keepdims=True)
            dp = _dot_nt(doh, vh)
            ds = (p * (dp - jnp.sum(p * dp, axis=-1, keepdims=True)) * scale).astype(BF16)
            dq_ref[:, sl] = _dot(ds, kh).astype(BF16)
            dk_ref[:, sl] += _dot_tn(ds, qh)
            dv_ref[:, sl] += _dot_tn(p.astype(BF16), doh)

    row = lambda i: (i, 0)
    fix = lambda i: (0, 0)
    return _pallas(
        body, name=name, grid=(S // tm,),
        out_shape=(pltpu.HBM((S, D), BF16), pltpu.HBM((Mlen, D), F32),
                   pltpu.HBM((Mlen, D), F32)),
        in_specs=[pl.BlockSpec((tm, D), row), pl.BlockSpec((tm, D), row), pl.BlockSpec((Mlen, D), fix),
                  pl.BlockSpec((Mlen, D), fix)],
        out_specs=(pl.BlockSpec((tm, D), row), pl.BlockSpec((Mlen, D), fix), pl.BlockSpec((Mlen, D), fix)),
        compiler_params=_params(6 * _nbytes((tm, D), F32), ("arbitrary",)))(q, do, k, v)


FFN_HALO = 8


def _conv3(ext, w, lo):
    tm = ext.shape[0] - FFN_HALO
    return (w[0:1, :] * ext[lo:lo + tm, :] + w[1:2, :] * ext[lo + 1:lo + 1 + tm, :]
            + w[2:3, :] * ext[lo + 2:lo + 2 + tm, :])


def ffn_up_fwd(xb, w_up, conv_w, conv_b, *, tm, tn, name, rider=None):
    S, D = xb.shape
    nsh, _, ns = w_up.shape
    F = nsh * ns // 2
    per = ns // tn
    ncol = F // tn
    KW = conv_w.shape[0]
    assert KW == 3

    def body(x_ref, wv_ref, wg_ref, cwv_ref, cwg_ref, cbv_ref, cbg_ref, uv_ref, ug_ref, mv_ref, mg_ref, h_ref,
             carry):
        @pl.when(pl.program_id(1) == 0)
        def _():
            carry[...] = jnp.zeros_like(carry)

        x = x_ref[...]
        uv = _dot(x, wv_ref[...])
        ug = _dot(x, wg_ref[...])
        uv_ref[...] = uv.astype(BF16)
        ug_ref[...] = ug.astype(BF16)
        lo = FFN_HALO - (KW - 1)
        cv = _conv3(jnp.concatenate([carry[0], uv], axis=0), cwv_ref[...], lo) + cbv_ref[...]
        cg = _conv3(jnp.concatenate([carry[1], ug], axis=0), cwg_ref[...], lo) + cbg_ref[...]
        carry[0] = uv[tm - FFN_HALO:, :]
        carry[1] = ug[tm - FFN_HALO:, :]
        sg = _sigmoid(cg)
        act = cg * sg
        mv_ref[...] = act.astype(BF16)
        mg_ref[...] = (cv * (sg + act * (1.0 - sg))).astype(BF16)
        h_ref[...] = (act * cv).astype(BF16)

    wspec = lambda half: pl.BlockSpec((None, D, tn), lambda j, i: (half * (nsh // 2) + j // per, 0, j % per))
    cspec = lambda rows, half: pl.BlockSpec((rows, tn), lambda j, i: (0, half * ncol + j))
    out = pl.BlockSpec((tm, tn), lambda j, i: (i, j))
    o_shape = pltpu.HBM((S, F), BF16)
    blk = _nbytes((tm, D), BF16) + 2 * _nbytes((D, tn), BF16) + 8 * _nbytes((tm, tn), F32)
    nrow = S // tm
    in_specs, out_specs, out_shape, scratch = _carry_specs(
        rider, [pl.BlockSpec((tm, D), lambda j, i: (i, 0)), wspec(0), wspec(1), cspec(KW, 0), cspec(KW, 1),
                cspec(1, 0), cspec(1, 1)], (out,) * 5, (o_shape,) * 5, [pltpu.VMEM((2, FFN_HALO, tn), F32)])
    first = lambda: (pl.program_id(0) == 0) & (pl.program_id(1) == 0)
    last = lambda: (pl.program_id(0) == ncol - 1) & (pl.program_id(1) == nrow - 1)
    res = _pallas(
        _carry(rider, body, 7, 5, first, last), name=name, grid=(ncol, nrow), out_shape=out_shape,
        in_specs=in_specs, out_specs=out_specs, scratch_shapes=scratch,
        compiler_params=_params(blk, ("arbitrary", "arbitrary")))(
            xb, w_up, w_up, conv_w, conv_w, conv_b, conv_b, *(rider.arrays if rider else ()))
    return res[:5], list(res[5:])


def ffn_mid_bwd(dzb, w_down, up_v, up_g, mult_v, mult_g, conv_w, *, tm, tn, name, rider=None):
    S, D = dzb.shape
    F = up_v.shape[1]
    ncol = F // tn
    nrow = S // tm
    KW = conv_w.shape[0]
    assert KW == 3

    def body(dz_ref, wd_ref, uv_ref, ug_ref, mv_ref, mg_ref, cwv_ref, cwg_ref,
             dv_ref, dg_ref, dwv_ref, dwg_ref, dbv_ref, dbg_ref, carry):
        @pl.when(pl.program_id(1) == 0)
        def _():
            carry[...] = jnp.zeros_like(carry)
            for r in (dwv_ref, dwg_ref, dbv_ref, dbg_ref):
                r[...] = jnp.zeros_like(r)

        dh = _dot_nt(dz_ref[...], wd_ref[...])
        dcv = dh * mv_ref[...].astype(F32)
        dcg = dh * mg_ref[...].astype(F32)

        def back(dc, u_ref, cw, slot, du_ref, dw_ref, db_ref):
            ext = jnp.concatenate([dc, carry[slot]], axis=0)
            ahead = [dc, ext[1:tm + 1, :], ext[2:tm + 2, :]]
            du = cw[2:3, :] * ahead[0] + cw[1:2, :] * ahead[1] + cw[0:1, :] * ahead[2]
            du_ref[...] = du.astype(BF16)
            carry[slot] = dc[0:FFN_HALO, :]
            u = u_ref[...].astype(F32)
            for k in range(KW):
                dw_ref[k:k + 1, :] += jnp.sum(ahead[KW - 1 - k] * u, axis=0, keepdims=True)
            db_ref[...] += jnp.sum(dc, axis=0, keepdims=True)

        back(dcv, uv_ref, cwv_ref[...], 0, dv_ref, dwv_ref, dbv_ref)
        back(dcg, ug_ref, cwg_ref[...], 1, dg_ref, dwg_ref, dbg_ref)

    rev = lambda i: nrow - 1 - i
    tile = pl.BlockSpec((tm, tn), lambda j, i: (rev(i), j))
    cspec = lambda half: pl.BlockSpec((KW, tn), lambda j, i: (0, half * ncol + j))
    acc = lambda rows: pl.BlockSpec((rows, tn), lambda j, i: (0, j))
    big = pltpu.HBM((S, F), BF16)
    blk = _nbytes((tm, D), BF16) + _nbytes((tn, D), BF16) + 10 * _nbytes((tm, tn), F32)
    in_specs, out_specs, out_shape, scratch = _carry_specs(
        rider, [pl.BlockSpec((tm, D), lambda j, i: (rev(i), 0)), pl.BlockSpec((tn, D), lambda j, i: (j, 0)),
                tile, tile, tile, tile, cspec(0), cspec(1)],
        (tile, tile, acc(KW), acc(KW), acc(1), acc(1)),
        (big, big, pltpu.HBM((KW, F), F32), pltpu.HBM((KW, F), F32), pltpu.HBM((1, F), F32),
         pltpu.HBM((1, F), F32)), [pltpu.VMEM((2, FFN_HALO, tn), F32)])
    first = lambda: (pl.program_id(0) == 0) & (pl.program_id(1) == 0)
    last = lambda: (pl.program_id(0) == ncol - 1) & (pl.program_id(1) == nrow - 1)
    res = _pallas(
        _carry(rider, body, 8, 6, first, last), name=name, grid=(ncol, nrow), out_shape=out_shape,
        in_specs=in_specs, out_specs=out_specs, scratch_shapes=scratch,
        compiler_params=_params(blk, ("arbitrary", "arbitrary")))(
            dzb, w_down, up_v, up_g, mult_v, mult_g, conv_w, conv_w, *(rider.arrays if rider else ()))
    return res[:6], list(res[6:])


def loss_head(y, target, *, tm, name):
    S, D = y.shape

    def body(y_ref, t_ref, dy_ref, l_ref):
        @pl.when(pl.program_id(0) == 0)
        def _():
            l_ref[...] = jnp.zeros_like(l_ref)

        e = y_ref[...] - t_ref[...]
        dy_ref[...] = e * (1.0 / D)
        l_ref[...] += 0.5 * jnp.sum(jnp.mean(e * e, axis=-1, keepdims=True), axis=0, keepdims=True)

    row = lambda i: (i, 0)
    return _pallas(
        body, name=name, grid=(S // tm,),
        out_shape=(pltpu.HBM((S, D), F32), pltpu.HBM((1, 1), F32)),
        in_specs=[pl.BlockSpec((tm, D), row), pl.BlockSpec((tm, D), row)],
        out_specs=(pl.BlockSpec((tm, D), row), pl.BlockSpec((1, 1), lambda i: (0, 0))),
        compiler_params=_params(3 * _nbytes((tm, D), F32), ("arbitrary",)))(y, target)


def adamw(w, g, m, v, *, tr, name):
    R, C = w.shape
    c1 = 1.0 - ADAM_B1 ** ADAM_STEP
    c2 = 1.0 - ADAM_B2 ** ADAM_STEP

    def body(w_ref, g_ref, m_ref, v_ref, go_ref, d_ref, mo_ref, vo_ref):
        gv = g_ref[...]
        mn = ADAM_B1 * m_ref[...] + (1.0 - ADAM_B1) * gv
        vn = ADAM_B2 * v_ref[...] + (1.0 - ADAM_B2) * (gv * gv)
        go_ref[...] = gv
        mo_ref[...] = mn
        vo_ref[...] = vn
        d_ref[...] = -ADAM_LR * ((mn / c1) / (jnp.sqrt(vn / c2) + ADAM_EPS) + ADAM_WD * w_ref[...])

    spec = pl.BlockSpec((tr, C), lambda i: (i, 0))
    shape = pltpu.HBM((R, C), F32)
    return _pallas(
        body, name=name, grid=(R // tr,), out_shape=(shape,) * 4, in_specs=[spec] * 4, out_specs=(spec,) * 4,
        compiler_params=_params(8 * _nbytes((tr, C), F32), ("parallel",)))(w, g, m, v)


def add_pairs(gs, gots, core, *, name):
    k = len(gs)

    def body(c_ref, *refs):
        for a_ref, b_ref, o_ref in zip(refs[:k], refs[k:2 * k], refs[2 * k:]):
            o_ref[...] = (a_ref[...].astype(F32) + b_ref[...].astype(F32)).astype(BF16)

    own = [pl.BlockSpec((None, None) + g.shape[2:], lambda i, c: (i, c[0], 0, 0)) for g in gs]
    half = [pl.BlockSpec((None,) + g.shape[1:], lambda i, c: (i, 0, 0)) for g in gots]
    grid_spec = pltpu.PrefetchScalarGridSpec(
        num_scalar_prefetch=1, grid=(N_CHIPS,), in_specs=own + half, out_specs=tuple(half))
    blk = 3 * sum(_nbytes(g.shape[1:], BF16) for g in gots)
    return _pallas(
        body, name=name, grid_spec=grid_spec, out_shape=tuple(pltpu.HBM(g.shape, BF16) for g in gots),
        compiler_params=_params(blk, ("parallel",)))(core, *gs, *gots)


def sum_chips_into(bs, dests, layer, core, *, name):
    k = len(bs)
    steps = 2

    def body(c_ref, *refs):
        for b_ref, o_ref in zip(refs[:k], refs[2 * k:]):
            acc = b_ref[0].astype(F32)
            for p in range(1, N_CHIPS):
                acc = acc + b_ref[p].astype(F32)
            o_ref[...] = acc

    ins = [pl.BlockSpec((N_CHIPS, b.shape[1] // steps, b.shape[2]), lambda i, c: (0, i, 0)) for b in bs]
    outs = tuple(pl.BlockSpec((None, None, b.shape[1] // steps, b.shape[2]), lambda i, c: (layer, c[0], i, 0))
                 for b in bs)
    grid_spec = pltpu.PrefetchScalarGridSpec(
        num_scalar_prefetch=1, grid=(steps,), in_specs=ins + [pl.BlockSpec(memory_space=pl.ANY)] * k,
        out_specs=outs)
    blk = sum(_nbytes(b.shape, BF16) + _nbytes(b.shape[1:], F32) for b in bs) // steps
    return _pallas(
        body, name=name, grid_spec=grid_spec, out_shape=tuple(pltpu.HBM(d.shape, F32) for d in dests),
        input_output_aliases={1 + k + w: w for w in range(k)},
        compiler_params=_params(blk, ("parallel",)))(core, *bs, *dests)


_HBM = pl.BlockSpec(memory_space=pltpu.HBM)


def _place():
    x, y, c = lax.axis_index("x"), lax.axis_index("y"), lax.axis_index("c")
    chips = [(1 - x, y), (x, 1 - y), (1 - x, 1 - y)]
    return x, y, c, chips


class GatherRider:
    def __init__(self, shards):
        self.arrays = list(shards)
        self.n = n = len(shards)
        self.out_shape = tuple(pltpu.HBM((N_CHIPS,) + s.shape, s.dtype) for s in shards)
        self.scratch = [pltpu.SemaphoreType.DMA((n, 3))] * 4 + [pltpu.SemaphoreType.DMA((n,))]

    def _copies(self, ins, outs, sems):
        send_ici, recv_ici, send_d2d, recv_d2d, local = sems
        x, y, c, chips = _place()
        me = 2 * x + y

        def own(w):
            return pltpu.make_async_copy(ins[w], outs[w].at[me], local.at[w])

        def ici(w, j):
            px, py = chips[j]
            return pltpu.make_async_remote_copy(
                src_ref=ins[w].at[c], dst_ref=outs[w].at[me, c], send_sem=send_ici.at[w, j],
                recv_sem=recv_ici.at[w, j], device_id=(px, py, c), device_id_type=MESH)

        def landed(w, j, half):
            px, py = chips[j]
            return outs[w].at[2 * px + py, half]

        def d2d(w, j, half):
            return pltpu.make_async_remote_copy(
                src_ref=landed(w, j, half), dst_ref=landed(w, j, half), send_sem=send_d2d.at[w, j],
                recv_sem=recv_d2d.at[w, j], device_id=(x, y, 1 - c), device_id_type=MESH)

        def ici_arrival(w, j):
            return pltpu.make_async_remote_copy(
                src_ref=landed(w, j, c), dst_ref=landed(w, j, c), send_sem=send_ici.at[w, j],
                recv_sem=recv_ici.at[w, j], device_id=(x, y, c), device_id_type=MESH)

        return c, own, ici, d2d, ici_arrival

    def start(self, ins, outs, sems):
        c, own, ici, d2d, ici_arrival = self._copies(ins, outs, sems)
        for w in range(self.n):
            own(w).start()
            for j in range(3):
                ici(w, j).start()

    def finish(self, ins, outs, sems):
        c, own, ici, d2d, ici_arrival = self._copies(ins, outs, sems)
        for w in range(self.n):
            for j in range(3):
                ici_arrival(w, j).wait_recv()
                d2d(w, j, c).start()
        for w in range(self.n):
            for j in range(3):
                d2d(w, j, 1 - c).wait_recv()
        for w in range(self.n):
            for j in range(3):
                ici(w, j).wait_send()
                d2d(w, j, c).wait_send()
            own(w).wait()


class ScatterRider:
    def __init__(self, parts):
        self.arrays = list(parts)
        self.n = n = len(parts)
        self.out_shape = tuple(pltpu.HBM(p.shape, p.dtype) for p in parts)
        self.scratch = [pltpu.SemaphoreType.DMA((n, 3))] * 2 + [pltpu.SemaphoreType.DMA((n,))]

    def _copies(self, ins, outs, sems):
        send, recv, local = sems
        x, y, c, chips = _place()
        me = 2 * x + y

        def own(w):
            return pltpu.make_async_copy(ins[w].at[me], outs[w].at[me], local.at[w])

        def copy(w, j):
            px, py = chips[j]
            return pltpu.make_async_remote_copy(
                src_ref=ins[w].at[2 * px + py], dst_ref=outs[w].at[me], send_sem=send.at[w, j],
                recv_sem=recv.at[w, j], device_id=(px, py, c), device_id_type=MESH)

        def arrival(w, j):
            px, py = chips[j]
            blk = outs[w].at[2 * px + py]
            return pltpu.make_async_remote_copy(
                src_ref=blk, dst_ref=blk, send_sem=send.at[w, j], recv_sem=recv.at[w, j],
                device_id=(x, y, c), device_id_type=MESH)

        return own, copy, arrival

    def start(self, ins, outs, sems):
        own, copy, arrival = self._copies(ins, outs, sems)
        for w in range(self.n):
            own(w).start()
            for j in range(3):
                copy(w, j).start()

    def finish(self, ins, outs, sems):
        own, copy, arrival = self._copies(ins, outs, sems)
        for w in range(self.n):
            for j in range(3):
                arrival(w, j).wait_recv()
        for w in range(self.n):
            for j in range(3):
                copy(w, j).wait_send()
            own(w).wait()


def _carry(rider, body, n_in, n_out, first, last):
    if rider is None:
        return body
    k, m = rider.n, len(rider.scratch)

    def carried(*refs):
        ins, r_in = refs[:n_in], refs[n_in:n_in + k]
        outs, r_out = refs[n_in + k:n_in + k + n_out], refs[n_in + k + n_out:n_in + 2 * k + n_out]
        rest = refs[n_in + 2 * k + n_out:]
        scratch, sems = rest[:len(rest) - m], rest[len(rest) - m:]

        @pl.when(first())
        def _():
            rider.start(r_in, r_out, sems)

        body(*ins, *outs, *scratch)

        @pl.when(last())
        def _():
            rider.finish(r_in, r_out, sems)

    return carried


def _carry_specs(rider, in_specs, out_specs, out_shape, scratch):
    if rider is None:
        return list(in_specs), tuple(out_specs), tuple(out_shape), list(scratch)
    k = rider.n
    return (list(in_specs) + [_HBM] * k, tuple(out_specs) + (_HBM,) * k, tuple(out_shape) + rider.out_shape,
            list(scratch) + list(rider.scratch))


def run_riders(riders, *, name):
    ks = [r.n for r in riders]
    ms = [len(r.scratch) for r in riders]
    k_all = sum(ks)

    def body(*refs):
        parts, i0, o0, s0 = [], 0, k_all, 2 * k_all
        for k, m in zip(ks, ms):
            parts.append((refs[i0:i0 + k], refs[o0:o0 + k], refs[s0:s0 + m]))
            i0, o0, s0 = i0 + k, o0 + k, s0 + m
        for r, p in zip(riders, parts):
            r.start(*p)
        for r, p in zip(riders, parts):
            r.finish(*p)

    res = _pallas(
        body, name=name, out_shape=tuple(o for r in riders for o in r.out_shape), in_specs=[_HBM] * k_all,
        out_specs=(_HBM,) * k_all, scratch_shapes=[s for r in riders for s in r.scratch],
    )(*[a for r in riders for a in r.arrays])
    out, o0 = [], 0
    for k in ks:
        out.append(list(res[o0:o0 + k]))
        o0 += k
    return out


def run_rider(rider, *, name):
    return run_riders([rider], name=name)[0]


class SmallGatherRider:
    def __init__(self, shards):
        self.arrays = list(shards)
        self.n = n = len(shards)
        self.out_shape = tuple(pltpu.HBM((N_CHIPS,) + s.shape, s.dtype) for s in shards)
        self.scratch = [pltpu.SemaphoreType.DMA((n, 3))] * 2 + [pltpu.SemaphoreType.DMA((n,))]

    def _copies(self, ins, outs, sems):
        send, recv, local = sems
        x, y, c, chips = _place()
        me = 2 * x + y

        def own(w):
            return pltpu.make_async_copy(ins[w], outs[w].at[me], local.at[w])

        def copy(w, j):
            px, py = chips[j]
            return pltpu.make_async_remote_copy(
                src_ref=ins[w], dst_ref=outs[w].at[me], send_sem=send.at[w, j], recv_sem=recv.at[w, j],
                device_id=(px, py, c), device_id_type=MESH)

        def arrival(w, j):
            px, py = chips[j]
            blk = outs[w].at[2 * px + py]
            return pltpu.make_async_remote_copy(
                src_ref=blk, dst_ref=blk, send_sem=send.at[w, j], recv_sem=recv.at[w, j],
                device_id=(x, y, c), device_id_type=MESH)

        return own, copy, arrival

    def start(self, ins, outs, sems):
        own, copy, arrival = self._copies(ins, outs, sems)
        for w in range(self.n):
            own(w).start()
            for j in range(3):
                copy(w, j).start()

    def finish(self, ins, outs, sems):
        own, copy, arrival = self._copies(ins, outs, sems)
        for w in range(self.n):
            for j in range(3):
                arrival(w, j).wait_recv()
        for w in range(self.n):
            for j in range(3):
                copy(w, j).wait_send()
            own(w).wait()


class SwapRider:
    def __init__(self, grads):
        self.arrays = list(grads)
        self.n = n = len(grads)
        self.out_shape = tuple(pltpu.HBM((N_CHIPS,) + g.shape[2:], g.dtype) for g in grads)
        self.scratch = [pltpu.SemaphoreType.DMA((n,))] * 2

    def _copies(self, ins, outs, sems):
        send, recv = sems
        x, y, c, _ = _place()
        return [pltpu.make_async_remote_copy(
            src_ref=ins[w].at[:, 1 - c], dst_ref=outs[w], send_sem=send.at[w], recv_sem=recv.at[w],
            device_id=(x, y, 1 - c), device_id_type=MESH) for w in range(self.n)]

    def start(self, ins, outs, sems):
        for cp in self._copies(ins, outs, sems):
            cp.start()

    def finish(self, ins, outs, sems):
        copies = self._copies(ins, outs, sems)
        for cp in copies:
            cp.wait_recv()
        for cp in copies:
            cp.wait_send()


def rs_sibling_share(stacked, *, name):
    n = len(stacked)

    def body(*refs):
        bufs = refs[n:2 * n]
        send, recv = refs[2 * n:]
        x, y, c, _ = _place()
        shares, arrivals = [], []
        for w in range(n):
            mine, other = bufs[w].at[:, c], bufs[w].at[:, 1 - c]
            shares.append(pltpu.make_async_remote_copy(
                src_ref=mine, dst_ref=mine, send_sem=send.at[w], recv_sem=recv.at[w],
                device_id=(x, y, 1 - c), device_id_type=MESH))
            arrivals.append(pltpu.make_async_remote_copy(
                src_ref=other, dst_ref=other, send_sem=send.at[w], recv_sem=recv.at[w],
                device_id=(x, y, c), device_id_type=MESH))
        for cp in shares:
            cp.start()
        for cp in arrivals:
            cp.wait_recv()
        for cp in shares:
            cp.wait_send()

    out_shape = tuple(pltpu.HBM(s.shape, F32) for s in stacked)
    return _pallas(
        body, name=name, out_shape=out_shape, in_specs=[_HBM] * n, out_specs=(_HBM,) * n,
        input_output_aliases={w: w for w in range(n)},
        scratch_shapes=[pltpu.SemaphoreType.DMA((n,))] * 2,
    )(*stacked)


class EveryoneRider:
    def __init__(self, v):
        self.arrays = [v]
        self.n = 1
        self.out_shape = (pltpu.HBM((N_DEV,) + v.shape, v.dtype),)
        self.scratch = [pltpu.SemaphoreType.DMA((N_DEV - 1,))] * 2 + [pltpu.SemaphoreType.DMA(())]

    def _copies(self, ins, outs, sems):
        send, recv, local = sems
        x, y, c, _ = _place()
        me = 4 * x + 2 * y + c

        def flip(k):
            return (1 - x) if k & 4 else x, (1 - y) if k & 2 else y, (1 - c) if k & 1 else c

        own = pltpu.make_async_copy(ins[0], outs[0].at[me], local)
        sends, arrivals = [], []
        for k in range(1, N_DEV):
            px, py, pc = flip(k)
            sends.append(pltpu.make_async_remote_copy(
                src_ref=ins[0], dst_ref=outs[0].at[me], send_sem=send.at[k - 1], recv_sem=recv.at[k - 1],
                device_id=(px, py, pc), device_id_type=MESH))
            blk = outs[0].at[4 * px + 2 * py + pc]
            arrivals.append(pltpu.make_async_remote_copy(
                src_ref=blk, dst_ref=blk, send_sem=send.at[k - 1], recv_sem=recv.at[k - 1],
                device_id=(x, y, c), device_id_type=MESH))
        return own, sends, arrivals

    def start(self, ins, outs, sems):
        own, sends, _ = self._copies(ins, outs, sems)
        own.start()
        for cp in sends:
            cp.start()

    def finish(self, ins, outs, sems):
        own, sends, arrivals = self._copies(ins, outs, sems)
        for cp in arrivals:
            cp.wait_recv()
        for cp in sends:
            cp.wait_send()
        own.wait()


def sum_devices(land, *, name):
    n, R, C = land.shape

    def body(l_ref, o_ref):
        acc = l_ref[0]
        for d in range(1, n):
            acc = acc + l_ref[d]
        o_ref[...] = acc

    return _pallas(
        body, name=name, grid=(1,), out_shape=pltpu.HBM((R, C), F32),
        in_specs=[pl.BlockSpec((n, R, C), lambda i: (0, 0, 0))], out_specs=pl.BlockSpec((R, C), lambda i: (0, 0)),
        compiler_params=_params(_nbytes(land.shape, F32), ("arbitrary",)))(land)


def _pack(arrays):
    flat = jnp.concatenate([a.reshape(-1) for a in arrays])
    return flat.reshape(-1, LANES)


def _unpack(packed, shapes):
    flat = packed.reshape(-1)
    out, off = [], 0
    for s in shapes:
        n = 1
        for d in s:
            n *= d
        out.append(flat[off:off + n].reshape(s))
        off += n
    return out


def _row_tile(rows, cap=512):
    t = 1 << (cap.bit_length() - 1)
    while rows % t:
        t //= 2
    return t


def _adamw_tile(rows, cols):
    return _row_tile(rows, max(8, (1 << 20) // (4 * cols)))


def kernel(x, mem, w_in, conv_w, conv_b, conv_ln_g, conv_ln_b, w_out, ln1_g, ln1_b, mem_wq, mem_wk, mem_wv, mem_wo, ln2_g, ln2_b, ffn_up, ffn_conv_w, ffn_conv_b, ffn_down, ln3_g, ln3_b, loss_target, m_w_in, m_conv_w, m_conv_b, m_conv_ln_g, m_conv_ln_b, m_w_out, m_ln1_g, m_ln1_b, m_mem_wq, m_mem_wk, m_mem_wv, m_mem_wo, m_ln2_g, m_ln2_b, m_ffn_up, m_ffn_conv_w, m_ffn_conv_b, m_ffn_down, m_ln3_g, m_ln3_b, v_w_in, v_conv_w, v_conv_b, v_conv_ln_g, v_conv_ln_b, v_w_out, v_ln1_g, v_ln1_b, v_mem_wq, v_mem_wk, v_mem_wv, v_mem_wo, v_ln2_g, v_ln2_b, v_ffn_up, v_ffn_conv_w, v_ffn_conv_b, v_ffn_down, v_ln3_g, v_ln3_b):
    W = dict(w_in=w_in, conv_w=conv_w, conv_b=conv_b, conv_ln_g=conv_ln_g, conv_ln_b=conv_ln_b, w_out=w_out,
             ln1_g=ln1_g, ln1_b=ln1_b, mem_wq=mem_wq, mem_wk=mem_wk, mem_wv=mem_wv, mem_wo=mem_wo, ln2_g=ln2_g,
             ln2_b=ln2_b, ffn_up=ffn_up, ffn_conv_w=ffn_conv_w, ffn_conv_b=ffn_conv_b, ffn_down=ffn_down,
             ln3_g=ln3_g, ln3_b=ln3_b)
    M1 = dict(w_in=m_w_in, conv_w=m_conv_w, conv_b=m_conv_b, conv_ln_g=m_conv_ln_g, conv_ln_b=m_conv_ln_b,
              w_out=m_w_out, ln1_g=m_ln1_g, ln1_b=m_ln1_b, mem_wq=m_mem_wq, mem_wk=m_mem_wk, mem_wv=m_mem_wv,
              mem_wo=m_mem_wo, ln2_g=m_ln2_g, ln2_b=m_ln2_b, ffn_up=m_ffn_up, ffn_conv_w=m_ffn_conv_w,
              ffn_conv_b=m_ffn_conv_b, ffn_down=m_ffn_down, ln3_g=m_ln3_g, ln3_b=m_ln3_b)
    V2 = dict(w_in=v_w_in, conv_w=v_conv_w, conv_b=v_conv_b, conv_ln_g=v_conv_ln_g, conv_ln_b=v_conv_ln_b,
              w_out=v_w_out, ln1_g=v_ln1_g, ln1_b=v_ln1_b, mem_wq=v_mem_wq, mem_wk=v_mem_wk, mem_wv=v_mem_wv,
              mem_wo=v_mem_wo, ln2_g=v_ln2_g, ln2_b=v_ln2_b, ffn_up=v_ffn_up, ffn_conv_w=v_ffn_conv_w,
              ffn_conv_b=v_ffn_conv_b, ffn_down=v_ffn_down, ln3_g=v_ln3_g, ln3_b=v_ln3_b)

    L = w_in.shape[0]
    S, D = x.shape[1], x.shape[2]
    C = conv_b.shape[1]
    alpha = (2.0 * L) ** 0.25
    chip = 2 * lax.axis_index("x") + lax.axis_index("y")
    xs, mems, tgt = x[0], mem[0], loss_target[0]
    mem_bf = mems.astype(BF16)
    tm = _row_tile(S)
    tm_ffn = _row_tile(S, 256)
    tm_big = _row_tile(S, 1024)
    tm_half = _row_tile(S, 2048)

    def shards_of(l, names):
        out = []
        for n in names:
            wl = W[n][l].astype(BF16)
            out.append(wl.reshape(2, wl.shape[0] // 2, wl.shape[1]))
        return out

    def gathered(names, got):
        layer = {}
        for n, g in zip(names, got):
            rows, cols = W[n].shape[1], W[n].shape[2]
            layer[n] = g.reshape(N_CHIPS, rows, cols) if n in COL_SHARDED else g.reshape(N_CHIPS * rows, cols)
        return layer

    full = [dict() for _ in range(L)]
    got, (cw_all, fcw_all) = run_riders(
        [GatherRider(shards_of(0, RIDE_IN)), SmallGatherRider([conv_w, ffn_conv_w])], name="allgather_first")
    full[0].update(gathered(RIDE_IN, got))
    cw_full = jnp.transpose(cw_all, (1, 2, 0, 3)).reshape(L, conv_w.shape[1], -1)
    fcw_full = jnp.transpose(fcw_all, (1, 2, 0, 3)).reshape(L, ffn_conv_w.shape[1], -1)

    saved = []
    h, hb = xs, xs.astype(BF16)
    for l in range(L):
        fw = full[l]
        s = dict(x=h, xb=hb)
        s['glu'], s['qkv'] = proj_split(hb, fw['w_in'], 2 * C, tm=tm, name="proj")
        on_conv = RIDE_ATT if l == 0 else RIDE_FFN[1:]
        on_sb = RIDE_FFN if l == 0 else RIDE_FFN[:1]
        s['u1'], got = conv_fwd(s['glu'], cw_full[l], conv_b[l][None], name="conv_fwd",
                                rider=GatherRider(shards_of(l, on_conv)))
        fw.update(gathered(on_conv, got))
        more = l + 1 < L
        s['o_sb'], s['ltot'], got = sb_fwd(
            s['qkv'], q_col=0, name="sb_fwd", rider=GatherRider(shards_of(l, on_sb)))
        fw.update(gathered(on_sb, got))
        s['ua'] = ln_silu(s['u1'], s['o_sb'], conv_ln_g[l][None], conv_ln_b[l][None], tm=tm, name="ln_silu")
        s['x1'], s['x1b'], s['zh1'], s['rs1'] = mm_ln(
            s['ua'], fw['w_out'], h, ln1_g[l][None], ln1_b[l][None], alpha, tm=tm, name="out_proj_ln")
        s['q2'] = mm_nn(s['x1b'], fw['mem_wq'], BF16, tm=tm, tn=D, name="mem_q")
        s['k2'], s['v2'] = mm_pair(mem_bf, fw['mem_wk'], fw['mem_wv'], name="mem_kv")
        s['o2'] = xattn_fwd(s['q2'], s['k2'], s['v2'], tm=tm, name="xattn_fwd")
        s['x2'], s['x2b'], s['zh2'], s['rs2'] = mm_ln(
            s['o2'], fw['mem_wo'], s['x1'], ln2_g[l][None], ln2_b[l][None], alpha, tm=tm, name="mem_o_ln")
        (s['upv'], s['upg'], s['mv'], s['mg'], s['hmid']), got = ffn_up_fwd(
            s['x2b'], fw['ffn_up'], fcw_full[l], ffn_conv_b[l][None], tm=tm_ffn, tn=fw['ffn_up'].shape[2],
            name="ffn_up_fwd", rider=GatherRider(shards_of(l + 1, RIDE_ATT + RIDE_IN)) if more else None)
        if more:
            full[l + 1].update(gathered(RIDE_ATT + RIDE_IN, got))
        h, hb, s['zh3'], s['rs3'] = mm_ln(
            s['hmid'], fw['ffn_down'], s['x2'], ln3_g[l][None], ln3_b[l][None], alpha, tm=tm, name="ffn_down_ln")
        saved.append(s)

    dx, loss_part = loss_head(h, tgt, tm=tm, name="loss_head")
    loss = lax.psum(loss_part[0, 0], ("x", "y", "c"))

    core = lax.axis_index("c").astype(jnp.int32).reshape(1)
    reduced_big = {n: lax.empty((L, 2, W[n].shape[1] // 2, W[n].shape[2]), F32) for n in BIG}
    small_grads = [None] * L

    def row_halves(g, names):
        parts = []
        for n in names:
            rows, cols = W[n].shape[1], W[n].shape[2]
            parts.append(g[n].reshape(N_CHIPS, 2, rows // 2, cols))
        return parts

    def pre_add(g, names):
        parts = row_halves(g, names)
        got = run_rider(SwapRider(parts), name="rs_sibling_swap")
        return list(add_pairs(parts, got, core, name="rs_add_pairs"))

    def reduce_into(names, scattered, layer):
        reduced_big.update(zip(names, sum_chips_into(
            list(scattered), [reduced_big[n] for n in names], layer, core, name="rs_sum_chips")))

    pending = None
    for l in reversed(range(L)):
        fw, s = full[l], saved[l]
        g = {}
        if l == L - 1:
            top = ln_bwd(dx, s['zh3'], s['rs3'], ln3_g[l][None], tm=tm, name="ln_bwd")
        dz3, dz3b, g['ln3_g'], g['ln3_b'] = top
        ftn = fw['ffn_up'].shape[2]
        (dupv, dupg, dfw_v, dfw_g, dfb_v, dfb_g), sc = ffn_mid_bwd(
            dz3b, fw['ffn_down'], s['upv'], s['upg'], s['mv'], s['mg'], fcw_full[l], tm=tm_ffn, tn=ftn,
            name="ffn_mid_bwd", rider=ScatterRider(pending) if pending else None)
        if pending:
            reduce_into(RIDE_MIX, sc, l + 1)
        g['ffn_conv_w'] = jnp.concatenate([dfw_v, dfw_g], axis=1)
        g['ffn_conv_b'] = jnp.concatenate([dfb_v, dfb_g], axis=1)[0]
        g['ffn_down'] = mm_tn(s['hmid'], [dz3b], tk=ftn, tn=D, tmc=min(1024, S), name="grad_ffn_down")
        dz2, dz2b, g['ln2_g'], g['ln2_b'] = mm_nt_ln_bwd(
            [dupv, dupg], fw['ffn_up'], dz3, alpha, s['zh2'], s['rs2'], ln2_g[l][None], tm=tm_ffn,
            name="ffn_up_bwd")
        g['ffn_up'] = mm_tn(s['x2b'], [dupv, dupg], tk=D, tn=ftn, shard_width=ftn, tmc=min(1024, S),
                            name="grad_ffn_up")

        do2 = mm_nt([dz2b], fw['mem_wo'], BF16, tm=tm, tk=D, name="mem_o_bwd")
        g['mem_wo'] = mm_tn(s['o2'], [dz2b], tk=D, tn=D, tmc=tm_half, name="grad_sq")
        dq2, dk2, dv2 = xattn_bwd(s['q2'], do2, s['k2'], s['v2'], tm=tm, name="xattn_bwd")
        dz1, dz1b, g['ln1_g'], g['ln1_b'] = mm_nt_ln_bwd(
            [dq2], fw['mem_wq'], dz2, alpha, s['zh1'], s['rs1'], ln1_g[l][None], tm=tm, name="mem_q_bwd")
        g['mem_wq'] = mm_tn(s['x1b'], [dq2], tk=D, tn=D, tmc=tm_half, name="grad_sq")
        g['mem_wk'], g['mem_wv'] = mm_tn_pair(mem_bf, dk2, dv2, name="grad_mem_kv")

        g['w_out'] = mm_tn(s['ua'], [dz1b], tk=D, tn=D, tmc=tm_half, name="grad_sq")
        rest = row_halves(g, RIDE_REST)
        dua, got = mm_nt([dz1b], fw['w_out'], F32, tm=tm, tk=D, name="out_proj_bwd", rider=SwapRider(rest))
        rest = list(add_pairs(rest, got, core, name="rs_add_pairs"))
        dq, dk, dv, sc = sb_bwd(
            s['qkv'], s['ltot'], dua, q_col=0, do_col=C, name="sb_bwd",
            rider=ScatterRider(rest[:-1]))
        reduce_into(RIDE_REST[:-1], sc, l)
        du1, g['conv_ln_g'], g['conv_ln_b'] = ln_silu_bwd(
            dua, s['u1'], conv_ln_g[l][None], conv_ln_b[l][None], tm=tm, name="ln_silu_bwd")
        (da, dg, g['conv_w'], dcb), sc = conv_bwd(du1, s['glu'], cw_full[l], name="conv_bwd",
                                                  rider=ScatterRider(rest[-1:]))
        reduce_into(RIDE_REST[-1:], sc, l)
        g['conv_b'] = dcb
        dproj = jnp.concatenate([da, dg, dq, dk, dv], axis=1)
        ns_in = fw['w_in'].shape[2]
        if l > 0:
            below = saved[l - 1]
            top = mm_nt_ln_bwd([dproj], fw['w_in'], dz1, alpha, below['zh3'], below['rs3'], ln3_g[l - 1][None],
                               tm=tm, name="proj_bwd")
        else:
            dx = mm_nt([dproj], fw['w_in'], F32, tm=tm_big, tk=512, res=dz1, alpha=alpha, name="proj_bwd_x")
        g['w_in'] = mm_tn(s['xb'], [dproj], tk=512, tn=ns_in, shard_width=ns_in, name="grad_w_in")

        pending = pre_add(g, RIDE_MIX)
        small_grads[l] = {n: g[n].reshape(W[n].shape[1:-1] + (-1,)) for n in SMALL}

    grad_x = dx[None]

    small_full_shapes = []
    small_stack = []
    for n in SMALL:
        st = jnp.stack([small_grads[l][n] for l in range(L)])
        small_stack.append(st)
        small_full_shapes.append(st.shape)
    scattered, everyone = run_riders([ScatterRider(pending), EveryoneRider(_pack(small_stack))],
                                     name="rs_tail_exchange")
    reduce_into(RIDE_MIX, scattered, 0)
    shared = rs_sibling_share([reduced_big[n] for n in BIG], name="rs_sibling_share")
    G = {}
    for n, sh in zip(BIG, shared):
        G[n] = sh.reshape(W[n].shape)
    reduced = _unpack(sum_devices(everyone[0], name="sum_small"), small_full_shapes)
    for n, r in zip(SMALL, reduced):
        if n in SMALL_SHARDED:
            width = W[n].shape[-1]
            r = lax.dynamic_slice_in_dim(r, chip * width, width, axis=2)
        G[n] = r

    out_g, out_d, out_m, out_v = {}, {}, {}, {}
    for n in BIG:
        shp = W[n].shape
        flat = lambda a: a.reshape(shp[0] * shp[1], shp[2])
        res = adamw(flat(W[n]), flat(G[n]), flat(M1[n]), flat(V2[n]), tr=_adamw_tile(shp[0] * shp[1], shp[2]), name="adamw")
        out_g[n], out_d[n], out_m[n], out_v[n] = [r.reshape(shp) for r in res]
    small_shapes = [W[n].shape for n in SMALL]
    packed = [_pack([d[n] for n in SMALL]) for d in (W, G, M1, V2)]
    res = adamw(*packed, tr=packed[0].shape[0], name="adamw_small")
    for d, r in zip((out_g, out_d, out_m, out_v), res):
        for n, a in zip(SMALL, _unpack(r, small_shapes)):
            d[n] = a

    return (loss, grad_x, *[out_g[n] for n in WEIGHTS], *[out_d[n] for n in WEIGHTS],
            *[out_m[n] for n in WEIGHTS], *[out_v[n] for n in WEIGHTS])
```

```python
import functools

import jax
import jax.numpy as jnp
from jax import lax
from jax.experimental import pallas as pl
from jax.experimental.pallas import tpu as pltpu

F32 = jnp.float32
BF16 = jnp.bfloat16
MESH = pl.DeviceIdType.MESH

LN_EPS = 1e-5
SB_HEADS = 8
MEM_HEADS = 4
ADAM_LR, ADAM_B1, ADAM_B2, ADAM_EPS, ADAM_WD, ADAM_STEP = 0.001, 0.9, 0.999, 1e-08, 0.01, 10

LANES = 128
V7X_VMEM_BYTES = 64 << 20
VMEM_CAP = V7X_VMEM_BYTES - (6 << 20)
N_CHIPS = 4
N_DEV = 8

BIG = ('w_in', 'w_out', 'mem_wq', 'mem_wk', 'mem_wv', 'mem_wo', 'ffn_up', 'ffn_down')
RIDE_IN = ('w_in',)
RIDE_ATT = ('w_out', 'mem_wq', 'mem_wk', 'mem_wv', 'mem_wo')
RIDE_FFN = ('ffn_up', 'ffn_down')
RIDE_MIX = ('w_in',)
RIDE_REST = ('w_out', 'mem_wq', 'mem_wk', 'mem_wv', 'mem_wo', 'ffn_up', 'ffn_down')
COL_SHARDED = ('w_in', 'ffn_up')
SMALL = ('conv_w', 'conv_b', 'conv_ln_g', 'conv_ln_b', 'ln1_g', 'ln1_b', 'ln2_g', 'ln2_b',
         'ffn_conv_w', 'ffn_conv_b', 'ln3_g', 'ln3_b')
SMALL_SHARDED = ('conv_w', 'ffn_conv_w')
WEIGHTS = ('w_in', 'conv_w', 'conv_b', 'conv_ln_g', 'conv_ln_b', 'w_out', 'ln1_g', 'ln1_b',
           'mem_wq', 'mem_wk', 'mem_wv', 'mem_wo', 'ln2_g', 'ln2_b', 'ffn_up', 'ffn_conv_w',
           'ffn_conv_b', 'ffn_down', 'ln3_g', 'ln3_b')


def _params(block_bytes, semantics=None, **kw):
    limit = int(min(max(2 * block_bytes + (8 << 20), 32 << 20), VMEM_CAP))
    return pltpu.CompilerParams(dimension_semantics=semantics, vmem_limit_bytes=limit, **kw)


def _pallas(body, **kw):
    call = pl.pallas_call(body, **kw)

    def run(*args):
        return call(*[pltpu.with_memory_space_constraint(a, pltpu.HBM)
                      if jnp.issubdtype(a.dtype, jnp.floating) else a for a in args])

    return run


def _nbytes(shape, dtype):
    n = 1
    for s in shape:
        n *= s
    return n * jnp.dtype(dtype).itemsize


def _dot(a, b):
    return jnp.dot(a, b, preferred_element_type=F32)


def _dot_nt(a, b):
    return lax.dot_general(a, b, (((1,), (1,)), ((), ())), preferred_element_type=F32)


def _dot_tn(a, b):
    return lax.dot_general(a, b, (((0,), (0,)), ((), ())), preferred_element_type=F32)


def _sigmoid(x):
    return 1.0 / (1.0 + jnp.exp(-x))


def mm_nn(a, b, out_dtype, *, tm, tn, name):
    M, K = a.shape
    sharded = b.ndim == 3
    if sharded:
        nsh, _, ns = b.shape
        N, per = nsh * ns, ns // tn
        b_spec = pl.BlockSpec((None, K, tn), lambda i, j: (j // per, 0, j % per))
    else:
        N = b.shape[1]
        b_spec = pl.BlockSpec((K, tn), lambda i, j: (0, j))

    def body(a_ref, b_ref, o_ref):
        o_ref[...] = _dot(a_ref[...].astype(BF16), b_ref[...]).astype(o_ref.dtype)

    blk = _nbytes((tm, K), a.dtype) + _nbytes((K, tn), BF16) + _nbytes((tm, tn), out_dtype)
    return _pallas(
        body, name=name, out_shape=pltpu.HBM((M, N), out_dtype), grid=(M // tm, N // tn),
        in_specs=[pl.BlockSpec((tm, K), lambda i, j: (i, 0)), b_spec],
        out_specs=pl.BlockSpec((tm, tn), lambda i, j: (i, j)),
        compiler_params=_params(blk, ("parallel", "parallel")))(a, b)


def mm_pair(a, b1, b2, *, name):
    M, K = a.shape
    N = b1.shape[1]

    def body(a_ref, b1_ref, b2_ref, o1_ref, o2_ref):
        av = a_ref[...]
        o1_ref[...] = _dot(av, b1_ref[...]).astype(BF16)
        o2_ref[...] = _dot(av, b2_ref[...]).astype(BF16)

    whole = lambda r, c: pl.BlockSpec((r, c), lambda i: (0, 0))
    return _pallas(
        body, name=name, grid=(1,), out_shape=(pltpu.HBM((M, N), BF16), pltpu.HBM((M, N), BF16)),
        in_specs=[whole(M, K), whole(K, N), whole(K, N)], out_specs=(whole(M, N), whole(M, N)),
        compiler_params=_params(3 * _nbytes((K, N), BF16), ("arbitrary",)))(a, b1, b2)


def mm_tn_pair(a, b1, b2, *, name):
    M, K = a.shape
    N = b1.shape[1]

    def body(a_ref, b1_ref, b2_ref, o1_ref, o2_ref):
        av = a_ref[...]
        o1_ref[...] = _dot_tn(av, b1_ref[...].astype(BF16)).astype(BF16)
        o2_ref[...] = _dot_tn(av, b2_ref[...].astype(BF16)).astype(BF16)

    whole = lambda r, c: pl.BlockSpec((r, c), lambda i: (0, 0))
    return _pallas(
        body, name=name, grid=(1,), out_shape=(pltpu.HBM((K, N), BF16), pltpu.HBM((K, N), BF16)),
        in_specs=[whole(M, K), whole(M, N), whole(M, N)], out_specs=(whole(K, N), whole(K, N)),
        compiler_params=_params(4 * _nbytes((K, N), BF16), ("arbitrary",)))(a, b1, b2)


def proj_split(a, b, n_f32, *, tm, name):
    M, K = a.shape
    nsh, _, ns = b.shape
    N = nsh * ns

    def body(a_ref, b_ref, lo_ref, hi_ref):
        av = a_ref[...]
        for s in range(nsh):
            acc = _dot(av, b_ref[s])
            c0, c1 = s * ns, (s + 1) * ns
            cut = min(max(n_f32 - c0, 0), ns)
            if cut > 0:
                lo_ref[:, c0:c0 + cut] = acc[:, 0:cut]
            if cut < ns:
                hi_ref[:, c0 + cut - n_f32:c1 - n_f32] = acc[:, cut:ns].astype(BF16)

    blk = _nbytes((tm, K), BF16) + _nbytes((K, N), BF16) + _nbytes((tm, N), F32)
    return _pallas(
        body, name=name, grid=(M // tm,),
        out_shape=(pltpu.HBM((M, n_f32), F32), pltpu.HBM((M, N - n_f32), BF16)),
        in_specs=[pl.BlockSpec((tm, K), lambda i: (i, 0)), pl.BlockSpec((nsh, K, ns), lambda i: (0, 0, 0))],
        out_specs=(pl.BlockSpec((tm, n_f32), lambda i: (i, 0)), pl.BlockSpec((tm, N - n_f32), lambda i: (i, 0))),
        compiler_params=_params(blk, ("parallel",)))(a, b)


def mm_ln(a, b, x, gamma, beta, alpha, *, tm, name):
    M, K = a.shape
    D = b.shape[1]

    def body(a_ref, b_ref, x_ref, g_ref, be_ref, y_ref, yb_ref, zh_ref, rs_ref):
        z = alpha * x_ref[...] + _dot(a_ref[...], b_ref[...])
        mu = jnp.mean(z, axis=-1, keepdims=True)
        zc = z - mu
        rstd = lax.rsqrt(jnp.mean(zc * zc, axis=-1, keepdims=True) + LN_EPS)
        zh = zc * rstd
        y = zh * g_ref[...] + be_ref[...]
        y_ref[...] = y
        yb_ref[...] = y.astype(BF16)
        zh_ref[...] = zh
        rs_ref[...] = rstd

    row = lambda i: (i, 0)
    fix = lambda i: (0, 0)
    blk = _nbytes((tm, K), BF16) + _nbytes((K, D), BF16) + 4 * _nbytes((tm, D), F32)
    return _pallas(
        body, name=name, grid=(M // tm,),
        out_shape=(pltpu.HBM((M, D), F32), pltpu.HBM((M, D), BF16),
                   pltpu.HBM((M, D), F32), pltpu.HBM((M, 1), F32)),
        in_specs=[pl.BlockSpec((tm, K), row), pl.BlockSpec((K, D), fix), pl.BlockSpec((tm, D), row),
                  pl.BlockSpec((1, D), fix), pl.BlockSpec((1, D), fix)],
        out_specs=(pl.BlockSpec((tm, D), row), pl.BlockSpec((tm, D), row), pl.BlockSpec((tm, D), row),
                   pl.BlockSpec((tm, 1), row)),
        compiler_params=_params(blk, ("parallel",)))(a, b, x, gamma, beta)


def ln_bwd(dy, zh, rstd, gamma, *, tm, name):
    M, D = dy.shape

    def body(dy_ref, zh_ref, rs_ref, g_ref, dz_ref, dzb_ref, dg_ref, db_ref):
        @pl.when(pl.program_id(0) == 0)
        def _():
            dg_ref[...] = jnp.zeros_like(dg_ref)
            db_ref[...] = jnp.zeros_like(db_ref)

        dyv, zhv = dy_ref[...], zh_ref[...]
        dg_ref[...] += jnp.sum(dyv * zhv, axis=0, keepdims=True)
        db_ref[...] += jnp.sum(dyv, axis=0, keepdims=True)
        dzh = dyv * g_ref[...]
        m1 = jnp.mean(dzh, axis=-1, keepdims=True)
        m2 = jnp.mean(dzh * zhv, axis=-1, keepdims=True)
        dz = rs_ref[...] * (dzh - m1 - zhv * m2)
        dz_ref[...] = dz
        dzb_ref[...] = dz.astype(BF16)

    row = lambda i: (i, 0)
    fix = lambda i: (0, 0)
    return _pallas(
        body, name=name, grid=(M // tm,),
        out_shape=(pltpu.HBM((M, D), F32), pltpu.HBM((M, D), BF16),
                   pltpu.HBM((1, D), F32), pltpu.HBM((1, D), F32)),
        in_specs=[pl.BlockSpec((tm, D), row), pl.BlockSpec((tm, D), row), pl.BlockSpec((tm, 1), row),
                  pl.BlockSpec((1, D), fix)],
        out_specs=(pl.BlockSpec((tm, D), row), pl.BlockSpec((tm, D), row), pl.BlockSpec((1, D), fix),
                   pl.BlockSpec((1, D), fix)),
        compiler_params=_params(4 * _nbytes((tm, D), F32), ("arbitrary",)))(dy, zh, rstd, gamma)


def mm_nt(a_list, b, out_dtype, *, tm, tk, name, res=None, alpha=None, rider=None):
    M = a_list[0].shape[0]
    widths = [a.shape[1] for a in a_list]
    sharded = b.ndim == 3
    if sharded:
        nsh, K, ns = b.shape
        b_spec = pl.BlockSpec((nsh, tk, ns), lambda i, j: (0, j, 0))
        for w in widths:
            assert w % ns == 0
    else:
        K, N = b.shape
        ns = None
        b_spec = pl.BlockSpec((tk, N), lambda i, j: (j, 0))
    n_a = len(a_list)

    def body(*refs):
        a_refs, b_ref = refs[:n_a], refs[n_a]
        o_ref = refs[-1]
        acc = None
        off = 0
        for a_ref, w in zip(a_refs, widths):
            if sharded:
                for p in range(w // ns):
                    t = _dot_nt(a_ref[:, p * ns:(p + 1) * ns].astype(BF16), b_ref[off // ns + p])
                    acc = t if acc is None else acc + t
            else:
                t = _dot_nt(a_ref[...].astype(BF16), b_ref[:, off:off + w])
                acc = t if acc is None else acc + t
            off += w
        if res is not None:
            acc = acc + alpha * refs[n_a + 1][...]
        o_ref[...] = acc.astype(o_ref.dtype)

    in_specs = [pl.BlockSpec((tm, w), lambda i, j: (i, 0)) for w in widths] + [b_spec]
    args = list(a_list) + [b]
    if res is not None:
        in_specs.append(pl.BlockSpec((tm, tk), lambda i, j: (i, j)))
        args.append(res)
    blk = (sum(_nbytes((tm, w), a.dtype) for a, w in zip(a_list, widths)) + _nbytes((tk, sum(widths)), BF16)
           + 2 * _nbytes((tm, tk), F32))
    in_specs, out_specs, out_shape, scratch = _carry_specs(
        rider, in_specs, (pl.BlockSpec((tm, tk), lambda i, j: (i, j)),), (pltpu.HBM((M, K), out_dtype),), [])
    first = lambda: (pl.program_id(0) == 0) & (pl.program_id(1) == 0)
    last = lambda: (pl.program_id(0) == M // tm - 1) & (pl.program_id(1) == K // tk - 1)
    res_all = _pallas(
        _carry(rider, body, len(args), 1, first, last), name=name, out_shape=out_shape, grid=(M // tm, K // tk),
        in_specs=in_specs, out_specs=out_specs, scratch_shapes=scratch,
        compiler_params=_params(blk, ("arbitrary", "arbitrary")))(*args, *(rider.arrays if rider else ()))
    return res_all[0] if rider is None else (res_all[0], list(res_all[1:]))


def mm_nt_ln_bwd(a_list, b, res, alpha, zh, rstd, gamma, *, tm, name, rider=None):
    M, D = res.shape
    widths = [a.shape[1] for a in a_list]
    sharded = b.ndim == 3
    if sharded:
        nsh, _, ns = b.shape
        b_spec = pl.BlockSpec((nsh, D, ns), lambda i: (0, 0, 0))
    else:
        ns = None
        b_spec = pl.BlockSpec((D, b.shape[1]), lambda i: (0, 0))
    n_a = len(a_list)

    def body(*refs):
        a_refs, b_ref = refs[:n_a], refs[n_a]
        res_ref, zh_ref, rs_ref, g_ref = refs[n_a + 1:n_a + 5]
        dz_ref, dzb_ref, dg_ref, db_ref = refs[n_a + 5:]

        @pl.when(pl.program_id(0) == 0)
        def _():
            dg_ref[...] = jnp.zeros_like(dg_ref)
            db_ref[...] = jnp.zeros_like(db_ref)

        dy = alpha * res_ref[...]
        off = 0
        for a_ref, w in zip(a_refs, widths):
            if sharded:
                for p in range(w // ns):
                    dy = dy + _dot_nt(a_ref[:, p * ns:(p + 1) * ns], b_ref[off // ns + p])
            else:
                dy = dy + _dot_nt(a_ref[...], b_ref[:, off:off + w])
            off += w
        zhv = zh_ref[...]
        dg_ref[...] += jnp.sum(dy * zhv, axis=0, keepdims=True)
        db_ref[...] += jnp.sum(dy, axis=0, keepdims=True)
        dzh = dy * g_ref[...]
        m1 = jnp.mean(dzh, axis=-1, keepdims=True)
        m2 = jnp.mean(dzh * zhv, axis=-1, keepdims=True)
        dz = rs_ref[...] * (dzh - m1 - zhv * m2)
        dz_ref[...] = dz
        dzb_ref[...] = dz.astype(BF16)

    row = lambda i: (i, 0)
    fix = lambda i: (0, 0)
    in_specs = [pl.BlockSpec((tm, w), row) for w in widths] + [
        b_spec, pl.BlockSpec((tm, D), row), pl.BlockSpec((tm, D), row), pl.BlockSpec((tm, 1), row),
        pl.BlockSpec((1, D), fix)]
    blk = (sum(_nbytes((tm, w), BF16) for w in widths) + _nbytes((D, sum(widths)), BF16)
           + 5 * _nbytes((tm, D), F32))
    in_specs, out_specs, out_shape, scratch = _carry_specs(
        rider, in_specs,
        (pl.BlockSpec((tm, D), row), pl.BlockSpec((tm, D), row), pl.BlockSpec((1, D), fix), pl.BlockSpec((1, D), fix)),
        (pltpu.HBM((M, D), F32), pltpu.HBM((M, D), BF16), pltpu.HBM((1, D), F32), pltpu.HBM((1, D), F32)), [])
    first = lambda: pl.program_id(0) == 0
    last = lambda: pl.program_id(0) == M // tm - 1
    res_all = _pallas(
        _carry(rider, body, n_a + 5, 4, first, last), name=name, grid=(M // tm,), out_shape=out_shape,
        in_specs=in_specs, out_specs=out_specs, scratch_shapes=scratch,
        compiler_params=_params(blk, ("arbitrary",)))(
            *a_list, b, res, zh, rstd, gamma, *(rider.arrays if rider else ()))
    return res_all[:4] if rider is None else (res_all[:4], list(res_all[4:]))


def mm_tn(a, b_list, *, tk, tn, name, shard_width=None, tmc=None):
    M, K = a.shape
    tmc = M if tmc is None else tmc
    nm = M // tmc
    widths = [b.shape[1] for b in b_list]
    N = sum(widths)
    starts, s = [], 0
    for w in widths:
        assert w % tn == 0
        starts.append(s)
        s += w // tn
    n_b = len(b_list)

    def body(*refs):
        a_ref, b_refs, o_ref, acc = refs[0], refs[1:1 + n_b], refs[-2], refs[-1]
        j, m = pl.program_id(1), pl.program_id(2)
        for b_ref, st, w in zip(b_refs, starts, widths):
            @pl.when((j >= st) & (j < st + w // tn))
            def _(b_ref=b_ref):
                t = _dot_tn(a_ref[...].astype(BF16), b_ref[...].astype(BF16))
                if nm == 1:
                    o_ref[...] = t.astype(o_ref.dtype)
                else:
                    @pl.when(m == 0)
                    def _():
                        acc[...] = t

                    @pl.when(m > 0)
                    def _():
                        acc[...] += t

                    @pl.when(m == nm - 1)
                    def _():
                        o_ref[...] = acc[...].astype(o_ref.dtype)

    def b_map(st, w):
        nb = w // tn
        return lambda i, j, m: (jnp.where((j >= st) & (j < st + nb), m, 0), jnp.clip(j - st, 0, nb - 1))

    in_specs = [pl.BlockSpec((tmc, tk), lambda i, j, m: (m, i))]
    in_specs += [pl.BlockSpec((tmc, tn), b_map(st, w)) for st, w in zip(starts, widths)]
    if shard_width is None:
        out_shape = pltpu.HBM((K, N), BF16)
        out_spec = pl.BlockSpec((tk, tn), lambda i, j, m: (i, j))
    else:
        per = shard_width // tn
        out_shape = pltpu.HBM((N // shard_width, K, shard_width), BF16)
        out_spec = pl.BlockSpec((None, tk, tn), lambda i, j, m: (j // per, i, j % per))
    acc_shape = (tk, tn) if nm > 1 else (8, LANES)
    blk = (_nbytes((tmc, tk), a.dtype) + n_b * _nbytes((tmc, tn), b_list[0].dtype) + 2 * _nbytes((tk, tn), F32))
    return _pallas(
        body, name=name, out_shape=out_shape, grid=(K // tk, N // tn, nm), in_specs=in_specs, out_specs=out_spec,
        scratch_shapes=[pltpu.VMEM(acc_shape, F32)],
        compiler_params=_params(blk, ("parallel", "arbitrary", "arbitrary")))(a, *b_list)


CONV_PAD = 32
CONV_CHUNK = 128


def _rows(win, off, n, shifts):
    b, a = off % 8, off // 8
    if b not in shifts:
        shifts[b] = win if b == 0 else win[b:b + n + CONV_PAD - 8, :]
    return shifts[b][8 * a:8 * a + n, :]


def _by_residue(n_taps, offset):
    return sorted(range(n_taps), key=lambda k: (offset(k) % 8, k))


def conv_fwd(proj, conv_w, conv_b, *, name, rider=None):
    S = proj.shape[0]
    KW, C = conv_w.shape
    nct = C // LANES
    rc = min(CONV_CHUNK, S)

    def body(a_ref, g_ref, w_ref, b_ref, o_ref, pad):
        pad[0:CONV_PAD, :] = jnp.zeros((CONV_PAD, LANES), F32)
        pad[CONV_PAD:, :] = a_ref[...] * _sigmoid(g_ref[...])
        w = w_ref[...]
        bias = b_ref[...]

        def chunk(i, _):
            base = pl.multiple_of(i * rc, rc)
            win = pad[pl.ds(base, rc + CONV_PAD), :]
            acc = jnp.zeros((rc, LANES), F32) + bias
            shifts = {}
            for k in _by_residue(KW, lambda k: CONV_PAD - (KW - 1) + k):
                acc = acc + w[k:k + 1, :] * _rows(win, CONV_PAD - (KW - 1) + k, rc, shifts)
            o_ref[pl.ds(base, rc), :] = acc
            return 0

        lax.fori_loop(0, S // rc, chunk, 0)

    in_specs, out_specs, out_shape, scratch = _carry_specs(
        rider, [pl.BlockSpec((S, LANES), lambda c: (0, c)), pl.BlockSpec((S, LANES), lambda c: (0, c + nct)),
                pl.BlockSpec((KW, LANES), lambda c: (0, c)), pl.BlockSpec((1, LANES), lambda c: (0, c))],
        (pl.BlockSpec((S, LANES), lambda c: (0, c)),), (pltpu.HBM((S, C), F32),),
        [pltpu.VMEM((S + CONV_PAD, LANES), F32)])
    first = lambda: pl.program_id(0) == 0
    last = lambda: pl.program_id(0) == nct - 1
    res = _pallas(
        _carry(rider, body, 4, 1, first, last), name=name, grid=(nct,), out_shape=out_shape,
        in_specs=in_specs, out_specs=out_specs, scratch_shapes=scratch,
        compiler_params=_params(4 * _nbytes((S, LANES), F32), ("arbitrary",)))(
            proj, proj, conv_w, conv_b, *(rider.arrays if rider else ()))
    return res[0], list(res[1:])


def conv_bwd(du1, proj, conv_w, *, name, rider=None):
    S = proj.shape[0]
    KW, C = conv_w.shape
    nct = C // LANES
    rc = min(CONV_CHUNK, S)

    def body(d_ref, a_ref, g_ref, w_ref, da_ref, dg_ref, dw_ref, db_ref, pad_u, pad_d, du0, dw_acc):
        dw_acc[...] = jnp.zeros_like(dw_acc)
        pad_u[0:CONV_PAD, :] = jnp.zeros((CONV_PAD, LANES), F32)
        pad_u[CONV_PAD:, :] = a_ref[...] * _sigmoid(g_ref[...])
        pad_d[0:S, :] = d_ref[...]
        pad_d[S:, :] = jnp.zeros((CONV_PAD, LANES), F32)
        w = w_ref[...]
        db_ref[...] = jnp.sum(d_ref[...], axis=0, keepdims=True)

        def chunk(i, _):
            base = pl.multiple_of(i * rc, rc)
            d = pad_d[pl.ds(base, rc), :]
            win_u = pad_u[pl.ds(base, rc + CONV_PAD), :]
            win_d = pad_d[pl.ds(base, rc + CONV_PAD), :]
            shifts = {}
            for k in _by_residue(KW, lambda k: CONV_PAD - (KW - 1) + k):
                u_k = _rows(win_u, CONV_PAD - (KW - 1) + k, rc, shifts)
                dw_acc[k:k + 1, :] += jnp.sum(d * u_k, axis=0, keepdims=True)
            acc = jnp.zeros((rc, LANES), F32)
            shifts = {}
            for k in _by_residue(KW, lambda k: KW - 1 - k):
                acc = acc + w[k:k + 1, :] * _rows(win_d, KW - 1 - k, rc, shifts)
            du0[pl.ds(base, rc), :] = acc
            return 0

        lax.fori_loop(0, S // rc, chunk, 0)
        dw_ref[...] = dw_acc[0:KW, :]
        a, sg = a_ref[...], _sigmoid(g_ref[...])
        d0 = du0[...]
        da_ref[...] = (d0 * sg).astype(BF16)
        dg_ref[...] = (d0 * a * sg * (1.0 - sg)).astype(BF16)

    col = lambda c: (0, c)
    in_specs, out_specs, out_shape, scratch = _carry_specs(
        rider, [pl.BlockSpec((S, LANES), col), pl.BlockSpec((S, LANES), col),
                pl.BlockSpec((S, LANES), lambda c: (0, c + nct)), pl.BlockSpec((KW, LANES), col)],
        (pl.BlockSpec((S, LANES), col), pl.BlockSpec((S, LANES), col), pl.BlockSpec((KW, LANES), col),
         pl.BlockSpec((1, LANES), col)),
        (pltpu.HBM((S, C), BF16), pltpu.HBM((S, C), BF16), pltpu.HBM((KW, C), F32), pltpu.HBM((1, C), F32)),
        [pltpu.VMEM((S + CONV_PAD, LANES), F32), pltpu.VMEM((S + CONV_PAD, LANES), F32),
         pltpu.VMEM((S, LANES), F32), pltpu.VMEM((CONV_PAD, LANES), F32)])
    first = lambda: pl.program_id(0) == 0
    last = lambda: pl.program_id(0) == nct - 1
    res = _pallas(
        _carry(rider, body, 4, 4, first, last), name=name, grid=(nct,), out_shape=out_shape,
        in_specs=in_specs, out_specs=out_specs, scratch_shapes=scratch,
        compiler_params=_params(8 * _nbytes((S, LANES), F32), ("arbitrary",)))(
            du1, proj, proj, conv_w, *(rider.arrays if rider else ()))
    return res[:4], list(res[4:])


def ln_silu(u1, o_sb, gamma, beta, *, tm, name):
    S, C = u1.shape

    def body(u_ref, o_ref, g_ref, b_ref, out_ref):
        z = u_ref[...]
        mu = jnp.mean(z, axis=-1, keepdims=True)
        zc = z - mu
        y = zc * lax.rsqrt(jnp.mean(zc * zc, axis=-1, keepdims=True) + LN_EPS) * g_ref[...] + b_ref[...]
        out_ref[:, 0:C] = (y * _sigmoid(y)).astype(BF16)
        out_ref[:, C:] = o_ref[...].astype(BF16)

    row = lambda i: (i, 0)
    fix = lambda i: (0, 0)
    return _pallas(
        body, name=name, out_shape=pltpu.HBM((S, 2 * C), BF16), grid=(S // tm,),
        in_specs=[pl.BlockSpec((tm, C), row), pl.BlockSpec((tm, C), row), pl.BlockSpec((1, C), fix),
                  pl.BlockSpec((1, C), fix)],
        out_specs=pl.BlockSpec((tm, 2 * C), row),
        compiler_params=_params(4 * _nbytes((tm, C), F32), ("parallel",)))(u1, o_sb, gamma, beta)


def ln_silu_bwd(dua, u1, gamma, beta, *, tm, name):
    S, C = u1.shape

    def body(d_ref, u_ref, g_ref, b_ref, du1_ref, dg_ref, db_ref):
        @pl.when(pl.program_id(0) == 0)
        def _():
            dg_ref[...] = jnp.zeros_like(dg_ref)
            db_ref[...] = jnp.zeros_like(db_ref)

        z = u_ref[...]
        mu = jnp.mean(z, axis=-1, keepdims=True)
        zc = z - mu
        rstd = lax.rsqrt(jnp.mean(zc * zc, axis=-1, keepdims=True) + LN_EPS)
        zh = zc * rstd
        y = zh * g_ref[...] + b_ref[...]
        sg = _sigmoid(y)
        dy = d_ref[...] * (sg * (1.0 + y * (1.0 - sg)))
        dg_ref[...] += jnp.sum(dy * zh, axis=0, keepdims=True)
        db_ref[...] += jnp.sum(dy, axis=0, keepdims=True)
        dzh = dy * g_ref[...]
        m1 = jnp.mean(dzh, axis=-1, keepdims=True)
        m2 = jnp.mean(dzh * zh, axis=-1, keepdims=True)
        du1_ref[...] = rstd * (dzh - m1 - zh * m2)

    row = lambda i: (i, 0)
    fix = lambda i: (0, 0)
    return _pallas(
        body, name=name, grid=(S // tm,),
        out_shape=(pltpu.HBM((S, C), F32), pltpu.HBM((1, C), F32),
                   pltpu.HBM((1, C), F32)),
        in_specs=[pl.BlockSpec((tm, C), row), pl.BlockSpec((tm, C), row), pl.BlockSpec((1, C), fix),
                  pl.BlockSpec((1, C), fix)],
        out_specs=(pl.BlockSpec((tm, C), row), pl.BlockSpec((1, C), fix), pl.BlockSpec((1, C), fix)),
        compiler_params=_params(4 * _nbytes((tm, C), F32), ("arbitrary",)))(dua, u1, gamma, beta)


SB_BLOCK = 256
SB_STOP = -105.0
SB_GROUP = 4


def _split_dot(x, tri):
    hi = x.astype(BF16)
    lo = (x - hi.astype(F32)).astype(BF16)
    return _dot(hi, tri) + _dot(lo, tri)


def _neg_softplus(z):
    return -(jnp.maximum(z, 0.0) + jnp.log(1.0 + jnp.exp(-jnp.abs(z))))


def sb_fwd(proj, *, q_col, name, rider=None):
    S = proj.shape[0]
    dh = LANES // 2
    W = SB_HEADS * dh
    BW = SB_GROUP * dh
    ngrp = W // BW
    T = min(SB_BLOCK, S)
    nblk = S // T
    scale = dh ** -0.5
    qb0 = q_col // BW
    heads = range(SB_GROUP)
    sl = [slice(h * dh, (h + 1) * dh) for h in heads]

    def body(q_ref, k_ref, v_ref, o_ref, l_ref, qs):
        r_i = lax.broadcasted_iota(jnp.int32, (T, T), 0)
        c_i = lax.broadcasted_iota(jnp.int32, (T, T), 1)
        tri = (r_i >= c_i).astype(BF16)
        vis = c_i < r_i
        lane = lax.broadcasted_iota(jnp.int32, (T, dh), 1)

        qs[...] = (q_ref[...] * scale).astype(BF16)

        def step(qb, blocks, st):
            nb = range(len(blocks))
            kb = [[k_ref[pl.ds(j0, T), sl[h]].astype(BF16) for h in heads] for j0, _ in blocks]
            vb = [[v_ref[pl.ds(j0, T), sl[h]].astype(BF16) for h in heads] for j0, _ in blocks]
            z = [[_dot_nt(qb[h], kb[b][h]) for h in heads] for b in nb]
            lk = [[_neg_softplus(z[b][h]) for h in heads] for b in nb]
            lk = [[jnp.where(vis, lk[b][h], 0.0) if blocks[b][1] else lk[b][h] for h in heads] for b in nb]
            C = [[_split_dot(lk[b][h], tri) for h in heads] for b in nb]
            R = [[st[2 * h + 1] for h in heads]]
            for b in nb:
                R.append([R[b][h] + C[b][h][:, 0:1] for h in heads])
            A = [[jnp.exp(z[b][h] + C[b][h] + R[b][h]) for h in heads] for b in nb]
            A = [[jnp.where(vis, A[b][h], 0.0) if blocks[b][1] else A[b][h] for h in heads] for b in nb]
            out = ()
            for h in heads:
                acc = st[2 * h]
                for b in nb:
                    acc = acc + _dot(A[b][h].astype(BF16), vb[b][h])
                out += (acc, R[-1][h])
            return out

        zero = (jnp.zeros((T, dh), F32), jnp.zeros((T, 1), F32))

        def finish(r0, i, c):
            walked = jnp.asarray(i - c[0]).astype(F32)
            for h in heads:
                o_ref[pl.ds(r0, T), sl[h]] = c[1 + 2 * h]
                l_ref[pl.ds(r0, T), sl[h]] = jnp.where(lane == 1, walked, c[2 + 2 * h])

        finish(0, 0, (-1,) + step([qs[0:T, sl[h]] for h in heads], [(0, True)], zero * SB_GROUP))

        def qblock(i, _):
            r0 = pl.multiple_of(i * T, T)
            qb = [qs[pl.ds(r0, T), sl[h]] for h in heads]
            state = step(qb, [(r0, True), (pl.multiple_of(r0 - T, T), False)], zero * SB_GROUP)

            def more(c):
                worst = c[2]
                for h in heads[1:]:
                    worst = jnp.maximum(worst, c[2 + 2 * h])
                return (c[0] >= 0) & (jnp.max(worst) >= SB_STOP)

            def walk(c):
                return (c[0] - 1,) + step(qb, [(pl.multiple_of(c[0] * T, T), False)], c[1:])

            finish(r0, i, lax.while_loop(more, walk, (i - 2,) + state))
            return 0

        lax.fori_loop(1, nblk, qblock, 0)

    blk = lambda off: pl.BlockSpec((S, BW), lambda g: (0, qb0 + off * ngrp + g), pipeline_mode=pl.Buffered(1))
    out = pl.BlockSpec((S, BW), lambda g: (0, g))
    in_specs, out_specs, out_shape, scratch = _carry_specs(
        rider, [blk(0), blk(1), blk(2)], (out, out), (pltpu.HBM((S, W), F32), pltpu.HBM((S, W), F32)),
        [pltpu.VMEM((S, BW), BF16)])
    first = lambda: pl.program_id(0) == 0
    last = lambda: pl.program_id(0) == ngrp - 1
    res = _pallas(
        _carry(rider, body, 3, 2, first, last), name=name, grid=(ngrp,), out_shape=out_shape,
        in_specs=in_specs, out_specs=out_specs, scratch_shapes=scratch,
        compiler_params=_params(5 * _nbytes((S, BW), F32), ("arbitrary",)))(
            proj, proj, proj, *(rider.arrays if rider else ()))
    return res[0], res[1], list(res[2:])


def sb_bwd(proj, ltot, dua, *, q_col, do_col, name, rider=None):
    S = proj.shape[0]
    dh = LANES // 2
    W = SB_HEADS * dh
    BW = SB_GROUP * dh
    ngrp = W // BW
    T = min(SB_BLOCK, S)
    nblk = S // T
    scale = dh ** -0.5
    qb0 = q_col // BW
    db0 = do_col // BW
    heads = range(SB_GROUP)
    sl = [slice(h * dh, (h + 1) * dh) for h in heads]

    def body(q_ref, k_ref, v_ref, l_ref, do_ref, dq_ref, dk_ref, dv_ref, dks, dvs):
        r_i = lax.broadcasted_iota(jnp.int32, (T, T), 0)
        c_i = lax.broadcasted_iota(jnp.int32, (T, T), 1)
        tri_rev = (r_i >= c_i).astype(BF16)
        tri_fwd = (r_i <= c_i).astype(BF16)
        vis = c_i < r_i

        dks[...] = jnp.zeros_like(dks)
        dvs[...] = jnp.zeros_like(dvs)

        def step(qb, dob, Lt, blocks, st):
            nb = range(len(blocks))
            kb = [[k_ref[pl.ds(j0, T), sl[h]].astype(BF16) for h in heads] for j0, _ in blocks]
            vb = [[v_ref[pl.ds(j0, T), sl[h]].astype(BF16) for h in heads] for j0, _ in blocks]
            z = [[_dot_nt(qb[h], kb[b][h]) for h in heads] for b in nb]
            dA =[[_dot_nt(dob[h], vb[b][h]) for h in heads] for b in nb]
            lk = [[_neg_softplus(z[b][h]) for h in heads] for b in nb]
            beta = [[jnp.exp(z[b][h] + lk[b][h]) for h in heads] for b in nb]
            lk = [[jnp.where(vis, lk[b][h], 0.0) if blocks[b][1] else lk[b][h] for h in heads] for b in nb]
            C = [[_split_dot(lk[b][h], tri_rev) for h in heads] for b in nb]
            P = [[st[3 * h + 1] for h in heads]]
            for b in nb:
                P.append([P[b][h] + C[b][h][:, 0:1] for h in heads])
            A = [[jnp.exp(z[b][h] + C[b][h] + (Lt[h] - P[b + 1][h])) for h in heads] for b in nb]
            A = [[jnp.where(vis, A[b][h], 0.0) if blocks[b][1] else A[b][h] for h in heads] for b in nb]
            g = [[A[b][h] * dA[b][h] for h in heads] for b in nb]
            Gin = [[_split_dot(g[b][h], tri_fwd) for h in heads] for b in nb]
            Gp = [[st[3 * h + 2] for h in heads]]
            for b in nb:
                Gp.append([Gp[b][h] + Gin[b][h][:, T - 1:T] for h in heads])
            dz = [[g[b][h] - beta[b][h] * (Gp[b][h] + Gin[b][h]) for h in heads] for b in nb]
            dz = [[jnp.where(vis, dz[b][h], 0.0) if blocks[b][1] else dz[b][h] for h in heads] for b in nb]
            dzb = [[dz[b][h].astype(BF16) for h in heads] for b in nb]
            out = ()
            for h in heads:
                dq = st[3 * h]
                for b in nb:
                    j0 = blocks[b][0]
                    dvs[pl.ds(j0, T), sl[h]] += _dot_tn(A[b][h].astype(BF16), dob[h])
                    dks[pl.ds(j0, T), sl[h]] += _dot_tn(dzb[b][h], qb[h])
                    dq = dq + _dot(dzb[b][h], kb[b][h])
                out += (dq, P[-1][h], Gp[-1][h])
            return out

        zero = jnp.zeros((T, 1), F32)
        init = (jnp.zeros((T, dh), F32), zero, zero)

        def operands(r0):
            return ([(q_ref[pl.ds(r0, T), sl[h]] * scale).astype(BF16) for h in heads],
                    [do_ref[pl.ds(r0, T), sl[h]].astype(BF16) for h in heads],
                    [l_ref[pl.ds(r0, T), h * dh:h * dh + 1] for h in heads])

        def finish(r0, c):
            for h in heads:
                dq_ref[pl.ds(r0, T), sl[h]] = (c[3 * h] * scale).astype(BF16)

        finish(0, step(*operands(0), [(0, True)], init * SB_GROUP))

        def qblock(i, _):
            r0 = pl.multiple_of(i * T, T)
            qb, dob, Lt = operands(r0)
            walked = jnp.clip(jnp.max(l_ref[pl.ds(r0, 8), 1:2]).astype(jnp.int32), 2, i + 1)

            def inner(j, c):
                return step(qb, dob, Lt, [(pl.multiple_of(j * T, T), False)], c)

            c = lax.fori_loop(i + 1 - walked, i - 1, inner, init * SB_GROUP)
            finish(r0, step(qb, dob, Lt, [(pl.multiple_of(r0 - T, T), False), (r0, True)], c))
            return 0

        lax.fori_loop(1, nblk, qblock, 0)
        dk_ref[...] = dks[...].astype(BF16)
        dv_ref[...] = dvs[...].astype(BF16)

    once = pl.Buffered(1)
    blk = lambda off: pl.BlockSpec((S, BW), lambda g: (0, qb0 + off * ngrp + g), pipeline_mode=once)
    out = pl.BlockSpec((S, BW), lambda g: (0, g))
    o_shape = pltpu.HBM((S, W), BF16)
    in_specs, out_specs, out_shape, scratch = _carry_specs(
        rider, [blk(0), blk(1), blk(2), pl.BlockSpec((S, BW), lambda g: (0, g), pipeline_mode=once),
                pl.BlockSpec((S, BW), lambda g: (0, db0 + g), pipeline_mode=once)], (out, out, out),
        (o_shape, o_shape, o_shape), [pltpu.VMEM((S, BW), F32)] * 2)
    first = lambda: pl.program_id(0) == 0
    last = lambda: pl.program_id(0) == ngrp - 1
    res = _pallas(
        _carry(rider, body, 5, 3, first, last), name=name, grid=(ngrp,), out_shape=out_shape,
        in_specs=in_specs, out_specs=out_specs, scratch_shapes=scratch,
        compiler_params=_params(6 * _nbytes((S, BW), F32), ("arbitrary",)))(
            proj, proj, proj, ltot, dua, *(rider.arrays if rider else ()))
    return res[0], res[1], res[2], list(res[3:])


def xattn_fwd(q, k, v, *, tm, name):
    S, D = q.shape
    Mlen = k.shape[0]
    hd = D // MEM_HEADS
    scale = hd ** -0.5

    def body(q_ref, k_ref, v_ref, o_ref):
        for h in range(MEM_HEADS):
            sl = slice(h * hd, (h + 1) * hd)
            s = _dot_nt(q_ref[:, sl], k_ref[:, sl]) * scale
            e = jnp.exp(s - jnp.max(s, axis=-1, keepdims=True))
            p = e / jnp.sum(e, axis=-1, keepdims=True)
            o_ref[:, sl] = _dot(p.astype(BF16), v_ref[:, sl]).astype(BF16)

    row = lambda i: (i, 0)
    fix = lambda i: (0, 0)
    return _pallas(
        body, name=name, out_shape=pltpu.HBM((S, D), BF16), grid=(S // tm,),
        in_specs=[pl.BlockSpec((tm, D), row), pl.BlockSpec((Mlen, D), fix), pl.BlockSpec((Mlen, D), fix)],
        out_specs=pl.BlockSpec((tm, D), row),
        compiler_params=_params(4 * _nbytes((tm, D), F32), ("parallel",)))(q, k, v)


def xattn_bwd(q, do, k, v, *, tm, name):
    S, D = q.shape
    Mlen = k.shape[0]
    hd = D // MEM_HEADS
    scale = hd ** -0.5

    def body(q_ref, do_ref, k_ref, v_ref, dq_ref, dk_ref, dv_ref):
        @pl.when(pl.program_id(0) == 0)
        def _():
            dk_ref[...] = jnp.zeros_like(dk_ref)
            dv_ref[...] = jnp.zeros_like(dv_ref)

        for h in range(MEM_HEADS):
            sl = slice(h * hd, (h + 1) * hd)
            qh, doh, kh, vh = q_ref[:, sl], do_ref[:, sl], k_ref[:, sl], v_ref[:, sl]
            s = _dot_nt(qh, kh) * scale
            e = jnp.exp(s - jnp.max(s, axis=-1, keepdims=True))
            p = e / jnp.sum(e, axis=-1, keepdims=True)
            dp = _dot_nt(doh, vh)
            ds = (p * (dp - jnp.sum(p * dp, axis=-1, keepdims=True)) * scale).astype(BF16)
            dq_ref[:, sl] = _dot(ds, kh).astype(BF16)
            dk_ref[:, sl] += _dot_tn(ds, qh)
            dv_ref[:, sl] += _dot_tn(p.astype(BF16), doh)

    row = lambda i: (i, 0)
    fix = lambda i: (0, 0)
    return _pallas(
        body, name=name, grid=(S // tm,),
        out_shape=(pltpu.HBM((S, D), BF16), pltpu.HBM((Mlen, D), F32),
                   pltpu.HBM((Mlen, D), F32)),
        in_specs=[pl.BlockSpec((tm, D), row), pl.BlockSpec((tm, D), row), pl.BlockSpec((Mlen, D), fix),
                  pl.BlockSpec((Mlen, D), fix)],
        out_specs=(pl.BlockSpec((tm, D), row), pl.BlockSpec((Mlen, D), fix), pl.BlockSpec((Mlen, D), fix)),
        compiler_params=_params(6 * _nbytes((tm, D), F32), ("arbitrary",)))(q, do, k, v)


FFN_HALO = 8


def _conv3(ext, w, lo):
    tm = ext.shape[0] - FFN_HALO
    return (w[0:1, :] * ext[lo:lo + tm, :] + w[1:2, :] * ext[lo + 1:lo + 1 + tm, :]
            + w[2:3, :] * ext[lo + 2:lo + 2 + tm, :])


def ffn_up_fwd(xb, w_up, conv_w, conv_b, *, tm, tn, name, rider=None):
    S, D = xb.shape
    nsh, _, ns = w_up.shape
    F = nsh * ns // 2
    per = ns // tn
    ncol = F // tn
    KW = conv_w.shape[0]
    assert KW == 3

    def body(x_ref, wv_ref, wg_ref, cwv_ref, cwg_ref, cbv_ref, cbg_ref, uv_ref, ug_ref, mv_ref, mg_ref, h_ref,
             carry):
        @pl.when(pl.program_id(1) == 0)
        def _():
            carry[...] = jnp.zeros_like(carry)

        x = x_ref[...]
        uv = _dot(x, wv_ref[...])
        ug = _dot(x, wg_ref[...])
        uv_ref[...] = uv.astype(BF16)
        ug_ref[...] = ug.astype(BF16)
        lo = FFN_HALO - (KW - 1)
        cv = _conv3(jnp.concatenate([carry[0], uv], axis=0), cwv_ref[...], lo) + cbv_ref[...]
        cg = _conv3(jnp.concatenate([carry[1], ug], axis=0), cwg_ref[...], lo) + cbg_ref[...]
        carry[0] = uv[tm - FFN_HALO:, :]
        carry[1] = ug[tm - FFN_HALO:, :]
        sg = _sigmoid(cg)
        act = cg * sg
        mv_ref[...] = act.astype(BF16)
        mg_ref[...] = (cv * (sg + act * (1.0 - sg))).astype(BF16)
        h_ref[...] = (act * cv).astype(BF16)

    wspec = lambda half: pl.BlockSpec((None, D, tn), lambda j, i: (half * (nsh // 2) + j // per, 0, j % per))
    cspec = lambda rows, half: pl.BlockSpec((rows, tn), lambda j, i: (0, half * ncol + j))
    out = pl.BlockSpec((tm, tn), lambda j, i: (i, j))
    o_shape = pltpu.HBM((S, F), BF16)
    blk = _nbytes((tm, D), BF16) + 2 * _nbytes((D, tn), BF16) + 8 * _nbytes((tm, tn), F32)
    nrow = S // tm
    in_specs, out_specs, out_shape, scratch = _carry_specs(
        rider, [pl.BlockSpec((tm, D), lambda j, i: (i, 0)), wspec(0), wspec(1), cspec(KW, 0), cspec(KW, 1),
                cspec(1, 0), cspec(1, 1)], (out,) * 5, (o_shape,) * 5, [pltpu.VMEM((2, FFN_HALO, tn), F32)])
    first = lambda: (pl.program_id(0) == 0) & (pl.program_id(1) == 0)
    last = lambda: (pl.program_id(0) == ncol - 1) & (pl.program_id(1) == nrow - 1)
    res = _pallas(
        _carry(rider, body, 7, 5, first, last), name=name, grid=(ncol, nrow), out_shape=out_shape,
        in_specs=in_specs, out_specs=out_specs, scratch_shapes=scratch,
        compiler_params=_params(blk, ("arbitrary", "arbitrary")))(
            xb, w_up, w_up, conv_w, conv_w, conv_b, conv_b, *(rider.arrays if rider else ()))
    return res[:5], list(res[5:])


def ffn_mid_bwd(dzb, w_down, up_v, up_g, mult_v, mult_g, conv_w, *, tm, tn, name, rider=None):
    S, D = dzb.shape
    F = up_v.shape[1]
    ncol = F // tn
    nrow = S // tm
    KW = conv_w.shape[0]
    assert KW == 3

    def body(dz_ref, wd_ref, uv_ref, ug_ref, mv_ref, mg_ref, cwv_ref, cwg_ref,
             dv_ref, dg_ref, dwv_ref, dwg_ref, dbv_ref, dbg_ref, carry):
        @pl.when(pl.program_id(1) == 0)
        def _():
            carry[...] = jnp.zeros_like(carry)
            for r in (dwv_ref, dwg_ref, dbv_ref, dbg_ref):
                r[...] = jnp.zeros_like(r)

        dh = _dot_nt(dz_ref[...], wd_ref[...])
        dcv = dh * mv_ref[...].astype(F32)
        dcg = dh * mg_ref[...].astype(F32)

        def back(dc, u_ref, cw, slot, du_ref, dw_ref, db_ref):
            ext = jnp.concatenate([dc, carry[slot]], axis=0)
            ahead = [dc, ext[1:tm + 1, :], ext[2:tm + 2, :]]
            du = cw[2:3, :] * ahead[0] + cw[1:2, :] * ahead[1] + cw[0:1, :] * ahead[2]
            du_ref[...] = du.astype(BF16)
            carry[slot] = dc[0:FFN_HALO, :]
            u = u_ref[...].astype(F32)
            for k in range(KW):
                dw_ref[k:k + 1, :] += jnp.sum(ahead[KW - 1 - k] * u, axis=0, keepdims=True)
            db_ref[...] += jnp.sum(dc, axis=0, keepdims=True)

        back(dcv, uv_ref, cwv_ref[...], 0, dv_ref, dwv_ref, dbv_ref)
        back(dcg, ug_ref, cwg_ref[...], 1, dg_ref, dwg_ref, dbg_ref)

    rev = lambda i: nrow - 1 - i
    tile = pl.BlockSpec((tm, tn), lambda j, i: (rev(i), j))
    cspec = lambda half: pl.BlockSpec((KW, tn), lambda j, i: (0, half * ncol + j))
    acc = lambda rows: pl.BlockSpec((rows, tn), lambda j, i: (0, j))
    big = pltpu.HBM((S, F), BF16)
    blk = _nbytes((tm, D), BF16) + _nbytes((tn, D), BF16) + 10 * _nbytes((tm, tn), F32)
    in_specs, out_specs, out_shape, scratch = _carry_specs(
        rider, [pl.BlockSpec((tm, D), lambda j, i: (rev(i), 0)), pl.BlockSpec((tn, D), lambda j, i: (j, 0)),
                tile, tile, tile, tile, cspec(0), cspec(1)],
        (tile, tile, acc(KW), acc(KW), acc(1), acc(1)),
        (big, big, pltpu.HBM((KW, F), F32), pltpu.HBM((KW, F), F32), pltpu.HBM((1, F), F32),
         pltpu.HBM((1, F), F32)), [pltpu.VMEM((2, FFN_HALO, tn), F32)])
    first = lambda: (pl.program_id(0) == 0) & (pl.program_id(1) == 0)
    last = lambda: (pl.program_id(0) == ncol - 1) & (pl.program_id(1) == nrow - 1)
    res = _pallas(
        _carry(rider, body, 8, 6, first, last), name=name, grid=(ncol, nrow), out_shape=out_shape,
        in_specs=in_specs, out_specs=out_specs, scratch_shapes=scratch,
        compiler_params=_params(blk, ("arbitrary", "arbitrary")))(
            dzb, w_down, up_v, up_g, mult_v, mult_g, conv_w, conv_w, *(rider.arrays if rider else ()))
    return res[:6], list(res[6:])


def loss_head(y, target, *, tm, name):
    S, D = y.shape

    def body(y_ref, t_ref, dy_ref, l_ref):
        @pl.when(pl.program_id(0) == 0)
        def _():
            l_ref[...] = jnp.zeros_like(l_ref)

        e = y_ref[...] - t_ref[...]
        dy_ref[...] = e * (1.0 / D)
        l_ref[...] += 0.5 * jnp.sum(jnp.mean(e * e, axis=-1, keepdims=True), axis=0, keepdims=True)

    row = lambda i: (i, 0)
    return _pallas(
        body, name=name, grid=(S // tm,),
        out_shape=(pltpu.HBM((S, D), F32), pltpu.HBM((1, 1), F32)),
        in_specs=[pl.BlockSpec((tm, D), row), pl.BlockSpec((tm, D), row)],
        out_specs=(pl.BlockSpec((tm, D), row), pl.BlockSpec((1, 1), lambda i: (0, 0))),
        compiler_params=_params(3 * _nbytes((tm, D), F32), ("arbitrary",)))(y, target)


def adamw(w, g, m, v, *, tr, name):
    R, C = w.shape
    c1 = 1.0 - ADAM_B1 ** ADAM_STEP
    c2 = 1.0 - ADAM_B2 ** ADAM_STEP

    def body(w_ref, g_ref, m_ref, v_ref, go_ref, d_ref, mo_ref, vo_ref):
        gv = g_ref[...]
        mn = ADAM_B1 * m_ref[...] + (1.0 - ADAM_B1) * gv
        vn = ADAM_B2 * v_ref[...] + (1.0 - ADAM_B2) * (gv * gv)
        go_ref[...] = gv
        mo_ref[...] = mn
        vo_ref[...] = vn
        d_ref[...] = -ADAM_LR * ((mn / c1) / (jnp.sqrt(vn / c2) + ADAM_EPS) + ADAM_WD * w_ref[...])

    spec = pl.BlockSpec((tr, C), lambda i: (i, 0))
    shape = pltpu.HBM((R, C), F32)
    return _pallas(
        body, name=name, grid=(R // tr,), out_shape=(shape,) * 4, in_specs=[spec] * 4, out_specs=(spec,) * 4,
        compiler_params=_params(8 * _nbytes((tr, C), F32), ("parallel",)))(w, g, m, v)


def add_pairs(gs, gots, core, *, name):
    k = len(gs)

    def body(c_ref, *refs):
        for a_ref, b_ref, o_ref in zip(refs[:k], refs[k:2 * k], refs[2 * k:]):
            o_ref[...] = (a_ref[...].astype(F32) + b_ref[...].astype(F32)).astype(BF16)

    own = [pl.BlockSpec((None, None) + g.shape[2:], lambda i, c: (i, c[0], 0, 0)) for g in gs]
    half = [pl.BlockSpec((None,) + g.shape[1:], lambda i, c: (i, 0, 0)) for g in gots]
    grid_spec = pltpu.PrefetchScalarGridSpec(
        num_scalar_prefetch=1, grid=(N_CHIPS,), in_specs=own + half, out_specs=tuple(half))
    blk = 3 * sum(_nbytes(g.shape[1:], BF16) for g in gots)
    return _pallas(
        body, name=name, grid_spec=grid_spec, out_shape=tuple(pltpu.HBM(g.shape, BF16) for g in gots),
        compiler_params=_params(blk, ("parallel",)))(core, *gs, *gots)


def sum_chips_into(bs, dests, layer, core, *, name):
    k = len(bs)
    steps = 2

    def body(c_ref, *refs):
        for b_ref, o_ref in zip(refs[:k], refs[2 * k:]):
            acc = b_ref[0].astype(F32)
            for p in range(1, N_CHIPS):
                acc = acc + b_ref[p].astype(F32)
            o_ref[...] = acc

    ins = [pl.BlockSpec((N_CHIPS, b.shape[1] // steps, b.shape[2]), lambda i, c: (0, i, 0)) for b in bs]
    outs = tuple(pl.BlockSpec((None, None, b.shape[1] // steps, b.shape[2]), lambda i, c: (layer, c[0], i, 0))
                 for b in bs)
    grid_spec = pltpu.PrefetchScalarGridSpec(
        num_scalar_prefetch=1, grid=(steps,), in_specs=ins + [pl.BlockSpec(memory_space=pl.ANY)] * k,
        out_specs=outs)
    blk = sum(_nbytes(b.shape, BF16) + _nbytes(b.shape[1:], F32) for b in bs) // steps
    return _pallas(
        body, name=name, grid_spec=grid_spec, out_shape=tuple(pltpu.HBM(d.shape, F32) for d in dests),
        input_output_aliases={1 + k + w: w for w in range(k)},
        compiler_params=_params(blk, ("parallel",)))(core, *bs, *dests)


_HBM = pl.BlockSpec(memory_space=pltpu.HBM)


def _place():
    x, y, c = lax.axis_index("x"), lax.axis_index("y"), lax.axis_index("c")
    chips = [(1 - x, y), (x, 1 - y), (1 - x, 1 - y)]
    return x, y, c, chips


class GatherRider:
    def __init__(self, shards):
        self.arrays = list(shards)
        self.n = n = len(shards)
        self.out_shape = tuple(pltpu.HBM((N_CHIPS,) + s.shape, s.dtype) for s in shards)
        self.scratch = [pltpu.SemaphoreType.DMA((n, 3))] * 4 + [pltpu.SemaphoreType.DMA((n,))]

    def _copies(self, ins, outs, sems):
        send_ici, recv_ici, send_d2d, recv_d2d, local = sems
        x, y, c, chips = _place()
        me = 2 * x + y

        def own(w):
            return pltpu.make_async_copy(ins[w], outs[w].at[me], local.at[w])

        def ici(w, j):
            px, py = chips[j]
            return pltpu.make_async_remote_copy(
                src_ref=ins[w].at[c], dst_ref=outs[w].at[me, c], send_sem=send_ici.at[w, j],
                recv_sem=recv_ici.at[w, j], device_id=(px, py, c), device_id_type=MESH)

        def landed(w, j, half):
            px, py = chips[j]
            return outs[w].at[2 * px + py, half]

        def d2d(w, j, half):
            return pltpu.make_async_remote_copy(
                src_ref=landed(w, j, half), dst_ref=landed(w, j, half), send_sem=send_d2d.at[w, j],
                recv_sem=recv_d2d.at[w, j], device_id=(x, y, 1 - c), device_id_type=MESH)

        def ici_arrival(w, j):
            return pltpu.make_async_remote_copy(
                src_ref=landed(w, j, c), dst_ref=landed(w, j, c), send_sem=send_ici.at[w, j],
                recv_sem=recv_ici.at[w, j], device_id=(x, y, c), device_id_type=MESH)

        return c, own, ici, d2d, ici_arrival

    def start(self, ins, outs, sems):
        c, own, ici, d2d, ici_arrival = self._copies(ins, outs, sems)
        for w in range(self.n):
            own(w).start()
            for j in range(3):
                ici(w, j).start()

    def finish(self, ins, outs, sems):
        c, own, ici, d2d, ici_arrival = self._copies(ins, outs, sems)
        for w in range(self.n):
            for j in range(3):
                ici_arrival(w, j).wait_recv()
                d2d(w, j, c).start()
        for w in range(self.n):
            for j in range(3):
                d2d(w, j, 1 - c).wait_recv()
        for w in range(self.n):
            for j in range(3):
                ici(w, j).wait_send()
                d2d(w, j, c).wait_send()
            own(w).wait()


class ScatterRider:
    def __init__(self, parts):
        self.arrays = list(parts)
        self.n = n = len(parts)
        self.out_shape = tuple(pltpu.HBM(p.shape, p.dtype) for p in parts)
        self.scratch = [pltpu.SemaphoreType.DMA((n, 3))] * 2 + [pltpu.SemaphoreType.DMA((n,))]

    def _copies(self, ins, outs, sems):
        send, recv, local = sems
        x, y, c, chips = _place()
        me = 2 * x + y

        def own(w):
            return pltpu.make_async_copy(ins[w].at[me], outs[w].at[me], local.at[w])

        def copy(w, j):
            px, py = chips[j]
            return pltpu.make_async_remote_copy(
                src_ref=ins[w].at[2 * px + py], dst_ref=outs[w].at[me], send_sem=send.at[w, j],
                recv_sem=recv.at[w, j], device_id=(px, py, c), device_id_type=MESH)

        def arrival(w, j):
            px, py = chips[j]
            blk = outs[w].at[2 * px + py]
            return pltpu.make_async_remote_copy(
                src_ref=blk, dst_ref=blk, send_sem=send.at[w, j], recv_sem=recv.at[w, j],
                device_id=(x, y, c), device_id_type=MESH)

        return own, copy, arrival

    def start(self, ins, outs, sems):
        own, copy, arrival = self._copies(ins, outs, sems)
        for w in range(self.n):
            own(w).start()
            for j in range(3):
                copy(w, j).start()

    def finish(self, ins, outs, sems):
        own, copy, arrival = self._copies(ins, outs, sems)
        for w in range(self.n):
            for j in range(3):
                arrival(w, j).wait_recv()
        for w in range(self.n):
            for j in range(3):
                copy(w, j).wait_send()
            own(w).wait()


def _carry(rider, body, n_in, n_out, first, last):
    if rider is None:
        return body
    k, m = rider.n, len(rider.scratch)

    def carried(*refs):
        ins, r_in = refs[:n_in], refs[n_in:n_in + k]
        outs, r_out = refs[n_in + k:n_in + k + n_out], refs[n_in + k + n_out:n_in + 2 * k + n_out]
        rest = refs[n_in + 2 * k + n_out:]
        scratch, sems = rest[:len(rest) - m], rest[len(rest) - m:]

        @pl.when(first())
        def _():
            rider.start(r_in, r_out, sems)

        body(*ins, *outs, *scratch)

        @pl.when(last())
        def _():
            rider.finish(r_in, r_out, sems)

    return carried


def _carry_specs(rider, in_specs, out_specs, out_shape, scratch):
    if rider is None:
        return list(in_specs), tuple(out_specs), tuple(out_shape), list(scratch)
    k = rider.n
    return (list(in_specs) + [_HBM] * k, tuple(out_specs) + (_HBM,) * k, tuple(out_shape) + rider.out_shape,
            list(scratch) + list(rider.scratch))


def run_riders(riders, *, name):
    ks = [r.n for r in riders]
    ms = [len(r.scratch) for r in riders]
    k_all = sum(ks)

    def body(*refs):
        parts, i0, o0, s0 = [], 0, k_all, 2 * k_all
        for k, m in zip(ks, ms):
            parts.append((refs[i0:i0 + k], refs[o0:o0 + k], refs[s0:s0 + m]))
            i0, o0, s0 = i0 + k, o0 + k, s0 + m
        for r, p in zip(riders, parts):
            r.start(*p)
        for r, p in zip(riders, parts):
            r.finish(*p)

    res = _pallas(
        body, name=name, out_shape=tuple(o for r in riders for o in r.out_shape), in_specs=[_HBM] * k_all,
        out_specs=(_HBM,) * k_all, scratch_shapes=[s for r in riders for s in r.scratch],
    )(*[a for r in riders for a in r.arrays])
    out, o0 = [], 0
    for k in ks:
        out.append(list(res[o0:o0 + k]))
        o0 += k
    return out


class SmallGatherRider:
    def __init__(self, shards):
        self.arrays = list(shards)
        self.n = n = len(shards)
        self.out_shape = tuple(pltpu.HBM((N_CHIPS,) + s.shape, s.dtype) for s in shards)
        self.scratch = [pltpu.SemaphoreType.DMA((n, 3))] * 2 + [pltpu.SemaphoreType.DMA((n,))]

    def _copies(self, ins, outs, sems):
        send, recv, local = sems
        x, y, c, chips = _place()
        me = 2 * x + y

        def own(w):
            return pltpu.make_async_copy(ins[w], outs[w].at[me], local.at[w])

        def copy(w, j):
            px, py = chips[j]
            return pltpu.make_async_remote_copy(
                src_ref=ins[w], dst_ref=outs[w].at[me], send_sem=send.at[w, j], recv_sem=recv.at[w, j],
                device_id=(px, py, c), device_id_type=MESH)

        def arrival(w, j):
            px, py = chips[j]
            blk = outs[w].at[2 * px + py]
            return pltpu.make_async_remote_copy(
                src_ref=blk, dst_ref=blk, send_sem=send.at[w, j], recv_sem=recv.at[w, j],
                device_id=(x, y, c), device_id_type=MESH)

        return own, copy, arrival

    def start(self, ins, outs, sems):
        own, copy, arrival = self._copies(ins, outs, sems)
        for w in range(self.n):
            own(w).start()
            for j in range(3):
                copy(w, j).start()

    def finish(self, ins, outs, sems):
        own, copy, arrival = self._copies(ins, outs, sems)
        for w in range(self.n):
            for j in range(3):
                arrival(w, j).wait_recv()
        for w in range(self.n):
            for j in range(3):
                copy(w, j).wait_send()
            own(w).wait()


class SwapRider:
    def __init__(self, grads):
        self.arrays = list(grads)
        self.n = n = len(grads)
        self.out_shape = tuple(pltpu.HBM((N_CHIPS,) + g.shape[2:], g.dtype) for g in grads)
        self.scratch = [pltpu.SemaphoreType.DMA((n,))] * 2

    def _copies(self, ins, outs, sems):
        send, recv = sems
        x, y, c, _ = _place()
        return [pltpu.make_async_remote_copy(
            src_ref=ins[w].at[:, 1 - c], dst_ref=outs[w], send_sem=send.at[w], recv_sem=recv.at[w],
            device_id=(x, y, 1 - c), device_id_type=MESH) for w in range(self.n)]

    def start(self, ins, outs, sems):
        for cp in self._copies(ins, outs, sems):
            cp.start()

    def finish(self, ins, outs, sems):
        copies = self._copies(ins, outs, sems)
        for cp in copies:
            cp.wait_recv()
        for cp in copies:
            cp.wait_send()


def rs_sibling_share(stacked, *, name):
    n = len(stacked)

    def body(*refs):
        bufs = refs[n:2 * n]
        send, recv = refs[2 * n:]
        x, y, c, _ = _place()
        shares, arrivals = [], []
        for w in range(n):
            mine, other = bufs[w].at[:, c], bufs[w].at[:, 1 - c]
            shares.append(pltpu.make_async_remote_copy(
                src_ref=mine, dst_ref=mine, send_sem=send.at[w], recv_sem=recv.at[w],
                device_id=(x, y, 1 - c), device_id_type=MESH))
            arrivals.append(pltpu.make_async_remote_copy(
                src_ref=other, dst_ref=other, send_sem=send.at[w], recv_sem=recv.at[w],
                device_id=(x, y, c), device_id_type=MESH))
        for cp in shares:
            cp.start()
        for cp in arrivals:
            cp.wait_recv()
        for cp in shares:
            cp.wait_send()

    out_shape = tuple(pltpu.HBM(s.shape, F32) for s in stacked)
    return _pallas(
        body, name=name, out_shape=out_shape, in_specs=[_HBM] * n, out_specs=(_HBM,) * n,
        input_output_aliases={w: w for w in range(n)},
        scratch_shapes=[pltpu.SemaphoreType.DMA((n,))] * 2,
    )(*stacked)


class EveryoneRider:
    def __init__(self, v):
        self.arrays = [v]
        self.n = 1
        self.out_shape = (pltpu.HBM((N_DEV,) + v.shape, v.dtype),)
        self.scratch = [pltpu.SemaphoreType.DMA((N_DEV - 1,))] * 2 + [pltpu.SemaphoreType.DMA(())]

    def _copies(self, ins, outs, sems):
        send, recv, local = sems
        x, y, c, _ = _place()
        me = 4 * x + 2 * y + c

        def flip(k):
            return (1 - x) if k & 4 else x, (1 - y) if k & 2 else y, (1 - c) if k & 1 else c

        own = pltpu.make_async_copy(ins[0], outs[0].at[me], local)
        sends, arrivals = [], []
        for k in range(1, N_DEV):
            px, py, pc = flip(k)
            sends.append(pltpu.make_async_remote_copy(
                src_ref=ins[0], dst_ref=outs[0].at[me], send_sem=send.at[k - 1], recv_sem=recv.at[k - 1],
                device_id=(px, py, pc), device_id_type=MESH))
            blk = outs[0].at[4 * px + 2 * py + pc]
            arrivals.append(pltpu.make_async_remote_copy(
                src_ref=blk, dst_ref=blk, send_sem=send.at[k - 1], recv_sem=recv.at[k - 1],
                device_id=(x, y, c), device_id_type=MESH))
        return own, sends, arrivals

    def start(self, ins, outs, sems):
        own, sends, _ = self._copies(ins, outs, sems)
        own.start()
        for cp in sends:
            cp.start()

    def finish(self, ins, outs, sems):
        own, sends, arrivals = self._copies(ins, outs, sems)
        for cp in arrivals:
            cp.wait_recv()
        for cp in sends:
            cp.wait_send()
        own.wait()


def sum_devices(land, *, name):
    n, R, C = land.shape

    def body(l_ref, o_ref):
        acc = l_ref[0]
        for d in range(1, n):
            acc = acc + l_ref[d]
        o_ref[...] = acc

    return _pallas(
        body, name=name, grid=(1,), out_shape=pltpu.HBM((R, C), F32),
        in_specs=[pl.BlockSpec((n, R, C), lambda i: (0, 0, 0))], out_specs=pl.BlockSpec((R, C), lambda i: (0, 0)),
        compiler_params=_params(_nbytes(land.shape, F32), ("arbitrary",)))(land)


def _pack(arrays):
    flat = jnp.concatenate([a.reshape(-1) for a in arrays])
    return flat.reshape(-1, LANES)


def _unpack(packed, shapes):
    flat = packed.reshape(-1)
    out, off = [], 0
    for s in shapes:
        n = 1
        for d in s:
            n *= d
        out.append(flat[off:off + n].reshape(s))
        off += n
    return out


def _row_tile(rows, cap=512):
    t = 1 << (cap.bit_length() - 1)
    while rows % t:
        t //= 2
    return t


def _adamw_tile(rows, cols):
    return _row_tile(rows, max(8, (1 << 20) // (4 * cols)))


def kernel(x, mem, w_in, conv_w, conv_b, conv_ln_g, conv_ln_b, w_out, ln1_g, ln1_b, mem_wq, mem_wk, mem_wv, mem_wo, ln2_g, ln2_b, ffn_up, ffn_conv_w, ffn_conv_b, ffn_down, ln3_g, ln3_b, loss_target, m_w_in, m_conv_w, m_conv_b, m_conv_ln_g, m_conv_ln_b, m_w_out, m_ln1_g, m_ln1_b, m_mem_wq, m_mem_wk, m_mem_wv, m_mem_wo, m_ln2_g, m_ln2_b, m_ffn_up, m_ffn_conv_w, m_ffn_conv_b, m_ffn_down, m_ln3_g, m_ln3_b, v_w_in, v_conv_w, v_conv_b, v_conv_ln_g, v_conv_ln_b, v_w_out, v_ln1_g, v_ln1_b, v_mem_wq, v_mem_wk, v_mem_wv, v_mem_wo, v_ln2_g, v_ln2_b, v_ffn_up, v_ffn_conv_w, v_ffn_conv_b, v_ffn_down, v_ln3_g, v_ln3_b):
    W = dict(w_in=w_in, conv_w=conv_w, conv_b=conv_b, conv_ln_g=conv_ln_g, conv_ln_b=conv_ln_b, w_out=w_out,
             ln1_g=ln1_g, ln1_b=ln1_b, mem_wq=mem_wq, mem_wk=mem_wk, mem_wv=mem_wv, mem_wo=mem_wo, ln2_g=ln2_g,
             ln2_b=ln2_b, ffn_up=ffn_up, ffn_conv_w=ffn_conv_w, ffn_conv_b=ffn_conv_b, ffn_down=ffn_down,
             ln3_g=ln3_g, ln3_b=ln3_b)
    M1 = dict(w_in=m_w_in, conv_w=m_conv_w, conv_b=m_conv_b, conv_ln_g=m_conv_ln_g, conv_ln_b=m_conv_ln_b,
              w_out=m_w_out, ln1_g=m_ln1_g, ln1_b=m_ln1_b, mem_wq=m_mem_wq, mem_wk=m_mem_wk, mem_wv=m_mem_wv,
              mem_wo=m_mem_wo, ln2_g=m_ln2_g, ln2_b=m_ln2_b, ffn_up=m_ffn_up, ffn_conv_w=m_ffn_conv_w,
              ffn_conv_b=m_ffn_conv_b, ffn_down=m_ffn_down, ln3_g=m_ln3_g, ln3_b=m_ln3_b)
    V2 = dict(w_in=v_w_in, conv_w=v_conv_w, conv_b=v_conv_b, conv_ln_g=v_conv_ln_g, conv_ln_b=v_conv_ln_b,
              w_out=v_w_out, ln1_g=v_ln1_g, ln1_b=v_ln1_b, mem_wq=v_mem_wq, mem_wk=v_mem_wk, mem_wv=v_mem_wv,
              mem_wo=v_mem_wo, ln2_g=v_ln2_g, ln2_b=v_ln2_b, ffn_up=v_ffn_up, ffn_conv_w=v_ffn_conv_w,
              ffn_conv_b=v_ffn_conv_b, ffn_down=v_ffn_down, ln3_g=v_ln3_g, ln3_b=v_ln3_b)

    L = w_in.shape[0]
    S, D = x.shape[1], x.shape[2]
    C = conv_b.shape[1]
    alpha = (2.0 * L) ** 0.25
    chip = 2 * lax.axis_index("x") + lax.axis_index("y")
    xs, mems, tgt = x[0], mem[0], loss_target[0]
    mem_bf = mems.astype(BF16)
    tm = _row_tile(S)
    tm_ffn = _row_tile(S, 256)
    tm_big = _row_tile(S, 1024)
    tm_half = _row_tile(S, 2048)

    def shards_of(l, names):
        out = []
        for n in names:
            wl = W[n][l].astype(BF16)
            out.append(wl.reshape(2, wl.shape[0] // 2, wl.shape[1]))
        return out

    def gathered(names, got):
        layer = {}
        for n, g in zip(names, got):
            rows, cols = W[n].shape[1], W[n].shape[2]
            layer[n] = g.reshape(N_CHIPS, rows, cols) if n in COL_SHARDED else g.reshape(N_CHIPS * rows, cols)
        return layer

    full = [dict() for _ in range(L)]
    got, (cw_all, fcw_all) = run_riders(
        [GatherRider(shards_of(0, RIDE_IN)), SmallGatherRider([conv_w, ffn_conv_w])], name="allgather_first")
    full[0].update(gathered(RIDE_IN, got))
    cw_full = jnp.transpose(cw_all, (1, 2, 0, 3)).reshape(L, conv_w.shape[1], -1)
    fcw_full = jnp.transpose(fcw_all, (1, 2, 0, 3)).reshape(L, ffn_conv_w.shape[1], -1)

    saved = []
    h, hb = xs, xs.astype(BF16)
    for l in range(L):
        fw = full[l]
        s = dict(x=h, xb=hb)
        s['glu'], s['qkv'] = proj_split(hb, fw['w_in'], 2 * C, tm=tm, name="proj")
        on_conv = RIDE_ATT if l == 0 else RIDE_FFN[1:]
        on_sb = RIDE_FFN if l == 0 else RIDE_FFN[:1]
        s['u1'], got = conv_fwd(s['glu'], cw_full[l], conv_b[l][None], name="conv_fwd",
                                rider=GatherRider(shards_of(l, on_conv)))
        fw.update(gathered(on_conv, got))
        more = l + 1 < L
        s['o_sb'], s['ltot'], got = sb_fwd(
            s['qkv'], q_col=0, name="sb_fwd", rider=GatherRider(shards_of(l, on_sb)))
        fw.update(gathered(on_sb, got))
        s['ua'] = ln_silu(s['u1'], s['o_sb'], conv_ln_g[l][None], conv_ln_b[l][None], tm=tm, name="ln_silu")
        s['x1'], s['x1b'], s['zh1'], s['rs1'] = mm_ln(
            s['ua'], fw['w_out'], h, ln1_g[l][None], ln1_b[l][None], alpha, tm=tm, name="out_proj_ln")
        s['q2'] = mm_nn(s['x1b'], fw['mem_wq'], BF16, tm=tm, tn=D, name="mem_q")
        s['k2'], s['v2'] = mm_pair(mem_bf, fw['mem_wk'], fw['mem_wv'], name="mem_kv")
        s['o2'] = xattn_fwd(s['q2'], s['k2'], s['v2'], tm=tm, name="xattn_fwd")
        s['x2'], s['x2b'], s['zh2'], s['rs2'] = mm_ln(
            s['o2'], fw['mem_wo'], s['x1'], ln2_g[l][None], ln2_b[l][None], alpha, tm=tm, name="mem_o_ln")
        (s['upv'], s['upg'], s['mv'], s['mg'], s['hmid']), got = ffn_up_fwd(
            s['x2b'], fw['ffn_up'], fcw_full[l], ffn_conv_b[l][None], tm=tm_ffn, tn=fw['ffn_up'].shape[2],
            name="ffn_up_fwd", rider=GatherRider(shards_of(l + 1, RIDE_ATT + RIDE_IN)) if more else None)
        if more:
            full[l + 1].update(gathered(RIDE_ATT + RIDE_IN, got))
        h, hb, s['zh3'], s['rs3'] = mm_ln(
            s['hmid'], fw['ffn_down'], s['x2'], ln3_g[l][None], ln3_b[l][None], alpha, tm=tm, name="ffn_down_ln")
        saved.append(s)

    dx, loss_part = loss_head(h, tgt, tm=tm, name="loss_head")
    loss = lax.psum(loss_part[0, 0], ("x", "y", "c"))

    core = lax.axis_index("c").astype(jnp.int32).reshape(1)
    reduced_big = {n: lax.empty((L, 2, W[n].shape[1] // 2, W[n].shape[2]), F32) for n in BIG}
    small_grads = [None] * L

    def row_halves(g, names):
        parts = []
        for n in names:
            rows, cols = W[n].shape[1], W[n].shape[2]
            parts.append(g[n].reshape(N_CHIPS, 2, rows // 2, cols))
        return parts

    def reduce_into(names, scattered, layer):
        reduced_big.update(zip(names, sum_chips_into(
            list(scattered), [reduced_big[n] for n in names], layer, core, name="rs_sum_chips")))

    pending = None
    for l in reversed(range(L)):
        fw, s = full[l], saved[l]
        g = {}
        if l == L - 1:
            top = ln_bwd(dx, s['zh3'], s['rs3'], ln3_g[l][None], tm=tm, name="ln_bwd")
        dz3, dz3b, g['ln3_g'], g['ln3_b'] = top
        ftn = fw['ffn_up'].shape[2]
        (dupv, dupg, dfw_v, dfw_g, dfb_v, dfb_g), sc = ffn_mid_bwd(
            dz3b, fw['ffn_down'], s['upv'], s['upg'], s['mv'], s['mg'], fcw_full[l], tm=tm_ffn, tn=ftn,
            name="ffn_mid_bwd", rider=ScatterRider(pending) if pending else None)
        if pending:
            reduce_into(RIDE_MIX, sc, l + 1)
        g['ffn_conv_w'] = jnp.concatenate([dfw_v, dfw_g], axis=1)
        g['ffn_conv_b'] = jnp.concatenate([dfb_v, dfb_g], axis=1)[0]
        g['ffn_down'] = mm_tn(s['hmid'], [dz3b], tk=ftn, tn=D, tmc=min(1024, S), name="grad_ffn_down")
        dz2, dz2b, g['ln2_g'], g['ln2_b'] = mm_nt_ln_bwd(
            [dupv, dupg], fw['ffn_up'], dz3, alpha, s['zh2'], s['rs2'], ln2_g[l][None], tm=tm_ffn,
            name="ffn_up_bwd")
        g['ffn_up'] = mm_tn(s['x2b'], [dupv, dupg], tk=D, tn=ftn, shard_width=ftn, tmc=min(1024, S),
                            name="grad_ffn_up")

        do2 = mm_nt([dz2b], fw['mem_wo'], BF16, tm=tm, tk=D, name="mem_o_bwd")
        g['mem_wo'] = mm_tn(s['o2'], [dz2b], tk=D, tn=D, tmc=tm_half, name="grad_sq")
        dq2, dk2, dv2 = xattn_bwd(s['q2'], do2, s['k2'], s['v2'], tm=tm, name="xattn_bwd")
        dz1, dz1b, g['ln1_g'], g['ln1_b'] = mm_nt_ln_bwd(
            [dq2], fw['mem_wq'], dz2, alpha, s['zh1'], s['rs1'], ln1_g[l][None], tm=tm, name="mem_q_bwd")
        g['mem_wq'] = mm_tn(s['x1b'], [dq2], tk=D, tn=D, tmc=tm_half, name="grad_sq")
        g['mem_wk'], g['mem_wv'] = mm_tn_pair(mem_bf, dk2, dv2, name="grad_mem_kv")

        g['w_out'] = mm_tn(s['ua'], [dz1b], tk=D, tn=D, tmc=tm_half, name="grad_sq")
        rest = row_halves(g, RIDE_REST)
        dua, got = mm_nt([dz1b], fw['w_out'], F32, tm=tm, tk=D, name="out_proj_bwd", rider=SwapRider(rest))
        rest = list(add_pairs(rest, got, core, name="rs_add_pairs"))
        dq, dk, dv, sc = sb_bwd(
            s['qkv'], s['ltot'], dua, q_col=0, do_col=C, name="sb_bwd",
            rider=ScatterRider(rest[:-1]))
        reduce_into(RIDE_REST[:-1], sc, l)
        du1, g['conv_ln_g'], g['conv_ln_b'] = ln_silu_bwd(
            dua, s['u1'], conv_ln_g[l][None], conv_ln_b[l][None], tm=tm, name="ln_silu_bwd")
        (da, dg, g['conv_w'], dcb), sc = conv_bwd(du1, s['glu'], cw_full[l], name="conv_bwd",
                                                  rider=ScatterRider(rest[-1:]))
        reduce_into(RIDE_REST[-1:], sc, l)
        g['conv_b'] = dcb
        dproj = jnp.concatenate([da, dg, dq, dk, dv], axis=1)
        ns_in = fw['w_in'].shape[2]
        g['w_in'] = mm_tn(s['xb'], [dproj], tk=512, tn=ns_in, shard_width=ns_in, name="grad_w_in")
        mix = row_halves(g, RIDE_MIX)
        if l > 0:
            below = saved[l - 1]
            top, got = mm_nt_ln_bwd([dproj], fw['w_in'], dz1, alpha, below['zh3'], below['rs3'],
                                    ln3_g[l - 1][None], tm=tm, name="proj_bwd", rider=SwapRider(mix))
        else:
            dx, got = mm_nt([dproj], fw['w_in'], F32, tm=tm_big, tk=512, res=dz1, alpha=alpha, name="proj_bwd_x",
                            rider=SwapRider(mix))
        pending = list(add_pairs(mix, got, core, name="rs_add_pairs"))
        small_grads[l] = {n: g[n].reshape(W[n].shape[1:-1] + (-1,)) for n in SMALL}

    grad_x = dx[None]

    small_full_shapes = []
    small_stack = []
    for n in SMALL:
        st = jnp.stack([small_grads[l][n] for l in range(L)])
        small_stack.append(st)
        small_full_shapes.append(st.shape)
    scattered, everyone = run_riders([ScatterRider(pending), EveryoneRider(_pack(small_stack))],
                                     name="rs_tail_exchange")
    reduce_into(RIDE_MIX, scattered, 0)
    shared = rs_sibling_share([reduced_big[n] for n in BIG], name="rs_sibling_share")
    G = {}
    for n, sh in zip(BIG, shared):
        G[n] = sh.reshape(W[n].shape)
    reduced = _unpack(sum_devices(everyone[0], name="sum_small"), small_full_shapes)
    for n, r in zip(SMALL, reduced):
        if n in SMALL_SHARDED:
            width = W[n].shape[-1]
            r = lax.dynamic_slice_in_dim(r, chip * width, width, axis=2)
        G[n] = r

    out_g, out_d, out_m, out_v = {}, {}, {}, {}
    for n in BIG:
        shp = W[n].shape
        flat = lambda a: a.reshape(shp[0] * shp[1], shp[2])
        res = adamw(flat(W[n]), flat(G[n]), flat(M1[n]), flat(V2[n]), tr=_adamw_tile(shp[0] * shp[1], shp[2]), name="adamw")
        out_g[n], out_d[n], out_m[n], out_v[n] = [r.reshape(shp) for r in res]
    small_shapes = [W[n].shape for n in SMALL]
    packed = [_pack([d[n] for n in SMALL]) for d in (W, G, M1, V2)]
    res = adamw(*packed, tr=packed[0].shape[0], name="adamw_small")
    for d, r in zip((out_g, out_d, out_m, out_v), res):
        for n, a in zip(SMALL, _unpack(r, small_shapes)):
            d[n] = a

    return (loss, grad_x, *[out_g[n] for n in WEIGHTS], *[out_d[n] for n in WEIGHTS],
            *[out_m[n] for n in WEIGHTS], *[out_v[n] for n in WEIGHTS])
```

```python
import functools

import jax
import jax.numpy as jnp
from jax import lax
from jax.experimental import pallas as pl
from jax.experimental.pallas import tpu as pltpu

F32 = jnp.float32
BF16 = jnp.bfloat16
MESH = pl.DeviceIdType.MESH

LN_EPS = 1e-5
SB_HEADS = 8
MEM_HEADS = 4
ADAM_LR, ADAM_B1, ADAM_B2, ADAM_EPS, ADAM_WD, ADAM_STEP = 0.001, 0.9, 0.999, 1e-08, 0.01, 10

LANES = 128
V7X_VMEM_BYTES = 64 << 20
VMEM_CAP = V7X_VMEM_BYTES - (6 << 20)
N_CHIPS = 4
N_DEV = 8

BIG = ('w_in', 'w_out', 'mem_wq', 'mem_wk', 'mem_wv', 'mem_wo', 'ffn_up', 'ffn_down')
RIDE_IN = ('w_in',)
RIDE_ATT = ('w_out', 'mem_wq', 'mem_wk', 'mem_wv', 'mem_wo')
RIDE_FFN = ('ffn_up', 'ffn_down')
RIDE_MIX = ('w_in',)
RIDE_REST = ('w_out', 'mem_wq', 'mem_wk', 'mem_wv', 'mem_wo', 'ffn_up', 'ffn_down')
COL_SHARDED = ('w_in', 'ffn_up')
SMALL = ('conv_w', 'conv_b', 'conv_ln_g', 'conv_ln_b', 'ln1_g', 'ln1_b', 'ln2_g', 'ln2_b',
         'ffn_conv_w', 'ffn_conv_b', 'ln3_g', 'ln3_b')
SMALL_SHARDED = ('conv_w', 'ffn_conv_w')
WEIGHTS = ('w_in', 'conv_w', 'conv_b', 'conv_ln_g', 'conv_ln_b', 'w_out', 'ln1_g', 'ln1_b',
           'mem_wq', 'mem_wk', 'mem_wv', 'mem_wo', 'ln2_g', 'ln2_b', 'ffn_up', 'ffn_conv_w',
           'ffn_conv_b', 'ffn_down', 'ln3_g', 'ln3_b')


def _params(block_bytes, semantics=None, **kw):
    limit = int(min(max(2 * block_bytes + (8 << 20), 32 << 20), VMEM_CAP))
    return pltpu.CompilerParams(dimension_semantics=semantics, vmem_limit_bytes=limit, **kw)


def _pallas(body, **kw):
    call = pl.pallas_call(body, **kw)

    def run(*args):
        return call(*[pltpu.with_memory_space_constraint(a, pltpu.HBM)
                      if jnp.issubdtype(a.dtype, jnp.floating) else a for a in args])

    return run


def _nbytes(shape, dtype):
    n = 1
    for s in shape:
        n *= s
    return n * jnp.dtype(dtype).itemsize


def _dot(a, b):
    return jnp.dot(a, b, preferred_element_type=F32)


def _dot_nt(a, b):
    return lax.dot_general(a, b, (((1,), (1,)), ((), ())), preferred_element_type=F32)


def _dot_tn(a, b):
    return lax.dot_general(a, b, (((0,), (0,)), ((), ())), preferred_element_type=F32)


def _sigmoid(x):
    return 1.0 / (1.0 + jnp.exp(-x))


def mm_nn(a, b, out_dtype, *, tm, tn, name):
    M, K = a.shape
    sharded = b.ndim == 3
    if sharded:
        nsh, _, ns = b.shape
        N, per = nsh * ns, ns // tn
        b_spec = pl.BlockSpec((None, K, tn), lambda i, j: (j // per, 0, j % per))
    else:
        N = b.shape[1]
        b_spec = pl.BlockSpec((K, tn), lambda i, j: (0, j))

    def body(a_ref, b_ref, o_ref):
        o_ref[...] = _dot(a_ref[...].astype(BF16), b_ref[...]).astype(o_ref.dtype)

    blk = _nbytes((tm, K), a.dtype) + _nbytes((K, tn), BF16) + _nbytes((tm, tn), out_dtype)
    return _pallas(
        body, name=name, out_shape=pltpu.HBM((M, N), out_dtype), grid=(M // tm, N // tn),
        in_specs=[pl.BlockSpec((tm, K), lambda i, j: (i, 0)), b_spec],
        out_specs=pl.BlockSpec((tm, tn), lambda i, j: (i, j)),
        compiler_params=_params(blk, ("parallel", "parallel")))(a, b)


def mm_pair(a, b1, b2, *, name):
    M, K = a.shape
    N = b1.shape[1]

    def body(a_ref, b1_ref, b2_ref, o1_ref, o2_ref):
        av = a_ref[...]
        o1_ref[...] = _dot(av, b1_ref[...]).astype(BF16)
        o2_ref[...] = _dot(av, b2_ref[...]).astype(BF16)

    whole = lambda r, c: pl.BlockSpec((r, c), lambda i: (0, 0))
    return _pallas(
        body, name=name, grid=(1,), out_shape=(pltpu.HBM((M, N), BF16), pltpu.HBM((M, N), BF16)),
        in_specs=[whole(M, K), whole(K, N), whole(K, N)], out_specs=(whole(M, N), whole(M, N)),
        compiler_params=_params(3 * _nbytes((K, N), BF16), ("arbitrary",)))(a, b1, b2)


def mm_tn_pair(a, b1, b2, *, name):
    M, K = a.shape
    N = b1.shape[1]

    def body(a_ref, b1_ref, b2_ref, o1_ref, o2_ref):
        av = a_ref[...]
        o1_ref[...] = _dot_tn(av, b1_ref[...].astype(BF16)).astype(BF16)
        o2_ref[...] = _dot_tn(av, b2_ref[...].astype(BF16)).astype(BF16)

    whole = lambda r, c: pl.BlockSpec((r, c), lambda i: (0, 0))
    return _pallas(
        body, name=name, grid=(1,), out_shape=(pltpu.HBM((K, N), BF16), pltpu.HBM((K, N), BF16)),
        in_specs=[whole(M, K), whole(M, N), whole(M, N)], out_specs=(whole(K, N), whole(K, N)),
        compiler_params=_params(4 * _nbytes((K, N), BF16), ("arbitrary",)))(a, b1, b2)


def proj_split(a, b, n_f32, *, tm, name):
    M, K = a.shape
    nsh, _, ns = b.shape
    N = nsh * ns

    def body(a_ref, b_ref, lo_ref, hi_ref):
        av = a_ref[...]
        for s in range(nsh):
            acc = _dot(av, b_ref[s])
            c0, c1 = s * ns, (s + 1) * ns
            cut = min(max(n_f32 - c0, 0), ns)
            if cut > 0:
                lo_ref[:, c0:c0 + cut] = acc[:, 0:cut]
            if cut < ns:
                hi_ref[:, c0 + cut - n_f32:c1 - n_f32] = acc[:, cut:ns].astype(BF16)

    blk = _nbytes((tm, K), BF16) + _nbytes((K, N), BF16) + _nbytes((tm, N), F32)
    return _pallas(
        body, name=name, grid=(M // tm,),
        out_shape=(pltpu.HBM((M, n_f32), F32), pltpu.HBM((M, N - n_f32), BF16)),
        in_specs=[pl.BlockSpec((tm, K), lambda i: (i, 0)), pl.BlockSpec((nsh, K, ns), lambda i: (0, 0, 0))],
        out_specs=(pl.BlockSpec((tm, n_f32), lambda i: (i, 0)), pl.BlockSpec((tm, N - n_f32), lambda i: (i, 0))),
        compiler_params=_params(blk, ("parallel",)))(a, b)


def mm_ln(a, b, x, gamma, beta, alpha, *, tm, name, rider=None):
    M, K = a.shape
    D = b.shape[1]

    def body(a_ref, b_ref, x_ref, g_ref, be_ref, y_ref, yb_ref, zh_ref, rs_ref):
        z = alpha * x_ref[...] + _dot(a_ref[...], b_ref[...])
        mu = jnp.mean(z, axis=-1, keepdims=True)
        zc = z - mu
        rstd = lax.rsqrt(jnp.mean(zc * zc, axis=-1, keepdims=True) + LN_EPS)
        zh = zc * rstd
        y = zh * g_ref[...] + be_ref[...]
        y_ref[...] = y
        yb_ref[...] = y.astype(BF16)
        zh_ref[...] = zh
        rs_ref[...] = rstd

    row = lambda i: (i, 0)
    fix = lambda i: (0, 0)
    blk = _nbytes((tm, K), BF16) + _nbytes((K, D), BF16) + 4 * _nbytes((tm, D), F32)
    in_specs, out_specs, out_shape, scratch = _carry_specs(
        rider, [pl.BlockSpec((tm, K), row), pl.BlockSpec((K, D), fix), pl.BlockSpec((tm, D), row),
                pl.BlockSpec((1, D), fix), pl.BlockSpec((1, D), fix)],
        (pl.BlockSpec((tm, D), row), pl.BlockSpec((tm, D), row), pl.BlockSpec((tm, D), row),
         pl.BlockSpec((tm, 1), row)),
        (pltpu.HBM((M, D), F32), pltpu.HBM((M, D), BF16), pltpu.HBM((M, D), F32), pltpu.HBM((M, 1), F32)), [])
    first = lambda: pl.program_id(0) == 0
    last = lambda: pl.program_id(0) == M // tm - 1
    res = _pallas(
        _carry(rider, body, 5, 4, first, last), name=name, grid=(M // tm,), out_shape=out_shape,
        in_specs=in_specs, out_specs=out_specs, scratch_shapes=scratch,
        compiler_params=_params(blk, ("arbitrary",)))(a, b, x, gamma, beta, *(rider.arrays if rider else ()))
    return res[:4] if rider is None else (res[:4], list(res[4:]))


def ln_bwd(dy, zh, rstd, gamma, *, tm, name):
    M, D = dy.shape

    def body(dy_ref, zh_ref, rs_ref, g_ref, dz_ref, dzb_ref, dg_ref, db_ref):
        @pl.when(pl.program_id(0) == 0)
        def _():
            dg_ref[...] = jnp.zeros_like(dg_ref)
            db_ref[...] = jnp.zeros_like(db_ref)

        dyv, zhv = dy_ref[...], zh_ref[...]
        dg_ref[...] += jnp.sum(dyv * zhv, axis=0, keepdims=True)
        db_ref[...] += jnp.sum(dyv, axis=0, keepdims=True)
        dzh = dyv * g_ref[...]
        m1 = jnp.mean(dzh, axis=-1, keepdims=True)
        m2 = jnp.mean(dzh * zhv, axis=-1, keepdims=True)
        dz = rs_ref[...] * (dzh - m1 - zhv * m2)
        dz_ref[...] = dz
        dzb_ref[...] = dz.astype(BF16)

    row = lambda i: (i, 0)
    fix = lambda i: (0, 0)
    return _pallas(
        body, name=name, grid=(M // tm,),
        out_shape=(pltpu.HBM((M, D), F32), pltpu.HBM((M, D), BF16),
                   pltpu.HBM((1, D), F32), pltpu.HBM((1, D), F32)),
        in_specs=[pl.BlockSpec((tm, D), row), pl.BlockSpec((tm, D), row), pl.BlockSpec((tm, 1), row),
                  pl.BlockSpec((1, D), fix)],
        out_specs=(pl.BlockSpec((tm, D), row), pl.BlockSpec((tm, D), row), pl.BlockSpec((1, D), fix),
                   pl.BlockSpec((1, D), fix)),
        compiler_params=_params(4 * _nbytes((tm, D), F32), ("arbitrary",)))(dy, zh, rstd, gamma)


def mm_nt(a_list, b, out_dtype, *, tm, tk, name, res=None, alpha=None, rider=None):
    M = a_list[0].shape[0]
    widths = [a.shape[1] for a in a_list]
    sharded = b.ndim == 3
    if sharded:
        nsh, K, ns = b.shape
        b_spec = pl.BlockSpec((nsh, tk, ns), lambda i, j: (0, j, 0))
        for w in widths:
            assert w % ns == 0
    else:
        K, N = b.shape
        ns = None
        b_spec = pl.BlockSpec((tk, N), lambda i, j: (j, 0))
    n_a = len(a_list)

    def body(*refs):
        a_refs, b_ref = refs[:n_a], refs[n_a]
        o_ref = refs[-1]
        acc = None
        off = 0
        for a_ref, w in zip(a_refs, widths):
            if sharded:
                for p in range(w // ns):
                    t = _dot_nt(a_ref[:, p * ns:(p + 1) * ns].astype(BF16), b_ref[off // ns + p])
                    acc = t if acc is None else acc + t
            else:
                t = _dot_nt(a_ref[...].astype(BF16), b_ref[:, off:off + w])
                acc = t if acc is None else acc + t
            off += w
        if res is not None:
            acc = acc + alpha * refs[n_a + 1][...]
        o_ref[...] = acc.astype(o_ref.dtype)

    in_specs = [pl.BlockSpec((tm, w), lambda i, j: (i, 0)) for w in widths] + [b_spec]
    args = list(a_list) + [b]
    if res is not None:
        in_specs.append(pl.BlockSpec((tm, tk), lambda i, j: (i, j)))
        args.append(res)
    blk = (sum(_nbytes((tm, w), a.dtype) for a, w in zip(a_list, widths)) + _nbytes((tk, sum(widths)), BF16)
           + 2 * _nbytes((tm, tk), F32))
    in_specs, out_specs, out_shape, scratch = _carry_specs(
        rider, in_specs, (pl.BlockSpec((tm, tk), lambda i, j: (i, j)),), (pltpu.HBM((M, K), out_dtype),), [])
    first = lambda: (pl.program_id(0) == 0) & (pl.program_id(1) == 0)
    last = lambda: (pl.program_id(0) == M // tm - 1) & (pl.program_id(1) == K // tk - 1)
    res_all = _pallas(
        _carry(rider, body, len(args), 1, first, last), name=name, out_shape=out_shape, grid=(M // tm, K // tk),
        in_specs=in_specs, out_specs=out_specs, scratch_shapes=scratch,
        compiler_params=_params(blk, ("arbitrary", "arbitrary")))(*args, *(rider.arrays if rider else ()))
    return res_all[0] if rider is None else (res_all[0], list(res_all[1:]))


def mm_nt_ln_bwd(a_list, b, res, alpha, zh, rstd, gamma, *, tm, name, rider=None):
    M, D = res.shape
    widths = [a.shape[1] for a in a_list]
    sharded = b.ndim == 3
    if sharded:
        nsh, _, ns = b.shape
        b_spec = pl.BlockSpec((nsh, D, ns), lambda i: (0, 0, 0))
    else:
        ns = None
        b_spec = pl.BlockSpec((D, b.shape[1]), lambda i: (0, 0))
    n_a = len(a_list)

    def body(*refs):
        a_refs, b_ref = refs[:n_a], refs[n_a]
        res_ref, zh_ref, rs_ref, g_ref = refs[n_a + 1:n_a + 5]
        dz_ref, dzb_ref, dg_ref, db_ref = refs[n_a + 5:]

        @pl.when(pl.program_id(0) == 0)
        def _():
            dg_ref[...] = jnp.zeros_like(dg_ref)
            db_ref[...] = jnp.zeros_like(db_ref)

        dy = alpha * res_ref[...]
        off = 0
        for a_ref, w in zip(a_refs, widths):
            if sharded:
                for p in range(w // ns):
                    dy = dy + _dot_nt(a_ref[:, p * ns:(p + 1) * ns], b_ref[off // ns + p])
            else:
                dy = dy + _dot_nt(a_ref[...], b_ref[:, off:off + w])
            off += w
        zhv = zh_ref[...]
        dg_ref[...] += jnp.sum(dy * zhv, axis=0, keepdims=True)
        db_ref[...] += jnp.sum(dy, axis=0, keepdims=True)
        dzh = dy * g_ref[...]
        m1 = jnp.mean(dzh, axis=-1, keepdims=True)
        m2 = jnp.mean(dzh * zhv, axis=-1, keepdims=True)
        dz = rs_ref[...] * (dzh - m1 - zhv * m2)
        dz_ref[...] = dz
        dzb_ref[...] = dz.astype(BF16)

    row = lambda i: (i, 0)
    fix = lambda i: (0, 0)
    in_specs = [pl.BlockSpec((tm, w), row) for w in widths] + [
        b_spec, pl.BlockSpec((tm, D), row), pl.BlockSpec((tm, D), row), pl.BlockSpec((tm, 1), row),
        pl.BlockSpec((1, D), fix)]
    blk = (sum(_nbytes((tm, w), BF16) for w in widths) + _nbytes((D, sum(widths)), BF16)
           + 5 * _nbytes((tm, D), F32))
    in_specs, out_specs, out_shape, scratch = _carry_specs(
        rider, in_specs,
        (pl.BlockSpec((tm, D), row), pl.BlockSpec((tm, D), row), pl.BlockSpec((1, D), fix), pl.BlockSpec((1, D), fix)),
        (pltpu.HBM((M, D), F32), pltpu.HBM((M, D), BF16), pltpu.HBM((1, D), F32), pltpu.HBM((1, D), F32)), [])
    first = lambda: pl.program_id(0) == 0
    last = lambda: pl.program_id(0) == M // tm - 1
    res_all = _pallas(
        _carry(rider, body, n_a + 5, 4, first, last), name=name, grid=(M // tm,), out_shape=out_shape,
        in_specs=in_specs, out_specs=out_specs, scratch_shapes=scratch,
        compiler_params=_params(blk, ("arbitrary",)))(
            *a_list, b, res, zh, rstd, gamma, *(rider.arrays if rider else ()))
    return res_all[:4] if rider is None else (res_all[:4], list(res_all[4:]))


def mm_tn(a, b_list, *, tk, tn, name, shard_width=None, tmc=None):
    M, K = a.shape
    tmc = M if tmc is None else tmc
    nm = M // tmc
    widths = [b.shape[1] for b in b_list]
    N = sum(widths)
    starts, s = [], 0
    for w in widths:
        assert w % tn == 0
        starts.append(s)
        s += w // tn
    n_b = len(b_list)

    def body(*refs):
        a_ref, b_refs, o_ref, acc = refs[0], refs[1:1 + n_b], refs[-2], refs[-1]
        j, m = pl.program_id(1), pl.program_id(2)
        for b_ref, st, w in zip(b_refs, starts, widths):
            @pl.when((j >= st) & (j < st + w // tn))
            def _(b_ref=b_ref):
                t = _dot_tn(a_ref[...].astype(BF16), b_ref[...].astype(BF16))
                if nm == 1:
                    o_ref[...] = t.astype(o_ref.dtype)
                else:
                    @pl.when(m == 0)
                    def _():
                        acc[...] = t

                    @pl.when(m > 0)
                    def _():
                        acc[...] += t

                    @pl.when(m == nm - 1)
                    def _():
                        o_ref[...] = acc[...].astype(o_ref.dtype)

    def b_map(st, w):
        nb = w // tn
        return lambda i, j, m: (jnp.where((j >= st) & (j < st + nb), m, 0), jnp.clip(j - st, 0, nb - 1))

    in_specs = [pl.BlockSpec((tmc, tk), lambda i, j, m: (m, i))]
    in_specs += [pl.BlockSpec((tmc, tn), b_map(st, w)) for st, w in zip(starts, widths)]
    if shard_width is None:
        out_shape = pltpu.HBM((K, N), BF16)
        out_spec = pl.BlockSpec((tk, tn), lambda i, j, m: (i, j))
    else:
        per = shard_width // tn
        out_shape = pltpu.HBM((N // shard_width, K, shard_width), BF16)
        out_spec = pl.BlockSpec((None, tk, tn), lambda i, j, m: (j // per, i, j % per))
    acc_shape = (tk, tn) if nm > 1 else (8, LANES)
    blk = (_nbytes((tmc, tk), a.dtype) + n_b * _nbytes((tmc, tn), b_list[0].dtype) + 2 * _nbytes((tk, tn), F32))
    return _pallas(
        body, name=name, out_shape=out_shape, grid=(K // tk, N // tn, nm), in_specs=in_specs, out_specs=out_spec,
        scratch_shapes=[pltpu.VMEM(acc_shape, F32)],
        compiler_params=_params(blk, ("parallel", "arbitrary", "arbitrary")))(a, *b_list)


CONV_PAD = 32
CONV_CHUNK = 128


def _rows(win, off, n, shifts):
    b, a = off % 8, off // 8
    if b not in shifts:
        shifts[b] = win if b == 0 else win[b:b + n + CONV_PAD - 8, :]
    return shifts[b][8 * a:8 * a + n, :]


def _by_residue(n_taps, offset):
    return sorted(range(n_taps), key=lambda k: (offset(k) % 8, k))


def conv_fwd(proj, conv_w, conv_b, *, name, rider=None):
    S = proj.shape[0]
    KW, C = conv_w.shape
    nct = C // LANES
    rc = min(CONV_CHUNK, S)

    def body(a_ref, g_ref, w_ref, b_ref, o_ref, pad):
        pad[0:CONV_PAD, :] = jnp.zeros((CONV_PAD, LANES), F32)
        pad[CONV_PAD:, :] = a_ref[...] * _sigmoid(g_ref[...])
        w = w_ref[...]
        bias = b_ref[...]

        def chunk(i, _):
            base = pl.multiple_of(i * rc, rc)
            win = pad[pl.ds(base, rc + CONV_PAD), :]
            acc = jnp.zeros((rc, LANES), F32) + bias
            shifts = {}
            for k in _by_residue(KW, lambda k: CONV_PAD - (KW - 1) + k):
                acc = acc + w[k:k + 1, :] * _rows(win, CONV_PAD - (KW - 1) + k, rc, shifts)
            o_ref[pl.ds(base, rc), :] = acc
            return 0

        lax.fori_loop(0, S // rc, chunk, 0)

    in_specs, out_specs, out_shape, scratch = _carry_specs(
        rider, [pl.BlockSpec((S, LANES), lambda c: (0, c)), pl.BlockSpec((S, LANES), lambda c: (0, c + nct)),
                pl.BlockSpec((KW, LANES), lambda c: (0, c)), pl.BlockSpec((1, LANES), lambda c: (0, c))],
        (pl.BlockSpec((S, LANES), lambda c: (0, c)),), (pltpu.HBM((S, C), F32),),
        [pltpu.VMEM((S + CONV_PAD, LANES), F32)])
    first = lambda: pl.program_id(0) == 0
    last = lambda: pl.program_id(0) == nct - 1
    res = _pallas(
        _carry(rider, body, 4, 1, first, last), name=name, grid=(nct,), out_shape=out_shape,
        in_specs=in_specs, out_specs=out_specs, scratch_shapes=scratch,
        compiler_params=_params(4 * _nbytes((S, LANES), F32), ("arbitrary",)))(
            proj, proj, conv_w, conv_b, *(rider.arrays if rider else ()))
    return res[0], list(res[1:])


def conv_bwd(du1, proj, conv_w, *, name, rider=None):
    S = proj.shape[0]
    KW, C = conv_w.shape
    nct = C // LANES
    rc = min(CONV_CHUNK, S)

    def body(d_ref, a_ref, g_ref, w_ref, da_ref, dg_ref, dw_ref, db_ref, pad_u, pad_d, du0, dw_acc):
        dw_acc[...] = jnp.zeros_like(dw_acc)
        pad_u[0:CONV_PAD, :] = jnp.zeros((CONV_PAD, LANES), F32)
        pad_u[CONV_PAD:, :] = a_ref[...] * _sigmoid(g_ref[...])
        pad_d[0:S, :] = d_ref[...]
        pad_d[S:, :] = jnp.zeros((CONV_PAD, LANES), F32)
        w = w_ref[...]
        db_ref[...] = jnp.sum(d_ref[...], axis=0, keepdims=True)

        def chunk(i, _):
            base = pl.multiple_of(i * rc, rc)
            d = pad_d[pl.ds(base, rc), :]
            win_u = pad_u[pl.ds(base, rc + CONV_PAD), :]
            win_d = pad_d[pl.ds(base, rc + CONV_PAD), :]
            shifts = {}
            for k in _by_residue(KW, lambda k: CONV_PAD - (KW - 1) + k):
                u_k = _rows(win_u, CONV_PAD - (KW - 1) + k, rc, shifts)
                dw_acc[k:k + 1, :] += jnp.sum(d * u_k, axis=0, keepdims=True)
            acc = jnp.zeros((rc, LANES), F32)
            shifts = {}
            for k in _by_residue(KW, lambda k: KW - 1 - k):
                acc = acc + w[k:k + 1, :] * _rows(win_d, KW - 1 - k, rc, shifts)
            du0[pl.ds(base, rc), :] = acc
            return 0

        lax.fori_loop(0, S // rc, chunk, 0)
        dw_ref[...] = dw_acc[0:KW, :]
        a, sg = a_ref[...], _sigmoid(g_ref[...])
        d0 = du0[...]
        da_ref[...] = (d0 * sg).astype(BF16)
        dg_ref[...] = (d0 * a * sg * (1.0 - sg)).astype(BF16)

    col = lambda c: (0, c)
    in_specs, out_specs, out_shape, scratch = _carry_specs(
        rider, [pl.BlockSpec((S, LANES), col), pl.BlockSpec((S, LANES), col),
                pl.BlockSpec((S, LANES), lambda c: (0, c + nct)), pl.BlockSpec((KW, LANES), col)],
        (pl.BlockSpec((S, LANES), col), pl.BlockSpec((S, LANES), col), pl.BlockSpec((KW, LANES), col),
         pl.BlockSpec((1, LANES), col)),
        (pltpu.HBM((S, C), BF16), pltpu.HBM((S, C), BF16), pltpu.HBM((KW, C), F32), pltpu.HBM((1, C), F32)),
        [pltpu.VMEM((S + CONV_PAD, LANES), F32), pltpu.VMEM((S + CONV_PAD, LANES), F32),
         pltpu.VMEM((S, LANES), F32), pltpu.VMEM((CONV_PAD, LANES), F32)])
    first = lambda: pl.program_id(0) == 0
    last = lambda: pl.program_id(0) == nct - 1
    res = _pallas(
        _carry(rider, body, 4, 4, first, last), name=name, grid=(nct,), out_shape=out_shape,
        in_specs=in_specs, out_specs=out_specs, scratch_shapes=scratch,
        compiler_params=_params(8 * _nbytes((S, LANES), F32), ("arbitrary",)))(
            du1, proj, proj, conv_w, *(rider.arrays if rider else ()))
    return res[:4], list(res[4:])


def ln_silu(u1, o_sb, gamma, beta, *, tm, name):
    S, C = u1.shape

    def body(u_ref, o_ref, g_ref, b_ref, out_ref):
        z = u_ref[...]
        mu = jnp.mean(z, axis=-1, keepdims=True)
        zc = z - mu
        y = zc * lax.rsqrt(jnp.mean(zc * zc, axis=-1, keepdims=True) + LN_EPS) * g_ref[...] + b_ref[...]
        out_ref[:, 0:C] = (y * _sigmoid(y)).astype(BF16)
        out_ref[:, C:] = o_ref[...].astype(BF16)

    row = lambda i: (i, 0)
    fix = lambda i: (0, 0)
    return _pallas(
        body, name=name, out_shape=pltpu.HBM((S, 2 * C), BF16), grid=(S // tm,),
        in_specs=[pl.BlockSpec((tm, C), row), pl.BlockSpec((tm, C), row), pl.BlockSpec((1, C), fix),
                  pl.BlockSpec((1, C), fix)],
        out_specs=pl.BlockSpec((tm, 2 * C), row),
        compiler_params=_params(4 * _nbytes((tm, C), F32), ("parallel",)))(u1, o_sb, gamma, beta)


def ln_silu_bwd(dua, u1, gamma, beta, *, tm, name):
    S, C = u1.shape

    def body(d_ref, u_ref, g_ref, b_ref, du1_ref, dg_ref, db_ref):
        @pl.when(pl.program_id(0) == 0)
        def _():
            dg_ref[...] = jnp.zeros_like(dg_ref)
            db_ref[...] = jnp.zeros_like(db_ref)

        z = u_ref[...]
        mu = jnp.mean(z, axis=-1, keepdims=True)
        zc = z - mu
        rstd = lax.rsqrt(jnp.mean(zc * zc, axis=-1, keepdims=True) + LN_EPS)
        zh = zc * rstd
        y = zh * g_ref[...] + b_ref[...]
        sg = _sigmoid(y)
        dy = d_ref[...] * (sg * (1.0 + y * (1.0 - sg)))
        dg_ref[...] += jnp.sum(dy * zh, axis=0, keepdims=True)
        db_ref[...] += jnp.sum(dy, axis=0, keepdims=True)
        dzh = dy * g_ref[...]
        m1 = jnp.mean(dzh, axis=-1, keepdims=True)
        m2 = jnp.mean(dzh * zh, axis=-1, keepdims=True)
        du1_ref[...] = rstd * (dzh - m1 - zh * m2)

    row = lambda i: (i, 0)
    fix = lambda i: (0, 0)
    return _pallas(
        body, name=name, grid=(S // tm,),
        out_shape=(pltpu.HBM((S, C), F32), pltpu.HBM((1, C), F32),
                   pltpu.HBM((1, C), F32)),
        in_specs=[pl.BlockSpec((tm, C), row), pl.BlockSpec((tm, C), row), pl.BlockSpec((1, C), fix),
                  pl.BlockSpec((1, C), fix)],
        out_specs=(pl.BlockSpec((tm, C), row), pl.BlockSpec((1, C), fix), pl.BlockSpec((1, C), fix)),
        compiler_params=_params(4 * _nbytes((tm, C), F32), ("arbitrary",)))(dua, u1, gamma, beta)


SB_BLOCK = 256
SB_STOP = -105.0
SB_GROUP = 4


def _split_dot(x, tri):
    hi = x.astype(BF16)
    lo = (x - hi.astype(F32)).astype(BF16)
    return _dot(hi, tri) + _dot(lo, tri)


def _neg_softplus(z):
    return -(jnp.maximum(z, 0.0) + jnp.log(1.0 + jnp.exp(-jnp.abs(z))))


def sb_fwd(proj, *, q_col, name, rider=None):
    S = proj.shape[0]
    dh = LANES // 2
    W = SB_HEADS * dh
    BW = SB_GROUP * dh
    ngrp = W // BW
    T = min(SB_BLOCK, S)
    nblk = S // T
    scale = dh ** -0.5
    qb0 = q_col // BW
    heads = range(SB_GROUP)
    sl = [slice(h * dh, (h + 1) * dh) for h in heads]

    def body(q_ref, k_ref, v_ref, o_ref, l_ref, qs):
        r_i = lax.broadcasted_iota(jnp.int32, (T, T), 0)
        c_i = lax.broadcasted_iota(jnp.int32, (T, T), 1)
        tri = (r_i >= c_i).astype(BF16)
        vis = c_i < r_i
        lane = lax.broadcasted_iota(jnp.int32, (T, dh), 1)

        qs[...] = (q_ref[...] * scale).astype(BF16)

        def step(qb, blocks, st):
            nb = range(len(blocks))
            kb = [[k_ref[pl.ds(j0, T), sl[h]].astype(BF16) for h in heads] for j0, _ in blocks]
            vb = [[v_ref[pl.ds(j0, T), sl[h]].astype(BF16) for h in heads] for j0, _ in blocks]
            z = [[_dot_nt(qb[h], kb[b][h]) for h in heads] for b in nb]
            lk = [[_neg_softplus(z[b][h]) for h in heads] for b in nb]
            lk = [[jnp.where(vis, lk[b][h], 0.0) if blocks[b][1] else lk[b][h] for h in heads] for b in nb]
            C = [[_split_dot(lk[b][h], tri) for h in heads] for b in nb]
            R = [[st[2 * h + 1] for h in heads]]
            for b in nb:
                R.append([R[b][h] + C[b][h][:, 0:1] for h in heads])
            A = [[jnp.exp(z[b][h] + C[b][h] + R[b][h]) for h in heads] for b in nb]
            A = [[jnp.where(vis, A[b][h], 0.0) if blocks[b][1] else A[b][h] for h in heads] for b in nb]
            out = ()
            for h in heads:
                acc = st[2 * h]
                for b in nb:
                    acc = acc + _dot(A[b][h].astype(BF16), vb[b][h])
                out += (acc, R[-1][h])
            return out

        zero = (jnp.zeros((T, dh), F32), jnp.zeros((T, 1), F32))

        def finish(r0, i, c):
            walked = jnp.asarray(i - c[0]).astype(F32)
            for h in heads:
                o_ref[pl.ds(r0, T), sl[h]] = c[1 + 2 * h]
                l_ref[pl.ds(r0, T), sl[h]] = jnp.where(lane == 1, walked, c[2 + 2 * h])

        finish(0, 0, (-1,) + step([qs[0:T, sl[h]] for h in heads], [(0, True)], zero * SB_GROUP))

        def qblock(i, _):
            r0 = pl.multiple_of(i * T, T)
            qb = [qs[pl.ds(r0, T), sl[h]] for h in heads]
            state = step(qb, [(r0, True), (pl.multiple_of(r0 - T, T), False)], zero * SB_GROUP)

            def more(c):
                worst = c[2]
                for h in heads[1:]:
                    worst = jnp.maximum(worst, c[2 + 2 * h])
                return (c[0] >= 0) & (jnp.max(worst) >= SB_STOP)

            def walk(c):
                return (c[0] - 1,) + step(qb, [(pl.multiple_of(c[0] * T, T), False)], c[1:])

            finish(r0, i, lax.while_loop(more, walk, (i - 2,) + state))
            return 0

        lax.fori_loop(1, nblk, qblock, 0)

    blk = lambda off: pl.BlockSpec((S, BW), lambda g: (0, qb0 + off * ngrp + g), pipeline_mode=pl.Buffered(1))
    out = pl.BlockSpec((S, BW), lambda g: (0, g))
    in_specs, out_specs, out_shape, scratch = _carry_specs(
        rider, [blk(0), blk(1), blk(2)], (out, out), (pltpu.HBM((S, W), F32), pltpu.HBM((S, W), F32)),
        [pltpu.VMEM((S, BW), BF16)])
    first = lambda: pl.program_id(0) == 0
    last = lambda: pl.program_id(0) == ngrp - 1
    res = _pallas(
        _carry(rider, body, 3, 2, first, last), name=name, grid=(ngrp,), out_shape=out_shape,
        in_specs=in_specs, out_specs=out_specs, scratch_shapes=scratch,
        compiler_params=_params(5 * _nbytes((S, BW), F32), ("arbitrary",)))(
            proj, proj, proj, *(rider.arrays if rider else ()))
    return res[0], res[1], list(res[2:])


def sb_bwd(proj, ltot, dua, *, q_col, do_col, name, rider=None):
    S = proj.shape[0]
    dh = LANES // 2
    W = SB_HEADS * dh
    BW = SB_GROUP * dh
    ngrp = W // BW
    T = min(SB_BLOCK, S)
    nblk = S // T
    scale = dh ** -0.5
    qb0 = q_col // BW
    db0 = do_col // BW
    heads = range(SB_GROUP)
    sl = [slice(h * dh, (h + 1) * dh) for h in heads]

    def body(q_ref, k_ref, v_ref, l_ref, do_ref, dq_ref, dk_ref, dv_ref, dks, dvs):
        r_i = lax.broadcasted_iota(jnp.int32, (T, T), 0)
        c_i = lax.broadcasted_iota(jnp.int32, (T, T), 1)
        tri_rev = (r_i >= c_i).astype(BF16)
        tri_fwd = (r_i <= c_i).astype(BF16)
        vis = c_i < r_i

        dks[...] = jnp.zeros_like(dks)
        dvs[...] = jnp.zeros_like(dvs)

        def step(qb, dob, Lt, blocks, st):
            nb = range(len(blocks))
            kb = [[k_ref[pl.ds(j0, T), sl[h]].astype(BF16) for h in heads] for j0, _ in blocks]
            vb = [[v_ref[pl.ds(j0, T), sl[h]].astype(BF16) for h in heads] for j0, _ in blocks]
            z = [[_dot_nt(qb[h], kb[b][h]) for h in heads] for b in nb]
            dA =[[_dot_nt(dob[h], vb[b][h]) for h in heads] for b in nb]
            lk = [[_neg_softplus(z[b][h]) for h in heads] for b in nb]
            beta = [[jnp.exp(z[b][h] + lk[b][h]) for h in heads] for b in nb]
            lk = [[jnp.where(vis, lk[b][h], 0.0) if blocks[b][1] else lk[b][h] for h in heads] for b in nb]
            C = [[_split_dot(lk[b][h], tri_rev) for h in heads] for b in nb]
            P = [[st[3 * h + 1] for h in heads]]
            for b in nb:
                P.append([P[b][h] + C[b][h][:, 0:1] for h in heads])
            A = [[jnp.exp(z[b][h] + C[b][h] + (Lt[h] - P[b + 1][h])) for h in heads] for b in nb]
            A = [[jnp.where(vis, A[b][h], 0.0) if blocks[b][1] else A[b][h] for h in heads] for b in nb]
            g = [[A[b][h] * dA[b][h] for h in heads] for b in nb]
            Gin = [[_split_dot(g[b][h], tri_fwd) for h in heads] for b in nb]
            Gp = [[st[3 * h + 2] for h in heads]]
            for b in nb:
                Gp.append([Gp[b][h] + Gin[b][h][:, T - 1:T] for h in heads])
            dz = [[g[b][h] - beta[b][h] * (Gp[b][h] + Gin[b][h]) for h in heads] for b in nb]
            dz = [[jnp.where(vis, dz[b][h], 0.0) if blocks[b][1] else dz[b][h] for h in heads] for b in nb]
            dzb = [[dz[b][h].astype(BF16) for h in heads] for b in nb]
            out = ()
            for h in heads:
                dq = st[3 * h]
                for b in nb:
                    j0 = blocks[b][0]
                    dvs[pl.ds(j0, T), sl[h]] += _dot_tn(A[b][h].astype(BF16), dob[h])
                    dks[pl.ds(j0, T), sl[h]] += _dot_tn(dzb[b][h], qb[h])
                    dq = dq + _dot(dzb[b][h], kb[b][h])
                out += (dq, P[-1][h], Gp[-1][h])
            return out

        zero = jnp.zeros((T, 1), F32)
        init = (jnp.zeros((T, dh), F32), zero, zero)

        def operands(r0):
            return ([(q_ref[pl.ds(r0, T), sl[h]] * scale).astype(BF16) for h in heads],
                    [do_ref[pl.ds(r0, T), sl[h]].astype(BF16) for h in heads],
                    [l_ref[pl.ds(r0, T), h * dh:h * dh + 1] for h in heads])

        def finish(r0, c):
            for h in heads:
                dq_ref[pl.ds(r0, T), sl[h]] = (c[3 * h] * scale).astype(BF16)

        finish(0, step(*operands(0), [(0, True)], init * SB_GROUP))

        def qblock(i, _):
            r0 = pl.multiple_of(i * T, T)
            qb, dob, Lt = operands(r0)
            walked = jnp.clip(jnp.max(l_ref[pl.ds(r0, 8), 1:2]).astype(jnp.int32), 2, i + 1)

            def inner(j, c):
                return step(qb, dob, Lt, [(pl.multiple_of(j * T, T), False)], c)

            c = lax.fori_loop(i + 1 - walked, i - 1, inner, init * SB_GROUP)
            finish(r0, step(qb, dob, Lt, [(pl.multiple_of(r0 - T, T), False), (r0, True)], c))
            return 0

        lax.fori_loop(1, nblk, qblock, 0)
        dk_ref[...] = dks[...].astype(BF16)
        dv_ref[...] = dvs[...].astype(BF16)

    once = pl.Buffered(1)
    blk = lambda off: pl.BlockSpec((S, BW), lambda g: (0, qb0 + off * ngrp + g), pipeline_mode=once)
    out = pl.BlockSpec((S, BW), lambda g: (0, g))
    o_shape = pltpu.HBM((S, W), BF16)
    in_specs, out_specs, out_shape, scratch = _carry_specs(
        rider, [blk(0), blk(1), blk(2), pl.BlockSpec((S, BW), lambda g: (0, g), pipeline_mode=once),
                pl.BlockSpec((S, BW), lambda g: (0, db0 + g), pipeline_mode=once)], (out, out, out),
        (o_shape, o_shape, o_shape), [pltpu.VMEM((S, BW), F32)] * 2)
    first = lambda: pl.program_id(0) == 0
    last = lambda: pl.program_id(0) == ngrp - 1
    res = _pallas(
        _carry(rider, body, 5, 3, first, last), name=name, grid=(ngrp,), out_shape=out_shape,
        in_specs=in_specs, out_specs=out_specs, scratch_shapes=scratch,
        compiler_params=_params(6 * _nbytes((S, BW), F32), ("arbitrary",)))(
            proj, proj, proj, ltot, dua, *(rider.arrays if rider else ()))
    return res[0], res[1], res[2], list(res[3:])


def xattn_fwd(q, k, v, *, tm, name):
    S, D = q.shape
    Mlen = k.shape[0]
    hd = D // MEM_HEADS
    scale = hd ** -0.5

    def body(q_ref, k_ref, v_ref, o_ref):
        for h in range(MEM_HEADS):
            sl = slice(h * hd, (h + 1) * hd)
            s = _dot_nt(q_ref[:, sl], k_ref[:, sl]) * scale
            e = jnp.exp(s - jnp.max(s, axis=-1, keepdims=True))
            p = e / jnp.sum(e, axis=-1, keepdims=True)
            o_ref[:, sl] = _dot(p.astype(BF16), v_ref[:, sl]).astype(BF16)

    row = lambda i: (i, 0)
    fix = lambda i: (0, 0)
    return _pallas(
        body, name=name, out_shape=pltpu.HBM((S, D), BF16), grid=(S // tm,),
        in_specs=[pl.BlockSpec((tm, D), row), pl.BlockSpec((Mlen, D), fix), pl.BlockSpec((Mlen, D), fix)],
        out_specs=pl.BlockSpec((tm, D), row),
        compiler_params=_params(4 * _nbytes((tm, D), F32), ("parallel",)))(q, k, v)


def xattn_bwd(q, do, k, v, *, tm, name):
    S, D = q.shape
    Mlen = k.shape[0]
    hd = D // MEM_HEADS
    scale = hd ** -0.5

    def body(q_ref, do_ref, k_ref, v_ref, dq_ref, dk_ref, dv_ref):
        @pl.when(pl.program_id(0) == 0)
        def _():
            dk_ref[...] = jnp.zeros_like(dk_ref)
            dv_ref[...] = jnp.zeros_like(dv_ref)

        for h in range(MEM_HEADS):
            sl = slice(h * hd, (h + 1) * hd)
            qh, doh, kh, vh = q_ref[:, sl], do_ref[:, sl], k_ref[:, sl], v_ref[:, sl]
            s = _dot_nt(qh, kh) * scale
            e = jnp.exp(s - jnp.max(s, axis=-1, keepdims=True))
            p = e / jnp.sum(e, axis=-1, keepdims=True)
            dp = _dot_nt(doh, vh)
            ds = (p * (dp - jnp.sum(p * dp, axis=-1, keepdims=True)) * scale).astype(BF16)
            dq_ref[:, sl] = _dot(ds, kh).astype(BF16)
            dk_ref[:, sl] += _dot_tn(ds, qh)
            dv_ref[:, sl] += _dot_tn(p.astype(BF16), doh)

    row = lambda i: (i, 0)
    fix = lambda i: (0, 0)
    return _pallas(
        body, name=name, grid=(S // tm,),
        out_shape=(pltpu.HBM((S, D), BF16), pltpu.HBM((Mlen, D), F32),
                   pltpu.HBM((Mlen, D), F32)),
        in_specs=[pl.BlockSpec((tm, D), row), pl.BlockSpec((tm, D), row), pl.BlockSpec((Mlen, D), fix),
                  pl.BlockSpec((Mlen, D), fix)],
        out_specs=(pl.BlockSpec((tm, D), row), pl.BlockSpec((Mlen, D), fix), pl.BlockSpec((Mlen, D), fix)),
        compiler_params=_params(6 * _nbytes((tm, D), F32), ("arbitrary",)))(q, do, k, v)


FFN_HALO = 8


def _conv3(ext, w, lo):
    tm = ext.shape[0] - FFN_HALO
    return (w[0:1, :] * ext[lo:lo + tm, :] + w[1:2, :] * ext[lo + 1:lo + 1 + tm, :]
            + w[2:3, :] * ext[lo + 2:lo + 2 + tm, :])


def ffn_up_fwd(xb, w_up, conv_w, conv_b, *, tm, tn, name, rider=None):
    S, D = xb.shape
    nsh, _, ns = w_up.shape
    F = nsh * ns // 2
    per = ns // tn
    ncol = F // tn
    KW = conv_w.shape[0]
    assert KW == 3

    def body(x_ref, wv_ref, wg_ref, cwv_ref, cwg_ref, cbv_ref, cbg_ref, uv_ref, ug_ref, mv_ref, mg_ref, h_ref,
             carry):
        @pl.when(pl.program_id(1) == 0)
        def _():
            carry[...] = jnp.zeros_like(carry)

        x = x_ref[...]
        uv = _dot(x, wv_ref[...])
        ug = _dot(x, wg_ref[...])
        uv_ref[...] = uv.astype(BF16)
        ug_ref[...] = ug.astype(BF16)
        lo = FFN_HALO - (KW - 1)
        cv = _conv3(jnp.concatenate([carry[0], uv], axis=0), cwv_ref[...], lo) + cbv_ref[...]
        cg = _conv3(jnp.concatenate([carry[1], ug], axis=0), cwg_ref[...], lo) + cbg_ref[...]
        carry[0] = uv[tm - FFN_HALO:, :]
        carry[1] = ug[tm - FFN_HALO:, :]
        sg = _sigmoid(cg)
        act = cg * sg
        mv_ref[...] = act.astype(BF16)
        mg_ref[...] = (cv * (sg + act * (1.0 - sg))).astype(BF16)
        h_ref[...] = (act * cv).astype(BF16)

    wspec = lambda half: pl.BlockSpec((None, D, tn), lambda j, i: (half * (nsh // 2) + j // per, 0, j % per))
    cspec = lambda rows, half: pl.BlockSpec((rows, tn), lambda j, i: (0, half * ncol + j))
    out = pl.BlockSpec((tm, tn), lambda j, i: (i, j))
    o_shape = pltpu.HBM((S, F), BF16)
    blk = _nbytes((tm, D), BF16) + 2 * _nbytes((D, tn), BF16) + 8 * _nbytes((tm, tn), F32)
    nrow = S // tm
    in_specs, out_specs, out_shape, scratch = _carry_specs(
        rider, [pl.BlockSpec((tm, D), lambda j, i: (i, 0)), wspec(0), wspec(1), cspec(KW, 0), cspec(KW, 1),
                cspec(1, 0), cspec(1, 1)], (out,) * 5, (o_shape,) * 5, [pltpu.VMEM((2, FFN_HALO, tn), F32)])
    first = lambda: (pl.program_id(0) == 0) & (pl.program_id(1) == 0)
    last = lambda: (pl.program_id(0) == ncol - 1) & (pl.program_id(1) == nrow - 1)
    res = _pallas(
        _carry(rider, body, 7, 5, first, last), name=name, grid=(ncol, nrow), out_shape=out_shape,
        in_specs=in_specs, out_specs=out_specs, scratch_shapes=scratch,
        compiler_params=_params(blk, ("arbitrary", "arbitrary")))(
            xb, w_up, w_up, conv_w, conv_w, conv_b, conv_b, *(rider.arrays if rider else ()))
    return res[:5], list(res[5:])


def ffn_mid_bwd(dzb, w_down, up_v, up_g, mult_v, mult_g, conv_w, *, tm, tn, name, rider=None):
    S, D = dzb.shape
    F = up_v.shape[1]
    ncol = F // tn
    nrow = S // tm
    KW = conv_w.shape[0]
    assert KW == 3

    def body(dz_ref, wd_ref, uv_ref, ug_ref, mv_ref, mg_ref, cwv_ref, cwg_ref,
             dv_ref, dg_ref, dwv_ref, dwg_ref, dbv_ref, dbg_ref, carry):
        @pl.when(pl.program_id(1) == 0)
        def _():
            carry[...] = jnp.zeros_like(carry)
            for r in (dwv_ref, dwg_ref, dbv_ref, dbg_ref):
                r[...] = jnp.zeros_like(r)

        dh = _dot_nt(dz_ref[...], wd_ref[...])
        dcv = dh * mv_ref[...].astype(F32)
        dcg = dh * mg_ref[...].astype(F32)

        def back(dc, u_ref, cw, slot, du_ref, dw_ref, db_ref):
            ext = jnp.concatenate([dc, carry[slot]], axis=0)
            ahead = [dc, ext[1:tm + 1, :], ext[2:tm + 2, :]]
            du = cw[2:3, :] * ahead[0] + cw[1:2, :] * ahead[1] + cw[0:1, :] * ahead[2]
            du_ref[...] = du.astype(BF16)
            carry[slot] = dc[0:FFN_HALO, :]
            u = u_ref[...].astype(F32)
            for k in range(KW):
                dw_ref[k:k + 1, :] += jnp.sum(ahead[KW - 1 - k] * u, axis=0, keepdims=True)
            db_ref[...] += jnp.sum(dc, axis=0, keepdims=True)

        back(dcv, uv_ref, cwv_ref[...], 0, dv_ref, dwv_ref, dbv_ref)
        back(dcg, ug_ref, cwg_ref[...], 1, dg_ref, dwg_ref, dbg_ref)

    rev = lambda i: nrow - 1 - i
    tile = pl.BlockSpec((tm, tn), lambda j, i: (rev(i), j))
    cspec = lambda half: pl.BlockSpec((KW, tn), lambda j, i: (0, half * ncol + j))
    acc = lambda rows: pl.BlockSpec((rows, tn), lambda j, i: (0, j))
    big = pltpu.HBM((S, F), BF16)
    blk = _nbytes((tm, D), BF16) + _nbytes((tn, D), BF16) + 10 * _nbytes((tm, tn), F32)
    in_specs, out_specs, out_shape, scratch = _carry_specs(
        rider, [pl.BlockSpec((tm, D), lambda j, i: (rev(i), 0)), pl.BlockSpec((tn, D), lambda j, i: (j, 0)),
                tile, tile, tile, tile, cspec(0), cspec(1)],
        (tile, tile, acc(KW), acc(KW), acc(1), acc(1)),
        (big, big, pltpu.HBM((KW, F), F32), pltpu.HBM((KW, F), F32), pltpu.HBM((1, F), F32),
         pltpu.HBM((1, F), F32)), [pltpu.VMEM((2, FFN_HALO, tn), F32)])
    first = lambda: (pl.program_id(0) == 0) & (pl.program_id(1) == 0)
    last = lambda: (pl.program_id(0) == ncol - 1) & (pl.program_id(1) == nrow - 1)
    res = _pallas(
        _carry(rider, body, 8, 6, first, last), name=name, grid=(ncol, nrow), out_shape=out_shape,
        in_specs=in_specs, out_specs=out_specs, scratch_shapes=scratch,
        compiler_params=_params(blk, ("arbitrary", "arbitrary")))(
            dzb, w_down, up_v, up_g, mult_v, mult_g, conv_w, conv_w, *(rider.arrays if rider else ()))
    return res[:6], list(res[6:])


def loss_head(y, target, *, tm, name):
    S, D = y.shape

    def body(y_ref, t_ref, dy_ref, l_ref):
        @pl.when(pl.program_id(0) == 0)
        def _():
            l_ref[...] = jnp.zeros_like(l_ref)

        e = y_ref[...] - t_ref[...]
        dy_ref[...] = e * (1.0 / D)
        l_ref[...] += 0.5 * jnp.sum(jnp.mean(e * e, axis=-1, keepdims=True), axis=0, keepdims=True)

    row = lambda i: (i, 0)
    return _pallas(
        body, name=name, grid=(S // tm,),
        out_shape=(pltpu.HBM((S, D), F32), pltpu.HBM((1, 1), F32)),
        in_specs=[pl.BlockSpec((tm, D), row), pl.BlockSpec((tm, D), row)],
        out_specs=(pl.BlockSpec((tm, D), row), pl.BlockSpec((1, 1), lambda i: (0, 0))),
        compiler_params=_params(3 * _nbytes((tm, D), F32), ("arbitrary",)))(y, target)


def adamw(w, g, m, v, *, tr, name):
    R, C = w.shape
    c1 = 1.0 - ADAM_B1 ** ADAM_STEP
    c2 = 1.0 - ADAM_B2 ** ADAM_STEP

    def body(w_ref, g_ref, m_ref, v_ref, go_ref, d_ref, mo_ref, vo_ref):
        gv = g_ref[...]
        mn = ADAM_B1 * m_ref[...] + (1.0 - ADAM_B1) * gv
        vn = ADAM_B2 * v_ref[...] + (1.0 - ADAM_B2) * (gv * gv)
        go_ref[...] = gv
        mo_ref[...] = mn
        vo_ref[...] = vn
        d_ref[...] = -ADAM_LR * ((mn / c1) / (jnp.sqrt(vn / c2) + ADAM_EPS) + ADAM_WD * w_ref[...])

    spec = pl.BlockSpec((tr, C), lambda i: (i, 0))
    shape = pltpu.HBM((R, C), F32)
    return _pallas(
        body, name=name, grid=(R // tr,), out_shape=(shape,) * 4, in_specs=[spec] * 4, out_specs=(spec,) * 4,
        compiler_params=_params(8 * _nbytes((tr, C), F32), ("parallel",)))(w, g, m, v)


def add_pairs(gs, gots, core, *, name):
    k = len(gs)

    def body(c_ref, *refs):
        for a_ref, b_ref, o_ref in zip(refs[:k], refs[k:2 * k], refs[2 * k:]):
            o_ref[...] = (a_ref[...].astype(F32) + b_ref[...].astype(F32)).astype(BF16)

    own = [pl.BlockSpec((None, None) + g.shape[2:], lambda i, c: (i, c[0], 0, 0)) for g in gs]
    half = [pl.BlockSpec((None,) + g.shape[1:], lambda i, c: (i, 0, 0)) for g in gots]
    grid_spec = pltpu.PrefetchScalarGridSpec(
        num_scalar_prefetch=1, grid=(N_CHIPS,), in_specs=own + half, out_specs=tuple(half))
    blk = 3 * sum(_nbytes(g.shape[1:], BF16) for g in gots)
    return _pallas(
        body, name=name, grid_spec=grid_spec, out_shape=tuple(pltpu.HBM(g.shape, BF16) for g in gots),
        compiler_params=_params(blk, ("parallel",)))(core, *gs, *gots)


def sum_chips_into(bs, dests, layer, core, *, name):
    k = len(bs)
    steps = 2

    def body(c_ref, *refs):
        for b_ref, o_ref in zip(refs[:k], refs[2 * k:]):
            acc = b_ref[0].astype(F32)
            for p in range(1, N_CHIPS):
                acc = acc + b_ref[p].astype(F32)
            o_ref[...] = acc

    ins = [pl.BlockSpec((N_CHIPS, b.shape[1] // steps, b.shape[2]), lambda i, c: (0, i, 0)) for b in bs]
    outs = tuple(pl.BlockSpec((None, None, b.shape[1] // steps, b.shape[2]), lambda i, c: (layer, c[0], i, 0))
                 for b in bs)
    grid_spec = pltpu.PrefetchScalarGridSpec(
        num_scalar_prefetch=1, grid=(steps,), in_specs=ins + [pl.BlockSpec(memory_space=pl.ANY)] * k,
        out_specs=outs)
    blk = sum(_nbytes(b.shape, BF16) + _nbytes(b.shape[1:], F32) for b in bs) // steps
    return _pallas(
        body, name=name, grid_spec=grid_spec, out_shape=tuple(pltpu.HBM(d.shape, F32) for d in dests),
        input_output_aliases={1 + k + w: w for w in range(k)},
        compiler_params=_params(blk, ("parallel",)))(core, *bs, *dests)


_HBM = pl.BlockSpec(memory_space=pltpu.HBM)


def _place():
    x, y, c = lax.axis_index("x"), lax.axis_index("y"), lax.axis_index("c")
    chips = [(1 - x, y), (x, 1 - y), (1 - x, 1 - y)]
    return x, y, c, chips


class GatherRider:
    def __init__(self, shards):
        self.arrays = list(shards)
        self.n = n = len(shards)
        self.out_shape = tuple(pltpu.HBM((N_CHIPS,) + s.shape, s.dtype) for s in shards)
        self.scratch = [pltpu.SemaphoreType.DMA((n, 3))] * 4 + [pltpu.SemaphoreType.DMA((n,))]

    def _copies(self, ins, outs, sems):
        send_ici, recv_ici, send_d2d, recv_d2d, local = sems
        x, y, c, chips = _place()
        me = 2 * x + y

        def own(w):
            return pltpu.make_async_copy(ins[w], outs[w].at[me], local.at[w])

        def ici(w, j):
            px, py = chips[j]
            return pltpu.make_async_remote_copy(
                src_ref=ins[w].at[c], dst_ref=outs[w].at[me, c], send_sem=send_ici.at[w, j],
                recv_sem=recv_ici.at[w, j], device_id=(px, py, c), device_id_type=MESH)

        def landed(w, j, half):
            px, py = chips[j]
            return outs[w].at[2 * px + py, half]

        def d2d(w, j, half):
            return pltpu.make_async_remote_copy(
                src_ref=landed(w, j, half), dst_ref=landed(w, j, half), send_sem=send_d2d.at[w, j],
                recv_sem=recv_d2d.at[w, j], device_id=(x, y, 1 - c), device_id_type=MESH)

        def ici_arrival(w, j):
            return pltpu.make_async_remote_copy(
                src_ref=landed(w, j, c), dst_ref=landed(w, j, c), send_sem=send_ici.at[w, j],
                recv_sem=recv_ici.at[w, j], device_id=(x, y, c), device_id_type=MESH)

        return c, own, ici, d2d, ici_arrival

    def start(self, ins, outs, sems):
        c, own, ici, d2d, ici_arrival = self._copies(ins, outs, sems)
        for w in range(self.n):
            own(w).start()
            for j in range(3):
                ici(w, j).start()

    def finish(self, ins, outs, sems):
        c, own, ici, d2d, ici_arrival = self._copies(ins, outs, sems)
        for w in range(self.n):
            for j in range(3):
                ici_arrival(w, j).wait_recv()
                d2d(w, j, c).start()
        for w in range(self.n):
            for j in range(3):
                d2d(w, j, 1 - c).wait_recv()
        for w in range(self.n):
            for j in range(3):
                ici(w, j).wait_send()
                d2d(w, j, c).wait_send()
            own(w).wait()


class ScatterRider:
    def __init__(self, parts):
        self.arrays = list(parts)
        self.n = n = len(parts)
        self.out_shape = tuple(pltpu.HBM(p.shape, p.dtype) for p in parts)
        self.scratch = [pltpu.SemaphoreType.DMA((n, 3))] * 2 + [pltpu.SemaphoreType.DMA((n,))]

    def _copies(self, ins, outs, sems):
        send, recv, local = sems
        x, y, c, chips = _place()
        me = 2 * x + y

        def own(w):
            return pltpu.make_async_copy(ins[w].at[me], outs[w].at[me], local.at[w])

        def copy(w, j):
            px, py = chips[j]
            return pltpu.make_async_remote_copy(
                src_ref=ins[w].at[2 * px + py], dst_ref=outs[w].at[me], send_sem=send.at[w, j],
                recv_sem=recv.at[w, j], device_id=(px, py, c), device_id_type=MESH)

        def arrival(w, j):
            px, py = chips[j]
            blk = outs[w].at[2 * px + py]
            return pltpu.make_async_remote_copy(
                src_ref=blk, dst_ref=blk, send_sem=send.at[w, j], recv_sem=recv.at[w, j],
                device_id=(x, y, c), device_id_type=MESH)

        return own, copy, arrival

    def start(self, ins, outs, sems):
        own, copy, arrival = self._copies(ins, outs, sems)
        for w in range(self.n):
            own(w).start()
            for j in range(3):
                copy(w, j).start()

    def finish(self, ins, outs, sems):
        own, copy, arrival = self._copies(ins, outs, sems)
        for w in range(self.n):
            for j in range(3):
                arrival(w, j).wait_recv()
        for w in range(self.n):
            for j in range(3):
                copy(w, j).wait_send()
            own(w).wait()


def _carry(rider, body, n_in, n_out, first, last):
    if rider is None:
        return body
    k, m = rider.n, len(rider.scratch)

    def carried(*refs):
        ins, r_in = refs[:n_in], refs[n_in:n_in + k]
        outs, r_out = refs[n_in + k:n_in + k + n_out], refs[n_in + k + n_out:n_in + 2 * k + n_out]
        rest = refs[n_in + 2 * k + n_out:]
        scratch, sems = rest[:len(rest) - m], rest[len(rest) - m:]

        @pl.when(first())
        def _():
            rider.start(r_in, r_out, sems)

        body(*ins, *outs, *scratch)

        @pl.when(last())
        def _():
            rider.finish(r_in, r_out, sems)

    return carried


def _carry_specs(rider, in_specs, out_specs, out_shape, scratch):
    if rider is None:
        return list(in_specs), tuple(out_specs), tuple(out_shape), list(scratch)
    k = rider.n
    return (list(in_specs) + [_HBM] * k, tuple(out_specs) + (_HBM,) * k, tuple(out_shape) + rider.out_shape,
            list(scratch) + list(rider.scratch))


def run_riders(riders, *, name):
    ks = [r.n for r in riders]
    ms = [len(r.scratch) for r in riders]
    k_all = sum(ks)

    def body(*refs):
        parts, i0, o0, s0 = [], 0, k_all, 2 * k_all
        for k, m in zip(ks, ms):
            parts.append((refs[i0:i0 + k], refs[o0:o0 + k], refs[s0:s0 + m]))
            i0, o0, s0 = i0 + k, o0 + k, s0 + m
        for r, p in zip(riders, parts):
            r.start(*p)
        for r, p in zip(riders, parts):
            r.finish(*p)

    res = _pallas(
        body, name=name, out_shape=tuple(o for r in riders for o in r.out_shape), in_specs=[_HBM] * k_all,
        out_specs=(_HBM,) * k_all, scratch_shapes=[s for r in riders for s in r.scratch],
    )(*[a for r in riders for a in r.arrays])
    out, o0 = [], 0
    for k in ks:
        out.append(list(res[o0:o0 + k]))
        o0 += k
    return out


class SmallGatherRider:
    def __init__(self, shards):
        self.arrays = list(shards)
        self.n = n = len(shards)
        self.out_shape = tuple(pltpu.HBM((N_CHIPS,) + s.shape, s.dtype) for s in shards)
        self.scratch = [pltpu.SemaphoreType.DMA((n, 3))] * 2 + [pltpu.SemaphoreType.DMA((n,))]

    def _copies(self, ins, outs, sems):
        send, recv, local = sems
        x, y, c, chips = _place()
        me = 2 * x + y

        def own(w):
            return pltpu.make_async_copy(ins[w], outs[w].at[me], local.at[w])

        def copy(w, j):
            px, py = chips[j]
            return pltpu.make_async_remote_copy(
                src_ref=ins[w], dst_ref=outs[w].at[me], send_sem=send.at[w, j], recv_sem=recv.at[w, j],
                device_id=(px, py, c), device_id_type=MESH)

        def arrival(w, j):
            px, py = chips[j]
            blk = outs[w].at[2 * px + py]
            return pltpu.make_async_remote_copy(
                src_ref=blk, dst_ref=blk, send_sem=send.at[w, j], recv_sem=recv.at[w, j],
                device_id=(x, y, c), device_id_type=MESH)

        return own, copy, arrival

    def start(self, ins, outs, sems):
        own, copy, arrival = self._copies(ins, outs, sems)
        for w in range(self.n):
            own(w).start()
            for j in range(3):
                copy(w, j).start()

    def finish(self, ins, outs, sems):
        own, copy, arrival = self._copies(ins, outs, sems)
        for w in range(self.n):
            for j in range(3):
                arrival(w, j).wait_recv()
        for w in range(self.n):
            for j in range(3):
                copy(w, j).wait_send()
            own(w).wait()


class SwapRider:
    def __init__(self, grads):
        self.arrays = list(grads)
        self.n = n = len(grads)
        self.out_shape = tuple(pltpu.HBM((N_CHIPS,) + g.shape[2:], g.dtype) for g in grads)
        self.scratch = [pltpu.SemaphoreType.DMA((n,))] * 2

    def _copies(self, ins, outs, sems):
        send, recv = sems
        x, y, c, _ = _place()
        return [pltpu.make_async_remote_copy(
            src_ref=ins[w].at[:, 1 - c], dst_ref=outs[w], send_sem=send.at[w], recv_sem=recv.at[w],
            device_id=(x, y, 1 - c), device_id_type=MESH) for w in range(self.n)]

    def start(self, ins, outs, sems):
        for cp in self._copies(ins, outs, sems):
            cp.start()

    def finish(self, ins, outs, sems):
        copies = self._copies(ins, outs, sems)
        for cp in copies:
            cp.wait_recv()
        for cp in copies:
            cp.wait_send()


def rs_sibling_share(stacked, *, name):
    n = len(stacked)

    def body(*refs):
        bufs = refs[n:2 * n]
        send, recv = refs[2 * n:]
        x, y, c, _ = _place()
        shares, arrivals = [], []
        for w in range(n):
            mine, other = bufs[w].at[:, c], bufs[w].at[:, 1 - c]
            shares.append(pltpu.make_async_remote_copy(
                src_ref=mine, dst_ref=mine, send_sem=send.at[w], recv_sem=recv.at[w],
                device_id=(x, y, 1 - c), device_id_type=MESH))
            arrivals.append(pltpu.make_async_remote_copy(
                src_ref=other, dst_ref=other, send_sem=send.at[w], recv_sem=recv.at[w],
                device_id=(x, y, c), device_id_type=MESH))
        for cp in shares:
            cp.start()
        for cp in arrivals:
            cp.wait_recv()
        for cp in shares:
            cp.wait_send()

    out_shape = tuple(pltpu.HBM(s.shape, F32) for s in stacked)
    return _pallas(
        body, name=name, out_shape=out_shape, in_specs=[_HBM] * n, out_specs=(_HBM,) * n,
        input_output_aliases={w: w for w in range(n)},
        scratch_shapes=[pltpu.SemaphoreType.DMA((n,))] * 2,
    )(*stacked)


class EveryoneRider:
    def __init__(self, v):
        self.arrays = [v]
        self.n = 1
        self.out_shape = (pltpu.HBM((N_DEV,) + v.shape, v.dtype),)
        self.scratch = [pltpu.SemaphoreType.DMA((N_DEV - 1,))] * 2 + [pltpu.SemaphoreType.DMA(())]

    def _copies(self, ins, outs, sems):
        send, recv, local = sems
        x, y, c, _ = _place()
        me = 4 * x + 2 * y + c

        def flip(k):
            return (1 - x) if k & 4 else x, (1 - y) if k & 2 else y, (1 - c) if k & 1 else c

        own = pltpu.make_async_copy(ins[0], outs[0].at[me], local)
        sends, arrivals = [], []
        for k in range(1, N_DEV):
            px, py, pc = flip(k)
            sends.append(pltpu.make_async_remote_copy(
                src_ref=ins[0], dst_ref=outs[0].at[me], send_sem=send.at[k - 1], recv_sem=recv.at[k - 1],
                device_id=(px, py, pc), device_id_type=MESH))
            blk = outs[0].at[4 * px + 2 * py + pc]
            arrivals.append(pltpu.make_async_remote_copy(
                src_ref=blk, dst_ref=blk, send_sem=send.at[k - 1], recv_sem=recv.at[k - 1],
                device_id=(x, y, c), device_id_type=MESH))
        return own, sends, arrivals

    def start(self, ins, outs, sems):
        own, sends, _ = self._copies(ins, outs, sems)
        own.start()
        for cp in sends:
            cp.start()

    def finish(self, ins, outs, sems):
        own, sends, arrivals = self._copies(ins, outs, sems)
        for cp in arrivals:
            cp.wait_recv()
        for cp in sends:
            cp.wait_send()
        own.wait()


def sum_devices(land, *, name):
    n, R, C = land.shape

    def body(l_ref, o_ref):
        acc = l_ref[0]
        for d in range(1, n):
            acc = acc + l_ref[d]
        o_ref[...] = acc

    return _pallas(
        body, name=name, grid=(1,), out_shape=pltpu.HBM((R, C), F32),
        in_specs=[pl.BlockSpec((n, R, C), lambda i: (0, 0, 0))], out_specs=pl.BlockSpec((R, C), lambda i: (0, 0)),
        compiler_params=_params(_nbytes(land.shape, F32), ("arbitrary",)))(land)


def _pack(arrays):
    flat = jnp.concatenate([a.reshape(-1) for a in arrays])
    return flat.reshape(-1, LANES)


def _unpack(packed, shapes):
    flat = packed.reshape(-1)
    out, off = [], 0
    for s in shapes:
        n = 1
        for d in s:
            n *= d
        out.append(flat[off:off + n].reshape(s))
        off += n
    return out


def _row_tile(rows, cap=512):
    t = 1 << (cap.bit_length() - 1)
    while rows % t:
        t //= 2
    return t


def _adamw_tile(rows, cols):
    return _row_tile(rows, max(8, (1 << 20) // (4 * cols)))


def kernel(x, mem, w_in, conv_w, conv_b, conv_ln_g, conv_ln_b, w_out, ln1_g, ln1_b, mem_wq, mem_wk, mem_wv, mem_wo, ln2_g, ln2_b, ffn_up, ffn_conv_w, ffn_conv_b, ffn_down, ln3_g, ln3_b, loss_target, m_w_in, m_conv_w, m_conv_b, m_conv_ln_g, m_conv_ln_b, m_w_out, m_ln1_g, m_ln1_b, m_mem_wq, m_mem_wk, m_mem_wv, m_mem_wo, m_ln2_g, m_ln2_b, m_ffn_up, m_ffn_conv_w, m_ffn_conv_b, m_ffn_down, m_ln3_g, m_ln3_b, v_w_in, v_conv_w, v_conv_b, v_conv_ln_g, v_conv_ln_b, v_w_out, v_ln1_g, v_ln1_b, v_mem_wq, v_mem_wk, v_mem_wv, v_mem_wo, v_ln2_g, v_ln2_b, v_ffn_up, v_ffn_conv_w, v_ffn_conv_b, v_ffn_down, v_ln3_g, v_ln3_b):
    W = dict(w_in=w_in, conv_w=conv_w, conv_b=conv_b, conv_ln_g=conv_ln_g, conv_ln_b=conv_ln_b, w_out=w_out,
             ln1_g=ln1_g, ln1_b=ln1_b, mem_wq=mem_wq, mem_wk=mem_wk, mem_wv=mem_wv, mem_wo=mem_wo, ln2_g=ln2_g,
             ln2_b=ln2_b, ffn_up=ffn_up, ffn_conv_w=ffn_conv_w, ffn_conv_b=ffn_conv_b, ffn_down=ffn_down,
             ln3_g=ln3_g, ln3_b=ln3_b)
    M1 = dict(w_in=m_w_in, conv_w=m_conv_w, conv_b=m_conv_b, conv_ln_g=m_conv_ln_g, conv_ln_b=m_conv_ln_b,
              w_out=m_w_out, ln1_g=m_ln1_g, ln1_b=m_ln1_b, mem_wq=m_mem_wq, mem_wk=m_mem_wk, mem_wv=m_mem_wv,
              mem_wo=m_mem_wo, ln2_g=m_ln2_g, ln2_b=m_ln2_b, ffn_up=m_ffn_up, ffn_conv_w=m_ffn_conv_w,
              ffn_conv_b=m_ffn_conv_b, ffn_down=m_ffn_down, ln3_g=m_ln3_g, ln3_b=m_ln3_b)
    V2 = dict(w_in=v_w_in, conv_w=v_conv_w, conv_b=v_conv_b, conv_ln_g=v_conv_ln_g, conv_ln_b=v_conv_ln_b,
              w_out=v_w_out, ln1_g=v_ln1_g, ln1_b=v_ln1_b, mem_wq=v_mem_wq, mem_wk=v_mem_wk, mem_wv=v_mem_wv,
              mem_wo=v_mem_wo, ln2_g=v_ln2_g, ln2_b=v_ln2_b, ffn_up=v_ffn_up, ffn_conv_w=v_ffn_conv_w,
              ffn_conv_b=v_ffn_conv_b, ffn_down=v_ffn_down, ln3_g=v_ln3_g, ln3_b=v_ln3_b)

    L = w_in.shape[0]
    S, D = x.shape[1], x.shape[2]
    C = conv_b.shape[1]
    alpha = (2.0 * L) ** 0.25
    chip = 2 * lax.axis_index("x") + lax.axis_index("y")
    xs, mems, tgt = x[0], mem[0], loss_target[0]
    mem_bf = mems.astype(BF16)
    tm = _row_tile(S)
    tm_ffn = _row_tile(S, 256)
    tm_big = _row_tile(S, 1024)
    tm_half = _row_tile(S, 2048)

    def shards_of(l, names):
        out = []
        for n in names:
            wl = W[n][l].astype(BF16)
            out.append(wl.reshape(2, wl.shape[0] // 2, wl.shape[1]))
        return out

    def gathered(names, got):
        layer = {}
        for n, g in zip(names, got):
            rows, cols = W[n].shape[1], W[n].shape[2]
            layer[n] = g.reshape(N_CHIPS, rows, cols) if n in COL_SHARDED else g.reshape(N_CHIPS * rows, cols)
        return layer

    full = [dict() for _ in range(L)]
    got, (cw_all, fcw_all) = run_riders(
        [GatherRider(shards_of(0, RIDE_IN)), SmallGatherRider([conv_w, ffn_conv_w])], name="allgather_first")
    full[0].update(gathered(RIDE_IN, got))
    cw_full = jnp.transpose(cw_all, (1, 2, 0, 3)).reshape(L, conv_w.shape[1], -1)
    fcw_full = jnp.transpose(fcw_all, (1, 2, 0, 3)).reshape(L, ffn_conv_w.shape[1], -1)

    saved = []
    h, hb = xs, xs.astype(BF16)
    for l in range(L):
        fw = full[l]
        s = dict(x=h, xb=hb)
        s['glu'], s['qkv'] = proj_split(hb, fw['w_in'], 2 * C, tm=tm, name="proj")
        on_conv = RIDE_ATT if l == 0 else RIDE_FFN[1:]
        on_sb = RIDE_FFN if l == 0 else RIDE_FFN[:1]
        s['u1'], got = conv_fwd(s['glu'], cw_full[l], conv_b[l][None], name="conv_fwd",
                                rider=GatherRider(shards_of(l, on_conv)))
        fw.update(gathered(on_conv, got))
        more = l + 1 < L
        s['o_sb'], s['ltot'], got = sb_fwd(
            s['qkv'], q_col=0, name="sb_fwd", rider=GatherRider(shards_of(l, on_sb)))
        fw.update(gathered(on_sb, got))
        s['ua'] = ln_silu(s['u1'], s['o_sb'], conv_ln_g[l][None], conv_ln_b[l][None], tm=tm, name="ln_silu")
        s['x1'], s['x1b'], s['zh1'], s['rs1'] = mm_ln(
            s['ua'], fw['w_out'], h, ln1_g[l][None], ln1_b[l][None], alpha, tm=tm, name="out_proj_ln")
        s['q2'] = mm_nn(s['x1b'], fw['mem_wq'], BF16, tm=tm, tn=D, name="mem_q")
        s['k2'], s['v2'] = mm_pair(mem_bf, fw['mem_wk'], fw['mem_wv'], name="mem_kv")
        s['o2'] = xattn_fwd(s['q2'], s['k2'], s['v2'], tm=tm, name="xattn_fwd")
        s['x2'], s['x2b'], s['zh2'], s['rs2'] = mm_ln(
            s['o2'], fw['mem_wo'], s['x1'], ln2_g[l][None], ln2_b[l][None], alpha, tm=tm, name="mem_o_ln")
        (s['upv'], s['upg'], s['mv'], s['mg'], s['hmid']), got = ffn_up_fwd(
            s['x2b'], fw['ffn_up'], fcw_full[l], ffn_conv_b[l][None], tm=tm_ffn, tn=fw['ffn_up'].shape[2],
            name="ffn_up_fwd", rider=GatherRider(shards_of(l + 1, RIDE_ATT)) if more else None)
        if more:
            full[l + 1].update(gathered(RIDE_ATT, got))
            (h, hb, s['zh3'], s['rs3']), got = mm_ln(
                s['hmid'], fw['ffn_down'], s['x2'], ln3_g[l][None], ln3_b[l][None], alpha, tm=tm, name="ffn_down_ln",
                rider=GatherRider(shards_of(l + 1, RIDE_IN)))
            full[l + 1].update(gathered(RIDE_IN, got))
        else:
            h, hb, s['zh3'], s['rs3'] = mm_ln(
                s['hmid'], fw['ffn_down'], s['x2'], ln3_g[l][None], ln3_b[l][None], alpha, tm=tm,
                name="ffn_down_ln")
        saved.append(s)

    dx, loss_part = loss_head(h, tgt, tm=tm, name="loss_head")
    loss = lax.psum(loss_part[0, 0], ("x", "y", "c"))

    core = lax.axis_index("c").astype(jnp.int32).reshape(1)
    reduced_big = {n: lax.empty((L, 2, W[n].shape[1] // 2, W[n].shape[2]), F32) for n in BIG}
    small_grads = [None] * L

    def row_halves(g, names):
        parts = []
        for n in names:
            rows, cols = W[n].shape[1], W[n].shape[2]
            parts.append(g[n].reshape(N_CHIPS, 2, rows // 2, cols))
        return parts

    def reduce_into(names, scattered, layer):
        reduced_big.update(zip(names, sum_chips_into(
            list(scattered), [reduced_big[n] for n in names], layer, core, name="rs_sum_chips")))

    pending = None
    for l in reversed(range(L)):
        fw, s = full[l], saved[l]
        g = {}
        if l == L - 1:
            top = ln_bwd(dx, s['zh3'], s['rs3'], ln3_g[l][None], tm=tm, name="ln_bwd")
        dz3, dz3b, g['ln3_g'], g['ln3_b'] = top
        ftn = fw['ffn_up'].shape[2]
        (dupv, dupg, dfw_v, dfw_g, dfb_v, dfb_g), sc = ffn_mid_bwd(
            dz3b, fw['ffn_down'], s['upv'], s['upg'], s['mv'], s['mg'], fcw_full[l], tm=tm_ffn, tn=ftn,
            name="ffn_mid_bwd", rider=ScatterRider(pending) if pending else None)
        if pending:
            reduce_into(RIDE_MIX, sc, l + 1)
        g['ffn_conv_w'] = jnp.concatenate([dfw_v, dfw_g], axis=1)
        g['ffn_conv_b'] = jnp.concatenate([dfb_v, dfb_g], axis=1)[0]
        g['ffn_down'] = mm_tn(s['hmid'], [dz3b], tk=ftn, tn=D, tmc=min(1024, S), name="grad_ffn_down")
        dz2, dz2b, g['ln2_g'], g['ln2_b'] = mm_nt_ln_bwd(
            [dupv, dupg], fw['ffn_up'], dz3, alpha, s['zh2'], s['rs2'], ln2_g[l][None], tm=tm_ffn,
            name="ffn_up_bwd")
        g['ffn_up'] = mm_tn(s['x2b'], [dupv, dupg], tk=D, tn=ftn, shard_width=ftn, tmc=min(1024, S),
                            name="grad_ffn_up")

        do2 = mm_nt([dz2b], fw['mem_wo'], BF16, tm=tm, tk=D, name="mem_o_bwd")
        g['mem_wo'] = mm_tn(s['o2'], [dz2b], tk=D, tn=D, tmc=tm_half, name="grad_sq")
        dq2, dk2, dv2 = xattn_bwd(s['q2'], do2, s['k2'], s['v2'], tm=tm, name="xattn_bwd")
        dz1, dz1b, g['ln1_g'], g['ln1_b'] = mm_nt_ln_bwd(
            [dq2], fw['mem_wq'], dz2, alpha, s['zh1'], s['rs1'], ln1_g[l][None], tm=tm, name="mem_q_bwd")
        g['mem_wq'] = mm_tn(s['x1b'], [dq2], tk=D, tn=D, tmc=tm_half, name="grad_sq")
        g['mem_wk'], g['mem_wv'] = mm_tn_pair(mem_bf, dk2, dv2, name="grad_mem_kv")

        g['w_out'] = mm_tn(s['ua'], [dz1b], tk=D, tn=D, tmc=tm_half, name="grad_sq")
        rest = row_halves(g, RIDE_REST)
        dua, got = mm_nt([dz1b], fw['w_out'], F32, tm=tm, tk=D, name="out_proj_bwd", rider=SwapRider(rest))
        rest = list(add_pairs(rest, got, core, name="rs_add_pairs"))
        dq, dk, dv, sc = sb_bwd(
            s['qkv'], s['ltot'], dua, q_col=0, do_col=C, name="sb_bwd",
            rider=ScatterRider(rest[:-1]))
        reduce_into(RIDE_REST[:-1], sc, l)
        du1, g['conv_ln_g'], g['conv_ln_b'] = ln_silu_bwd(
            dua, s['u1'], conv_ln_g[l][None], conv_ln_b[l][None], tm=tm, name="ln_silu_bwd")
        (da, dg, g['conv_w'], dcb), sc = conv_bwd(du1, s['glu'], cw_full[l], name="conv_bwd",
                                                  rider=ScatterRider(rest[-1:]))
        reduce_into(RIDE_REST[-1:], sc, l)
        g['conv_b'] = dcb
        dproj = jnp.concatenate([da, dg, dq, dk, dv], axis=1)
        ns_in = fw['w_in'].shape[2]
        g['w_in'] = mm_tn(s['xb'], [dproj], tk=512, tn=ns_in, shard_width=ns_in, name="grad_w_in")
        mix = row_halves(g, RIDE_MIX)
        if l > 0:
            below = saved[l - 1]
            top, got = mm_nt_ln_bwd([dproj], fw['w_in'], dz1, alpha, below['zh3'], below['rs3'],
                                    ln3_g[l - 1][None], tm=tm, name="proj_bwd", rider=SwapRider(mix))
        else:
            dx, got = mm_nt([dproj], fw['w_in'], F32, tm=tm_big, tk=512, res=dz1, alpha=alpha, name="proj_bwd_x",
                            rider=SwapRider(mix))
        pending = list(add_pairs(mix, got, core, name="rs_add_pairs"))
        small_grads[l] = {n: g[n].reshape(W[n].shape[1:-1] + (-1,)) for n in SMALL}

    grad_x = dx[None]

    small_full_shapes = []
    small_stack = []
    for n in SMALL:
        st = jnp.stack([small_grads[l][n] for l in range(L)])
        small_stack.append(st)
        small_full_shapes.append(st.shape)
    scattered, everyone = run_riders([ScatterRider(pending), EveryoneRider(_pack(small_stack))],
                                     name="rs_tail_exchange")
    reduce_into(RIDE_MIX, scattered, 0)
    shared = rs_sibling_share([reduced_big[n] for n in BIG], name="rs_sibling_share")
    G = {}
    for n, sh in zip(BIG, shared):
        G[n] = sh.reshape(W[n].shape)
    reduced = _unpack(sum_devices(everyone[0], name="sum_small"), small_full_shapes)
    for n, r in zip(SMALL, reduced):
        if n in SMALL_SHARDED:
            width = W[n].shape[-1]
            r = lax.dynamic_slice_in_dim(r, chip * width, width, axis=2)
        G[n] = r

    out_g, out_d, out_m, out_v = {}, {}, {}, {}
    for n in BIG:
        shp = W[n].shape
        flat = lambda a: a.reshape(shp[0] * shp[1], shp[2])
        res = adamw(flat(W[n]), flat(G[n]), flat(M1[n]), flat(V2[n]), tr=_adamw_tile(shp[0] * shp[1], shp[2]), name="adamw")
        out_g[n], out_d[n], out_m[n], out_v[n] = [r.reshape(shp) for r in res]
    small_shapes = [W[n].shape for n in SMALL]
    packed = [_pack([d[n] for n in SMALL]) for d in (W, G, M1, V2)]
    res = adamw(*packed, tr=packed[0].shape[0], name="adamw_small")
    for d, r in zip((out_g, out_d, out_m, out_v), res):
        for n, a in zip(SMALL, _unpack(r, small_shapes)):
            d[n] = a

    return (loss, grad_x, *[out_g[n] for n in WEIGHTS], *[out_d[n] for n in WEIGHTS],
            *[out_m[n] for n in WEIGHTS], *[out_v[n] for n in WEIGHTS])
```

```python
import functools

import jax
import jax.numpy as jnp
from jax import lax
from jax.experimental import pallas as pl
from jax.experimental.pallas import tpu as pltpu

F32 = jnp.float32
BF16 = jnp.bfloat16
MESH = pl.DeviceIdType.MESH

LN_EPS = 1e-5
SB_HEADS = 8
MEM_HEADS = 4
ADAM_LR, ADAM_B1, ADAM_B2, ADAM_EPS, ADAM_WD, ADAM_STEP = 0.001, 0.9, 0.999, 1e-08, 0.01, 10

LANES = 128
V7X_VMEM_BYTES = 64 << 20
VMEM_CAP = V7X_VMEM_BYTES - (6 << 20)
N_CHIPS = 4
N_DEV = 8

BIG = ('w_in', 'w_out', 'mem_wq', 'mem_wk', 'mem_wv', 'mem_wo', 'ffn_up', 'ffn_down')
RIDE_IN = ('w_in',)
RIDE_ATT = ('w_out', 'mem_wq', 'mem_wk', 'mem_wv', 'mem_wo')
RIDE_FFN = ('ffn_up', 'ffn_down')
RIDE_MIX = ('w_in',)
RIDE_REST = ('w_out', 'mem_wq', 'mem_wk', 'mem_wv', 'mem_wo', 'ffn_up', 'ffn_down')
COL_SHARDED = ('w_in', 'ffn_up')
SMALL = ('conv_w', 'conv_b', 'conv_ln_g', 'conv_ln_b', 'ln1_g', 'ln1_b', 'ln2_g', 'ln2_b',
         'ffn_conv_w', 'ffn_conv_b', 'ln3_g', 'ln3_b')
SMALL_SHARDED = ('conv_w', 'ffn_conv_w')
WEIGHTS = ('w_in', 'conv_w', 'conv_b', 'conv_ln_g', 'conv_ln_b', 'w_out', 'ln1_g', 'ln1_b',
           'mem_wq', 'mem_wk', 'mem_wv', 'mem_wo', 'ln2_g', 'ln2_b', 'ffn_up', 'ffn_conv_w',
           'ffn_conv_b', 'ffn_down', 'ln3_g', 'ln3_b')


def _params(block_bytes, semantics=None, **kw):
    limit = int(min(max(2 * block_bytes + (8 << 20), 32 << 20), VMEM_CAP))
    return pltpu.CompilerParams(dimension_semantics=semantics, vmem_limit_bytes=limit, **kw)


def _pallas(body, **kw):
    call = pl.pallas_call(body, **kw)

    def run(*args):
        return call(*[pltpu.with_memory_space_constraint(a, pltpu.HBM)
                      if jnp.issubdtype(a.dtype, jnp.floating) else a for a in args])

    return run


def _nbytes(shape, dtype):
    n = 1
    for s in shape:
        n *= s
    return n * jnp.dtype(dtype).itemsize


def _dot(a, b):
    return jnp.dot(a, b, preferred_element_type=F32)


def _dot_nt(a, b):
    return lax.dot_general(a, b, (((1,), (1,)), ((), ())), preferred_element_type=F32)


def _dot_tn(a, b):
    return lax.dot_general(a, b, (((0,), (0,)), ((), ())), preferred_element_type=F32)


def _sigmoid(x):
    return 1.0 / (1.0 + jnp.exp(-x))


def mm_nn(a, b, out_dtype, *, tm, tn, name):
    M, K = a.shape
    sharded = b.ndim == 3
    if sharded:
        nsh, _, ns = b.shape
        N, per = nsh * ns, ns // tn
        b_spec = pl.BlockSpec((None, K, tn), lambda i, j: (j // per, 0, j % per))
    else:
        N = b.shape[1]
        b_spec = pl.BlockSpec((K, tn), lambda i, j: (0, j))

    def body(a_ref, b_ref, o_ref):
        o_ref[...] = _dot(a_ref[...].astype(BF16), b_ref[...]).astype(o_ref.dtype)

    blk = _nbytes((tm, K), a.dtype) + _nbytes((K, tn), BF16) + _nbytes((tm, tn), out_dtype)
    return _pallas(
        body, name=name, out_shape=pltpu.HBM((M, N), out_dtype), grid=(M // tm, N // tn),
        in_specs=[pl.BlockSpec((tm, K), lambda i, j: (i, 0)), b_spec],
        out_specs=pl.BlockSpec((tm, tn), lambda i, j: (i, j)),
        compiler_params=_params(blk, ("parallel", "parallel")))(a, b)


def mm_pair(a, b1, b2, *, name):
    M, K = a.shape
    N = b1.shape[1]

    def body(a_ref, b1_ref, b2_ref, o1_ref, o2_ref):
        av = a_ref[...]
        o1_ref[...] = _dot(av, b1_ref[...]).astype(BF16)
        o2_ref[...] = _dot(av, b2_ref[...]).astype(BF16)

    whole = lambda r, c: pl.BlockSpec((r, c), lambda i: (0, 0))
    return _pallas(
        body, name=name, grid=(1,), out_shape=(pltpu.HBM((M, N), BF16), pltpu.HBM((M, N), BF16)),
        in_specs=[whole(M, K), whole(K, N), whole(K, N)], out_specs=(whole(M, N), whole(M, N)),
        compiler_params=_params(3 * _nbytes((K, N), BF16), ("arbitrary",)))(a, b1, b2)


def mm_tn_pair(a, b1, b2, *, name):
    M, K = a.shape
    N = b1.shape[1]

    def body(a_ref, b1_ref, b2_ref, o1_ref, o2_ref):
        av = a_ref[...]
        o1_ref[...] = _dot_tn(av, b1_ref[...].astype(BF16)).astype(BF16)
        o2_ref[...] = _dot_tn(av, b2_ref[...].astype(BF16)).astype(BF16)

    whole = lambda r, c: pl.BlockSpec((r, c), lambda i: (0, 0))
    return _pallas(
        body, name=name, grid=(1,), out_shape=(pltpu.HBM((K, N), BF16), pltpu.HBM((K, N), BF16)),
        in_specs=[whole(M, K), whole(M, N), whole(M, N)], out_specs=(whole(K, N), whole(K, N)),
        compiler_params=_params(4 * _nbytes((K, N), BF16), ("arbitrary",)))(a, b1, b2)


def proj_split(a, b, n_f32, *, tm, name):
    M, K = a.shape
    nsh, _, ns = b.shape
    N = nsh * ns

    def body(a_ref, b_ref, lo_ref, hi_ref):
        av = a_ref[...]
        for s in range(nsh):
            acc = _dot(av, b_ref[s])
            c0, c1 = s * ns, (s + 1) * ns
            cut = min(max(n_f32 - c0, 0), ns)
            if cut > 0:
                lo_ref[:, c0:c0 + cut] = acc[:, 0:cut]
            if cut < ns:
                hi_ref[:, c0 + cut - n_f32:c1 - n_f32] = acc[:, cut:ns].astype(BF16)

    blk = _nbytes((tm, K), BF16) + _nbytes((K, N), BF16) + _nbytes((tm, N), F32)
    return _pallas(
        body, name=name, grid=(M // tm,),
        out_shape=(pltpu.HBM((M, n_f32), F32), pltpu.HBM((M, N - n_f32), BF16)),
        in_specs=[pl.BlockSpec((tm, K), lambda i: (i, 0)), pl.BlockSpec((nsh, K, ns), lambda i: (0, 0, 0))],
        out_specs=(pl.BlockSpec((tm, n_f32), lambda i: (i, 0)), pl.BlockSpec((tm, N - n_f32), lambda i: (i, 0))),
        compiler_params=_params(blk, ("parallel",)))(a, b)


def mm_ln(a, b, x, gamma, beta, alpha, *, tm, name):
    M, K = a.shape
    D = b.shape[1]

    def body(a_ref, b_ref, x_ref, g_ref, be_ref, y_ref, yb_ref, zh_ref, rs_ref):
        z = alpha * x_ref[...] + _dot(a_ref[...], b_ref[...])
        mu = jnp.mean(z, axis=-1, keepdims=True)
        zc = z - mu
        rstd = lax.rsqrt(jnp.mean(zc * zc, axis=-1, keepdims=True) + LN_EPS)
        zh = zc * rstd
        y = zh * g_ref[...] + be_ref[...]
        y_ref[...] = y
        yb_ref[...] = y.astype(BF16)
        zh_ref[...] = zh
        rs_ref[...] = rstd

    row = lambda i: (i, 0)
    fix = lambda i: (0, 0)
    blk = _nbytes((tm, K), BF16) + _nbytes((K, D), BF16) + 4 * _nbytes((tm, D), F32)
    return _pallas(
        body, name=name, grid=(M // tm,),
        out_shape=(pltpu.HBM((M, D), F32), pltpu.HBM((M, D), BF16),
                   pltpu.HBM((M, D), F32), pltpu.HBM((M, 1), F32)),
        in_specs=[pl.BlockSpec((tm, K), row), pl.BlockSpec((K, D), fix), pl.BlockSpec((tm, D), row),
                  pl.BlockSpec((1, D), fix), pl.BlockSpec((1, D), fix)],
        out_specs=(pl.BlockSpec((tm, D), row), pl.BlockSpec((tm, D), row), pl.BlockSpec((tm, D), row),
                   pl.BlockSpec((tm, 1), row)),
        compiler_params=_params(blk, ("parallel",)))(a, b, x, gamma, beta)


def ln_bwd(dy, zh, rstd, gamma, *, tm, name):
    M, D = dy.shape

    def body(dy_ref, zh_ref, rs_ref, g_ref, dz_ref, dzb_ref, dg_ref, db_ref):
        @pl.when(pl.program_id(0) == 0)
        def _():
            dg_ref[...] = jnp.zeros_like(dg_ref)
            db_ref[...] = jnp.zeros_like(db_ref)

        dyv, zhv = dy_ref[...], zh_ref[...]
        dg_ref[...] += jnp.sum(dyv * zhv, axis=0, keepdims=True)
        db_ref[...] += jnp.sum(dyv, axis=0, keepdims=True)
        dzh = dyv * g_ref[...]
        m1 = jnp.mean(dzh, axis=-1, keepdims=True)
        m2 = jnp.mean(dzh * zhv, axis=-1, keepdims=True)
        dz = rs_ref[...] * (dzh - m1 - zhv * m2)
        dz_ref[...] = dz
        dzb_ref[...] = dz.astype(BF16)

    row = lambda i: (i, 0)
    fix = lambda i: (0, 0)
    return _pallas(
        body, name=name, grid=(M // tm,),
        out_shape=(pltpu.HBM((M, D), F32), pltpu.HBM((M, D), BF16),
                   pltpu.HBM((1, D), F32), pltpu.HBM((1, D), F32)),
        in_specs=[pl.BlockSpec((tm, D), row), pl.BlockSpec((tm, D), row), pl.BlockSpec((tm, 1), row),
                  pl.BlockSpec((1, D), fix)],
        out_specs=(pl.BlockSpec((tm, D), row), pl.BlockSpec((tm, D), row), pl.BlockSpec((1, D), fix),
                   pl.BlockSpec((1, D), fix)),
        compiler_params=_params(4 * _nbytes((tm, D), F32), ("arbitrary",)))(dy, zh, rstd, gamma)


def mm_nt(a_list, b, out_dtype, *, tm, tk, name, res=None, alpha=None, rider=None):
    M = a_list[0].shape[0]
    widths = [a.shape[1] for a in a_list]
    sharded = b.ndim == 3
    if sharded:
        nsh, K, ns = b.shape
        b_spec = pl.BlockSpec((nsh, tk, ns), lambda i, j: (0, j, 0))
        for w in widths:
            assert w % ns == 0
    else:
        K, N = b.shape
        ns = None
        b_spec = pl.BlockSpec((tk, N), lambda i, j: (j, 0))
    n_a = len(a_list)

    def body(*refs):
        a_refs, b_ref = refs[:n_a], refs[n_a]
        o_ref = refs[-1]
        acc = None
        off = 0
        for a_ref, w in zip(a_refs, widths):
            if sharded:
                for p in range(w // ns):
                    t = _dot_nt(a_ref[:, p * ns:(p + 1) * ns].astype(BF16), b_ref[off // ns + p])
                    acc = t if acc is None else acc + t
            else:
                t = _dot_nt(a_ref[...].astype(BF16), b_ref[:, off:off + w])
                acc = t if acc is None else acc + t
            off += w
        if res is not None:
            acc = acc + alpha * refs[n_a + 1][...]
        o_ref[...] = acc.astype(o_ref.dtype)

    in_specs = [pl.BlockSpec((tm, w), lambda i, j: (i, 0)) for w in widths] + [b_spec]
    args = list(a_list) + [b]
    if res is not None:
        in_specs.append(pl.BlockSpec((tm, tk), lambda i, j: (i, j)))
        args.append(res)
    blk = (sum(_nbytes((tm, w), a.dtype) for a, w in zip(a_list, widths)) + _nbytes((tk, sum(widths)), BF16)
           + 2 * _nbytes((tm, tk), F32))
    in_specs, out_specs, out_shape, scratch = _carry_specs(
        rider, in_specs, (pl.BlockSpec((tm, tk), lambda i, j: (i, j)),), (pltpu.HBM((M, K), out_dtype),), [])
    first = lambda: (pl.program_id(0) == 0) & (pl.program_id(1) == 0)
    last = lambda: (pl.program_id(0) == M // tm - 1) & (pl.program_id(1) == K // tk - 1)
    res_all = _pallas(
        _carry(rider, body, len(args), 1, first, last), name=name, out_shape=out_shape, grid=(M // tm, K // tk),
        in_specs=in_specs, out_specs=out_specs, scratch_shapes=scratch,
        compiler_params=_params(blk, ("arbitrary", "arbitrary")))(*args, *(rider.arrays if rider else ()))
    return res_all[0] if rider is None else (res_all[0], list(res_all[1:]))


def mm_nt_ln_bwd(a_list, b, res, alpha, zh, rstd, gamma, *, tm, name, rider=None):
    M, D = res.shape
    widths = [a.shape[1] for a in a_list]
    sharded = b.ndim == 3
    if sharded:
        nsh, _, ns = b.shape
        b_spec = pl.BlockSpec((nsh, D, ns), lambda i: (0, 0, 0))
    else:
        ns = None
        b_spec = pl.BlockSpec((D, b.shape[1]), lambda i: (0, 0))
    n_a = len(a_list)

    def body(*refs):
        a_refs, b_ref = refs[:n_a], refs[n_a]
        res_ref, zh_ref, rs_ref, g_ref = refs[n_a + 1:n_a + 5]
        dz_ref, dzb_ref, dg_ref, db_ref = refs[n_a + 5:]

        @pl.when(pl.program_id(0) == 0)
        def _():
            dg_ref[...] = jnp.zeros_like(dg_ref)
            db_ref[...] = jnp.zeros_like(db_ref)

        dy = alpha * res_ref[...]
        off = 0
        for a_ref, w in zip(a_refs, widths):
            if sharded:
                for p in range(w // ns):
                    dy = dy + _dot_nt(a_ref[:, p * ns:(p + 1) * ns], b_ref[off // ns + p])
            else:
                dy = dy + _dot_nt(a_ref[...], b_ref[:, off:off + w])
            off += w
        zhv = zh_ref[...]
        dg_ref[...] += jnp.sum(dy * zhv, axis=0, keepdims=True)
        db_ref[...] += jnp.sum(dy, axis=0, keepdims=True)
        dzh = dy * g_ref[...]
        m1 = jnp.mean(dzh, axis=-1, keepdims=True)
        m2 = jnp.mean(dzh * zhv, axis=-1, keepdims=True)
        dz = rs_ref[...] * (dzh - m1 - zhv * m2)
        dz_ref[...] = dz
        dzb_ref[...] = dz.astype(BF16)

    row = lambda i: (i, 0)
    fix = lambda i: (0, 0)
    in_specs = [pl.BlockSpec((tm, w), row) for w in widths] + [
        b_spec, pl.BlockSpec((tm, D), row), pl.BlockSpec((tm, D), row), pl.BlockSpec((tm, 1), row),
        pl.BlockSpec((1, D), fix)]
    blk = (sum(_nbytes((tm, w), BF16) for w in widths) + _nbytes((D, sum(widths)), BF16)
           + 5 * _nbytes((tm, D), F32))
    in_specs, out_specs, out_shape, scratch = _carry_specs(
        rider, in_specs,
        (pl.BlockSpec((tm, D), row), pl.BlockSpec((tm, D), row), pl.BlockSpec((1, D), fix), pl.BlockSpec((1, D), fix)),
        (pltpu.HBM((M, D), F32), pltpu.HBM((M, D), BF16), pltpu.HBM((1, D), F32), pltpu.HBM((1, D), F32)), [])
    first = lambda: pl.program_id(0) == 0
    last = lambda: pl.program_id(0) == M // tm - 1
    res_all = _pallas(
        _carry(rider, body, n_a + 5, 4, first, last), name=name, grid=(M // tm,), out_shape=out_shape,
        in_specs=in_specs, out_specs=out_specs, scratch_shapes=scratch,
        compiler_params=_params(blk, ("arbitrary",)))(
            *a_list, b, res, zh, rstd, gamma, *(rider.arrays if rider else ()))
    return res_all[:4] if rider is None else (res_all[:4], list(res_all[4:]))


def mm_tn(a, b_list, *, tk, tn, name, shard_width=None, tmc=None):
    M, K = a.shape
    tmc = M if tmc is None else tmc
    nm = M // tmc
    widths = [b.shape[1] for b in b_list]
    N = sum(widths)
    starts, s = [], 0
    for w in widths:
        assert w % tn == 0
        starts.append(s)
        s += w // tn
    n_b = len(b_list)

    def body(*refs):
        a_ref, b_refs, o_ref, acc = refs[0], refs[1:1 + n_b], refs[-2], refs[-1]
        j, m = pl.program_id(1), pl.program_id(2)
        for b_ref, st, w in zip(b_refs, starts, widths):
            @pl.when((j >= st) & (j < st + w // tn))
            def _(b_ref=b_ref):
                t = _dot_tn(a_ref[...].astype(BF16), b_ref[...].astype(BF16))
                if nm == 1:
                    o_ref[...] = t.astype(o_ref.dtype)
                else:
                    @pl.when(m == 0)
                    def _():
                        acc[...] = t

                    @pl.when(m > 0)
                    def _():
                        acc[...] += t

                    @pl.when(m == nm - 1)
                    def _():
                        o_ref[...] = acc[...].astype(o_ref.dtype)

    def b_map(st, w):
        nb = w // tn
        return lambda i, j, m: (jnp.where((j >= st) & (j < st + nb), m, 0), jnp.clip(j - st, 0, nb - 1))

    in_specs = [pl.BlockSpec((tmc, tk), lambda i, j, m: (m, i))]
    in_specs += [pl.BlockSpec((tmc, tn), b_map(st, w)) for st, w in zip(starts, widths)]
    if shard_width is None:
        out_shape = pltpu.HBM((K, N), BF16)
        out_spec = pl.BlockSpec((tk, tn), lambda i, j, m: (i, j))
    else:
        per = shard_width // tn
        out_shape = pltpu.HBM((N // shard_width, K, shard_width), BF16)
        out_spec = pl.BlockSpec((None, tk, tn), lambda i, j, m: (j // per, i, j % per))
    acc_shape = (tk, tn) if nm > 1 else (8, LANES)
    blk = (_nbytes((tmc, tk), a.dtype) + n_b * _nbytes((tmc, tn), b_list[0].dtype) + 2 * _nbytes((tk, tn), F32))
    return _pallas(
        body, name=name, out_shape=out_shape, grid=(K // tk, N // tn, nm), in_specs=in_specs, out_specs=out_spec,
        scratch_shapes=[pltpu.VMEM(acc_shape, F32)],
        compiler_params=_params(blk, ("parallel", "arbitrary", "arbitrary")))(a, *b_list)


CONV_PAD = 32
CONV_CHUNK = 128


def _rows(win, off, n, shifts):
    b, a = off % 8, off // 8
    if b not in shifts:
        shifts[b] = win if b == 0 else win[b:b + n + CONV_PAD - 8, :]
    return shifts[b][8 * a:8 * a + n, :]


def _by_residue(n_taps, offset):
    return sorted(range(n_taps), key=lambda k: (offset(k) % 8, k))


def conv_fwd(proj, conv_w, conv_b, *, name, rider=None):
    S = proj.shape[0]
    KW, C = conv_w.shape
    nct = C // LANES
    rc = min(CONV_CHUNK, S)

    def body(a_ref, g_ref, w_ref, b_ref, o_ref, pad):
        pad[0:CONV_PAD, :] = jnp.zeros((CONV_PAD, LANES), F32)
        pad[CONV_PAD:, :] = a_ref[...] * _sigmoid(g_ref[...])
        w = w_ref[...]
        bias = b_ref[...]

        def chunk(i, _):
            base = pl.multiple_of(i * rc, rc)
            win = pad[pl.ds(base, rc + CONV_PAD), :]
            acc = jnp.zeros((rc, LANES), F32) + bias
            shifts = {}
            for k in _by_residue(KW, lambda k: CONV_PAD - (KW - 1) + k):
                acc = acc + w[k:k + 1, :] * _rows(win, CONV_PAD - (KW - 1) + k, rc, shifts)
            o_ref[pl.ds(base, rc), :] = acc
            return 0

        lax.fori_loop(0, S // rc, chunk, 0)

    in_specs, out_specs, out_shape, scratch = _carry_specs(
        rider, [pl.BlockSpec((S, LANES), lambda c: (0, c)), pl.BlockSpec((S, LANES), lambda c: (0, c + nct)),
                pl.BlockSpec((KW, LANES), lambda c: (0, c)), pl.BlockSpec((1, LANES), lambda c: (0, c))],
        (pl.BlockSpec((S, LANES), lambda c: (0, c)),), (pltpu.HBM((S, C), F32),),
        [pltpu.VMEM((S + CONV_PAD, LANES), F32)])
    first = lambda: pl.program_id(0) == 0
    last = lambda: pl.program_id(0) == nct - 1
    res = _pallas(
        _carry(rider, body, 4, 1, first, last), name=name, grid=(nct,), out_shape=out_shape,
        in_specs=in_specs, out_specs=out_specs, scratch_shapes=scratch,
        compiler_params=_params(4 * _nbytes((S, LANES), F32), ("arbitrary",)))(
            proj, proj, conv_w, conv_b, *(rider.arrays if rider else ()))
    return res[0], list(res[1:])


def conv_bwd(du1, proj, conv_w, *, name, rider=None):
    S = proj.shape[0]
    KW, C = conv_w.shape
    nct = C // LANES
    rc = min(CONV_CHUNK, S)

    def body(d_ref, a_ref, g_ref, w_ref, da_ref, dg_ref, dw_ref, db_ref, pad_u, pad_d, du0, dw_acc):
        dw_acc[...] = jnp.zeros_like(dw_acc)
        pad_u[0:CONV_PAD, :] = jnp.zeros((CONV_PAD, LANES), F32)
        pad_u[CONV_PAD:, :] = a_ref[...] * _sigmoid(g_ref[...])
        pad_d[0:S, :] = d_ref[...]
        pad_d[S:, :] = jnp.zeros((CONV_PAD, LANES), F32)
        w = w_ref[...]
        db_ref[...] = jnp.sum(d_ref[...], axis=0, keepdims=True)

        def chunk(i, _):
            base = pl.multiple_of(i * rc, rc)
            d = pad_d[pl.ds(base, rc), :]
            win_u = pad_u[pl.ds(base, rc + CONV_PAD), :]
            win_d = pad_d[pl.ds(base, rc + CONV_PAD), :]
            shifts = {}
            for k in _by_residue(KW, lambda k: CONV_PAD - (KW - 1) + k):
                u_k = _rows(win_u, CONV_PAD - (KW - 1) + k, rc, shifts)
                p = d * u_k
                part = p[0:8, :]
                for r in range(8, rc, 8):
                    part = part + p[r:r + 8, :]
                dw_acc[k] += part
            acc = jnp.zeros((rc, LANES), F32)
            shifts = {}
            for k in _by_residue(KW, lambda k: KW - 1 - k):
                acc = acc + w[k:k + 1, :] * _rows(win_d, KW - 1 - k, rc, shifts)
            du0[pl.ds(base, rc), :] = acc
            return 0

        lax.fori_loop(0, S // rc, chunk, 0)
        for k in range(KW):
            dw_ref[k:k + 1, :] = jnp.sum(dw_acc[k], axis=0, keepdims=True)
        a, sg = a_ref[...], _sigmoid(g_ref[...])
        d0 = du0[...]
        da_ref[...] = (d0 * sg).astype(BF16)
        dg_ref[...] = (d0 * a * sg * (1.0 - sg)).astype(BF16)

    col = lambda c: (0, c)
    in_specs, out_specs, out_shape, scratch = _carry_specs(
        rider, [pl.BlockSpec((S, LANES), col), pl.BlockSpec((S, LANES), col),
                pl.BlockSpec((S, LANES), lambda c: (0, c + nct)), pl.BlockSpec((KW, LANES), col)],
        (pl.BlockSpec((S, LANES), col), pl.BlockSpec((S, LANES), col), pl.BlockSpec((KW, LANES), col),
         pl.BlockSpec((1, LANES), col)),
        (pltpu.HBM((S, C), BF16), pltpu.HBM((S, C), BF16), pltpu.HBM((KW, C), F32), pltpu.HBM((1, C), F32)),
        [pltpu.VMEM((S + CONV_PAD, LANES), F32), pltpu.VMEM((S + CONV_PAD, LANES), F32),
         pltpu.VMEM((S, LANES), F32), pltpu.VMEM((CONV_PAD, 8, LANES), F32)])
    first = lambda: pl.program_id(0) == 0
    last = lambda: pl.program_id(0) == nct - 1
    res = _pallas(
        _carry(rider, body, 4, 4, first, last), name=name, grid=(nct,), out_shape=out_shape,
        in_specs=in_specs, out_specs=out_specs, scratch_shapes=scratch,
        compiler_params=_params(8 * _nbytes((S, LANES), F32), ("arbitrary",)))(
            du1, proj, proj, conv_w, *(rider.arrays if rider else ()))
    return res[:4], list(res[4:])


def ln_silu(u1, o_sb, gamma, beta, *, tm, name):
    S, C = u1.shape

    def body(u_ref, o_ref, g_ref, b_ref, out_ref):
        z = u_ref[...]
        mu = jnp.mean(z, axis=-1, keepdims=True)
        zc = z - mu
        y = zc * lax.rsqrt(jnp.mean(zc * zc, axis=-1, keepdims=True) + LN_EPS) * g_ref[...] + b_ref[...]
        out_ref[:, 0:C] = (y * _sigmoid(y)).astype(BF16)
        out_ref[:, C:] = o_ref[...].astype(BF16)

    row = lambda i: (i, 0)
    fix = lambda i: (0, 0)
    return _pallas(
        body, name=name, out_shape=pltpu.HBM((S, 2 * C), BF16), grid=(S // tm,),
        in_specs=[pl.BlockSpec((tm, C), row), pl.BlockSpec((tm, C), row), pl.BlockSpec((1, C), fix),
                  pl.BlockSpec((1, C), fix)],
        out_specs=pl.BlockSpec((tm, 2 * C), row),
        compiler_params=_params(4 * _nbytes((tm, C), F32), ("parallel",)))(u1, o_sb, gamma, beta)


def ln_silu_bwd(dua, u1, gamma, beta, *, tm, name):
    S, C = u1.shape

    def body(d_ref, u_ref, g_ref, b_ref, du1_ref, dg_ref, db_ref):
        @pl.when(pl.program_id(0) == 0)
        def _():
            dg_ref[...] = jnp.zeros_like(dg_ref)
            db_ref[...] = jnp.zeros_like(db_ref)

        z = u_ref[...]
        mu = jnp.mean(z, axis=-1, keepdims=True)
        zc = z - mu
        rstd = lax.rsqrt(jnp.mean(zc * zc, axis=-1, keepdims=True) + LN_EPS)
        zh = zc * rstd
        y = zh * g_ref[...] + b_ref[...]
        sg = _sigmoid(y)
        dy = d_ref[...] * (sg * (1.0 + y * (1.0 - sg)))
        dg_ref[...] += jnp.sum(dy * zh, axis=0, keepdims=True)
        db_ref[...] += jnp.sum(dy, axis=0, keepdims=True)
        dzh = dy * g_ref[...]
        m1 = jnp.mean(dzh, axis=-1, keepdims=True)
        m2 = jnp.mean(dzh * zh, axis=-1, keepdims=True)
        du1_ref[...] = rstd * (dzh - m1 - zh * m2)

    row = lambda i: (i, 0)
    fix = lambda i: (0, 0)
    return _pallas(
        body, name=name, grid=(S // tm,),
        out_shape=(pltpu.HBM((S, C), F32), pltpu.HBM((1, C), F32),
                   pltpu.HBM((1, C), F32)),
        in_specs=[pl.BlockSpec((tm, C), row), pl.BlockSpec((tm, C), row), pl.BlockSpec((1, C), fix),
                  pl.BlockSpec((1, C), fix)],
        out_specs=(pl.BlockSpec((tm, C), row), pl.BlockSpec((1, C), fix), pl.BlockSpec((1, C), fix)),
        compiler_params=_params(4 * _nbytes((tm, C), F32), ("arbitrary",)))(dua, u1, gamma, beta)


SB_BLOCK = 256
SB_STOP = -105.0
SB_GROUP = 4


def _split_dot(x, tri):
    hi = x.astype(BF16)
    lo = (x - hi.astype(F32)).astype(BF16)
    return _dot(hi, tri) + _dot(lo, tri)


def _neg_softplus(z):
    return -(jnp.maximum(z, 0.0) + jnp.log(1.0 + jnp.exp(-jnp.abs(z))))


def sb_fwd(proj, *, q_col, name, rider=None):
    S = proj.shape[0]
    dh = LANES // 2
    W = SB_HEADS * dh
    BW = SB_GROUP * dh
    ngrp = W // BW
    T = min(SB_BLOCK, S)
    nblk = S // T
    scale = dh ** -0.5
    qb0 = q_col // BW
    heads = range(SB_GROUP)
    sl = [slice(h * dh, (h + 1) * dh) for h in heads]

    def body(q_ref, k_ref, v_ref, o_ref, l_ref, qs):
        r_i = lax.broadcasted_iota(jnp.int32, (T, T), 0)
        c_i = lax.broadcasted_iota(jnp.int32, (T, T), 1)
        tri = (r_i >= c_i).astype(BF16)
        vis = c_i < r_i
        lane = lax.broadcasted_iota(jnp.int32, (T, dh), 1)

        qs[...] = (q_ref[...] * scale).astype(BF16)

        def step(qb, blocks, st):
            nb = range(len(blocks))
            kb = [[k_ref[pl.ds(j0, T), sl[h]].astype(BF16) for h in heads] for j0, _ in blocks]
            vb = [[v_ref[pl.ds(j0, T), sl[h]].astype(BF16) for h in heads] for j0, _ in blocks]
            z = [[_dot_nt(qb[h], kb[b][h]) for h in heads] for b in nb]
            lk = [[_neg_softplus(z[b][h]) for h in heads] for b in nb]
            lk = [[jnp.where(vis, lk[b][h], 0.0) if blocks[b][1] else lk[b][h] for h in heads] for b in nb]
            C = [[_split_dot(lk[b][h], tri) for h in heads] for b in nb]
            R = [[st[2 * h + 1] for h in heads]]
            for b in nb:
                R.append([R[b][h] + C[b][h][:, 0:1] for h in heads])
            A = [[jnp.exp(z[b][h] + C[b][h] + R[b][h]) for h in heads] for b in nb]
            A = [[jnp.where(vis, A[b][h], 0.0) if blocks[b][1] else A[b][h] for h in heads] for b in nb]
            out = ()
            for h in heads:
                acc = st[2 * h]
                for b in nb:
                    acc = acc + _dot(A[b][h].astype(BF16), vb[b][h])
                out += (acc, R[-1][h])
            return out

        zero = (jnp.zeros((T, dh), F32), jnp.zeros((T, 1), F32))

        def finish(r0, i, c):
            walked = jnp.asarray(i - c[0]).astype(F32)
            for h in heads:
                o_ref[pl.ds(r0, T), sl[h]] = c[1 + 2 * h]
                l_ref[pl.ds(r0, T), sl[h]] = jnp.where(lane == 1, walked, c[2 + 2 * h])

        finish(0, 0, (-1,) + step([qs[0:T, sl[h]] for h in heads], [(0, True)], zero * SB_GROUP))

        def qblock(i, _):
            r0 = pl.multiple_of(i * T, T)
            qb = [qs[pl.ds(r0, T), sl[h]] for h in heads]
            state = step(qb, [(r0, True), (pl.multiple_of(r0 - T, T), False)], zero * SB_GROUP)

            def more(c):
                worst = c[2]
                for h in heads[1:]:
                    worst = jnp.maximum(worst, c[2 + 2 * h])
                return (c[0] >= 0) & (jnp.max(worst) >= SB_STOP)

            def walk(c):
                return (c[0] - 1,) + step(qb, [(pl.multiple_of(c[0] * T, T), False)], c[1:])

            finish(r0, i, lax.while_loop(more, walk, (i - 2,) + state))
            return 0

        lax.fori_loop(1, nblk, qblock, 0)

    blk = lambda off: pl.BlockSpec((S, BW), lambda g: (0, qb0 + off * ngrp + g), pipeline_mode=pl.Buffered(1))
    out = pl.BlockSpec((S, BW), lambda g: (0, g))
    in_specs, out_specs, out_shape, scratch = _carry_specs(
        rider, [blk(0), blk(1), blk(2)], (out, out), (pltpu.HBM((S, W), F32), pltpu.HBM((S, W), F32)),
        [pltpu.VMEM((S, BW), BF16)])
    first = lambda: pl.program_id(0) == 0
    last = lambda: pl.program_id(0) == ngrp - 1
    res = _pallas(
        _carry(rider, body, 3, 2, first, last), name=name, grid=(ngrp,), out_shape=out_shape,
        in_specs=in_specs, out_specs=out_specs, scratch_shapes=scratch,
        compiler_params=_params(5 * _nbytes((S, BW), F32), ("arbitrary",)))(
            proj, proj, proj, *(rider.arrays if rider else ()))
    return res[0], res[1], list(res[2:])


def sb_bwd(proj, ltot, dua, *, q_col, do_col, name, rider=None):
    S = proj.shape[0]
    dh = LANES // 2
    W = SB_HEADS * dh
    BW = SB_GROUP * dh
    ngrp = W // BW
    T = min(SB_BLOCK, S)
    nblk = S // T
    scale = dh ** -0.5
    qb0 = q_col // BW
    db0 = do_col // BW
    heads = range(SB_GROUP)
    sl = [slice(h * dh, (h + 1) * dh) for h in heads]

    def body(q_ref, k_ref, v_ref, l_ref, do_ref, dq_ref, dk_ref, dv_ref, dks, dvs):
        r_i = lax.broadcasted_iota(jnp.int32, (T, T), 0)
        c_i = lax.broadcasted_iota(jnp.int32, (T, T), 1)
        tri_rev = (r_i >= c_i).astype(BF16)
        tri_fwd = (r_i <= c_i).astype(BF16)
        vis = c_i < r_i

        dks[...] = jnp.zeros_like(dks)
        dvs[...] = jnp.zeros_like(dvs)

        def step(qb, dob, Lt, blocks, st):
            nb = range(len(blocks))
            kb = [[k_ref[pl.ds(j0, T), sl[h]].astype(BF16) for h in heads] for j0, _ in blocks]
            vb = [[v_ref[pl.ds(j0, T), sl[h]].astype(BF16) for h in heads] for j0, _ in blocks]
            z = [[_dot_nt(qb[h], kb[b][h]) for h in heads] for b in nb]
            dA =[[_dot_nt(dob[h], vb[b][h]) for h in heads] for b in nb]
            lk = [[_neg_softplus(z[b][h]) for h in heads] for b in nb]
            beta = [[jnp.exp(z[b][h] + lk[b][h]) for h in heads] for b in nb]
            lk = [[jnp.where(vis, lk[b][h], 0.0) if blocks[b][1] else lk[b][h] for h in heads] for b in nb]
            C = [[_split_dot(lk[b][h], tri_rev) for h in heads] for b in nb]
            P = [[st[3 * h + 1] for h in heads]]
            for b in nb:
                P.append([P[b][h] + C[b][h][:, 0:1] for h in heads])
            A = [[jnp.exp(z[b][h] + C[b][h] + (Lt[h] - P[b + 1][h])) for h in heads] for b in nb]
            A = [[jnp.where(vis, A[b][h], 0.0) if blocks[b][1] else A[b][h] for h in heads] for b in nb]
            g = [[A[b][h] * dA[b][h] for h in heads] for b in nb]
            Gin = [[_split_dot(g[b][h], tri_fwd) for h in heads] for b in nb]
            Gp = [[st[3 * h + 2] for h in heads]]
            for b in nb:
                Gp.append([Gp[b][h] + Gin[b][h][:, T - 1:T] for h in heads])
            dz = [[g[b][h] - beta[b][h] * (Gp[b][h] + Gin[b][h]) for h in heads] for b in nb]
            dz = [[jnp.where(vis, dz[b][h], 0.0) if blocks[b][1] else dz[b][h] for h in heads] for b in nb]
            dzb = [[dz[b][h].astype(BF16) for h in heads] for b in nb]
            out = ()
            for h in heads:
                dq = st[3 * h]
                for b in nb:
                    j0 = blocks[b][0]
                    dvs[pl.ds(j0, T), sl[h]] += _dot_tn(A[b][h].astype(BF16), dob[h])
                    dks[pl.ds(j0, T), sl[h]] += _dot_tn(dzb[b][h], qb[h])
                    dq = dq + _dot(dzb[b][h], kb[b][h])
                out += (dq, P[-1][h], Gp[-1][h])
            return out

        zero = jnp.zeros((T, 1), F32)
        init = (jnp.zeros((T, dh), F32), zero, zero)

        def operands(r0):
            return ([(q_ref[pl.ds(r0, T), sl[h]] * scale).astype(BF16) for h in heads],
                    [do_ref[pl.ds(r0, T), sl[h]].astype(BF16) for h in heads],
                    [l_ref[pl.ds(r0, T), h * dh:h * dh + 1] for h in heads])

        def finish(r0, c):
            for h in heads:
                dq_ref[pl.ds(r0, T), sl[h]] = (c[3 * h] * scale).astype(BF16)

        finish(0, step(*operands(0), [(0, True)], init * SB_GROUP))

        def qblock(i, _):
            r0 = pl.multiple_of(i * T, T)
            qb, dob, Lt = operands(r0)
            walked = jnp.clip(jnp.max(l_ref[pl.ds(r0, 8), 1:2]).astype(jnp.int32), 2, i + 1)

            def inner(j, c):
                return step(qb, dob, Lt, [(pl.multiple_of(j * T, T), False)], c)

            c = lax.fori_loop(i + 1 - walked, i - 1, inner, init * SB_GROUP)
            finish(r0, step(qb, dob, Lt, [(pl.multiple_of(r0 - T, T), False), (r0, True)], c))
            return 0

        lax.fori_loop(1, nblk, qblock, 0)
        dk_ref[...] = dks[...].astype(BF16)
        dv_ref[...] = dvs[...].astype(BF16)

    once = pl.Buffered(1)
    blk = lambda off: pl.BlockSpec((S, BW), lambda g: (0, qb0 + off * ngrp + g), pipeline_mode=once)
    out = pl.BlockSpec((S, BW), lambda g: (0, g))
    o_shape = pltpu.HBM((S, W), BF16)
    in_specs, out_specs, out_shape, scratch = _carry_specs(
        rider, [blk(0), blk(1), blk(2), pl.BlockSpec((S, BW), lambda g: (0, g), pipeline_mode=once),
                pl.BlockSpec((S, BW), lambda g: (0, db0 + g), pipeline_mode=once)], (out, out, out),
        (o_shape, o_shape, o_shape), [pltpu.VMEM((S, BW), F32)] * 2)
    first = lambda: pl.program_id(0) == 0
    last = lambda: pl.program_id(0) == ngrp - 1
    res = _pallas(
        _carry(rider, body, 5, 3, first, last), name=name, grid=(ngrp,), out_shape=out_shape,
        in_specs=in_specs, out_specs=out_specs, scratch_shapes=scratch,
        compiler_params=_params(6 * _nbytes((S, BW), F32), ("arbitrary",)))(
            proj, proj, proj, ltot, dua, *(rider.arrays if rider else ()))
    return res[0], res[1], res[2], list(res[3:])


def xattn_fwd(q, k, v, *, tm, name):
    S, D = q.shape
    Mlen = k.shape[0]
    hd = D // MEM_HEADS
    scale = hd ** -0.5

    def body(q_ref, k_ref, v_ref, o_ref):
        for h in range(MEM_HEADS):
            sl = slice(h * hd, (h + 1) * hd)
            s = _dot_nt(q_ref[:, sl], k_ref[:, sl]) * scale
            e = jnp.exp(s - jnp.max(s, axis=-1, keepdims=True))
            p = e / jnp.sum(e, axis=-1, keepdims=True)
            o_ref[:, sl] = _dot(p.astype(BF16), v_ref[:, sl]).astype(BF16)

    row = lambda i: (i, 0)
    fix = lambda i: (0, 0)
    return _pallas(
        body, name=name, out_shape=pltpu.HBM((S, D), BF16), grid=(S // tm,),
        in_specs=[pl.BlockSpec((tm, D), row), pl.BlockSpec((Mlen, D), fix), pl.BlockSpec((Mlen, D), fix)],
        out_specs=pl.BlockSpec((tm, D), row),
        compiler_params=_params(4 * _nbytes((tm, D), F32), ("parallel",)))(q, k, v)


def xattn_bwd(q, do, k, v, *, tm, name):
    S, D = q.shape
    Mlen = k.shape[0]
    hd = D // MEM_HEADS
    scale = hd ** -0.5

    def body(q_ref, do_ref, k_ref, v_ref, dq_ref, dk_ref, dv_ref):
        @pl.when(pl.program_id(0) == 0)
        def _():
            dk_ref[...] = jnp.zeros_like(dk_ref)
            dv_ref[...] = jnp.zeros_like(dv_ref)

        for h in range(MEM_HEADS):
            sl = slice(h * hd, (h + 1) * hd)
            qh, doh, kh, vh = q_ref[:, sl], do_ref[:, sl], k_ref[:, sl], v_ref[:, sl]
            s = _dot_nt(qh, kh) * scale
            e = jnp.exp(s - jnp.max(s, axis=-1, keepdims=True))
            p = e / jnp.sum(e, axis=-1, keepdims=True)
            dp = _dot_nt(doh, vh)
            ds = (p * (dp - jnp.sum(p * dp, axis=-1, keepdims=True)) * scale).astype(BF16)
            dq_ref[:, sl] = _dot(ds, kh).astype(BF16)
            dk_ref[:, sl] += _dot_tn(ds, qh)
            dv_ref[:, sl] += _dot_tn(p.astype(BF16), doh)

    row = lambda i: (i, 0)
    fix = lambda i: (0, 0)
    return _pallas(
        body, name=name, grid=(S // tm,),
        out_shape=(pltpu.HBM((S, D), BF16), pltpu.HBM((Mlen, D), F32),
                   pltpu.HBM((Mlen, D), F32)),
        in_specs=[pl.BlockSpec((tm, D), row), pl.BlockSpec((tm, D), row), pl.BlockSpec((Mlen, D), fix),
                  pl.BlockSpec((Mlen, D), fix)],
        out_specs=(pl.BlockSpec((tm, D), row), pl.BlockSpec((Mlen, D), fix), pl.BlockSpec((Mlen, D), fix)),
        compiler_params=_params(6 * _nbytes((tm, D), F32), ("arbitrary",)))(q, do, k, v)


FFN_HALO = 8


def _conv3(ext, w, lo):
    tm = ext.shape[0] - FFN_HALO
    return (w[0:1, :] * ext[lo:lo + tm, :] + w[1:2, :] * ext[lo + 1:lo + 1 + tm, :]
            + w[2:3, :] * ext[lo + 2:lo + 2 + tm, :])


def ffn_up_fwd(xb, w_up, conv_w, conv_b, *, tm, tn, name, rider=None):
    S, D = xb.shape
    nsh, _, ns = w_up.shape
    F = nsh * ns // 2
    per = ns // tn
    ncol = F // tn
    KW = conv_w.shape[0]
    assert KW == 3

    def body(x_ref, wv_ref, wg_ref, cwv_ref, cwg_ref, cbv_ref, cbg_ref, uv_ref, ug_ref, mv_ref, mg_ref, h_ref,
             carry):
        @pl.when(pl.program_id(1) == 0)
        def _():
            carry[...] = jnp.zeros_like(carry)

        x = x_ref[...]
        uv = _dot(x, wv_ref[...])
        ug = _dot(x, wg_ref[...])
        uv_ref[...] = uv.astype(BF16)
        ug_ref[...] = ug.astype(BF16)
        lo = FFN_HALO - (KW - 1)
        cv = _conv3(jnp.concatenate([carry[0], uv], axis=0), cwv_ref[...], lo) + cbv_ref[...]
        cg = _conv3(jnp.concatenate([carry[1], ug], axis=0), cwg_ref[...], lo) + cbg_ref[...]
        carry[0] = uv[tm - FFN_HALO:, :]
        carry[1] = ug[tm - FFN_HALO:, :]
        sg = _sigmoid(cg)
        act = cg * sg
        mv_ref[...] = act.astype(BF16)
        mg_ref[...] = (cv * (sg + act * (1.0 - sg))).astype(BF16)
        h_ref[...] = (act * cv).astype(BF16)

    wspec = lambda half: pl.BlockSpec((None, D, tn), lambda j, i: (half * (nsh // 2) + j // per, 0, j % per))
    cspec = lambda rows, half: pl.BlockSpec((rows, tn), lambda j, i: (0, half * ncol + j))
    out = pl.BlockSpec((tm, tn), lambda j, i: (i, j))
    o_shape = pltpu.HBM((S, F), BF16)
    blk = _nbytes((tm, D), BF16) + 2 * _nbytes((D, tn), BF16) + 8 * _nbytes((tm, tn), F32)
    nrow = S // tm
    in_specs, out_specs, out_shape, scratch = _carry_specs(
        rider, [pl.BlockSpec((tm, D), lambda j, i: (i, 0)), wspec(0), wspec(1), cspec(KW, 0), cspec(KW, 1),
                cspec(1, 0), cspec(1, 1)], (out,) * 5, (o_shape,) * 5, [pltpu.VMEM((2, FFN_HALO, tn), F32)])
    first = lambda: (pl.program_id(0) == 0) & (pl.program_id(1) == 0)
    last = lambda: (pl.program_id(0) == ncol - 1) & (pl.program_id(1) == nrow - 1)
    res = _pallas(
        _carry(rider, body, 7, 5, first, last), name=name, grid=(ncol, nrow), out_shape=out_shape,
        in_specs=in_specs, out_specs=out_specs, scratch_shapes=scratch,
        compiler_params=_params(blk, ("arbitrary", "arbitrary")))(
            xb, w_up, w_up, conv_w, conv_w, conv_b, conv_b, *(rider.arrays if rider else ()))
    return res[:5], list(res[5:])


def ffn_mid_bwd(dzb, w_down, up_v, up_g, mult_v, mult_g, conv_w, *, tm, tn, name, rider=None):
    S, D = dzb.shape
    F = up_v.shape[1]
    ncol = F // tn
    nrow = S // tm
    KW = conv_w.shape[0]
    assert KW == 3

    def body(dz_ref, wd_ref, uv_ref, ug_ref, mv_ref, mg_ref, cwv_ref, cwg_ref,
             dv_ref, dg_ref, dwv_ref, dwg_ref, dbv_ref, dbg_ref, carry):
        @pl.when(pl.program_id(1) == 0)
        def _():
            carry[...] = jnp.zeros_like(carry)
            for r in (dwv_ref, dwg_ref, dbv_ref, dbg_ref):
                r[...] = jnp.zeros_like(r)

        dh = _dot_nt(dz_ref[...], wd_ref[...])
        dcv = dh * mv_ref[...].astype(F32)
        dcg = dh * mg_ref[...].astype(F32)

        def back(dc, u_ref, cw, slot, du_ref, dw_ref, db_ref):
            ext = jnp.concatenate([dc, carry[slot]], axis=0)
            ahead = [dc, ext[1:tm + 1, :], ext[2:tm + 2, :]]
            du = cw[2:3, :] * ahead[0] + cw[1:2, :] * ahead[1] + cw[0:1, :] * ahead[2]
            du_ref[...] = du.astype(BF16)
            carry[slot] = dc[0:FFN_HALO, :]
            u = u_ref[...].astype(F32)
            for k in range(KW):
                dw_ref[k:k + 1, :] += jnp.sum(ahead[KW - 1 - k] * u, axis=0, keepdims=True)
            db_ref[...] += jnp.sum(dc, axis=0, keepdims=True)

        back(dcv, uv_ref, cwv_ref[...], 0, dv_ref, dwv_ref, dbv_ref)
        back(dcg, ug_ref, cwg_ref[...], 1, dg_ref, dwg_ref, dbg_ref)

    rev = lambda i: nrow - 1 - i
    tile = pl.BlockSpec((tm, tn), lambda j, i: (rev(i), j))
    cspec = lambda half: pl.BlockSpec((KW, tn), lambda j, i: (0, half * ncol + j))
    acc = lambda rows: pl.BlockSpec((rows, tn), lambda j, i: (0, j))
    big = pltpu.HBM((S, F), BF16)
    blk = _nbytes((tm, D), BF16) + _nbytes((tn, D), BF16) + 10 * _nbytes((tm, tn), F32)
    in_specs, out_specs, out_shape, scratch = _carry_specs(
        rider, [pl.BlockSpec((tm, D), lambda j, i: (rev(i), 0)), pl.BlockSpec((tn, D), lambda j, i: (j, 0)),
                tile, tile, tile, tile, cspec(0), cspec(1)],
        (tile, tile, acc(KW), acc(KW), acc(1), acc(1)),
        (big, big, pltpu.HBM((KW, F), F32), pltpu.HBM((KW, F), F32), pltpu.HBM((1, F), F32),
         pltpu.HBM((1, F), F32)), [pltpu.VMEM((2, FFN_HALO, tn), F32)])
    first = lambda: (pl.program_id(0) == 0) & (pl.program_id(1) == 0)
    last = lambda: (pl.program_id(0) == ncol - 1) & (pl.program_id(1) == nrow - 1)
    res = _pallas(
        _carry(rider, body, 8, 6, first, last), name=name, grid=(ncol, nrow), out_shape=out_shape,
        in_specs=in_specs, out_specs=out_specs, scratch_shapes=scratch,
        compiler_params=_params(blk, ("arbitrary", "arbitrary")))(
            dzb, w_down, up_v, up_g, mult_v, mult_g, conv_w, conv_w, *(rider.arrays if rider else ()))
    return res[:6], list(res[6:])


def loss_head(y, target, *, tm, name):
    S, D = y.shape

    def body(y_ref, t_ref, dy_ref, l_ref):
        @pl.when(pl.program_id(0) == 0)
        def _():
            l_ref[...] = jnp.zeros_like(l_ref)

        e = y_ref[...] - t_ref[...]
        dy_ref[...] = e * (1.0 / D)
        l_ref[...] += 0.5 * jnp.sum(jnp.mean(e * e, axis=-1, keepdims=True), axis=0, keepdims=True)

    row = lambda i: (i, 0)
    return _pallas(
        body, name=name, grid=(S // tm,),
        out_shape=(pltpu.HBM((S, D), F32), pltpu.HBM((1, 1), F32)),
        in_specs=[pl.BlockSpec((tm, D), row), pl.BlockSpec((tm, D), row)],
        out_specs=(pl.BlockSpec((tm, D), row), pl.BlockSpec((1, 1), lambda i: (0, 0))),
        compiler_params=_params(3 * _nbytes((tm, D), F32), ("arbitrary",)))(y, target)


def adamw(w, g, m, v, *, tr, name):
    R, C = w.shape
    c1 = 1.0 - ADAM_B1 ** ADAM_STEP
    c2 = 1.0 - ADAM_B2 ** ADAM_STEP

    def body(w_ref, g_ref, m_ref, v_ref, go_ref, d_ref, mo_ref, vo_ref):
        gv = g_ref[...]
        mn = ADAM_B1 * m_ref[...] + (1.0 - ADAM_B1) * gv
        vn = ADAM_B2 * v_ref[...] + (1.0 - ADAM_B2) * (gv * gv)
        go_ref[...] = gv
        mo_ref[...] = mn
        vo_ref[...] = vn
        d_ref[...] = -ADAM_LR * ((mn / c1) / (jnp.sqrt(vn / c2) + ADAM_EPS) + ADAM_WD * w_ref[...])

    spec = pl.BlockSpec((tr, C), lambda i: (i, 0))
    shape = pltpu.HBM((R, C), F32)
    return _pallas(
        body, name=name, grid=(R // tr,), out_shape=(shape,) * 4, in_specs=[spec] * 4, out_specs=(spec,) * 4,
        compiler_params=_params(8 * _nbytes((tr, C), F32), ("parallel",)))(w, g, m, v)


def add_pairs(gs, gots, core, *, name):
    k = len(gs)

    def body(c_ref, *refs):
        for a_ref, b_ref, o_ref in zip(refs[:k], refs[k:2 * k], refs[2 * k:]):
            o_ref[...] = (a_ref[...].astype(F32) + b_ref[...].astype(F32)).astype(BF16)

    own = [pl.BlockSpec((None, None) + g.shape[2:], lambda i, c: (i, c[0], 0, 0)) for g in gs]
    half = [pl.BlockSpec((None,) + g.shape[1:], lambda i, c: (i, 0, 0)) for g in gots]
    grid_spec = pltpu.PrefetchScalarGridSpec(
        num_scalar_prefetch=1, grid=(N_CHIPS,), in_specs=own + half, out_specs=tuple(half))
    blk = 3 * sum(_nbytes(g.shape[1:], BF16) for g in gots)
    return _pallas(
        body, name=name, grid_spec=grid_spec, out_shape=tuple(pltpu.HBM(g.shape, BF16) for g in gots),
        compiler_params=_params(blk, ("parallel",)))(core, *gs, *gots)


def sum_chips_into(bs, dests, layer, core, *, name):
    k = len(bs)
    steps = 2

    def body(c_ref, *refs):
        for b_ref, o_ref in zip(refs[:k], refs[2 * k:]):
            acc = b_ref[0].astype(F32)
            for p in range(1, N_CHIPS):
                acc = acc + b_ref[p].astype(F32)
            o_ref[...] = acc

    ins = [pl.BlockSpec((N_CHIPS, b.shape[1] // steps, b.shape[2]), lambda i, c: (0, i, 0)) for b in bs]
    outs = tuple(pl.BlockSpec((None, None, b.shape[1] // steps, b.shape[2]), lambda i, c: (layer, c[0], i, 0))
                 for b in bs)
    grid_spec = pltpu.PrefetchScalarGridSpec(
        num_scalar_prefetch=1, grid=(steps,), in_specs=ins + [pl.BlockSpec(memory_space=pl.ANY)] * k,
        out_specs=outs)
    blk = sum(_nbytes(b.shape, BF16) + _nbytes(b.shape[1:], F32) for b in bs) // steps
    return _pallas(
        body, name=name, grid_spec=grid_spec, out_shape=tuple(pltpu.HBM(d.shape, F32) for d in dests),
        input_output_aliases={1 + k + w: w for w in range(k)},
        compiler_params=_params(blk, ("parallel",)))(core, *bs, *dests)


_HBM = pl.BlockSpec(memory_space=pltpu.HBM)


def _place():
    x, y, c = lax.axis_index("x"), lax.axis_index("y"), lax.axis_index("c")
    chips = [(1 - x, y), (x, 1 - y), (1 - x, 1 - y)]
    return x, y, c, chips


class GatherRider:
    def __init__(self, shards):
        self.arrays = list(shards)
        self.n = n = len(shards)
        self.out_shape = tuple(pltpu.HBM((N_CHIPS,) + s.shape, s.dtype) for s in shards)
        self.scratch = [pltpu.SemaphoreType.DMA((n, 3))] * 4 + [pltpu.SemaphoreType.DMA((n,))]

    def _copies(self, ins, outs, sems):
        send_ici, recv_ici, send_d2d, recv_d2d, local = sems
        x, y, c, chips = _place()
        me = 2 * x + y

        def own(w):
            return pltpu.make_async_copy(ins[w], outs[w].at[me], local.at[w])

        def ici(w, j):
            px, py = chips[j]
            return pltpu.make_async_remote_copy(
                src_ref=ins[w].at[c], dst_ref=outs[w].at[me, c], send_sem=send_ici.at[w, j],
                recv_sem=recv_ici.at[w, j], device_id=(px, py, c), device_id_type=MESH)

        def landed(w, j, half):
            px, py = chips[j]
            return outs[w].at[2 * px + py, half]

        def d2d(w, j, half):
            return pltpu.make_async_remote_copy(
                src_ref=landed(w, j, half), dst_ref=landed(w, j, half), send_sem=send_d2d.at[w, j],
                recv_sem=recv_d2d.at[w, j], device_id=(x, y, 1 - c), device_id_type=MESH)

        def ici_arrival(w, j):
            return pltpu.make_async_remote_copy(
                src_ref=landed(w, j, c), dst_ref=landed(w, j, c), send_sem=send_ici.at[w, j],
                recv_sem=recv_ici.at[w, j], device_id=(x, y, c), device_id_type=MESH)

        return c, own, ici, d2d, ici_arrival

    def start(self, ins, outs, sems):
        c, own, ici, d2d, ici_arrival = self._copies(ins, outs, sems)
        for w in range(self.n):
            own(w).start()
            for j in range(3):
                ici(w, j).start()

    def finish(self, ins, outs, sems):
        c, own, ici, d2d, ici_arrival = self._copies(ins, outs, sems)
        for w in range(self.n):
            for j in range(3):
                ici_arrival(w, j).wait_recv()
                d2d(w, j, c).start()
        for w in range(self.n):
            for j in range(3):
                d2d(w, j, 1 - c).wait_recv()
        for w in range(self.n):
            for j in range(3):
                ici(w, j).wait_send()
                d2d(w, j, c).wait_send()
            own(w).wait()


class ScatterRider:
    def __init__(self, parts):
        self.arrays = list(parts)
        self.n = n = len(parts)
        self.out_shape = tuple(pltpu.HBM(p.shape, p.dtype) for p in parts)
        self.scratch = [pltpu.SemaphoreType.DMA((n, 3))] * 2 + [pltpu.SemaphoreType.DMA((n,))]

    def _copies(self, ins, outs, sems):
        send, recv, local = sems
        x, y, c, chips = _place()
        me = 2 * x + y

        def own(w):
            return pltpu.make_async_copy(ins[w].at[me], outs[w].at[me], local.at[w])

        def copy(w, j):
            px, py = chips[j]
            return pltpu.make_async_remote_copy(
                src_ref=ins[w].at[2 * px + py], dst_ref=outs[w].at[me], send_sem=send.at[w, j],
                recv_sem=recv.at[w, j], device_id=(px, py, c), device_id_type=MESH)

        def arrival(w, j):
            px, py = chips[j]
            blk = outs[w].at[2 * px + py]
            return pltpu.make_async_remote_copy(
                src_ref=blk, dst_ref=blk, send_sem=send.at[w, j], recv_sem=recv.at[w, j],
                device_id=(x, y, c), device_id_type=MESH)

        return own, copy, arrival

    def start(self, ins, outs, sems):
        own, copy, arrival = self._copies(ins, outs, sems)
        for w in range(self.n):
            own(w).start()
            for j in range(3):
                copy(w, j).start()

    def finish(self, ins, outs, sems):
        own, copy, arrival = self._copies(ins, outs, sems)
        for w in range(self.n):
            for j in range(3):
                arrival(w, j).wait_recv()
        for w in range(self.n):
            for j in range(3):
                copy(w, j).wait_send()
            own(w).wait()


def _carry(rider, body, n_in, n_out, first, last):
    if rider is None:
        return body
    k, m = rider.n, len(rider.scratch)

    def carried(*refs):
        ins, r_in = refs[:n_in], refs[n_in:n_in + k]
        outs, r_out = refs[n_in + k:n_in + k + n_out], refs[n_in + k + n_out:n_in + 2 * k + n_out]
        rest = refs[n_in + 2 * k + n_out:]
        scratch, sems = rest[:len(rest) - m], rest[len(rest) - m:]

        @pl.when(first())
        def _():
            rider.start(r_in, r_out, sems)

        body(*ins, *outs, *scratch)

        @pl.when(last())
        def _():
            rider.finish(r_in, r_out, sems)

    return carried


def _carry_specs(rider, in_specs, out_specs, out_shape, scratch):
    if rider is None:
        return list(in_specs), tuple(out_specs), tuple(out_shape), list(scratch)
    k = rider.n
    return (list(in_specs) + [_HBM] * k, tuple(out_specs) + (_HBM,) * k, tuple(out_shape) + rider.out_shape,
            list(scratch) + list(rider.scratch))


def run_riders(riders, *, name):
    ks = [r.n for r in riders]
    ms = [len(r.scratch) for r in riders]
    k_all = sum(ks)

    def body(*refs):
        parts, i0, o0, s0 = [], 0, k_all, 2 * k_all
        for k, m in zip(ks, ms):
            parts.append((refs[i0:i0 + k], refs[o0:o0 + k], refs[s0:s0 + m]))
            i0, o0, s0 = i0 + k, o0 + k, s0 + m
        for r, p in zip(riders, parts):
            r.start(*p)
        for r, p in zip(riders, parts):
            r.finish(*p)

    res = _pallas(
        body, name=name, out_shape=tuple(o for r in riders for o in r.out_shape), in_specs=[_HBM] * k_all,
        out_specs=(_HBM,) * k_all, scratch_shapes=[s for r in riders for s in r.scratch],
    )(*[a for r in riders for a in r.arrays])
    out, o0 = [], 0
    for k in ks:
        out.append(list(res[o0:o0 + k]))
        o0 += k
    return out


class SmallGatherRider:
    def __init__(self, shards):
        self.arrays = list(shards)
        self.n = n = len(shards)
        self.out_shape = tuple(pltpu.HBM((N_CHIPS,) + s.shape, s.dtype) for s in shards)
        self.scratch = [pltpu.SemaphoreType.DMA((n, 3))] * 2 + [pltpu.SemaphoreType.DMA((n,))]

    def _copies(self, ins, outs, sems):
        send, recv, local = sems
        x, y, c, chips = _place()
        me = 2 * x + y

        def own(w):
            return pltpu.make_async_copy(ins[w], outs[w].at[me], local.at[w])

        def copy(w, j):
            px, py = chips[j]
            return pltpu.make_async_remote_copy(
                src_ref=ins[w], dst_ref=outs[w].at[me], send_sem=send.at[w, j], recv_sem=recv.at[w, j],
                device_id=(px, py, c), device_id_type=MESH)

        def arrival(w, j):
            px, py = chips[j]
            blk = outs[w].at[2 * px + py]
            return pltpu.make_async_remote_copy(
                src_ref=blk, dst_ref=blk, send_sem=send.at[w, j], recv_sem=recv.at[w, j],
                device_id=(x, y, c), device_id_type=MESH)

        return own, copy, arrival

    def start(self, ins, outs, sems):
        own, copy, arrival = self._copies(ins, outs, sems)
        for w in range(self.n):
            own(w).start()
            for j in range(3):
                copy(w, j).start()

    def finish(self, ins, outs, sems):
        own, copy, arrival = self._copies(ins, outs, sems)
        for w in range(self.n):
            for j in range(3):
                arrival(w, j).wait_recv()
        for w in range(self.n):
            for j in range(3):
                copy(w, j).wait_send()
            own(w).wait()


class SwapRider:
    def __init__(self, grads):
        self.arrays = list(grads)
        self.n = n = len(grads)
        self.out_shape = tuple(pltpu.HBM((N_CHIPS,) + g.shape[2:], g.dtype) for g in grads)
        self.scratch = [pltpu.SemaphoreType.DMA((n,))] * 2

    def _copies(self, ins, outs, sems):
        send, recv = sems
        x, y, c, _ = _place()
        return [pltpu.make_async_remote_copy(
            src_ref=ins[w].at[:, 1 - c], dst_ref=outs[w], send_sem=send.at[w], recv_sem=recv.at[w],
            device_id=(x, y, 1 - c), device_id_type=MESH) for w in range(self.n)]

    def start(self, ins, outs, sems):
        for cp in self._copies(ins, outs, sems):
            cp.start()

    def finish(self, ins, outs, sems):
        copies = self._copies(ins, outs, sems)
        for cp in copies:
            cp.wait_recv()
        for cp in copies:
            cp.wait_send()


def rs_sibling_share(stacked, *, name):
    n = len(stacked)

    def body(*refs):
        bufs = refs[n:2 * n]
        send, recv = refs[2 * n:]
        x, y, c, _ = _place()
        shares, arrivals = [], []
        for w in range(n):
            mine, other = bufs[w].at[:, c], bufs[w].at[:, 1 - c]
            shares.append(pltpu.make_async_remote_copy(
                src_ref=mine, dst_ref=mine, send_sem=send.at[w], recv_sem=recv.at[w],
                device_id=(x, y, 1 - c), device_id_type=MESH))
            arrivals.append(pltpu.make_async_remote_copy(
                src_ref=other, dst_ref=other, send_sem=send.at[w], recv_sem=recv.at[w],
                device_id=(x, y, c), device_id_type=MESH))
        for cp in shares:
            cp.start()
        for cp in arrivals:
            cp.wait_recv()
        for cp in shares:
            cp.wait_send()

    out_shape = tuple(pltpu.HBM(s.shape, F32) for s in stacked)
    return _pallas(
        body, name=name, out_shape=out_shape, in_specs=[_HBM] * n, out_specs=(_HBM,) * n,
        input_output_aliases={w: w for w in range(n)},
        scratch_shapes=[pltpu.SemaphoreType.DMA((n,))] * 2,
    )(*stacked)


class EveryoneRider:
    def __init__(self, v):
        self.arrays = [v]
        self.n = 1
        self.out_shape = (pltpu.HBM((N_DEV,) + v.shape, v.dtype),)
        self.scratch = [pltpu.SemaphoreType.DMA((N_DEV - 1,))] * 2 + [pltpu.SemaphoreType.DMA(())]

    def _copies(self, ins, outs, sems):
        send, recv, local = sems
        x, y, c, _ = _place()
        me = 4 * x + 2 * y + c

        def flip(k):
            return (1 - x) if k & 4 else x, (1 - y) if k & 2 else y, (1 - c) if k & 1 else c

        own = pltpu.make_async_copy(ins[0], outs[0].at[me], local)
        sends, arrivals = [], []
        for k in range(1, N_DEV):
            px, py, pc = flip(k)
            sends.append(pltpu.make_async_remote_copy(
                src_ref=ins[0], dst_ref=outs[0].at[me], send_sem=send.at[k - 1], recv_sem=recv.at[k - 1],
                device_id=(px, py, pc), device_id_type=MESH))
            blk = outs[0].at[4 * px + 2 * py + pc]
            arrivals.append(pltpu.make_async_remote_copy(
                src_ref=blk, dst_ref=blk, send_sem=send.at[k - 1], recv_sem=recv.at[k - 1],
                device_id=(x, y, c), device_id_type=MESH))
        return own, sends, arrivals

    def start(self, ins, outs, sems):
        own, sends, _ = self._copies(ins, outs, sems)
        own.start()
        for cp in sends:
            cp.start()

    def finish(self, ins, outs, sems):
        own, sends, arrivals = self._copies(ins, outs, sems)
        for cp in arrivals:
            cp.wait_recv()
        for cp in sends:
            cp.wait_send()
        own.wait()


def sum_devices(land, *, name):
    n, R, C = land.shape

    def body(l_ref, o_ref):
        acc = l_ref[0]
        for d in range(1, n):
            acc = acc + l_ref[d]
        o_ref[...] = acc

    return _pallas(
        body, name=name, grid=(1,), out_shape=pltpu.HBM((R, C), F32),
        in_specs=[pl.BlockSpec((n, R, C), lambda i: (0, 0, 0))], out_specs=pl.BlockSpec((R, C), lambda i: (0, 0)),
        compiler_params=_params(_nbytes(land.shape, F32), ("arbitrary",)))(land)


def _pack(arrays):
    flat = jnp.concatenate([a.reshape(-1) for a in arrays])
    return flat.reshape(-1, LANES)


def _unpack(packed, shapes):
    flat = packed.reshape(-1)
    out, off = [], 0
    for s in shapes:
        n = 1
        for d in s:
            n *= d
        out.append(flat[off:off + n].reshape(s))
        off += n
    return out


def _row_tile(rows, cap=512):
    t = 1 << (cap.bit_length() - 1)
    while rows % t:
        t //= 2
    return t


def _adamw_tile(rows, cols):
    return _row_tile(rows, max(8, (1 << 20) // (4 * cols)))


def kernel(x, mem, w_in, conv_w, conv_b, conv_ln_g, conv_ln_b, w_out, ln1_g, ln1_b, mem_wq, mem_wk, mem_wv, mem_wo, ln2_g, ln2_b, ffn_up, ffn_conv_w, ffn_conv_b, ffn_down, ln3_g, ln3_b, loss_target, m_w_in, m_conv_w, m_conv_b, m_conv_ln_g, m_conv_ln_b, m_w_out, m_ln1_g, m_ln1_b, m_mem_wq, m_mem_wk, m_mem_wv, m_mem_wo, m_ln2_g, m_ln2_b, m_ffn_up, m_ffn_conv_w, m_ffn_conv_b, m_ffn_down, m_ln3_g, m_ln3_b, v_w_in, v_conv_w, v_conv_b, v_conv_ln_g, v_conv_ln_b, v_w_out, v_ln1_g, v_ln1_b, v_mem_wq, v_mem_wk, v_mem_wv, v_mem_wo, v_ln2_g, v_ln2_b, v_ffn_up, v_ffn_conv_w, v_ffn_conv_b, v_ffn_down, v_ln3_g, v_ln3_b):
    W = dict(w_in=w_in, conv_w=conv_w, conv_b=conv_b, conv_ln_g=conv_ln_g, conv_ln_b=conv_ln_b, w_out=w_out,
             ln1_g=ln1_g, ln1_b=ln1_b, mem_wq=mem_wq, mem_wk=mem_wk, mem_wv=mem_wv, mem_wo=mem_wo, ln2_g=ln2_g,
             ln2_b=ln2_b, ffn_up=ffn_up, ffn_conv_w=ffn_conv_w, ffn_conv_b=ffn_conv_b, ffn_down=ffn_down,
             ln3_g=ln3_g, ln3_b=ln3_b)
    M1 = dict(w_in=m_w_in, conv_w=m_conv_w, conv_b=m_conv_b, conv_ln_g=m_conv_ln_g, conv_ln_b=m_conv_ln_b,
              w_out=m_w_out, ln1_g=m_ln1_g, ln1_b=m_ln1_b, mem_wq=m_mem_wq, mem_wk=m_mem_wk, mem_wv=m_mem_wv,
              mem_wo=m_mem_wo, ln2_g=m_ln2_g, ln2_b=m_ln2_b, ffn_up=m_ffn_up, ffn_conv_w=m_ffn_conv_w,
              ffn_conv_b=m_ffn_conv_b, ffn_down=m_ffn_down, ln3_g=m_ln3_g, ln3_b=m_ln3_b)
    V2 = dict(w_in=v_w_in, conv_w=v_conv_w, conv_b=v_conv_b, conv_ln_g=v_conv_ln_g, conv_ln_b=v_conv_ln_b,
              w_out=v_w_out, ln1_g=v_ln1_g, ln1_b=v_ln1_b, mem_wq=v_mem_wq, mem_wk=v_mem_wk, mem_wv=v_mem_wv,
              mem_wo=v_mem_wo, ln2_g=v_ln2_g, ln2_b=v_ln2_b, ffn_up=v_ffn_up, ffn_conv_w=v_ffn_conv_w,
              ffn_conv_b=v_ffn_conv_b, ffn_down=v_ffn_down, ln3_g=v_ln3_g, ln3_b=v_ln3_b)

    L = w_in.shape[0]
    S, D = x.shape[1], x.shape[2]
    C = conv_b.shape[1]
    alpha = (2.0 * L) ** 0.25
    chip = 2 * lax.axis_index("x") + lax.axis_index("y")
    xs, mems, tgt = x[0], mem[0], loss_target[0]
    mem_bf = mems.astype(BF16)
    tm = _row_tile(S)
    tm_ffn = _row_tile(S, 256)
    tm_big = _row_tile(S, 1024)
    tm_half = _row_tile(S, 2048)

    def shards_of(l, names):
        out = []
        for n in names:
            wl = W[n][l].astype(BF16)
            out.append(wl.reshape(2, wl.shape[0] // 2, wl.shape[1]))
        return out

    def gathered(names, got):
        layer = {}
        for n, g in zip(names, got):
            rows, cols = W[n].shape[1], W[n].shape[2]
            layer[n] = g.reshape(N_CHIPS, rows, cols) if n in COL_SHARDED else g.reshape(N_CHIPS * rows, cols)
        return layer

    full = [dict() for _ in range(L)]
    got, (cw_all, fcw_all) = run_riders(
        [GatherRider(shards_of(0, RIDE_IN)), SmallGatherRider([conv_w, ffn_conv_w])], name="allgather_first")
    full[0].update(gathered(RIDE_IN, got))
    cw_full = jnp.transpose(cw_all, (1, 2, 0, 3)).reshape(L, conv_w.shape[1], -1)
    fcw_full = jnp.transpose(fcw_all, (1, 2, 0, 3)).reshape(L, ffn_conv_w.shape[1], -1)

    saved = []
    h, hb = xs, xs.astype(BF16)
    for l in range(L):
        fw = full[l]
        s = dict(x=h, xb=hb)
        s['glu'], s['qkv'] = proj_split(hb, fw['w_in'], 2 * C, tm=tm, name="proj")
        on_conv = RIDE_ATT if l == 0 else RIDE_FFN[1:]
        on_sb = RIDE_FFN if l == 0 else RIDE_FFN[:1]
        s['u1'], got = conv_fwd(s['glu'], cw_full[l], conv_b[l][None], name="conv_fwd",
                                rider=GatherRider(shards_of(l, on_conv)))
        fw.update(gathered(on_conv, got))
        more = l + 1 < L
        s['o_sb'], s['ltot'], got = sb_fwd(
            s['qkv'], q_col=0, name="sb_fwd", rider=GatherRider(shards_of(l, on_sb)))
        fw.update(gathered(on_sb, got))
        s['ua'] = ln_silu(s['u1'], s['o_sb'], conv_ln_g[l][None], conv_ln_b[l][None], tm=tm, name="ln_silu")
        s['x1'], s['x1b'], s['zh1'], s['rs1'] = mm_ln(
            s['ua'], fw['w_out'], h, ln1_g[l][None], ln1_b[l][None], alpha, tm=tm, name="out_proj_ln")
        s['q2'] = mm_nn(s['x1b'], fw['mem_wq'], BF16, tm=tm, tn=D, name="mem_q")
        s['k2'], s['v2'] = mm_pair(mem_bf, fw['mem_wk'], fw['mem_wv'], name="mem_kv")
        s['o2'] = xattn_fwd(s['q2'], s['k2'], s['v2'], tm=tm, name="xattn_fwd")
        s['x2'], s['x2b'], s['zh2'], s['rs2'] = mm_ln(
            s['o2'], fw['mem_wo'], s['x1'], ln2_g[l][None], ln2_b[l][None], alpha, tm=tm, name="mem_o_ln")
        (s['upv'], s['upg'], s['mv'], s['mg'], s['hmid']), got = ffn_up_fwd(
            s['x2b'], fw['ffn_up'], fcw_full[l], ffn_conv_b[l][None], tm=tm_ffn, tn=fw['ffn_up'].shape[2],
            name="ffn_up_fwd", rider=GatherRider(shards_of(l + 1, RIDE_ATT + RIDE_IN)) if more else None)
        if more:
            full[l + 1].update(gathered(RIDE_ATT + RIDE_IN, got))
        h, hb, s['zh3'], s['rs3'] = mm_ln(
            s['hmid'], fw['ffn_down'], s['x2'], ln3_g[l][None], ln3_b[l][None], alpha, tm=tm, name="ffn_down_ln")
        saved.append(s)

    dx, loss_part = loss_head(h, tgt, tm=tm, name="loss_head")
    loss = lax.psum(loss_part[0, 0], ("x", "y", "c"))

    core = lax.axis_index("c").astype(jnp.int32).reshape(1)
    reduced_big = {n: lax.empty((L, 2, W[n].shape[1] // 2, W[n].shape[2]), F32) for n in BIG}
    small_grads = [None] * L

    def row_halves(g, names):
        parts = []
        for n in names:
            rows, cols = W[n].shape[1], W[n].shape[2]
            parts.append(g[n].reshape(N_CHIPS, 2, rows // 2, cols))
        return parts

    def reduce_into(names, scattered, layer):
        reduced_big.update(zip(names, sum_chips_into(
            list(scattered), [reduced_big[n] for n in names], layer, core, name="rs_sum_chips")))

    pending = None
    for l in reversed(range(L)):
        fw, s = full[l], saved[l]
        g = {}
        if l == L - 1:
            top = ln_bwd(dx, s['zh3'], s['rs3'], ln3_g[l][None], tm=tm, name="ln_bwd")
        dz3, dz3b, g['ln3_g'], g['ln3_b'] = top
        ftn = fw['ffn_up'].shape[2]
        (dupv, dupg, dfw_v, dfw_g, dfb_v, dfb_g), sc = ffn_mid_bwd(
            dz3b, fw['ffn_down'], s['upv'], s['upg'], s['mv'], s['mg'], fcw_full[l], tm=tm_ffn, tn=ftn,
            name="ffn_mid_bwd", rider=ScatterRider(pending) if pending else None)
        if pending:
            reduce_into(RIDE_MIX, sc, l + 1)
        g['ffn_conv_w'] = jnp.concatenate([dfw_v, dfw_g], axis=1)
        g['ffn_conv_b'] = jnp.concatenate([dfb_v, dfb_g], axis=1)[0]
        g['ffn_down'] = mm_tn(s['hmid'], [dz3b], tk=ftn, tn=D, tmc=min(1024, S), name="grad_ffn_down")
        dz2, dz2b, g['ln2_g'], g['ln2_b'] = mm_nt_ln_bwd(
            [dupv, dupg], fw['ffn_up'], dz3, alpha, s['zh2'], s['rs2'], ln2_g[l][None], tm=tm_ffn,
            name="ffn_up_bwd")
        g['ffn_up'] = mm_tn(s['x2b'], [dupv, dupg], tk=D, tn=ftn, shard_width=ftn, tmc=min(1024, S),
                            name="grad_ffn_up")

        do2 = mm_nt([dz2b], fw['mem_wo'], BF16, tm=tm, tk=D, name="mem_o_bwd")
        g['mem_wo'] = mm_tn(s['o2'], [dz2b], tk=D, tn=D, tmc=tm_half, name="grad_sq")
        dq2, dk2, dv2 = xattn_bwd(s['q2'], do2, s['k2'], s['v2'], tm=tm, name="xattn_bwd")
        dz1, dz1b, g['ln1_g'], g['ln1_b'] = mm_nt_ln_bwd(
            [dq2], fw['mem_wq'], dz2, alpha, s['zh1'], s['rs1'], ln1_g[l][None], tm=tm, name="mem_q_bwd")
        g['mem_wq'] = mm_tn(s['x1b'], [dq2], tk=D, tn=D, tmc=tm_half, name="grad_sq")
        g['mem_wk'], g['mem_wv'] = mm_tn_pair(mem_bf, dk2, dv2, name="grad_mem_kv")

        g['w_out'] = mm_tn(s['ua'], [dz1b], tk=D, tn=D, tmc=tm_half, name="grad_sq")
        rest = row_halves(g, RIDE_REST)
        dua, got = mm_nt([dz1b], fw['w_out'], F32, tm=tm, tk=D, name="out_proj_bwd", rider=SwapRider(rest))
        rest = list(add_pairs(rest, got, core, name="rs_add_pairs"))
        dq, dk, dv, sc = sb_bwd(
            s['qkv'], s['ltot'], dua, q_col=0, do_col=C, name="sb_bwd",
            rider=ScatterRider(rest[:-1]))
        reduce_into(RIDE_REST[:-1], sc, l)
        du1, g['conv_ln_g'], g['conv_ln_b'] = ln_silu_bwd(
            dua, s['u1'], conv_ln_g[l][None], conv_ln_b[l][None], tm=tm, name="ln_silu_bwd")
        (da, dg, g['conv_w'], dcb), sc = conv_bwd(du1, s['glu'], cw_full[l], name="conv_bwd",
                                                  rider=ScatterRider(rest[-1:]))
        reduce_into(RIDE_REST[-1:], sc, l)
        g['conv_b'] = dcb
        dproj = jnp.concatenate([da, dg, dq, dk, dv], axis=1)
        ns_in = fw['w_in'].shape[2]
        g['w_in'] = mm_tn(s['xb'], [dproj], tk=512, tn=ns_in, shard_width=ns_in, name="grad_w_in")
        mix = row_halves(g, RIDE_MIX)
        if l > 0:
            below = saved[l - 1]
            top, got = mm_nt_ln_bwd([dproj], fw['w_in'], dz1, alpha, below['zh3'], below['rs3'],
                                    ln3_g[l - 1][None], tm=tm, name="proj_bwd", rider=SwapRider(mix))
        else:
            dx, got = mm_nt([dproj], fw['w_in'], F32, tm=tm_big, tk=512, res=dz1, alpha=alpha, name="proj_bwd_x",
                            rider=SwapRider(mix))
        pending = list(add_pairs(mix, got, core, name="rs_add_pairs"))
        small_grads[l] = {n: g[n].reshape(W[n].shape[1:-1] + (-1,)) for n in SMALL}

    grad_x = dx[None]

    small_full_shapes = []
    small_stack = []
    for n in SMALL:
        st = jnp.stack([small_grads[l][n] for l in range(L)])
        small_stack.append(st)
        small_full_shapes.append(st.shape)
    scattered, everyone = run_riders([ScatterRider(pending), EveryoneRider(_pack(small_stack))],
                                     name="rs_tail_exchange")
    reduce_into(RIDE_MIX, scattered, 0)
    shared = rs_sibling_share([reduced_big[n] for n in BIG], name="rs_sibling_share")
    G = {}
    for n, sh in zip(BIG, shared):
        G[n] = sh.reshape(W[n].shape)
    reduced = _unpack(sum_devices(everyone[0], name="sum_small"), small_full_shapes)
    for n, r in zip(SMALL, reduced):
        if n in SMALL_SHARDED:
            width = W[n].shape[-1]
            r = lax.dynamic_slice_in_dim(r, chip * width, width, axis=2)
        G[n] = r

    out_g, out_d, out_m, out_v = {}, {}, {}, {}
    for n in BIG:
        shp = W[n].shape
        flat = lambda a: a.reshape(shp[0] * shp[1], shp[2])
        res = adamw(flat(W[n]), flat(G[n]), flat(M1[n]), flat(V2[n]), tr=_adamw_tile(shp[0] * shp[1], shp[2]), name="adamw")
        out_g[n], out_d[n], out_m[n], out_v[n] = [r.reshape(shp) for r in res]
    small_shapes = [W[n].shape for n in SMALL]
    packed = [_pack([d[n] for n in SMALL]) for d in (W, G, M1, V2)]
    res = adamw(*packed, tr=packed[0].shape[0], name="adamw_small")
    for d, r in zip((out_g, out_d, out_m, out_v), res):
        for n, a in zip(SMALL, _unpack(r, small_shapes)):
            d[n] = a

    return (loss, grad_x, *[out_g[n] for n in WEIGHTS], *[out_d[n] for n in WEIGHTS],
            *[out_m[n] for n in WEIGHTS], *[out_v[n] for n in WEIGHTS])
```
